```python
import jax, jax.numpy as jnp
from jax import lax
import numpy as np

D_MODEL = 1024
BATCH = 8
SEQ = 4096
DEPTH = 2

D_MIX = D_MODEL
D_A = D_MIX // 2
D_B = D_MIX - D_A
CHUNK = 128
H_A = 4
HD_A = D_A // H_A
H_B = 4
HD_B = D_B // H_B
CONV_W = 31
D_FF = 4 * D_MODEL
IN_COLS = 2 * D_A + 2 * D_B
EPS = 1e-6

kernel_name = "hybrid_sgu_conformer_conv_block"


def rms_norm(x, g):
    xf = x.astype(jnp.float32)
    y = xf * lax.rsqrt(jnp.mean(xf * xf, axis=-1, keepdims=True) + EPS)
    return (y * g.astype(jnp.float32)).astype(x.dtype)


def layer_norm(x, g, b):
    xf = x.astype(jnp.float32)
    mu = jnp.mean(xf, axis=-1, keepdims=True)
    var = jnp.mean(jnp.square(xf - mu), axis=-1, keepdims=True)
    y = (xf - mu) * lax.rsqrt(var + EPS)
    return (y * g.astype(jnp.float32) + b.astype(jnp.float32)).astype(x.dtype)


def spatial_gating(u_a, v_a, ln_g, ln_b, w_s, b_s):
    bsz, seq, _ = u_a.shape
    u = jax.nn.gelu(u_a, approximate=False)
    v = layer_norm(jax.nn.gelu(v_a, approximate=False), ln_g, ln_b)
    v = v.reshape(bsz, seq // CHUNK, CHUNK, H_A, HD_A)
    mask = jnp.tril(jnp.ones((CHUNK, CHUNK), dtype=w_s.dtype))
    w = w_s * mask[None]
    mixed = jnp.einsum('hts,bcshd->bcthd', w, v)
    mixed = mixed + jnp.transpose(b_s)[None, None, :, :, None]
    return u * mixed.reshape(bsz, seq, D_A)


def conformer_conv(val_b, gate_b, conv_w, conv_b, ln_g, ln_b):
    bsz, seq, _ = val_b.shape
    g = val_b * jax.nn.sigmoid(gate_b)
    c = lax.conv_general_dilated(
        g, conv_w[:, None, :].astype(g.dtype),
        window_strides=(1,), padding=[(CONV_W - 1, 0)],
        dimension_numbers=('NWC', 'WIO', 'NWC'),
        feature_group_count=D_B)
    c = c + conv_b
    c = layer_norm(c.reshape(bsz, seq, H_B, HD_B),
                   ln_g.reshape(H_B, HD_B), ln_b.reshape(H_B, HD_B))
    return jax.nn.silu(c).reshape(bsz, seq, D_B)


def _fwd_setup_inputs(seed: int = 0) -> dict:
    key = jax.random.key(seed)
    ks = jax.random.split(key, 20)
    f32 = jnp.float32
    nrm = lambda k, shape, scale: jax.random.normal(k, shape, f32) * scale
    return {
        "x": jax.random.normal(ks[0], (BATCH, SEQ, D_MODEL), f32),
        "norm1_g": 1.0 + nrm(ks[1], (DEPTH, D_MODEL), 0.05),
        "w_in": nrm(ks[2], (DEPTH, D_MODEL, IN_COLS), D_MODEL ** -0.5),
        "sgu_ln_g": 1.0 + nrm(ks[3], (DEPTH, D_A), 0.05),
        "sgu_ln_b": nrm(ks[4], (DEPTH, D_A), 0.02),
        "sgu_w": nrm(ks[5], (DEPTH, H_A, CHUNK, CHUNK), CHUNK ** -0.5),
        "sgu_b": 1.0 + nrm(ks[6], (DEPTH, H_A, CHUNK), 0.1),
        "conv_w": nrm(ks[7], (DEPTH, CONV_W, D_B), CONV_W ** -0.5),
        "conv_b": nrm(ks[8], (DEPTH, D_B), 0.02),
        "conv_ln_g": 1.0 + nrm(ks[9], (DEPTH, D_B), 0.05),
        "conv_ln_b": nrm(ks[10], (DEPTH, D_B), 0.02),
        "w_out": nrm(ks[11], (DEPTH, D_MIX, D_MODEL), D_MIX ** -0.5),
        "norm2_g": 1.0 + nrm(ks[12], (DEPTH, D_MODEL), 0.05),
        "w_ff1": nrm(ks[13], (DEPTH, D_MODEL, D_FF), D_MODEL ** -0.5),
        "w_ff2": nrm(ks[14], (DEPTH, D_FF, D_MODEL), D_FF ** -0.5),
        "final_g": 1.0 + nrm(ks[15], (D_MODEL,), 0.05),
    }


def _fwd_reference(x, norm1_g, w_in, sgu_ln_g, sgu_ln_b, sgu_w, sgu_b, conv_w, conv_b,
              conv_ln_g, conv_ln_b, w_out, norm2_g, w_ff1, w_ff2, final_g):
    for l in range(DEPTH):
        h = rms_norm(x, norm1_g[l])
        proj = jnp.einsum('bsd,dc->bsc', h, w_in[l])
        u_a = proj[..., :D_A]
        v_a = proj[..., D_A:2 * D_A]
        val_b = proj[..., 2 * D_A:2 * D_A + D_B]
        gate_b = proj[..., 2 * D_A + D_B:]
        a_out = spatial_gating(u_a, v_a, sgu_ln_g[l], sgu_ln_b[l], sgu_w[l], sgu_b[l])
        b_out = conformer_conv(val_b, gate_b, conv_w[l], conv_b[l],
                               conv_ln_g[l], conv_ln_b[l])
        mix = jnp.concatenate([a_out, b_out], axis=-1)
        x = x + jnp.einsum('bsc,cd->bsd', mix, w_out[l])
        h = rms_norm(x, norm2_g[l])
        f = jnp.square(jax.nn.relu(jnp.einsum('bsd,df->bsf', h, w_ff1[l])))
        x = x + jnp.einsum('bsf,fd->bsd', f, w_ff2[l])
    return rms_norm(x, final_g)


import jax as _jax
import jax.numpy as _jnp

TWIN_FORMAT = 'train_step'
FWD_PARAMS = ['x', 'norm1_g', 'w_in', 'sgu_ln_g', 'sgu_ln_b', 'sgu_w', 'sgu_b', 'conv_w', 'conv_b', 'conv_ln_g', 'conv_ln_b', 'w_out', 'norm2_g', 'w_ff1', 'w_ff2', 'final_g']
TWIN_WEIGHTS = ['norm1_g', 'w_in', 'sgu_ln_g', 'sgu_ln_b', 'sgu_w', 'sgu_b', 'conv_w', 'conv_b', 'conv_ln_g', 'conv_ln_b', 'w_out', 'norm2_g', 'w_ff1', 'w_ff2', 'final_g']
TWIN_DIFF_INPUT = 'x'
TWIN_INPUTS = ['x', 'norm1_g', 'w_in', 'sgu_ln_g', 'sgu_ln_b', 'sgu_w', 'sgu_b', 'conv_w', 'conv_b', 'conv_ln_g', 'conv_ln_b', 'w_out', 'norm2_g', 'w_ff1', 'w_ff2', 'final_g', 'loss_target', 'm_norm1_g', 'm_w_in', 'm_sgu_ln_g', 'm_sgu_ln_b', 'm_sgu_w', 'm_sgu_b', 'm_conv_w', 'm_conv_b', 'm_conv_ln_g', 'm_conv_ln_b', 'm_w_out', 'm_norm2_g', 'm_w_ff1', 'm_w_ff2', 'm_final_g', 'v_norm1_g', 'v_w_in', 'v_sgu_ln_g', 'v_sgu_ln_b', 'v_sgu_w', 'v_sgu_b', 'v_conv_w', 'v_conv_b', 'v_conv_ln_g', 'v_conv_ln_b', 'v_w_out', 'v_norm2_g', 'v_w_ff1', 'v_w_ff2', 'v_final_g']
TWIN_OUTPUTS = ['loss', 'grad_x', 'grad_norm1_g', 'grad_w_in', 'grad_sgu_ln_g', 'grad_sgu_ln_b', 'grad_sgu_w', 'grad_sgu_b', 'grad_conv_w', 'grad_conv_b', 'grad_conv_ln_g', 'grad_conv_ln_b', 'grad_w_out', 'grad_norm2_g', 'grad_w_ff1', 'grad_w_ff2', 'grad_final_g', 'delta_norm1_g', 'delta_w_in', 'delta_sgu_ln_g', 'delta_sgu_ln_b', 'delta_sgu_w', 'delta_sgu_b', 'delta_conv_w', 'delta_conv_b', 'delta_conv_ln_g', 'delta_conv_ln_b', 'delta_w_out', 'delta_norm2_g', 'delta_w_ff1', 'delta_w_ff2', 'delta_final_g', 'new_m_norm1_g', 'new_m_w_in', 'new_m_sgu_ln_g', 'new_m_sgu_ln_b', 'new_m_sgu_w', 'new_m_sgu_b', 'new_m_conv_w', 'new_m_conv_b', 'new_m_conv_ln_g', 'new_m_conv_ln_b', 'new_m_w_out', 'new_m_norm2_g', 'new_m_w_ff1', 'new_m_w_ff2', 'new_m_final_g', 'new_v_norm1_g', 'new_v_w_in', 'new_v_sgu_ln_g', 'new_v_sgu_ln_b', 'new_v_sgu_w', 'new_v_sgu_b', 'new_v_conv_w', 'new_v_conv_b', 'new_v_conv_ln_g', 'new_v_conv_ln_b', 'new_v_w_out', 'new_v_norm2_g', 'new_v_w_ff1', 'new_v_w_ff2', 'new_v_final_g']
TWIN_LEAF_KINDS = {'loss': 'loss', 'grad_x': 'grad_x', 'grad_norm1_g': 'grad_w', 'grad_w_in': 'grad_w', 'grad_sgu_ln_g': 'grad_w', 'grad_sgu_ln_b': 'grad_w', 'grad_sgu_w': 'grad_w', 'grad_sgu_b': 'grad_w', 'grad_conv_w': 'grad_w', 'grad_conv_b': 'grad_w', 'grad_conv_ln_g': 'grad_w', 'grad_conv_ln_b': 'grad_w', 'grad_w_out': 'grad_w', 'grad_norm2_g': 'grad_w', 'grad_w_ff1': 'grad_w', 'grad_w_ff2': 'grad_w', 'grad_final_g': 'grad_w', 'delta_norm1_g': 'delta_w', 'delta_w_in': 'delta_w', 'delta_sgu_ln_g': 'delta_w', 'delta_sgu_ln_b': 'delta_w', 'delta_sgu_w': 'delta_w', 'delta_sgu_b': 'delta_w', 'delta_conv_w': 'delta_w', 'delta_conv_b': 'delta_w', 'delta_conv_ln_g': 'delta_w', 'delta_conv_ln_b': 'delta_w', 'delta_w_out': 'delta_w', 'delta_norm2_g': 'delta_w', 'delta_w_ff1': 'delta_w', 'delta_w_ff2': 'delta_w', 'delta_final_g': 'delta_w', 'new_m_norm1_g': 'new_m', 'new_m_w_in': 'new_m', 'new_m_sgu_ln_g': 'new_m', 'new_m_sgu_ln_b': 'new_m', 'new_m_sgu_w': 'new_m', 'new_m_sgu_b': 'new_m', 'new_m_conv_w': 'new_m', 'new_m_conv_b': 'new_m', 'new_m_conv_ln_g': 'new_m', 'new_m_conv_ln_b': 'new_m', 'new_m_w_out': 'new_m', 'new_m_norm2_g': 'new_m', 'new_m_w_ff1': 'new_m', 'new_m_w_ff2': 'new_m', 'new_m_final_g': 'new_m', 'new_v_norm1_g': 'new_v', 'new_v_w_in': 'new_v', 'new_v_sgu_ln_g': 'new_v', 'new_v_sgu_ln_b': 'new_v', 'new_v_sgu_w': 'new_v', 'new_v_sgu_b': 'new_v', 'new_v_conv_w': 'new_v', 'new_v_conv_b': 'new_v', 'new_v_conv_ln_g': 'new_v', 'new_v_conv_ln_b': 'new_v', 'new_v_w_out': 'new_v', 'new_v_norm2_g': 'new_v', 'new_v_w_ff1': 'new_v', 'new_v_w_ff2': 'new_v', 'new_v_final_g': 'new_v'}


def _forward(args):
    return _fwd_reference(*[args[k] for k in FWD_PARAMS])


def _output_shape():
    out = _jax.eval_shape(lambda: _forward(_fwd_setup_inputs(0)))
    return out.shape, out.dtype

N_MICROBATCH = 1
ADAM_LR = 0.001
ADAM_B1 = 0.9
ADAM_B2 = 0.999
ADAM_EPS = 1e-08
ADAM_WD = 0.01
ADAM_STEP = 10
PER_EXAMPLE_BATCH_AXIS = {'x': 0, 'loss_target': 0}
SHARED_INPUTS = []
_WEIGHT_DTYPES = {'norm1_g': _jnp.float32, 'w_in': _jnp.float32, 'sgu_ln_g': _jnp.float32, 'sgu_ln_b': _jnp.float32, 'sgu_w': _jnp.float32, 'sgu_b': _jnp.float32, 'conv_w': _jnp.float32, 'conv_b': _jnp.float32, 'conv_ln_g': _jnp.float32, 'conv_ln_b': _jnp.float32, 'w_out': _jnp.float32, 'norm2_g': _jnp.float32, 'w_ff1': _jnp.float32, 'w_ff2': _jnp.float32, 'final_g': _jnp.float32}
MOMENT_SCALE = {'norm1_g': 1.196176e-01, 'w_in': 8.743275e-02, 'sgu_ln_g': 6.094559e-02, 'sgu_ln_b': 6.262573e-02, 'sgu_w': 6.289570e-02, 'sgu_b': 9.047284e-02, 'conv_w': 1.025810e-01, 'conv_b': 4.175758e-01, 'conv_ln_g': 1.903900e-01, 'conv_ln_b': 2.623752e-01, 'w_out': 1.475625e-01, 'norm2_g': 1.577963e-01, 'w_ff1': 8.049255e-02, 'w_ff2': 2.464179e-01, 'final_g': 3.260122e+01}


def _to_microbatches(a, axis):
    t = _jnp.moveaxis(a, axis, 0)
    t = t.reshape((N_MICROBATCH, t.shape[0] // N_MICROBATCH) + t.shape[1:])
    return _jnp.moveaxis(t, 1, axis + 1)


def setup_inputs(seed: int = 0) -> dict:
    inp = _fwd_setup_inputs(seed)
    key = _jax.random.fold_in(_jax.random.key(seed), 7919)
    shape, _ = _output_shape()
    out = dict(inp)
    out["loss_target"] = _jax.random.normal(_jax.random.fold_in(key, 0), shape, _jnp.float32)
    for i, name in enumerate(TWIN_WEIGHTS):
        w = inp[name].astype(_jnp.float32)
        if MOMENT_SCALE is None:
            s = _jnp.sqrt(_jnp.mean(_jnp.square(w)) + 1e-30)
        else:
            s = MOMENT_SCALE[name]
        km, kv = _jax.random.split(_jax.random.fold_in(key, i + 1))
        out[name] = w
        out["m_" + name] = s * _jax.random.normal(km, w.shape, _jnp.float32)
        out["v_" + name] = (s * s) * _jax.random.uniform(kv, w.shape, _jnp.float32, 0.5, 1.5)
    if N_MICROBATCH > 1:
        for name, axis in PER_EXAMPLE_BATCH_AXIS.items():
            out[name] = _to_microbatches(out[name], axis)
    return {'x': out['x'], 'norm1_g': out['norm1_g'], 'w_in': out['w_in'], 'sgu_ln_g': out['sgu_ln_g'], 'sgu_ln_b': out['sgu_ln_b'], 'sgu_w': out['sgu_w'], 'sgu_b': out['sgu_b'], 'conv_w': out['conv_w'], 'conv_b': out['conv_b'], 'conv_ln_g': out['conv_ln_g'], 'conv_ln_b': out['conv_ln_b'], 'w_out': out['w_out'], 'norm2_g': out['norm2_g'], 'w_ff1': out['w_ff1'], 'w_ff2': out['w_ff2'], 'final_g': out['final_g'], 'loss_target': out['loss_target'], 'm_norm1_g': out['m_norm1_g'], 'm_w_in': out['m_w_in'], 'm_sgu_ln_g': out['m_sgu_ln_g'], 'm_sgu_ln_b': out['m_sgu_ln_b'], 'm_sgu_w': out['m_sgu_w'], 'm_sgu_b': out['m_sgu_b'], 'm_conv_w': out['m_conv_w'], 'm_conv_b': out['m_conv_b'], 'm_conv_ln_g': out['m_conv_ln_g'], 'm_conv_ln_b': out['m_conv_ln_b'], 'm_w_out': out['m_w_out'], 'm_norm2_g': out['m_norm2_g'], 'm_w_ff1': out['m_w_ff1'], 'm_w_ff2': out['m_w_ff2'], 'm_final_g': out['m_final_g'], 'v_norm1_g': out['v_norm1_g'], 'v_w_in': out['v_w_in'], 'v_sgu_ln_g': out['v_sgu_ln_g'], 'v_sgu_ln_b': out['v_sgu_ln_b'], 'v_sgu_w': out['v_sgu_w'], 'v_sgu_b': out['v_sgu_b'], 'v_conv_w': out['v_conv_w'], 'v_conv_b': out['v_conv_b'], 'v_conv_ln_g': out['v_conv_ln_g'], 'v_conv_ln_b': out['v_conv_ln_b'], 'v_w_out': out['v_w_out'], 'v_norm2_g': out['v_norm2_g'], 'v_w_ff1': out['v_w_ff1'], 'v_w_ff2': out['v_w_ff2'], 'v_final_g': out['v_final_g']}


def _loss(weights, diff, rest, loss_target):
    with _jax.named_scope("forward"):
        args = {**rest, TWIN_DIFF_INPUT: diff, **{k: w.astype(_WEIGHT_DTYPES[k]) for k, w in weights.items()}}
        y = _forward(args)
    with _jax.named_scope("loss_head"):
        err = _jnp.square(y.astype(_jnp.float32) - loss_target)
        return 0.5 * _jnp.sum(_jnp.mean(err, axis=-1)) if err.ndim else 0.5 * err


def _adamw(w, g, m, v):
    m = ADAM_B1 * m + (1.0 - ADAM_B1) * g
    v = ADAM_B2 * v + (1.0 - ADAM_B2) * _jnp.square(g)
    m_hat = m / (1.0 - ADAM_B1 ** ADAM_STEP)
    v_hat = v / (1.0 - ADAM_B2 ** ADAM_STEP)
    delta = -ADAM_LR * (m_hat / (_jnp.sqrt(v_hat) + ADAM_EPS) + ADAM_WD * w)
    return delta, m, v


def reference(x, norm1_g, w_in, sgu_ln_g, sgu_ln_b, sgu_w, sgu_b, conv_w, conv_b, conv_ln_g, conv_ln_b, w_out, norm2_g, w_ff1, w_ff2, final_g, loss_target, m_norm1_g, m_w_in, m_sgu_ln_g, m_sgu_ln_b, m_sgu_w, m_sgu_b, m_conv_w, m_conv_b, m_conv_ln_g, m_conv_ln_b, m_w_out, m_norm2_g, m_w_ff1, m_w_ff2, m_final_g, v_norm1_g, v_w_in, v_sgu_ln_g, v_sgu_ln_b, v_sgu_w, v_sgu_b, v_conv_w, v_conv_b, v_conv_ln_g, v_conv_ln_b, v_w_out, v_norm2_g, v_w_ff1, v_w_ff2, v_final_g):
    given = dict(x=x, norm1_g=norm1_g, w_in=w_in, sgu_ln_g=sgu_ln_g, sgu_ln_b=sgu_ln_b, sgu_w=sgu_w, sgu_b=sgu_b, conv_w=conv_w, conv_b=conv_b, conv_ln_g=conv_ln_g, conv_ln_b=conv_ln_b, w_out=w_out, norm2_g=norm2_g, w_ff1=w_ff1, w_ff2=w_ff2, final_g=final_g, loss_target=loss_target, m_norm1_g=m_norm1_g, m_w_in=m_w_in, m_sgu_ln_g=m_sgu_ln_g, m_sgu_ln_b=m_sgu_ln_b, m_sgu_w=m_sgu_w, m_sgu_b=m_sgu_b, m_conv_w=m_conv_w, m_conv_b=m_conv_b, m_conv_ln_g=m_conv_ln_g, m_conv_ln_b=m_conv_ln_b, m_w_out=m_w_out, m_norm2_g=m_norm2_g, m_w_ff1=m_w_ff1, m_w_ff2=m_w_ff2, m_final_g=m_final_g, v_norm1_g=v_norm1_g, v_w_in=v_w_in, v_sgu_ln_g=v_sgu_ln_g, v_sgu_ln_b=v_sgu_ln_b, v_sgu_w=v_sgu_w, v_sgu_b=v_sgu_b, v_conv_w=v_conv_w, v_conv_b=v_conv_b, v_conv_ln_g=v_conv_ln_g, v_conv_ln_b=v_conv_ln_b, v_w_out=v_w_out, v_norm2_g=v_norm2_g, v_w_ff1=v_w_ff1, v_w_ff2=v_w_ff2, v_final_g=v_final_g)
    weights = {n: given[n] for n in TWIN_WEIGHTS}
    shared = {n: given[n] for n in SHARED_INPUTS}
    per_example = {n: given[n] for n in ['x']}
    grad_fn = _jax.value_and_grad(_loss, argnums=(0, 1))

    def one_microbatch(ex, loss_target):
        ex = dict(ex)
        diff = ex.pop(TWIN_DIFF_INPUT)
        return grad_fn(weights, diff, {**shared, **ex}, loss_target)

    if N_MICROBATCH == 1:
        loss, (grad_w, grad_x) = one_microbatch(per_example, given["loss_target"])
    else:
        def body(carry, xs):
            loss_sum, grad_sum = carry
            l_k, (gw_k, gx_k) = one_microbatch(xs[0], xs[1])
            with _jax.named_scope("update"):
                return (loss_sum + l_k, _jax.tree.map(_jnp.add, grad_sum, gw_k)), gx_k

        init = (_jnp.zeros((), _jnp.float32), _jax.tree.map(_jnp.zeros_like, weights))
        (loss, grad_w), grad_x = _jax.lax.scan(body, init, (per_example, given["loss_target"]))
    with _jax.named_scope("update"):
        delta_w, new_m, new_v = {}, {}, {}
        for n in TWIN_WEIGHTS:
            delta_w[n], new_m[n], new_v[n] = _adamw(weights[n], grad_w[n], given["m_" + n], given["v_" + n])
    return (loss, grad_x, *[grad_w[n] for n in TWIN_WEIGHTS], *[delta_w[n] for n in TWIN_WEIGHTS],
            *[new_m[n] for n in TWIN_WEIGHTS], *[new_v[n] for n in TWIN_WEIGHTS])
```

```python
import functools

import jax
import jax.numpy as jnp
from jax import lax
from jax.experimental import pallas as pl
from jax.experimental.pallas import tpu as pltpu

F32 = jnp.float32
BF16 = jnp.bfloat16

N_DEV = 8
DEPTH = 2
T = 4096
D = 1024
D_A = 512
D_B = 512
CHUNK = 128
H_A = 4
H_B = 4
CONV_W = 31
HALO = 32
D_FF = 4096
IN_COLS = 2048
EPS = 1e-6

ADAM_LR = 0.001
ADAM_B1 = 0.9
ADAM_B2 = 0.999
ADAM_EPS = 1e-08
ADAM_WD = 0.01
ADAM_STEP = 10

TM = 512
TM_MIX = 256
RB = 64
LANES = 128
MIB = 1024 * 1024

SQRT_HALF = 0.7071067811865476
INV_SQRT_2PI = 0.3989422804014327

MESH_ID = pl.DeviceIdType.MESH
ANY = pl.BlockSpec(memory_space=pl.ANY)


def _cparams(vmem_mib, sem=("arbitrary",)):
    return pltpu.CompilerParams(dimension_semantics=sem, vmem_limit_bytes=vmem_mib * MIB)


def _full(shape):
    return pl.BlockSpec(shape, lambda *_: (0,) * len(shape))


def _rows(tm, cols):
    return pl.BlockSpec((tm, cols), lambda i: (i, 0))


def _gelu(x):
    cdf = 0.5 * (1.0 + lax.erf(x * SQRT_HALF))
    return x * cdf, cdf


def _gelu_grad(x, cdf):
    return cdf + x * (INV_SQRT_2PI * jnp.exp(-0.5 * x * x))


def _sigmoid(x):
    return 1.0 / (1.0 + jnp.exp(-x))


def _ln(x):
    mu = jnp.mean(x, axis=-1, keepdims=True)
    xc = x - mu
    rstd = lax.rsqrt(jnp.mean(xc * xc, axis=-1, keepdims=True) + EPS)
    return xc * rstd, rstd


def _ln_bwd(dyh, xhat, rstd):
    return rstd * (dyh - jnp.mean(dyh, axis=-1, keepdims=True) - xhat * jnp.mean(dyh * xhat, axis=-1, keepdims=True))


def _rms(x):
    return lax.rsqrt(jnp.mean(x * x, axis=-1, keepdims=True) + EPS)


def _rms_bwd(dh, x, r, g):
    n = x * r
    dn = dh * g
    dx = r * (dn - n * jnp.mean(dn * n, axis=-1, keepdims=True))
    return dx, jnp.sum(dh * n, axis=0, keepdims=True)


def _dot(a, b):
    return jnp.dot(a, b, preferred_element_type=F32)


def _dot_nt(a, b):
    return lax.dot_general(a, b, (((1,), (1,)), ((), ())), preferred_element_type=F32)


def _dot_tn(a, b):
    return lax.dot_general(a, b, (((0,), (0,)), ((), ())), preferred_element_type=F32)


def _tril_mask():
    r = lax.broadcasted_iota(jnp.int32, (CHUNK, CHUNK), 0)
    c = lax.broadcasted_iota(jnp.int32, (CHUNK, CHUNK), 1)
    return r >= c


def _fwd_in(x, g1, wg):
    bn = wg.shape[2]

    def body(x_ref, g_ref, w_ref, h_ref, p_ref):
        xv = x_ref[...]
        h = (xv * _rms(xv) * g_ref[...]).astype(BF16)
        h_ref[...] = h
        for j in range(N_DEV):
            p_ref[:, j * bn:(j + 1) * bn] = _dot(h, w_ref[j])

    return pl.pallas_call(
        body, name="fwd_in", grid=(T // TM,),
        in_specs=[_rows(TM, D), _full((1, D)), _full(wg.shape)],
        out_specs=[_rows(TM, D), _rows(TM, IN_COLS)],
        out_shape=[jax.ShapeDtypeStruct((T, D), BF16), jax.ShapeDtypeStruct((T, IN_COLS), F32)],
        compiler_params=_cparams(32),
    )(x, g1, wg)


def _conv_taps(src_ref, w_ref, cols, base, first_tap_row, step):
    acc = jnp.zeros((RB, LANES), F32)
    for k in range(CONV_W):
        r0 = base + first_tap_row + step * k
        acc = acc + src_ref[r0:r0 + RB, cols] * w_ref[k:k + 1, cols]
    return acc


def _mixer_fwd(proj, lg, lb, wm, bst, cw, cb, clg, clb):
    tm = TM_MIX
    hb = tm // HALO

    def body(p_ref, ph_ref, lg_ref, lb_ref, wm_ref, bs_ref, cw_ref, cb_ref, clg_ref, clb_ref,
             mix_ref, c_ref, g_ref, gbuf):
        i = pl.program_id(0)
        u, _ = _gelu(p_ref[:, 0:D_A])
        vg, _ = _gelu(p_ref[:, D_A:2 * D_A])
        xhat, _ = _ln(vg)
        v = (xhat * lg_ref[...] + lb_ref[...]).astype(BF16)
        mask = _tril_mask()
        for h in range(H_A):
            hc = slice(h * CHUNK, (h + 1) * CHUNK)
            wmh = jnp.where(mask, wm_ref[h], 0.0).astype(BF16)
            for c in range(tm // CHUNK):
                rc = slice(c * CHUNK, (c + 1) * CHUNK)
                mixed = _dot(wmh, v[rc, hc]) + bs_ref[:, h:h + 1]
                mix_ref[rc, hc] = (u[rc, hc] * mixed).astype(BF16)

        g = p_ref[:, 2 * D_A:2 * D_A + D_B] * _sigmoid(p_ref[:, 2 * D_A + D_B:IN_COLS])
        g_ref[...] = g
        gh = ph_ref[:, 0:D_B] * _sigmoid(ph_ref[:, D_B:2 * D_B])
        gbuf[0:HALO, :] = jnp.where(i > 0, gh, 0.0)
        gbuf[HALO:HALO + tm, :] = g
        for q in range(H_B):
            cols = slice(q * LANES, (q + 1) * LANES)
            for rb in range(tm // RB):
                acc = _conv_taps(gbuf, cw_ref, cols, rb * RB, HALO - (CONV_W - 1), 1)
                c_ref[rb * RB:(rb + 1) * RB, cols] = acc + cb_ref[:, cols]
        for q in range(H_B):
            cols = slice(q * LANES, (q + 1) * LANES)
            chat, _ = _ln(c_ref[:, cols])
            z = chat * clg_ref[:, cols] + clb_ref[:, cols]
            mix_ref[:, D_A + q * LANES:D_A + (q + 1) * LANES] = (z * _sigmoid(z)).astype(BF16)

    vec = _full((1, D_A))
    return pl.pallas_call(
        body, name="mixer_fwd", grid=(T // tm,),
        in_specs=[_rows(tm, IN_COLS),
                  pl.BlockSpec((HALO, 2 * D_B), lambda i: (jnp.maximum(i * hb - 1, 0), 1)),
                  vec, vec, _full((H_A, CHUNK, CHUNK)), _full((CHUNK, H_A)),
                  _full((CONV_W, D_B)), vec, vec, vec],
        out_specs=[_rows(tm, D), _rows(tm, D_B), _rows(tm, D_B)],
        out_shape=[jax.ShapeDtypeStruct((T, D), BF16), jax.ShapeDtypeStruct((T, D_B), F32),
                   jax.ShapeDtypeStruct((T, D_B), F32)],
        scratch_shapes=[pltpu.VMEM((HALO + tm, D_B), F32)],
        compiler_params=_cparams(32),
    )(proj, proj, lg, lb, wm, bst, cw, cb, clg, clb)


def _fwd_out(mix, wout, x, g2):
    def body(m_ref, w_ref, x_ref, g_ref, x1_ref, h2_ref):
        x1 = x_ref[...] + _dot(m_ref[...], w_ref[...])
        x1_ref[...] = x1
        h2_ref[...] = (x1 * _rms(x1) * g_ref[...]).astype(BF16)

    return pl.pallas_call(
        body, name="fwd_out", grid=(T // TM,),
        in_specs=[_rows(TM, D), _full((D, D)), _rows(TM, D), _full((1, D))],
        out_specs=[_rows(TM, D), _rows(TM, D)],
        out_shape=[jax.ShapeDtypeStruct((T, D), F32), jax.ShapeDtypeStruct((T, D), BF16)],
        compiler_params=_cparams(32),
    )(mix, wout, x, g2)


def _fwd_ff1(h2, wg):
    bn = wg.shape[2]

    def body(h_ref, w_ref, r_ref, f_ref):
        h = h_ref[...]
        for j in range(N_DEV):
            r = jnp.maximum(_dot(h, w_ref[j]), 0.0)
            r_ref[:, j * bn:(j + 1) * bn] = r.astype(BF16)
            f_ref[:, j * bn:(j + 1) * bn] = (r * r).astype(BF16)

    return pl.pallas_call(
        body, name="fwd_ff1", grid=(T // TM,),
        in_specs=[_rows(TM, D), _full(wg.shape)],
        out_specs=[_rows(TM, D_FF), _rows(TM, D_FF)],
        out_shape=[jax.ShapeDtypeStruct((T, D_FF), BF16), jax.ShapeDtypeStruct((T, D_FF), BF16)],
        compiler_params=_cparams(48),
    )(h2, wg)


def _fwd_ff2(f, w2, x1):
    def body(f_ref, w_ref, x_ref, o_ref):
        o_ref[...] = x_ref[...] + _dot(f_ref[...], w_ref[...])

    return pl.pallas_call(
        body, name="fwd_ff2", grid=(T // TM,),
        in_specs=[_rows(TM, D_FF), _full((D_FF, D)), _rows(TM, D)],
        out_specs=_rows(TM, D),
        out_shape=jax.ShapeDtypeStruct((T, D), F32),
        compiler_params=_cparams(48),
    )(f, w2, x1)


def _loss_bwd(xf, gf, tgt):
    def body(x_ref, g_ref, t_ref, dx_ref, dxb_ref, loss_ref, dg_ref):
        i = pl.program_id(0)

        @pl.when(i == 0)
        def _():
            loss_ref[...] = jnp.zeros(loss_ref.shape, F32)
            dg_ref[...] = jnp.zeros(dg_ref.shape, F32)

        xv = x_ref[...]
        r = _rms(xv)
        diff = xv * r * g_ref[...] - t_ref[...]
        loss_ref[...] += 0.5 * jnp.sum(jnp.mean(diff * diff, axis=-1, keepdims=True), axis=0, keepdims=True)
        dx, dg = _rms_bwd(diff * (1.0 / D), xv, r, g_ref[...])
        dx_ref[...] = dx
        dxb_ref[...] = dx.astype(BF16)
        dg_ref[...] += dg

    return pl.pallas_call(
        body, name="loss_bwd", grid=(T // TM,),
        in_specs=[_rows(TM, D), _full((1, D)), _rows(TM, D)],
        out_specs=[_rows(TM, D), _rows(TM, D), _full((8, LANES)), _full((1, D))],
        out_shape=[jax.ShapeDtypeStruct((T, D), F32), jax.ShapeDtypeStruct((T, D), BF16),
                   jax.ShapeDtypeStruct((8, LANES), F32), jax.ShapeDtypeStruct((1, D), F32)],
        compiler_params=_cparams(32),
    )(xf, gf, tgt)


def _bwd_ff2(dxb, w2, r):
    kb = 1024

    def body(d_ref, w_ref, r_ref, o_ref):
        d = d_ref[...]
        for j in range(D_FF // kb):
            cols = slice(j * kb, (j + 1) * kb)
            df = _dot_nt(d, w_ref[cols, :])
            o_ref[:, cols] = (2.0 * r_ref[:, cols].astype(F32) * df).astype(BF16)

    return pl.pallas_call(
        body, name="bwd_ff2", grid=(T // TM,),
        in_specs=[_rows(TM, D), _full((D_FF, D)), _rows(TM, D_FF)],
        out_specs=_rows(TM, D_FF),
        out_shape=jax.ShapeDtypeStruct((T, D_FF), BF16),
        compiler_params=_cparams(48),
    )(dxb, w2, r)


def _bwd_norm_matmul(name, dy, wg, x, g, dres):
    bn = wg.shape[2]
    ncol = dy.shape[1]

    def body(dy_ref, w_ref, x_ref, g_ref, dres_ref, dx_ref, dxb_ref, dg_ref):
        i = pl.program_id(0)

        @pl.when(i == 0)
        def _():
            dg_ref[...] = jnp.zeros(dg_ref.shape, F32)

        dh = jnp.zeros((TM, D), F32)
        for j in range(N_DEV):
            dh = dh + _dot_nt(dy_ref[:, j * bn:(j + 1) * bn], w_ref[j])
        xv = x_ref[...]
        dxn, dg = _rms_bwd(dh, xv, _rms(xv), g_ref[...])
        dx = dres_ref[...] + dxn
        dx_ref[...] = dx
        dxb_ref[...] = dx.astype(BF16)
        dg_ref[...] += dg

    return pl.pallas_call(
        body, name=name, grid=(T // TM,),
        in_specs=[_rows(TM, ncol), _full(wg.shape), _rows(TM, D), _full((1, D)), _rows(TM, D)],
        out_specs=[_rows(TM, D), _rows(TM, D), _full((1, D))],
        out_shape=[jax.ShapeDtypeStruct((T, D), F32), jax.ShapeDtypeStruct((T, D), BF16),
                   jax.ShapeDtypeStruct((1, D), F32)],
        compiler_params=_cparams(48),
    )(dy, wg, x, g, dres)


def _mixer_bwd_a(dxb, wout, proj, c, lg, lb, wm, bst, clg, clb):
    tm = TM_MIX
    n_tiles = T // tm

    def body(dx_ref, wo_ref, p_ref, c_ref, lg_ref, lb_ref, wm_ref, bs_ref, clg_ref, clb_ref,
             dpa_ref, dc_ref, dlg_ref, dlb_ref, dwm_ref, dbs_ref, dcb_ref, dclg_ref, dclb_ref,
             dv_buf, db_acc):
        i = pl.program_id(0)

        @pl.when(i == 0)
        def _():
            for ref in (dlg_ref, dlb_ref, dwm_ref, dbs_ref, dcb_ref, dclg_ref, dclb_ref, db_acc):
                ref[...] = jnp.zeros(ref.shape, F32)

        dmix = _dot_nt(dx_ref[...], wo_ref[...])
        ua = p_ref[:, 0:D_A]
        va = p_ref[:, D_A:2 * D_A]
        u, cdf_u = _gelu(ua)
        vg, cdf_v = _gelu(va)
        xhat, rstd = _ln(vg)
        v = (xhat * lg_ref[...] + lb_ref[...]).astype(BF16)
        mask = _tril_mask()
        for h in range(H_A):
            hc = slice(h * CHUNK, (h + 1) * CHUNK)
            wmh = jnp.where(mask, wm_ref[h], 0.0).astype(BF16)
            for cidx in range(tm // CHUNK):
                rc = slice(cidx * CHUNK, (cidx + 1) * CHUNK)
                vb = v[rc, hc]
                mixed = _dot(wmh, vb) + bs_ref[:, h:h + 1]
                da = dmix[rc, hc]
                dpa_ref[rc, hc] = (da * mixed * _gelu_grad(ua[rc, hc], cdf_u[rc, hc])).astype(BF16)
                dmixed = da * u[rc, hc]
                dmb = dmixed.astype(BF16)
                dv_buf[rc, hc] = _dot_tn(wmh, dmb)
                dwm_ref[h] += _dot_nt(dmb, vb)
                db_acc[:, hc] += dmixed
        dv = dv_buf[...]
        dlb_ref[...] += jnp.sum(dv, axis=0, keepdims=True)
        dlg_ref[...] += jnp.sum(dv * xhat, axis=0, keepdims=True)
        dvg = _ln_bwd(dv * lg_ref[...], xhat, rstd)
        dpa_ref[:, D_A:2 * D_A] = (dvg * _gelu_grad(va, cdf_v)).astype(BF16)

        for q in range(H_B):
            cols = slice(q * LANES, (q + 1) * LANES)
            chat, crstd = _ln(c_ref[:, cols])
            z = chat * clg_ref[:, cols] + clb_ref[:, cols]
            sg = _sigmoid(z)
            dz = dmix[:, D_A + q * LANES:D_A + (q + 1) * LANES] * (sg * (1.0 + z * (1.0 - sg)))
            dclb_ref[:, cols] += jnp.sum(dz, axis=0, keepdims=True)
            dclg_ref[:, cols] += jnp.sum(dz * chat, axis=0, keepdims=True)
            dc = _ln_bwd(dz * clg_ref[:, cols], chat, crstd)
            dc_ref[:, cols] = dc
            dcb_ref[:, cols] += jnp.sum(dc, axis=0, keepdims=True)

        @pl.when(i == n_tiles - 1)
        def _():
            for h in range(H_A):
                hc = slice(h * CHUNK, (h + 1) * CHUNK)
                dwm_ref[h] = jnp.where(mask, dwm_ref[h], 0.0)
                dbs_ref[h:h + 1, :] = jnp.sum(db_acc[:, hc].T, axis=0, keepdims=True)

    vec = _full((1, D_A))
    vshape = jax.ShapeDtypeStruct((1, D_A), F32)
    return pl.pallas_call(
        body, name="mixer_bwd_a", grid=(n_tiles,),
        in_specs=[_rows(tm, D), _full((D, D)), _rows(tm, IN_COLS), _rows(tm, D_B), vec, vec,
                  _full((H_A, CHUNK, CHUNK)), _full((CHUNK, H_A)), vec, vec],
        out_specs=[_rows(tm, 2 * D_A), _rows(tm, D_B), vec, vec, _full((H_A, CHUNK, CHUNK)),
                   _full((H_A, CHUNK)), vec, vec, vec],
        out_shape=[jax.ShapeDtypeStruct((T, 2 * D_A), BF16), jax.ShapeDtypeStruct((T, D_B), F32), vshape, vshape,
                   jax.ShapeDtypeStruct((H_A, CHUNK, CHUNK), F32), jax.ShapeDtypeStruct((H_A, CHUNK), F32),
                   vshape, vshape, vshape],
        scratch_shapes=[pltpu.VMEM((tm, D_A), F32), pltpu.VMEM((CHUNK, D_A), F32)],
        compiler_params=_cparams(32),
    )(dxb, wout, proj, c, lg, lb, wm, bst, clg, clb)


def _mixer_bwd_b(dc, g, proj, cw, dpa):
    tm = TM_MIX
    n_tiles = T // tm
    hb = tm // HALO

    def body(dc_ref, dch_ref, g_ref, gh_ref, p_ref, cw_ref, dpa_ref, dp_ref, dcw_ref, dcbuf, gbuf, dwacc):
        i = pl.program_id(0)

        @pl.when(i == 0)
        def _():
            dwacc[...] = jnp.zeros(dwacc.shape, F32)

        dcbuf[0:tm, :] = dc_ref[...]
        dcbuf[tm:tm + HALO, :] = jnp.where(i < n_tiles - 1, dch_ref[...], 0.0)
        gbuf[0:HALO, :] = jnp.where(i > 0, gh_ref[...], 0.0)
        gbuf[HALO:HALO + tm, :] = g_ref[...]
        dp_ref[:, 0:2 * D_A] = dpa_ref[...]
        for q in range(H_B):
            cols = slice(q * LANES, (q + 1) * LANES)
            for rb in range(tm // RB):
                rows = slice(rb * RB, (rb + 1) * RB)
                dg = _conv_taps(dcbuf, cw_ref, cols, rb * RB, CONV_W - 1, -1)
                val = p_ref[rows, cols]
                sg = _sigmoid(p_ref[rows, D_B + q * LANES:D_B + (q + 1) * LANES])
                dp_ref[rows, 2 * D_A + q * LANES:2 * D_A + (q + 1) * LANES] = (dg * sg).astype(BF16)
                dp_ref[rows, 2 * D_A + D_B + q * LANES:2 * D_A + D_B + (q + 1) * LANES] = (
                    dg * val * sg * (1.0 - sg)).astype(BF16)
                dcv = dc_ref[rows, cols]
                for k in range(CONV_W):
                    r0 = rb * RB + HALO - (CONV_W - 1) + k
                    prod = dcv * gbuf[r0:r0 + RB, cols]
                    dwacc[k * 8:(k + 1) * 8, cols] += jnp.sum(prod.reshape(RB // 8, 8, LANES), axis=0)

        @pl.when(i == n_tiles - 1)
        def _():
            for k in range(CONV_W):
                dcw_ref[k:k + 1, :] = jnp.sum(dwacc[k * 8:(k + 1) * 8, :], axis=0, keepdims=True)

    return pl.pallas_call(
        body, name="mixer_bwd_b", grid=(n_tiles,),
        in_specs=[_rows(tm, D_B),
                  pl.BlockSpec((HALO, D_B), lambda i: (jnp.minimum((i + 1) * hb, T // HALO - 1), 0)),
                  _rows(tm, D_B),
                  pl.BlockSpec((HALO, D_B), lambda i: (jnp.maximum(i * hb - 1, 0), 0)),
                  pl.BlockSpec((tm, 2 * D_B), lambda i: (i, 1)),
                  _full((CONV_W, D_B)), _rows(tm, 2 * D_A)],
        out_specs=[_rows(tm, IN_COLS), _full((CONV_W, D_B))],
        out_shape=[jax.ShapeDtypeStruct((T, IN_COLS), BF16), jax.ShapeDtypeStruct((CONV_W, D_B), F32)],
        scratch_shapes=[pltpu.VMEM((tm + HALO, D_B), F32), pltpu.VMEM((HALO + tm, D_B), F32),
                        pltpu.VMEM((CONV_W * 8, D_B), F32)],
        compiler_params=_cparams(32),
    )(dc, dc, g, g, proj, cw, dpa)


def _wgrad(name, a, g, bn):
    k = a.shape[1]
    n = g.shape[1]
    tk = min(k, 1024)
    tn = min(n, max(bn, 1024))
    nsub = tn // bn
    tt = TM
    nt = T // tt

    def body(a_ref, g_ref, o_ref, ob_ref):
        t = pl.program_id(2)

        @pl.when(t == 0)
        def _():
            o_ref[...] = jnp.zeros(o_ref.shape, F32)

        av = a_ref[...]
        for s in range(nsub):
            o_ref[s] += _dot_tn(av, g_ref[:, s * bn:(s + 1) * bn])

        @pl.when(t == nt - 1)
        def _():
            ob_ref[...] = o_ref[...].astype(BF16)

    ospec = pl.BlockSpec((nsub, tk, bn), lambda ki, ni, ti: (ni, ki, 0))
    return pl.pallas_call(
        body, name=name, grid=(k // tk, n // tn, nt),
        in_specs=[pl.BlockSpec((tt, tk), lambda ki, ni, ti: (ti, ki)),
                  pl.BlockSpec((tt, tn), lambda ki, ni, ti: (ti, ni))],
        out_specs=[ospec, ospec],
        out_shape=[jax.ShapeDtypeStruct((n // bn, k, bn), F32), jax.ShapeDtypeStruct((n // bn, k, bn), BF16)],
        compiler_params=_cparams(40, ("arbitrary", "arbitrary", "arbitrary")),
    )(a, g)


def _coords():
    return lax.axis_index("x"), lax.axis_index("y"), lax.axis_index("c")


def _flat(x, y, c):
    return 4 * x + 2 * y + c


def _all_gather(shards):
    n = len(shards)

    def body(*refs):
        ins, outs = refs[:n], refs[n:2 * n]
        send_sems, recv_sems, local_sems = refs[2 * n:]
        x, y, c = _coords()
        me, sibling = (x, y, c), (x, y, 1 - c)
        chips = [(1 - x, y), (x, 1 - y), (1 - x, 1 - y)]

        def copy(a, k, block, to, src=None):
            dst = outs[a].at[_flat(*block)]
            return pltpu.make_async_remote_copy(
                src_ref=dst if src is None else src, dst_ref=dst,
                send_sem=send_sems.at[a, k], recv_sem=recv_sems.at[a, k],
                device_id=to, device_id_type=MESH_ID)

        mine = [pltpu.make_async_copy(ins[a], outs[a].at[_flat(*me)], local_sems.at[a]) for a in range(n)]
        for cp in mine:
            cp.start()
        first = []
        for a in range(n):
            first.append(copy(a, 0, me, sibling, src=ins[a]))
            first += [copy(a, 1 + j, me, (*chip, c), src=ins[a]) for j, chip in enumerate(chips)]
        for cp in first:
            cp.start()
        passed = []
        for a in range(n):
            for j, chip in enumerate(chips):
                copy(a, 1 + j, (*chip, c), me).wait_recv()
                fwd = copy(a, 4 + j, (*chip, c), sibling)
                fwd.start()
                passed.append(fwd)
        for a in range(n):
            copy(a, 0, sibling, me).wait_recv()
            for j, chip in enumerate(chips):
                copy(a, 4 + j, (*chip, 1 - c), me).wait_recv()
        for cp in first + passed:
            cp.wait_send()
        for cp in mine:
            cp.wait()

    return pl.pallas_call(
        body, name="all_gather",
        in_specs=[ANY] * n, out_specs=[ANY] * n,
        out_shape=[jax.ShapeDtypeStruct((N_DEV,) + s.shape, s.dtype) for s in shards],
        scratch_shapes=[pltpu.SemaphoreType.DMA((n, 7)), pltpu.SemaphoreType.DMA((n, 7)),
                        pltpu.SemaphoreType.DMA((n,))],
    )(*shards)


def _peer(k):
    x, y, c = _coords()
    return (x ^ ((k >> 2) & 1), y ^ ((k >> 1) & 1), c ^ (k & 1))


def _grad_exchange(grads_f32, grads_bf16):
    nw = len(grads_f32)
    flat32 = [g for w in grads_f32 for g in w]
    flat16 = [g for w in grads_bf16 for g in w]
    n = len(flat32)

    def body(*refs):
        g32, g16 = refs[:n], refs[n:2 * n]
        own, recv = refs[2 * n:2 * n + nw], refs[2 * n + nw:2 * n + 2 * nw]
        send_sems, recv_sems, local_sems = refs[2 * n + 2 * nw:]
        me = _flat(*_coords())
        local = []
        remote = []
        for w in range(nw):
            for l in range(DEPTH):
                a = w * DEPTH + l
                local.append(pltpu.make_async_copy(g32[a].at[me], own[w].at[l], local_sems.at[a]))
                for k in range(1, N_DEV):
                    to = _peer(k)
                    remote.append(pltpu.make_async_remote_copy(
                        src_ref=g16[a].at[_flat(*to)], dst_ref=recv[w].at[l, k - 1],
                        send_sem=send_sems.at[a, k - 1], recv_sem=recv_sems.at[a, k - 1],
                        device_id=to, device_id_type=MESH_ID))
        for cp in local + remote:
            cp.start()
        for cp in remote:
            cp.wait()
        for cp in local:
            cp.wait()

    out_shape = ([jax.ShapeDtypeStruct((DEPTH,) + w[0].shape[1:], F32) for w in grads_f32]
                 + [jax.ShapeDtypeStruct((DEPTH, N_DEV - 1) + w[0].shape[1:], BF16) for w in grads_f32])
    outs = pl.pallas_call(
        body, name="grad_exchange",
        in_specs=[ANY] * (2 * n), out_specs=[ANY] * (2 * nw), out_shape=out_shape,
        scratch_shapes=[pltpu.SemaphoreType.DMA((n, N_DEV - 1)), pltpu.SemaphoreType.DMA((n, N_DEV - 1)),
                        pltpu.SemaphoreType.DMA((n,))],
    )(*flat32, *flat16)
    return outs[:nw], outs[nw:]


def _all_reduce_small(part):
    rows = part.shape[0]

    def body(p_ref, o_ref, slots, send_sems, recv_sems):
        me = _flat(*_coords())
        slots[me] = p_ref[...]
        copies = []
        for k in range(1, N_DEV):
            copies.append(pltpu.make_async_remote_copy(
                src_ref=p_ref, dst_ref=slots.at[me],
                send_sem=send_sems.at[k - 1], recv_sem=recv_sems.at[k - 1],
                device_id=_peer(k), device_id_type=MESH_ID))
        for cp in copies:
            cp.start()
        for cp in copies:
            cp.wait()
        acc = slots[0]
        for d in range(1, N_DEV):
            acc = acc + slots[d]
        o_ref[...] = acc

    return pl.pallas_call(
        body, name="all_reduce_small",
        in_specs=[pl.BlockSpec(memory_space=pltpu.VMEM)], out_specs=pl.BlockSpec(memory_space=pltpu.VMEM),
        out_shape=jax.ShapeDtypeStruct(part.shape, F32),
        scratch_shapes=[pltpu.VMEM((N_DEV, rows, LANES), F32), pltpu.SemaphoreType.DMA((N_DEV - 1,)),
                        pltpu.SemaphoreType.DMA((N_DEV - 1,))],
        compiler_params=pltpu.CompilerParams(vmem_limit_bytes=24 * MIB),
    )(part)


def _adam_math(w, g, m, v):
    m = ADAM_B1 * m + (1.0 - ADAM_B1) * g
    v = ADAM_B2 * v + (1.0 - ADAM_B2) * (g * g)
    m_hat = m / (1.0 - ADAM_B1 ** ADAM_STEP)
    v_hat = v / (1.0 - ADAM_B2 ** ADAM_STEP)
    delta = -ADAM_LR * (m_hat / (jnp.sqrt(v_hat) + ADAM_EPS) + ADAM_WD * w)
    return delta, m, v


def _adam_sharded(name, own, recv, w, m, v):
    _, r, c = w.shape
    tr = min(r, 256)

    def body(own_ref, recv_ref, w_ref, m_ref, v_ref, g_ref, d_ref, nm_ref, nv_ref):
        g = own_ref[...]
        for k in range(N_DEV - 1):
            g = g + recv_ref[k].astype(F32)
        delta, nm, nv = _adam_math(w_ref[...], g, m_ref[...], v_ref[...])
        g_ref[...] = g
        d_ref[...] = delta
        nm_ref[...] = nm
        nv_ref[...] = nv

    blk = pl.BlockSpec((None, tr, c), lambda l, i: (l, i, 0))
    shp = jax.ShapeDtypeStruct(w.shape, F32)
    return pl.pallas_call(
        body, name=name, grid=(DEPTH, r // tr),
        in_specs=[blk, pl.BlockSpec((None, N_DEV - 1, tr, c), lambda l, i: (l, 0, i, 0)), blk, blk, blk],
        out_specs=[blk] * 4, out_shape=[shp] * 4,
        compiler_params=_cparams(32, ("arbitrary", "arbitrary")),
    )(own, recv, w, m, v)


def _adam_small(g, w, m, v):
    rows = g.shape[0]

    def body(g_ref, w_ref, m_ref, v_ref, d_ref, nm_ref, nv_ref):
        delta, nm, nv = _adam_math(w_ref[...], g_ref[...], m_ref[...], v_ref[...])
        d_ref[...] = delta
        nm_ref[...] = nm
        nv_ref[...] = nv

    spec = _full((rows, LANES))
    shp = jax.ShapeDtypeStruct((rows, LANES), F32)
    return pl.pallas_call(
        body, name="adam_small", grid=(1,),
        in_specs=[spec] * 4, out_specs=[spec] * 3, out_shape=[shp] * 3,
        compiler_params=_cparams(24),
    )(g, w, m, v)


SMALL = ["norm1_g", "sgu_ln_g", "sgu_ln_b", "sgu_w", "sgu_b", "conv_b", "conv_ln_g", "conv_ln_b", "norm2_g",
         "final_g"]


def _pack(arrays):
    flat = jnp.concatenate([a.reshape(-1) for a in arrays])
    pad = (-flat.shape[0]) % (8 * LANES)
    return jnp.pad(flat, (0, pad)).reshape(-1, LANES)


def _unpack(packed, shapes):
    flat = packed.reshape(-1)
    out, off = [], 0
    for s in shapes:
        size = 1
        for d in s:
            size *= d
        out.append(flat[off:off + size].reshape(s))
        off += size
    return out


def kernel(x, norm1_g, w_in, sgu_ln_g, sgu_ln_b, sgu_w, sgu_b, conv_w, conv_b, conv_ln_g, conv_ln_b, w_out, norm2_g, w_ff1, w_ff2, final_g, loss_target, m_norm1_g, m_w_in, m_sgu_ln_g, m_sgu_ln_b, m_sgu_w, m_sgu_b, m_conv_w, m_conv_b, m_conv_ln_g, m_conv_ln_b, m_w_out, m_norm2_g, m_w_ff1, m_w_ff2, m_final_g, v_norm1_g, v_w_in, v_sgu_ln_g, v_sgu_ln_b, v_sgu_w, v_sgu_b, v_conv_w, v_conv_b, v_conv_ln_g, v_conv_ln_b, v_w_out, v_norm2_g, v_w_ff1, v_w_ff2, v_final_g):
    x2d = x.reshape(T, D)
    tgt = loss_target.reshape(T, D)
    cw_shard = conv_w.reshape(CONV_W, LANES)

    shards = []
    for l in range(DEPTH):
        shards += [w_in[l].astype(BF16), w_out[l].astype(BF16), w_ff1[l].astype(BF16), w_ff2[l].astype(BF16)]
    shards.append(cw_shard)
    gathered = _all_gather(shards)
    cw_full = gathered[-1].reshape(N_DEV, DEPTH, CONV_W, D_B // N_DEV).transpose(1, 2, 0, 3).reshape(
        DEPTH, CONV_W, D_B)

    saved = []
    xl = x2d
    for l in range(DEPTH):
        wg_in, wout_g, wg_ff1, w2_g = gathered[4 * l:4 * l + 4]
        wout = wout_g.reshape(D, D)
        w2 = w2_g.reshape(D_FF, D)
        g1 = norm1_g[l].reshape(1, D)
        g2 = norm2_g[l].reshape(1, D)
        lg, lb = sgu_ln_g[l].reshape(1, D_A), sgu_ln_b[l].reshape(1, D_A)
        bst = sgu_b[l].T
        cb = conv_b[l].reshape(1, D_B)
        clg, clb = conv_ln_g[l].reshape(1, D_B), conv_ln_b[l].reshape(1, D_B)
        h, proj = _fwd_in(xl, g1, wg_in)
        mix, c, g = _mixer_fwd(proj, lg, lb, sgu_w[l], bst, cw_full[l], cb, clg, clb)
        x1, h2 = _fwd_out(mix, wout, xl, g2)
        r, f = _fwd_ff1(h2, wg_ff1)
        x2 = _fwd_ff2(f, w2, x1)
        saved.append(dict(x=xl, h=h, proj=proj, mix=mix, c=c, g=g, x1=x1, h2=h2, r=r, f=f, wg_in=wg_in, wout=wout,
                          wg_ff1=wg_ff1, w2=w2, g1=g1, g2=g2, lg=lg, lb=lb, bst=bst, clg=clg, clb=clb))
        xl = x2

    dx, dxb, loss_part, d_final_g = _loss_bwd(xl, final_g.reshape(1, D), tgt)
    loss = lax.psum(loss_part[0, 0], ("x", "y", "c"))

    small = {}
    big32 = {"w_in": [None] * DEPTH, "w_out": [None] * DEPTH, "w_ff1": [None] * DEPTH, "w_ff2": [None] * DEPTH}
    big16 = {k: [None] * DEPTH for k in big32}
    d_conv_w = [None] * DEPTH
    for l in reversed(range(DEPTH)):
        s = saved[l]
        df1 = _bwd_ff2(dxb, s["w2"], s["r"])
        big32["w_ff2"][l], big16["w_ff2"][l] = _wgrad("wgrad_ff2", s["f"], dxb, D)
        dx1, dx1b, dg2 = _bwd_norm_matmul("bwd_ff1", df1, s["wg_ff1"], s["x1"], s["g2"], dx)
        big32["w_ff1"][l], big16["w_ff1"][l] = _wgrad("wgrad_ff1", s["h2"], df1, D_FF // N_DEV)
        dpa, dc, dlg, dlb, dwm, dbs, dcb, dclg, dclb = _mixer_bwd_a(
            dx1b, s["wout"], s["proj"], s["c"], s["lg"], s["lb"], sgu_w[l], s["bst"], s["clg"], s["clb"])
        big32["w_out"][l], big16["w_out"][l] = _wgrad("wgrad_out", s["mix"], dx1b, D)
        dproj, d_conv_w[l] = _mixer_bwd_b(dc, s["g"], s["proj"], cw_full[l], dpa)
        dx, dxb, dg1 = _bwd_norm_matmul("bwd_in", dproj, s["wg_in"], s["x"], s["g1"], dx1)
        big32["w_in"][l], big16["w_in"][l] = _wgrad("wgrad_in", s["h"], dproj, IN_COLS // N_DEV)
        small[l] = dict(norm1_g=dg1, sgu_ln_g=dlg, sgu_ln_b=dlb, sgu_w=dwm, sgu_b=dbs, conv_b=dcb, conv_ln_g=dclg,
                        conv_ln_b=dclb, norm2_g=dg2)
    grad_x = dx.reshape(1, T, D)

    for name, rows in (("w_out", D // N_DEV), ("w_ff2", D_FF // N_DEV)):
        big32[name] = [a.reshape(N_DEV, rows, D) for a in big32[name]]
        big16[name] = [a.reshape(N_DEV, rows, D) for a in big16[name]]
    names = ["w_in", "w_out", "w_ff1", "w_ff2"]
    own, recv = _grad_exchange([big32[k] for k in names], [big16[k] for k in names])
    wmv = dict(w_in=(w_in, m_w_in, v_w_in), w_out=(w_out, m_w_out, v_w_out), w_ff1=(w_ff1, m_w_ff1, v_w_ff1),
               w_ff2=(w_ff2, m_w_ff2, v_w_ff2))
    res = {}
    for idx, k in enumerate(names):
        res[k] = _adam_sharded("adam_" + k, own[idx], recv[idx], *wmv[k])

    rep = dict(norm1_g=(norm1_g, m_norm1_g, v_norm1_g), sgu_ln_g=(sgu_ln_g, m_sgu_ln_g, v_sgu_ln_g),
               sgu_ln_b=(sgu_ln_b, m_sgu_ln_b, v_sgu_ln_b), sgu_w=(sgu_w, m_sgu_w, v_sgu_w),
               sgu_b=(sgu_b, m_sgu_b, v_sgu_b), conv_b=(conv_b, m_conv_b, v_conv_b),
               conv_ln_g=(conv_ln_g, m_conv_ln_g, v_conv_ln_g), conv_ln_b=(conv_ln_b, m_conv_ln_b, v_conv_ln_b),
               norm2_g=(norm2_g, m_norm2_g, v_norm2_g), final_g=(final_g, m_final_g, v_final_g))
    parts = []
    for k in SMALL:
        if k == "final_g":
            parts.append(d_final_g.reshape(rep[k][0].shape))
        else:
            parts.append(jnp.stack([small[l][k].reshape(rep[k][0].shape[1:]) for l in range(DEPTH)]))
    parts.append(jnp.stack(d_conv_w))
    shapes = [p.shape for p in parts]
    summed = _unpack(_all_reduce_small(_pack(parts)), shapes)
    me = _flat(*_coords())
    g_conv_w = lax.dynamic_slice_in_dim(summed[-1], me * (D_B // N_DEV), D_B // N_DEV, axis=2)
    g_small = summed[:-1] + [g_conv_w]
    keys = SMALL + ["conv_w"]
    rep["conv_w"] = (conv_w, m_conv_w, v_conv_w)
    shapes = [a.shape for a in g_small]
    packed = [_pack(g_small)] + [_pack([rep[k][i] for k in keys]) for i in range(3)]
    d_s, nm_s, nv_s = (_unpack(a, shapes) for a in _adam_small(*packed))
    for i, k in enumerate(keys):
        res[k] = (g_small[i], d_s[i], nm_s[i], nv_s[i])

    order = ["norm1_g", "w_in", "sgu_ln_g", "sgu_ln_b", "sgu_w", "sgu_b", "conv_w", "conv_b", "conv_ln_g",
             "conv_ln_b", "w_out", "norm2_g", "w_ff1", "w_ff2", "final_g"]
    return (loss, grad_x, *[res[k][0] for k in order], *[res[k][1] for k in order],
            *[res[k][2] for k in order], *[res[k][3] for k in order])
```

```python
import functools

import jax
import jax.numpy as jnp
from jax import lax
from jax.experimental import pallas as pl
from jax.experimental.pallas import tpu as pltpu

F32 = jnp.float32
BF16 = jnp.bfloat16

N_DEV = 8
DEPTH = 2
T = 4096
D = 1024
D_A = 512
D_B = 512
CHUNK = 128
H_A = 4
H_B = 4
CONV_W = 31
HALO = 32
D_FF = 4096
IN_COLS = 2048
EPS = 1e-6

ADAM_LR = 0.001
ADAM_B1 = 0.9
ADAM_B2 = 0.999
ADAM_EPS = 1e-08
ADAM_WD = 0.01
ADAM_STEP = 10

TM = 512
TM_MIX = 256
RB = 64
LANES = 128
MIB = 1024 * 1024

SQRT_HALF = 0.7071067811865476
INV_SQRT_2PI = 0.3989422804014327

MESH_ID = pl.DeviceIdType.MESH
ANY = pl.BlockSpec(memory_space=pl.ANY)
HBM = pl.BlockSpec(memory_space=pltpu.HBM)
SEM = pl.BlockSpec(memory_space=pltpu.SEMAPHORE)
EFFECT = pltpu.SideEffectType.DATAFLOW_SIDE_EFFECTING


def _cparams(vmem_mib, sem=("arbitrary",)):
    return pltpu.CompilerParams(dimension_semantics=sem, vmem_limit_bytes=vmem_mib * MIB)


def _full(shape):
    return pl.BlockSpec(shape, lambda *_: (0,) * len(shape))


def _rows(tm, cols):
    return pl.BlockSpec((tm, cols), lambda i: (i, 0))


def _gelu(x):
    cdf = 0.5 * (1.0 + lax.erf(x * SQRT_HALF))
    return x * cdf, cdf


def _gelu_grad(x, cdf):
    return cdf + x * (INV_SQRT_2PI * jnp.exp(-0.5 * x * x))


def _sigmoid(x):
    return 1.0 / (1.0 + jnp.exp(-x))


def _ln(x):
    mu = jnp.mean(x, axis=-1, keepdims=True)
    xc = x - mu
    rstd = lax.rsqrt(jnp.mean(xc * xc, axis=-1, keepdims=True) + EPS)
    return xc * rstd, rstd


def _ln_bwd(dyh, xhat, rstd):
    return rstd * (dyh - jnp.mean(dyh, axis=-1, keepdims=True) - xhat * jnp.mean(dyh * xhat, axis=-1, keepdims=True))


def _rms(x):
    return lax.rsqrt(jnp.mean(x * x, axis=-1, keepdims=True) + EPS)


def _rms_bwd(dh, x, r, g):
    n = x * r
    dn = dh * g
    dx = r * (dn - n * jnp.mean(dn * n, axis=-1, keepdims=True))
    return dx, jnp.sum(dh * n, axis=0, keepdims=True)


def _dot(a, b):
    return jnp.dot(a, b, preferred_element_type=F32)


def _dot_nt(a, b):
    return lax.dot_general(a, b, (((1,), (1,)), ((), ())), preferred_element_type=F32)


def _dot_tn(a, b):
    return lax.dot_general(a, b, (((0,), (0,)), ((), ())), preferred_element_type=F32)


def _tril_mask():
    r = lax.broadcasted_iota(jnp.int32, (CHUNK, CHUNK), 0)
    c = lax.broadcasted_iota(jnp.int32, (CHUNK, CHUNK), 1)
    return r >= c


def _fwd_in(x, g1, wg):
    bn = wg.shape[2]

    def body(x_ref, g_ref, w_ref, h_ref, p_ref):
        xv = x_ref[...]
        h = (xv * _rms(xv) * g_ref[...]).astype(BF16)
        h_ref[...] = h
        for j in range(N_DEV):
            p_ref[:, j * bn:(j + 1) * bn] = _dot(h, w_ref[j])

    return pl.pallas_call(
        body, name="fwd_in", grid=(T // TM,),
        in_specs=[_rows(TM, D), _full((1, D)), _full(wg.shape)],
        out_specs=[_rows(TM, D), _rows(TM, IN_COLS)],
        out_shape=[jax.ShapeDtypeStruct((T, D), BF16), jax.ShapeDtypeStruct((T, IN_COLS), F32)],
        compiler_params=_cparams(32),
    )(x, g1, wg)


def _conv_taps(src_ref, w_ref, cols, base, first_tap_row, step):
    acc = jnp.zeros((RB, LANES), F32)
    for k in range(CONV_W):
        r0 = base + first_tap_row + step * k
        acc = acc + src_ref[r0:r0 + RB, cols] * w_ref[k:k + 1, cols]
    return acc


def _mixer_fwd(proj, lg, lb, wm, bst, cw, cb, clg, clb):
    tm = TM_MIX
    hb = tm // HALO

    def body(p_ref, ph_ref, lg_ref, lb_ref, wm_ref, bs_ref, cw_ref, cb_ref, clg_ref, clb_ref,
             mix_ref, c_ref, g_ref, gbuf):
        i = pl.program_id(0)
        u, _ = _gelu(p_ref[:, 0:D_A])
        vg, _ = _gelu(p_ref[:, D_A:2 * D_A])
        xhat, _ = _ln(vg)
        v = (xhat * lg_ref[...] + lb_ref[...]).astype(BF16)
        mask = _tril_mask()
        for h in range(H_A):
            hc = slice(h * CHUNK, (h + 1) * CHUNK)
            wmh = jnp.where(mask, wm_ref[h], 0.0).astype(BF16)
            for c in range(tm // CHUNK):
                rc = slice(c * CHUNK, (c + 1) * CHUNK)
                mixed = _dot(wmh, v[rc, hc]) + bs_ref[:, h:h + 1]
                mix_ref[rc, hc] = (u[rc, hc] * mixed).astype(BF16)

        g = p_ref[:, 2 * D_A:2 * D_A + D_B] * _sigmoid(p_ref[:, 2 * D_A + D_B:IN_COLS])
        g_ref[...] = g
        gh = ph_ref[:, 0:D_B] * _sigmoid(ph_ref[:, D_B:2 * D_B])
        gbuf[0:HALO, :] = jnp.where(i > 0, gh, 0.0)
        gbuf[HALO:HALO + tm, :] = g
        for q in range(H_B):
            cols = slice(q * LANES, (q + 1) * LANES)
            for rb in range(tm // RB):
                acc = _conv_taps(gbuf, cw_ref, cols, rb * RB, HALO - (CONV_W - 1), 1)
                c_ref[rb * RB:(rb + 1) * RB, cols] = acc + cb_ref[:, cols]
        for q in range(H_B):
            cols = slice(q * LANES, (q + 1) * LANES)
            chat, _ = _ln(c_ref[:, cols])
            z = chat * clg_ref[:, cols] + clb_ref[:, cols]
            mix_ref[:, D_A + q * LANES:D_A + (q + 1) * LANES] = (z * _sigmoid(z)).astype(BF16)

    vec = _full((1, D_A))
    return pl.pallas_call(
        body, name="mixer_fwd", grid=(T // tm,),
        in_specs=[_rows(tm, IN_COLS),
                  pl.BlockSpec((HALO, 2 * D_B), lambda i: (jnp.maximum(i * hb - 1, 0), 1)),
                  vec, vec, _full((H_A, CHUNK, CHUNK)), _full((CHUNK, H_A)),
                  _full((CONV_W, D_B)), vec, vec, vec],
        out_specs=[_rows(tm, D), _rows(tm, D_B), _rows(tm, D_B)],
        out_shape=[jax.ShapeDtypeStruct((T, D), BF16), jax.ShapeDtypeStruct((T, D_B), F32),
                   jax.ShapeDtypeStruct((T, D_B), F32)],
        scratch_shapes=[pltpu.VMEM((HALO + tm, D_B), F32)],
        compiler_params=_cparams(32),
    )(proj, proj, lg, lb, wm, bst, cw, cb, clg, clb)


def _fwd_out(mix, wout, x, g2):
    def body(m_ref, w_ref, x_ref, g_ref, x1_ref, h2_ref):
        x1 = x_ref[...] + _dot(m_ref[...], w_ref[...])
        x1_ref[...] = x1
        h2_ref[...] = (x1 * _rms(x1) * g_ref[...]).astype(BF16)

    return pl.pallas_call(
        body, name="fwd_out", grid=(T // TM,),
        in_specs=[_rows(TM, D), _full((D, D)), _rows(TM, D), _full((1, D))],
        out_specs=[_rows(TM, D), _rows(TM, D)],
        out_shape=[jax.ShapeDtypeStruct((T, D), F32), jax.ShapeDtypeStruct((T, D), BF16)],
        compiler_params=_cparams(32),
    )(mix, wout, x, g2)


def _fwd_ff1(h2, wg):
    bn = wg.shape[2]

    def body(h_ref, w_ref, r_ref, f_ref):
        h = h_ref[...]
        for j in range(N_DEV):
            r = jnp.maximum(_dot(h, w_ref[j]), 0.0)
            r_ref[:, j * bn:(j + 1) * bn] = r.astype(BF16)
            f_ref[:, j * bn:(j + 1) * bn] = (r * r).astype(BF16)

    return pl.pallas_call(
        body, name="fwd_ff1", grid=(T // TM,),
        in_specs=[_rows(TM, D), _full(wg.shape)],
        out_specs=[_rows(TM, D_FF), _rows(TM, D_FF)],
        out_shape=[jax.ShapeDtypeStruct((T, D_FF), BF16), jax.ShapeDtypeStruct((T, D_FF), BF16)],
        compiler_params=_cparams(48),
    )(h2, wg)


def _fwd_ff2(f, w2, x1):
    def body(f_ref, w_ref, x_ref, o_ref):
        o_ref[...] = x_ref[...] + _dot(f_ref[...], w_ref[...])

    return pl.pallas_call(
        body, name="fwd_ff2", grid=(T // TM,),
        in_specs=[_rows(TM, D_FF), _full((D_FF, D)), _rows(TM, D)],
        out_specs=_rows(TM, D),
        out_shape=jax.ShapeDtypeStruct((T, D), F32),
        compiler_params=_cparams(48),
    )(f, w2, x1)


def _loss_bwd(xf, gf, tgt):
    def body(x_ref, g_ref, t_ref, dx_ref, dxb_ref, loss_ref, dg_ref):
        i = pl.program_id(0)

        @pl.when(i == 0)
        def _():
            loss_ref[...] = jnp.zeros(loss_ref.shape, F32)
            dg_ref[...] = jnp.zeros(dg_ref.shape, F32)

        xv = x_ref[...]
        r = _rms(xv)
        diff = xv * r * g_ref[...] - t_ref[...]
        loss_ref[...] += 0.5 * jnp.sum(jnp.mean(diff * diff, axis=-1, keepdims=True), axis=0, keepdims=True)
        dx, dg = _rms_bwd(diff * (1.0 / D), xv, r, g_ref[...])
        dx_ref[...] = dx
        dxb_ref[...] = dx.astype(BF16)
        dg_ref[...] += dg

    return pl.pallas_call(
        body, name="loss_bwd", grid=(T // TM,),
        in_specs=[_rows(TM, D), _full((1, D)), _rows(TM, D)],
        out_specs=[_rows(TM, D), _rows(TM, D), _full((8, LANES)), _full((1, D))],
        out_shape=[jax.ShapeDtypeStruct((T, D), F32), jax.ShapeDtypeStruct((T, D), BF16),
                   jax.ShapeDtypeStruct((8, LANES), F32), jax.ShapeDtypeStruct((1, D), F32)],
        compiler_params=_cparams(32),
    )(xf, gf, tgt)


def _bwd_ff2(dxb, w2, r):
    kb = 1024

    def body(d_ref, w_ref, r_ref, o_ref):
        d = d_ref[...]
        for j in range(D_FF // kb):
            cols = slice(j * kb, (j + 1) * kb)
            df = _dot_nt(d, w_ref[cols, :])
            o_ref[:, cols] = (2.0 * r_ref[:, cols].astype(F32) * df).astype(BF16)

    return pl.pallas_call(
        body, name="bwd_ff2", grid=(T // TM,),
        in_specs=[_rows(TM, D), _full((D_FF, D)), _rows(TM, D_FF)],
        out_specs=_rows(TM, D_FF),
        out_shape=jax.ShapeDtypeStruct((T, D_FF), BF16),
        compiler_params=_cparams(48),
    )(dxb, w2, r)


def _bwd_norm_matmul(name, dy, wg, x, g, dres):
    bn = wg.shape[2]
    ncol = dy.shape[1]

    def body(dy_ref, w_ref, x_ref, g_ref, dres_ref, dx_ref, dxb_ref, dg_ref):
        i = pl.program_id(0)

        @pl.when(i == 0)
        def _():
            dg_ref[...] = jnp.zeros(dg_ref.shape, F32)

        dh = jnp.zeros((TM, D), F32)
        for j in range(N_DEV):
            dh = dh + _dot_nt(dy_ref[:, j * bn:(j + 1) * bn], w_ref[j])
        xv = x_ref[...]
        dxn, dg = _rms_bwd(dh, xv, _rms(xv), g_ref[...])
        dx = dres_ref[...] + dxn
        dx_ref[...] = dx
        dxb_ref[...] = dx.astype(BF16)
        dg_ref[...] += dg

    return pl.pallas_call(
        body, name=name, grid=(T // TM,),
        in_specs=[_rows(TM, ncol), _full(wg.shape), _rows(TM, D), _full((1, D)), _rows(TM, D)],
        out_specs=[_rows(TM, D), _rows(TM, D), _full((1, D))],
        out_shape=[jax.ShapeDtypeStruct((T, D), F32), jax.ShapeDtypeStruct((T, D), BF16),
                   jax.ShapeDtypeStruct((1, D), F32)],
        compiler_params=_cparams(48),
    )(dy, wg, x, g, dres)


def _mixer_bwd_a(dxb, wout, proj, c, lg, lb, wm, bst, clg, clb):
    tm = TM_MIX
    n_tiles = T // tm

    def body(dx_ref, wo_ref, p_ref, c_ref, lg_ref, lb_ref, wm_ref, bs_ref, clg_ref, clb_ref,
             dpa_ref, dc_ref, dlg_ref, dlb_ref, dwm_ref, dbs_ref, dcb_ref, dclg_ref, dclb_ref,
             dv_buf, db_acc):
        i = pl.program_id(0)

        @pl.when(i == 0)
        def _():
            for ref in (dlg_ref, dlb_ref, dwm_ref, dbs_ref, dcb_ref, dclg_ref, dclb_ref, db_acc):
                ref[...] = jnp.zeros(ref.shape, F32)

        dmix = _dot_nt(dx_ref[...], wo_ref[...])
        ua = p_ref[:, 0:D_A]
        va = p_ref[:, D_A:2 * D_A]
        u, cdf_u = _gelu(ua)
        vg, cdf_v = _gelu(va)
        xhat, rstd = _ln(vg)
        v = (xhat * lg_ref[...] + lb_ref[...]).astype(BF16)
        mask = _tril_mask()
        for h in range(H_A):
            hc = slice(h * CHUNK, (h + 1) * CHUNK)
            wmh = jnp.where(mask, wm_ref[h], 0.0).astype(BF16)
            for cidx in range(tm // CHUNK):
                rc = slice(cidx * CHUNK, (cidx + 1) * CHUNK)
                vb = v[rc, hc]
                mixed = _dot(wmh, vb) + bs_ref[:, h:h + 1]
                da = dmix[rc, hc]
                dpa_ref[rc, hc] = (da * mixed * _gelu_grad(ua[rc, hc], cdf_u[rc, hc])).astype(BF16)
                dmixed = da * u[rc, hc]
                dmb = dmixed.astype(BF16)
                dv_buf[rc, hc] = _dot_tn(wmh, dmb)
                dwm_ref[h] += _dot_nt(dmb, vb)
                db_acc[:, hc] += dmixed
        dv = dv_buf[...]
        dlb_ref[...] += jnp.sum(dv, axis=0, keepdims=True)
        dlg_ref[...] += jnp.sum(dv * xhat, axis=0, keepdims=True)
        dvg = _ln_bwd(dv * lg_ref[...], xhat, rstd)
        dpa_ref[:, D_A:2 * D_A] = (dvg * _gelu_grad(va, cdf_v)).astype(BF16)

        for q in range(H_B):
            cols = slice(q * LANES, (q + 1) * LANES)
            chat, crstd = _ln(c_ref[:, cols])
            z = chat * clg_ref[:, cols] + clb_ref[:, cols]
            sg = _sigmoid(z)
            dz = dmix[:, D_A + q * LANES:D_A + (q + 1) * LANES] * (sg * (1.0 + z * (1.0 - sg)))
            dclb_ref[:, cols] += jnp.sum(dz, axis=0, keepdims=True)
            dclg_ref[:, cols] += jnp.sum(dz * chat, axis=0, keepdims=True)
            dc = _ln_bwd(dz * clg_ref[:, cols], chat, crstd)
            dc_ref[:, cols] = dc
            dcb_ref[:, cols] += jnp.sum(dc, axis=0, keepdims=True)

        @pl.when(i == n_tiles - 1)
        def _():
            for h in range(H_A):
                hc = slice(h * CHUNK, (h + 1) * CHUNK)
                dwm_ref[h] = jnp.where(mask, dwm_ref[h], 0.0)
                dbs_ref[h:h + 1, :] = jnp.sum(db_acc[:, hc].T, axis=0, keepdims=True)

    vec = _full((1, D_A))
    vshape = jax.ShapeDtypeStruct((1, D_A), F32)
    return pl.pallas_call(
        body, name="mixer_bwd_a", grid=(n_tiles,),
        in_specs=[_rows(tm, D), _full((D, D)), _rows(tm, IN_COLS), _rows(tm, D_B), vec, vec,
                  _full((H_A, CHUNK, CHUNK)), _full((CHUNK, H_A)), vec, vec],
        out_specs=[_rows(tm, 2 * D_A), _rows(tm, D_B), vec, vec, _full((H_A, CHUNK, CHUNK)),
                   _full((H_A, CHUNK)), vec, vec, vec],
        out_shape=[jax.ShapeDtypeStruct((T, 2 * D_A), BF16), jax.ShapeDtypeStruct((T, D_B), F32), vshape, vshape,
                   jax.ShapeDtypeStruct((H_A, CHUNK, CHUNK), F32), jax.ShapeDtypeStruct((H_A, CHUNK), F32),
                   vshape, vshape, vshape],
        scratch_shapes=[pltpu.VMEM((tm, D_A), F32), pltpu.VMEM((CHUNK, D_A), F32)],
        compiler_params=_cparams(32),
    )(dxb, wout, proj, c, lg, lb, wm, bst, clg, clb)


def _mixer_bwd_b(dc, g, proj, cw, dpa):
    tm = TM_MIX
    n_tiles = T // tm
    hb = tm // HALO

    def body(dc_ref, dch_ref, g_ref, gh_ref, p_ref, cw_ref, dpa_ref, dp_ref, dcw_ref, dcbuf, gbuf, dwacc):
        i = pl.program_id(0)

        @pl.when(i == 0)
        def _():
            dwacc[...] = jnp.zeros(dwacc.shape, F32)

        dcbuf[0:tm, :] = dc_ref[...]
        dcbuf[tm:tm + HALO, :] = jnp.where(i < n_tiles - 1, dch_ref[...], 0.0)
        gbuf[0:HALO, :] = jnp.where(i > 0, gh_ref[...], 0.0)
        gbuf[HALO:HALO + tm, :] = g_ref[...]
        dp_ref[:, 0:2 * D_A] = dpa_ref[...]
        for q in range(H_B):
            cols = slice(q * LANES, (q + 1) * LANES)
            for rb in range(tm // RB):
                rows = slice(rb * RB, (rb + 1) * RB)
                dg = _conv_taps(dcbuf, cw_ref, cols, rb * RB, CONV_W - 1, -1)
                val = p_ref[rows, cols]
                sg = _sigmoid(p_ref[rows, D_B + q * LANES:D_B + (q + 1) * LANES])
                dp_ref[rows, 2 * D_A + q * LANES:2 * D_A + (q + 1) * LANES] = (dg * sg).astype(BF16)
                dp_ref[rows, 2 * D_A + D_B + q * LANES:2 * D_A + D_B + (q + 1) * LANES] = (
                    dg * val * sg * (1.0 - sg)).astype(BF16)
                dcv = dc_ref[rows, cols]
                for k in range(CONV_W):
                    r0 = rb * RB + HALO - (CONV_W - 1) + k
                    prod = dcv * gbuf[r0:r0 + RB, cols]
                    dwacc[k * 8:(k + 1) * 8, cols] += jnp.sum(prod.reshape(RB // 8, 8, LANES), axis=0)

        @pl.when(i == n_tiles - 1)
        def _():
            for k in range(CONV_W):
                dcw_ref[k:k + 1, :] = jnp.sum(dwacc[k * 8:(k + 1) * 8, :], axis=0, keepdims=True)

    return pl.pallas_call(
        body, name="mixer_bwd_b", grid=(n_tiles,),
        in_specs=[_rows(tm, D_B),
                  pl.BlockSpec((HALO, D_B), lambda i: (jnp.minimum((i + 1) * hb, T // HALO - 1), 0)),
                  _rows(tm, D_B),
                  pl.BlockSpec((HALO, D_B), lambda i: (jnp.maximum(i * hb - 1, 0), 0)),
                  pl.BlockSpec((tm, 2 * D_B), lambda i: (i, 1)),
                  _full((CONV_W, D_B)), _rows(tm, 2 * D_A)],
        out_specs=[_rows(tm, IN_COLS), _full((CONV_W, D_B))],
        out_shape=[jax.ShapeDtypeStruct((T, IN_COLS), BF16), jax.ShapeDtypeStruct((CONV_W, D_B), F32)],
        scratch_shapes=[pltpu.VMEM((tm + HALO, D_B), F32), pltpu.VMEM((HALO + tm, D_B), F32),
                        pltpu.VMEM((CONV_W * 8, D_B), F32)],
        compiler_params=_cparams(32),
    )(dc, dc, g, g, proj, cw, dpa)


def _wgrad(name, a, g, bn):
    k = a.shape[1]
    n = g.shape[1]
    tk = min(k, 1024)
    tn = min(n, max(bn, 1024))
    nsub = tn // bn
    tt = TM
    nt = T // tt

    def body(a_ref, g_ref, o_ref, ob_ref):
        t = pl.program_id(2)

        @pl.when(t == 0)
        def _():
            o_ref[...] = jnp.zeros(o_ref.shape, F32)

        av = a_ref[...]
        for s in range(nsub):
            o_ref[s] += _dot_tn(av, g_ref[:, s * bn:(s + 1) * bn])

        @pl.when(t == nt - 1)
        def _():
            ob_ref[...] = o_ref[...].astype(BF16)

    ospec = pl.BlockSpec((nsub, tk, bn), lambda ki, ni, ti: (ni, ki, 0))
    return pl.pallas_call(
        body, name=name, grid=(k // tk, n // tn, nt),
        in_specs=[pl.BlockSpec((tt, tk), lambda ki, ni, ti: (ti, ki)),
                  pl.BlockSpec((tt, tn), lambda ki, ni, ti: (ti, ni))],
        out_specs=[ospec, ospec],
        out_shape=[jax.ShapeDtypeStruct((n // bn, k, bn), F32), jax.ShapeDtypeStruct((n // bn, k, bn), BF16)],
        compiler_params=_cparams(40, ("arbitrary", "arbitrary", "arbitrary")),
    )(a, g)


def _coords():
    return lax.axis_index("x"), lax.axis_index("y"), lax.axis_index("c")


def _flat(x, y, c):
    return 4 * x + 2 * y + c


def _peer(k):
    x, y, c = _coords()
    return (x ^ ((k >> 2) & 1), y ^ ((k >> 1) & 1), c ^ (k & 1))


def _hbm(a):
    return pltpu.with_memory_space_constraint(a, pltpu.HBM)


def _hbm_like(a):
    return pltpu.HBM(a.shape, a.dtype)


def _peer_sems():
    return pltpu.SemaphoreType.DMA((N_DEV - 1,))


def _place_own(shards):
    n = len(shards)

    def body(*refs):
        ins, outs, sems = refs[:n], refs[n:2 * n], refs[2 * n]
        me = _flat(*_coords())
        copies = [pltpu.make_async_copy(ins[a], outs[a].at[me], sems.at[a]) for a in range(n)]
        for cp in copies:
            cp.start()
        for cp in copies:
            cp.wait()

    return pl.pallas_call(
        body, name="place_own", in_specs=[ANY] * n, out_specs=[ANY] * n,
        out_shape=[jax.ShapeDtypeStruct((N_DEV,) + s.shape, s.dtype) for s in shards],
        scratch_shapes=[pltpu.SemaphoreType.DMA((n,))],
    )(*shards)


def _gather_start(shards, lands):
    n = len(shards)

    def body(*refs):
        ins, lnd = refs[:n], refs[n:2 * n]
        send, recv = refs[2 * n:3 * n], refs[3 * n:4 * n]
        me = _flat(*_coords())
        for a in range(n):
            for k in range(1, N_DEV):
                pltpu.make_async_remote_copy(
                    src_ref=ins[a], dst_ref=lnd[a].at[me], send_sem=send[a].at[k - 1], recv_sem=recv[a].at[k - 1],
                    device_id=_peer(k), device_id_type=MESH_ID).start()

    outs = pl.pallas_call(
        body, name="gather_start",
        out_shape=(*[_peer_sems() for _ in range(2 * n)], *[_hbm_like(s) for s in shards],
                   *[_hbm_like(l) for l in lands]),
        in_specs=[HBM] * (2 * n), out_specs=(*[SEM] * (2 * n), *[HBM] * (2 * n)),
        input_output_aliases={i: 2 * n + i for i in range(2 * n)},
        compiler_params=pltpu.CompilerParams(has_side_effects=EFFECT),
    )(*[_hbm(s) for s in shards], *[_hbm(l) for l in lands])
    return outs[:n], outs[n:2 * n], outs[2 * n:3 * n], outs[3 * n:]


def _gather_wait(name, shard, land, send, recv, after):
    def body(s_ref, l_ref, send_ref, recv_ref, after_ref, s_out, l_out):
        for k in range(1, N_DEV):
            cp = pltpu.make_async_remote_copy(
                src_ref=s_ref, dst_ref=l_ref.at[0], send_sem=send_ref.at[k - 1], recv_sem=recv_ref.at[k - 1],
                device_id=_peer(k), device_id_type=MESH_ID)
            cp.wait_send()
            cp.wait_recv()

    return pl.pallas_call(
        body, name=name, out_shape=(_hbm_like(shard), _hbm_like(land)),
        in_specs=(HBM, HBM, SEM, SEM, ANY), out_specs=(HBM, HBM), input_output_aliases={0: 0, 1: 1},
        compiler_params=pltpu.CompilerParams(has_side_effects=EFFECT),
    )(shard, land, send, recv, after)[1]


def _scatter_start(name, g16, land, layer):
    def body(g_ref, land_ref, send, recv, g_out, land_out):
        for k in range(1, N_DEV):
            to = _peer(k)
            pltpu.make_async_remote_copy(
                src_ref=g_ref.at[_flat(*to)], dst_ref=land_ref.at[layer, k - 1],
                send_sem=send.at[k - 1], recv_sem=recv.at[k - 1], device_id=to, device_id_type=MESH_ID).start()

    return pl.pallas_call(
        body, name=name,
        out_shape=(_peer_sems(), _peer_sems(), _hbm_like(g16), _hbm_like(land)),
        in_specs=[HBM, HBM], out_specs=(SEM, SEM, HBM, HBM), input_output_aliases={0: 2, 1: 3},
        compiler_params=pltpu.CompilerParams(has_side_effects=EFFECT),
    )(_hbm(g16), _hbm(land))


def _scatter_wait(grads, lands, sends, recvs, after):
    n, nw = len(grads), len(lands)

    def body(*refs):
        g, lnd = refs[:n], refs[n:n + nw]
        send, recv = refs[n + nw:2 * n + nw], refs[2 * n + nw:3 * n + nw]
        for a in range(n):
            for k in range(1, N_DEV):
                cp = pltpu.make_async_remote_copy(
                    src_ref=g[a].at[0], dst_ref=lnd[a // DEPTH].at[a % DEPTH, 0],
                    send_sem=send[a].at[k - 1], recv_sem=recv[a].at[k - 1],
                    device_id=_peer(k), device_id_type=MESH_ID)
                cp.wait_send()
                cp.wait_recv()

    outs = pl.pallas_call(
        body, name="scatter_wait",
        out_shape=(*[_hbm_like(g) for g in grads], *[_hbm_like(l) for l in lands]),
        in_specs=(*[HBM] * (n + nw), *[SEM] * (2 * n), ANY), out_specs=[HBM] * (n + nw),
        input_output_aliases={i: i for i in range(n + nw)},
        compiler_params=pltpu.CompilerParams(has_side_effects=EFFECT),
    )(*grads, *lands, *sends, *recvs, after)
    return outs[n:]


def _own_blocks(grads_f32):
    nw = len(grads_f32)
    flat32 = [g for w in grads_f32 for g in w]
    n = len(flat32)

    def body(*refs):
        g32, own, sems = refs[:n], refs[n:n + nw], refs[n + nw]
        me = _flat(*_coords())
        copies = [pltpu.make_async_copy(g32[a].at[me], own[a // DEPTH].at[a % DEPTH], sems.at[a]) for a in range(n)]
        for cp in copies:
            cp.start()
        for cp in copies:
            cp.wait()

    return pl.pallas_call(
        body, name="own_blocks", in_specs=[ANY] * n, out_specs=[ANY] * nw,
        out_shape=[jax.ShapeDtypeStruct((DEPTH,) + w[0].shape[1:], F32) for w in grads_f32],
        scratch_shapes=[pltpu.SemaphoreType.DMA((n,))],
    )(*flat32)


def _all_reduce_small(part):
    rows = part.shape[0]

    def body(p_ref, o_ref, slots, send_sems, recv_sems):
        me = _flat(*_coords())
        slots[me] = p_ref[...]
        copies = []
        for k in range(1, N_DEV):
            copies.append(pltpu.make_async_remote_copy(
                src_ref=p_ref, dst_ref=slots.at[me],
                send_sem=send_sems.at[k - 1], recv_sem=recv_sems.at[k - 1],
                device_id=_peer(k), device_id_type=MESH_ID))
        for cp in copies:
            cp.start()
        for cp in copies:
            cp.wait()
        acc = slots[0]
        for d in range(1, N_DEV):
            acc = acc + slots[d]
        o_ref[...] = acc

    return pl.pallas_call(
        body, name="all_reduce_small",
        in_specs=[pl.BlockSpec(memory_space=pltpu.VMEM)], out_specs=pl.BlockSpec(memory_space=pltpu.VMEM),
        out_shape=jax.ShapeDtypeStruct(part.shape, F32),
        scratch_shapes=[pltpu.VMEM((N_DEV, rows, LANES), F32), pltpu.SemaphoreType.DMA((N_DEV - 1,)),
                        pltpu.SemaphoreType.DMA((N_DEV - 1,))],
        compiler_params=pltpu.CompilerParams(vmem_limit_bytes=24 * MIB),
    )(part)


def _adam_math(w, g, m, v):
    m = ADAM_B1 * m + (1.0 - ADAM_B1) * g
    v = ADAM_B2 * v + (1.0 - ADAM_B2) * (g * g)
    m_hat = m / (1.0 - ADAM_B1 ** ADAM_STEP)
    v_hat = v / (1.0 - ADAM_B2 ** ADAM_STEP)
    delta = -ADAM_LR * (m_hat / (jnp.sqrt(v_hat) + ADAM_EPS) + ADAM_WD * w)
    return delta, m, v


def _adam_sharded(name, own, recv, w, m, v):
    _, r, c = w.shape
    tr = min(r, 256)

    def body(own_ref, recv_ref, w_ref, m_ref, v_ref, g_ref, d_ref, nm_ref, nv_ref):
        g = own_ref[...]
        for k in range(N_DEV - 1):
            g = g + recv_ref[k].astype(F32)
        delta, nm, nv = _adam_math(w_ref[...], g, m_ref[...], v_ref[...])
        g_ref[...] = g
        d_ref[...] = delta
        nm_ref[...] = nm
        nv_ref[...] = nv

    blk = pl.BlockSpec((None, tr, c), lambda l, i: (l, i, 0))
    shp = jax.ShapeDtypeStruct(w.shape, F32)
    return pl.pallas_call(
        body, name=name, grid=(DEPTH, r // tr),
        in_specs=[blk, pl.BlockSpec((None, N_DEV - 1, tr, c), lambda l, i: (l, 0, i, 0)), blk, blk, blk],
        out_specs=[blk] * 4, out_shape=[shp] * 4,
        compiler_params=_cparams(32, ("arbitrary", "arbitrary")),
    )(own, recv, w, m, v)


def _adam_small(g, w, m, v):
    rows = g.shape[0]

    def body(g_ref, w_ref, m_ref, v_ref, d_ref, nm_ref, nv_ref):
        delta, nm, nv = _adam_math(w_ref[...], g_ref[...], m_ref[...], v_ref[...])
        d_ref[...] = delta
        nm_ref[...] = nm
        nv_ref[...] = nv

    spec = _full((rows, LANES))
    shp = jax.ShapeDtypeStruct((rows, LANES), F32)
    return pl.pallas_call(
        body, name="adam_small", grid=(1,),
        in_specs=[spec] * 4, out_specs=[spec] * 3, out_shape=[shp] * 3,
        compiler_params=_cparams(24),
    )(g, w, m, v)


SMALL = ["norm1_g", "sgu_ln_g", "sgu_ln_b", "sgu_w", "sgu_b", "conv_b", "conv_ln_g", "conv_ln_b", "norm2_g",
         "final_g"]


def _pack(arrays):
    flat = jnp.concatenate([a.reshape(-1) for a in arrays])
    pad = (-flat.shape[0]) % (8 * LANES)
    return jnp.pad(flat, (0, pad)).reshape(-1, LANES)


def _unpack(packed, shapes):
    flat = packed.reshape(-1)
    out, off = [], 0
    for s in shapes:
        size = 1
        for d in s:
            size *= d
        out.append(flat[off:off + size].reshape(s))
        off += size
    return out


def kernel(x, norm1_g, w_in, sgu_ln_g, sgu_ln_b, sgu_w, sgu_b, conv_w, conv_b, conv_ln_g, conv_ln_b, w_out, norm2_g, w_ff1, w_ff2, final_g, loss_target, m_norm1_g, m_w_in, m_sgu_ln_g, m_sgu_ln_b, m_sgu_w, m_sgu_b, m_conv_w, m_conv_b, m_conv_ln_g, m_conv_ln_b, m_w_out, m_norm2_g, m_w_ff1, m_w_ff2, m_final_g, v_norm1_g, v_w_in, v_sgu_ln_g, v_sgu_ln_b, v_sgu_w, v_sgu_b, v_conv_w, v_conv_b, v_conv_ln_g, v_conv_ln_b, v_w_out, v_norm2_g, v_w_ff1, v_w_ff2, v_final_g):
    x2d = x.reshape(T, D)
    tgt = loss_target.reshape(T, D)
    cw_shard = conv_w.reshape(CONV_W, LANES)

    gnames, shards = [], []
    for l in range(DEPTH):
        for k, w in (("w_in", w_in), ("w_out", w_out), ("w_ff1", w_ff1), ("w_ff2", w_ff2)):
            gnames.append(f"{k}{l}")
            shards.append(w[l].astype(BF16))
        if l == 0:
            gnames.insert(1, "conv_w")
            shards.insert(1, cw_shard)
    sends, recvs, shards_thru, lands = _gather_start(shards, _place_own(shards))
    gidx = {k: i for i, k in enumerate(gnames)}

    def gathered(k, after):
        i = gidx[k]
        return _gather_wait("gather_wait_" + k, shards_thru[i], lands[i], sends[i], recvs[i], after)

    saved = []
    xl = x2d
    cw_full = None
    for l in range(DEPTH):
        g1 = norm1_g[l].reshape(1, D)
        g2 = norm2_g[l].reshape(1, D)
        lg, lb = sgu_ln_g[l].reshape(1, D_A), sgu_ln_b[l].reshape(1, D_A)
        bst = sgu_b[l].T
        cb = conv_b[l].reshape(1, D_B)
        clg, clb = conv_ln_g[l].reshape(1, D_B), conv_ln_b[l].reshape(1, D_B)
        wg_in = gathered(f"w_in{l}", xl)
        h, proj = _fwd_in(xl, g1, wg_in)
        if l == 0:
            cw_full = gathered("conv_w", proj).reshape(N_DEV, DEPTH, CONV_W, D_B // N_DEV).transpose(
                1, 2, 0, 3).reshape(DEPTH, CONV_W, D_B)
        mix, c, g = _mixer_fwd(proj, lg, lb, sgu_w[l], bst, cw_full[l], cb, clg, clb)
        wout = gathered(f"w_out{l}", mix).reshape(D, D)
        x1, h2 = _fwd_out(mix, wout, xl, g2)
        wg_ff1 = gathered(f"w_ff1{l}", h2)
        r, f = _fwd_ff1(h2, wg_ff1)
        w2 = gathered(f"w_ff2{l}", f).reshape(D_FF, D)
        x2 = _fwd_ff2(f, w2, x1)
        saved.append(dict(x=xl, h=h, proj=proj, mix=mix, c=c, g=g, x1=x1, h2=h2, r=r, f=f, wg_in=wg_in, wout=wout,
                          wg_ff1=wg_ff1, w2=w2, g1=g1, g2=g2, lg=lg, lb=lb, bst=bst, clg=clg, clb=clb))
        xl = x2

    dx, dxb, loss_part, d_final_g = _loss_bwd(xl, final_g.reshape(1, D), tgt)
    loss = lax.psum(loss_part[0, 0], ("x", "y", "c"))

    names = ["w_in", "w_out", "w_ff1", "w_ff2"]
    block = dict(w_in=(D, IN_COLS // N_DEV), w_out=(D // N_DEV, D), w_ff1=(D, D_FF // N_DEV), w_ff2=(D_FF // N_DEV, D))
    land = {k: lax.empty((DEPTH, N_DEV - 1) + block[k], BF16) for k in names}
    big32 = {k: [None] * DEPTH for k in names}
    big16 = {k: [None] * DEPTH for k in names}
    ssend = {k: [None] * DEPTH for k in names}
    srecv = {k: [None] * DEPTH for k in names}

    def send_grad(k, l, g32, g16):
        big32[k][l] = g32.reshape((N_DEV,) + block[k])
        ssend[k][l], srecv[k][l], big16[k][l], land[k] = _scatter_start(
            f"scatter_start_{k}{l}", g16.reshape((N_DEV,) + block[k]), land[k], l)

    small = {}
    d_conv_w = [None] * DEPTH
    for l in reversed(range(DEPTH)):
        s = saved[l]
        send_grad("w_ff2", l, *_wgrad("wgrad_ff2", s["f"], dxb, D))
        df1 = _bwd_ff2(dxb, s["w2"], s["r"])
        send_grad("w_ff1", l, *_wgrad("wgrad_ff1", s["h2"], df1, D_FF // N_DEV))
        dx1, dx1b, dg2 = _bwd_norm_matmul("bwd_ff1", df1, s["wg_ff1"], s["x1"], s["g2"], dx)
        send_grad("w_out", l, *_wgrad("wgrad_out", s["mix"], dx1b, D))
        dpa, dc, dlg, dlb, dwm, dbs, dcb, dclg, dclb = _mixer_bwd_a(
            dx1b, s["wout"], s["proj"], s["c"], s["lg"], s["lb"], sgu_w[l], s["bst"], s["clg"], s["clb"])
        dproj, d_conv_w[l] = _mixer_bwd_b(dc, s["g"], s["proj"], cw_full[l], dpa)
        send_grad("w_in", l, *_wgrad("wgrad_in", s["h"], dproj, IN_COLS // N_DEV))
        dx, dxb, dg1 = _bwd_norm_matmul("bwd_in", dproj, s["wg_in"], s["x"], s["g1"], dx1)
        small[l] = dict(norm1_g=dg1, sgu_ln_g=dlg, sgu_ln_b=dlb, sgu_w=dwm, sgu_b=dbs, conv_b=dcb, conv_ln_g=dclg,
                        conv_ln_b=dclb, norm2_g=dg2)
    grad_x = dx.reshape(1, T, D)

    own = _own_blocks([big32[k] for k in names])
    recv = _scatter_wait([g for k in names for g in big16[k]], [land[k] for k in names],
                         [q for k in names for q in ssend[k]], [q for k in names for q in srecv[k]], dx)
    wmv = dict(w_in=(w_in, m_w_in, v_w_in), w_out=(w_out, m_w_out, v_w_out), w_ff1=(w_ff1, m_w_ff1, v_w_ff1),
               w_ff2=(w_ff2, m_w_ff2, v_w_ff2))
    res = {}
    for idx, k in enumerate(names):
        res[k] = _adam_sharded("adam_" + k, own[idx], recv[idx], *wmv[k])

    rep = dict(norm1_g=(norm1_g, m_norm1_g, v_norm1_g), sgu_ln_g=(sgu_ln_g, m_sgu_ln_g, v_sgu_ln_g),
               sgu_ln_b=(sgu_ln_b, m_sgu_ln_b, v_sgu_ln_b), sgu_w=(sgu_w, m_sgu_w, v_sgu_w),
               sgu_b=(sgu_b, m_sgu_b, v_sgu_b), conv_b=(conv_b, m_conv_b, v_conv_b),
               conv_ln_g=(conv_ln_g, m_conv_ln_g, v_conv_ln_g), conv_ln_b=(conv_ln_b, m_conv_ln_b, v_conv_ln_b),
               norm2_g=(norm2_g, m_norm2_g, v_norm2_g), final_g=(final_g, m_final_g, v_final_g))
    parts = []
    for k in SMALL:
        if k == "final_g":
            parts.append(d_final_g.reshape(rep[k][0].shape))
        else:
            parts.append(jnp.stack([small[l][k].reshape(rep[k][0].shape[1:]) for l in range(DEPTH)]))
    parts.append(jnp.stack(d_conv_w))
    shapes = [p.shape for p in parts]
    summed = _unpack(_all_reduce_small(_pack(parts)), shapes)
    me = _flat(*_coords())
    g_conv_w = lax.dynamic_slice_in_dim(summed[-1], me * (D_B // N_DEV), D_B // N_DEV, axis=2)
    g_small = summed[:-1] + [g_conv_w]
    keys = SMALL + ["conv_w"]
    rep["conv_w"] = (conv_w, m_conv_w, v_conv_w)
    shapes = [a.shape for a in g_small]
    packed = [_pack(g_small)] + [_pack([rep[k][i] for k in keys]) for i in range(3)]
    d_s, nm_s, nv_s = (_unpack(a, shapes) for a in _adam_small(*packed))
    for i, k in enumerate(keys):
        res[k] = (g_small[i], d_s[i], nm_s[i], nv_s[i])

    order = ["norm1_g", "w_in", "sgu_ln_g", "sgu_ln_b", "sgu_w", "sgu_b", "conv_w", "conv_b", "conv_ln_g",
             "conv_ln_b", "w_out", "norm2_g", "w_ff1", "w_ff2", "final_g"]
    return (loss, grad_x, *[res[k][0] for k in order], *[res[k][1] for k in order],
            *[res[k][2] for k in order], *[res[k][3] for k in order])
```

```python
import functools

import jax
import jax.numpy as jnp
from jax import lax
from jax.experimental import pallas as pl
from jax.experimental.pallas import tpu as pltpu

F32 = jnp.float32
BF16 = jnp.bfloat16

N_DEV = 8
DEPTH = 2
T = 4096
D = 1024
D_A = 512
D_B = 512
CHUNK = 128
H_A = 4
H_B = 4
CONV_W = 31
HALO = 32
D_FF = 4096
IN_COLS = 2048
EPS = 1e-6

ADAM_LR = 0.001
ADAM_B1 = 0.9
ADAM_B2 = 0.999
ADAM_EPS = 1e-08
ADAM_WD = 0.01
ADAM_STEP = 10

TM = 512
TM_MIX = 256
RB = 64
LANES = 128
MIB = 1024 * 1024

SQRT_HALF = 0.7071067811865476
INV_SQRT_2PI = 0.3989422804014327

MESH_ID = pl.DeviceIdType.MESH
ANY = pl.BlockSpec(memory_space=pl.ANY)
HBM = pl.BlockSpec(memory_space=pltpu.HBM)
SEM = pl.BlockSpec(memory_space=pltpu.SEMAPHORE)
EFFECT = pltpu.SideEffectType.DATAFLOW_SIDE_EFFECTING


def _cparams(vmem_mib, sem=("arbitrary",)):
    return pltpu.CompilerParams(dimension_semantics=sem, vmem_limit_bytes=vmem_mib * MIB)


def _full(shape):
    return pl.BlockSpec(shape, lambda *_: (0,) * len(shape))


def _rows(tm, cols):
    return pl.BlockSpec((tm, cols), lambda i: (i, 0))


def _gelu(x):
    cdf = 0.5 * (1.0 + lax.erf(x * SQRT_HALF))
    return x * cdf, cdf


def _gelu_grad(x, cdf):
    return cdf + x * (INV_SQRT_2PI * jnp.exp(-0.5 * x * x))


def _sigmoid(x):
    return 1.0 / (1.0 + jnp.exp(-x))


def _ln(x):
    mu = jnp.mean(x, axis=-1, keepdims=True)
    xc = x - mu
    rstd = lax.rsqrt(jnp.mean(xc * xc, axis=-1, keepdims=True) + EPS)
    return xc * rstd, rstd


def _ln_bwd(dyh, xhat, rstd):
    return rstd * (dyh - jnp.mean(dyh, axis=-1, keepdims=True) - xhat * jnp.mean(dyh * xhat, axis=-1, keepdims=True))


def _rms(x):
    return lax.rsqrt(jnp.mean(x * x, axis=-1, keepdims=True) + EPS)


def _rms_bwd(dh, x, r, g):
    n = x * r
    dn = dh * g
    dx = r * (dn - n * jnp.mean(dn * n, axis=-1, keepdims=True))
    return dx, jnp.sum(dh * n, axis=0, keepdims=True)


def _dot(a, b):
    return jnp.dot(a, b, preferred_element_type=F32)


def _dot_nt(a, b):
    return lax.dot_general(a, b, (((1,), (1,)), ((), ())), preferred_element_type=F32)


def _dot_tn(a, b):
    return lax.dot_general(a, b, (((0,), (0,)), ((), ())), preferred_element_type=F32)


def _tril_mask():
    r = lax.broadcasted_iota(jnp.int32, (CHUNK, CHUNK), 0)
    c = lax.broadcasted_iota(jnp.int32, (CHUNK, CHUNK), 1)
    return r >= c


def _fwd_in(x, g1, wg):
    bn = wg.shape[2]

    def body(x_ref, g_ref, w_ref, h_ref, p_ref):
        xv = x_ref[...]
        h = (xv * _rms(xv) * g_ref[...]).astype(BF16)
        h_ref[...] = h
        for j in range(N_DEV):
            p_ref[:, j * bn:(j + 1) * bn] = _dot(h, w_ref[j])

    return pl.pallas_call(
        body, name="fwd_in", grid=(T // TM,),
        in_specs=[_rows(TM, D), _full((1, D)), _full(wg.shape)],
        out_specs=[_rows(TM, D), _rows(TM, IN_COLS)],
        out_shape=[jax.ShapeDtypeStruct((T, D), BF16), jax.ShapeDtypeStruct((T, IN_COLS), F32)],
        compiler_params=_cparams(32),
    )(x, g1, wg)


def _conv_taps(src_ref, w_ref, cols, base, first_tap_row, step):
    acc = jnp.zeros((RB, LANES), F32)
    for k in range(CONV_W):
        r0 = base + first_tap_row + step * k
        acc = acc + src_ref[r0:r0 + RB, cols] * w_ref[k:k + 1, cols]
    return acc


def _mixer_fwd(proj, lg, lb, wm, bst, cw, cb, clg, clb):
    tm = TM_MIX
    hb = tm // HALO

    def body(p_ref, ph_ref, lg_ref, lb_ref, wm_ref, bs_ref, cw_ref, cb_ref, clg_ref, clb_ref,
             mix_ref, c_ref, g_ref, gbuf):
        i = pl.program_id(0)
        u, _ = _gelu(p_ref[:, 0:D_A])
        vg, _ = _gelu(p_ref[:, D_A:2 * D_A])
        xhat, _ = _ln(vg)
        v = (xhat * lg_ref[...] + lb_ref[...]).astype(BF16)
        mask = _tril_mask()
        for h in range(H_A):
            hc = slice(h * CHUNK, (h + 1) * CHUNK)
            wmh = jnp.where(mask, wm_ref[h], 0.0).astype(BF16)
            for c in range(tm // CHUNK):
                rc = slice(c * CHUNK, (c + 1) * CHUNK)
                mixed = _dot(wmh, v[rc, hc]) + bs_ref[:, h:h + 1]
                mix_ref[rc, hc] = (u[rc, hc] * mixed).astype(BF16)

        g = p_ref[:, 2 * D_A:2 * D_A + D_B] * _sigmoid(p_ref[:, 2 * D_A + D_B:IN_COLS])
        g_ref[...] = g
        gh = ph_ref[:, 0:D_B] * _sigmoid(ph_ref[:, D_B:2 * D_B])
        gbuf[0:HALO, :] = jnp.where(i > 0, gh, 0.0)
        gbuf[HALO:HALO + tm, :] = g
        for q in range(H_B):
            cols = slice(q * LANES, (q + 1) * LANES)
            for rb in range(tm // RB):
                acc = _conv_taps(gbuf, cw_ref, cols, rb * RB, HALO - (CONV_W - 1), 1)
                c_ref[rb * RB:(rb + 1) * RB, cols] = acc + cb_ref[:, cols]
        for q in range(H_B):
            cols = slice(q * LANES, (q + 1) * LANES)
            chat, _ = _ln(c_ref[:, cols])
            z = chat * clg_ref[:, cols] + clb_ref[:, cols]
            mix_ref[:, D_A + q * LANES:D_A + (q + 1) * LANES] = (z * _sigmoid(z)).astype(BF16)

    vec = _full((1, D_A))
    return pl.pallas_call(
        body, name="mixer_fwd", grid=(T // tm,),
        in_specs=[_rows(tm, IN_COLS),
                  pl.BlockSpec((HALO, 2 * D_B), lambda i: (jnp.maximum(i * hb - 1, 0), 1)),
                  vec, vec, _full((H_A, CHUNK, CHUNK)), _full((CHUNK, H_A)),
                  _full((CONV_W, D_B)), vec, vec, vec],
        out_specs=[_rows(tm, D), _rows(tm, D_B), _rows(tm, D_B)],
        out_shape=[jax.ShapeDtypeStruct((T, D), BF16), jax.ShapeDtypeStruct((T, D_B), F32),
                   jax.ShapeDtypeStruct((T, D_B), F32)],
        scratch_shapes=[pltpu.VMEM((HALO + tm, D_B), F32)],
        compiler_params=_cparams(32),
    )(proj, proj, lg, lb, wm, bst, cw, cb, clg, clb)


def _fwd_out(mix, wout, x, g2):
    def body(m_ref, w_ref, x_ref, g_ref, x1_ref, h2_ref):
        x1 = x_ref[...] + _dot(m_ref[...], w_ref[...])
        x1_ref[...] = x1
        h2_ref[...] = (x1 * _rms(x1) * g_ref[...]).astype(BF16)

    return pl.pallas_call(
        body, name="fwd_out", grid=(T // TM,),
        in_specs=[_rows(TM, D), _full((D, D)), _rows(TM, D), _full((1, D))],
        out_specs=[_rows(TM, D), _rows(TM, D)],
        out_shape=[jax.ShapeDtypeStruct((T, D), F32), jax.ShapeDtypeStruct((T, D), BF16)],
        compiler_params=_cparams(32),
    )(mix, wout, x, g2)


def _fwd_ff1(h2, wg):
    bn = wg.shape[2]

    def body(h_ref, w_ref, r_ref, f_ref):
        h = h_ref[...]
        for j in range(N_DEV):
            r = jnp.maximum(_dot(h, w_ref[j]), 0.0)
            r_ref[:, j * bn:(j + 1) * bn] = r.astype(BF16)
            f_ref[:, j * bn:(j + 1) * bn] = (r * r).astype(BF16)

    return pl.pallas_call(
        body, name="fwd_ff1", grid=(T // TM,),
        in_specs=[_rows(TM, D), _full(wg.shape)],
        out_specs=[_rows(TM, D_FF), _rows(TM, D_FF)],
        out_shape=[jax.ShapeDtypeStruct((T, D_FF), BF16), jax.ShapeDtypeStruct((T, D_FF), BF16)],
        compiler_params=_cparams(48),
    )(h2, wg)


def _fwd_ff2(f, w2, x1):
    def body(f_ref, w_ref, x_ref, o_ref):
        o_ref[...] = x_ref[...] + _dot(f_ref[...], w_ref[...])

    return pl.pallas_call(
        body, name="fwd_ff2", grid=(T // TM,),
        in_specs=[_rows(TM, D_FF), _full((D_FF, D)), _rows(TM, D)],
        out_specs=_rows(TM, D),
        out_shape=jax.ShapeDtypeStruct((T, D), F32),
        compiler_params=_cparams(48),
    )(f, w2, x1)


def _loss_bwd(xf, gf, tgt):
    def body(x_ref, g_ref, t_ref, dx_ref, dxb_ref, loss_ref, dg_ref):
        i = pl.program_id(0)

        @pl.when(i == 0)
        def _():
            loss_ref[...] = jnp.zeros(loss_ref.shape, F32)
            dg_ref[...] = jnp.zeros(dg_ref.shape, F32)

        xv = x_ref[...]
        r = _rms(xv)
        diff = xv * r * g_ref[...] - t_ref[...]
        loss_ref[...] += 0.5 * jnp.sum(jnp.mean(diff * diff, axis=-1, keepdims=True), axis=0, keepdims=True)
        dx, dg = _rms_bwd(diff * (1.0 / D), xv, r, g_ref[...])
        dx_ref[...] = dx
        dxb_ref[...] = dx.astype(BF16)
        dg_ref[...] += dg

    return pl.pallas_call(
        body, name="loss_bwd", grid=(T // TM,),
        in_specs=[_rows(TM, D), _full((1, D)), _rows(TM, D)],
        out_specs=[_rows(TM, D), _rows(TM, D), _full((8, LANES)), _full((1, D))],
        out_shape=[jax.ShapeDtypeStruct((T, D), F32), jax.ShapeDtypeStruct((T, D), BF16),
                   jax.ShapeDtypeStruct((8, LANES), F32), jax.ShapeDtypeStruct((1, D), F32)],
        compiler_params=_cparams(32),
    )(xf, gf, tgt)


def _bwd_ff2(dxb, w2, r, dep):
    kb = 1024

    def body(d_ref, w_ref, r_ref, dep_ref, o_ref):
        d = d_ref[...]
        for j in range(D_FF // kb):
            cols = slice(j * kb, (j + 1) * kb)
            df = _dot_nt(d, w_ref[cols, :])
            o_ref[:, cols] = (2.0 * r_ref[:, cols].astype(F32) * df).astype(BF16)

    return pl.pallas_call(
        body, name="bwd_ff2", grid=(T // TM,),
        in_specs=[_rows(TM, D), _full((D_FF, D)), _rows(TM, D_FF), ANY],
        out_specs=_rows(TM, D_FF),
        out_shape=jax.ShapeDtypeStruct((T, D_FF), BF16),
        compiler_params=_cparams(48),
    )(dxb, w2, r, dep)


def _bwd_norm_matmul(name, dy, wg, x, g, dres, dep):
    bn = wg.shape[2]
    ncol = dy.shape[1]

    def body(dy_ref, w_ref, x_ref, g_ref, dres_ref, dep_ref, dx_ref, dxb_ref, dg_ref):
        i = pl.program_id(0)

        @pl.when(i == 0)
        def _():
            dg_ref[...] = jnp.zeros(dg_ref.shape, F32)

        dh = jnp.zeros((TM, D), F32)
        for j in range(N_DEV):
            dh = dh + _dot_nt(dy_ref[:, j * bn:(j + 1) * bn], w_ref[j])
        xv = x_ref[...]
        dxn, dg = _rms_bwd(dh, xv, _rms(xv), g_ref[...])
        dx = dres_ref[...] + dxn
        dx_ref[...] = dx
        dxb_ref[...] = dx.astype(BF16)
        dg_ref[...] += dg

    return pl.pallas_call(
        body, name=name, grid=(T // TM,),
        in_specs=[_rows(TM, ncol), _full(wg.shape), _rows(TM, D), _full((1, D)), _rows(TM, D), ANY],
        out_specs=[_rows(TM, D), _rows(TM, D), _full((1, D))],
        out_shape=[jax.ShapeDtypeStruct((T, D), F32), jax.ShapeDtypeStruct((T, D), BF16),
                   jax.ShapeDtypeStruct((1, D), F32)],
        compiler_params=_cparams(48),
    )(dy, wg, x, g, dres, dep)


def _mixer_bwd_a(dxb, wout, proj, c, lg, lb, wm, bst, clg, clb, dep):
    tm = TM_MIX
    n_tiles = T // tm

    def body(dx_ref, wo_ref, p_ref, c_ref, lg_ref, lb_ref, wm_ref, bs_ref, clg_ref, clb_ref, dep_ref,
             dpa_ref, dc_ref, dlg_ref, dlb_ref, dwm_ref, dbs_ref, dcb_ref, dclg_ref, dclb_ref,
             dv_buf, db_acc):
        i = pl.program_id(0)

        @pl.when(i == 0)
        def _():
            for ref in (dlg_ref, dlb_ref, dwm_ref, dbs_ref, dcb_ref, dclg_ref, dclb_ref, db_acc):
                ref[...] = jnp.zeros(ref.shape, F32)

        dmix = _dot_nt(dx_ref[...], wo_ref[...])
        ua = p_ref[:, 0:D_A]
        va = p_ref[:, D_A:2 * D_A]
        u, cdf_u = _gelu(ua)
        vg, cdf_v = _gelu(va)
        xhat, rstd = _ln(vg)
        v = (xhat * lg_ref[...] + lb_ref[...]).astype(BF16)
        mask = _tril_mask()
        for h in range(H_A):
            hc = slice(h * CHUNK, (h + 1) * CHUNK)
            wmh = jnp.where(mask, wm_ref[h], 0.0).astype(BF16)
            for cidx in range(tm // CHUNK):
                rc = slice(cidx * CHUNK, (cidx + 1) * CHUNK)
                vb = v[rc, hc]
                mixed = _dot(wmh, vb) + bs_ref[:, h:h + 1]
                da = dmix[rc, hc]
                dpa_ref[rc, hc] = (da * mixed * _gelu_grad(ua[rc, hc], cdf_u[rc, hc])).astype(BF16)
                dmixed = da * u[rc, hc]
                dmb = dmixed.astype(BF16)
                dv_buf[rc, hc] = _dot_tn(wmh, dmb)
                dwm_ref[h] += _dot_nt(dmb, vb)
                db_acc[:, hc] += dmixed
        dv = dv_buf[...]
        dlb_ref[...] += jnp.sum(dv, axis=0, keepdims=True)
        dlg_ref[...] += jnp.sum(dv * xhat, axis=0, keepdims=True)
        dvg = _ln_bwd(dv * lg_ref[...], xhat, rstd)
        dpa_ref[:, D_A:2 * D_A] = (dvg * _gelu_grad(va, cdf_v)).astype(BF16)

        for q in range(H_B):
            cols = slice(q * LANES, (q + 1) * LANES)
            chat, crstd = _ln(c_ref[:, cols])
            z = chat * clg_ref[:, cols] + clb_ref[:, cols]
            sg = _sigmoid(z)
            dz = dmix[:, D_A + q * LANES:D_A + (q + 1) * LANES] * (sg * (1.0 + z * (1.0 - sg)))
            dclb_ref[:, cols] += jnp.sum(dz, axis=0, keepdims=True)
            dclg_ref[:, cols] += jnp.sum(dz * chat, axis=0, keepdims=True)
            dc = _ln_bwd(dz * clg_ref[:, cols], chat, crstd)
            dc_ref[:, cols] = dc
            dcb_ref[:, cols] += jnp.sum(dc, axis=0, keepdims=True)

        @pl.when(i == n_tiles - 1)
        def _():
            for h in range(H_A):
                hc = slice(h * CHUNK, (h + 1) * CHUNK)
                dwm_ref[h] = jnp.where(mask, dwm_ref[h], 0.0)
                dbs_ref[h:h + 1, :] = jnp.sum(db_acc[:, hc].T, axis=0, keepdims=True)

    vec = _full((1, D_A))
    vshape = jax.ShapeDtypeStruct((1, D_A), F32)
    return pl.pallas_call(
        body, name="mixer_bwd_a", grid=(n_tiles,),
        in_specs=[_rows(tm, D), _full((D, D)), _rows(tm, IN_COLS), _rows(tm, D_B), vec, vec,
                  _full((H_A, CHUNK, CHUNK)), _full((CHUNK, H_A)), vec, vec, ANY],
        out_specs=[_rows(tm, 2 * D_A), _rows(tm, D_B), vec, vec, _full((H_A, CHUNK, CHUNK)),
                   _full((H_A, CHUNK)), vec, vec, vec],
        out_shape=[jax.ShapeDtypeStruct((T, 2 * D_A), BF16), jax.ShapeDtypeStruct((T, D_B), F32), vshape, vshape,
                   jax.ShapeDtypeStruct((H_A, CHUNK, CHUNK), F32), jax.ShapeDtypeStruct((H_A, CHUNK), F32),
                   vshape, vshape, vshape],
        scratch_shapes=[pltpu.VMEM((tm, D_A), F32), pltpu.VMEM((CHUNK, D_A), F32)],
        compiler_params=_cparams(32),
    )(dxb, wout, proj, c, lg, lb, wm, bst, clg, clb, dep)


def _mixer_bwd_b(dc, g, proj, cw, dpa):
    tm = TM_MIX
    n_tiles = T // tm
    hb = tm // HALO

    def body(dc_ref, dch_ref, g_ref, gh_ref, p_ref, cw_ref, dpa_ref, dp_ref, dcw_ref, dcbuf, gbuf, dwacc):
        i = pl.program_id(0)

        @pl.when(i == 0)
        def _():
            dwacc[...] = jnp.zeros(dwacc.shape, F32)

        dcbuf[0:tm, :] = dc_ref[...]
        dcbuf[tm:tm + HALO, :] = jnp.where(i < n_tiles - 1, dch_ref[...], 0.0)
        gbuf[0:HALO, :] = jnp.where(i > 0, gh_ref[...], 0.0)
        gbuf[HALO:HALO + tm, :] = g_ref[...]
        dp_ref[:, 0:2 * D_A] = dpa_ref[...]
        for q in range(H_B):
            cols = slice(q * LANES, (q + 1) * LANES)
            for rb in range(tm // RB):
                rows = slice(rb * RB, (rb + 1) * RB)
                dg = _conv_taps(dcbuf, cw_ref, cols, rb * RB, CONV_W - 1, -1)
                val = p_ref[rows, cols]
                sg = _sigmoid(p_ref[rows, D_B + q * LANES:D_B + (q + 1) * LANES])
                dp_ref[rows, 2 * D_A + q * LANES:2 * D_A + (q + 1) * LANES] = (dg * sg).astype(BF16)
                dp_ref[rows, 2 * D_A + D_B + q * LANES:2 * D_A + D_B + (q + 1) * LANES] = (
                    dg * val * sg * (1.0 - sg)).astype(BF16)
                dcv = dc_ref[rows, cols]
                for k in range(CONV_W):
                    r0 = rb * RB + HALO - (CONV_W - 1) + k
                    prod = dcv * gbuf[r0:r0 + RB, cols]
                    dwacc[k * 8:(k + 1) * 8, cols] += jnp.sum(prod.reshape(RB // 8, 8, LANES), axis=0)

        @pl.when(i == n_tiles - 1)
        def _():
            for k in range(CONV_W):
                dcw_ref[k:k + 1, :] = jnp.sum(dwacc[k * 8:(k + 1) * 8, :], axis=0, keepdims=True)

    return pl.pallas_call(
        body, name="mixer_bwd_b", grid=(n_tiles,),
        in_specs=[_rows(tm, D_B),
                  pl.BlockSpec((HALO, D_B), lambda i: (jnp.minimum((i + 1) * hb, T // HALO - 1), 0)),
                  _rows(tm, D_B),
                  pl.BlockSpec((HALO, D_B), lambda i: (jnp.maximum(i * hb - 1, 0), 0)),
                  pl.BlockSpec((tm, 2 * D_B), lambda i: (i, 1)),
                  _full((CONV_W, D_B)), _rows(tm, 2 * D_A)],
        out_specs=[_rows(tm, IN_COLS), _full((CONV_W, D_B))],
        out_shape=[jax.ShapeDtypeStruct((T, IN_COLS), BF16), jax.ShapeDtypeStruct((CONV_W, D_B), F32)],
        scratch_shapes=[pltpu.VMEM((tm + HALO, D_B), F32), pltpu.VMEM((HALO + tm, D_B), F32),
                        pltpu.VMEM((CONV_W * 8, D_B), F32)],
        compiler_params=_cparams(32),
    )(dc, dc, g, g, proj, cw, dpa)


def _wgrad(name, a, g, bn):
    k = a.shape[1]
    n = g.shape[1]
    tk = min(k, 1024)
    tn = min(n, max(bn, 1024))
    nsub = tn // bn
    tt = TM
    nt = T // tt

    def body(a_ref, g_ref, o_ref, ob_ref):
        t = pl.program_id(2)

        @pl.when(t == 0)
        def _():
            o_ref[...] = jnp.zeros(o_ref.shape, F32)

        av = a_ref[...]
        for s in range(nsub):
            o_ref[s] += _dot_tn(av, g_ref[:, s * bn:(s + 1) * bn])

        @pl.when(t == nt - 1)
        def _():
            ob_ref[...] = o_ref[...].astype(BF16)

    ospec = pl.BlockSpec((nsub, tk, bn), lambda ki, ni, ti: (ni, ki, 0))
    return pl.pallas_call(
        body, name=name, grid=(k // tk, n // tn, nt),
        in_specs=[pl.BlockSpec((tt, tk), lambda ki, ni, ti: (ti, ki)),
                  pl.BlockSpec((tt, tn), lambda ki, ni, ti: (ti, ni))],
        out_specs=[ospec, ospec],
        out_shape=[jax.ShapeDtypeStruct((n // bn, k, bn), F32), jax.ShapeDtypeStruct((n // bn, k, bn), BF16)],
        compiler_params=_cparams(40, ("arbitrary", "arbitrary", "arbitrary")),
    )(a, g)


def _coords():
    return lax.axis_index("x"), lax.axis_index("y"), lax.axis_index("c")


def _flat(x, y, c):
    return 4 * x + 2 * y + c


def _peer(k):
    x, y, c = _coords()
    return (x ^ ((k >> 2) & 1), y ^ ((k >> 1) & 1), c ^ (k & 1))


def _hbm(a):
    return pltpu.with_memory_space_constraint(a, pltpu.HBM)


def _hbm_like(a):
    return pltpu.HBM(a.shape, a.dtype)


def _peer_sems():
    return pltpu.SemaphoreType.DMA((N_DEV - 1,))


def _place_own(shards):
    n = len(shards)

    def body(*refs):
        ins, outs = refs[:n], refs[n:2 * n]
        stage_in, stage_out = refs[2 * n:3 * n], refs[3 * n:4 * n]
        in_sems, out_sems = refs[4 * n], refs[4 * n + 1]
        me = _flat(*_coords())
        loads = [pltpu.make_async_copy(ins[a], stage_in[a], in_sems.at[a]) for a in range(n)]
        stores = [pltpu.make_async_copy(stage_out[a], outs[a].at[me], out_sems.at[a]) for a in range(n)]
        for cp in loads:
            cp.start()
        for a in range(n):
            loads[a].wait()
            stage_out[a][...] = stage_in[a][...].astype(stage_out[a].dtype)
            stores[a].start()
        for cp in stores:
            cp.wait()

    return pl.pallas_call(
        body, name="place_own", in_specs=[ANY] * n, out_specs=[ANY] * n,
        out_shape=[jax.ShapeDtypeStruct((N_DEV,) + s.shape, dt) for s, dt in shards],
        scratch_shapes=[*[pltpu.VMEM(s.shape, s.dtype) for s, _ in shards],
                        *[pltpu.VMEM(s.shape, dt) for s, dt in shards],
                        pltpu.SemaphoreType.DMA((n,)), pltpu.SemaphoreType.DMA((n,))],
        compiler_params=pltpu.CompilerParams(vmem_limit_bytes=40 * MIB),
    )(*[s for s, _ in shards])


def _gather_start(lands):
    n = len(lands)

    def body(*refs):
        lnd, send, recv = refs[:n], refs[n:2 * n], refs[2 * n:3 * n]
        me = _flat(*_coords())
        for a in range(n):
            for k in range(1, N_DEV):
                pltpu.make_async_remote_copy(
                    src_ref=lnd[a].at[me], dst_ref=lnd[a].at[me], send_sem=send[a].at[k - 1],
                    recv_sem=recv[a].at[k - 1], device_id=_peer(k), device_id_type=MESH_ID).start()

    outs = pl.pallas_call(
        body, name="gather_start",
        out_shape=(*[_peer_sems() for _ in range(2 * n)], *[_hbm_like(l) for l in lands]),
        in_specs=[HBM] * n, out_specs=(*[SEM] * (2 * n), *[HBM] * n),
        input_output_aliases={i: 2 * n + i for i in range(n)},
        compiler_params=pltpu.CompilerParams(has_side_effects=EFFECT),
    )(*[_hbm(l) for l in lands])
    return outs[:n], outs[n:2 * n], outs[2 * n:]


def _gather_wait(name, land, send, recv, after):
    def body(l_ref, send_ref, recv_ref, after_ref, l_out):
        for k in range(1, N_DEV):
            cp = pltpu.make_async_remote_copy(
                src_ref=l_ref.at[0], dst_ref=l_ref.at[0], send_sem=send_ref.at[k - 1], recv_sem=recv_ref.at[k - 1],
                device_id=_peer(k), device_id_type=MESH_ID)
            cp.wait_send()
            cp.wait_recv()

    return pl.pallas_call(
        body, name=name, out_shape=_hbm_like(land),
        in_specs=(HBM, SEM, SEM, ANY), out_specs=HBM, input_output_aliases={0: 0},
        compiler_params=pltpu.CompilerParams(has_side_effects=EFFECT),
    )(land, send, recv, after)


def _scatter_start(name, g16, land, layer):
    def body(g_ref, land_ref, send, recv, g_out, land_out):
        for k in range(1, N_DEV):
            to = _peer(k)
            pltpu.make_async_remote_copy(
                src_ref=g_ref.at[_flat(*to)], dst_ref=land_ref.at[layer, k - 1],
                send_sem=send.at[k - 1], recv_sem=recv.at[k - 1], device_id=to, device_id_type=MESH_ID).start()

    return pl.pallas_call(
        body, name=name,
        out_shape=(_peer_sems(), _peer_sems(), _hbm_like(g16), _hbm_like(land)),
        in_specs=[HBM, HBM], out_specs=(SEM, SEM, HBM, HBM), input_output_aliases={0: 2, 1: 3},
        compiler_params=pltpu.CompilerParams(has_side_effects=EFFECT),
    )(_hbm(g16), _hbm(land))


def _scatter_wait(grads, lands, sends, recvs, after):
    n, nw = len(grads), len(lands)

    def body(*refs):
        g, lnd = refs[:n], refs[n:n + nw]
        send, recv = refs[n + nw:2 * n + nw], refs[2 * n + nw:3 * n + nw]
        for a in range(n):
            for k in range(1, N_DEV):
                cp = pltpu.make_async_remote_copy(
                    src_ref=g[a].at[0], dst_ref=lnd[a // DEPTH].at[a % DEPTH, 0],
                    send_sem=send[a].at[k - 1], recv_sem=recv[a].at[k - 1],
                    device_id=_peer(k), device_id_type=MESH_ID)
                cp.wait_send()
                cp.wait_recv()

    outs = pl.pallas_call(
        body, name="scatter_wait",
        out_shape=(*[_hbm_like(g) for g in grads], *[_hbm_like(l) for l in lands]),
        in_specs=(*[HBM] * (n + nw), *[SEM] * (2 * n), ANY), out_specs=[HBM] * (n + nw),
        input_output_aliases={i: i for i in range(n + nw)},
        compiler_params=pltpu.CompilerParams(has_side_effects=EFFECT),
    )(*grads, *lands, *sends, *recvs, after)
    return outs[n:]


def _own_blocks(grads_f32):
    nw = len(grads_f32)
    flat32 = [g for w in grads_f32 for g in w]
    n = len(flat32)

    def body(*refs):
        g32, own, stage = refs[:n], refs[n:n + nw], refs[n + nw:2 * n + nw]
        in_sems, out_sems = refs[2 * n + nw], refs[2 * n + nw + 1]
        me = _flat(*_coords())
        loads = [pltpu.make_async_copy(g32[a].at[me], stage[a], in_sems.at[a]) for a in range(n)]
        stores = [pltpu.make_async_copy(stage[a], own[a // DEPTH].at[a % DEPTH], out_sems.at[a]) for a in range(n)]
        for cp in loads:
            cp.start()
        for a in range(n):
            loads[a].wait()
            stores[a].start()
        for cp in stores:
            cp.wait()

    return pl.pallas_call(
        body, name="own_blocks", in_specs=[ANY] * n, out_specs=[ANY] * nw,
        out_shape=[jax.ShapeDtypeStruct((DEPTH,) + w[0].shape[1:], F32) for w in grads_f32],
        scratch_shapes=[*[pltpu.VMEM(g.shape[1:], F32) for g in flat32],
                        pltpu.SemaphoreType.DMA((n,)), pltpu.SemaphoreType.DMA((n,))],
        compiler_params=pltpu.CompilerParams(vmem_limit_bytes=24 * MIB),
    )(*flat32)


def _all_reduce_small(part):
    rows = part.shape[0]

    def body(p_ref, o_ref, slots, send_sems, recv_sems):
        me = _flat(*_coords())
        slots[me] = p_ref[...]
        copies = []
        for k in range(1, N_DEV):
            copies.append(pltpu.make_async_remote_copy(
                src_ref=p_ref, dst_ref=slots.at[me],
                send_sem=send_sems.at[k - 1], recv_sem=recv_sems.at[k - 1],
                device_id=_peer(k), device_id_type=MESH_ID))
        for cp in copies:
            cp.start()
        for cp in copies:
            cp.wait()
        acc = slots[0]
        for d in range(1, N_DEV):
            acc = acc + slots[d]
        o_ref[...] = acc

    return pl.pallas_call(
        body, name="all_reduce_small",
        in_specs=[pl.BlockSpec(memory_space=pltpu.VMEM)], out_specs=pl.BlockSpec(memory_space=pltpu.VMEM),
        out_shape=jax.ShapeDtypeStruct(part.shape, F32),
        scratch_shapes=[pltpu.VMEM((N_DEV, rows, LANES), F32), pltpu.SemaphoreType.DMA((N_DEV - 1,)),
                        pltpu.SemaphoreType.DMA((N_DEV - 1,))],
        compiler_params=pltpu.CompilerParams(vmem_limit_bytes=24 * MIB),
    )(part)


def _adam_math(w, g, m, v):
    m = ADAM_B1 * m + (1.0 - ADAM_B1) * g
    v = ADAM_B2 * v + (1.0 - ADAM_B2) * (g * g)
    m_hat = m / (1.0 - ADAM_B1 ** ADAM_STEP)
    v_hat = v / (1.0 - ADAM_B2 ** ADAM_STEP)
    delta = -ADAM_LR * (m_hat / (jnp.sqrt(v_hat) + ADAM_EPS) + ADAM_WD * w)
    return delta, m, v


def _adam_sharded(name, own, recv, w, m, v):
    _, r, c = w.shape
    tr = min(r, 256)

    def body(own_ref, recv_ref, w_ref, m_ref, v_ref, g_ref, d_ref, nm_ref, nv_ref):
        g = own_ref[...]
        for k in range(N_DEV - 1):
            g = g + recv_ref[k].astype(F32)
        delta, nm, nv = _adam_math(w_ref[...], g, m_ref[...], v_ref[...])
        g_ref[...] = g
        d_ref[...] = delta
        nm_ref[...] = nm
        nv_ref[...] = nv

    blk = pl.BlockSpec((None, tr, c), lambda l, i: (l, i, 0))
    shp = jax.ShapeDtypeStruct(w.shape, F32)
    return pl.pallas_call(
        body, name=name, grid=(DEPTH, r // tr),
        in_specs=[blk, pl.BlockSpec((None, N_DEV - 1, tr, c), lambda l, i: (l, 0, i, 0)), blk, blk, blk],
        out_specs=[blk] * 4, out_shape=[shp] * 4,
        compiler_params=_cparams(32, ("arbitrary", "arbitrary")),
    )(own, recv, w, m, v)


def _adam_small(g, w, m, v):
    rows = g.shape[0]

    def body(g_ref, w_ref, m_ref, v_ref, d_ref, nm_ref, nv_ref):
        delta, nm, nv = _adam_math(w_ref[...], g_ref[...], m_ref[...], v_ref[...])
        d_ref[...] = delta
        nm_ref[...] = nm
        nv_ref[...] = nv

    spec = _full((rows, LANES))
    shp = jax.ShapeDtypeStruct((rows, LANES), F32)
    return pl.pallas_call(
        body, name="adam_small", grid=(1,),
        in_specs=[spec] * 4, out_specs=[spec] * 3, out_shape=[shp] * 3,
        compiler_params=_cparams(24),
    )(g, w, m, v)


SMALL = ["norm1_g", "sgu_ln_g", "sgu_ln_b", "sgu_w", "sgu_b", "conv_b", "conv_ln_g", "conv_ln_b", "norm2_g",
         "final_g"]


def _pack(arrays):
    flat = jnp.concatenate([a.reshape(-1) for a in arrays])
    pad = (-flat.shape[0]) % (8 * LANES)
    return jnp.pad(flat, (0, pad)).reshape(-1, LANES)


def _unpack(packed, shapes):
    flat = packed.reshape(-1)
    out, off = [], 0
    for s in shapes:
        size = 1
        for d in s:
            size *= d
        out.append(flat[off:off + size].reshape(s))
        off += size
    return out


def kernel(x, norm1_g, w_in, sgu_ln_g, sgu_ln_b, sgu_w, sgu_b, conv_w, conv_b, conv_ln_g, conv_ln_b, w_out, norm2_g, w_ff1, w_ff2, final_g, loss_target, m_norm1_g, m_w_in, m_sgu_ln_g, m_sgu_ln_b, m_sgu_w, m_sgu_b, m_conv_w, m_conv_b, m_conv_ln_g, m_conv_ln_b, m_w_out, m_norm2_g, m_w_ff1, m_w_ff2, m_final_g, v_norm1_g, v_w_in, v_sgu_ln_g, v_sgu_ln_b, v_sgu_w, v_sgu_b, v_conv_w, v_conv_b, v_conv_ln_g, v_conv_ln_b, v_w_out, v_norm2_g, v_w_ff1, v_w_ff2, v_final_g):
    x2d = x.reshape(T, D)
    tgt = loss_target.reshape(T, D)
    cw_shard = conv_w.reshape(CONV_W, LANES)

    gnames, shards = [], []
    for l in range(DEPTH):
        for k, w in (("w_in", w_in), ("w_out", w_out), ("w_ff1", w_ff1), ("w_ff2", w_ff2)):
            gnames.append(f"{k}{l}")
            shards.append((w[l], BF16))
        if l == 0:
            gnames.insert(1, "conv_w")
            shards.insert(1, (cw_shard, F32))
    sends, recvs, lands = _gather_start(_place_own(shards))
    gidx = {k: i for i, k in enumerate(gnames)}

    def gathered(k, after):
        i = gidx[k]
        return _gather_wait("gather_wait_" + k, lands[i], sends[i], recvs[i], after)

    saved = []
    xl = x2d
    cw_full = None
    for l in range(DEPTH):
        g1 = norm1_g[l].reshape(1, D)
        g2 = norm2_g[l].reshape(1, D)
        lg, lb = sgu_ln_g[l].reshape(1, D_A), sgu_ln_b[l].reshape(1, D_A)
        bst = sgu_b[l].T
        cb = conv_b[l].reshape(1, D_B)
        clg, clb = conv_ln_g[l].reshape(1, D_B), conv_ln_b[l].reshape(1, D_B)
        wg_in = gathered(f"w_in{l}", xl)
        h, proj = _fwd_in(xl, g1, wg_in)
        if l == 0:
            cw_full = gathered("conv_w", proj).reshape(N_DEV, DEPTH, CONV_W, D_B // N_DEV).transpose(
                1, 2, 0, 3).reshape(DEPTH, CONV_W, D_B)
        mix, c, g = _mixer_fwd(proj, lg, lb, sgu_w[l], bst, cw_full[l], cb, clg, clb)
        wout = gathered(f"w_out{l}", mix).reshape(D, D)
        x1, h2 = _fwd_out(mix, wout, xl, g2)
        wg_ff1 = gathered(f"w_ff1{l}", h2)
        r, f = _fwd_ff1(h2, wg_ff1)
        w2 = gathered(f"w_ff2{l}", f).reshape(D_FF, D)
        x2 = _fwd_ff2(f, w2, x1)
        saved.append(dict(x=xl, h=h, proj=proj, mix=mix, c=c, g=g, x1=x1, h2=h2, r=r, f=f, wg_in=wg_in, wout=wout,
                          wg_ff1=wg_ff1, w2=w2, g1=g1, g2=g2, lg=lg, lb=lb, bst=bst, clg=clg, clb=clb))
        xl = x2

    dx, dxb, loss_part, d_final_g = _loss_bwd(xl, final_g.reshape(1, D), tgt)
    loss = lax.psum(loss_part[0, 0], ("x", "y", "c"))

    names = ["w_in", "w_out", "w_ff1", "w_ff2"]
    block = dict(w_in=(D, IN_COLS // N_DEV), w_out=(D // N_DEV, D), w_ff1=(D, D_FF // N_DEV), w_ff2=(D_FF // N_DEV, D))
    land = {k: lax.empty((DEPTH, N_DEV - 1) + block[k], BF16) for k in names}
    big32 = {k: [None] * DEPTH for k in names}
    big16 = {k: [None] * DEPTH for k in names}
    ssend = {k: [None] * DEPTH for k in names}
    srecv = {k: [None] * DEPTH for k in names}

    def send_grad(k, l, g32, g16):
        big32[k][l] = g32.reshape((N_DEV,) + block[k])
        ssend[k][l], srecv[k][l], big16[k][l], land[k] = _scatter_start(
            f"scatter_start_{k}{l}", g16.reshape((N_DEV,) + block[k]), land[k], l)

    small = {}
    d_conv_w = [None] * DEPTH
    for l in reversed(range(DEPTH)):
        s = saved[l]
        send_grad("w_ff2", l, *_wgrad("wgrad_ff2", s["f"], dxb, D))
        df1 = _bwd_ff2(dxb, s["w2"], s["r"], big16["w_ff2"][l])
        send_grad("w_ff1", l, *_wgrad("wgrad_ff1", s["h2"], df1, D_FF // N_DEV))
        dx1, dx1b, dg2 = _bwd_norm_matmul("bwd_ff1", df1, s["wg_ff1"], s["x1"], s["g2"], dx, big16["w_ff1"][l])
        send_grad("w_out", l, *_wgrad("wgrad_out", s["mix"], dx1b, D))
        dpa, dc, dlg, dlb, dwm, dbs, dcb, dclg, dclb = _mixer_bwd_a(
            dx1b, s["wout"], s["proj"], s["c"], s["lg"], s["lb"], sgu_w[l], s["bst"], s["clg"], s["clb"],
            big16["w_out"][l])
        dproj, d_conv_w[l] = _mixer_bwd_b(dc, s["g"], s["proj"], cw_full[l], dpa)
        send_grad("w_in", l, *_wgrad("wgrad_in", s["h"], dproj, IN_COLS // N_DEV))
        dx, dxb, dg1 = _bwd_norm_matmul("bwd_in", dproj, s["wg_in"], s["x"], s["g1"], dx1, big16["w_in"][l])
        small[l] = dict(norm1_g=dg1, sgu_ln_g=dlg, sgu_ln_b=dlb, sgu_w=dwm, sgu_b=dbs, conv_b=dcb, conv_ln_g=dclg,
                        conv_ln_b=dclb, norm2_g=dg2)
    grad_x = dx.reshape(1, T, D)

    own = _own_blocks([big32[k] for k in names])
    recv = _scatter_wait([g for k in names for g in big16[k]], [land[k] for k in names],
                         [q for k in names for q in ssend[k]], [q for k in names for q in srecv[k]], dx)
    wmv = dict(w_in=(w_in, m_w_in, v_w_in), w_out=(w_out, m_w_out, v_w_out), w_ff1=(w_ff1, m_w_ff1, v_w_ff1),
               w_ff2=(w_ff2, m_w_ff2, v_w_ff2))
    res = {}
    for idx, k in enumerate(names):
        res[k] = _adam_sharded("adam_" + k, own[idx], recv[idx], *wmv[k])

    rep = dict(norm1_g=(norm1_g, m_norm1_g, v_norm1_g), sgu_ln_g=(sgu_ln_g, m_sgu_ln_g, v_sgu_ln_g),
               sgu_ln_b=(sgu_ln_b, m_sgu_ln_b, v_sgu_ln_b), sgu_w=(sgu_w, m_sgu_w, v_sgu_w),
               sgu_b=(sgu_b, m_sgu_b, v_sgu_b), conv_b=(conv_b, m_conv_b, v_conv_b),
               conv_ln_g=(conv_ln_g, m_conv_ln_g, v_conv_ln_g), conv_ln_b=(conv_ln_b, m_conv_ln_b, v_conv_ln_b),
               norm2_g=(norm2_g, m_norm2_g, v_norm2_g), final_g=(final_g, m_final_g, v_final_g))
    parts = []
    for k in SMALL:
        if k == "final_g":
            parts.append(d_final_g.reshape(rep[k][0].shape))
        else:
            parts.append(jnp.stack([small[l][k].reshape(rep[k][0].shape[1:]) for l in range(DEPTH)]))
    parts.append(jnp.stack(d_conv_w))
    shapes = [p.shape for p in parts]
    summed = _unpack(_all_reduce_small(_pack(parts)), shapes)
    me = _flat(*_coords())
    g_conv_w = lax.dynamic_slice_in_dim(summed[-1], me * (D_B // N_DEV), D_B // N_DEV, axis=2)
    g_small = summed[:-1] + [g_conv_w]
    keys = SMALL + ["conv_w"]
    rep["conv_w"] = (conv_w, m_conv_w, v_conv_w)
    shapes = [a.shape for a in g_small]
    packed = [_pack(g_small)] + [_pack([rep[k][i] for k in keys]) for i in range(3)]
    d_s, nm_s, nv_s = (_unpack(a, shapes) for a in _adam_small(*packed))
    for i, k in enumerate(keys):
        res[k] = (g_small[i], d_s[i], nm_s[i], nv_s[i])

    order = ["norm1_g", "w_in", "sgu_ln_g", "sgu_ln_b", "sgu_w", "sgu_b", "conv_w", "conv_b", "conv_ln_g",
             "conv_ln_b", "w_out", "norm2_g", "w_ff1", "w_ff2", "final_g"]
    return (loss, grad_x, *[res[k][0] for k in order], *[res[k][1] for k in order],
            *[res[k][2] for k in order], *[res[k][3] for k in order])
```

```python
import functools

import jax
import jax.numpy as jnp
from jax import lax
from jax.experimental import pallas as pl
from jax.experimental.pallas import tpu as pltpu

F32 = jnp.float32
BF16 = jnp.bfloat16

N_DEV = 8
DEPTH = 2
T = 4096
D = 1024
D_A = 512
D_B = 512
CHUNK = 128
H_A = 4
H_B = 4
CONV_W = 31
HALO = 32
D_FF = 4096
IN_COLS = 2048
EPS = 1e-6

ADAM_LR = 0.001
ADAM_B1 = 0.9
ADAM_B2 = 0.999
ADAM_EPS = 1e-08
ADAM_WD = 0.01
ADAM_STEP = 10

TM = 512
TM_MIX = 256
RB = 64
LANES = 128
MIB = 1024 * 1024
SCOPED_VMEM_MIB = 60

SQRT_HALF = 0.7071067811865476
INV_SQRT_2PI = 0.3989422804014327

MESH_ID = pl.DeviceIdType.MESH
ANY = pl.BlockSpec(memory_space=pl.ANY)
HBM = pl.BlockSpec(memory_space=pltpu.HBM)
SEM = pl.BlockSpec(memory_space=pltpu.SEMAPHORE)
EFFECT = pltpu.SideEffectType.DATAFLOW_SIDE_EFFECTING


def _cparams(vmem_mib, sem=("arbitrary",)):
    assert vmem_mib <= SCOPED_VMEM_MIB
    return pltpu.CompilerParams(dimension_semantics=sem, vmem_limit_bytes=SCOPED_VMEM_MIB * MIB)


def _full(shape):
    return pl.BlockSpec(shape, lambda *_: (0,) * len(shape))


def _rows(tm, cols):
    return pl.BlockSpec((tm, cols), lambda i: (i, 0))


def _gelu(x):
    cdf = 0.5 * (1.0 + lax.erf(x * SQRT_HALF))
    return x * cdf, cdf


def _gelu_grad(x, cdf):
    return cdf + x * (INV_SQRT_2PI * jnp.exp(-0.5 * x * x))


def _sigmoid(x):
    return 1.0 / (1.0 + jnp.exp(-x))


def _ln(x):
    mu = jnp.mean(x, axis=-1, keepdims=True)
    xc = x - mu
    rstd = lax.rsqrt(jnp.mean(xc * xc, axis=-1, keepdims=True) + EPS)
    return xc * rstd, rstd


def _ln_bwd(dyh, xhat, rstd):
    return rstd * (dyh - jnp.mean(dyh, axis=-1, keepdims=True) - xhat * jnp.mean(dyh * xhat, axis=-1, keepdims=True))


def _rms(x):
    return lax.rsqrt(jnp.mean(x * x, axis=-1, keepdims=True) + EPS)


def _rms_bwd(dh, x, r, g):
    n = x * r
    dn = dh * g
    dx = r * (dn - n * jnp.mean(dn * n, axis=-1, keepdims=True))
    return dx, jnp.sum(dh * n, axis=0, keepdims=True)


def _dot(a, b):
    return jnp.dot(a, b, preferred_element_type=F32)


def _dot_nt(a, b):
    return lax.dot_general(a, b, (((1,), (1,)), ((), ())), preferred_element_type=F32)


def _dot_tn(a, b):
    return lax.dot_general(a, b, (((0,), (0,)), ((), ())), preferred_element_type=F32)


def _tril_mask():
    r = lax.broadcasted_iota(jnp.int32, (CHUNK, CHUNK), 0)
    c = lax.broadcasted_iota(jnp.int32, (CHUNK, CHUNK), 1)
    return r >= c


def _fwd_in(x, g1, wg):
    bn = wg.shape[2]

    def body(x_ref, g_ref, w_ref, h_ref, p_ref):
        xv = x_ref[...]
        h = (xv * _rms(xv) * g_ref[...]).astype(BF16)
        h_ref[...] = h
        for j in range(N_DEV):
            p_ref[:, j * bn:(j + 1) * bn] = _dot(h, w_ref[j])

    return pl.pallas_call(
        body, name="fwd_in", grid=(T // TM,),
        in_specs=[_rows(TM, D), _full((1, D)), _full(wg.shape)],
        out_specs=[_rows(TM, D), _rows(TM, IN_COLS)],
        out_shape=[jax.ShapeDtypeStruct((T, D), BF16), jax.ShapeDtypeStruct((T, IN_COLS), F32)],
        compiler_params=_cparams(32),
    )(x, g1, wg)


def _conv_taps(src_ref, w_ref, cols, base, first_tap_row, step):
    acc = jnp.zeros((RB, LANES), F32)
    for k in range(CONV_W):
        r0 = base + first_tap_row + step * k
        acc = acc + src_ref[r0:r0 + RB, cols] * w_ref[k:k + 1, cols]
    return acc


def _mixer_fwd(proj, lg, lb, wm, bst, cw, cb, clg, clb):
    tm = TM_MIX
    hb = tm // HALO

    def body(p_ref, ph_ref, lg_ref, lb_ref, wm_ref, bs_ref, cw_ref, cb_ref, clg_ref, clb_ref,
             mix_ref, c_ref, g_ref, gbuf):
        i = pl.program_id(0)
        u, _ = _gelu(p_ref[:, 0:D_A])
        vg, _ = _gelu(p_ref[:, D_A:2 * D_A])
        xhat, _ = _ln(vg)
        v = (xhat * lg_ref[...] + lb_ref[...]).astype(BF16)
        mask = _tril_mask()
        for h in range(H_A):
            hc = slice(h * CHUNK, (h + 1) * CHUNK)
            wmh = jnp.where(mask, wm_ref[h], 0.0).astype(BF16)
            for c in range(tm // CHUNK):
                rc = slice(c * CHUNK, (c + 1) * CHUNK)
                mixed = _dot(wmh, v[rc, hc]) + bs_ref[:, h:h + 1]
                mix_ref[rc, hc] = (u[rc, hc] * mixed).astype(BF16)

        g = p_ref[:, 2 * D_A:2 * D_A + D_B] * _sigmoid(p_ref[:, 2 * D_A + D_B:IN_COLS])
        g_ref[...] = g
        gh = ph_ref[:, 0:D_B] * _sigmoid(ph_ref[:, D_B:2 * D_B])
        gbuf[0:HALO, :] = jnp.where(i > 0, gh, 0.0)
        gbuf[HALO:HALO + tm, :] = g
        for q in range(H_B):
            cols = slice(q * LANES, (q + 1) * LANES)
            for rb in range(tm // RB):
                acc = _conv_taps(gbuf, cw_ref, cols, rb * RB, HALO - (CONV_W - 1), 1)
                c_ref[rb * RB:(rb + 1) * RB, cols] = acc + cb_ref[:, cols]
        for q in range(H_B):
            cols = slice(q * LANES, (q + 1) * LANES)
            chat, _ = _ln(c_ref[:, cols])
            z = chat * clg_ref[:, cols] + clb_ref[:, cols]
            mix_ref[:, D_A + q * LANES:D_A + (q + 1) * LANES] = (z * _sigmoid(z)).astype(BF16)

    vec = _full((1, D_A))
    return pl.pallas_call(
        body, name="mixer_fwd", grid=(T // tm,),
        in_specs=[_rows(tm, IN_COLS),
                  pl.BlockSpec((HALO, 2 * D_B), lambda i: (jnp.maximum(i * hb - 1, 0), 1)),
                  vec, vec, _full((H_A, CHUNK, CHUNK)), _full((CHUNK, H_A)),
                  _full((CONV_W, D_B)), vec, vec, vec],
        out_specs=[_rows(tm, D), _rows(tm, D_B), _rows(tm, D_B)],
        out_shape=[jax.ShapeDtypeStruct((T, D), BF16), jax.ShapeDtypeStruct((T, D_B), F32),
                   jax.ShapeDtypeStruct((T, D_B), F32)],
        scratch_shapes=[pltpu.VMEM((HALO + tm, D_B), F32)],
        compiler_params=_cparams(32),
    )(proj, proj, lg, lb, wm, bst, cw, cb, clg, clb)


def _fwd_out(mix, wout, x, g2):
    def body(m_ref, w_ref, x_ref, g_ref, x1_ref, h2_ref):
        x1 = x_ref[...] + _dot(m_ref[...], w_ref[...])
        x1_ref[...] = x1
        h2_ref[...] = (x1 * _rms(x1) * g_ref[...]).astype(BF16)

    return pl.pallas_call(
        body, name="fwd_out", grid=(T // TM,),
        in_specs=[_rows(TM, D), _full((D, D)), _rows(TM, D), _full((1, D))],
        out_specs=[_rows(TM, D), _rows(TM, D)],
        out_shape=[jax.ShapeDtypeStruct((T, D), F32), jax.ShapeDtypeStruct((T, D), BF16)],
        compiler_params=_cparams(32),
    )(mix, wout, x, g2)


def _fwd_ff1(h2, wg):
    bn = wg.shape[2]

    def body(h_ref, w_ref, r_ref, f_ref):
        h = h_ref[...]
        for j in range(N_DEV):
            r = jnp.maximum(_dot(h, w_ref[j]), 0.0)
            r_ref[:, j * bn:(j + 1) * bn] = r.astype(BF16)
            f_ref[:, j * bn:(j + 1) * bn] = (r * r).astype(BF16)

    return pl.pallas_call(
        body, name="fwd_ff1", grid=(T // TM,),
        in_specs=[_rows(TM, D), _full(wg.shape)],
        out_specs=[_rows(TM, D_FF), _rows(TM, D_FF)],
        out_shape=[jax.ShapeDtypeStruct((T, D_FF), BF16), jax.ShapeDtypeStruct((T, D_FF), BF16)],
        compiler_params=_cparams(48),
    )(h2, wg)


def _fwd_ff2(f, w2, x1):
    def body(f_ref, w_ref, x_ref, o_ref):
        o_ref[...] = x_ref[...] + _dot(f_ref[...], w_ref[...])

    return pl.pallas_call(
        body, name="fwd_ff2", grid=(T // TM,),
        in_specs=[_rows(TM, D_FF), _full((D_FF, D)), _rows(TM, D)],
        out_specs=_rows(TM, D),
        out_shape=jax.ShapeDtypeStruct((T, D), F32),
        compiler_params=_cparams(48),
    )(f, w2, x1)


def _loss_bwd(xf, gf, tgt):
    def body(x_ref, g_ref, t_ref, dx_ref, dxb_ref, loss_ref, dg_ref):
        i = pl.program_id(0)

        @pl.when(i == 0)
        def _():
            loss_ref[...] = jnp.zeros(loss_ref.shape, F32)
            dg_ref[...] = jnp.zeros(dg_ref.shape, F32)

        xv = x_ref[...]
        r = _rms(xv)
        diff = xv * r * g_ref[...] - t_ref[...]
        loss_ref[...] += 0.5 * jnp.sum(jnp.mean(diff * diff, axis=-1, keepdims=True), axis=0, keepdims=True)
        dx, dg = _rms_bwd(diff * (1.0 / D), xv, r, g_ref[...])
        dx_ref[...] = dx
        dxb_ref[...] = dx.astype(BF16)
        dg_ref[...] += dg

    return pl.pallas_call(
        body, name="loss_bwd", grid=(T // TM,),
        in_specs=[_rows(TM, D), _full((1, D)), _rows(TM, D)],
        out_specs=[_rows(TM, D), _rows(TM, D), _full((8, LANES)), _full((1, D))],
        out_shape=[jax.ShapeDtypeStruct((T, D), F32), jax.ShapeDtypeStruct((T, D), BF16),
                   jax.ShapeDtypeStruct((8, LANES), F32), jax.ShapeDtypeStruct((1, D), F32)],
        compiler_params=_cparams(32),
    )(xf, gf, tgt)


def _bwd_ff2(dxb, w2, r, dep):
    kb = 1024

    def body(d_ref, w_ref, r_ref, dep_ref, o_ref):
        d = d_ref[...]
        for j in range(D_FF // kb):
            cols = slice(j * kb, (j + 1) * kb)
            df = _dot_nt(d, w_ref[cols, :])
            o_ref[:, cols] = (2.0 * r_ref[:, cols].astype(F32) * df).astype(BF16)

    return pl.pallas_call(
        body, name="bwd_ff2", grid=(T // TM,),
        in_specs=[_rows(TM, D), _full((D_FF, D)), _rows(TM, D_FF), ANY],
        out_specs=_rows(TM, D_FF),
        out_shape=jax.ShapeDtypeStruct((T, D_FF), BF16),
        compiler_params=_cparams(48),
    )(dxb, w2, r, dep)


def _bwd_norm_matmul(name, dy, wg, x, g, dres, dep):
    bn = wg.shape[2]
    ncol = dy.shape[1]

    def body(dy_ref, w_ref, x_ref, g_ref, dres_ref, dep_ref, dx_ref, dxb_ref, dg_ref):
        i = pl.program_id(0)

        @pl.when(i == 0)
        def _():
            dg_ref[...] = jnp.zeros(dg_ref.shape, F32)

        dh = jnp.zeros((TM, D), F32)
        for j in range(N_DEV):
            dh = dh + _dot_nt(dy_ref[:, j * bn:(j + 1) * bn], w_ref[j])
        xv = x_ref[...]
        dxn, dg = _rms_bwd(dh, xv, _rms(xv), g_ref[...])
        dx = dres_ref[...] + dxn
        dx_ref[...] = dx
        dxb_ref[...] = dx.astype(BF16)
        dg_ref[...] += dg

    return pl.pallas_call(
        body, name=name, grid=(T // TM,),
        in_specs=[_rows(TM, ncol), _full(wg.shape), _rows(TM, D), _full((1, D)), _rows(TM, D), ANY],
        out_specs=[_rows(TM, D), _rows(TM, D), _full((1, D))],
        out_shape=[jax.ShapeDtypeStruct((T, D), F32), jax.ShapeDtypeStruct((T, D), BF16),
                   jax.ShapeDtypeStruct((1, D), F32)],
        compiler_params=_cparams(48),
    )(dy, wg, x, g, dres, dep)


def _mixer_bwd_a(dxb, wout, proj, c, lg, lb, wm, bst, clg, clb, dep):
    tm = TM_MIX
    n_tiles = T // tm

    def body(dx_ref, wo_ref, p_ref, c_ref, lg_ref, lb_ref, wm_ref, bs_ref, clg_ref, clb_ref, dep_ref,
             dpa_ref, dc_ref, dlg_ref, dlb_ref, dwm_ref, dbs_ref, dcb_ref, dclg_ref, dclb_ref,
             dv_buf, db_acc):
        i = pl.program_id(0)

        @pl.when(i == 0)
        def _():
            for ref in (dlg_ref, dlb_ref, dwm_ref, dbs_ref, dcb_ref, dclg_ref, dclb_ref, db_acc):
                ref[...] = jnp.zeros(ref.shape, F32)

        dmix = _dot_nt(dx_ref[...], wo_ref[...])
        ua = p_ref[:, 0:D_A]
        va = p_ref[:, D_A:2 * D_A]
        u, cdf_u = _gelu(ua)
        vg, cdf_v = _gelu(va)
        xhat, rstd = _ln(vg)
        v = (xhat * lg_ref[...] + lb_ref[...]).astype(BF16)
        mask = _tril_mask()
        for h in range(H_A):
            hc = slice(h * CHUNK, (h + 1) * CHUNK)
            wmh = jnp.where(mask, wm_ref[h], 0.0).astype(BF16)
            for cidx in range(tm // CHUNK):
                rc = slice(cidx * CHUNK, (cidx + 1) * CHUNK)
                vb = v[rc, hc]
                mixed = _dot(wmh, vb) + bs_ref[:, h:h + 1]
                da = dmix[rc, hc]
                dpa_ref[rc, hc] = (da * mixed * _gelu_grad(ua[rc, hc], cdf_u[rc, hc])).astype(BF16)
                dmixed = da * u[rc, hc]
                dmb = dmixed.astype(BF16)
                dv_buf[rc, hc] = _dot_tn(wmh, dmb)
                dwm_ref[h] += _dot_nt(dmb, vb)
                db_acc[:, hc] += dmixed
        dv = dv_buf[...]
        dlb_ref[...] += jnp.sum(dv, axis=0, keepdims=True)
        dlg_ref[...] += jnp.sum(dv * xhat, axis=0, keepdims=True)
        dvg = _ln_bwd(dv * lg_ref[...], xhat, rstd)
        dpa_ref[:, D_A:2 * D_A] = (dvg * _gelu_grad(va, cdf_v)).astype(BF16)

        for q in range(H_B):
            cols = slice(q * LANES, (q + 1) * LANES)
            chat, crstd = _ln(c_ref[:, cols])
            z = chat * clg_ref[:, cols] + clb_ref[:, cols]
            sg = _sigmoid(z)
            dz = dmix[:, D_A + q * LANES:D_A + (q + 1) * LANES] * (sg * (1.0 + z * (1.0 - sg)))
            dclb_ref[:, cols] += jnp.sum(dz, axis=0, keepdims=True)
            dclg_ref[:, cols] += jnp.sum(dz * chat, axis=0, keepdims=True)
            dc = _ln_bwd(dz * clg_ref[:, cols], chat, crstd)
            dc_ref[:, cols] = dc
            dcb_ref[:, cols] += jnp.sum(dc, axis=0, keepdims=True)

        @pl.when(i == n_tiles - 1)
        def _():
            for h in range(H_A):
                hc = slice(h * CHUNK, (h + 1) * CHUNK)
                dwm_ref[h] = jnp.where(mask, dwm_ref[h], 0.0)
                dbs_ref[h:h + 1, :] = jnp.sum(db_acc[:, hc].T, axis=0, keepdims=True)

    vec = _full((1, D_A))
    vshape = jax.ShapeDtypeStruct((1, D_A), F32)
    return pl.pallas_call(
        body, name="mixer_bwd_a", grid=(n_tiles,),
        in_specs=[_rows(tm, D), _full((D, D)), _rows(tm, IN_COLS), _rows(tm, D_B), vec, vec,
                  _full((H_A, CHUNK, CHUNK)), _full((CHUNK, H_A)), vec, vec, ANY],
        out_specs=[_rows(tm, 2 * D_A), _rows(tm, D_B), vec, vec, _full((H_A, CHUNK, CHUNK)),
                   _full((H_A, CHUNK)), vec, vec, vec],
        out_shape=[jax.ShapeDtypeStruct((T, 2 * D_A), BF16), jax.ShapeDtypeStruct((T, D_B), F32), vshape, vshape,
                   jax.ShapeDtypeStruct((H_A, CHUNK, CHUNK), F32), jax.ShapeDtypeStruct((H_A, CHUNK), F32),
                   vshape, vshape, vshape],
        scratch_shapes=[pltpu.VMEM((tm, D_A), F32), pltpu.VMEM((CHUNK, D_A), F32)],
        compiler_params=_cparams(32),
    )(dxb, wout, proj, c, lg, lb, wm, bst, clg, clb, dep)


def _mixer_bwd_b(dc, g, proj, cw, dpa):
    tm = TM_MIX
    n_tiles = T // tm
    hb = tm // HALO

    def body(dc_ref, dch_ref, g_ref, gh_ref, p_ref, cw_ref, dpa_ref, dp_ref, dcw_ref, dcbuf, gbuf, dwacc):
        i = pl.program_id(0)

        @pl.when(i == 0)
        def _():
            dwacc[...] = jnp.zeros(dwacc.shape, F32)

        dcbuf[0:tm, :] = dc_ref[...]
        dcbuf[tm:tm + HALO, :] = jnp.where(i < n_tiles - 1, dch_ref[...], 0.0)
        gbuf[0:HALO, :] = jnp.where(i > 0, gh_ref[...], 0.0)
        gbuf[HALO:HALO + tm, :] = g_ref[...]
        dp_ref[:, 0:2 * D_A] = dpa_ref[...]
        for q in range(H_B):
            cols = slice(q * LANES, (q + 1) * LANES)
            for rb in range(tm // RB):
                rows = slice(rb * RB, (rb + 1) * RB)
                dg = _conv_taps(dcbuf, cw_ref, cols, rb * RB, CONV_W - 1, -1)
                val = p_ref[rows, cols]
                sg = _sigmoid(p_ref[rows, D_B + q * LANES:D_B + (q + 1) * LANES])
                dp_ref[rows, 2 * D_A + q * LANES:2 * D_A + (q + 1) * LANES] = (dg * sg).astype(BF16)
                dp_ref[rows, 2 * D_A + D_B + q * LANES:2 * D_A + D_B + (q + 1) * LANES] = (
                    dg * val * sg * (1.0 - sg)).astype(BF16)
                dcv = dc_ref[rows, cols]
                for k in range(CONV_W):
                    r0 = rb * RB + HALO - (CONV_W - 1) + k
                    prod = dcv * gbuf[r0:r0 + RB, cols]
                    dwacc[k * 8:(k + 1) * 8, cols] += jnp.sum(prod.reshape(RB // 8, 8, LANES), axis=0)

        @pl.when(i == n_tiles - 1)
        def _():
            for k in range(CONV_W):
                dcw_ref[k:k + 1, :] = jnp.sum(dwacc[k * 8:(k + 1) * 8, :], axis=0, keepdims=True)

    return pl.pallas_call(
        body, name="mixer_bwd_b", grid=(n_tiles,),
        in_specs=[_rows(tm, D_B),
                  pl.BlockSpec((HALO, D_B), lambda i: (jnp.minimum((i + 1) * hb, T // HALO - 1), 0)),
                  _rows(tm, D_B),
                  pl.BlockSpec((HALO, D_B), lambda i: (jnp.maximum(i * hb - 1, 0), 0)),
                  pl.BlockSpec((tm, 2 * D_B), lambda i: (i, 1)),
                  _full((CONV_W, D_B)), _rows(tm, 2 * D_A)],
        out_specs=[_rows(tm, IN_COLS), _full((CONV_W, D_B))],
        out_shape=[jax.ShapeDtypeStruct((T, IN_COLS), BF16), jax.ShapeDtypeStruct((CONV_W, D_B), F32)],
        scratch_shapes=[pltpu.VMEM((tm + HALO, D_B), F32), pltpu.VMEM((HALO + tm, D_B), F32),
                        pltpu.VMEM((CONV_W * 8, D_B), F32)],
        compiler_params=_cparams(32),
    )(dc, dc, g, g, proj, cw, dpa)


def _wgrad(name, a, g, bn):
    k = a.shape[1]
    n = g.shape[1]
    tk = min(k, 1024)
    tn = min(n, max(bn, 1024))
    nsub = tn // bn
    tt = TM
    nt = T // tt

    def body(a_ref, g_ref, o_ref, ob_ref):
        t = pl.program_id(2)

        @pl.when(t == 0)
        def _():
            o_ref[...] = jnp.zeros(o_ref.shape, F32)

        av = a_ref[...]
        for s in range(nsub):
            o_ref[s] += _dot_tn(av, g_ref[:, s * bn:(s + 1) * bn])

        @pl.when(t == nt - 1)
        def _():
            ob_ref[...] = o_ref[...].astype(BF16)

    ospec = pl.BlockSpec((nsub, tk, bn), lambda ki, ni, ti: (ni, ki, 0))
    return pl.pallas_call(
        body, name=name, grid=(k // tk, n // tn, nt),
        in_specs=[pl.BlockSpec((tt, tk), lambda ki, ni, ti: (ti, ki)),
                  pl.BlockSpec((tt, tn), lambda ki, ni, ti: (ti, ni))],
        out_specs=[ospec, ospec],
        out_shape=[jax.ShapeDtypeStruct((n // bn, k, bn), F32), jax.ShapeDtypeStruct((n // bn, k, bn), BF16)],
        compiler_params=_cparams(40, ("arbitrary", "arbitrary", "arbitrary")),
    )(a, g)


def _coords():
    return lax.axis_index("x"), lax.axis_index("y"), lax.axis_index("c")


def _flat(x, y, c):
    return 4 * x + 2 * y + c


def _peer(k):
    x, y, c = _coords()
    return (x ^ ((k >> 2) & 1), y ^ ((k >> 1) & 1), c ^ (k & 1))


def _hbm(a):
    return pltpu.with_memory_space_constraint(a, pltpu.HBM)


def _hbm_like(a):
    return pltpu.HBM(a.shape, a.dtype)


def _peer_sems():
    return pltpu.SemaphoreType.DMA((N_DEV - 1,))


def _place_own(shards):
    n = len(shards)

    def body(*refs):
        ins, outs = refs[:n], refs[n:2 * n]
        stage_in, stage_out = refs[2 * n:3 * n], refs[3 * n:4 * n]
        in_sems, out_sems = refs[4 * n], refs[4 * n + 1]
        me = _flat(*_coords())
        loads = [pltpu.make_async_copy(ins[a], stage_in[a], in_sems.at[a]) for a in range(n)]
        stores = [pltpu.make_async_copy(stage_out[a], outs[a].at[me], out_sems.at[a]) for a in range(n)]
        for cp in loads:
            cp.start()
        for a in range(n):
            loads[a].wait()
            stage_out[a][...] = stage_in[a][...].astype(stage_out[a].dtype)
            stores[a].start()
        for cp in stores:
            cp.wait()

    return pl.pallas_call(
        body, name="place_own", in_specs=[ANY] * n, out_specs=[ANY] * n,
        out_shape=[jax.ShapeDtypeStruct((N_DEV,) + s.shape, dt) for s, dt in shards],
        scratch_shapes=[*[pltpu.VMEM(s.shape, s.dtype) for s, _ in shards],
                        *[pltpu.VMEM(s.shape, dt) for s, dt in shards],
                        pltpu.SemaphoreType.DMA((n,)), pltpu.SemaphoreType.DMA((n,))],
        compiler_params=pltpu.CompilerParams(vmem_limit_bytes=40 * MIB),
    )(*[s for s, _ in shards])


def _gather_start(lands):
    n = len(lands)

    def body(*refs):
        lnd, send, recv = refs[:n], refs[n:2 * n], refs[2 * n:3 * n]
        me = _flat(*_coords())
        for a in range(n):
            for k in range(1, N_DEV):
                pltpu.make_async_remote_copy(
                    src_ref=lnd[a].at[me], dst_ref=lnd[a].at[me], send_sem=send[a].at[k - 1],
                    recv_sem=recv[a].at[k - 1], device_id=_peer(k), device_id_type=MESH_ID).start()

    outs = pl.pallas_call(
        body, name="gather_start",
        out_shape=(*[_peer_sems() for _ in range(2 * n)], *[_hbm_like(l) for l in lands]),
        in_specs=[HBM] * n, out_specs=(*[SEM] * (2 * n), *[HBM] * n),
        input_output_aliases={i: 2 * n + i for i in range(n)},
        compiler_params=pltpu.CompilerParams(has_side_effects=EFFECT),
    )(*[_hbm(l) for l in lands])
    return outs[:n], outs[n:2 * n], outs[2 * n:]


def _gather_wait(name, land, send, recv, after):
    def body(l_ref, send_ref, recv_ref, after_ref, l_out):
        for k in range(1, N_DEV):
            cp = pltpu.make_async_remote_copy(
                src_ref=l_ref.at[0], dst_ref=l_ref.at[0], send_sem=send_ref.at[k - 1], recv_sem=recv_ref.at[k - 1],
                device_id=_peer(k), device_id_type=MESH_ID)
            cp.wait_send()
            cp.wait_recv()

    return pl.pallas_call(
        body, name=name, out_shape=_hbm_like(land),
        in_specs=(HBM, SEM, SEM, ANY), out_specs=HBM, input_output_aliases={0: 0},
        compiler_params=pltpu.CompilerParams(has_side_effects=EFFECT),
    )(land, send, recv, after)


def _scatter_start(name, g16, land, layer):
    def body(g_ref, land_ref, send, recv, g_out, land_out):
        for k in range(1, N_DEV):
            to = _peer(k)
            pltpu.make_async_remote_copy(
                src_ref=g_ref.at[_flat(*to)], dst_ref=land_ref.at[layer, k - 1],
                send_sem=send.at[k - 1], recv_sem=recv.at[k - 1], device_id=to, device_id_type=MESH_ID).start()

    return pl.pallas_call(
        body, name=name,
        out_shape=(_peer_sems(), _peer_sems(), _hbm_like(g16), _hbm_like(land)),
        in_specs=[HBM, HBM], out_specs=(SEM, SEM, HBM, HBM), input_output_aliases={0: 2, 1: 3},
        compiler_params=pltpu.CompilerParams(has_side_effects=EFFECT),
    )(_hbm(g16), _hbm(land))


def _scatter_wait(grads, lands, sends, recvs, after):
    n, nw = len(grads), len(lands)

    def body(*refs):
        g, lnd = refs[:n], refs[n:n + nw]
        send, recv = refs[n + nw:2 * n + nw], refs[2 * n + nw:3 * n + nw]
        for a in range(n):
            for k in range(1, N_DEV):
                cp = pltpu.make_async_remote_copy(
                    src_ref=g[a].at[0], dst_ref=lnd[a // DEPTH].at[a % DEPTH, 0],
                    send_sem=send[a].at[k - 1], recv_sem=recv[a].at[k - 1],
                    device_id=_peer(k), device_id_type=MESH_ID)
                cp.wait_send()
                cp.wait_recv()

    outs = pl.pallas_call(
        body, name="scatter_wait",
        out_shape=(*[_hbm_like(g) for g in grads], *[_hbm_like(l) for l in lands]),
        in_specs=(*[HBM] * (n + nw), *[SEM] * (2 * n), ANY), out_specs=[HBM] * (n + nw),
        input_output_aliases={i: i for i in range(n + nw)},
        compiler_params=pltpu.CompilerParams(has_side_effects=EFFECT),
    )(*grads, *lands, *sends, *recvs, after)
    return outs[n:]


def _own_blocks(grads_f32):
    nw = len(grads_f32)
    flat32 = [g for w in grads_f32 for g in w]
    n = len(flat32)

    def body(*refs):
        g32, own, stage = refs[:n], refs[n:n + nw], refs[n + nw:2 * n + nw]
        in_sems, out_sems = refs[2 * n + nw], refs[2 * n + nw + 1]
        me = _flat(*_coords())
        loads = [pltpu.make_async_copy(g32[a].at[me], stage[a], in_sems.at[a]) for a in range(n)]
        stores = [pltpu.make_async_copy(stage[a], own[a // DEPTH].at[a % DEPTH], out_sems.at[a]) for a in range(n)]
        for cp in loads:
            cp.start()
        for a in range(n):
            loads[a].wait()
            stores[a].start()
        for cp in stores:
            cp.wait()

    return pl.pallas_call(
        body, name="own_blocks", in_specs=[ANY] * n, out_specs=[ANY] * nw,
        out_shape=[jax.ShapeDtypeStruct((DEPTH,) + w[0].shape[1:], F32) for w in grads_f32],
        scratch_shapes=[*[pltpu.VMEM(g.shape[1:], F32) for g in flat32],
                        pltpu.SemaphoreType.DMA((n,)), pltpu.SemaphoreType.DMA((n,))],
        compiler_params=pltpu.CompilerParams(vmem_limit_bytes=24 * MIB),
    )(*flat32)


def _all_reduce_small(part):
    rows = part.shape[0]
    br = rows // N_DEV
    assert br * N_DEV == rows and br % 8 == 0

    def body(p_ref, o_ref, slots, send1, recv1, send2, recv2):
        me = _flat(*_coords())

        def block(ref, d):
            return ref.at[pl.ds(pl.multiple_of(d * br, 8), br), :]

        slots[me] = p_ref[pl.ds(pl.multiple_of(me * br, 8), br), :]
        scatter, gather = [], []
        for k in range(1, N_DEV):
            to = _peer(k)
            scatter.append(pltpu.make_async_remote_copy(
                src_ref=block(p_ref, _flat(*to)), dst_ref=slots.at[me],
                send_sem=send1.at[k - 1], recv_sem=recv1.at[k - 1], device_id=to, device_id_type=MESH_ID))
            gather.append(pltpu.make_async_remote_copy(
                src_ref=block(o_ref, me), dst_ref=block(o_ref, me),
                send_sem=send2.at[k - 1], recv_sem=recv2.at[k - 1], device_id=to, device_id_type=MESH_ID))
        for cp in scatter:
            cp.start()
        for cp in scatter:
            cp.wait()
        acc = slots[0]
        for d in range(1, N_DEV):
            acc = acc + slots[d]
        o_ref[pl.ds(pl.multiple_of(me * br, 8), br), :] = acc
        for cp in gather:
            cp.start()
        for cp in gather:
            cp.wait()

    return pl.pallas_call(
        body, name="all_reduce_small",
        in_specs=[pl.BlockSpec(memory_space=pltpu.VMEM)], out_specs=pl.BlockSpec(memory_space=pltpu.VMEM),
        out_shape=jax.ShapeDtypeStruct(part.shape, F32),
        scratch_shapes=[pltpu.VMEM((N_DEV, br, LANES), F32)] + [pltpu.SemaphoreType.DMA((N_DEV - 1,))] * 4,
        compiler_params=pltpu.CompilerParams(vmem_limit_bytes=24 * MIB),
    )(part)


def _adam_math(w, g, m, v):
    m = ADAM_B1 * m + (1.0 - ADAM_B1) * g
    v = ADAM_B2 * v + (1.0 - ADAM_B2) * (g * g)
    m_hat = m / (1.0 - ADAM_B1 ** ADAM_STEP)
    v_hat = v / (1.0 - ADAM_B2 ** ADAM_STEP)
    delta = -ADAM_LR * (m_hat / (jnp.sqrt(v_hat) + ADAM_EPS) + ADAM_WD * w)
    return delta, m, v


def _adam_sharded(name, own, recv, w, m, v):
    _, r, c = w.shape
    tr = min(r, 256)

    def body(own_ref, recv_ref, w_ref, m_ref, v_ref, g_ref, d_ref, nm_ref, nv_ref):
        g = own_ref[...]
        for k in range(N_DEV - 1):
            g = g + recv_ref[k].astype(F32)
        delta, nm, nv = _adam_math(w_ref[...], g, m_ref[...], v_ref[...])
        g_ref[...] = g
        d_ref[...] = delta
        nm_ref[...] = nm
        nv_ref[...] = nv

    blk = pl.BlockSpec((None, tr, c), lambda l, i: (l, i, 0))
    shp = jax.ShapeDtypeStruct(w.shape, F32)
    return pl.pallas_call(
        body, name=name, grid=(DEPTH, r // tr),
        in_specs=[blk, pl.BlockSpec((None, N_DEV - 1, tr, c), lambda l, i: (l, 0, i, 0)), blk, blk, blk],
        out_specs=[blk] * 4, out_shape=[shp] * 4,
        compiler_params=_cparams(32, ("arbitrary", "arbitrary")),
    )(own, recv, w, m, v)


def _adam_small(g, w, m, v):
    rows = g.shape[0]

    def body(g_ref, w_ref, m_ref, v_ref, d_ref, nm_ref, nv_ref):
        delta, nm, nv = _adam_math(w_ref[...], g_ref[...], m_ref[...], v_ref[...])
        d_ref[...] = delta
        nm_ref[...] = nm
        nv_ref[...] = nv

    spec = _full((rows, LANES))
    shp = jax.ShapeDtypeStruct((rows, LANES), F32)
    return pl.pallas_call(
        body, name="adam_small", grid=(1,),
        in_specs=[spec] * 4, out_specs=[spec] * 3, out_shape=[shp] * 3,
        compiler_params=_cparams(24),
    )(g, w, m, v)


SMALL = ["norm1_g", "sgu_ln_g", "sgu_ln_b", "sgu_w", "sgu_b", "conv_b", "conv_ln_g", "conv_ln_b", "norm2_g",
         "final_g"]


def _pack(arrays):
    flat = jnp.concatenate([a.reshape(-1) for a in arrays])
    pad = (-flat.shape[0]) % (N_DEV * 8 * LANES)
    return jnp.pad(flat, (0, pad)).reshape(-1, LANES)


def _unpack(packed, shapes):
    flat = packed.reshape(-1)
    out, off = [], 0
    for s in shapes:
        size = 1
        for d in s:
            size *= d
        out.append(flat[off:off + size].reshape(s))
        off += size
    return out


def kernel(x, norm1_g, w_in, sgu_ln_g, sgu_ln_b, sgu_w, sgu_b, conv_w, conv_b, conv_ln_g, conv_ln_b, w_out, norm2_g, w_ff1, w_ff2, final_g, loss_target, m_norm1_g, m_w_in, m_sgu_ln_g, m_sgu_ln_b, m_sgu_w, m_sgu_b, m_conv_w, m_conv_b, m_conv_ln_g, m_conv_ln_b, m_w_out, m_norm2_g, m_w_ff1, m_w_ff2, m_final_g, v_norm1_g, v_w_in, v_sgu_ln_g, v_sgu_ln_b, v_sgu_w, v_sgu_b, v_conv_w, v_conv_b, v_conv_ln_g, v_conv_ln_b, v_w_out, v_norm2_g, v_w_ff1, v_w_ff2, v_final_g):
    x2d = x.reshape(T, D)
    tgt = loss_target.reshape(T, D)
    cw_shard = conv_w.reshape(CONV_W, LANES)

    gnames, shards = [], []
    for l in range(DEPTH):
        for k, w in (("w_in", w_in), ("w_out", w_out), ("w_ff1", w_ff1), ("w_ff2", w_ff2)):
            gnames.append(f"{k}{l}")
            shards.append((w[l], BF16))
        if l == 0:
            gnames.insert(1, "conv_w")
            shards.insert(1, (cw_shard, F32))
    sends, recvs, lands = _gather_start(_place_own(shards))
    gidx = {k: i for i, k in enumerate(gnames)}

    def gathered(k, after):
        i = gidx[k]
        return _gather_wait("gather_wait_" + k, lands[i], sends[i], recvs[i], after)

    saved = []
    xl = x2d
    cw_full = None
    for l in range(DEPTH):
        g1 = norm1_g[l].reshape(1, D)
        g2 = norm2_g[l].reshape(1, D)
        lg, lb = sgu_ln_g[l].reshape(1, D_A), sgu_ln_b[l].reshape(1, D_A)
        bst = sgu_b[l].T
        cb = conv_b[l].reshape(1, D_B)
        clg, clb = conv_ln_g[l].reshape(1, D_B), conv_ln_b[l].reshape(1, D_B)
        wg_in = gathered(f"w_in{l}", xl)
        h, proj = _fwd_in(xl, g1, wg_in)
        if l == 0:
            cw_full = gathered("conv_w", proj).reshape(N_DEV, DEPTH, CONV_W, D_B // N_DEV).transpose(
                1, 2, 0, 3).reshape(DEPTH, CONV_W, D_B)
        mix, c, g = _mixer_fwd(proj, lg, lb, sgu_w[l], bst, cw_full[l], cb, clg, clb)
        wout = gathered(f"w_out{l}", mix).reshape(D, D)
        x1, h2 = _fwd_out(mix, wout, xl, g2)
        wg_ff1 = gathered(f"w_ff1{l}", h2)
        r, f = _fwd_ff1(h2, wg_ff1)
        w2 = gathered(f"w_ff2{l}", f).reshape(D_FF, D)
        x2 = _fwd_ff2(f, w2, x1)
        saved.append(dict(x=xl, h=h, proj=proj, mix=mix, c=c, g=g, x1=x1, h2=h2, r=r, f=f, wg_in=wg_in, wout=wout,
                          wg_ff1=wg_ff1, w2=w2, g1=g1, g2=g2, lg=lg, lb=lb, bst=bst, clg=clg, clb=clb))
        xl = x2

    dx, dxb, loss_part, d_final_g = _loss_bwd(xl, final_g.reshape(1, D), tgt)
    loss = lax.psum(loss_part[0, 0], ("x", "y", "c"))

    names = ["w_in", "w_out", "w_ff1", "w_ff2"]
    block = dict(w_in=(D, IN_COLS // N_DEV), w_out=(D // N_DEV, D), w_ff1=(D, D_FF // N_DEV), w_ff2=(D_FF // N_DEV, D))
    land = {k: lax.empty((DEPTH, N_DEV - 1) + block[k], BF16) for k in names}
    big32 = {k: [None] * DEPTH for k in names}
    big16 = {k: [None] * DEPTH for k in names}
    ssend = {k: [None] * DEPTH for k in names}
    srecv = {k: [None] * DEPTH for k in names}

    def send_grad(k, l, g32, g16):
        big32[k][l] = g32.reshape((N_DEV,) + block[k])
        ssend[k][l], srecv[k][l], big16[k][l], land[k] = _scatter_start(
            f"scatter_start_{k}{l}", g16.reshape((N_DEV,) + block[k]), land[k], l)

    small = {}
    d_conv_w = [None] * DEPTH
    for l in reversed(range(DEPTH)):
        s = saved[l]
        send_grad("w_ff2", l, *_wgrad("wgrad_ff2", s["f"], dxb, D))
        df1 = _bwd_ff2(dxb, s["w2"], s["r"], big16["w_ff2"][l])
        send_grad("w_ff1", l, *_wgrad("wgrad_ff1", s["h2"], df1, D_FF // N_DEV))
        dx1, dx1b, dg2 = _bwd_norm_matmul("bwd_ff1", df1, s["wg_ff1"], s["x1"], s["g2"], dx, big16["w_ff1"][l])
        send_grad("w_out", l, *_wgrad("wgrad_out", s["mix"], dx1b, D))
        dpa, dc, dlg, dlb, dwm, dbs, dcb, dclg, dclb = _mixer_bwd_a(
            dx1b, s["wout"], s["proj"], s["c"], s["lg"], s["lb"], sgu_w[l], s["bst"], s["clg"], s["clb"],
            big16["w_out"][l])
        dproj, d_conv_w[l] = _mixer_bwd_b(dc, s["g"], s["proj"], cw_full[l], dpa)
        send_grad("w_in", l, *_wgrad("wgrad_in", s["h"], dproj, IN_COLS // N_DEV))
        dx, dxb, dg1 = _bwd_norm_matmul("bwd_in", dproj, s["wg_in"], s["x"], s["g1"], dx1, big16["w_in"][l])
        small[l] = dict(norm1_g=dg1, sgu_ln_g=dlg, sgu_ln_b=dlb, sgu_w=dwm, sgu_b=dbs, conv_b=dcb, conv_ln_g=dclg,
                        conv_ln_b=dclb, norm2_g=dg2)
    grad_x = dx.reshape(1, T, D)

    own = _own_blocks([big32[k] for k in names])
    recv = _scatter_wait([g for k in names for g in big16[k]], [land[k] for k in names],
                         [q for k in names for q in ssend[k]], [q for k in names for q in srecv[k]], dx)
    wmv = dict(w_in=(w_in, m_w_in, v_w_in), w_out=(w_out, m_w_out, v_w_out), w_ff1=(w_ff1, m_w_ff1, v_w_ff1),
               w_ff2=(w_ff2, m_w_ff2, v_w_ff2))
    res = {}
    for idx, k in enumerate(names):
        res[k] = _adam_sharded("adam_" + k, own[idx], recv[idx], *wmv[k])

    rep = dict(norm1_g=(norm1_g, m_norm1_g, v_norm1_g), sgu_ln_g=(sgu_ln_g, m_sgu_ln_g, v_sgu_ln_g),
               sgu_ln_b=(sgu_ln_b, m_sgu_ln_b, v_sgu_ln_b), sgu_w=(sgu_w, m_sgu_w, v_sgu_w),
               sgu_b=(sgu_b, m_sgu_b, v_sgu_b), conv_b=(conv_b, m_conv_b, v_conv_b),
               conv_ln_g=(conv_ln_g, m_conv_ln_g, v_conv_ln_g), conv_ln_b=(conv_ln_b, m_conv_ln_b, v_conv_ln_b),
               norm2_g=(norm2_g, m_norm2_g, v_norm2_g), final_g=(final_g, m_final_g, v_final_g))
    parts = []
    for k in SMALL:
        if k == "final_g":
            parts.append(d_final_g.reshape(rep[k][0].shape))
        else:
            parts.append(jnp.stack([small[l][k].reshape(rep[k][0].shape[1:]) for l in range(DEPTH)]))
    parts.append(jnp.stack(d_conv_w))
    shapes = [p.shape for p in parts]
    summed = _unpack(_all_reduce_small(_pack(parts)), shapes)
    me = _flat(*_coords())
    g_conv_w = lax.dynamic_slice_in_dim(summed[-1], me * (D_B // N_DEV), D_B // N_DEV, axis=2)
    g_small = summed[:-1] + [g_conv_w]
    keys = SMALL + ["conv_w"]
    rep["conv_w"] = (conv_w, m_conv_w, v_conv_w)
    shapes = [a.shape for a in g_small]
    packed = [_pack(g_small)] + [_pack([rep[k][i] for k in keys]) for i in range(3)]
    d_s, nm_s, nv_s = (_unpack(a, shapes) for a in _adam_small(*packed))
    for i, k in enumerate(keys):
        res[k] = (g_small[i], d_s[i], nm_s[i], nv_s[i])

    order = ["norm1_g", "w_in", "sgu_ln_g", "sgu_ln_b", "sgu_w", "sgu_b", "conv_w", "conv_b", "conv_ln_g",
             "conv_ln_b", "w_out", "norm2_g", "w_ff1", "w_ff2", "final_g"]
    return (loss, grad_x, *[res[k][0] for k in order], *[res[k][1] for k in order],
            *[res[k][2] for k in order], *[res[k][3] for k in order])
```

```python
import functools

import jax
import jax.numpy as jnp
from jax import lax
from jax.experimental import pallas as pl
from jax.experimental.pallas import tpu as pltpu

F32 = jnp.float32
BF16 = jnp.bfloat16

N_DEV = 8
DEPTH = 2
T = 4096
D = 1024
D_A = 512
D_B = 512
CHUNK = 128
H_A = 4
H_B = 4
CONV_W = 31
HALO = 32
D_FF = 4096
IN_COLS = 2048
EPS = 1e-6

ADAM_LR = 0.001
ADAM_B1 = 0.9
ADAM_B2 = 0.999
ADAM_EPS = 1e-08
ADAM_WD = 0.01
ADAM_STEP = 10

TM = 512
TM_MIX = 256
RB = 64
LANES = 128
MIB = 1024 * 1024
SCOPED_VMEM_MIB = 60

SQRT_HALF = 0.7071067811865476
INV_SQRT_2PI = 0.3989422804014327

MESH_ID = pl.DeviceIdType.MESH
ANY = pl.BlockSpec(memory_space=pl.ANY)
HBM = pl.BlockSpec(memory_space=pltpu.HBM)
SEM = pl.BlockSpec(memory_space=pltpu.SEMAPHORE)
EFFECT = pltpu.SideEffectType.DATAFLOW_SIDE_EFFECTING


def _cparams(vmem_mib, sem=("arbitrary",)):
    assert vmem_mib <= SCOPED_VMEM_MIB
    return pltpu.CompilerParams(dimension_semantics=sem, vmem_limit_bytes=SCOPED_VMEM_MIB * MIB)


def _full(shape):
    return pl.BlockSpec(shape, lambda *_: (0,) * len(shape))


def _rows(tm, cols):
    return pl.BlockSpec((tm, cols), lambda i: (i, 0))


def _gelu(x):
    cdf = 0.5 * (1.0 + lax.erf(x * SQRT_HALF))
    return x * cdf, cdf


def _gelu_grad(x, cdf):
    return cdf + x * (INV_SQRT_2PI * jnp.exp(-0.5 * x * x))


def _sigmoid(x):
    return 1.0 / (1.0 + jnp.exp(-x))


def _ln(x):
    mu = jnp.mean(x, axis=-1, keepdims=True)
    xc = x - mu
    rstd = lax.rsqrt(jnp.mean(xc * xc, axis=-1, keepdims=True) + EPS)
    return xc * rstd, rstd


def _ln_bwd(dyh, xhat, rstd):
    return rstd * (dyh - jnp.mean(dyh, axis=-1, keepdims=True) - xhat * jnp.mean(dyh * xhat, axis=-1, keepdims=True))


def _rms(x):
    return lax.rsqrt(jnp.mean(x * x, axis=-1, keepdims=True) + EPS)


def _rms_bwd(dh, x, r, g):
    n = x * r
    dn = dh * g
    dx = r * (dn - n * jnp.mean(dn * n, axis=-1, keepdims=True))
    return dx, jnp.sum(dh * n, axis=0, keepdims=True)


def _dot(a, b):
    return jnp.dot(a, b, preferred_element_type=F32)


def _dot_nt(a, b):
    return lax.dot_general(a, b, (((1,), (1,)), ((), ())), preferred_element_type=F32)


def _dot_tn(a, b):
    return lax.dot_general(a, b, (((0,), (0,)), ((), ())), preferred_element_type=F32)


def _tril_mask():
    r = lax.broadcasted_iota(jnp.int32, (CHUNK, CHUNK), 0)
    c = lax.broadcasted_iota(jnp.int32, (CHUNK, CHUNK), 1)
    return r >= c


def _fwd_in(x, g1, wg):
    bn = wg.shape[2]

    def body(x_ref, g_ref, w_ref, h_ref, p_ref):
        xv = x_ref[...]
        h = (xv * _rms(xv) * g_ref[...]).astype(BF16)
        h_ref[...] = h
        for j in range(N_DEV):
            p_ref[:, j * bn:(j + 1) * bn] = _dot(h, w_ref[j])

    return pl.pallas_call(
        body, name="fwd_in", grid=(T // TM,),
        in_specs=[_rows(TM, D), _full((1, D)), _full(wg.shape)],
        out_specs=[_rows(TM, D), _rows(TM, IN_COLS)],
        out_shape=[jax.ShapeDtypeStruct((T, D), BF16), jax.ShapeDtypeStruct((T, IN_COLS), F32)],
        compiler_params=_cparams(32),
    )(x, g1, wg)


def _build_shifts(sh_ref):
    rows = sh_ref.shape[1]
    for p in range(1, 8):
        sh_ref[p, 0:rows - 8, :] = sh_ref[0, p:p + rows - 8, :]


def _shifted(sh_ref, row, cols):
    return sh_ref[row % 8, row - row % 8:row - row % 8 + RB, cols]


def _conv_taps(sh_ref, w_ref, cols, base, first_tap_row, step):
    acc = jnp.zeros((RB, LANES), F32)
    for k in range(CONV_W):
        acc = acc + _shifted(sh_ref, base + first_tap_row + step * k, cols) * w_ref[k:k + 1, cols]
    return acc


def _mixer_fwd(proj, lg, lb, wm, bst, cw, cb, clg, clb):
    tm = TM_MIX
    hb = tm // HALO

    def body(p_ref, ph_ref, lg_ref, lb_ref, wm_ref, bs_ref, cw_ref, cb_ref, clg_ref, clb_ref,
             mix_ref, c_ref, g_ref, gbuf):
        i = pl.program_id(0)
        u, _ = _gelu(p_ref[:, 0:D_A])
        vg, _ = _gelu(p_ref[:, D_A:2 * D_A])
        xhat, _ = _ln(vg)
        v = (xhat * lg_ref[...] + lb_ref[...]).astype(BF16)
        mask = _tril_mask()
        for h in range(H_A):
            hc = slice(h * CHUNK, (h + 1) * CHUNK)
            wmh = jnp.where(mask, wm_ref[h], 0.0).astype(BF16)
            for c in range(tm // CHUNK):
                rc = slice(c * CHUNK, (c + 1) * CHUNK)
                mixed = _dot(wmh, v[rc, hc]) + bs_ref[:, h:h + 1]
                mix_ref[rc, hc] = (u[rc, hc] * mixed).astype(BF16)

        g = p_ref[:, 2 * D_A:2 * D_A + D_B] * _sigmoid(p_ref[:, 2 * D_A + D_B:IN_COLS])
        g_ref[...] = g
        gh = ph_ref[:, 0:D_B] * _sigmoid(ph_ref[:, D_B:2 * D_B])
        gbuf[0, 0:HALO, :] = jnp.where(i > 0, gh, 0.0)
        gbuf[0, HALO:HALO + tm, :] = g
        _build_shifts(gbuf)
        for q in range(H_B):
            cols = slice(q * LANES, (q + 1) * LANES)
            for rb in range(tm // RB):
                acc = _conv_taps(gbuf, cw_ref, cols, rb * RB, HALO - (CONV_W - 1), 1)
                c_ref[rb * RB:(rb + 1) * RB, cols] = acc + cb_ref[:, cols]
        for q in range(H_B):
            cols = slice(q * LANES, (q + 1) * LANES)
            chat, _ = _ln(c_ref[:, cols])
            z = chat * clg_ref[:, cols] + clb_ref[:, cols]
            mix_ref[:, D_A + q * LANES:D_A + (q + 1) * LANES] = (z * _sigmoid(z)).astype(BF16)

    vec = _full((1, D_A))
    return pl.pallas_call(
        body, name="mixer_fwd", grid=(T // tm,),
        in_specs=[_rows(tm, IN_COLS),
                  pl.BlockSpec((HALO, 2 * D_B), lambda i: (jnp.maximum(i * hb - 1, 0), 1)),
                  vec, vec, _full((H_A, CHUNK, CHUNK)), _full((CHUNK, H_A)),
                  _full((CONV_W, D_B)), vec, vec, vec],
        out_specs=[_rows(tm, D), _rows(tm, D_B), _rows(tm, D_B)],
        out_shape=[jax.ShapeDtypeStruct((T, D), BF16), jax.ShapeDtypeStruct((T, D_B), F32),
                   jax.ShapeDtypeStruct((T, D_B), F32)],
        scratch_shapes=[pltpu.VMEM((8, HALO + tm, D_B), F32)],
        compiler_params=_cparams(32),
    )(proj, proj, lg, lb, wm, bst, cw, cb, clg, clb)


def _fwd_out(mix, wout, x, g2):
    def body(m_ref, w_ref, x_ref, g_ref, x1_ref, h2_ref):
        x1 = x_ref[...] + _dot(m_ref[...], w_ref[...])
        x1_ref[...] = x1
        h2_ref[...] = (x1 * _rms(x1) * g_ref[...]).astype(BF16)

    return pl.pallas_call(
        body, name="fwd_out", grid=(T // TM,),
        in_specs=[_rows(TM, D), _full((D, D)), _rows(TM, D), _full((1, D))],
        out_specs=[_rows(TM, D), _rows(TM, D)],
        out_shape=[jax.ShapeDtypeStruct((T, D), F32), jax.ShapeDtypeStruct((T, D), BF16)],
        compiler_params=_cparams(32),
    )(mix, wout, x, g2)


def _fwd_ff1(h2, wg):
    bn = wg.shape[2]

    def body(h_ref, w_ref, r_ref):
        h = h_ref[...]
        for j in range(N_DEV):
            r_ref[:, j * bn:(j + 1) * bn] = jnp.maximum(_dot(h, w_ref[j]), 0.0).astype(BF16)

    return pl.pallas_call(
        body, name="fwd_ff1", grid=(T // TM,),
        in_specs=[_rows(TM, D), _full(wg.shape)],
        out_specs=_rows(TM, D_FF),
        out_shape=jax.ShapeDtypeStruct((T, D_FF), BF16),
        compiler_params=_cparams(48),
    )(h2, wg)


def _fwd_ff2(r, w2, x1):
    def body(r_ref, w_ref, x_ref, o_ref):
        rv = r_ref[...]
        o_ref[...] = x_ref[...] + _dot(rv * rv, w_ref[...])

    return pl.pallas_call(
        body, name="fwd_ff2", grid=(T // TM,),
        in_specs=[_rows(TM, D_FF), _full((D_FF, D)), _rows(TM, D)],
        out_specs=_rows(TM, D),
        out_shape=jax.ShapeDtypeStruct((T, D), F32),
        compiler_params=_cparams(48),
    )(r, w2, x1)


def _loss_bwd(xf, gf, tgt):
    def body(x_ref, g_ref, t_ref, dx_ref, dxb_ref, loss_ref, dg_ref):
        i = pl.program_id(0)

        @pl.when(i == 0)
        def _():
            loss_ref[...] = jnp.zeros(loss_ref.shape, F32)
            dg_ref[...] = jnp.zeros(dg_ref.shape, F32)

        xv = x_ref[...]
        r = _rms(xv)
        diff = xv * r * g_ref[...] - t_ref[...]
        loss_ref[...] += 0.5 * jnp.sum(jnp.mean(diff * diff, axis=-1, keepdims=True), axis=0, keepdims=True)
        dx, dg = _rms_bwd(diff * (1.0 / D), xv, r, g_ref[...])
        dx_ref[...] = dx
        dxb_ref[...] = dx.astype(BF16)
        dg_ref[...] += dg

    return pl.pallas_call(
        body, name="loss_bwd", grid=(T // TM,),
        in_specs=[_rows(TM, D), _full((1, D)), _rows(TM, D)],
        out_specs=[_rows(TM, D), _rows(TM, D), _full((8, LANES)), _full((1, D))],
        out_shape=[jax.ShapeDtypeStruct((T, D), F32), jax.ShapeDtypeStruct((T, D), BF16),
                   jax.ShapeDtypeStruct((8, LANES), F32), jax.ShapeDtypeStruct((1, D), F32)],
        compiler_params=_cparams(32),
    )(xf, gf, tgt)


def _bwd_ff2(dxb, w2, r, dep):
    kb = 1024

    def body(d_ref, w_ref, r_ref, dep_ref, o_ref):
        d = d_ref[...]
        for j in range(D_FF // kb):
            cols = slice(j * kb, (j + 1) * kb)
            df = _dot_nt(d, w_ref[cols, :])
            o_ref[:, cols] = (2.0 * r_ref[:, cols].astype(F32) * df).astype(BF16)

    return pl.pallas_call(
        body, name="bwd_ff2", grid=(T // TM,),
        in_specs=[_rows(TM, D), _full((D_FF, D)), _rows(TM, D_FF), ANY],
        out_specs=_rows(TM, D_FF),
        out_shape=jax.ShapeDtypeStruct((T, D_FF), BF16),
        compiler_params=_cparams(48),
    )(dxb, w2, r, dep)


def _bwd_norm_matmul(name, dy, wg, x, g, dres, dep):
    bn = wg.shape[2]
    ncol = dy.shape[1]

    def body(dy_ref, w_ref, x_ref, g_ref, dres_ref, dep_ref, dx_ref, dxb_ref, dg_ref):
        i = pl.program_id(0)

        @pl.when(i == 0)
        def _():
            dg_ref[...] = jnp.zeros(dg_ref.shape, F32)

        dh = jnp.zeros((TM, D), F32)
        for j in range(N_DEV):
            dh = dh + _dot_nt(dy_ref[:, j * bn:(j + 1) * bn], w_ref[j])
        xv = x_ref[...]
        dxn, dg = _rms_bwd(dh, xv, _rms(xv), g_ref[...])
        dx = dres_ref[...] + dxn
        dx_ref[...] = dx
        dxb_ref[...] = dx.astype(BF16)
        dg_ref[...] += dg

    return pl.pallas_call(
        body, name=name, grid=(T // TM,),
        in_specs=[_rows(TM, ncol), _full(wg.shape), _rows(TM, D), _full((1, D)), _rows(TM, D), ANY],
        out_specs=[_rows(TM, D), _rows(TM, D), _full((1, D))],
        out_shape=[jax.ShapeDtypeStruct((T, D), F32), jax.ShapeDtypeStruct((T, D), BF16),
                   jax.ShapeDtypeStruct((1, D), F32)],
        compiler_params=_cparams(48),
    )(dy, wg, x, g, dres, dep)


def _mixer_bwd_a(dxb, wout, proj, c, lg, lb, wm, bst, clg, clb, dep):
    tm = TM_MIX
    n_tiles = T // tm

    def body(dx_ref, wo_ref, p_ref, c_ref, lg_ref, lb_ref, wm_ref, bs_ref, clg_ref, clb_ref, dep_ref,
             dpa_ref, dc_ref, dlg_ref, dlb_ref, dwm_ref, dbs_ref, dcb_ref, dclg_ref, dclb_ref,
             dv_buf, db_acc):
        i = pl.program_id(0)

        @pl.when(i == 0)
        def _():
            for ref in (dlg_ref, dlb_ref, dwm_ref, dbs_ref, dcb_ref, dclg_ref, dclb_ref, db_acc):
                ref[...] = jnp.zeros(ref.shape, F32)

        dmix = _dot_nt(dx_ref[...], wo_ref[...])
        ua = p_ref[:, 0:D_A]
        va = p_ref[:, D_A:2 * D_A]
        u, cdf_u = _gelu(ua)
        vg, cdf_v = _gelu(va)
        xhat, rstd = _ln(vg)
        v = (xhat * lg_ref[...] + lb_ref[...]).astype(BF16)
        mask = _tril_mask()
        for h in range(H_A):
            hc = slice(h * CHUNK, (h + 1) * CHUNK)
            wmh = jnp.where(mask, wm_ref[h], 0.0).astype(BF16)
            for cidx in range(tm // CHUNK):
                rc = slice(cidx * CHUNK, (cidx + 1) * CHUNK)
                vb = v[rc, hc]
                mixed = _dot(wmh, vb) + bs_ref[:, h:h + 1]
                da = dmix[rc, hc]
                dpa_ref[rc, hc] = (da * mixed * _gelu_grad(ua[rc, hc], cdf_u[rc, hc])).astype(BF16)
                dmixed = da * u[rc, hc]
                dmb = dmixed.astype(BF16)
                dv_buf[rc, hc] = _dot_tn(wmh, dmb)
                dwm_ref[h] += _dot_nt(dmb, vb)
                db_acc[:, hc] += dmixed
        dv = dv_buf[...]
        dlb_ref[...] += jnp.sum(dv, axis=0, keepdims=True)
        dlg_ref[...] += jnp.sum(dv * xhat, axis=0, keepdims=True)
        dvg = _ln_bwd(dv * lg_ref[...], xhat, rstd)
        dpa_ref[:, D_A:2 * D_A] = (dvg * _gelu_grad(va, cdf_v)).astype(BF16)

        for q in range(H_B):
            cols = slice(q * LANES, (q + 1) * LANES)
            chat, crstd = _ln(c_ref[:, cols])
            z = chat * clg_ref[:, cols] + clb_ref[:, cols]
            sg = _sigmoid(z)
            dz = dmix[:, D_A + q * LANES:D_A + (q + 1) * LANES] * (sg * (1.0 + z * (1.0 - sg)))
            dclb_ref[:, cols] += jnp.sum(dz, axis=0, keepdims=True)
            dclg_ref[:, cols] += jnp.sum(dz * chat, axis=0, keepdims=True)
            dc = _ln_bwd(dz * clg_ref[:, cols], chat, crstd)
            dc_ref[:, cols] = dc
            dcb_ref[:, cols] += jnp.sum(dc, axis=0, keepdims=True)

        @pl.when(i == n_tiles - 1)
        def _():
            for h in range(H_A):
                hc = slice(h * CHUNK, (h + 1) * CHUNK)
                dwm_ref[h] = jnp.where(mask, dwm_ref[h], 0.0)
                dbs_ref[h:h + 1, :] = jnp.sum(db_acc[:, hc].T, axis=0, keepdims=True)

    vec = _full((1, D_A))
    vshape = jax.ShapeDtypeStruct((1, D_A), F32)
    return pl.pallas_call(
        body, name="mixer_bwd_a", grid=(n_tiles,),
        in_specs=[_rows(tm, D), _full((D, D)), _rows(tm, IN_COLS), _rows(tm, D_B), vec, vec,
                  _full((H_A, CHUNK, CHUNK)), _full((CHUNK, H_A)), vec, vec, ANY],
        out_specs=[_rows(tm, 2 * D_A), _rows(tm, D_B), vec, vec, _full((H_A, CHUNK, CHUNK)),
                   _full((H_A, CHUNK)), vec, vec, vec],
        out_shape=[jax.ShapeDtypeStruct((T, 2 * D_A), BF16), jax.ShapeDtypeStruct((T, D_B), F32), vshape, vshape,
                   jax.ShapeDtypeStruct((H_A, CHUNK, CHUNK), F32), jax.ShapeDtypeStruct((H_A, CHUNK), F32),
                   vshape, vshape, vshape],
        scratch_shapes=[pltpu.VMEM((tm, D_A), F32), pltpu.VMEM((CHUNK, D_A), F32)],
        compiler_params=_cparams(32),
    )(dxb, wout, proj, c, lg, lb, wm, bst, clg, clb, dep)


def _mixer_bwd_b(dc, g, proj, cw, dpa):
    tm = TM_MIX
    n_tiles = T // tm
    hb = tm // HALO

    def body(dc_ref, dch_ref, g_ref, gh_ref, p_ref, cw_ref, dpa_ref, dp_ref, dcw_ref, dcbuf, gbuf, dwacc):
        i = pl.program_id(0)

        @pl.when(i == 0)
        def _():
            dwacc[...] = jnp.zeros(dwacc.shape, F32)

        dcbuf[0, 0:tm, :] = dc_ref[...]
        dcbuf[0, tm:tm + HALO, :] = jnp.where(i < n_tiles - 1, dch_ref[...], 0.0)
        gbuf[0, 0:HALO, :] = jnp.where(i > 0, gh_ref[...], 0.0)
        gbuf[0, HALO:HALO + tm, :] = g_ref[...]
        _build_shifts(dcbuf)
        _build_shifts(gbuf)
        dp_ref[:, 0:2 * D_A] = dpa_ref[...]
        for q in range(H_B):
            cols = slice(q * LANES, (q + 1) * LANES)
            for rb in range(tm // RB):
                rows = slice(rb * RB, (rb + 1) * RB)
                dg = _conv_taps(dcbuf, cw_ref, cols, rb * RB, CONV_W - 1, -1)
                val = p_ref[rows, cols]
                sg = _sigmoid(p_ref[rows, D_B + q * LANES:D_B + (q + 1) * LANES])
                dp_ref[rows, 2 * D_A + q * LANES:2 * D_A + (q + 1) * LANES] = (dg * sg).astype(BF16)
                dp_ref[rows, 2 * D_A + D_B + q * LANES:2 * D_A + D_B + (q + 1) * LANES] = (
                    dg * val * sg * (1.0 - sg)).astype(BF16)
                dcv = dc_ref[rows, cols]
                for k in range(CONV_W):
                    prod = dcv * _shifted(gbuf, rb * RB + HALO - (CONV_W - 1) + k, cols)
                    dwacc[k * 8:(k + 1) * 8, cols] += jnp.sum(prod.reshape(RB // 8, 8, LANES), axis=0)

        @pl.when(i == n_tiles - 1)
        def _():
            for k in range(CONV_W):
                dcw_ref[k:k + 1, :] = jnp.sum(dwacc[k * 8:(k + 1) * 8, :], axis=0, keepdims=True)

    return pl.pallas_call(
        body, name="mixer_bwd_b", grid=(n_tiles,),
        in_specs=[_rows(tm, D_B),
                  pl.BlockSpec((HALO, D_B), lambda i: (jnp.minimum((i + 1) * hb, T // HALO - 1), 0)),
                  _rows(tm, D_B),
                  pl.BlockSpec((HALO, D_B), lambda i: (jnp.maximum(i * hb - 1, 0), 0)),
                  pl.BlockSpec((tm, 2 * D_B), lambda i: (i, 1)),
                  _full((CONV_W, D_B)), _rows(tm, 2 * D_A)],
        out_specs=[_rows(tm, IN_COLS), _full((CONV_W, D_B))],
        out_shape=[jax.ShapeDtypeStruct((T, IN_COLS), BF16), jax.ShapeDtypeStruct((CONV_W, D_B), F32)],
        scratch_shapes=[pltpu.VMEM((8, tm + HALO, D_B), F32), pltpu.VMEM((8, HALO + tm, D_B), F32),
                        pltpu.VMEM((CONV_W * 8, D_B), F32)],
        compiler_params=_cparams(32),
    )(dc, dc, g, g, proj, cw, dpa)


def _wgrad(name, a, g, bn, square_a=False):
    k = a.shape[1]
    n = g.shape[1]
    tk = min(k, 1024)
    tn = min(n, max(bn, 1024))
    nsub = tn // bn
    tt = TM
    nt = T // tt

    def body(a_ref, g_ref, o_ref, ob_ref):
        t = pl.program_id(2)

        @pl.when(t == 0)
        def _():
            o_ref[...] = jnp.zeros(o_ref.shape, F32)

        av = a_ref[...]
        if square_a:
            av = av * av
        for s in range(nsub):
            o_ref[s] += _dot_tn(av, g_ref[:, s * bn:(s + 1) * bn])

        @pl.when(t == nt - 1)
        def _():
            ob_ref[...] = o_ref[...].astype(BF16)

    ospec = pl.BlockSpec((nsub, tk, bn), lambda ki, ni, ti: (ni, ki, 0))
    return pl.pallas_call(
        body, name=name, grid=(k // tk, n // tn, nt),
        in_specs=[pl.BlockSpec((tt, tk), lambda ki, ni, ti: (ti, ki)),
                  pl.BlockSpec((tt, tn), lambda ki, ni, ti: (ti, ni))],
        out_specs=[ospec, ospec],
        out_shape=[jax.ShapeDtypeStruct((n // bn, k, bn), F32), jax.ShapeDtypeStruct((n // bn, k, bn), BF16)],
        compiler_params=_cparams(40, ("arbitrary", "arbitrary", "arbitrary")),
    )(a, g)


def _coords():
    return lax.axis_index("x"), lax.axis_index("y"), lax.axis_index("c")


def _flat(x, y, c):
    return 4 * x + 2 * y + c


def _peer(k):
    x, y, c = _coords()
    return (x ^ ((k >> 2) & 1), y ^ ((k >> 1) & 1), c ^ (k & 1))


def _hbm(a):
    return pltpu.with_memory_space_constraint(a, pltpu.HBM)


def _hbm_like(a):
    return pltpu.HBM(a.shape, a.dtype)


def _peer_sems():
    return pltpu.SemaphoreType.DMA((N_DEV - 1,))


def _place_own(shards):
    n = len(shards)

    def body(*refs):
        ins, outs = refs[:n], refs[n:2 * n]
        stage_in, stage_out = refs[2 * n:3 * n], refs[3 * n:4 * n]
        in_sems, out_sems = refs[4 * n], refs[4 * n + 1]
        me = _flat(*_coords())
        loads = [pltpu.make_async_copy(ins[a], stage_in[a], in_sems.at[a]) for a in range(n)]
        stores = [pltpu.make_async_copy(stage_out[a], outs[a].at[me], out_sems.at[a]) for a in range(n)]
        for cp in loads:
            cp.start()
        for a in range(n):
            loads[a].wait()
            stage_out[a][...] = stage_in[a][...].astype(stage_out[a].dtype)
            stores[a].start()
        for cp in stores:
            cp.wait()

    return pl.pallas_call(
        body, name="place_own", in_specs=[ANY] * n, out_specs=[ANY] * n,
        out_shape=[jax.ShapeDtypeStruct((N_DEV,) + s.shape, dt) for s, dt in shards],
        scratch_shapes=[*[pltpu.VMEM(s.shape, s.dtype) for s, _ in shards],
                        *[pltpu.VMEM(s.shape, dt) for s, dt in shards],
                        pltpu.SemaphoreType.DMA((n,)), pltpu.SemaphoreType.DMA((n,))],
        compiler_params=pltpu.CompilerParams(vmem_limit_bytes=40 * MIB),
    )(*[s for s, _ in shards])


def _gather_start(lands):
    n = len(lands)

    def body(*refs):
        lnd, send, recv = refs[:n], refs[n:2 * n], refs[2 * n:3 * n]
        me = _flat(*_coords())
        for a in range(n):
            for k in range(1, N_DEV):
                pltpu.make_async_remote_copy(
                    src_ref=lnd[a].at[me], dst_ref=lnd[a].at[me], send_sem=send[a].at[k - 1],
                    recv_sem=recv[a].at[k - 1], device_id=_peer(k), device_id_type=MESH_ID).start()

    outs = pl.pallas_call(
        body, name="gather_start",
        out_shape=(*[_peer_sems() for _ in range(2 * n)], *[_hbm_like(l) for l in lands]),
        in_specs=[HBM] * n, out_specs=(*[SEM] * (2 * n), *[HBM] * n),
        input_output_aliases={i: 2 * n + i for i in range(n)},
        compiler_params=pltpu.CompilerParams(has_side_effects=EFFECT),
    )(*[_hbm(l) for l in lands])
    return outs[:n], outs[n:2 * n], outs[2 * n:]


def _gather_wait(name, land, send, recv, after):
    def body(l_ref, send_ref, recv_ref, after_ref, l_out):
        for k in range(1, N_DEV):
            cp = pltpu.make_async_remote_copy(
                src_ref=l_ref.at[0], dst_ref=l_ref.at[0], send_sem=send_ref.at[k - 1], recv_sem=recv_ref.at[k - 1],
                device_id=_peer(k), device_id_type=MESH_ID)
            cp.wait_send()
            cp.wait_recv()

    return pl.pallas_call(
        body, name=name, out_shape=_hbm_like(land),
        in_specs=(HBM, SEM, SEM, ANY), out_specs=HBM, input_output_aliases={0: 0},
        compiler_params=pltpu.CompilerParams(has_side_effects=EFFECT),
    )(land, send, recv, after)


def _scatter_start(name, g16, land, layer):
    def body(g_ref, land_ref, send, recv, g_out, land_out):
        for k in range(1, N_DEV):
            to = _peer(k)
            pltpu.make_async_remote_copy(
                src_ref=g_ref.at[_flat(*to)], dst_ref=land_ref.at[layer, k - 1],
                send_sem=send.at[k - 1], recv_sem=recv.at[k - 1], device_id=to, device_id_type=MESH_ID).start()

    return pl.pallas_call(
        body, name=name,
        out_shape=(_peer_sems(), _peer_sems(), _hbm_like(g16), _hbm_like(land)),
        in_specs=[HBM, HBM], out_specs=(SEM, SEM, HBM, HBM), input_output_aliases={0: 2, 1: 3},
        compiler_params=pltpu.CompilerParams(has_side_effects=EFFECT),
    )(_hbm(g16), _hbm(land))


def _scatter_wait(grads, lands, sends, recvs, after):
    n, nw = len(grads), len(lands)

    def body(*refs):
        g, lnd = refs[:n], refs[n:n + nw]
        send, recv = refs[n + nw:2 * n + nw], refs[2 * n + nw:3 * n + nw]
        for a in range(n):
            for k in range(1, N_DEV):
                cp = pltpu.make_async_remote_copy(
                    src_ref=g[a].at[0], dst_ref=lnd[a // DEPTH].at[a % DEPTH, 0],
                    send_sem=send[a].at[k - 1], recv_sem=recv[a].at[k - 1],
                    device_id=_peer(k), device_id_type=MESH_ID)
                cp.wait_send()
                cp.wait_recv()

    outs = pl.pallas_call(
        body, name="scatter_wait",
        out_shape=(*[_hbm_like(g) for g in grads], *[_hbm_like(l) for l in lands]),
        in_specs=(*[HBM] * (n + nw), *[SEM] * (2 * n), ANY), out_specs=[HBM] * (n + nw),
        input_output_aliases={i: i for i in range(n + nw)},
        compiler_params=pltpu.CompilerParams(has_side_effects=EFFECT),
    )(*grads, *lands, *sends, *recvs, after)
    return outs[n:]


def _own_blocks(grads_f32):
    nw = len(grads_f32)
    flat32 = [g for w in grads_f32 for g in w]
    n = len(flat32)

    def body(*refs):
        g32, own, stage = refs[:n], refs[n:n + nw], refs[n + nw:2 * n + nw]
        in_sems, out_sems = refs[2 * n + nw], refs[2 * n + nw + 1]
        me = _flat(*_coords())
        loads = [pltpu.make_async_copy(g32[a].at[me], stage[a], in_sems.at[a]) for a in range(n)]
        stores = [pltpu.make_async_copy(stage[a], own[a // DEPTH].at[a % DEPTH], out_sems.at[a]) for a in range(n)]
        for cp in loads:
            cp.start()
        for a in range(n):
            loads[a].wait()
            stores[a].start()
        for cp in stores:
            cp.wait()

    return pl.pallas_call(
        body, name="own_blocks", in_specs=[ANY] * n, out_specs=[ANY] * nw,
        out_shape=[jax.ShapeDtypeStruct((DEPTH,) + w[0].shape[1:], F32) for w in grads_f32],
        scratch_shapes=[*[pltpu.VMEM(g.shape[1:], F32) for g in flat32],
                        pltpu.SemaphoreType.DMA((n,)), pltpu.SemaphoreType.DMA((n,))],
        compiler_params=pltpu.CompilerParams(vmem_limit_bytes=24 * MIB),
    )(*flat32)


def _all_reduce_small(part):
    rows = part.shape[0]
    br = rows // N_DEV
    assert br * N_DEV == rows and br % 8 == 0

    def body(p_ref, o_ref, slots, send1, recv1, send2, recv2):
        me = _flat(*_coords())

        def block(ref, d):
            return ref.at[pl.ds(pl.multiple_of(d * br, 8), br), :]

        slots[me] = p_ref[pl.ds(pl.multiple_of(me * br, 8), br), :]
        scatter, gather = [], []
        for k in range(1, N_DEV):
            to = _peer(k)
            scatter.append(pltpu.make_async_remote_copy(
                src_ref=block(p_ref, _flat(*to)), dst_ref=slots.at[me],
                send_sem=send1.at[k - 1], recv_sem=recv1.at[k - 1], device_id=to, device_id_type=MESH_ID))
            gather.append(pltpu.make_async_remote_copy(
                src_ref=block(o_ref, me), dst_ref=block(o_ref, me),
                send_sem=send2.at[k - 1], recv_sem=recv2.at[k - 1], device_id=to, device_id_type=MESH_ID))
        for cp in scatter:
            cp.start()
        for cp in scatter:
            cp.wait()
        acc = slots[0]
        for d in range(1, N_DEV):
            acc = acc + slots[d]
        o_ref[pl.ds(pl.multiple_of(me * br, 8), br), :] = acc
        for cp in gather:
            cp.start()
        for cp in gather:
            cp.wait()

    return pl.pallas_call(
        body, name="all_reduce_small",
        in_specs=[pl.BlockSpec(memory_space=pltpu.VMEM)], out_specs=pl.BlockSpec(memory_space=pltpu.VMEM),
        out_shape=jax.ShapeDtypeStruct(part.shape, F32),
        scratch_shapes=[pltpu.VMEM((N_DEV, br, LANES), F32)] + [pltpu.SemaphoreType.DMA((N_DEV - 1,))] * 4,
        compiler_params=pltpu.CompilerParams(vmem_limit_bytes=24 * MIB),
    )(part)


def _adam_math(w, g, m, v):
    m = ADAM_B1 * m + (1.0 - ADAM_B1) * g
    v = ADAM_B2 * v + (1.0 - ADAM_B2) * (g * g)
    m_hat = m / (1.0 - ADAM_B1 ** ADAM_STEP)
    v_hat = v / (1.0 - ADAM_B2 ** ADAM_STEP)
    delta = -ADAM_LR * (m_hat / (jnp.sqrt(v_hat) + ADAM_EPS) + ADAM_WD * w)
    return delta, m, v


def _adam_sharded(name, own, recv, w, m, v):
    _, r, c = w.shape
    tr = min(r, 256)

    def body(own_ref, recv_ref, w_ref, m_ref, v_ref, g_ref, d_ref, nm_ref, nv_ref):
        g = own_ref[...]
        for k in range(N_DEV - 1):
            g = g + recv_ref[k].astype(F32)
        delta, nm, nv = _adam_math(w_ref[...], g, m_ref[...], v_ref[...])
        g_ref[...] = g
        d_ref[...] = delta
        nm_ref[...] = nm
        nv_ref[...] = nv

    blk = pl.BlockSpec((None, tr, c), lambda l, i: (l, i, 0))
    shp = jax.ShapeDtypeStruct(w.shape, F32)
    return pl.pallas_call(
        body, name=name, grid=(DEPTH, r // tr),
        in_specs=[blk, pl.BlockSpec((None, N_DEV - 1, tr, c), lambda l, i: (l, 0, i, 0)), blk, blk, blk],
        out_specs=[blk] * 4, out_shape=[shp] * 4,
        compiler_params=_cparams(32, ("arbitrary", "arbitrary")),
    )(own, recv, w, m, v)


def _adam_small(g, w, m, v):
    rows = g.shape[0]

    def body(g_ref, w_ref, m_ref, v_ref, d_ref, nm_ref, nv_ref):
        delta, nm, nv = _adam_math(w_ref[...], g_ref[...], m_ref[...], v_ref[...])
        d_ref[...] = delta
        nm_ref[...] = nm
        nv_ref[...] = nv

    spec = _full((rows, LANES))
    shp = jax.ShapeDtypeStruct((rows, LANES), F32)
    return pl.pallas_call(
        body, name="adam_small", grid=(1,),
        in_specs=[spec] * 4, out_specs=[spec] * 3, out_shape=[shp] * 3,
        compiler_params=_cparams(24),
    )(g, w, m, v)


SMALL = ["norm1_g", "sgu_ln_g", "sgu_ln_b", "sgu_w", "sgu_b", "conv_b", "conv_ln_g", "conv_ln_b", "norm2_g",
         "final_g"]


def _pack(arrays):
    flat = jnp.concatenate([a.reshape(-1) for a in arrays])
    pad = (-flat.shape[0]) % (N_DEV * 8 * LANES)
    return jnp.pad(flat, (0, pad)).reshape(-1, LANES)


def _unpack(packed, shapes):
    flat = packed.reshape(-1)
    out, off = [], 0
    for s in shapes:
        size = 1
        for d in s:
            size *= d
        out.append(flat[off:off + size].reshape(s))
        off += size
    return out


def kernel(x, norm1_g, w_in, sgu_ln_g, sgu_ln_b, sgu_w, sgu_b, conv_w, conv_b, conv_ln_g, conv_ln_b, w_out, norm2_g, w_ff1, w_ff2, final_g, loss_target, m_norm1_g, m_w_in, m_sgu_ln_g, m_sgu_ln_b, m_sgu_w, m_sgu_b, m_conv_w, m_conv_b, m_conv_ln_g, m_conv_ln_b, m_w_out, m_norm2_g, m_w_ff1, m_w_ff2, m_final_g, v_norm1_g, v_w_in, v_sgu_ln_g, v_sgu_ln_b, v_sgu_w, v_sgu_b, v_conv_w, v_conv_b, v_conv_ln_g, v_conv_ln_b, v_w_out, v_norm2_g, v_w_ff1, v_w_ff2, v_final_g):
    x2d = x.reshape(T, D)
    tgt = loss_target.reshape(T, D)
    cw_shard = conv_w.reshape(CONV_W, LANES)

    gnames, shards = [], []
    for l in range(DEPTH):
        for k, w in (("w_in", w_in), ("w_out", w_out), ("w_ff1", w_ff1), ("w_ff2", w_ff2)):
            gnames.append(f"{k}{l}")
            shards.append((w[l], BF16))
        if l == 0:
            gnames.insert(1, "conv_w")
            shards.insert(1, (cw_shard, F32))
    sends, recvs, lands = _gather_start(_place_own(shards))
    gidx = {k: i for i, k in enumerate(gnames)}

    def gathered(k, after):
        i = gidx[k]
        return _gather_wait("gather_wait_" + k, lands[i], sends[i], recvs[i], after)

    saved = []
    xl = x2d
    cw_full = None
    for l in range(DEPTH):
        g1 = norm1_g[l].reshape(1, D)
        g2 = norm2_g[l].reshape(1, D)
        lg, lb = sgu_ln_g[l].reshape(1, D_A), sgu_ln_b[l].reshape(1, D_A)
        bst = sgu_b[l].T
        cb = conv_b[l].reshape(1, D_B)
        clg, clb = conv_ln_g[l].reshape(1, D_B), conv_ln_b[l].reshape(1, D_B)
        wg_in = gathered(f"w_in{l}", xl)
        h, proj = _fwd_in(xl, g1, wg_in)
        if l == 0:
            cw_full = gathered("conv_w", proj).reshape(N_DEV, DEPTH, CONV_W, D_B // N_DEV).transpose(
                1, 2, 0, 3).reshape(DEPTH, CONV_W, D_B)
        mix, c, g = _mixer_fwd(proj, lg, lb, sgu_w[l], bst, cw_full[l], cb, clg, clb)
        wout = gathered(f"w_out{l}", mix).reshape(D, D)
        x1, h2 = _fwd_out(mix, wout, xl, g2)
        wg_ff1 = gathered(f"w_ff1{l}", h2)
        r = _fwd_ff1(h2, wg_ff1)
        w2 = gathered(f"w_ff2{l}", r).reshape(D_FF, D)
        x2 = _fwd_ff2(r, w2, x1)
        saved.append(dict(x=xl, h=h, proj=proj, mix=mix, c=c, g=g, x1=x1, h2=h2, r=r, wg_in=wg_in, wout=wout,
                          wg_ff1=wg_ff1, w2=w2, g1=g1, g2=g2, lg=lg, lb=lb, bst=bst, clg=clg, clb=clb))
        xl = x2

    dx, dxb, loss_part, d_final_g = _loss_bwd(xl, final_g.reshape(1, D), tgt)
    loss = lax.psum(loss_part[0, 0], ("x", "y", "c"))

    names = ["w_in", "w_out", "w_ff1", "w_ff2"]
    block = dict(w_in=(D, IN_COLS // N_DEV), w_out=(D // N_DEV, D), w_ff1=(D, D_FF // N_DEV), w_ff2=(D_FF // N_DEV, D))
    land = {k: lax.empty((DEPTH, N_DEV - 1) + block[k], BF16) for k in names}
    big32 = {k: [None] * DEPTH for k in names}
    big16 = {k: [None] * DEPTH for k in names}
    ssend = {k: [None] * DEPTH for k in names}
    srecv = {k: [None] * DEPTH for k in names}

    def send_grad(k, l, g32, g16):
        big32[k][l] = g32.reshape((N_DEV,) + block[k])
        ssend[k][l], srecv[k][l], big16[k][l], land[k] = _scatter_start(
            f"scatter_start_{k}{l}", g16.reshape((N_DEV,) + block[k]), land[k], l)

    small = {}
    d_conv_w = [None] * DEPTH
    for l in reversed(range(DEPTH)):
        s = saved[l]
        send_grad("w_ff2", l, *_wgrad("wgrad_ff2", s["r"], dxb, D, square_a=True))
        df1 = _bwd_ff2(dxb, s["w2"], s["r"], big16["w_ff2"][l])
        send_grad("w_ff1", l, *_wgrad("wgrad_ff1", s["h2"], df1, D_FF // N_DEV))
        dx1, dx1b, dg2 = _bwd_norm_matmul("bwd_ff1", df1, s["wg_ff1"], s["x1"], s["g2"], dx, big16["w_ff1"][l])
        send_grad("w_out", l, *_wgrad("wgrad_out", s["mix"], dx1b, D))
        dpa, dc, dlg, dlb, dwm, dbs, dcb, dclg, dclb = _mixer_bwd_a(
            dx1b, s["wout"], s["proj"], s["c"], s["lg"], s["lb"], sgu_w[l], s["bst"], s["clg"], s["clb"],
            big16["w_out"][l])
        dproj, d_conv_w[l] = _mixer_bwd_b(dc, s["g"], s["proj"], cw_full[l], dpa)
        send_grad("w_in", l, *_wgrad("wgrad_in", s["h"], dproj, IN_COLS // N_DEV))
        dx, dxb, dg1 = _bwd_norm_matmul("bwd_in", dproj, s["wg_in"], s["x"], s["g1"], dx1, big16["w_in"][l])
        small[l] = dict(norm1_g=dg1, sgu_ln_g=dlg, sgu_ln_b=dlb, sgu_w=dwm, sgu_b=dbs, conv_b=dcb, conv_ln_g=dclg,
                        conv_ln_b=dclb, norm2_g=dg2)
    grad_x = dx.reshape(1, T, D)

    own = _own_blocks([big32[k] for k in names])
    recv = _scatter_wait([g for k in names for g in big16[k]], [land[k] for k in names],
                         [q for k in names for q in ssend[k]], [q for k in names for q in srecv[k]], dx)
    wmv = dict(w_in=(w_in, m_w_in, v_w_in), w_out=(w_out, m_w_out, v_w_out), w_ff1=(w_ff1, m_w_ff1, v_w_ff1),
               w_ff2=(w_ff2, m_w_ff2, v_w_ff2))
    res = {}
    for idx, k in enumerate(names):
        res[k] = _adam_sharded("adam_" + k, own[idx], recv[idx], *wmv[k])

    rep = dict(norm1_g=(norm1_g, m_norm1_g, v_norm1_g), sgu_ln_g=(sgu_ln_g, m_sgu_ln_g, v_sgu_ln_g),
               sgu_ln_b=(sgu_ln_b, m_sgu_ln_b, v_sgu_ln_b), sgu_w=(sgu_w, m_sgu_w, v_sgu_w),
               sgu_b=(sgu_b, m_sgu_b, v_sgu_b), conv_b=(conv_b, m_conv_b, v_conv_b),
               conv_ln_g=(conv_ln_g, m_conv_ln_g, v_conv_ln_g), conv_ln_b=(conv_ln_b, m_conv_ln_b, v_conv_ln_b),
               norm2_g=(norm2_g, m_norm2_g, v_norm2_g), final_g=(final_g, m_final_g, v_final_g))
    parts = []
    for k in SMALL:
        if k == "final_g":
            parts.append(d_final_g.reshape(rep[k][0].shape))
        else:
            parts.append(jnp.stack([small[l][k].reshape(rep[k][0].shape[1:]) for l in range(DEPTH)]))
    parts.append(jnp.stack(d_conv_w))
    shapes = [p.shape for p in parts]
    summed = _unpack(_all_reduce_small(_pack(parts)), shapes)
    me = _flat(*_coords())
    g_conv_w = lax.dynamic_slice_in_dim(summed[-1], me * (D_B // N_DEV), D_B // N_DEV, axis=2)
    g_small = summed[:-1] + [g_conv_w]
    keys = SMALL + ["conv_w"]
    rep["conv_w"] = (conv_w, m_conv_w, v_conv_w)
    shapes = [a.shape for a in g_small]
    packed = [_pack(g_small)] + [_pack([rep[k][i] for k in keys]) for i in range(3)]
    d_s, nm_s, nv_s = (_unpack(a, shapes) for a in _adam_small(*packed))
    for i, k in enumerate(keys):
        res[k] = (g_small[i], d_s[i], nm_s[i], nv_s[i])

    order = ["norm1_g", "w_in", "sgu_ln_g", "sgu_ln_b", "sgu_w", "sgu_b", "conv_w", "conv_b", "conv_ln_g",
             "conv_ln_b", "w_out", "norm2_g", "w_ff1", "w_ff2", "final_g"]
    return (loss, grad_x, *[res[k][0] for k in order], *[res[k][1] for k in order],
            *[res[k][2] for k in order], *[res[k][3] for k in order])
```

```python
import functools

import jax
import jax.numpy as jnp
from jax import lax
from jax.experimental import pallas as pl
from jax.experimental.pallas import tpu as pltpu

F32 = jnp.float32
BF16 = jnp.bfloat16

N_DEV = 8
DEPTH = 2
T = 4096
D = 1024
D_A = 512
D_B = 512
CHUNK = 128
H_A = 4
H_B = 4
CONV_W = 31
HALO = 32
D_FF = 4096
IN_COLS = 2048
EPS = 1e-6

ADAM_LR = 0.001
ADAM_B1 = 0.9
ADAM_B2 = 0.999
ADAM_EPS = 1e-08
ADAM_WD = 0.01
ADAM_STEP = 10

TM = 512
TM_MIX = 256
RB = 64
LANES = 128
MIB = 1024 * 1024
SCOPED_VMEM_MIB = 60

SQRT_HALF = 0.7071067811865476
INV_SQRT_2PI = 0.3989422804014327

MESH_ID = pl.DeviceIdType.MESH
ANY = pl.BlockSpec(memory_space=pl.ANY)
HBM = pl.BlockSpec(memory_space=pltpu.HBM)
SEM = pl.BlockSpec(memory_space=pltpu.SEMAPHORE)
EFFECT = pltpu.SideEffectType.DATAFLOW_SIDE_EFFECTING


def _cparams(vmem_mib, sem=("arbitrary",)):
    assert vmem_mib <= SCOPED_VMEM_MIB
    return pltpu.CompilerParams(dimension_semantics=sem, vmem_limit_bytes=SCOPED_VMEM_MIB * MIB)


def _full(shape):
    return pl.BlockSpec(shape, lambda *_: (0,) * len(shape))


def _rows(tm, cols):
    return pl.BlockSpec((tm, cols), lambda i: (i, 0))


def _gelu(x):
    cdf = 0.5 * (1.0 + lax.erf(x * SQRT_HALF))
    return x * cdf, cdf


def _gelu_grad(x, cdf):
    return cdf + x * (INV_SQRT_2PI * jnp.exp(-0.5 * x * x))


def _sigmoid(x):
    return 1.0 / (1.0 + jnp.exp(-x))


def _ln(x):
    mu = jnp.mean(x, axis=-1, keepdims=True)
    xc = x - mu
    rstd = lax.rsqrt(jnp.mean(xc * xc, axis=-1, keepdims=True) + EPS)
    return xc * rstd, rstd


def _ln_bwd(dyh, xhat, rstd):
    return rstd * (dyh - jnp.mean(dyh, axis=-1, keepdims=True) - xhat * jnp.mean(dyh * xhat, axis=-1, keepdims=True))


def _rms(x):
    return lax.rsqrt(jnp.mean(x * x, axis=-1, keepdims=True) + EPS)


def _rms_bwd(dh, x, r, g):
    n = x * r
    dn = dh * g
    dx = r * (dn - n * jnp.mean(dn * n, axis=-1, keepdims=True))
    return dx, jnp.sum(dh * n, axis=0, keepdims=True)


def _dot(a, b):
    return jnp.dot(a, b, preferred_element_type=F32)


def _dot_nt(a, b):
    return lax.dot_general(a, b, (((1,), (1,)), ((), ())), preferred_element_type=F32)


def _dot_tn(a, b):
    return lax.dot_general(a, b, (((0,), (0,)), ((), ())), preferred_element_type=F32)


def _tril_mask():
    r = lax.broadcasted_iota(jnp.int32, (CHUNK, CHUNK), 0)
    c = lax.broadcasted_iota(jnp.int32, (CHUNK, CHUNK), 1)
    return r >= c


def _fwd_in(x, g1, wg):
    bn = wg.shape[2]

    def body(x_ref, g_ref, w_ref, h_ref, p_ref):
        xv = x_ref[...]
        h = (xv * _rms(xv) * g_ref[...]).astype(BF16)
        h_ref[...] = h
        for j in range(N_DEV):
            p_ref[:, j * bn:(j + 1) * bn] = _dot(h, w_ref[j])

    return pl.pallas_call(
        body, name="fwd_in", grid=(T // TM,),
        in_specs=[_rows(TM, D), _full((1, D)), _full(wg.shape)],
        out_specs=[_rows(TM, D), _rows(TM, IN_COLS)],
        out_shape=[jax.ShapeDtypeStruct((T, D), BF16), jax.ShapeDtypeStruct((T, IN_COLS), F32)],
        compiler_params=_cparams(32),
    )(x, g1, wg)


def _build_shifts(sh_ref):
    rows = sh_ref.shape[1]
    for p in range(1, 8):
        sh_ref[p, 0:rows - 8, :] = sh_ref[0, p:p + rows - 8, :]


def _shifted(sh_ref, row, cols):
    return sh_ref[row % 8, row - row % 8:row - row % 8 + RB, cols]


def _conv_taps(sh_ref, w_ref, cols, base, first_tap_row, step):
    acc = jnp.zeros((RB, LANES), F32)
    for k in range(CONV_W):
        acc = acc + _shifted(sh_ref, base + first_tap_row + step * k, cols) * w_ref[k:k + 1, cols]
    return acc


def _mixer_fwd(proj, lg, lb, wm, bst, cw, cb, clg, clb):
    tm = TM_MIX
    hb = tm // HALO

    def body(p_ref, ph_ref, lg_ref, lb_ref, wm_ref, bs_ref, cw_ref, cb_ref, clg_ref, clb_ref,
             mix_ref, c_ref, g_ref, gbuf):
        i = pl.program_id(0)
        u, _ = _gelu(p_ref[:, 0:D_A])
        vg, _ = _gelu(p_ref[:, D_A:2 * D_A])
        xhat, _ = _ln(vg)
        v = (xhat * lg_ref[...] + lb_ref[...]).astype(BF16)
        mask = _tril_mask()
        for h in range(H_A):
            hc = slice(h * CHUNK, (h + 1) * CHUNK)
            wmh = jnp.where(mask, wm_ref[h], 0.0).astype(BF16)
            for c in range(tm // CHUNK):
                rc = slice(c * CHUNK, (c + 1) * CHUNK)
                mixed = _dot(wmh, v[rc, hc]) + bs_ref[:, h:h + 1]
                mix_ref[rc, hc] = (u[rc, hc] * mixed).astype(BF16)

        g = p_ref[:, 2 * D_A:2 * D_A + D_B] * _sigmoid(p_ref[:, 2 * D_A + D_B:IN_COLS])
        g_ref[...] = g
        gh = ph_ref[:, 0:D_B] * _sigmoid(ph_ref[:, D_B:2 * D_B])
        gbuf[0, 0:HALO, :] = jnp.where(i > 0, gh, 0.0)
        gbuf[0, HALO:HALO + tm, :] = g
        _build_shifts(gbuf)
        for q in range(H_B):
            cols = slice(q * LANES, (q + 1) * LANES)
            for rb in range(tm // RB):
                acc = _conv_taps(gbuf, cw_ref, cols, rb * RB, HALO - (CONV_W - 1), 1)
                c_ref[rb * RB:(rb + 1) * RB, cols] = acc + cb_ref[:, cols]
        for q in range(H_B):
            cols = slice(q * LANES, (q + 1) * LANES)
            chat, _ = _ln(c_ref[:, cols])
            z = chat * clg_ref[:, cols] + clb_ref[:, cols]
            mix_ref[:, D_A + q * LANES:D_A + (q + 1) * LANES] = (z * _sigmoid(z)).astype(BF16)

    vec = _full((1, D_A))
    return pl.pallas_call(
        body, name="mixer_fwd", grid=(T // tm,),
        in_specs=[_rows(tm, IN_COLS),
                  pl.BlockSpec((HALO, 2 * D_B), lambda i: (jnp.maximum(i * hb - 1, 0), 1)),
                  vec, vec, _full((H_A, CHUNK, CHUNK)), _full((CHUNK, H_A)),
                  _full((CONV_W, D_B)), vec, vec, vec],
        out_specs=[_rows(tm, D), _rows(tm, D_B), _rows(tm, D_B)],
        out_shape=[jax.ShapeDtypeStruct((T, D), BF16), jax.ShapeDtypeStruct((T, D_B), F32),
                   jax.ShapeDtypeStruct((T, D_B), F32)],
        scratch_shapes=[pltpu.VMEM((8, HALO + tm, D_B), F32)],
        compiler_params=_cparams(32),
    )(proj, proj, lg, lb, wm, bst, cw, cb, clg, clb)


def _fwd_out(mix, wout, x, g2):
    def body(m_ref, w_ref, x_ref, g_ref, x1_ref, h2_ref):
        x1 = x_ref[...] + _dot(m_ref[...], w_ref[...])
        x1_ref[...] = x1
        h2_ref[...] = (x1 * _rms(x1) * g_ref[...]).astype(BF16)

    return pl.pallas_call(
        body, name="fwd_out", grid=(T // TM,),
        in_specs=[_rows(TM, D), _full((D, D)), _rows(TM, D), _full((1, D))],
        out_specs=[_rows(TM, D), _rows(TM, D)],
        out_shape=[jax.ShapeDtypeStruct((T, D), F32), jax.ShapeDtypeStruct((T, D), BF16)],
        compiler_params=_cparams(32),
    )(mix, wout, x, g2)


def _fwd_ff1(h2, wg):
    bn = wg.shape[2]

    def body(h_ref, w_ref, r_ref):
        h = h_ref[...]
        for j in range(N_DEV):
            r_ref[:, j * bn:(j + 1) * bn] = jnp.maximum(_dot(h, w_ref[j]), 0.0).astype(BF16)

    return pl.pallas_call(
        body, name="fwd_ff1", grid=(T // TM,),
        in_specs=[_rows(TM, D), _full(wg.shape)],
        out_specs=_rows(TM, D_FF),
        out_shape=jax.ShapeDtypeStruct((T, D_FF), BF16),
        compiler_params=_cparams(48),
    )(h2, wg)


def _fwd_ff2(r, w2, x1):
    def body(r_ref, w_ref, x_ref, o_ref):
        rv = r_ref[...]
        o_ref[...] = x_ref[...] + _dot(rv * rv, w_ref[...])

    return pl.pallas_call(
        body, name="fwd_ff2", grid=(T // TM,),
        in_specs=[_rows(TM, D_FF), _full((D_FF, D)), _rows(TM, D)],
        out_specs=_rows(TM, D),
        out_shape=jax.ShapeDtypeStruct((T, D), F32),
        compiler_params=_cparams(48),
    )(r, w2, x1)


def _loss_bwd(xf, gf, tgt):
    def body(x_ref, g_ref, t_ref, dx_ref, dxb_ref, loss_ref, dg_ref):
        i = pl.program_id(0)

        @pl.when(i == 0)
        def _():
            loss_ref[...] = jnp.zeros(loss_ref.shape, F32)
            dg_ref[...] = jnp.zeros(dg_ref.shape, F32)

        xv = x_ref[...]
        r = _rms(xv)
        diff = xv * r * g_ref[...] - t_ref[...]
        loss_ref[...] += 0.5 * jnp.sum(jnp.mean(diff * diff, axis=-1, keepdims=True), axis=0, keepdims=True)
        dx, dg = _rms_bwd(diff * (1.0 / D), xv, r, g_ref[...])
        dx_ref[...] = dx
        dxb_ref[...] = dx.astype(BF16)
        dg_ref[...] += dg

    return pl.pallas_call(
        body, name="loss_bwd", grid=(T // TM,),
        in_specs=[_rows(TM, D), _full((1, D)), _rows(TM, D)],
        out_specs=[_rows(TM, D), _rows(TM, D), _full((8, LANES)), _full((1, D))],
        out_shape=[jax.ShapeDtypeStruct((T, D), F32), jax.ShapeDtypeStruct((T, D), BF16),
                   jax.ShapeDtypeStruct((8, LANES), F32), jax.ShapeDtypeStruct((1, D), F32)],
        compiler_params=_cparams(32),
    )(xf, gf, tgt)


def _bwd_ff2(dxb, w2, r, dep):
    kb = 1024

    def body(d_ref, w_ref, r_ref, dep_ref, o_ref):
        d = d_ref[...]
        for j in range(D_FF // kb):
            cols = slice(j * kb, (j + 1) * kb)
            df = _dot_nt(d, w_ref[cols, :])
            o_ref[:, cols] = (2.0 * r_ref[:, cols].astype(F32) * df).astype(BF16)

    return pl.pallas_call(
        body, name="bwd_ff2", grid=(T // TM,),
        in_specs=[_rows(TM, D), _full((D_FF, D)), _rows(TM, D_FF), ANY],
        out_specs=_rows(TM, D_FF),
        out_shape=jax.ShapeDtypeStruct((T, D_FF), BF16),
        compiler_params=_cparams(48),
    )(dxb, w2, r, dep)


def _bwd_norm_matmul(name, dy, wg, x, g, dres, dep):
    bn = wg.shape[2]
    ncol = dy.shape[1]

    def body(dy_ref, w_ref, x_ref, g_ref, dres_ref, dep_ref, dx_ref, dxb_ref, dg_ref):
        i = pl.program_id(0)

        @pl.when(i == 0)
        def _():
            dg_ref[...] = jnp.zeros(dg_ref.shape, F32)

        dh = jnp.zeros((TM, D), F32)
        for j in range(N_DEV):
            dh = dh + _dot_nt(dy_ref[:, j * bn:(j + 1) * bn], w_ref[j])
        xv = x_ref[...]
        dxn, dg = _rms_bwd(dh, xv, _rms(xv), g_ref[...])
        dx = dres_ref[...] + dxn
        dx_ref[...] = dx
        dxb_ref[...] = dx.astype(BF16)
        dg_ref[...] += dg

    return pl.pallas_call(
        body, name=name, grid=(T // TM,),
        in_specs=[_rows(TM, ncol), _full(wg.shape), _rows(TM, D), _full((1, D)), _rows(TM, D), ANY],
        out_specs=[_rows(TM, D), _rows(TM, D), _full((1, D))],
        out_shape=[jax.ShapeDtypeStruct((T, D), F32), jax.ShapeDtypeStruct((T, D), BF16),
                   jax.ShapeDtypeStruct((1, D), F32)],
        compiler_params=_cparams(48),
    )(dy, wg, x, g, dres, dep)


def _mixer_bwd_a(dxb, wout, proj, c, lg, lb, wm, bst, clg, clb, dep):
    tm = TM_MIX
    n_tiles = T // tm

    def body(dx_ref, wo_ref, p_ref, c_ref, lg_ref, lb_ref, wm_ref, bs_ref, clg_ref, clb_ref, dep_ref,
             dpa_ref, dc_ref, dlg_ref, dlb_ref, dwm_ref, dbs_ref, dcb_ref, dclg_ref, dclb_ref,
             dv_buf, db_acc):
        i = pl.program_id(0)

        @pl.when(i == 0)
        def _():
            for ref in (dlg_ref, dlb_ref, dwm_ref, dbs_ref, dcb_ref, dclg_ref, dclb_ref, db_acc):
                ref[...] = jnp.zeros(ref.shape, F32)

        dmix = _dot_nt(dx_ref[...], wo_ref[...])
        ua = p_ref[:, 0:D_A]
        va = p_ref[:, D_A:2 * D_A]
        u, cdf_u = _gelu(ua)
        vg, cdf_v = _gelu(va)
        xhat, rstd = _ln(vg)
        v = (xhat * lg_ref[...] + lb_ref[...]).astype(BF16)
        mask = _tril_mask()
        for h in range(H_A):
            hc = slice(h * CHUNK, (h + 1) * CHUNK)
            wmh = jnp.where(mask, wm_ref[h], 0.0).astype(BF16)
            for cidx in range(tm // CHUNK):
                rc = slice(cidx * CHUNK, (cidx + 1) * CHUNK)
                vb = v[rc, hc]
                mixed = _dot(wmh, vb) + bs_ref[:, h:h + 1]
                da = dmix[rc, hc]
                dpa_ref[rc, hc] = (da * mixed * _gelu_grad(ua[rc, hc], cdf_u[rc, hc])).astype(BF16)
                dmixed = da * u[rc, hc]
                dmb = dmixed.astype(BF16)
                dv_buf[rc, hc] = _dot_tn(wmh, dmb)
                dwm_ref[h] += _dot_nt(dmb, vb)
                db_acc[:, hc] += dmixed
        dv = dv_buf[...]
        dlb_ref[...] += jnp.sum(dv, axis=0, keepdims=True)
        dlg_ref[...] += jnp.sum(dv * xhat, axis=0, keepdims=True)
        dvg = _ln_bwd(dv * lg_ref[...], xhat, rstd)
        dpa_ref[:, D_A:2 * D_A] = (dvg * _gelu_grad(va, cdf_v)).astype(BF16)

        for q in range(H_B):
            cols = slice(q * LANES, (q + 1) * LANES)
            chat, crstd = _ln(c_ref[:, cols])
            z = chat * clg_ref[:, cols] + clb_ref[:, cols]
            sg = _sigmoid(z)
            dz = dmix[:, D_A + q * LANES:D_A + (q + 1) * LANES] * (sg * (1.0 + z * (1.0 - sg)))
            dclb_ref[:, cols] += jnp.sum(dz, axis=0, keepdims=True)
            dclg_ref[:, cols] += jnp.sum(dz * chat, axis=0, keepdims=True)
            dc = _ln_bwd(dz * clg_ref[:, cols], chat, crstd)
            dc_ref[:, cols] = dc
            dcb_ref[:, cols] += jnp.sum(dc, axis=0, keepdims=True)

        @pl.when(i == n_tiles - 1)
        def _():
            for h in range(H_A):
                hc = slice(h * CHUNK, (h + 1) * CHUNK)
                dwm_ref[h] = jnp.where(mask, dwm_ref[h], 0.0)
                dbs_ref[h:h + 1, :] = jnp.sum(db_acc[:, hc].T, axis=0, keepdims=True)

    vec = _full((1, D_A))
    vshape = jax.ShapeDtypeStruct((1, D_A), F32)
    return pl.pallas_call(
        body, name="mixer_bwd_a", grid=(n_tiles,),
        in_specs=[_rows(tm, D), _full((D, D)), _rows(tm, IN_COLS), _rows(tm, D_B), vec, vec,
                  _full((H_A, CHUNK, CHUNK)), _full((CHUNK, H_A)), vec, vec, ANY],
        out_specs=[_rows(tm, 2 * D_A), _rows(tm, D_B), vec, vec, _full((H_A, CHUNK, CHUNK)),
                   _full((H_A, CHUNK)), vec, vec, vec],
        out_shape=[jax.ShapeDtypeStruct((T, 2 * D_A), BF16), jax.ShapeDtypeStruct((T, D_B), F32), vshape, vshape,
                   jax.ShapeDtypeStruct((H_A, CHUNK, CHUNK), F32), jax.ShapeDtypeStruct((H_A, CHUNK), F32),
                   vshape, vshape, vshape],
        scratch_shapes=[pltpu.VMEM((tm, D_A), F32), pltpu.VMEM((CHUNK, D_A), F32)],
        compiler_params=_cparams(32),
    )(dxb, wout, proj, c, lg, lb, wm, bst, clg, clb, dep)


def _mixer_bwd_b(dc, g, proj, cw, dpa):
    tm = TM_MIX
    n_tiles = T // tm
    hb = tm // HALO

    def body(dc_ref, dch_ref, g_ref, gh_ref, p_ref, cw_ref, dpa_ref, dp_ref, dcw_ref, dcbuf, gbuf, dwacc):
        i = pl.program_id(0)

        @pl.when(i == 0)
        def _():
            dwacc[...] = jnp.zeros(dwacc.shape, F32)

        dcbuf[0, 0:tm, :] = dc_ref[...]
        dcbuf[0, tm:tm + HALO, :] = jnp.where(i < n_tiles - 1, dch_ref[...], 0.0)
        gbuf[0, 0:HALO, :] = jnp.where(i > 0, gh_ref[...], 0.0)
        gbuf[0, HALO:HALO + tm, :] = g_ref[...]
        _build_shifts(dcbuf)
        _build_shifts(gbuf)
        dp_ref[:, 0:2 * D_A] = dpa_ref[...]
        for q in range(H_B):
            cols = slice(q * LANES, (q + 1) * LANES)
            for rb in range(tm // RB):
                rows = slice(rb * RB, (rb + 1) * RB)
                dg = _conv_taps(dcbuf, cw_ref, cols, rb * RB, CONV_W - 1, -1)
                val = p_ref[rows, cols]
                sg = _sigmoid(p_ref[rows, D_B + q * LANES:D_B + (q + 1) * LANES])
                dp_ref[rows, 2 * D_A + q * LANES:2 * D_A + (q + 1) * LANES] = (dg * sg).astype(BF16)
                dp_ref[rows, 2 * D_A + D_B + q * LANES:2 * D_A + D_B + (q + 1) * LANES] = (
                    dg * val * sg * (1.0 - sg)).astype(BF16)
                dcv = dc_ref[rows, cols]
                for k in range(CONV_W):
                    prod = dcv * _shifted(gbuf, rb * RB + HALO - (CONV_W - 1) + k, cols)
                    dwacc[k * 8:(k + 1) * 8, cols] += jnp.sum(prod.reshape(RB // 8, 8, LANES), axis=0)

        @pl.when(i == n_tiles - 1)
        def _():
            for k in range(CONV_W):
                dcw_ref[k:k + 1, :] = jnp.sum(dwacc[k * 8:(k + 1) * 8, :], axis=0, keepdims=True)

    return pl.pallas_call(
        body, name="mixer_bwd_b", grid=(n_tiles,),
        in_specs=[_rows(tm, D_B),
                  pl.BlockSpec((HALO, D_B), lambda i: (jnp.minimum((i + 1) * hb, T // HALO - 1), 0)),
                  _rows(tm, D_B),
                  pl.BlockSpec((HALO, D_B), lambda i: (jnp.maximum(i * hb - 1, 0), 0)),
                  pl.BlockSpec((tm, 2 * D_B), lambda i: (i, 1)),
                  _full((CONV_W, D_B)), _rows(tm, 2 * D_A)],
        out_specs=[_rows(tm, IN_COLS), _full((CONV_W, D_B))],
        out_shape=[jax.ShapeDtypeStruct((T, IN_COLS), BF16), jax.ShapeDtypeStruct((CONV_W, D_B), F32)],
        scratch_shapes=[pltpu.VMEM((8, tm + HALO, D_B), F32), pltpu.VMEM((8, HALO + tm, D_B), F32),
                        pltpu.VMEM((CONV_W * 8, D_B), F32)],
        compiler_params=_cparams(32),
    )(dc, dc, g, g, proj, cw, dpa)


def _wgrad(name, a, g, bn, square_a=False):
    k = a.shape[1]
    n = g.shape[1]
    tk = min(k, 1024)
    tn = min(n, max(bn, 1024))
    nsub = tn // bn
    tt = TM
    nt = T // tt

    def body(a_ref, g_ref, o_ref, ob_ref):
        t = pl.program_id(2)

        @pl.when(t == 0)
        def _():
            o_ref[...] = jnp.zeros(o_ref.shape, F32)

        av = a_ref[...]
        if square_a:
            av = av * av
        for s in range(nsub):
            o_ref[s] += _dot_tn(av, g_ref[:, s * bn:(s + 1) * bn])

        @pl.when(t == nt - 1)
        def _():
            ob_ref[...] = o_ref[...].astype(BF16)

    ospec = pl.BlockSpec((nsub, tk, bn), lambda ki, ni, ti: (ni, ki, 0))
    return pl.pallas_call(
        body, name=name, grid=(k // tk, n // tn, nt),
        in_specs=[pl.BlockSpec((tt, tk), lambda ki, ni, ti: (ti, ki)),
                  pl.BlockSpec((tt, tn), lambda ki, ni, ti: (ti, ni))],
        out_specs=[ospec, ospec],
        out_shape=[jax.ShapeDtypeStruct((n // bn, k, bn), F32), jax.ShapeDtypeStruct((n // bn, k, bn), BF16)],
        compiler_params=_cparams(40, ("arbitrary", "arbitrary", "arbitrary")),
    )(a, g)


def _coords():
    return lax.axis_index("x"), lax.axis_index("y"), lax.axis_index("c")


def _flat(x, y, c):
    return 4 * x + 2 * y + c


def _peer(k):
    x, y, c = _coords()
    return (x ^ ((k >> 2) & 1), y ^ ((k >> 1) & 1), c ^ (k & 1))


def _hbm(a):
    return pltpu.with_memory_space_constraint(a, pltpu.HBM)


def _hbm_like(a):
    return pltpu.HBM(a.shape, a.dtype)


def _peer_sems():
    return pltpu.SemaphoreType.DMA((N_DEV - 1,))


def _place_own(shards):
    n = len(shards)

    def body(*refs):
        ins, outs = refs[:n], refs[n:2 * n]
        stage_in, stage_out = refs[2 * n:3 * n], refs[3 * n:4 * n]
        in_sems, out_sems = refs[4 * n], refs[4 * n + 1]
        me = _flat(*_coords())
        loads = [pltpu.make_async_copy(ins[a], stage_in[a], in_sems.at[a]) for a in range(n)]
        stores = [pltpu.make_async_copy(stage_out[a], outs[a].at[me], out_sems.at[a]) for a in range(n)]
        for cp in loads:
            cp.start()
        for a in range(n):
            loads[a].wait()
            stage_out[a][...] = stage_in[a][...].astype(stage_out[a].dtype)
            stores[a].start()
        for cp in stores:
            cp.wait()

    return pl.pallas_call(
        body, name="place_own", in_specs=[ANY] * n, out_specs=[ANY] * n,
        out_shape=[jax.ShapeDtypeStruct((N_DEV,) + s.shape, dt) for s, dt in shards],
        scratch_shapes=[*[pltpu.VMEM(s.shape, s.dtype) for s, _ in shards],
                        *[pltpu.VMEM(s.shape, dt) for s, dt in shards],
                        pltpu.SemaphoreType.DMA((n,)), pltpu.SemaphoreType.DMA((n,))],
        compiler_params=pltpu.CompilerParams(vmem_limit_bytes=40 * MIB),
    )(*[s for s, _ in shards])


SIBLING = 1
CHIP_PEERS = (2, 4, 6)
FIRST_PEERS = (SIBLING,) + CHIP_PEERS


def _gather_start(lands):
    n = len(lands)

    def body(*refs):
        lnd, send, recv = refs[:n], refs[n:2 * n], refs[2 * n:3 * n]
        me = _flat(*_coords())
        for a in range(n):
            for j, k in enumerate(FIRST_PEERS):
                pltpu.make_async_remote_copy(
                    src_ref=lnd[a].at[me], dst_ref=lnd[a].at[me], send_sem=send[a].at[j],
                    recv_sem=recv[a].at[j], device_id=_peer(k), device_id_type=MESH_ID).start()

    sems = pltpu.SemaphoreType.DMA((len(FIRST_PEERS),))
    outs = pl.pallas_call(
        body, name="gather_start",
        out_shape=(*[sems] * (2 * n), *[_hbm_like(l) for l in lands]),
        in_specs=[HBM] * n, out_specs=(*[SEM] * (2 * n), *[HBM] * n),
        input_output_aliases={i: 2 * n + i for i in range(n)},
        compiler_params=pltpu.CompilerParams(has_side_effects=EFFECT),
    )(*[_hbm(l) for l in lands])
    return outs[:n], outs[n:2 * n], outs[2 * n:]


def _gather_pass_on(name, land, send, recv, after):
    def body(l_ref, send_ref, recv_ref, after_ref, send2, recv2, l_out):
        sibling = _peer(SIBLING)
        for j, k in enumerate(CHIP_PEERS):
            cp = pltpu.make_async_remote_copy(
                src_ref=l_ref.at[0], dst_ref=l_ref.at[0], send_sem=send_ref.at[1 + j], recv_sem=recv_ref.at[1 + j],
                device_id=_peer(k), device_id_type=MESH_ID)
            cp.wait_send()
            cp.wait_recv()
            blk = _flat(*_peer(k))
            pltpu.make_async_remote_copy(
                src_ref=l_ref.at[blk], dst_ref=l_ref.at[blk], send_sem=send2.at[j], recv_sem=recv2.at[j],
                device_id=sibling, device_id_type=MESH_ID).start()

    sems = pltpu.SemaphoreType.DMA((len(CHIP_PEERS),))
    return pl.pallas_call(
        body, name=name, out_shape=(sems, sems, _hbm_like(land)),
        in_specs=(HBM, SEM, SEM, ANY), out_specs=(SEM, SEM, HBM), input_output_aliases={0: 2},
        compiler_params=pltpu.CompilerParams(has_side_effects=EFFECT),
    )(land, send, recv, after)


def _gather_wait(name, land, send, recv, send2, recv2, after):
    def body(l_ref, send_ref, recv_ref, send2_ref, recv2_ref, after_ref, l_out):
        sibling = _peer(SIBLING)
        own = pltpu.make_async_remote_copy(
            src_ref=l_ref.at[0], dst_ref=l_ref.at[0], send_sem=send_ref.at[0], recv_sem=recv_ref.at[0],
            device_id=sibling, device_id_type=MESH_ID)
        own.wait_send()
        own.wait_recv()
        for j in range(len(CHIP_PEERS)):
            cp = pltpu.make_async_remote_copy(
                src_ref=l_ref.at[0], dst_ref=l_ref.at[0], send_sem=send2_ref.at[j], recv_sem=recv2_ref.at[j],
                device_id=sibling, device_id_type=MESH_ID)
            cp.wait_send()
            cp.wait_recv()

    return pl.pallas_call(
        body, name=name, out_shape=_hbm_like(land),
        in_specs=(HBM, SEM, SEM, SEM, SEM, ANY), out_specs=HBM, input_output_aliases={0: 0},
        compiler_params=pltpu.CompilerParams(has_side_effects=EFFECT),
    )(land, send, recv, send2, recv2, after)


def _scatter_start(name, g16, land, layer):
    def body(g_ref, land_ref, send, recv, g_out, land_out):
        for k in range(1, N_DEV):
            to = _peer(k)
            pltpu.make_async_remote_copy(
                src_ref=g_ref.at[_flat(*to)], dst_ref=land_ref.at[layer, k - 1],
                send_sem=send.at[k - 1], recv_sem=recv.at[k - 1], device_id=to, device_id_type=MESH_ID).start()

    return pl.pallas_call(
        body, name=name,
        out_shape=(_peer_sems(), _peer_sems(), _hbm_like(g16), _hbm_like(land)),
        in_specs=[HBM, HBM], out_specs=(SEM, SEM, HBM, HBM), input_output_aliases={0: 2, 1: 3},
        compiler_params=pltpu.CompilerParams(has_side_effects=EFFECT),
    )(_hbm(g16), _hbm(land))


def _scatter_wait(grads, lands, sends, recvs, after):
    n, nw = len(grads), len(lands)

    def body(*refs):
        g, lnd = refs[:n], refs[n:n + nw]
        send, recv = refs[n + nw:2 * n + nw], refs[2 * n + nw:3 * n + nw]
        for a in range(n):
            for k in range(1, N_DEV):
                cp = pltpu.make_async_remote_copy(
                    src_ref=g[a].at[0], dst_ref=lnd[a // DEPTH].at[a % DEPTH, 0],
                    send_sem=send[a].at[k - 1], recv_sem=recv[a].at[k - 1],
                    device_id=_peer(k), device_id_type=MESH_ID)
                cp.wait_send()
                cp.wait_recv()

    outs = pl.pallas_call(
        body, name="scatter_wait",
        out_shape=(*[_hbm_like(g) for g in grads], *[_hbm_like(l) for l in lands]),
        in_specs=(*[HBM] * (n + nw), *[SEM] * (2 * n), ANY), out_specs=[HBM] * (n + nw),
        input_output_aliases={i: i for i in range(n + nw)},
        compiler_params=pltpu.CompilerParams(has_side_effects=EFFECT),
    )(*grads, *lands, *sends, *recvs, after)
    return outs[n:]


def _own_blocks(grads_f32):
    nw = len(grads_f32)
    flat32 = [g for w in grads_f32 for g in w]
    n = len(flat32)

    def body(*refs):
        g32, own, stage = refs[:n], refs[n:n + nw], refs[n + nw:2 * n + nw]
        in_sems, out_sems = refs[2 * n + nw], refs[2 * n + nw + 1]
        me = _flat(*_coords())
        loads = [pltpu.make_async_copy(g32[a].at[me], stage[a], in_sems.at[a]) for a in range(n)]
        stores = [pltpu.make_async_copy(stage[a], own[a // DEPTH].at[a % DEPTH], out_sems.at[a]) for a in range(n)]
        for cp in loads:
            cp.start()
        for a in range(n):
            loads[a].wait()
            stores[a].start()
        for cp in stores:
            cp.wait()

    return pl.pallas_call(
        body, name="own_blocks", in_specs=[ANY] * n, out_specs=[ANY] * nw,
        out_shape=[jax.ShapeDtypeStruct((DEPTH,) + w[0].shape[1:], F32) for w in grads_f32],
        scratch_shapes=[*[pltpu.VMEM(g.shape[1:], F32) for g in flat32],
                        pltpu.SemaphoreType.DMA((n,)), pltpu.SemaphoreType.DMA((n,))],
        compiler_params=pltpu.CompilerParams(vmem_limit_bytes=24 * MIB),
    )(*flat32)


def _all_reduce_small(part):
    rows = part.shape[0]
    br = rows // N_DEV
    assert br * N_DEV == rows and br % 8 == 0

    def body(p_ref, o_ref, slots, send1, recv1, send2, recv2):
        me = _flat(*_coords())

        def block(ref, d):
            return ref.at[pl.ds(pl.multiple_of(d * br, 8), br), :]

        slots[me] = p_ref[pl.ds(pl.multiple_of(me * br, 8), br), :]
        scatter, gather = [], []
        for k in range(1, N_DEV):
            to = _peer(k)
            scatter.append(pltpu.make_async_remote_copy(
                src_ref=block(p_ref, _flat(*to)), dst_ref=slots.at[me],
                send_sem=send1.at[k - 1], recv_sem=recv1.at[k - 1], device_id=to, device_id_type=MESH_ID))
            gather.append(pltpu.make_async_remote_copy(
                src_ref=block(o_ref, me), dst_ref=block(o_ref, me),
                send_sem=send2.at[k - 1], recv_sem=recv2.at[k - 1], device_id=to, device_id_type=MESH_ID))
        for cp in scatter:
            cp.start()
        for cp in scatter:
            cp.wait()
        acc = slots[0]
        for d in range(1, N_DEV):
            acc = acc + slots[d]
        o_ref[pl.ds(pl.multiple_of(me * br, 8), br), :] = acc
        for cp in gather:
            cp.start()
        for cp in gather:
            cp.wait()

    return pl.pallas_call(
        body, name="all_reduce_small",
        in_specs=[pl.BlockSpec(memory_space=pltpu.VMEM)], out_specs=pl.BlockSpec(memory_space=pltpu.VMEM),
        out_shape=jax.ShapeDtypeStruct(part.shape, F32),
        scratch_shapes=[pltpu.VMEM((N_DEV, br, LANES), F32)] + [pltpu.SemaphoreType.DMA((N_DEV - 1,))] * 4,
        compiler_params=pltpu.CompilerParams(vmem_limit_bytes=24 * MIB),
    )(part)


def _adam_math(w, g, m, v):
    m = ADAM_B1 * m + (1.0 - ADAM_B1) * g
    v = ADAM_B2 * v + (1.0 - ADAM_B2) * (g * g)
    m_hat = m / (1.0 - ADAM_B1 ** ADAM_STEP)
    v_hat = v / (1.0 - ADAM_B2 ** ADAM_STEP)
    delta = -ADAM_LR * (m_hat / (jnp.sqrt(v_hat) + ADAM_EPS) + ADAM_WD * w)
    return delta, m, v


def _adam_sharded(name, own, recv, w, m, v):
    _, r, c = w.shape
    tr = min(r, 256)

    def body(own_ref, recv_ref, w_ref, m_ref, v_ref, g_ref, d_ref, nm_ref, nv_ref):
        g = own_ref[...]
        for k in range(N_DEV - 1):
            g = g + recv_ref[k].astype(F32)
        delta, nm, nv = _adam_math(w_ref[...], g, m_ref[...], v_ref[...])
        g_ref[...] = g
        d_ref[...] = delta
        nm_ref[...] = nm
        nv_ref[...] = nv

    blk = pl.BlockSpec((None, tr, c), lambda l, i: (l, i, 0))
    shp = jax.ShapeDtypeStruct(w.shape, F32)
    return pl.pallas_call(
        body, name=name, grid=(DEPTH, r // tr),
        in_specs=[blk, pl.BlockSpec((None, N_DEV - 1, tr, c), lambda l, i: (l, 0, i, 0)), blk, blk, blk],
        out_specs=[blk] * 4, out_shape=[shp] * 4,
        compiler_params=_cparams(32, ("arbitrary", "arbitrary")),
    )(own, recv, w, m, v)


def _adam_small(g, w, m, v):
    rows = g.shape[0]

    def body(g_ref, w_ref, m_ref, v_ref, d_ref, nm_ref, nv_ref):
        delta, nm, nv = _adam_math(w_ref[...], g_ref[...], m_ref[...], v_ref[...])
        d_ref[...] = delta
        nm_ref[...] = nm
        nv_ref[...] = nv

    spec = _full((rows, LANES))
    shp = jax.ShapeDtypeStruct((rows, LANES), F32)
    return pl.pallas_call(
        body, name="adam_small", grid=(1,),
        in_specs=[spec] * 4, out_specs=[spec] * 3, out_shape=[shp] * 3,
        compiler_params=_cparams(24),
    )(g, w, m, v)


SMALL = ["norm1_g", "sgu_ln_g", "sgu_ln_b", "sgu_w", "sgu_b", "conv_b", "conv_ln_g", "conv_ln_b", "norm2_g",
         "final_g"]


def _pack(arrays):
    flat = jnp.concatenate([a.reshape(-1) for a in arrays])
    pad = (-flat.shape[0]) % (N_DEV * 8 * LANES)
    return jnp.pad(flat, (0, pad)).reshape(-1, LANES)


def _unpack(packed, shapes):
    flat = packed.reshape(-1)
    out, off = [], 0
    for s in shapes:
        size = 1
        for d in s:
            size *= d
        out.append(flat[off:off + size].reshape(s))
        off += size
    return out


def kernel(x, norm1_g, w_in, sgu_ln_g, sgu_ln_b, sgu_w, sgu_b, conv_w, conv_b, conv_ln_g, conv_ln_b, w_out, norm2_g, w_ff1, w_ff2, final_g, loss_target, m_norm1_g, m_w_in, m_sgu_ln_g, m_sgu_ln_b, m_sgu_w, m_sgu_b, m_conv_w, m_conv_b, m_conv_ln_g, m_conv_ln_b, m_w_out, m_norm2_g, m_w_ff1, m_w_ff2, m_final_g, v_norm1_g, v_w_in, v_sgu_ln_g, v_sgu_ln_b, v_sgu_w, v_sgu_b, v_conv_w, v_conv_b, v_conv_ln_g, v_conv_ln_b, v_w_out, v_norm2_g, v_w_ff1, v_w_ff2, v_final_g):
    x2d = x.reshape(T, D)
    tgt = loss_target.reshape(T, D)
    cw_shard = conv_w.reshape(CONV_W, LANES)

    gnames, shards = [], []
    for l in range(DEPTH):
        for k, w in (("w_in", w_in), ("w_out", w_out), ("w_ff1", w_ff1), ("w_ff2", w_ff2)):
            gnames.append(f"{k}{l}")
            shards.append((w[l], BF16))
        if l == 0:
            gnames.insert(1, "conv_w")
            shards.insert(1, (cw_shard, F32))
    sends, recvs, lands = _gather_start(_place_own(shards))
    gidx = {k: i for i, k in enumerate(gnames)}

    passed = {}

    def pass_on(k, after):
        i = gidx[k]
        passed[k] = _gather_pass_on("gather_pass_on_" + k, lands[i], sends[i], recvs[i], after)

    def gathered(k, after):
        i = gidx[k]
        send2, recv2, land = passed[k]
        return _gather_wait("gather_wait_" + k, land, sends[i], recvs[i], send2, recv2, after)

    saved = []
    xl = x2d
    cw_full = None
    for l in range(DEPTH):
        g1 = norm1_g[l].reshape(1, D)
        g2 = norm2_g[l].reshape(1, D)
        lg, lb = sgu_ln_g[l].reshape(1, D_A), sgu_ln_b[l].reshape(1, D_A)
        bst = sgu_b[l].T
        cb = conv_b[l].reshape(1, D_B)
        clg, clb = conv_ln_g[l].reshape(1, D_B), conv_ln_b[l].reshape(1, D_B)
        if l == 0:
            pass_on("w_in0", xl)
        wg_in = gathered(f"w_in{l}", xl)
        if l == 0:
            pass_on("conv_w", xl)
        h, proj = _fwd_in(xl, g1, wg_in)
        if l == 0:
            cw_full = gathered("conv_w", proj).reshape(N_DEV, DEPTH, CONV_W, D_B // N_DEV).transpose(
                1, 2, 0, 3).reshape(DEPTH, CONV_W, D_B)
        pass_on(f"w_out{l}", proj)
        mix, c, g = _mixer_fwd(proj, lg, lb, sgu_w[l], bst, cw_full[l], cb, clg, clb)
        pass_on(f"w_ff1{l}", mix)
        wout = gathered(f"w_out{l}", mix).reshape(D, D)
        x1, h2 = _fwd_out(mix, wout, xl, g2)
        pass_on(f"w_ff2{l}", h2)
        wg_ff1 = gathered(f"w_ff1{l}", h2)
        r = _fwd_ff1(h2, wg_ff1)
        if l + 1 < DEPTH:
            pass_on(f"w_in{l + 1}", r)
        w2 = gathered(f"w_ff2{l}", r).reshape(D_FF, D)
        x2 = _fwd_ff2(r, w2, x1)
        saved.append(dict(x=xl, h=h, proj=proj, mix=mix, c=c, g=g, x1=x1, h2=h2, r=r, wg_in=wg_in, wout=wout,
                          wg_ff1=wg_ff1, w2=w2, g1=g1, g2=g2, lg=lg, lb=lb, bst=bst, clg=clg, clb=clb))
        xl = x2

    dx, dxb, loss_part, d_final_g = _loss_bwd(xl, final_g.reshape(1, D), tgt)
    loss = lax.psum(loss_part[0, 0], ("x", "y", "c"))

    names = ["w_in", "w_out", "w_ff1", "w_ff2"]
    block = dict(w_in=(D, IN_COLS // N_DEV), w_out=(D // N_DEV, D), w_ff1=(D, D_FF // N_DEV), w_ff2=(D_FF // N_DEV, D))
    land = {k: lax.empty((DEPTH, N_DEV - 1) + block[k], BF16) for k in names}
    big32 = {k: [None] * DEPTH for k in names}
    big16 = {k: [None] * DEPTH for k in names}
    ssend = {k: [None] * DEPTH for k in names}
    srecv = {k: [None] * DEPTH for k in names}

    def send_grad(k, l, g32, g16):
        big32[k][l] = g32.reshape((N_DEV,) + block[k])
        ssend[k][l], srecv[k][l], big16[k][l], land[k] = _scatter_start(
            f"scatter_start_{k}{l}", g16.reshape((N_DEV,) + block[k]), land[k], l)

    small = {}
    d_conv_w = [None] * DEPTH
    for l in reversed(range(DEPTH)):
        s = saved[l]
        send_grad("w_ff2", l, *_wgrad("wgrad_ff2", s["r"], dxb, D, square_a=True))
        df1 = _bwd_ff2(dxb, s["w2"], s["r"], big16["w_ff2"][l])
        send_grad("w_ff1", l, *_wgrad("wgrad_ff1", s["h2"], df1, D_FF // N_DEV))
        dx1, dx1b, dg2 = _bwd_norm_matmul("bwd_ff1", df1, s["wg_ff1"], s["x1"], s["g2"], dx, big16["w_ff1"][l])
        send_grad("w_out", l, *_wgrad("wgrad_out", s["mix"], dx1b, D))
        dpa, dc, dlg, dlb, dwm, dbs, dcb, dclg, dclb = _mixer_bwd_a(
            dx1b, s["wout"], s["proj"], s["c"], s["lg"], s["lb"], sgu_w[l], s["bst"], s["clg"], s["clb"],
            big16["w_out"][l])
        dproj, d_conv_w[l] = _mixer_bwd_b(dc, s["g"], s["proj"], cw_full[l], dpa)
        send_grad("w_in", l, *_wgrad("wgrad_in", s["h"], dproj, IN_COLS // N_DEV))
        dx, dxb, dg1 = _bwd_norm_matmul("bwd_in", dproj, s["wg_in"], s["x"], s["g1"], dx1, big16["w_in"][l])
        small[l] = dict(norm1_g=dg1, sgu_ln_g=dlg, sgu_ln_b=dlb, sgu_w=dwm, sgu_b=dbs, conv_b=dcb, conv_ln_g=dclg,
                        conv_ln_b=dclb, norm2_g=dg2)
    grad_x = dx.reshape(1, T, D)

    own = _own_blocks([big32[k] for k in names])
    recv = _scatter_wait([g for k in names for g in big16[k]], [land[k] for k in names],
                         [q for k in names for q in ssend[k]], [q for k in names for q in srecv[k]], dx)
    wmv = dict(w_in=(w_in, m_w_in, v_w_in), w_out=(w_out, m_w_out, v_w_out), w_ff1=(w_ff1, m_w_ff1, v_w_ff1),
               w_ff2=(w_ff2, m_w_ff2, v_w_ff2))
    res = {}
    for idx, k in enumerate(names):
        res[k] = _adam_sharded("adam_" + k, own[idx], recv[idx], *wmv[k])

    rep = dict(norm1_g=(norm1_g, m_norm1_g, v_norm1_g), sgu_ln_g=(sgu_ln_g, m_sgu_ln_g, v_sgu_ln_g),
               sgu_ln_b=(sgu_ln_b, m_sgu_ln_b, v_sgu_ln_b), sgu_w=(sgu_w, m_sgu_w, v_sgu_w),
               sgu_b=(sgu_b, m_sgu_b, v_sgu_b), conv_b=(conv_b, m_conv_b, v_conv_b),
               conv_ln_g=(conv_ln_g, m_conv_ln_g, v_conv_ln_g), conv_ln_b=(conv_ln_b, m_conv_ln_b, v_conv_ln_b),
               norm2_g=(norm2_g, m_norm2_g, v_norm2_g), final_g=(final_g, m_final_g, v_final_g))
    parts = []
    for k in SMALL:
        if k == "final_g":
            parts.append(d_final_g.reshape(rep[k][0].shape))
        else:
            parts.append(jnp.stack([small[l][k].reshape(rep[k][0].shape[1:]) for l in range(DEPTH)]))
    parts.append(jnp.stack(d_conv_w))
    shapes = [p.shape for p in parts]
    summed = _unpack(_all_reduce_small(_pack(parts)), shapes)
    me = _flat(*_coords())
    g_conv_w = lax.dynamic_slice_in_dim(summed[-1], me * (D_B // N_DEV), D_B // N_DEV, axis=2)
    g_small = summed[:-1] + [g_conv_w]
    keys = SMALL + ["conv_w"]
    rep["conv_w"] = (conv_w, m_conv_w, v_conv_w)
    shapes = [a.shape for a in g_small]
    packed = [_pack(g_small)] + [_pack([rep[k][i] for k in keys]) for i in range(3)]
    d_s, nm_s, nv_s = (_unpack(a, shapes) for a in _adam_small(*packed))
    for i, k in enumerate(keys):
        res[k] = (g_small[i], d_s[i], nm_s[i], nv_s[i])

    order = ["norm1_g", "w_in", "sgu_ln_g", "sgu_ln_b", "sgu_w", "sgu_b", "conv_w", "conv_b", "conv_ln_g",
             "conv_ln_b", "w_out", "norm2_g", "w_ff1", "w_ff2", "final_g"]
    return (loss, grad_x, *[res[k][0] for k in order], *[res[k][1] for k in order],
            *[res[k][2] for k in order], *[res[k][3] for k in order])
```

```python
import functools

import jax
import jax.numpy as jnp
from jax import lax
from jax.experimental import pallas as pl
from jax.experimental.pallas import tpu as pltpu

F32 = jnp.float32
BF16 = jnp.bfloat16

N_DEV = 8
DEPTH = 2
T = 4096
D = 1024
D_A = 512
D_B = 512
CHUNK = 128
H_A = 4
H_B = 4
CONV_W = 31
HALO = 32
D_FF = 4096
IN_COLS = 2048
EPS = 1e-6

ADAM_LR = 0.001
ADAM_B1 = 0.9
ADAM_B2 = 0.999
ADAM_EPS = 1e-08
ADAM_WD = 0.01
ADAM_STEP = 10

TM = 512
TM_FWD = 1024
TT_WGRAD = 1024
TM_MIX = 256
RB = 64
LANES = 128
MIB = 1024 * 1024
SCOPED_VMEM_MIB = 60

SQRT_HALF = 0.7071067811865476
INV_SQRT_2PI = 0.3989422804014327

MESH_ID = pl.DeviceIdType.MESH
ANY = pl.BlockSpec(memory_space=pl.ANY)
HBM = pl.BlockSpec(memory_space=pltpu.HBM)
SEM = pl.BlockSpec(memory_space=pltpu.SEMAPHORE)
EFFECT = pltpu.SideEffectType.DATAFLOW_SIDE_EFFECTING


def _cparams(vmem_mib, sem=("arbitrary",)):
    assert vmem_mib <= SCOPED_VMEM_MIB
    return pltpu.CompilerParams(dimension_semantics=sem, vmem_limit_bytes=SCOPED_VMEM_MIB * MIB)


def _full(shape):
    return pl.BlockSpec(shape, lambda *_: (0,) * len(shape))


def _rows(tm, cols):
    return pl.BlockSpec((tm, cols), lambda i: (i, 0))


def _gelu(x):
    cdf = 0.5 * (1.0 + lax.erf(x * SQRT_HALF))
    return x * cdf, cdf


def _gelu_grad(x, cdf):
    return cdf + x * (INV_SQRT_2PI * jnp.exp(-0.5 * x * x))


def _sigmoid(x):
    return 1.0 / (1.0 + jnp.exp(-x))


def _ln(x):
    mu = jnp.mean(x, axis=-1, keepdims=True)
    xc = x - mu
    rstd = lax.rsqrt(jnp.mean(xc * xc, axis=-1, keepdims=True) + EPS)
    return xc * rstd, rstd


def _ln_bwd(dyh, xhat, rstd):
    return rstd * (dyh - jnp.mean(dyh, axis=-1, keepdims=True) - xhat * jnp.mean(dyh * xhat, axis=-1, keepdims=True))


def _rms(x):
    return lax.rsqrt(jnp.mean(x * x, axis=-1, keepdims=True) + EPS)


def _rms_bwd(dh, x, r, g):
    n = x * r
    dn = dh * g
    dx = r * (dn - n * jnp.mean(dn * n, axis=-1, keepdims=True))
    return dx, jnp.sum(dh * n, axis=0, keepdims=True)


def _dot(a, b):
    return jnp.dot(a, b, preferred_element_type=F32)


def _dot_nt(a, b):
    return lax.dot_general(a, b, (((1,), (1,)), ((), ())), preferred_element_type=F32)


def _dot_tn(a, b):
    return lax.dot_general(a, b, (((0,), (0,)), ((), ())), preferred_element_type=F32)


def _tril_mask():
    r = lax.broadcasted_iota(jnp.int32, (CHUNK, CHUNK), 0)
    c = lax.broadcasted_iota(jnp.int32, (CHUNK, CHUNK), 1)
    return r >= c


def _fwd_in(x, g1, wg):
    bn = wg.shape[2]

    def body(x_ref, g_ref, w_ref, h_ref, p_ref):
        xv = x_ref[...]
        h = (xv * _rms(xv) * g_ref[...]).astype(BF16)
        h_ref[...] = h
        for j in range(N_DEV):
            p_ref[:, j * bn:(j + 1) * bn] = _dot(h, w_ref[j])

    return pl.pallas_call(
        body, name="fwd_in", grid=(T // TM_FWD,),
        in_specs=[_rows(TM_FWD, D), _full((1, D)), _full(wg.shape)],
        out_specs=[_rows(TM_FWD, D), _rows(TM_FWD, IN_COLS)],
        out_shape=[jax.ShapeDtypeStruct((T, D), BF16), jax.ShapeDtypeStruct((T, IN_COLS), F32)],
        compiler_params=_cparams(32),
    )(x, g1, wg)


def _build_shifts(sh_ref):
    rows = sh_ref.shape[1]
    for p in range(1, 8):
        sh_ref[p, 0:rows - 8, :] = sh_ref[0, p:p + rows - 8, :]


def _shifted(sh_ref, row, cols):
    return sh_ref[row % 8, row - row % 8:row - row % 8 + RB, cols]


def _conv_taps(sh_ref, w_ref, cols, base, first_tap_row, step):
    acc = jnp.zeros((RB, LANES), F32)
    for k in range(CONV_W):
        acc = acc + _shifted(sh_ref, base + first_tap_row + step * k, cols) * w_ref[k:k + 1, cols]
    return acc


def _mixer_fwd(proj, lg, lb, wm, bst, cw, cb, clg, clb):
    tm = TM_MIX
    hb = tm // HALO

    def body(p_ref, ph_ref, lg_ref, lb_ref, wm_ref, bs_ref, cw_ref, cb_ref, clg_ref, clb_ref,
             mix_ref, c_ref, g_ref, gbuf):
        i = pl.program_id(0)
        u, _ = _gelu(p_ref[:, 0:D_A])
        vg, _ = _gelu(p_ref[:, D_A:2 * D_A])
        xhat, _ = _ln(vg)
        v = (xhat * lg_ref[...] + lb_ref[...]).astype(BF16)
        mask = _tril_mask()
        for h in range(H_A):
            hc = slice(h * CHUNK, (h + 1) * CHUNK)
            wmh = jnp.where(mask, wm_ref[h], 0.0).astype(BF16)
            for c in range(tm // CHUNK):
                rc = slice(c * CHUNK, (c + 1) * CHUNK)
                mixed = _dot(wmh, v[rc, hc]) + bs_ref[:, h:h + 1]
                mix_ref[rc, hc] = (u[rc, hc] * mixed).astype(BF16)

        g = p_ref[:, 2 * D_A:2 * D_A + D_B] * _sigmoid(p_ref[:, 2 * D_A + D_B:IN_COLS])
        g_ref[...] = g
        gh = ph_ref[:, 0:D_B] * _sigmoid(ph_ref[:, D_B:2 * D_B])
        gbuf[0, 0:HALO, :] = jnp.where(i > 0, gh, 0.0)
        gbuf[0, HALO:HALO + tm, :] = g
        _build_shifts(gbuf)
        for q in range(H_B):
            cols = slice(q * LANES, (q + 1) * LANES)
            for rb in range(tm // RB):
                acc = _conv_taps(gbuf, cw_ref, cols, rb * RB, HALO - (CONV_W - 1), 1)
                c_ref[rb * RB:(rb + 1) * RB, cols] = acc + cb_ref[:, cols]
        for q in range(H_B):
            cols = slice(q * LANES, (q + 1) * LANES)
            chat, _ = _ln(c_ref[:, cols])
            z = chat * clg_ref[:, cols] + clb_ref[:, cols]
            mix_ref[:, D_A + q * LANES:D_A + (q + 1) * LANES] = (z * _sigmoid(z)).astype(BF16)

    vec = _full((1, D_A))
    return pl.pallas_call(
        body, name="mixer_fwd", grid=(T // tm,),
        in_specs=[_rows(tm, IN_COLS),
                  pl.BlockSpec((HALO, 2 * D_B), lambda i: (jnp.maximum(i * hb - 1, 0), 1)),
                  vec, vec, _full((H_A, CHUNK, CHUNK)), _full((CHUNK, H_A)),
                  _full((CONV_W, D_B)), vec, vec, vec],
        out_specs=[_rows(tm, D), _rows(tm, D_B), _rows(tm, D_B)],
        out_shape=[jax.ShapeDtypeStruct((T, D), BF16), jax.ShapeDtypeStruct((T, D_B), F32),
                   jax.ShapeDtypeStruct((T, D_B), F32)],
        scratch_shapes=[pltpu.VMEM((8, HALO + tm, D_B), F32)],
        compiler_params=_cparams(32),
    )(proj, proj, lg, lb, wm, bst, cw, cb, clg, clb)


def _fwd_out(mix, wout, x, g2):
    def body(m_ref, w_ref, x_ref, g_ref, x1_ref, h2_ref):
        x1 = x_ref[...] + _dot(m_ref[...], w_ref[...])
        x1_ref[...] = x1
        h2_ref[...] = (x1 * _rms(x1) * g_ref[...]).astype(BF16)

    return pl.pallas_call(
        body, name="fwd_out", grid=(T // TM_FWD,),
        in_specs=[_rows(TM_FWD, D), _full((D, D)), _rows(TM_FWD, D), _full((1, D))],
        out_specs=[_rows(TM_FWD, D), _rows(TM_FWD, D)],
        out_shape=[jax.ShapeDtypeStruct((T, D), F32), jax.ShapeDtypeStruct((T, D), BF16)],
        compiler_params=_cparams(32),
    )(mix, wout, x, g2)


def _fwd_ff1(h2, wg):
    bn = wg.shape[2]

    def body(h_ref, w_ref, r_ref):
        h = h_ref[...]
        for j in range(N_DEV):
            r_ref[:, j * bn:(j + 1) * bn] = jnp.maximum(_dot(h, w_ref[j]), 0.0).astype(BF16)

    return pl.pallas_call(
        body, name="fwd_ff1", grid=(T // TM_FWD,),
        in_specs=[_rows(TM_FWD, D), _full(wg.shape)],
        out_specs=_rows(TM_FWD, D_FF),
        out_shape=jax.ShapeDtypeStruct((T, D_FF), BF16),
        compiler_params=_cparams(48),
    )(h2, wg)


def _fwd_ff2(r, w2, x1):
    def body(r_ref, w_ref, x_ref, o_ref):
        rv = r_ref[...]
        o_ref[...] = x_ref[...] + _dot(rv * rv, w_ref[...])

    return pl.pallas_call(
        body, name="fwd_ff2", grid=(T // TM_FWD,),
        in_specs=[_rows(TM_FWD, D_FF), _full((D_FF, D)), _rows(TM_FWD, D)],
        out_specs=_rows(TM_FWD, D),
        out_shape=jax.ShapeDtypeStruct((T, D), F32),
        compiler_params=_cparams(48),
    )(r, w2, x1)


def _loss_bwd(xf, gf, tgt):
    def body(x_ref, g_ref, t_ref, dx_ref, dxb_ref, loss_ref, dg_ref):
        i = pl.program_id(0)

        @pl.when(i == 0)
        def _():
            loss_ref[...] = jnp.zeros(loss_ref.shape, F32)
            dg_ref[...] = jnp.zeros(dg_ref.shape, F32)

        xv = x_ref[...]
        r = _rms(xv)
        diff = xv * r * g_ref[...] - t_ref[...]
        loss_ref[...] += 0.5 * jnp.sum(jnp.mean(diff * diff, axis=-1, keepdims=True), axis=0, keepdims=True)
        dx, dg = _rms_bwd(diff * (1.0 / D), xv, r, g_ref[...])
        dx_ref[...] = dx
        dxb_ref[...] = dx.astype(BF16)
        dg_ref[...] += dg

    return pl.pallas_call(
        body, name="loss_bwd", grid=(T // TM,),
        in_specs=[_rows(TM, D), _full((1, D)), _rows(TM, D)],
        out_specs=[_rows(TM, D), _rows(TM, D), _full((8, LANES)), _full((1, D))],
        out_shape=[jax.ShapeDtypeStruct((T, D), F32), jax.ShapeDtypeStruct((T, D), BF16),
                   jax.ShapeDtypeStruct((8, LANES), F32), jax.ShapeDtypeStruct((1, D), F32)],
        compiler_params=_cparams(32),
    )(xf, gf, tgt)


def _bwd_ff2(dxb, w2, r, dep):
    kb = 1024

    def body(d_ref, w_ref, r_ref, dep_ref, o_ref):
        d = d_ref[...]
        for j in range(D_FF // kb):
            cols = slice(j * kb, (j + 1) * kb)
            df = _dot_nt(d, w_ref[cols, :])
            o_ref[:, cols] = (2.0 * r_ref[:, cols].astype(F32) * df).astype(BF16)

    return pl.pallas_call(
        body, name="bwd_ff2", grid=(T // TM,),
        in_specs=[_rows(TM, D), _full((D_FF, D)), _rows(TM, D_FF), ANY],
        out_specs=_rows(TM, D_FF),
        out_shape=jax.ShapeDtypeStruct((T, D_FF), BF16),
        compiler_params=_cparams(48),
    )(dxb, w2, r, dep)


def _bwd_norm_matmul(name, dy, wg, x, g, dres, dep):
    bn = wg.shape[2]
    ncol = dy.shape[1]

    def body(dy_ref, w_ref, x_ref, g_ref, dres_ref, dep_ref, dx_ref, dxb_ref, dg_ref):
        i = pl.program_id(0)

        @pl.when(i == 0)
        def _():
            dg_ref[...] = jnp.zeros(dg_ref.shape, F32)

        dh = jnp.zeros((TM, D), F32)
        for j in range(N_DEV):
            dh = dh + _dot_nt(dy_ref[:, j * bn:(j + 1) * bn], w_ref[j])
        xv = x_ref[...]
        dxn, dg = _rms_bwd(dh, xv, _rms(xv), g_ref[...])
        dx = dres_ref[...] + dxn
        dx_ref[...] = dx
        dxb_ref[...] = dx.astype(BF16)
        dg_ref[...] += dg

    return pl.pallas_call(
        body, name=name, grid=(T // TM,),
        in_specs=[_rows(TM, ncol), _full(wg.shape), _rows(TM, D), _full((1, D)), _rows(TM, D), ANY],
        out_specs=[_rows(TM, D), _rows(TM, D), _full((1, D))],
        out_shape=[jax.ShapeDtypeStruct((T, D), F32), jax.ShapeDtypeStruct((T, D), BF16),
                   jax.ShapeDtypeStruct((1, D), F32)],
        compiler_params=_cparams(48),
    )(dy, wg, x, g, dres, dep)


def _mixer_bwd_a(dxb, wout, proj, c, lg, lb, wm, bst, clg, clb, dep):
    tm = TM_MIX
    n_tiles = T // tm

    def body(dx_ref, wo_ref, p_ref, c_ref, lg_ref, lb_ref, wm_ref, bs_ref, clg_ref, clb_ref, dep_ref,
             dpa_ref, dc_ref, dlg_ref, dlb_ref, dwm_ref, dbs_ref, dcb_ref, dclg_ref, dclb_ref,
             dv_buf, db_acc):
        i = pl.program_id(0)

        @pl.when(i == 0)
        def _():
            for ref in (dlg_ref, dlb_ref, dwm_ref, dbs_ref, dcb_ref, dclg_ref, dclb_ref, db_acc):
                ref[...] = jnp.zeros(ref.shape, F32)

        dmix = _dot_nt(dx_ref[...], wo_ref[...])
        ua = p_ref[:, 0:D_A]
        va = p_ref[:, D_A:2 * D_A]
        u, cdf_u = _gelu(ua)
        vg, cdf_v = _gelu(va)
        xhat, rstd = _ln(vg)
        v = (xhat * lg_ref[...] + lb_ref[...]).astype(BF16)
        mask = _tril_mask()
        for h in range(H_A):
            hc = slice(h * CHUNK, (h + 1) * CHUNK)
            wmh = jnp.where(mask, wm_ref[h], 0.0).astype(BF16)
            for cidx in range(tm // CHUNK):
                rc = slice(cidx * CHUNK, (cidx + 1) * CHUNK)
                vb = v[rc, hc]
                mixed = _dot(wmh, vb) + bs_ref[:, h:h + 1]
                da = dmix[rc, hc]
                dpa_ref[rc, hc] = (da * mixed * _gelu_grad(ua[rc, hc], cdf_u[rc, hc])).astype(BF16)
                dmixed = da * u[rc, hc]
                dmb = dmixed.astype(BF16)
                dv_buf[rc, hc] = _dot_tn(wmh, dmb)
                dwm_ref[h] += _dot_nt(dmb, vb)
                db_acc[:, hc] += dmixed
        dv = dv_buf[...]
        dlb_ref[...] += jnp.sum(dv, axis=0, keepdims=True)
        dlg_ref[...] += jnp.sum(dv * xhat, axis=0, keepdims=True)
        dvg = _ln_bwd(dv * lg_ref[...], xhat, rstd)
        dpa_ref[:, D_A:2 * D_A] = (dvg * _gelu_grad(va, cdf_v)).astype(BF16)

        for q in range(H_B):
            cols = slice(q * LANES, (q + 1) * LANES)
            chat, crstd = _ln(c_ref[:, cols])
            z = chat * clg_ref[:, cols] + clb_ref[:, cols]
            sg = _sigmoid(z)
            dz = dmix[:, D_A + q * LANES:D_A + (q + 1) * LANES] * (sg * (1.0 + z * (1.0 - sg)))
            dclb_ref[:, cols] += jnp.sum(dz, axis=0, keepdims=True)
            dclg_ref[:, cols] += jnp.sum(dz * chat, axis=0, keepdims=True)
            dc = _ln_bwd(dz * clg_ref[:, cols], chat, crstd)
            dc_ref[:, cols] = dc
            dcb_ref[:, cols] += jnp.sum(dc, axis=0, keepdims=True)

        @pl.when(i == n_tiles - 1)
        def _():
            for h in range(H_A):
                hc = slice(h * CHUNK, (h + 1) * CHUNK)
                dwm_ref[h] = jnp.where(mask, dwm_ref[h], 0.0)
                dbs_ref[h:h + 1, :] = jnp.sum(db_acc[:, hc].T, axis=0, keepdims=True)

    vec = _full((1, D_A))
    vshape = jax.ShapeDtypeStruct((1, D_A), F32)
    return pl.pallas_call(
        body, name="mixer_bwd_a", grid=(n_tiles,),
        in_specs=[_rows(tm, D), _full((D, D)), _rows(tm, IN_COLS), _rows(tm, D_B), vec, vec,
                  _full((H_A, CHUNK, CHUNK)), _full((CHUNK, H_A)), vec, vec, ANY],
        out_specs=[_rows(tm, 2 * D_A), _rows(tm, D_B), vec, vec, _full((H_A, CHUNK, CHUNK)),
                   _full((H_A, CHUNK)), vec, vec, vec],
        out_shape=[jax.ShapeDtypeStruct((T, 2 * D_A), BF16), jax.ShapeDtypeStruct((T, D_B), F32), vshape, vshape,
                   jax.ShapeDtypeStruct((H_A, CHUNK, CHUNK), F32), jax.ShapeDtypeStruct((H_A, CHUNK), F32),
                   vshape, vshape, vshape],
        scratch_shapes=[pltpu.VMEM((tm, D_A), F32), pltpu.VMEM((CHUNK, D_A), F32)],
        compiler_params=_cparams(32),
    )(dxb, wout, proj, c, lg, lb, wm, bst, clg, clb, dep)


def _mixer_bwd_b(dc, g, proj, cw, dpa):
    tm = TM_MIX
    n_tiles = T // tm
    hb = tm // HALO

    def body(dc_ref, dch_ref, g_ref, gh_ref, p_ref, cw_ref, dpa_ref, dp_ref, dcw_ref, dcbuf, gbuf, dwacc):
        i = pl.program_id(0)

        @pl.when(i == 0)
        def _():
            dwacc[...] = jnp.zeros(dwacc.shape, F32)

        dcbuf[0, 0:tm, :] = dc_ref[...]
        dcbuf[0, tm:tm + HALO, :] = jnp.where(i < n_tiles - 1, dch_ref[...], 0.0)
        gbuf[0, 0:HALO, :] = jnp.where(i > 0, gh_ref[...], 0.0)
        gbuf[0, HALO:HALO + tm, :] = g_ref[...]
        _build_shifts(dcbuf)
        _build_shifts(gbuf)
        dp_ref[:, 0:2 * D_A] = dpa_ref[...]
        for q in range(H_B):
            cols = slice(q * LANES, (q + 1) * LANES)
            for rb in range(tm // RB):
                rows = slice(rb * RB, (rb + 1) * RB)
                dg = _conv_taps(dcbuf, cw_ref, cols, rb * RB, CONV_W - 1, -1)
                val = p_ref[rows, cols]
                sg = _sigmoid(p_ref[rows, D_B + q * LANES:D_B + (q + 1) * LANES])
                dp_ref[rows, 2 * D_A + q * LANES:2 * D_A + (q + 1) * LANES] = (dg * sg).astype(BF16)
                dp_ref[rows, 2 * D_A + D_B + q * LANES:2 * D_A + D_B + (q + 1) * LANES] = (
                    dg * val * sg * (1.0 - sg)).astype(BF16)
                dcv = dc_ref[rows, cols]
                for k in range(CONV_W):
                    prod = dcv * _shifted(gbuf, rb * RB + HALO - (CONV_W - 1) + k, cols)
                    dwacc[k * 8:(k + 1) * 8, cols] += jnp.sum(prod.reshape(RB // 8, 8, LANES), axis=0)

        @pl.when(i == n_tiles - 1)
        def _():
            for k in range(CONV_W):
                dcw_ref[k:k + 1, :] = jnp.sum(dwacc[k * 8:(k + 1) * 8, :], axis=0, keepdims=True)

    return pl.pallas_call(
        body, name="mixer_bwd_b", grid=(n_tiles,),
        in_specs=[_rows(tm, D_B),
                  pl.BlockSpec((HALO, D_B), lambda i: (jnp.minimum((i + 1) * hb, T // HALO - 1), 0)),
                  _rows(tm, D_B),
                  pl.BlockSpec((HALO, D_B), lambda i: (jnp.maximum(i * hb - 1, 0), 0)),
                  pl.BlockSpec((tm, 2 * D_B), lambda i: (i, 1)),
                  _full((CONV_W, D_B)), _rows(tm, 2 * D_A)],
        out_specs=[_rows(tm, IN_COLS), _full((CONV_W, D_B))],
        out_shape=[jax.ShapeDtypeStruct((T, IN_COLS), BF16), jax.ShapeDtypeStruct((CONV_W, D_B), F32)],
        scratch_shapes=[pltpu.VMEM((8, tm + HALO, D_B), F32), pltpu.VMEM((8, HALO + tm, D_B), F32),
                        pltpu.VMEM((CONV_W * 8, D_B), F32)],
        compiler_params=_cparams(32),
    )(dc, dc, g, g, proj, cw, dpa)


def _wgrad(name, a, g, bn, square_a=False):
    k = a.shape[1]
    n = g.shape[1]
    tk = min(k, 1024)
    tn = min(n, max(bn, 1024))
    nsub = tn // bn
    tt = TT_WGRAD
    nt = T // tt

    def body(a_ref, g_ref, o_ref, ob_ref):
        t = pl.program_id(2)

        @pl.when(t == 0)
        def _():
            o_ref[...] = jnp.zeros(o_ref.shape, F32)

        av = a_ref[...]
        if square_a:
            av = av * av
        for s in range(nsub):
            o_ref[s] += _dot_tn(av, g_ref[:, s * bn:(s + 1) * bn])

        @pl.when(t == nt - 1)
        def _():
            ob_ref[...] = o_ref[...].astype(BF16)

    ospec = pl.BlockSpec((nsub, tk, bn), lambda ki, ni, ti: (ni, ki, 0))
    return pl.pallas_call(
        body, name=name, grid=(k // tk, n // tn, nt),
        in_specs=[pl.BlockSpec((tt, tk), lambda ki, ni, ti: (ti, ki)),
                  pl.BlockSpec((tt, tn), lambda ki, ni, ti: (ti, ni))],
        out_specs=[ospec, ospec],
        out_shape=[jax.ShapeDtypeStruct((n // bn, k, bn), F32), jax.ShapeDtypeStruct((n // bn, k, bn), BF16)],
        compiler_params=_cparams(40, ("arbitrary", "arbitrary", "arbitrary")),
    )(a, g)


def _coords():
    return lax.axis_index("x"), lax.axis_index("y"), lax.axis_index("c")


def _flat(x, y, c):
    return 4 * x + 2 * y + c


def _peer(k):
    x, y, c = _coords()
    return (x ^ ((k >> 2) & 1), y ^ ((k >> 1) & 1), c ^ (k & 1))


def _hbm(a):
    return pltpu.with_memory_space_constraint(a, pltpu.HBM)


def _hbm_like(a):
    return pltpu.HBM(a.shape, a.dtype)


def _peer_sems():
    return pltpu.SemaphoreType.DMA((N_DEV - 1,))


def _place_own(shards):
    n = len(shards)

    def body(*refs):
        ins, outs = refs[:n], refs[n:2 * n]
        stage_in, stage_out = refs[2 * n:3 * n], refs[3 * n:4 * n]
        in_sems, out_sems = refs[4 * n], refs[4 * n + 1]
        me = _flat(*_coords())
        loads = [pltpu.make_async_copy(ins[a], stage_in[a], in_sems.at[a]) for a in range(n)]
        stores = [pltpu.make_async_copy(stage_out[a], outs[a].at[me], out_sems.at[a]) for a in range(n)]
        for cp in loads:
            cp.start()
        for a in range(n):
            loads[a].wait()
            stage_out[a][...] = stage_in[a][...].astype(stage_out[a].dtype)
            stores[a].start()
        for cp in stores:
            cp.wait()

    return pl.pallas_call(
        body, name="place_own", in_specs=[ANY] * n, out_specs=[ANY] * n,
        out_shape=[jax.ShapeDtypeStruct((N_DEV,) + s.shape, dt) for s, dt in shards],
        scratch_shapes=[*[pltpu.VMEM(s.shape, s.dtype) for s, _ in shards],
                        *[pltpu.VMEM(s.shape, dt) for s, dt in shards],
                        pltpu.SemaphoreType.DMA((n,)), pltpu.SemaphoreType.DMA((n,))],
        compiler_params=pltpu.CompilerParams(vmem_limit_bytes=40 * MIB),
    )(*[s for s, _ in shards])


SIBLING = 1
CHIP_PEERS = (2, 4, 6)
FIRST_PEERS = (SIBLING,) + CHIP_PEERS


def _gather_start(lands):
    n = len(lands)

    def body(*refs):
        lnd, send, recv = refs[:n], refs[n:2 * n], refs[2 * n:3 * n]
        me = _flat(*_coords())
        for a in range(n):
            for j, k in enumerate(FIRST_PEERS):
                pltpu.make_async_remote_copy(
                    src_ref=lnd[a].at[me], dst_ref=lnd[a].at[me], send_sem=send[a].at[j],
                    recv_sem=recv[a].at[j], device_id=_peer(k), device_id_type=MESH_ID).start()

    sems = pltpu.SemaphoreType.DMA((len(FIRST_PEERS),))
    outs = pl.pallas_call(
        body, name="gather_start",
        out_shape=(*[sems] * (2 * n), *[_hbm_like(l) for l in lands]),
        in_specs=[HBM] * n, out_specs=(*[SEM] * (2 * n), *[HBM] * n),
        input_output_aliases={i: 2 * n + i for i in range(n)},
        compiler_params=pltpu.CompilerParams(has_side_effects=EFFECT),
    )(*[_hbm(l) for l in lands])
    return outs[:n], outs[n:2 * n], outs[2 * n:]


def _gather_pass_on(name, land, send, recv, after):
    def body(l_ref, send_ref, recv_ref, after_ref, send2, recv2, l_out):
        sibling = _peer(SIBLING)
        for j, k in enumerate(CHIP_PEERS):
            cp = pltpu.make_async_remote_copy(
                src_ref=l_ref.at[0], dst_ref=l_ref.at[0], send_sem=send_ref.at[1 + j], recv_sem=recv_ref.at[1 + j],
                device_id=_peer(k), device_id_type=MESH_ID)
            cp.wait_send()
            cp.wait_recv()
            blk = _flat(*_peer(k))
            pltpu.make_async_remote_copy(
                src_ref=l_ref.at[blk], dst_ref=l_ref.at[blk], send_sem=send2.at[j], recv_sem=recv2.at[j],
                device_id=sibling, device_id_type=MESH_ID).start()

    sems = pltpu.SemaphoreType.DMA((len(CHIP_PEERS),))
    return pl.pallas_call(
        body, name=name, out_shape=(sems, sems, _hbm_like(land)),
        in_specs=(HBM, SEM, SEM, ANY), out_specs=(SEM, SEM, HBM), input_output_aliases={0: 2},
        compiler_params=pltpu.CompilerParams(has_side_effects=EFFECT),
    )(land, send, recv, after)


def _gather_wait(name, land, send, recv, send2, recv2, after):
    def body(l_ref, send_ref, recv_ref, send2_ref, recv2_ref, after_ref, l_out):
        sibling = _peer(SIBLING)
        own = pltpu.make_async_remote_copy(
            src_ref=l_ref.at[0], dst_ref=l_ref.at[0], send_sem=send_ref.at[0], recv_sem=recv_ref.at[0],
            device_id=sibling, device_id_type=MESH_ID)
        own.wait_send()
        own.wait_recv()
        for j in range(len(CHIP_PEERS)):
            cp = pltpu.make_async_remote_copy(
                src_ref=l_ref.at[0], dst_ref=l_ref.at[0], send_sem=send2_ref.at[j], recv_sem=recv2_ref.at[j],
                device_id=sibling, device_id_type=MESH_ID)
            cp.wait_send()
            cp.wait_recv()

    return pl.pallas_call(
        body, name=name, out_shape=_hbm_like(land),
        in_specs=(HBM, SEM, SEM, SEM, SEM, ANY), out_specs=HBM, input_output_aliases={0: 0},
        compiler_params=pltpu.CompilerParams(has_side_effects=EFFECT),
    )(land, send, recv, send2, recv2, after)


def _scatter_start(name, g16, land, layer):
    def body(g_ref, land_ref, send, recv, g_out, land_out):
        for k in range(1, N_DEV):
            to = _peer(k)
            pltpu.make_async_remote_copy(
                src_ref=g_ref.at[_flat(*to)], dst_ref=land_ref.at[layer, k - 1],
                send_sem=send.at[k - 1], recv_sem=recv.at[k - 1], device_id=to, device_id_type=MESH_ID).start()

    return pl.pallas_call(
        body, name=name,
        out_shape=(_peer_sems(), _peer_sems(), _hbm_like(g16), _hbm_like(land)),
        in_specs=[HBM, HBM], out_specs=(SEM, SEM, HBM, HBM), input_output_aliases={0: 2, 1: 3},
        compiler_params=pltpu.CompilerParams(has_side_effects=EFFECT),
    )(_hbm(g16), _hbm(land))


def _scatter_wait(grads, lands, sends, recvs, after):
    n, nw = len(grads), len(lands)

    def body(*refs):
        g, lnd = refs[:n], refs[n:n + nw]
        send, recv = refs[n + nw:2 * n + nw], refs[2 * n + nw:3 * n + nw]
        for a in range(n):
            for k in range(1, N_DEV):
                cp = pltpu.make_async_remote_copy(
                    src_ref=g[a].at[0], dst_ref=lnd[a // DEPTH].at[a % DEPTH, 0],
                    send_sem=send[a].at[k - 1], recv_sem=recv[a].at[k - 1],
                    device_id=_peer(k), device_id_type=MESH_ID)
                cp.wait_send()
                cp.wait_recv()

    outs = pl.pallas_call(
        body, name="scatter_wait",
        out_shape=(*[_hbm_like(g) for g in grads], *[_hbm_like(l) for l in lands]),
        in_specs=(*[HBM] * (n + nw), *[SEM] * (2 * n), ANY), out_specs=[HBM] * (n + nw),
        input_output_aliases={i: i for i in range(n + nw)},
        compiler_params=pltpu.CompilerParams(has_side_effects=EFFECT),
    )(*grads, *lands, *sends, *recvs, after)
    return outs[n:]


def _own_blocks(grads_f32):
    nw = len(grads_f32)
    flat32 = [g for w in grads_f32 for g in w]
    n = len(flat32)

    def body(*refs):
        g32, own, stage = refs[:n], refs[n:n + nw], refs[n + nw:2 * n + nw]
        in_sems, out_sems = refs[2 * n + nw], refs[2 * n + nw + 1]
        me = _flat(*_coords())
        loads = [pltpu.make_async_copy(g32[a].at[me], stage[a], in_sems.at[a]) for a in range(n)]
        stores = [pltpu.make_async_copy(stage[a], own[a // DEPTH].at[a % DEPTH], out_sems.at[a]) for a in range(n)]
        for cp in loads:
            cp.start()
        for a in range(n):
            loads[a].wait()
            stores[a].start()
        for cp in stores:
            cp.wait()

    return pl.pallas_call(
        body, name="own_blocks", in_specs=[ANY] * n, out_specs=[ANY] * nw,
        out_shape=[jax.ShapeDtypeStruct((DEPTH,) + w[0].shape[1:], F32) for w in grads_f32],
        scratch_shapes=[*[pltpu.VMEM(g.shape[1:], F32) for g in flat32],
                        pltpu.SemaphoreType.DMA((n,)), pltpu.SemaphoreType.DMA((n,))],
        compiler_params=pltpu.CompilerParams(vmem_limit_bytes=24 * MIB),
    )(*flat32)


def _all_reduce_small(part):
    rows = part.shape[0]
    br = rows // N_DEV
    assert br * N_DEV == rows and br % 8 == 0

    def body(p_ref, o_ref, slots, send1, recv1, send2, recv2):
        me = _flat(*_coords())

        def block(ref, d):
            return ref.at[pl.ds(pl.multiple_of(d * br, 8), br), :]

        slots[me] = p_ref[pl.ds(pl.multiple_of(me * br, 8), br), :]
        scatter, gather = [], []
        for k in range(1, N_DEV):
            to = _peer(k)
            scatter.append(pltpu.make_async_remote_copy(
                src_ref=block(p_ref, _flat(*to)), dst_ref=slots.at[me],
                send_sem=send1.at[k - 1], recv_sem=recv1.at[k - 1], device_id=to, device_id_type=MESH_ID))
            gather.append(pltpu.make_async_remote_copy(
                src_ref=block(o_ref, me), dst_ref=block(o_ref, me),
                send_sem=send2.at[k - 1], recv_sem=recv2.at[k - 1], device_id=to, device_id_type=MESH_ID))
        for cp in scatter:
            cp.start()
        for cp in scatter:
            cp.wait()
        acc = slots[0]
        for d in range(1, N_DEV):
            acc = acc + slots[d]
        o_ref[pl.ds(pl.multiple_of(me * br, 8), br), :] = acc
        for cp in gather:
            cp.start()
        for cp in gather:
            cp.wait()

    return pl.pallas_call(
        body, name="all_reduce_small",
        in_specs=[pl.BlockSpec(memory_space=pltpu.VMEM)], out_specs=pl.BlockSpec(memory_space=pltpu.VMEM),
        out_shape=jax.ShapeDtypeStruct(part.shape, F32),
        scratch_shapes=[pltpu.VMEM((N_DEV, br, LANES), F32)] + [pltpu.SemaphoreType.DMA((N_DEV - 1,))] * 4,
        compiler_params=pltpu.CompilerParams(vmem_limit_bytes=24 * MIB),
    )(part)


def _adam_math(w, g, m, v):
    m = ADAM_B1 * m + (1.0 - ADAM_B1) * g
    v = ADAM_B2 * v + (1.0 - ADAM_B2) * (g * g)
    m_hat = m / (1.0 - ADAM_B1 ** ADAM_STEP)
    v_hat = v / (1.0 - ADAM_B2 ** ADAM_STEP)
    delta = -ADAM_LR * (m_hat / (jnp.sqrt(v_hat) + ADAM_EPS) + ADAM_WD * w)
    return delta, m, v


def _adam_sharded(name, own, recv, w, m, v):
    _, r, c = w.shape
    tr = min(r, 256)

    def body(own_ref, recv_ref, w_ref, m_ref, v_ref, g_ref, d_ref, nm_ref, nv_ref):
        g = own_ref[...]
        for k in range(N_DEV - 1):
            g = g + recv_ref[k].astype(F32)
        delta, nm, nv = _adam_math(w_ref[...], g, m_ref[...], v_ref[...])
        g_ref[...] = g
        d_ref[...] = delta
        nm_ref[...] = nm
        nv_ref[...] = nv

    blk = pl.BlockSpec((None, tr, c), lambda l, i: (l, i, 0))
    shp = jax.ShapeDtypeStruct(w.shape, F32)
    return pl.pallas_call(
        body, name=name, grid=(DEPTH, r // tr),
        in_specs=[blk, pl.BlockSpec((None, N_DEV - 1, tr, c), lambda l, i: (l, 0, i, 0)), blk, blk, blk],
        out_specs=[blk] * 4, out_shape=[shp] * 4,
        compiler_params=_cparams(32, ("arbitrary", "arbitrary")),
    )(own, recv, w, m, v)


def _adam_small(g, w, m, v):
    rows = g.shape[0]

    def body(g_ref, w_ref, m_ref, v_ref, d_ref, nm_ref, nv_ref):
        delta, nm, nv = _adam_math(w_ref[...], g_ref[...], m_ref[...], v_ref[...])
        d_ref[...] = delta
        nm_ref[...] = nm
        nv_ref[...] = nv

    spec = _full((rows, LANES))
    shp = jax.ShapeDtypeStruct((rows, LANES), F32)
    return pl.pallas_call(
        body, name="adam_small", grid=(1,),
        in_specs=[spec] * 4, out_specs=[spec] * 3, out_shape=[shp] * 3,
        compiler_params=_cparams(24),
    )(g, w, m, v)


SMALL = ["norm1_g", "sgu_ln_g", "sgu_ln_b", "sgu_w", "sgu_b", "conv_b", "conv_ln_g", "conv_ln_b", "norm2_g",
         "final_g"]


def _pack(arrays):
    flat = jnp.concatenate([a.reshape(-1) for a in arrays])
    pad = (-flat.shape[0]) % (N_DEV * 8 * LANES)
    return jnp.pad(flat, (0, pad)).reshape(-1, LANES)


def _unpack(packed, shapes):
    flat = packed.reshape(-1)
    out, off = [], 0
    for s in shapes:
        size = 1
        for d in s:
            size *= d
        out.append(flat[off:off + size].reshape(s))
        off += size
    return out


def kernel(x, norm1_g, w_in, sgu_ln_g, sgu_ln_b, sgu_w, sgu_b, conv_w, conv_b, conv_ln_g, conv_ln_b, w_out, norm2_g, w_ff1, w_ff2, final_g, loss_target, m_norm1_g, m_w_in, m_sgu_ln_g, m_sgu_ln_b, m_sgu_w, m_sgu_b, m_conv_w, m_conv_b, m_conv_ln_g, m_conv_ln_b, m_w_out, m_norm2_g, m_w_ff1, m_w_ff2, m_final_g, v_norm1_g, v_w_in, v_sgu_ln_g, v_sgu_ln_b, v_sgu_w, v_sgu_b, v_conv_w, v_conv_b, v_conv_ln_g, v_conv_ln_b, v_w_out, v_norm2_g, v_w_ff1, v_w_ff2, v_final_g):
    x2d = x.reshape(T, D)
    tgt = loss_target.reshape(T, D)
    cw_shard = conv_w.reshape(CONV_W, LANES)

    gnames, shards = [], []
    for l in range(DEPTH):
        for k, w in (("w_in", w_in), ("w_out", w_out), ("w_ff1", w_ff1), ("w_ff2", w_ff2)):
            gnames.append(f"{k}{l}")
            shards.append((w[l], BF16))
        if l == 0:
            gnames.insert(1, "conv_w")
            shards.insert(1, (cw_shard, F32))
    sends, recvs, lands = _gather_start(_place_own(shards))
    gidx = {k: i for i, k in enumerate(gnames)}

    passed = {}

    def pass_on(k, after):
        i = gidx[k]
        passed[k] = _gather_pass_on("gather_pass_on_" + k, lands[i], sends[i], recvs[i], after)

    def gathered(k, after):
        i = gidx[k]
        send2, recv2, land = passed[k]
        return _gather_wait("gather_wait_" + k, land, sends[i], recvs[i], send2, recv2, after)

    saved = []
    xl = x2d
    cw_full = None
    for l in range(DEPTH):
        g1 = norm1_g[l].reshape(1, D)
        g2 = norm2_g[l].reshape(1, D)
        lg, lb = sgu_ln_g[l].reshape(1, D_A), sgu_ln_b[l].reshape(1, D_A)
        bst = sgu_b[l].T
        cb = conv_b[l].reshape(1, D_B)
        clg, clb = conv_ln_g[l].reshape(1, D_B), conv_ln_b[l].reshape(1, D_B)
        if l == 0:
            pass_on("w_in0", xl)
        wg_in = gathered(f"w_in{l}", xl)
        if l == 0:
            pass_on("conv_w", xl)
        h, proj = _fwd_in(xl, g1, wg_in)
        if l == 0:
            cw_full = gathered("conv_w", proj).reshape(N_DEV, DEPTH, CONV_W, D_B // N_DEV).transpose(
                1, 2, 0, 3).reshape(DEPTH, CONV_W, D_B)
        pass_on(f"w_out{l}", proj)
        mix, c, g = _mixer_fwd(proj, lg, lb, sgu_w[l], bst, cw_full[l], cb, clg, clb)
        pass_on(f"w_ff1{l}", mix)
        wout = gathered(f"w_out{l}", mix).reshape(D, D)
        x1, h2 = _fwd_out(mix, wout, xl, g2)
        pass_on(f"w_ff2{l}", h2)
        wg_ff1 = gathered(f"w_ff1{l}", h2)
        r = _fwd_ff1(h2, wg_ff1)
        if l + 1 < DEPTH:
            pass_on(f"w_in{l + 1}", r)
        w2 = gathered(f"w_ff2{l}", r).reshape(D_FF, D)
        x2 = _fwd_ff2(r, w2, x1)
        saved.append(dict(x=xl, h=h, proj=proj, mix=mix, c=c, g=g, x1=x1, h2=h2, r=r, wg_in=wg_in, wout=wout,
                          wg_ff1=wg_ff1, w2=w2, g1=g1, g2=g2, lg=lg, lb=lb, bst=bst, clg=clg, clb=clb))
        xl = x2

    dx, dxb, loss_part, d_final_g = _loss_bwd(xl, final_g.reshape(1, D), tgt)
    loss = lax.psum(loss_part[0, 0], ("x", "y", "c"))

    names = ["w_in", "w_out", "w_ff1", "w_ff2"]
    block = dict(w_in=(D, IN_COLS // N_DEV), w_out=(D // N_DEV, D), w_ff1=(D, D_FF // N_DEV), w_ff2=(D_FF // N_DEV, D))
    land = {k: lax.empty((DEPTH, N_DEV - 1) + block[k], BF16) for k in names}
    big32 = {k: [None] * DEPTH for k in names}
    big16 = {k: [None] * DEPTH for k in names}
    ssend = {k: [None] * DEPTH for k in names}
    srecv = {k: [None] * DEPTH for k in names}

    def send_grad(k, l, g32, g16):
        big32[k][l] = g32.reshape((N_DEV,) + block[k])
        ssend[k][l], srecv[k][l], big16[k][l], land[k] = _scatter_start(
            f"scatter_start_{k}{l}", g16.reshape((N_DEV,) + block[k]), land[k], l)

    small = {}
    d_conv_w = [None] * DEPTH
    for l in reversed(range(DEPTH)):
        s = saved[l]
        send_grad("w_ff2", l, *_wgrad("wgrad_ff2", s["r"], dxb, D, square_a=True))
        df1 = _bwd_ff2(dxb, s["w2"], s["r"], big16["w_ff2"][l])
        send_grad("w_ff1", l, *_wgrad("wgrad_ff1", s["h2"], df1, D_FF // N_DEV))
        dx1, dx1b, dg2 = _bwd_norm_matmul("bwd_ff1", df1, s["wg_ff1"], s["x1"], s["g2"], dx, big16["w_ff1"][l])
        send_grad("w_out", l, *_wgrad("wgrad_out", s["mix"], dx1b, D))
        dpa, dc, dlg, dlb, dwm, dbs, dcb, dclg, dclb = _mixer_bwd_a(
            dx1b, s["wout"], s["proj"], s["c"], s["lg"], s["lb"], sgu_w[l], s["bst"], s["clg"], s["clb"],
            big16["w_out"][l])
        dproj, d_conv_w[l] = _mixer_bwd_b(dc, s["g"], s["proj"], cw_full[l], dpa)
        send_grad("w_in", l, *_wgrad("wgrad_in", s["h"], dproj, IN_COLS // N_DEV))
        dx, dxb, dg1 = _bwd_norm_matmul("bwd_in", dproj, s["wg_in"], s["x"], s["g1"], dx1, big16["w_in"][l])
        small[l] = dict(norm1_g=dg1, sgu_ln_g=dlg, sgu_ln_b=dlb, sgu_w=dwm, sgu_b=dbs, conv_b=dcb, conv_ln_g=dclg,
                        conv_ln_b=dclb, norm2_g=dg2)
    grad_x = dx.reshape(1, T, D)

    own = _own_blocks([big32[k] for k in names])
    recv = _scatter_wait([g for k in names for g in big16[k]], [land[k] for k in names],
                         [q for k in names for q in ssend[k]], [q for k in names for q in srecv[k]], dx)
    wmv = dict(w_in=(w_in, m_w_in, v_w_in), w_out=(w_out, m_w_out, v_w_out), w_ff1=(w_ff1, m_w_ff1, v_w_ff1),
               w_ff2=(w_ff2, m_w_ff2, v_w_ff2))
    res = {}
    for idx, k in enumerate(names):
        res[k] = _adam_sharded("adam_" + k, own[idx], recv[idx], *wmv[k])

    rep = dict(norm1_g=(norm1_g, m_norm1_g, v_norm1_g), sgu_ln_g=(sgu_ln_g, m_sgu_ln_g, v_sgu_ln_g),
               sgu_ln_b=(sgu_ln_b, m_sgu_ln_b, v_sgu_ln_b), sgu_w=(sgu_w, m_sgu_w, v_sgu_w),
               sgu_b=(sgu_b, m_sgu_b, v_sgu_b), conv_b=(conv_b, m_conv_b, v_conv_b),
               conv_ln_g=(conv_ln_g, m_conv_ln_g, v_conv_ln_g), conv_ln_b=(conv_ln_b, m_conv_ln_b, v_conv_ln_b),
               norm2_g=(norm2_g, m_norm2_g, v_norm2_g), final_g=(final_g, m_final_g, v_final_g))
    parts = []
    for k in SMALL:
        if k == "final_g":
            parts.append(d_final_g.reshape(rep[k][0].shape))
        else:
            parts.append(jnp.stack([small[l][k].reshape(rep[k][0].shape[1:]) for l in range(DEPTH)]))
    parts.append(jnp.stack(d_conv_w))
    shapes = [p.shape for p in parts]
    summed = _unpack(_all_reduce_small(_pack(parts)), shapes)
    me = _flat(*_coords())
    g_conv_w = lax.dynamic_slice_in_dim(summed[-1], me * (D_B // N_DEV), D_B // N_DEV, axis=2)
    g_small = summed[:-1] + [g_conv_w]
    keys = SMALL + ["conv_w"]
    rep["conv_w"] = (conv_w, m_conv_w, v_conv_w)
    shapes = [a.shape for a in g_small]
    packed = [_pack(g_small)] + [_pack([rep[k][i] for k in keys]) for i in range(3)]
    d_s, nm_s, nv_s = (_unpack(a, shapes) for a in _adam_small(*packed))
    for i, k in enumerate(keys):
        res[k] = (g_small[i], d_s[i], nm_s[i], nv_s[i])

    order = ["norm1_g", "w_in", "sgu_ln_g", "sgu_ln_b", "sgu_w", "sgu_b", "conv_w", "conv_b", "conv_ln_g",
             "conv_ln_b", "w_out", "norm2_g", "w_ff1", "w_ff2", "final_g"]
    return (loss, grad_x, *[res[k][0] for k in order], *[res[k][1] for k in order],
            *[res[k][2] for k in order], *[res[k][3] for k in order])
```

```python
import functools

import jax
import jax.numpy as jnp
from jax import lax
from jax.experimental import pallas as pl
from jax.experimental.pallas import tpu as pltpu

F32 = jnp.float32
BF16 = jnp.bfloat16

N_DEV = 8
DEPTH = 2
T = 4096
D = 1024
D_A = 512
D_B = 512
CHUNK = 128
H_A = 4
H_B = 4
CONV_W = 31
HALO = 32
D_FF = 4096
IN_COLS = 2048
EPS = 1e-6

ADAM_LR = 0.001
ADAM_B1 = 0.9
ADAM_B2 = 0.999
ADAM_EPS = 1e-08
ADAM_WD = 0.01
ADAM_STEP = 10

TM = 512
TM_FWD = 1024
TT_WGRAD = 1024
TM_MIX = 256
RB = 64
CONV_CHAINS = 2
LANES = 128
MIB = 1024 * 1024
SCOPED_VMEM_MIB = 60

SQRT_HALF = 0.7071067811865476
INV_SQRT_2PI = 0.3989422804014327

MESH_ID = pl.DeviceIdType.MESH
ANY = pl.BlockSpec(memory_space=pl.ANY)
HBM = pl.BlockSpec(memory_space=pltpu.HBM)
SEM = pl.BlockSpec(memory_space=pltpu.SEMAPHORE)
EFFECT = pltpu.SideEffectType.DATAFLOW_SIDE_EFFECTING


def _cparams(vmem_mib, sem=("arbitrary",)):
    assert vmem_mib <= SCOPED_VMEM_MIB
    return pltpu.CompilerParams(dimension_semantics=sem, vmem_limit_bytes=SCOPED_VMEM_MIB * MIB)


def _full(shape):
    return pl.BlockSpec(shape, lambda *_: (0,) * len(shape))


def _rows(tm, cols):
    return pl.BlockSpec((tm, cols), lambda i: (i, 0))


def _gelu(x):
    cdf = 0.5 * (1.0 + lax.erf(x * SQRT_HALF))
    return x * cdf, cdf


def _gelu_grad(x, cdf):
    return cdf + x * (INV_SQRT_2PI * jnp.exp(-0.5 * x * x))


def _sigmoid(x):
    return 1.0 / (1.0 + jnp.exp(-x))


def _ln(x):
    mu = jnp.mean(x, axis=-1, keepdims=True)
    xc = x - mu
    rstd = lax.rsqrt(jnp.mean(xc * xc, axis=-1, keepdims=True) + EPS)
    return xc * rstd, rstd


def _ln_bwd(dyh, xhat, rstd):
    return rstd * (dyh - jnp.mean(dyh, axis=-1, keepdims=True) - xhat * jnp.mean(dyh * xhat, axis=-1, keepdims=True))


def _rms(x):
    return lax.rsqrt(jnp.mean(x * x, axis=-1, keepdims=True) + EPS)


def _rms_bwd(dh, x, r, g):
    n = x * r
    dn = dh * g
    dx = r * (dn - n * jnp.mean(dn * n, axis=-1, keepdims=True))
    return dx, jnp.sum(dh * n, axis=0, keepdims=True)


def _dot(a, b):
    return jnp.dot(a, b, preferred_element_type=F32)


def _dot_nt(a, b):
    return lax.dot_general(a, b, (((1,), (1,)), ((), ())), preferred_element_type=F32)


def _dot_tn(a, b):
    return lax.dot_general(a, b, (((0,), (0,)), ((), ())), preferred_element_type=F32)


def _tril_mask():
    r = lax.broadcasted_iota(jnp.int32, (CHUNK, CHUNK), 0)
    c = lax.broadcasted_iota(jnp.int32, (CHUNK, CHUNK), 1)
    return r >= c


def _fwd_in(x, g1, wg):
    bn = wg.shape[2]

    def body(x_ref, g_ref, w_ref, h_ref, p_ref):
        xv = x_ref[...]
        h = (xv * _rms(xv) * g_ref[...]).astype(BF16)
        h_ref[...] = h
        for j in range(N_DEV):
            p_ref[:, j * bn:(j + 1) * bn] = _dot(h, w_ref[j])

    return pl.pallas_call(
        body, name="fwd_in", grid=(T // TM_FWD,),
        in_specs=[_rows(TM_FWD, D), _full((1, D)), _full(wg.shape)],
        out_specs=[_rows(TM_FWD, D), _rows(TM_FWD, IN_COLS)],
        out_shape=[jax.ShapeDtypeStruct((T, D), BF16), jax.ShapeDtypeStruct((T, IN_COLS), F32)],
        compiler_params=_cparams(32),
    )(x, g1, wg)


def _fill_shift_buffer(sh_ref, row0, value):
    for q in range(sh_ref.shape[1]):
        sh_ref[0, q, row0:row0 + value.shape[0], :] = value[:, q * LANES:(q + 1) * LANES]


def _build_shifts(sh_ref):
    rows = sh_ref.shape[2]
    for p in range(1, 8):
        for q in range(sh_ref.shape[1]):
            sh_ref[p, q, 0:rows - 8, :] = sh_ref[0, q, p:p + rows - 8, :]


def _shifted(sh_ref, q, base, off):
    start = base + (off - off % 8)
    if not isinstance(start, int):
        start = pl.multiple_of(start, 8)
    return sh_ref[off % 8, q, pl.ds(start, RB), :]


def _conv_taps(sh_ref, w_ref, q, base, first_tap_row, step):
    cols = slice(q * LANES, (q + 1) * LANES)
    acc = [jnp.zeros((RB, LANES), F32) for _ in range(CONV_CHAINS)]
    for k in range(CONV_W):
        term = _shifted(sh_ref, q, base, first_tap_row + step * k) * w_ref[k:k + 1, cols]
        acc[k % CONV_CHAINS] = acc[k % CONV_CHAINS] + term
    return functools.reduce(lambda a, b: a + b, acc)


def _mixer_fwd(proj, lg, lb, wm, bst, cw, cb, clg, clb):
    tm = TM_MIX
    hb = tm // HALO

    def body(p_ref, ph_ref, lg_ref, lb_ref, wm_ref, bs_ref, cw_ref, cb_ref, clg_ref, clb_ref,
             mix_ref, c_ref, g_ref, gbuf):
        i = pl.program_id(0)
        u, _ = _gelu(p_ref[:, 0:D_A])
        vg, _ = _gelu(p_ref[:, D_A:2 * D_A])
        xhat, _ = _ln(vg)
        v = (xhat * lg_ref[...] + lb_ref[...]).astype(BF16)
        mask = _tril_mask()
        for h in range(H_A):
            hc = slice(h * CHUNK, (h + 1) * CHUNK)
            wmh = jnp.where(mask, wm_ref[h], 0.0).astype(BF16)
            for c in range(tm // CHUNK):
                rc = slice(c * CHUNK, (c + 1) * CHUNK)
                mixed = _dot(wmh, v[rc, hc]) + bs_ref[:, h:h + 1]
                mix_ref[rc, hc] = (u[rc, hc] * mixed).astype(BF16)

        g = p_ref[:, 2 * D_A:2 * D_A + D_B] * _sigmoid(p_ref[:, 2 * D_A + D_B:IN_COLS])
        g_ref[...] = g
        gh = ph_ref[:, 0:D_B] * _sigmoid(ph_ref[:, D_B:2 * D_B])
        _fill_shift_buffer(gbuf, 0, jnp.where(i > 0, gh, 0.0))
        _fill_shift_buffer(gbuf, HALO, g)
        _build_shifts(gbuf)
        for q in range(H_B):
            cols = slice(q * LANES, (q + 1) * LANES)
            for rb in range(tm // RB):
                acc = _conv_taps(gbuf, cw_ref, q, rb * RB, HALO - (CONV_W - 1), 1)
                c_ref[rb * RB:(rb + 1) * RB, cols] = acc + cb_ref[:, cols]
        for q in range(H_B):
            cols = slice(q * LANES, (q + 1) * LANES)
            chat, _ = _ln(c_ref[:, cols])
            z = chat * clg_ref[:, cols] + clb_ref[:, cols]
            mix_ref[:, D_A + q * LANES:D_A + (q + 1) * LANES] = (z * _sigmoid(z)).astype(BF16)

    vec = _full((1, D_A))
    return pl.pallas_call(
        body, name="mixer_fwd", grid=(T // tm,),
        in_specs=[_rows(tm, IN_COLS),
                  pl.BlockSpec((HALO, 2 * D_B), lambda i: (jnp.maximum(i * hb - 1, 0), 1)),
                  vec, vec, _full((H_A, CHUNK, CHUNK)), _full((CHUNK, H_A)),
                  _full((CONV_W, D_B)), vec, vec, vec],
        out_specs=[_rows(tm, D), _rows(tm, D_B), _rows(tm, D_B)],
        out_shape=[jax.ShapeDtypeStruct((T, D), BF16), jax.ShapeDtypeStruct((T, D_B), F32),
                   jax.ShapeDtypeStruct((T, D_B), F32)],
        scratch_shapes=[pltpu.VMEM((8, H_B, HALO + tm, LANES), F32)],
        compiler_params=_cparams(32),
    )(proj, proj, lg, lb, wm, bst, cw, cb, clg, clb)


def _fwd_out(mix, wout, x, g2):
    def body(m_ref, w_ref, x_ref, g_ref, x1_ref, h2_ref):
        x1 = x_ref[...] + _dot(m_ref[...], w_ref[...])
        x1_ref[...] = x1
        h2_ref[...] = (x1 * _rms(x1) * g_ref[...]).astype(BF16)

    return pl.pallas_call(
        body, name="fwd_out", grid=(T // TM_FWD,),
        in_specs=[_rows(TM_FWD, D), _full((D, D)), _rows(TM_FWD, D), _full((1, D))],
        out_specs=[_rows(TM_FWD, D), _rows(TM_FWD, D)],
        out_shape=[jax.ShapeDtypeStruct((T, D), F32), jax.ShapeDtypeStruct((T, D), BF16)],
        compiler_params=_cparams(32),
    )(mix, wout, x, g2)


def _fwd_ff1(h2, wg):
    bn = wg.shape[2]

    def body(h_ref, w_ref, r_ref):
        h = h_ref[...]
        for j in range(N_DEV):
            r_ref[:, j * bn:(j + 1) * bn] = jnp.maximum(_dot(h, w_ref[j]), 0.0).astype(BF16)

    return pl.pallas_call(
        body, name="fwd_ff1", grid=(T // TM_FWD,),
        in_specs=[_rows(TM_FWD, D), _full(wg.shape)],
        out_specs=_rows(TM_FWD, D_FF),
        out_shape=jax.ShapeDtypeStruct((T, D_FF), BF16),
        compiler_params=_cparams(48),
    )(h2, wg)


def _fwd_ff2(r, w2, x1):
    def body(r_ref, w_ref, x_ref, o_ref):
        rv = r_ref[...]
        o_ref[...] = x_ref[...] + _dot(rv * rv, w_ref[...])

    return pl.pallas_call(
        body, name="fwd_ff2", grid=(T // TM_FWD,),
        in_specs=[_rows(TM_FWD, D_FF), _full((D_FF, D)), _rows(TM_FWD, D)],
        out_specs=_rows(TM_FWD, D),
        out_shape=jax.ShapeDtypeStruct((T, D), F32),
        compiler_params=_cparams(48),
    )(r, w2, x1)


def _loss_bwd(xf, gf, tgt):
    def body(x_ref, g_ref, t_ref, dx_ref, dxb_ref, loss_ref, dg_ref):
        i = pl.program_id(0)

        @pl.when(i == 0)
        def _():
            loss_ref[...] = jnp.zeros(loss_ref.shape, F32)
            dg_ref[...] = jnp.zeros(dg_ref.shape, F32)

        xv = x_ref[...]
        r = _rms(xv)
        diff = xv * r * g_ref[...] - t_ref[...]
        loss_ref[...] += 0.5 * jnp.sum(jnp.mean(diff * diff, axis=-1, keepdims=True), axis=0, keepdims=True)
        dx, dg = _rms_bwd(diff * (1.0 / D), xv, r, g_ref[...])
        dx_ref[...] = dx
        dxb_ref[...] = dx.astype(BF16)
        dg_ref[...] += dg

    return pl.pallas_call(
        body, name="loss_bwd", grid=(T // TM,),
        in_specs=[_rows(TM, D), _full((1, D)), _rows(TM, D)],
        out_specs=[_rows(TM, D), _rows(TM, D), _full((8, LANES)), _full((1, D))],
        out_shape=[jax.ShapeDtypeStruct((T, D), F32), jax.ShapeDtypeStruct((T, D), BF16),
                   jax.ShapeDtypeStruct((8, LANES), F32), jax.ShapeDtypeStruct((1, D), F32)],
        compiler_params=_cparams(32),
    )(xf, gf, tgt)


def _bwd_ff2(dxb, w2, r, dep):
    kb = 1024

    def body(d_ref, w_ref, r_ref, dep_ref, o_ref):
        d = d_ref[...]
        for j in range(D_FF // kb):
            cols = slice(j * kb, (j + 1) * kb)
            df = _dot_nt(d, w_ref[cols, :])
            o_ref[:, cols] = (2.0 * r_ref[:, cols].astype(F32) * df).astype(BF16)

    return pl.pallas_call(
        body, name="bwd_ff2", grid=(T // TM,),
        in_specs=[_rows(TM, D), _full((D_FF, D)), _rows(TM, D_FF), ANY],
        out_specs=_rows(TM, D_FF),
        out_shape=jax.ShapeDtypeStruct((T, D_FF), BF16),
        compiler_params=_cparams(48),
    )(dxb, w2, r, dep)


def _bwd_norm_matmul(name, dy, wg, x, g, dres, dep):
    bn = wg.shape[2]
    ncol = dy.shape[1]

    def body(dy_ref, w_ref, x_ref, g_ref, dres_ref, dep_ref, dx_ref, dxb_ref, dg_ref):
        i = pl.program_id(0)

        @pl.when(i == 0)
        def _():
            dg_ref[...] = jnp.zeros(dg_ref.shape, F32)

        dh = jnp.zeros((TM, D), F32)
        for j in range(N_DEV):
            dh = dh + _dot_nt(dy_ref[:, j * bn:(j + 1) * bn], w_ref[j])
        xv = x_ref[...]
        dxn, dg = _rms_bwd(dh, xv, _rms(xv), g_ref[...])
        dx = dres_ref[...] + dxn
        dx_ref[...] = dx
        dxb_ref[...] = dx.astype(BF16)
        dg_ref[...] += dg

    return pl.pallas_call(
        body, name=name, grid=(T // TM,),
        in_specs=[_rows(TM, ncol), _full(wg.shape), _rows(TM, D), _full((1, D)), _rows(TM, D), ANY],
        out_specs=[_rows(TM, D), _rows(TM, D), _full((1, D))],
        out_shape=[jax.ShapeDtypeStruct((T, D), F32), jax.ShapeDtypeStruct((T, D), BF16),
                   jax.ShapeDtypeStruct((1, D), F32)],
        compiler_params=_cparams(48),
    )(dy, wg, x, g, dres, dep)


def _mixer_bwd_a(dxb, wout, proj, c, lg, lb, wm, bst, clg, clb, dep):
    tm = TM_MIX
    n_tiles = T // tm

    def body(dx_ref, wo_ref, p_ref, c_ref, lg_ref, lb_ref, wm_ref, bs_ref, clg_ref, clb_ref, dep_ref,
             dpa_ref, dc_ref, dlg_ref, dlb_ref, dwm_ref, dbs_ref, dcb_ref, dclg_ref, dclb_ref,
             dv_buf, db_acc):
        i = pl.program_id(0)

        @pl.when(i == 0)
        def _():
            for ref in (dlg_ref, dlb_ref, dwm_ref, dbs_ref, dcb_ref, dclg_ref, dclb_ref, db_acc):
                ref[...] = jnp.zeros(ref.shape, F32)

        dmix = _dot_nt(dx_ref[...], wo_ref[...])
        ua = p_ref[:, 0:D_A]
        va = p_ref[:, D_A:2 * D_A]
        u, cdf_u = _gelu(ua)
        vg, cdf_v = _gelu(va)
        xhat, rstd = _ln(vg)
        v = (xhat * lg_ref[...] + lb_ref[...]).astype(BF16)
        mask = _tril_mask()
        for h in range(H_A):
            hc = slice(h * CHUNK, (h + 1) * CHUNK)
            wmh = jnp.where(mask, wm_ref[h], 0.0).astype(BF16)
            for cidx in range(tm // CHUNK):
                rc = slice(cidx * CHUNK, (cidx + 1) * CHUNK)
                vb = v[rc, hc]
                mixed = _dot(wmh, vb) + bs_ref[:, h:h + 1]
                da = dmix[rc, hc]
                dpa_ref[rc, hc] = (da * mixed * _gelu_grad(ua[rc, hc], cdf_u[rc, hc])).astype(BF16)
                dmixed = da * u[rc, hc]
                dmb = dmixed.astype(BF16)
                dv_buf[rc, hc] = _dot_tn(wmh, dmb)
                dwm_ref[h] += _dot_nt(dmb, vb)
                db_acc[:, hc] += dmixed
        dv = dv_buf[...]
        dlb_ref[...] += jnp.sum(dv, axis=0, keepdims=True)
        dlg_ref[...] += jnp.sum(dv * xhat, axis=0, keepdims=True)
        dvg = _ln_bwd(dv * lg_ref[...], xhat, rstd)
        dpa_ref[:, D_A:2 * D_A] = (dvg * _gelu_grad(va, cdf_v)).astype(BF16)

        for q in range(H_B):
            cols = slice(q * LANES, (q + 1) * LANES)
            chat, crstd = _ln(c_ref[:, cols])
            z = chat * clg_ref[:, cols] + clb_ref[:, cols]
            sg = _sigmoid(z)
            dz = dmix[:, D_A + q * LANES:D_A + (q + 1) * LANES] * (sg * (1.0 + z * (1.0 - sg)))
            dclb_ref[:, cols] += jnp.sum(dz, axis=0, keepdims=True)
            dclg_ref[:, cols] += jnp.sum(dz * chat, axis=0, keepdims=True)
            dc = _ln_bwd(dz * clg_ref[:, cols], chat, crstd)
            dc_ref[:, cols] = dc
            dcb_ref[:, cols] += jnp.sum(dc, axis=0, keepdims=True)

        @pl.when(i == n_tiles - 1)
        def _():
            for h in range(H_A):
                hc = slice(h * CHUNK, (h + 1) * CHUNK)
                dwm_ref[h] = jnp.where(mask, dwm_ref[h], 0.0)
                dbs_ref[h:h + 1, :] = jnp.sum(db_acc[:, hc].T, axis=0, keepdims=True)

    vec = _full((1, D_A))
    vshape = jax.ShapeDtypeStruct((1, D_A), F32)
    return pl.pallas_call(
        body, name="mixer_bwd_a", grid=(n_tiles,),
        in_specs=[_rows(tm, D), _full((D, D)), _rows(tm, IN_COLS), _rows(tm, D_B), vec, vec,
                  _full((H_A, CHUNK, CHUNK)), _full((CHUNK, H_A)), vec, vec, ANY],
        out_specs=[_rows(tm, 2 * D_A), _rows(tm, D_B), vec, vec, _full((H_A, CHUNK, CHUNK)),
                   _full((H_A, CHUNK)), vec, vec, vec],
        out_shape=[jax.ShapeDtypeStruct((T, 2 * D_A), BF16), jax.ShapeDtypeStruct((T, D_B), F32), vshape, vshape,
                   jax.ShapeDtypeStruct((H_A, CHUNK, CHUNK), F32), jax.ShapeDtypeStruct((H_A, CHUNK), F32),
                   vshape, vshape, vshape],
        scratch_shapes=[pltpu.VMEM((tm, D_A), F32), pltpu.VMEM((CHUNK, D_A), F32)],
        compiler_params=_cparams(32),
    )(dxb, wout, proj, c, lg, lb, wm, bst, clg, clb, dep)


def _mixer_bwd_b(dc, g, proj, cw, dpa):
    tm = TM_MIX
    n_tiles = T // tm
    hb = tm // HALO

    def body(dc_ref, dch_ref, g_ref, gh_ref, p_ref, cw_ref, dpa_ref, dp_ref, dcw_ref, dcbuf, gbuf, dwacc):
        i = pl.program_id(0)

        @pl.when(i == 0)
        def _():
            dwacc[...] = jnp.zeros(dwacc.shape, F32)

        _fill_shift_buffer(dcbuf, 0, dc_ref[...])
        _fill_shift_buffer(dcbuf, tm, jnp.where(i < n_tiles - 1, dch_ref[...], 0.0))
        _fill_shift_buffer(gbuf, 0, jnp.where(i > 0, gh_ref[...], 0.0))
        _fill_shift_buffer(gbuf, HALO, g_ref[...])
        _build_shifts(dcbuf)
        _build_shifts(gbuf)
        dp_ref[:, 0:2 * D_A] = dpa_ref[...]
        for q in range(H_B):
            cols = slice(q * LANES, (q + 1) * LANES)

            def row_block(rb, carry, q=q, cols=cols):
                base = pl.multiple_of(rb * RB, RB)
                rows = pl.ds(base, RB)
                dg = _conv_taps(dcbuf, cw_ref, q, base, CONV_W - 1, -1)
                val = p_ref[rows, cols]
                sg = _sigmoid(p_ref[rows, D_B + q * LANES:D_B + (q + 1) * LANES])
                dp_ref[rows, 2 * D_A + q * LANES:2 * D_A + (q + 1) * LANES] = (dg * sg).astype(BF16)
                dp_ref[rows, 2 * D_A + D_B + q * LANES:2 * D_A + D_B + (q + 1) * LANES] = (
                    dg * val * sg * (1.0 - sg)).astype(BF16)
                dcv = dcbuf[0, q, rows, :]
                parts = []
                for k in range(CONV_W):
                    prod = dcv * _shifted(gbuf, q, base, HALO - (CONV_W - 1) + k)
                    parts.append(jnp.sum(prod.reshape(RB // 8, 8, LANES), axis=0))
                for k in range(CONV_W):
                    dwacc[k * 8:(k + 1) * 8, cols] += parts[k]
                return carry

            lax.fori_loop(0, tm // RB, row_block, 0)

        @pl.when(i == n_tiles - 1)
        def _():
            for k in range(CONV_W):
                dcw_ref[k:k + 1, :] = jnp.sum(dwacc[k * 8:(k + 1) * 8, :], axis=0, keepdims=True)

    return pl.pallas_call(
        body, name="mixer_bwd_b", grid=(n_tiles,),
        in_specs=[_rows(tm, D_B),
                  pl.BlockSpec((HALO, D_B), lambda i: (jnp.minimum((i + 1) * hb, T // HALO - 1), 0)),
                  _rows(tm, D_B),
                  pl.BlockSpec((HALO, D_B), lambda i: (jnp.maximum(i * hb - 1, 0), 0)),
                  pl.BlockSpec((tm, 2 * D_B), lambda i: (i, 1)),
                  _full((CONV_W, D_B)), _rows(tm, 2 * D_A)],
        out_specs=[_rows(tm, IN_COLS), _full((CONV_W, D_B))],
        out_shape=[jax.ShapeDtypeStruct((T, IN_COLS), BF16), jax.ShapeDtypeStruct((CONV_W, D_B), F32)],
        scratch_shapes=[pltpu.VMEM((8, H_B, tm + HALO, LANES), F32), pltpu.VMEM((8, H_B, HALO + tm, LANES), F32),
                        pltpu.VMEM((CONV_W * 8, D_B), F32)],
        compiler_params=_cparams(32),
    )(dc, dc, g, g, proj, cw, dpa)


def _wgrad(name, a, g, bn, square_a=False):
    k = a.shape[1]
    n = g.shape[1]
    tk = min(k, 1024)
    tn = min(n, max(bn, 1024))
    nsub = tn // bn
    tt = TT_WGRAD
    nt = T // tt

    def body(a_ref, g_ref, o_ref, ob_ref):
        t = pl.program_id(2)

        @pl.when(t == 0)
        def _():
            o_ref[...] = jnp.zeros(o_ref.shape, F32)

        av = a_ref[...]
        if square_a:
            av = av * av
        for s in range(nsub):
            o_ref[s] += _dot_tn(av, g_ref[:, s * bn:(s + 1) * bn])

        @pl.when(t == nt - 1)
        def _():
            ob_ref[...] = o_ref[...].astype(BF16)

    ospec = pl.BlockSpec((nsub, tk, bn), lambda ki, ni, ti: (ni, ki, 0))
    return pl.pallas_call(
        body, name=name, grid=(k // tk, n // tn, nt),
        in_specs=[pl.BlockSpec((tt, tk), lambda ki, ni, ti: (ti, ki)),
                  pl.BlockSpec((tt, tn), lambda ki, ni, ti: (ti, ni))],
        out_specs=[ospec, ospec],
        out_shape=[jax.ShapeDtypeStruct((n // bn, k, bn), F32), jax.ShapeDtypeStruct((n // bn, k, bn), BF16)],
        compiler_params=_cparams(40, ("arbitrary", "arbitrary", "arbitrary")),
    )(a, g)


def _coords():
    return lax.axis_index("x"), lax.axis_index("y"), lax.axis_index("c")


def _flat(x, y, c):
    return 4 * x + 2 * y + c


def _peer(k):
    x, y, c = _coords()
    return (x ^ ((k >> 2) & 1), y ^ ((k >> 1) & 1), c ^ (k & 1))


def _hbm(a):
    return pltpu.with_memory_space_constraint(a, pltpu.HBM)


def _hbm_like(a):
    return pltpu.HBM(a.shape, a.dtype)


def _peer_sems():
    return pltpu.SemaphoreType.DMA((N_DEV - 1,))


def _place_own(shards):
    n = len(shards)

    def body(*refs):
        ins, outs = refs[:n], refs[n:2 * n]
        stage_in, stage_out = refs[2 * n:3 * n], refs[3 * n:4 * n]
        in_sems, out_sems = refs[4 * n], refs[4 * n + 1]
        me = _flat(*_coords())
        loads = [pltpu.make_async_copy(ins[a], stage_in[a], in_sems.at[a]) for a in range(n)]
        stores = [pltpu.make_async_copy(stage_out[a], outs[a].at[me], out_sems.at[a]) for a in range(n)]
        for cp in loads:
            cp.start()
        for a in range(n):
            loads[a].wait()
            stage_out[a][...] = stage_in[a][...].astype(stage_out[a].dtype)
            stores[a].start()
        for cp in stores:
            cp.wait()

    return pl.pallas_call(
        body, name="place_own", in_specs=[ANY] * n, out_specs=[ANY] * n,
        out_shape=[jax.ShapeDtypeStruct((N_DEV,) + s.shape, dt) for s, dt in shards],
        scratch_shapes=[*[pltpu.VMEM(s.shape, s.dtype) for s, _ in shards],
                        *[pltpu.VMEM(s.shape, dt) for s, dt in shards],
                        pltpu.SemaphoreType.DMA((n,)), pltpu.SemaphoreType.DMA((n,))],
        compiler_params=pltpu.CompilerParams(vmem_limit_bytes=40 * MIB),
    )(*[s for s, _ in shards])


SIBLING = 1
CHIP_PEERS = (2, 4, 6)
FIRST_PEERS = (SIBLING,) + CHIP_PEERS


def _gather_start(lands):
    n = len(lands)

    def body(*refs):
        lnd, send, recv = refs[:n], refs[n:2 * n], refs[2 * n:3 * n]
        me = _flat(*_coords())
        for a in range(n):
            for j, k in enumerate(FIRST_PEERS):
                pltpu.make_async_remote_copy(
                    src_ref=lnd[a].at[me], dst_ref=lnd[a].at[me], send_sem=send[a].at[j],
                    recv_sem=recv[a].at[j], device_id=_peer(k), device_id_type=MESH_ID).start()

    sems = pltpu.SemaphoreType.DMA((len(FIRST_PEERS),))
    outs = pl.pallas_call(
        body, name="gather_start",
        out_shape=(*[sems] * (2 * n), *[_hbm_like(l) for l in lands]),
        in_specs=[HBM] * n, out_specs=(*[SEM] * (2 * n), *[HBM] * n),
        input_output_aliases={i: 2 * n + i for i in range(n)},
        compiler_params=pltpu.CompilerParams(has_side_effects=EFFECT),
    )(*[_hbm(l) for l in lands])
    return outs[:n], outs[n:2 * n], outs[2 * n:]


def _gather_pass_on(name, land, send, recv, after):
    def body(l_ref, send_ref, recv_ref, after_ref, send2, recv2, l_out):
        sibling = _peer(SIBLING)
        for j, k in enumerate(CHIP_PEERS):
            cp = pltpu.make_async_remote_copy(
                src_ref=l_ref.at[0], dst_ref=l_ref.at[0], send_sem=send_ref.at[1 + j], recv_sem=recv_ref.at[1 + j],
                device_id=_peer(k), device_id_type=MESH_ID)
            cp.wait_send()
            cp.wait_recv()
            blk = _flat(*_peer(k))
            pltpu.make_async_remote_copy(
                src_ref=l_ref.at[blk], dst_ref=l_ref.at[blk], send_sem=send2.at[j], recv_sem=recv2.at[j],
                device_id=sibling, device_id_type=MESH_ID).start()

    sems = pltpu.SemaphoreType.DMA((len(CHIP_PEERS),))
    return pl.pallas_call(
        body, name=name, out_shape=(sems, sems, _hbm_like(land)),
        in_specs=(HBM, SEM, SEM, ANY), out_specs=(SEM, SEM, HBM), input_output_aliases={0: 2},
        compiler_params=pltpu.CompilerParams(has_side_effects=EFFECT),
    )(land, send, recv, after)


def _gather_wait(name, land, send, recv, send2, recv2, after):
    def body(l_ref, send_ref, recv_ref, send2_ref, recv2_ref, after_ref, l_out):
        sibling = _peer(SIBLING)
        own = pltpu.make_async_remote_copy(
            src_ref=l_ref.at[0], dst_ref=l_ref.at[0], send_sem=send_ref.at[0], recv_sem=recv_ref.at[0],
            device_id=sibling, device_id_type=MESH_ID)
        own.wait_send()
        own.wait_recv()
        for j in range(len(CHIP_PEERS)):
            cp = pltpu.make_async_remote_copy(
                src_ref=l_ref.at[0], dst_ref=l_ref.at[0], send_sem=send2_ref.at[j], recv_sem=recv2_ref.at[j],
                device_id=sibling, device_id_type=MESH_ID)
            cp.wait_send()
            cp.wait_recv()

    return pl.pallas_call(
        body, name=name, out_shape=_hbm_like(land),
        in_specs=(HBM, SEM, SEM, SEM, SEM, ANY), out_specs=HBM, input_output_aliases={0: 0},
        compiler_params=pltpu.CompilerParams(has_side_effects=EFFECT),
    )(land, send, recv, send2, recv2, after)


def _scatter_start(name, g16, land, layer):
    def body(g_ref, land_ref, send, recv, g_out, land_out):
        for k in range(1, N_DEV):
            to = _peer(k)
            pltpu.make_async_remote_copy(
                src_ref=g_ref.at[_flat(*to)], dst_ref=land_ref.at[layer, k - 1],
                send_sem=send.at[k - 1], recv_sem=recv.at[k - 1], device_id=to, device_id_type=MESH_ID).start()

    return pl.pallas_call(
        body, name=name,
        out_shape=(_peer_sems(), _peer_sems(), _hbm_like(g16), _hbm_like(land)),
        in_specs=[HBM, HBM], out_specs=(SEM, SEM, HBM, HBM), input_output_aliases={0: 2, 1: 3},
        compiler_params=pltpu.CompilerParams(has_side_effects=EFFECT),
    )(_hbm(g16), _hbm(land))


def _scatter_wait(grads, lands, sends, recvs, after):
    n, nw = len(grads), len(lands)

    def body(*refs):
        g, lnd = refs[:n], refs[n:n + nw]
        send, recv = refs[n + nw:2 * n + nw], refs[2 * n + nw:3 * n + nw]
        for a in range(n):
            for k in range(1, N_DEV):
                cp = pltpu.make_async_remote_copy(
                    src_ref=g[a].at[0], dst_ref=lnd[a // DEPTH].at[a % DEPTH, 0],
                    send_sem=send[a].at[k - 1], recv_sem=recv[a].at[k - 1],
                    device_id=_peer(k), device_id_type=MESH_ID)
                cp.wait_send()
                cp.wait_recv()

    outs = pl.pallas_call(
        body, name="scatter_wait",
        out_shape=(*[_hbm_like(g) for g in grads], *[_hbm_like(l) for l in lands]),
        in_specs=(*[HBM] * (n + nw), *[SEM] * (2 * n), ANY), out_specs=[HBM] * (n + nw),
        input_output_aliases={i: i for i in range(n + nw)},
        compiler_params=pltpu.CompilerParams(has_side_effects=EFFECT),
    )(*grads, *lands, *sends, *recvs, after)
    return outs[n:]


def _own_blocks(grads_f32):
    nw = len(grads_f32)
    flat32 = [g for w in grads_f32 for g in w]
    n = len(flat32)

    def body(*refs):
        g32, own, stage = refs[:n], refs[n:n + nw], refs[n + nw:2 * n + nw]
        in_sems, out_sems = refs[2 * n + nw], refs[2 * n + nw + 1]
        me = _flat(*_coords())
        loads = [pltpu.make_async_copy(g32[a].at[me], stage[a], in_sems.at[a]) for a in range(n)]
        stores = [pltpu.make_async_copy(stage[a], own[a // DEPTH].at[a % DEPTH], out_sems.at[a]) for a in range(n)]
        for cp in loads:
            cp.start()
        for a in range(n):
            loads[a].wait()
            stores[a].start()
        for cp in stores:
            cp.wait()

    return pl.pallas_call(
        body, name="own_blocks", in_specs=[ANY] * n, out_specs=[ANY] * nw,
        out_shape=[jax.ShapeDtypeStruct((DEPTH,) + w[0].shape[1:], F32) for w in grads_f32],
        scratch_shapes=[*[pltpu.VMEM(g.shape[1:], F32) for g in flat32],
                        pltpu.SemaphoreType.DMA((n,)), pltpu.SemaphoreType.DMA((n,))],
        compiler_params=pltpu.CompilerParams(vmem_limit_bytes=24 * MIB),
    )(*flat32)


def _all_reduce_small(part):
    rows = part.shape[0]
    br = rows // N_DEV
    assert br * N_DEV == rows and br % 8 == 0

    def body(p_ref, o_ref, slots, send1, recv1, send2, recv2):
        me = _flat(*_coords())

        def block(ref, d):
            return ref.at[pl.ds(pl.multiple_of(d * br, 8), br), :]

        slots[me] = p_ref[pl.ds(pl.multiple_of(me * br, 8), br), :]
        scatter, gather = [], []
        for k in range(1, N_DEV):
            to = _peer(k)
            scatter.append(pltpu.make_async_remote_copy(
                src_ref=block(p_ref, _flat(*to)), dst_ref=slots.at[me],
                send_sem=send1.at[k - 1], recv_sem=recv1.at[k - 1], device_id=to, device_id_type=MESH_ID))
            gather.append(pltpu.make_async_remote_copy(
                src_ref=block(o_ref, me), dst_ref=block(o_ref, me),
                send_sem=send2.at[k - 1], recv_sem=recv2.at[k - 1], device_id=to, device_id_type=MESH_ID))
        for cp in scatter:
            cp.start()
        for cp in scatter:
            cp.wait()
        acc = slots[0]
        for d in range(1, N_DEV):
            acc = acc + slots[d]
        o_ref[pl.ds(pl.multiple_of(me * br, 8), br), :] = acc
        for cp in gather:
            cp.start()
        for cp in gather:
            cp.wait()

    return pl.pallas_call(
        body, name="all_reduce_small",
        in_specs=[pl.BlockSpec(memory_space=pltpu.VMEM)], out_specs=pl.BlockSpec(memory_space=pltpu.VMEM),
        out_shape=jax.ShapeDtypeStruct(part.shape, F32),
        scratch_shapes=[pltpu.VMEM((N_DEV, br, LANES), F32)] + [pltpu.SemaphoreType.DMA((N_DEV - 1,))] * 4,
        compiler_params=pltpu.CompilerParams(vmem_limit_bytes=24 * MIB),
    )(part)


def _adam_math(w, g, m, v):
    m = ADAM_B1 * m + (1.0 - ADAM_B1) * g
    v = ADAM_B2 * v + (1.0 - ADAM_B2) * (g * g)
    m_hat = m / (1.0 - ADAM_B1 ** ADAM_STEP)
    v_hat = v / (1.0 - ADAM_B2 ** ADAM_STEP)
    delta = -ADAM_LR * (m_hat / (jnp.sqrt(v_hat) + ADAM_EPS) + ADAM_WD * w)
    return delta, m, v


def _adam_sharded(name, own, recv, w, m, v):
    _, r, c = w.shape
    tr = min(r, 256)

    def body(own_ref, recv_ref, w_ref, m_ref, v_ref, g_ref, d_ref, nm_ref, nv_ref):
        g = own_ref[...]
        for k in range(N_DEV - 1):
            g = g + recv_ref[k].astype(F32)
        delta, nm, nv = _adam_math(w_ref[...], g, m_ref[...], v_ref[...])
        g_ref[...] = g
        d_ref[...] = delta
        nm_ref[...] = nm
        nv_ref[...] = nv

    blk = pl.BlockSpec((None, tr, c), lambda l, i: (l, i, 0))
    shp = jax.ShapeDtypeStruct(w.shape, F32)
    return pl.pallas_call(
        body, name=name, grid=(DEPTH, r // tr),
        in_specs=[blk, pl.BlockSpec((None, N_DEV - 1, tr, c), lambda l, i: (l, 0, i, 0)), blk, blk, blk],
        out_specs=[blk] * 4, out_shape=[shp] * 4,
        compiler_params=_cparams(32, ("arbitrary", "arbitrary")),
    )(own, recv, w, m, v)


def _adam_small(g, w, m, v):
    rows = g.shape[0]

    def body(g_ref, w_ref, m_ref, v_ref, d_ref, nm_ref, nv_ref):
        delta, nm, nv = _adam_math(w_ref[...], g_ref[...], m_ref[...], v_ref[...])
        d_ref[...] = delta
        nm_ref[...] = nm
        nv_ref[...] = nv

    spec = _full((rows, LANES))
    shp = jax.ShapeDtypeStruct((rows, LANES), F32)
    return pl.pallas_call(
        body, name="adam_small", grid=(1,),
        in_specs=[spec] * 4, out_specs=[spec] * 3, out_shape=[shp] * 3,
        compiler_params=_cparams(24),
    )(g, w, m, v)


SMALL = ["norm1_g", "sgu_ln_g", "sgu_ln_b", "sgu_w", "sgu_b", "conv_b", "conv_ln_g", "conv_ln_b", "norm2_g",
         "final_g"]


def _pack(arrays):
    flat = jnp.concatenate([a.reshape(-1) for a in arrays])
    pad = (-flat.shape[0]) % (N_DEV * 8 * LANES)
    return jnp.pad(flat, (0, pad)).reshape(-1, LANES)


def _unpack(packed, shapes):
    flat = packed.reshape(-1)
    out, off = [], 0
    for s in shapes:
        size = 1
        for d in s:
            size *= d
        out.append(flat[off:off + size].reshape(s))
        off += size
    return out


def kernel(x, norm1_g, w_in, sgu_ln_g, sgu_ln_b, sgu_w, sgu_b, conv_w, conv_b, conv_ln_g, conv_ln_b, w_out, norm2_g, w_ff1, w_ff2, final_g, loss_target, m_norm1_g, m_w_in, m_sgu_ln_g, m_sgu_ln_b, m_sgu_w, m_sgu_b, m_conv_w, m_conv_b, m_conv_ln_g, m_conv_ln_b, m_w_out, m_norm2_g, m_w_ff1, m_w_ff2, m_final_g, v_norm1_g, v_w_in, v_sgu_ln_g, v_sgu_ln_b, v_sgu_w, v_sgu_b, v_conv_w, v_conv_b, v_conv_ln_g, v_conv_ln_b, v_w_out, v_norm2_g, v_w_ff1, v_w_ff2, v_final_g):
    x2d = x.reshape(T, D)
    tgt = loss_target.reshape(T, D)
    cw_shard = conv_w.reshape(CONV_W, LANES)

    gnames, shards = [], []
    for l in range(DEPTH):
        for k, w in (("w_in", w_in), ("w_out", w_out), ("w_ff1", w_ff1), ("w_ff2", w_ff2)):
            gnames.append(f"{k}{l}")
            shards.append((w[l], BF16))
        if l == 0:
            gnames.insert(1, "conv_w")
            shards.insert(1, (cw_shard, F32))
    sends, recvs, lands = _gather_start(_place_own(shards))
    gidx = {k: i for i, k in enumerate(gnames)}

    passed = {}

    def pass_on(k, after):
        i = gidx[k]
        passed[k] = _gather_pass_on("gather_pass_on_" + k, lands[i], sends[i], recvs[i], after)

    def gathered(k, after):
        i = gidx[k]
        send2, recv2, land = passed[k]
        return _gather_wait("gather_wait_" + k, land, sends[i], recvs[i], send2, recv2, after)

    saved = []
    xl = x2d
    cw_full = None
    for l in range(DEPTH):
        g1 = norm1_g[l].reshape(1, D)
        g2 = norm2_g[l].reshape(1, D)
        lg, lb = sgu_ln_g[l].reshape(1, D_A), sgu_ln_b[l].reshape(1, D_A)
        bst = sgu_b[l].T
        cb = conv_b[l].reshape(1, D_B)
        clg, clb = conv_ln_g[l].reshape(1, D_B), conv_ln_b[l].reshape(1, D_B)
        if l == 0:
            pass_on("w_in0", xl)
        wg_in = gathered(f"w_in{l}", xl)
        if l == 0:
            pass_on("conv_w", xl)
        h, proj = _fwd_in(xl, g1, wg_in)
        if l == 0:
            cw_full = gathered("conv_w", proj).reshape(N_DEV, DEPTH, CONV_W, D_B // N_DEV).transpose(
                1, 2, 0, 3).reshape(DEPTH, CONV_W, D_B)
        pass_on(f"w_out{l}", proj)
        mix, c, g = _mixer_fwd(proj, lg, lb, sgu_w[l], bst, cw_full[l], cb, clg, clb)
        pass_on(f"w_ff1{l}", mix)
        wout = gathered(f"w_out{l}", mix).reshape(D, D)
        x1, h2 = _fwd_out(mix, wout, xl, g2)
        pass_on(f"w_ff2{l}", h2)
        wg_ff1 = gathered(f"w_ff1{l}", h2)
        r = _fwd_ff1(h2, wg_ff1)
        if l + 1 < DEPTH:
            pass_on(f"w_in{l + 1}", r)
        w2 = gathered(f"w_ff2{l}", r).reshape(D_FF, D)
        x2 = _fwd_ff2(r, w2, x1)
        saved.append(dict(x=xl, h=h, proj=proj, mix=mix, c=c, g=g, x1=x1, h2=h2, r=r, wg_in=wg_in, wout=wout,
                          wg_ff1=wg_ff1, w2=w2, g1=g1, g2=g2, lg=lg, lb=lb, bst=bst, clg=clg, clb=clb))
        xl = x2

    dx, dxb, loss_part, d_final_g = _loss_bwd(xl, final_g.reshape(1, D), tgt)
    loss = lax.psum(loss_part[0, 0], ("x", "y", "c"))

    names = ["w_in", "w_out", "w_ff1", "w_ff2"]
    block = dict(w_in=(D, IN_COLS // N_DEV), w_out=(D // N_DEV, D), w_ff1=(D, D_FF // N_DEV), w_ff2=(D_FF // N_DEV, D))
    land = {k: lax.empty((DEPTH, N_DEV - 1) + block[k], BF16) for k in names}
    big32 = {k: [None] * DEPTH for k in names}
    big16 = {k: [None] * DEPTH for k in names}
    ssend = {k: [None] * DEPTH for k in names}
    srecv = {k: [None] * DEPTH for k in names}

    def send_grad(k, l, g32, g16):
        big32[k][l] = g32.reshape((N_DEV,) + block[k])
        ssend[k][l], srecv[k][l], big16[k][l], land[k] = _scatter_start(
            f"scatter_start_{k}{l}", g16.reshape((N_DEV,) + block[k]), land[k], l)

    small = {}
    d_conv_w = [None] * DEPTH
    for l in reversed(range(DEPTH)):
        s = saved[l]
        send_grad("w_ff2", l, *_wgrad("wgrad_ff2", s["r"], dxb, D, square_a=True))
        df1 = _bwd_ff2(dxb, s["w2"], s["r"], big16["w_ff2"][l])
        send_grad("w_ff1", l, *_wgrad("wgrad_ff1", s["h2"], df1, D_FF // N_DEV))
        dx1, dx1b, dg2 = _bwd_norm_matmul("bwd_ff1", df1, s["wg_ff1"], s["x1"], s["g2"], dx, big16["w_ff1"][l])
        send_grad("w_out", l, *_wgrad("wgrad_out", s["mix"], dx1b, D))
        dpa, dc, dlg, dlb, dwm, dbs, dcb, dclg, dclb = _mixer_bwd_a(
            dx1b, s["wout"], s["proj"], s["c"], s["lg"], s["lb"], sgu_w[l], s["bst"], s["clg"], s["clb"],
            big16["w_out"][l])
        dproj, d_conv_w[l] = _mixer_bwd_b(dc, s["g"], s["proj"], cw_full[l], dpa)
        send_grad("w_in", l, *_wgrad("wgrad_in", s["h"], dproj, IN_COLS // N_DEV))
        dx, dxb, dg1 = _bwd_norm_matmul("bwd_in", dproj, s["wg_in"], s["x"], s["g1"], dx1, big16["w_in"][l])
        small[l] = dict(norm1_g=dg1, sgu_ln_g=dlg, sgu_ln_b=dlb, sgu_w=dwm, sgu_b=dbs, conv_b=dcb, conv_ln_g=dclg,
                        conv_ln_b=dclb, norm2_g=dg2)
    grad_x = dx.reshape(1, T, D)

    own = _own_blocks([big32[k] for k in names])
    recv = _scatter_wait([g for k in names for g in big16[k]], [land[k] for k in names],
                         [q for k in names for q in ssend[k]], [q for k in names for q in srecv[k]], dx)
    wmv = dict(w_in=(w_in, m_w_in, v_w_in), w_out=(w_out, m_w_out, v_w_out), w_ff1=(w_ff1, m_w_ff1, v_w_ff1),
               w_ff2=(w_ff2, m_w_ff2, v_w_ff2))
    res = {}
    for idx, k in enumerate(names):
        res[k] = _adam_sharded("adam_" + k, own[idx], recv[idx], *wmv[k])

    rep = dict(norm1_g=(norm1_g, m_norm1_g, v_norm1_g), sgu_ln_g=(sgu_ln_g, m_sgu_ln_g, v_sgu_ln_g),
               sgu_ln_b=(sgu_ln_b, m_sgu_ln_b, v_sgu_ln_b), sgu_w=(sgu_w, m_sgu_w, v_sgu_w),
               sgu_b=(sgu_b, m_sgu_b, v_sgu_b), conv_b=(conv_b, m_conv_b, v_conv_b),
               conv_ln_g=(conv_ln_g, m_conv_ln_g, v_conv_ln_g), conv_ln_b=(conv_ln_b, m_conv_ln_b, v_conv_ln_b),
               norm2_g=(norm2_g, m_norm2_g, v_norm2_g), final_g=(final_g, m_final_g, v_final_g))
    parts = []
    for k in SMALL:
        if k == "final_g":
            parts.append(d_final_g.reshape(rep[k][0].shape))
        else:
            parts.append(jnp.stack([small[l][k].reshape(rep[k][0].shape[1:]) for l in range(DEPTH)]))
    parts.append(jnp.stack(d_conv_w))
    shapes = [p.shape for p in parts]
    summed = _unpack(_all_reduce_small(_pack(parts)), shapes)
    me = _flat(*_coords())
    g_conv_w = lax.dynamic_slice_in_dim(summed[-1], me * (D_B // N_DEV), D_B // N_DEV, axis=2)
    g_small = summed[:-1] + [g_conv_w]
    keys = SMALL + ["conv_w"]
    rep["conv_w"] = (conv_w, m_conv_w, v_conv_w)
    shapes = [a.shape for a in g_small]
    packed = [_pack(g_small)] + [_pack([rep[k][i] for k in keys]) for i in range(3)]
    d_s, nm_s, nv_s = (_unpack(a, shapes) for a in _adam_small(*packed))
    for i, k in enumerate(keys):
        res[k] = (g_small[i], d_s[i], nm_s[i], nv_s[i])

    order = ["norm1_g", "w_in", "sgu_ln_g", "sgu_ln_b", "sgu_w", "sgu_b", "conv_w", "conv_b", "conv_ln_g",
             "conv_ln_b", "w_out", "norm2_g", "w_ff1", "w_ff2", "final_g"]
    return (loss, grad_x, *[res[k][0] for k in order], *[res[k][1] for k in order],
            *[res[k][2] for k in order], *[res[k][3] for k in order])
```

```python
import functools

import jax
import jax.numpy as jnp
from jax import lax
from jax.experimental import pallas as pl
from jax.experimental.pallas import tpu as pltpu

F32 = jnp.float32
BF16 = jnp.bfloat16

N_DEV = 8
DEPTH = 2
T = 4096
D = 1024
D_A = 512
D_B = 512
CHUNK = 128
H_A = 4
H_B = 4
CONV_W = 31
HALO = 32
D_FF = 4096
IN_COLS = 2048
EPS = 1e-6

ADAM_LR = 0.001
ADAM_B1 = 0.9
ADAM_B2 = 0.999
ADAM_EPS = 1e-08
ADAM_WD = 0.01
ADAM_STEP = 10

TM = 512
TM_FWD = 1024
TT_WGRAD = 1024
TM_MIX = 256
RB = 64
CONV_CHAINS = 2
LANES = 128
MIB = 1024 * 1024
SCOPED_VMEM_MIB = 60

SQRT_HALF = 0.7071067811865476
INV_SQRT_2PI = 0.3989422804014327

MESH_ID = pl.DeviceIdType.MESH
ANY = pl.BlockSpec(memory_space=pl.ANY)
HBM = pl.BlockSpec(memory_space=pltpu.HBM)
SEM = pl.BlockSpec(memory_space=pltpu.SEMAPHORE)
EFFECT = pltpu.SideEffectType.DATAFLOW_SIDE_EFFECTING


def _cparams(vmem_mib, sem=("arbitrary",)):
    assert vmem_mib <= SCOPED_VMEM_MIB
    return pltpu.CompilerParams(dimension_semantics=sem, vmem_limit_bytes=SCOPED_VMEM_MIB * MIB)


def _full(shape):
    return pl.BlockSpec(shape, lambda *_: (0,) * len(shape))


def _rows(tm, cols):
    return pl.BlockSpec((tm, cols), lambda i: (i, 0))


def _gelu(x):
    cdf = 0.5 * (1.0 + lax.erf(x * SQRT_HALF))
    return x * cdf, cdf


def _gelu_grad(x, cdf):
    return cdf + x * (INV_SQRT_2PI * jnp.exp(-0.5 * x * x))


def _sigmoid(x):
    return 1.0 / (1.0 + jnp.exp(-x))


def _ln(x):
    mu = jnp.mean(x, axis=-1, keepdims=True)
    xc = x - mu
    rstd = lax.rsqrt(jnp.mean(xc * xc, axis=-1, keepdims=True) + EPS)
    return xc * rstd, rstd


def _ln_bwd(dyh, xhat, rstd):
    return rstd * (dyh - jnp.mean(dyh, axis=-1, keepdims=True) - xhat * jnp.mean(dyh * xhat, axis=-1, keepdims=True))


def _rms(x):
    return lax.rsqrt(jnp.mean(x * x, axis=-1, keepdims=True) + EPS)


def _rms_bwd(dh, x, r, g):
    n = x * r
    dn = dh * g
    dx = r * (dn - n * jnp.mean(dn * n, axis=-1, keepdims=True))
    return dx, jnp.sum(dh * n, axis=0, keepdims=True)


def _dot(a, b):
    return jnp.dot(a, b, preferred_element_type=F32)


def _dot_nt(a, b):
    return lax.dot_general(a, b, (((1,), (1,)), ((), ())), preferred_element_type=F32)


def _dot_tn(a, b):
    return lax.dot_general(a, b, (((0,), (0,)), ((), ())), preferred_element_type=F32)


def _tril_mask():
    r = lax.broadcasted_iota(jnp.int32, (CHUNK, CHUNK), 0)
    c = lax.broadcasted_iota(jnp.int32, (CHUNK, CHUNK), 1)
    return r >= c


def _fwd_in(x, g1, wg):
    bn = wg.shape[2]

    def body(x_ref, g_ref, w_ref, h_ref, p_ref):
        xv = x_ref[...]
        h = (xv * _rms(xv) * g_ref[...]).astype(BF16)
        h_ref[...] = h
        for j in range(N_DEV):
            p_ref[:, j * bn:(j + 1) * bn] = _dot(h, w_ref[j])

    return pl.pallas_call(
        body, name="fwd_in", grid=(T // TM_FWD,),
        in_specs=[_rows(TM_FWD, D), _full((1, D)), _full(wg.shape)],
        out_specs=[_rows(TM_FWD, D), _rows(TM_FWD, IN_COLS)],
        out_shape=[jax.ShapeDtypeStruct((T, D), BF16), jax.ShapeDtypeStruct((T, IN_COLS), F32)],
        compiler_params=_cparams(32),
    )(x, g1, wg)


def _fill_shift_buffer(sh_ref, row0, value):
    for q in range(sh_ref.shape[1]):
        sh_ref[0, q, row0:row0 + value.shape[0], :] = value[:, q * LANES:(q + 1) * LANES]


def _build_shifts(sh_ref):
    rows = sh_ref.shape[2]
    for p in range(1, 8):
        for q in range(sh_ref.shape[1]):
            sh_ref[p, q, 0:rows - 8, :] = sh_ref[0, q, p:p + rows - 8, :]


def _shifted(sh_ref, q, base, off):
    start = base + (off - off % 8)
    if not isinstance(start, int):
        start = pl.multiple_of(start, 8)
    return sh_ref[off % 8, q, pl.ds(start, RB), :]


def _conv_taps(sh_ref, w_ref, q, base, first_tap_row, step):
    cols = slice(q * LANES, (q + 1) * LANES)
    acc = [jnp.zeros((RB, LANES), F32) for _ in range(CONV_CHAINS)]
    for k in range(CONV_W):
        term = _shifted(sh_ref, q, base, first_tap_row + step * k) * w_ref[k:k + 1, cols]
        acc[k % CONV_CHAINS] = acc[k % CONV_CHAINS] + term
    return functools.reduce(lambda a, b: a + b, acc)


def _mixer_fwd(proj, lg, lb, wm, bst, cw, cb, clg, clb):
    tm = TM_MIX
    hb = tm // HALO

    def body(p_ref, ph_ref, lg_ref, lb_ref, wm_ref, bs_ref, cw_ref, cb_ref, clg_ref, clb_ref,
             mix_ref, c_ref, g_ref, gbuf):
        i = pl.program_id(0)
        u, _ = _gelu(p_ref[:, 0:D_A])
        vg, _ = _gelu(p_ref[:, D_A:2 * D_A])
        xhat, _ = _ln(vg)
        v = (xhat * lg_ref[...] + lb_ref[...]).astype(BF16)
        mask = _tril_mask()
        for h in range(H_A):
            hc = slice(h * CHUNK, (h + 1) * CHUNK)
            wmh = jnp.where(mask, wm_ref[h], 0.0).astype(BF16)
            for c in range(tm // CHUNK):
                rc = slice(c * CHUNK, (c + 1) * CHUNK)
                mixed = _dot(wmh, v[rc, hc]) + bs_ref[:, h:h + 1]
                mix_ref[rc, hc] = (u[rc, hc] * mixed).astype(BF16)

        g = p_ref[:, 2 * D_A:2 * D_A + D_B] * _sigmoid(p_ref[:, 2 * D_A + D_B:IN_COLS])
        g_ref[...] = g
        gh = ph_ref[:, 0:D_B] * _sigmoid(ph_ref[:, D_B:2 * D_B])
        _fill_shift_buffer(gbuf, 0, jnp.where(i > 0, gh, 0.0))
        _fill_shift_buffer(gbuf, HALO, g)
        _build_shifts(gbuf)
        for q in range(H_B):
            cols = slice(q * LANES, (q + 1) * LANES)
            for rb in range(tm // RB):
                acc = _conv_taps(gbuf, cw_ref, q, rb * RB, HALO - (CONV_W - 1), 1)
                c_ref[rb * RB:(rb + 1) * RB, cols] = acc + cb_ref[:, cols]
        for q in range(H_B):
            cols = slice(q * LANES, (q + 1) * LANES)
            chat, _ = _ln(c_ref[:, cols])
            z = chat * clg_ref[:, cols] + clb_ref[:, cols]
            mix_ref[:, D_A + q * LANES:D_A + (q + 1) * LANES] = (z * _sigmoid(z)).astype(BF16)

    vec = _full((1, D_A))
    return pl.pallas_call(
        body, name="mixer_fwd", grid=(T // tm,),
        in_specs=[_rows(tm, IN_COLS),
                  pl.BlockSpec((HALO, 2 * D_B), lambda i: (jnp.maximum(i * hb - 1, 0), 1)),
                  vec, vec, _full((H_A, CHUNK, CHUNK)), _full((CHUNK, H_A)),
                  _full((CONV_W, D_B)), vec, vec, vec],
        out_specs=[_rows(tm, D), _rows(tm, D_B), _rows(tm, D_B)],
        out_shape=[jax.ShapeDtypeStruct((T, D), BF16), jax.ShapeDtypeStruct((T, D_B), F32),
                   jax.ShapeDtypeStruct((T, D_B), F32)],
        scratch_shapes=[pltpu.VMEM((8, H_B, HALO + tm, LANES), F32)],
        compiler_params=_cparams(32),
    )(proj, proj, lg, lb, wm, bst, cw, cb, clg, clb)


def _fwd_out(mix, wout, x, g2):
    def body(m_ref, w_ref, x_ref, g_ref, x1_ref, h2_ref):
        x1 = x_ref[...] + _dot(m_ref[...], w_ref[...])
        x1_ref[...] = x1
        h2_ref[...] = (x1 * _rms(x1) * g_ref[...]).astype(BF16)

    return pl.pallas_call(
        body, name="fwd_out", grid=(T // TM_FWD,),
        in_specs=[_rows(TM_FWD, D), _full((D, D)), _rows(TM_FWD, D), _full((1, D))],
        out_specs=[_rows(TM_FWD, D), _rows(TM_FWD, D)],
        out_shape=[jax.ShapeDtypeStruct((T, D), F32), jax.ShapeDtypeStruct((T, D), BF16)],
        compiler_params=_cparams(32),
    )(mix, wout, x, g2)


def _fwd_ff1(h2, wg):
    bn = wg.shape[2]

    def body(h_ref, w_ref, r_ref):
        h = h_ref[...]
        for j in range(N_DEV):
            r_ref[:, j * bn:(j + 1) * bn] = jnp.maximum(_dot(h, w_ref[j]), 0.0).astype(BF16)

    return pl.pallas_call(
        body, name="fwd_ff1", grid=(T // TM_FWD,),
        in_specs=[_rows(TM_FWD, D), _full(wg.shape)],
        out_specs=_rows(TM_FWD, D_FF),
        out_shape=jax.ShapeDtypeStruct((T, D_FF), BF16),
        compiler_params=_cparams(48),
    )(h2, wg)


def _fwd_ff2(r, w2, x1):
    def body(r_ref, w_ref, x_ref, o_ref):
        rv = r_ref[...]
        o_ref[...] = x_ref[...] + _dot(rv * rv, w_ref[...])

    return pl.pallas_call(
        body, name="fwd_ff2", grid=(T // TM_FWD,),
        in_specs=[_rows(TM_FWD, D_FF), _full((D_FF, D)), _rows(TM_FWD, D)],
        out_specs=_rows(TM_FWD, D),
        out_shape=jax.ShapeDtypeStruct((T, D), F32),
        compiler_params=_cparams(48),
    )(r, w2, x1)


def _loss_bwd(xf, gf, tgt):
    def body(x_ref, g_ref, t_ref, dx_ref, dxb_ref, loss_ref, dg_ref):
        i = pl.program_id(0)

        @pl.when(i == 0)
        def _():
            loss_ref[...] = jnp.zeros(loss_ref.shape, F32)
            dg_ref[...] = jnp.zeros(dg_ref.shape, F32)

        xv = x_ref[...]
        r = _rms(xv)
        diff = xv * r * g_ref[...] - t_ref[...]
        loss_ref[...] += 0.5 * jnp.sum(jnp.mean(diff * diff, axis=-1, keepdims=True), axis=0, keepdims=True)
        dx, dg = _rms_bwd(diff * (1.0 / D), xv, r, g_ref[...])
        dx_ref[...] = dx
        dxb_ref[...] = dx.astype(BF16)
        dg_ref[...] += dg

    return pl.pallas_call(
        body, name="loss_bwd", grid=(T // TM,),
        in_specs=[_rows(TM, D), _full((1, D)), _rows(TM, D)],
        out_specs=[_rows(TM, D), _rows(TM, D), _full((8, LANES)), _full((1, D))],
        out_shape=[jax.ShapeDtypeStruct((T, D), F32), jax.ShapeDtypeStruct((T, D), BF16),
                   jax.ShapeDtypeStruct((8, LANES), F32), jax.ShapeDtypeStruct((1, D), F32)],
        compiler_params=_cparams(32),
    )(xf, gf, tgt)


def _bwd_ff2(dxb, w2, r, dep):
    kb = 1024

    def body(d_ref, w_ref, r_ref, dep_ref, o_ref):
        d = d_ref[...]
        for j in range(D_FF // kb):
            cols = slice(j * kb, (j + 1) * kb)
            df = _dot_nt(d, w_ref[cols, :])
            o_ref[:, cols] = (2.0 * r_ref[:, cols].astype(F32) * df).astype(BF16)

    return pl.pallas_call(
        body, name="bwd_ff2", grid=(T // TM,),
        in_specs=[_rows(TM, D), _full((D_FF, D)), _rows(TM, D_FF), ANY],
        out_specs=_rows(TM, D_FF),
        out_shape=jax.ShapeDtypeStruct((T, D_FF), BF16),
        compiler_params=_cparams(48),
    )(dxb, w2, r, dep)


def _bwd_norm_matmul(name, dy, wg, x, g, dres, dep):
    bn = wg.shape[2]
    ncol = dy.shape[1]

    def body(dy_ref, w_ref, x_ref, g_ref, dres_ref, dep_ref, dx_ref, dxb_ref, dg_ref):
        i = pl.program_id(0)

        @pl.when(i == 0)
        def _():
            dg_ref[...] = jnp.zeros(dg_ref.shape, F32)

        dh = jnp.zeros((TM, D), F32)
        for j in range(N_DEV):
            dh = dh + _dot_nt(dy_ref[:, j * bn:(j + 1) * bn], w_ref[j])
        xv = x_ref[...]
        dxn, dg = _rms_bwd(dh, xv, _rms(xv), g_ref[...])
        dx = dres_ref[...] + dxn
        dx_ref[...] = dx
        dxb_ref[...] = dx.astype(BF16)
        dg_ref[...] += dg

    return pl.pallas_call(
        body, name=name, grid=(T // TM,),
        in_specs=[_rows(TM, ncol), _full(wg.shape), _rows(TM, D), _full((1, D)), _rows(TM, D), ANY],
        out_specs=[_rows(TM, D), _rows(TM, D), _full((1, D))],
        out_shape=[jax.ShapeDtypeStruct((T, D), F32), jax.ShapeDtypeStruct((T, D), BF16),
                   jax.ShapeDtypeStruct((1, D), F32)],
        compiler_params=_cparams(48),
    )(dy, wg, x, g, dres, dep)


def _mixer_bwd_a(dxb, wout, proj, c, lg, lb, wm, bst, clg, clb, dep):
    tm = TM_MIX
    n_tiles = T // tm

    def body(dx_ref, wo_ref, p_ref, c_ref, lg_ref, lb_ref, wm_ref, bs_ref, clg_ref, clb_ref, dep_ref,
             dpa_ref, dc_ref, dlg_ref, dlb_ref, dwm_ref, dbs_ref, dcb_ref, dclg_ref, dclb_ref,
             dv_buf, db_acc):
        i = pl.program_id(0)

        @pl.when(i == 0)
        def _():
            for ref in (dlg_ref, dlb_ref, dwm_ref, dbs_ref, dcb_ref, dclg_ref, dclb_ref, db_acc):
                ref[...] = jnp.zeros(ref.shape, F32)

        dmix = _dot_nt(dx_ref[...], wo_ref[...])
        ua = p_ref[:, 0:D_A]
        va = p_ref[:, D_A:2 * D_A]
        u, cdf_u = _gelu(ua)
        vg, cdf_v = _gelu(va)
        xhat, rstd = _ln(vg)
        v = (xhat * lg_ref[...] + lb_ref[...]).astype(BF16)
        mask = _tril_mask()
        for h in range(H_A):
            hc = slice(h * CHUNK, (h + 1) * CHUNK)
            wmh = jnp.where(mask, wm_ref[h], 0.0).astype(BF16)
            for cidx in range(tm // CHUNK):
                rc = slice(cidx * CHUNK, (cidx + 1) * CHUNK)
                vb = v[rc, hc]
                mixed = _dot(wmh, vb) + bs_ref[:, h:h + 1]
                da = dmix[rc, hc]
                dpa_ref[rc, hc] = (da * mixed * _gelu_grad(ua[rc, hc], cdf_u[rc, hc])).astype(BF16)
                dmixed = da * u[rc, hc]
                dmb = dmixed.astype(BF16)
                dv_buf[rc, hc] = _dot_tn(wmh, dmb)
                dwm_ref[h] += _dot_nt(dmb, vb)
                db_acc[:, hc] += dmixed
        dv = dv_buf[...]
        dlb_ref[...] += jnp.sum(dv, axis=0, keepdims=True)
        dlg_ref[...] += jnp.sum(dv * xhat, axis=0, keepdims=True)
        dvg = _ln_bwd(dv * lg_ref[...], xhat, rstd)
        dpa_ref[:, D_A:2 * D_A] = (dvg * _gelu_grad(va, cdf_v)).astype(BF16)

        for q in range(H_B):
            cols = slice(q * LANES, (q + 1) * LANES)
            chat, crstd = _ln(c_ref[:, cols])
            z = chat * clg_ref[:, cols] + clb_ref[:, cols]
            sg = _sigmoid(z)
            dz = dmix[:, D_A + q * LANES:D_A + (q + 1) * LANES] * (sg * (1.0 + z * (1.0 - sg)))
            dclb_ref[:, cols] += jnp.sum(dz, axis=0, keepdims=True)
            dclg_ref[:, cols] += jnp.sum(dz * chat, axis=0, keepdims=True)
            dc = _ln_bwd(dz * clg_ref[:, cols], chat, crstd)
            dc_ref[:, cols] = dc
            dcb_ref[:, cols] += jnp.sum(dc, axis=0, keepdims=True)

        @pl.when(i == n_tiles - 1)
        def _():
            for h in range(H_A):
                hc = slice(h * CHUNK, (h + 1) * CHUNK)
                dwm_ref[h] = jnp.where(mask, dwm_ref[h], 0.0)
                dbs_ref[h:h + 1, :] = jnp.sum(db_acc[:, hc].T, axis=0, keepdims=True)

    vec = _full((1, D_A))
    vshape = jax.ShapeDtypeStruct((1, D_A), F32)
    return pl.pallas_call(
        body, name="mixer_bwd_a", grid=(n_tiles,),
        in_specs=[_rows(tm, D), _full((D, D)), _rows(tm, IN_COLS), _rows(tm, D_B), vec, vec,
                  _full((H_A, CHUNK, CHUNK)), _full((CHUNK, H_A)), vec, vec, ANY],
        out_specs=[_rows(tm, 2 * D_A), _rows(tm, D_B), vec, vec, _full((H_A, CHUNK, CHUNK)),
                   _full((H_A, CHUNK)), vec, vec, vec],
        out_shape=[jax.ShapeDtypeStruct((T, 2 * D_A), BF16), jax.ShapeDtypeStruct((T, D_B), F32), vshape, vshape,
                   jax.ShapeDtypeStruct((H_A, CHUNK, CHUNK), F32), jax.ShapeDtypeStruct((H_A, CHUNK), F32),
                   vshape, vshape, vshape],
        scratch_shapes=[pltpu.VMEM((tm, D_A), F32), pltpu.VMEM((CHUNK, D_A), F32)],
        compiler_params=_cparams(32),
    )(dxb, wout, proj, c, lg, lb, wm, bst, clg, clb, dep)


def _mixer_bwd_b(dc, g, proj, cw, dpa):
    tm = TM_MIX
    n_tiles = T // tm
    hb = tm // HALO

    def body(dc_ref, dch_ref, g_ref, gh_ref, p_ref, cw_ref, dpa_ref, dp_ref, dcw_ref, dcbuf, gbuf, dwacc):
        i = pl.program_id(0)

        @pl.when(i == 0)
        def _():
            dwacc[...] = jnp.zeros(dwacc.shape, F32)

        _fill_shift_buffer(dcbuf, 0, dc_ref[...])
        _fill_shift_buffer(dcbuf, tm, jnp.where(i < n_tiles - 1, dch_ref[...], 0.0))
        _fill_shift_buffer(gbuf, 0, jnp.where(i > 0, gh_ref[...], 0.0))
        _fill_shift_buffer(gbuf, HALO, g_ref[...])
        _build_shifts(dcbuf)
        _build_shifts(gbuf)
        dp_ref[:, 0:2 * D_A] = dpa_ref[...]
        for q in range(H_B):
            cols = slice(q * LANES, (q + 1) * LANES)

            def row_block(rb, carry, q=q, cols=cols):
                base = pl.multiple_of(rb * RB, RB)
                rows = pl.ds(base, RB)
                dg = _conv_taps(dcbuf, cw_ref, q, base, CONV_W - 1, -1)
                val = p_ref[rows, cols]
                sg = _sigmoid(p_ref[rows, D_B + q * LANES:D_B + (q + 1) * LANES])
                dp_ref[rows, 2 * D_A + q * LANES:2 * D_A + (q + 1) * LANES] = (dg * sg).astype(BF16)
                dp_ref[rows, 2 * D_A + D_B + q * LANES:2 * D_A + D_B + (q + 1) * LANES] = (
                    dg * val * sg * (1.0 - sg)).astype(BF16)
                dcv = dcbuf[0, q, rows, :]
                parts = []
                for k in range(CONV_W):
                    prod = dcv * _shifted(gbuf, q, base, HALO - (CONV_W - 1) + k)
                    parts.append(jnp.sum(prod.reshape(RB // 8, 8, LANES), axis=0))
                for k in range(CONV_W):
                    dwacc[k * 8:(k + 1) * 8, cols] += parts[k]
                return carry

            lax.fori_loop(0, tm // RB, row_block, 0)

        @pl.when(i == n_tiles - 1)
        def _():
            for k in range(CONV_W):
                dcw_ref[k:k + 1, :] = jnp.sum(dwacc[k * 8:(k + 1) * 8, :], axis=0, keepdims=True)

    return pl.pallas_call(
        body, name="mixer_bwd_b", grid=(n_tiles,),
        in_specs=[_rows(tm, D_B),
                  pl.BlockSpec((HALO, D_B), lambda i: (jnp.minimum((i + 1) * hb, T // HALO - 1), 0)),
                  _rows(tm, D_B),
                  pl.BlockSpec((HALO, D_B), lambda i: (jnp.maximum(i * hb - 1, 0), 0)),
                  pl.BlockSpec((tm, 2 * D_B), lambda i: (i, 1)),
                  _full((CONV_W, D_B)), _rows(tm, 2 * D_A)],
        out_specs=[_rows(tm, IN_COLS), _full((CONV_W, D_B))],
        out_shape=[jax.ShapeDtypeStruct((T, IN_COLS), BF16), jax.ShapeDtypeStruct((CONV_W, D_B), F32)],
        scratch_shapes=[pltpu.VMEM((8, H_B, tm + HALO, LANES), F32), pltpu.VMEM((8, H_B, HALO + tm, LANES), F32),
                        pltpu.VMEM((CONV_W * 8, D_B), F32)],
        compiler_params=_cparams(32),
    )(dc, dc, g, g, proj, cw, dpa)


def _wgrad(name, a, g, bn, square_a=False):
    k = a.shape[1]
    n = g.shape[1]
    tk = min(k, 1024)
    tn = min(n, max(bn, 1024))
    nsub = tn // bn
    tt = TT_WGRAD
    nt = T // tt

    def body(a_ref, g_ref, o_ref, ob_ref):
        t = pl.program_id(2)

        @pl.when(t == 0)
        def _():
            o_ref[...] = jnp.zeros(o_ref.shape, F32)

        av = a_ref[...]
        if square_a:
            av = av * av
        for s in range(nsub):
            o_ref[s] += _dot_tn(av, g_ref[:, s * bn:(s + 1) * bn])

        @pl.when(t == nt - 1)
        def _():
            ob_ref[...] = o_ref[...].astype(BF16)

    ospec = pl.BlockSpec((nsub, tk, bn), lambda ki, ni, ti: (ni, ki, 0))
    return pl.pallas_call(
        body, name=name, grid=(k // tk, n // tn, nt),
        in_specs=[pl.BlockSpec((tt, tk), lambda ki, ni, ti: (ti, ki)),
                  pl.BlockSpec((tt, tn), lambda ki, ni, ti: (ti, ni))],
        out_specs=[ospec, ospec],
        out_shape=[jax.ShapeDtypeStruct((n // bn, k, bn), F32), jax.ShapeDtypeStruct((n // bn, k, bn), BF16)],
        compiler_params=_cparams(40, ("arbitrary", "arbitrary", "arbitrary")),
    )(a, g)


def _coords():
    return lax.axis_index("x"), lax.axis_index("y"), lax.axis_index("c")


def _flat(x, y, c):
    return 4 * x + 2 * y + c


def _peer(k):
    x, y, c = _coords()
    return (x ^ ((k >> 2) & 1), y ^ ((k >> 1) & 1), c ^ (k & 1))


def _hbm(a):
    return pltpu.with_memory_space_constraint(a, pltpu.HBM)


def _hbm_like(a):
    return pltpu.HBM(a.shape, a.dtype)


def _peer_sems():
    return pltpu.SemaphoreType.DMA((N_DEV - 1,))


def _place_own(shards):
    n = len(shards)
    shapes = [(s.shape if l is None else s.shape[1:]) for s, l, _ in shards]

    def body(*refs):
        ins, outs = refs[:n], refs[n:2 * n]
        stage_in, stage_out = refs[2 * n:3 * n], refs[3 * n:4 * n]
        in_sems, out_sems = refs[4 * n], refs[4 * n + 1]
        me = _flat(*_coords())
        srcs = [ins[a] if shards[a][1] is None else ins[a].at[shards[a][1]] for a in range(n)]
        loads = [pltpu.make_async_copy(srcs[a], stage_in[a], in_sems.at[a]) for a in range(n)]
        stores = [pltpu.make_async_copy(stage_out[a], outs[a].at[me], out_sems.at[a]) for a in range(n)]
        for cp in loads:
            cp.start()
        for a in range(n):
            loads[a].wait()
            stage_out[a][...] = stage_in[a][...].astype(stage_out[a].dtype)
            stores[a].start()
        for cp in stores:
            cp.wait()

    return pl.pallas_call(
        body, name="place_own", in_specs=[ANY] * n, out_specs=[ANY] * n,
        out_shape=[jax.ShapeDtypeStruct((N_DEV,) + shapes[a], shards[a][2]) for a in range(n)],
        scratch_shapes=[*[pltpu.VMEM(shapes[a], shards[a][0].dtype) for a in range(n)],
                        *[pltpu.VMEM(shapes[a], shards[a][2]) for a in range(n)],
                        pltpu.SemaphoreType.DMA((n,)), pltpu.SemaphoreType.DMA((n,))],
        compiler_params=pltpu.CompilerParams(vmem_limit_bytes=40 * MIB),
    )(*[s for s, _, _ in shards])


SIBLING = 1
CHIP_PEERS = (2, 4, 6)
FIRST_PEERS = (SIBLING,) + CHIP_PEERS


def _gather_start(lands):
    n = len(lands)

    def body(*refs):
        lnd, send, recv = refs[:n], refs[n:2 * n], refs[2 * n:3 * n]
        me = _flat(*_coords())
        for a in range(n):
            for j, k in enumerate(FIRST_PEERS):
                pltpu.make_async_remote_copy(
                    src_ref=lnd[a].at[me], dst_ref=lnd[a].at[me], send_sem=send[a].at[j],
                    recv_sem=recv[a].at[j], device_id=_peer(k), device_id_type=MESH_ID).start()

    sems = pltpu.SemaphoreType.DMA((len(FIRST_PEERS),))
    outs = pl.pallas_call(
        body, name="gather_start",
        out_shape=(*[sems] * (2 * n), *[_hbm_like(l) for l in lands]),
        in_specs=[HBM] * n, out_specs=(*[SEM] * (2 * n), *[HBM] * n),
        input_output_aliases={i: 2 * n + i for i in range(n)},
        compiler_params=pltpu.CompilerParams(has_side_effects=EFFECT),
    )(*[_hbm(l) for l in lands])
    return outs[:n], outs[n:2 * n], outs[2 * n:]


def _gather_pass_on(name, land, send, recv, after):
    def body(l_ref, send_ref, recv_ref, after_ref, send2, recv2, l_out):
        sibling = _peer(SIBLING)
        for j, k in enumerate(CHIP_PEERS):
            cp = pltpu.make_async_remote_copy(
                src_ref=l_ref.at[0], dst_ref=l_ref.at[0], send_sem=send_ref.at[1 + j], recv_sem=recv_ref.at[1 + j],
                device_id=_peer(k), device_id_type=MESH_ID)
            cp.wait_send()
            cp.wait_recv()
            blk = _flat(*_peer(k))
            pltpu.make_async_remote_copy(
                src_ref=l_ref.at[blk], dst_ref=l_ref.at[blk], send_sem=send2.at[j], recv_sem=recv2.at[j],
                device_id=sibling, device_id_type=MESH_ID).start()

    sems = pltpu.SemaphoreType.DMA((len(CHIP_PEERS),))
    return pl.pallas_call(
        body, name=name, out_shape=(sems, sems, _hbm_like(land)),
        in_specs=(HBM, SEM, SEM, ANY), out_specs=(SEM, SEM, HBM), input_output_aliases={0: 2},
        compiler_params=pltpu.CompilerParams(has_side_effects=EFFECT),
    )(land, send, recv, after)


def _gather_wait(name, land, send, recv, send2, recv2, after):
    def body(l_ref, send_ref, recv_ref, send2_ref, recv2_ref, after_ref, l_out):
        sibling = _peer(SIBLING)
        own = pltpu.make_async_remote_copy(
            src_ref=l_ref.at[0], dst_ref=l_ref.at[0], send_sem=send_ref.at[0], recv_sem=recv_ref.at[0],
            device_id=sibling, device_id_type=MESH_ID)
        own.wait_send()
        own.wait_recv()
        for j in range(len(CHIP_PEERS)):
            cp = pltpu.make_async_remote_copy(
                src_ref=l_ref.at[0], dst_ref=l_ref.at[0], send_sem=send2_ref.at[j], recv_sem=recv2_ref.at[j],
                device_id=sibling, device_id_type=MESH_ID)
            cp.wait_send()
            cp.wait_recv()

    return pl.pallas_call(
        body, name=name, out_shape=_hbm_like(land),
        in_specs=(HBM, SEM, SEM, SEM, SEM, ANY), out_specs=HBM, input_output_aliases={0: 0},
        compiler_params=pltpu.CompilerParams(has_side_effects=EFFECT),
    )(land, send, recv, send2, recv2, after)


def _scatter_start(name, grads, lands, layer):
    n = len(grads)

    def body(*refs):
        g, lnd, send, recv = refs[:n], refs[n:2 * n], refs[2 * n:3 * n], refs[3 * n:4 * n]
        for a in range(n):
            for k in range(1, N_DEV):
                to = _peer(k)
                pltpu.make_async_remote_copy(
                    src_ref=g[a].at[_flat(*to)], dst_ref=lnd[a].at[layer, k - 1], send_sem=send[a].at[k - 1],
                    recv_sem=recv[a].at[k - 1], device_id=to, device_id_type=MESH_ID).start()

    outs = pl.pallas_call(
        body, name=name,
        out_shape=(*[_peer_sems()] * (2 * n), *[_hbm_like(g) for g in grads], *[_hbm_like(l) for l in lands]),
        in_specs=[HBM] * (2 * n), out_specs=(*[SEM] * (2 * n), *[HBM] * (2 * n)),
        input_output_aliases={i: 2 * n + i for i in range(2 * n)},
        compiler_params=pltpu.CompilerParams(has_side_effects=EFFECT),
    )(*[_hbm(g) for g in grads], *[_hbm(l) for l in lands])
    return outs[:n], outs[n:2 * n], outs[2 * n:3 * n], outs[3 * n:]


def _scatter_wait(grads, lands, sends, recvs, after):
    n, nw = len(grads), len(lands)

    def body(*refs):
        g, lnd = refs[:n], refs[n:n + nw]
        send, recv = refs[n + nw:2 * n + nw], refs[2 * n + nw:3 * n + nw]
        for a in range(n):
            for k in range(1, N_DEV):
                cp = pltpu.make_async_remote_copy(
                    src_ref=g[a].at[0], dst_ref=lnd[a // DEPTH].at[a % DEPTH, 0],
                    send_sem=send[a].at[k - 1], recv_sem=recv[a].at[k - 1],
                    device_id=_peer(k), device_id_type=MESH_ID)
                cp.wait_send()
                cp.wait_recv()

    outs = pl.pallas_call(
        body, name="scatter_wait",
        out_shape=(*[_hbm_like(g) for g in grads], *[_hbm_like(l) for l in lands]),
        in_specs=(*[HBM] * (n + nw), *[SEM] * (2 * n), ANY), out_specs=[HBM] * (n + nw),
        input_output_aliases={i: i for i in range(n + nw)},
        compiler_params=pltpu.CompilerParams(has_side_effects=EFFECT),
    )(*grads, *lands, *sends, *recvs, after)
    return outs[n:]


def _own_blocks(grads_f32):
    nw = len(grads_f32)
    flat32 = [g for w in grads_f32 for g in w]
    n = len(flat32)

    def body(*refs):
        g32, own, stage = refs[:n], refs[n:n + nw], refs[n + nw:2 * n + nw]
        in_sems, out_sems = refs[2 * n + nw], refs[2 * n + nw + 1]
        me = _flat(*_coords())
        loads = [pltpu.make_async_copy(g32[a].at[me], stage[a], in_sems.at[a]) for a in range(n)]
        stores = [pltpu.make_async_copy(stage[a], own[a // DEPTH].at[a % DEPTH], out_sems.at[a]) for a in range(n)]
        for cp in loads:
            cp.start()
        for a in range(n):
            loads[a].wait()
            stores[a].start()
        for cp in stores:
            cp.wait()

    return pl.pallas_call(
        body, name="own_blocks", in_specs=[ANY] * n, out_specs=[ANY] * nw,
        out_shape=[jax.ShapeDtypeStruct((DEPTH,) + w[0].shape[1:], F32) for w in grads_f32],
        scratch_shapes=[*[pltpu.VMEM(g.shape[1:], F32) for g in flat32],
                        pltpu.SemaphoreType.DMA((n,)), pltpu.SemaphoreType.DMA((n,))],
        compiler_params=pltpu.CompilerParams(vmem_limit_bytes=24 * MIB),
    )(*flat32)


def _all_reduce_small(part):
    rows = part.shape[0]
    br = rows // N_DEV
    assert br * N_DEV == rows and br % 8 == 0

    def body(p_ref, o_ref, slots, send1, recv1, send2, recv2):
        me = _flat(*_coords())

        def block(ref, d):
            return ref.at[pl.ds(pl.multiple_of(d * br, 8), br), :]

        slots[me] = p_ref[pl.ds(pl.multiple_of(me * br, 8), br), :]
        scatter, gather = [], []
        for k in range(1, N_DEV):
            to = _peer(k)
            scatter.append(pltpu.make_async_remote_copy(
                src_ref=block(p_ref, _flat(*to)), dst_ref=slots.at[me],
                send_sem=send1.at[k - 1], recv_sem=recv1.at[k - 1], device_id=to, device_id_type=MESH_ID))
            gather.append(pltpu.make_async_remote_copy(
                src_ref=block(o_ref, me), dst_ref=block(o_ref, me),
                send_sem=send2.at[k - 1], recv_sem=recv2.at[k - 1], device_id=to, device_id_type=MESH_ID))
        for cp in scatter:
            cp.start()
        for cp in scatter:
            cp.wait()
        acc = slots[0]
        for d in range(1, N_DEV):
            acc = acc + slots[d]
        o_ref[pl.ds(pl.multiple_of(me * br, 8), br), :] = acc
        for cp in gather:
            cp.start()
        for cp in gather:
            cp.wait()

    return pl.pallas_call(
        body, name="all_reduce_small",
        in_specs=[pl.BlockSpec(memory_space=pltpu.VMEM)], out_specs=pl.BlockSpec(memory_space=pltpu.VMEM),
        out_shape=jax.ShapeDtypeStruct(part.shape, F32),
        scratch_shapes=[pltpu.VMEM((N_DEV, br, LANES), F32)] + [pltpu.SemaphoreType.DMA((N_DEV - 1,))] * 4,
        compiler_params=pltpu.CompilerParams(vmem_limit_bytes=24 * MIB),
    )(part)


def _adam_math(w, g, m, v):
    m = ADAM_B1 * m + (1.0 - ADAM_B1) * g
    v = ADAM_B2 * v + (1.0 - ADAM_B2) * (g * g)
    m_hat = m / (1.0 - ADAM_B1 ** ADAM_STEP)
    v_hat = v / (1.0 - ADAM_B2 ** ADAM_STEP)
    delta = -ADAM_LR * (m_hat / (jnp.sqrt(v_hat) + ADAM_EPS) + ADAM_WD * w)
    return delta, m, v


def _adam_sharded(name, own, recv, w, m, v):
    _, r, c = w.shape
    tr = min(r, 256)

    def body(own_ref, recv_ref, w_ref, m_ref, v_ref, g_ref, d_ref, nm_ref, nv_ref):
        g = own_ref[...]
        for k in range(N_DEV - 1):
            g = g + recv_ref[k].astype(F32)
        delta, nm, nv = _adam_math(w_ref[...], g, m_ref[...], v_ref[...])
        g_ref[...] = g
        d_ref[...] = delta
        nm_ref[...] = nm
        nv_ref[...] = nv

    blk = pl.BlockSpec((None, tr, c), lambda l, i: (l, i, 0))
    shp = jax.ShapeDtypeStruct(w.shape, F32)
    return pl.pallas_call(
        body, name=name, grid=(DEPTH, r // tr),
        in_specs=[blk, pl.BlockSpec((None, N_DEV - 1, tr, c), lambda l, i: (l, 0, i, 0)), blk, blk, blk],
        out_specs=[blk] * 4, out_shape=[shp] * 4,
        compiler_params=_cparams(32, ("arbitrary", "arbitrary")),
    )(own, recv, w, m, v)


def _adam_small(g, w, m, v):
    rows = g.shape[0]

    def body(g_ref, w_ref, m_ref, v_ref, d_ref, nm_ref, nv_ref):
        delta, nm, nv = _adam_math(w_ref[...], g_ref[...], m_ref[...], v_ref[...])
        d_ref[...] = delta
        nm_ref[...] = nm
        nv_ref[...] = nv

    spec = _full((rows, LANES))
    shp = jax.ShapeDtypeStruct((rows, LANES), F32)
    return pl.pallas_call(
        body, name="adam_small", grid=(1,),
        in_specs=[spec] * 4, out_specs=[spec] * 3, out_shape=[shp] * 3,
        compiler_params=_cparams(24),
    )(g, w, m, v)


SMALL = ["norm1_g", "sgu_ln_g", "sgu_ln_b", "sgu_w", "sgu_b", "conv_b", "conv_ln_g", "conv_ln_b", "norm2_g",
         "final_g"]


def _pack(arrays):
    flat = jnp.concatenate([a.reshape(-1) for a in arrays])
    pad = (-flat.shape[0]) % (N_DEV * 8 * LANES)
    return jnp.pad(flat, (0, pad)).reshape(-1, LANES)


def _unpack(packed, shapes):
    flat = packed.reshape(-1)
    out, off = [], 0
    for s in shapes:
        size = 1
        for d in s:
            size *= d
        out.append(flat[off:off + size].reshape(s))
        off += size
    return out


def kernel(x, norm1_g, w_in, sgu_ln_g, sgu_ln_b, sgu_w, sgu_b, conv_w, conv_b, conv_ln_g, conv_ln_b, w_out, norm2_g, w_ff1, w_ff2, final_g, loss_target, m_norm1_g, m_w_in, m_sgu_ln_g, m_sgu_ln_b, m_sgu_w, m_sgu_b, m_conv_w, m_conv_b, m_conv_ln_g, m_conv_ln_b, m_w_out, m_norm2_g, m_w_ff1, m_w_ff2, m_final_g, v_norm1_g, v_w_in, v_sgu_ln_g, v_sgu_ln_b, v_sgu_w, v_sgu_b, v_conv_w, v_conv_b, v_conv_ln_g, v_conv_ln_b, v_w_out, v_norm2_g, v_w_ff1, v_w_ff2, v_final_g):
    x2d = x.reshape(T, D)
    tgt = loss_target.reshape(T, D)
    cw_shard = conv_w.reshape(CONV_W, LANES)

    gnames, shards = [], []
    for l in range(DEPTH):
        for k, w in (("w_in", w_in), ("w_out", w_out), ("w_ff1", w_ff1), ("w_ff2", w_ff2)):
            gnames.append(f"{k}{l}")
            shards.append((w, l, BF16))
        if l == 0:
            gnames.insert(1, "conv_w")
            shards.insert(1, (cw_shard, None, F32))
    sends, recvs, lands = _gather_start(_place_own(shards))
    gidx = {k: i for i, k in enumerate(gnames)}

    passed = {}

    def pass_on(k, after):
        i = gidx[k]
        passed[k] = _gather_pass_on("gather_pass_on_" + k, lands[i], sends[i], recvs[i], after)

    def gathered(k, after):
        i = gidx[k]
        send2, recv2, land = passed[k]
        return _gather_wait("gather_wait_" + k, land, sends[i], recvs[i], send2, recv2, after)

    saved = []
    xl = x2d
    cw_full = None
    for l in range(DEPTH):
        g1 = norm1_g[l].reshape(1, D)
        g2 = norm2_g[l].reshape(1, D)
        lg, lb = sgu_ln_g[l].reshape(1, D_A), sgu_ln_b[l].reshape(1, D_A)
        bst = sgu_b[l].T
        cb = conv_b[l].reshape(1, D_B)
        clg, clb = conv_ln_g[l].reshape(1, D_B), conv_ln_b[l].reshape(1, D_B)
        if l == 0:
            pass_on("w_in0", xl)
        wg_in = gathered(f"w_in{l}", xl)
        if l == 0:
            pass_on("conv_w", xl)
        h, proj = _fwd_in(xl, g1, wg_in)
        if l == 0:
            cw_full = gathered("conv_w", proj).reshape(N_DEV, DEPTH, CONV_W, D_B // N_DEV).transpose(
                1, 2, 0, 3).reshape(DEPTH, CONV_W, D_B)
        pass_on(f"w_out{l}", proj)
        mix, c, g = _mixer_fwd(proj, lg, lb, sgu_w[l], bst, cw_full[l], cb, clg, clb)
        pass_on(f"w_ff1{l}", mix)
        wout = gathered(f"w_out{l}", mix).reshape(D, D)
        x1, h2 = _fwd_out(mix, wout, xl, g2)
        pass_on(f"w_ff2{l}", h2)
        wg_ff1 = gathered(f"w_ff1{l}", h2)
        r = _fwd_ff1(h2, wg_ff1)
        if l + 1 < DEPTH:
            pass_on(f"w_in{l + 1}", r)
        w2 = gathered(f"w_ff2{l}", r).reshape(D_FF, D)
        x2 = _fwd_ff2(r, w2, x1)
        saved.append(dict(x=xl, h=h, proj=proj, mix=mix, c=c, g=g, x1=x1, h2=h2, r=r, wg_in=wg_in, wout=wout,
                          wg_ff1=wg_ff1, w2=w2, g1=g1, g2=g2, lg=lg, lb=lb, bst=bst, clg=clg, clb=clb))
        xl = x2

    dx, dxb, loss_part, d_final_g = _loss_bwd(xl, final_g.reshape(1, D), tgt)

    names = ["w_in", "w_out", "w_ff1", "w_ff2"]
    block = dict(w_in=(D, IN_COLS // N_DEV), w_out=(D // N_DEV, D), w_ff1=(D, D_FF // N_DEV), w_ff2=(D_FF // N_DEV, D))
    land = {k: lax.empty((DEPTH, N_DEV - 1) + block[k], BF16) for k in names}
    big32 = {k: [None] * DEPTH for k in names}
    big16 = {k: [None] * DEPTH for k in names}
    ssend = {k: [None] * DEPTH for k in names}
    srecv = {k: [None] * DEPTH for k in names}

    def send_grads(l, grads):
        ks = list(grads)
        for k in ks:
            big32[k][l] = grads[k][0].reshape((N_DEV,) + block[k])
        sends, recvs, g16s, lands = _scatter_start(
            "scatter_start_" + "_".join(ks) + str(l), [grads[k][1].reshape((N_DEV,) + block[k]) for k in ks],
            [land[k] for k in ks], l)
        for i, k in enumerate(ks):
            ssend[k][l], srecv[k][l], big16[k][l], land[k] = sends[i], recvs[i], g16s[i], lands[i]

    small = {}
    d_conv_w = [None] * DEPTH
    for l in reversed(range(DEPTH)):
        s = saved[l]
        g_ff2 = _wgrad("wgrad_ff2", s["r"], dxb, D, square_a=True)
        df1 = _bwd_ff2(dxb, s["w2"], s["r"], g_ff2[1])
        send_grads(l, dict(w_ff2=g_ff2, w_ff1=_wgrad("wgrad_ff1", s["h2"], df1, D_FF // N_DEV)))
        dx1, dx1b, dg2 = _bwd_norm_matmul("bwd_ff1", df1, s["wg_ff1"], s["x1"], s["g2"], dx, big16["w_ff1"][l])
        g_out = _wgrad("wgrad_out", s["mix"], dx1b, D)
        dpa, dc, dlg, dlb, dwm, dbs, dcb, dclg, dclb = _mixer_bwd_a(
            dx1b, s["wout"], s["proj"], s["c"], s["lg"], s["lb"], sgu_w[l], s["bst"], s["clg"], s["clb"], g_out[1])
        dproj, d_conv_w[l] = _mixer_bwd_b(dc, s["g"], s["proj"], cw_full[l], dpa)
        send_grads(l, dict(w_out=g_out, w_in=_wgrad("wgrad_in", s["h"], dproj, IN_COLS // N_DEV)))
        dx, dxb, dg1 = _bwd_norm_matmul("bwd_in", dproj, s["wg_in"], s["x"], s["g1"], dx1, big16["w_in"][l])
        small[l] = dict(norm1_g=dg1, sgu_ln_g=dlg, sgu_ln_b=dlb, sgu_w=dwm, sgu_b=dbs, conv_b=dcb, conv_ln_g=dclg,
                        conv_ln_b=dclb, norm2_g=dg2)
    grad_x = dx.reshape(1, T, D)

    own = _own_blocks([big32[k] for k in names])
    recv = _scatter_wait([g for k in names for g in big16[k]], [land[k] for k in names],
                         [q for k in names for q in ssend[k]], [q for k in names for q in srecv[k]], dx)
    wmv = dict(w_in=(w_in, m_w_in, v_w_in), w_out=(w_out, m_w_out, v_w_out), w_ff1=(w_ff1, m_w_ff1, v_w_ff1),
               w_ff2=(w_ff2, m_w_ff2, v_w_ff2))
    res = {}
    for idx, k in enumerate(names):
        res[k] = _adam_sharded("adam_" + k, own[idx], recv[idx], *wmv[k])

    rep = dict(norm1_g=(norm1_g, m_norm1_g, v_norm1_g), sgu_ln_g=(sgu_ln_g, m_sgu_ln_g, v_sgu_ln_g),
               sgu_ln_b=(sgu_ln_b, m_sgu_ln_b, v_sgu_ln_b), sgu_w=(sgu_w, m_sgu_w, v_sgu_w),
               sgu_b=(sgu_b, m_sgu_b, v_sgu_b), conv_b=(conv_b, m_conv_b, v_conv_b),
               conv_ln_g=(conv_ln_g, m_conv_ln_g, v_conv_ln_g), conv_ln_b=(conv_ln_b, m_conv_ln_b, v_conv_ln_b),
               norm2_g=(norm2_g, m_norm2_g, v_norm2_g), final_g=(final_g, m_final_g, v_final_g))
    parts = []
    for k in SMALL:
        if k == "final_g":
            parts.append(d_final_g.reshape(rep[k][0].shape))
        else:
            parts.append(jnp.stack([small[l][k].reshape(rep[k][0].shape[1:]) for l in range(DEPTH)]))
    parts.append(jnp.stack(d_conv_w))
    parts.append(loss_part[0, 0:1])
    shapes = [p.shape for p in parts]
    summed = _unpack(_all_reduce_small(_pack(parts)), shapes)
    loss = summed[-1][0]
    me = _flat(*_coords())
    g_conv_w = lax.dynamic_slice_in_dim(summed[-2], me * (D_B // N_DEV), D_B // N_DEV, axis=2)
    g_small = summed[:-2] + [g_conv_w]
    keys = SMALL + ["conv_w"]
    rep["conv_w"] = (conv_w, m_conv_w, v_conv_w)
    shapes = [a.shape for a in g_small]
    packed = [_pack(g_small)] + [_pack([rep[k][i] for k in keys]) for i in range(3)]
    d_s, nm_s, nv_s = (_unpack(a, shapes) for a in _adam_small(*packed))
    for i, k in enumerate(keys):
        res[k] = (g_small[i], d_s[i], nm_s[i], nv_s[i])

    order = ["norm1_g", "w_in", "sgu_ln_g", "sgu_ln_b", "sgu_w", "sgu_b", "conv_w", "conv_b", "conv_ln_g",
             "conv_ln_b", "w_out", "norm2_g", "w_ff1", "w_ff2", "final_g"]
    return (loss, grad_x, *[res[k][0] for k in order], *[res[k][1] for k in order],
            *[res[k][2] for k in order], *[res[k][3] for k in order])
```

```python
import functools

import jax
import jax.numpy as jnp
from jax import lax
from jax.experimental import pallas as pl
from jax.experimental.pallas import tpu as pltpu

F32 = jnp.float32
BF16 = jnp.bfloat16

N_DEV = 8
DEPTH = 2
T = 4096
D = 1024
D_A = 512
D_B = 512
CHUNK = 128
H_A = 4
H_B = 4
CONV_W = 31
HALO = 32
D_FF = 4096
IN_COLS = 2048
EPS = 1e-6

ADAM_LR = 0.001
ADAM_B1 = 0.9
ADAM_B2 = 0.999
ADAM_EPS = 1e-08
ADAM_WD = 0.01
ADAM_STEP = 10

TM = 512
TM_FWD = 1024
TT_WGRAD = 1024
TM_MIX = 256
TM_SGU = 512
RB = 64
CONV_CHAINS = 2
LANES = 128
MIB = 1024 * 1024
SCOPED_VMEM_MIB = 60

SQRT_HALF = 0.7071067811865476
INV_SQRT_2PI = 0.3989422804014327

MESH_ID = pl.DeviceIdType.MESH
ANY = pl.BlockSpec(memory_space=pl.ANY)
HBM = pl.BlockSpec(memory_space=pltpu.HBM)
SEM = pl.BlockSpec(memory_space=pltpu.SEMAPHORE)
EFFECT = pltpu.SideEffectType.DATAFLOW_SIDE_EFFECTING


def _cparams(vmem_mib, sem=("arbitrary",)):
    assert vmem_mib <= SCOPED_VMEM_MIB
    return pltpu.CompilerParams(dimension_semantics=sem, vmem_limit_bytes=SCOPED_VMEM_MIB * MIB)


def _full(shape):
    return pl.BlockSpec(shape, lambda *_: (0,) * len(shape))


def _rows(tm, cols):
    return pl.BlockSpec((tm, cols), lambda i: (i, 0))


def _gelu(x):
    cdf = 0.5 * (1.0 + lax.erf(x * SQRT_HALF))
    return x * cdf, cdf


def _gelu_grad(x, cdf):
    return cdf + x * (INV_SQRT_2PI * jnp.exp(-0.5 * x * x))


def _sigmoid(x):
    return 1.0 / (1.0 + jnp.exp(-x))


def _ln(x):
    mu = jnp.mean(x, axis=-1, keepdims=True)
    xc = x - mu
    rstd = lax.rsqrt(jnp.mean(xc * xc, axis=-1, keepdims=True) + EPS)
    return xc * rstd, rstd


def _ln_bwd(dyh, xhat, rstd):
    return rstd * (dyh - jnp.mean(dyh, axis=-1, keepdims=True) - xhat * jnp.mean(dyh * xhat, axis=-1, keepdims=True))


def _rms(x):
    return lax.rsqrt(jnp.mean(x * x, axis=-1, keepdims=True) + EPS)


def _rms_bwd(dh, x, r, g):
    n = x * r
    dn = dh * g
    dx = r * (dn - n * jnp.mean(dn * n, axis=-1, keepdims=True))
    return dx, jnp.sum(dh * n, axis=0, keepdims=True)


def _dot(a, b):
    return jnp.dot(a, b, preferred_element_type=F32)


def _dot_nt(a, b):
    return lax.dot_general(a, b, (((1,), (1,)), ((), ())), preferred_element_type=F32)


def _dot_tn(a, b):
    return lax.dot_general(a, b, (((0,), (0,)), ((), ())), preferred_element_type=F32)


def _tril_mask():
    r = lax.broadcasted_iota(jnp.int32, (CHUNK, CHUNK), 0)
    c = lax.broadcasted_iota(jnp.int32, (CHUNK, CHUNK), 1)
    return r >= c


def _fwd_in(x, g1, wg):
    bn = wg.shape[2]

    def body(x_ref, g_ref, w_ref, h_ref, p_ref):
        xv = x_ref[...]
        h = (xv * _rms(xv) * g_ref[...]).astype(BF16)
        h_ref[...] = h
        for j in range(N_DEV):
            p_ref[:, j * bn:(j + 1) * bn] = _dot(h, w_ref[j])

    return pl.pallas_call(
        body, name="fwd_in", grid=(T // TM_FWD,),
        in_specs=[_rows(TM_FWD, D), _full((1, D)), _full(wg.shape)],
        out_specs=[_rows(TM_FWD, D), _rows(TM_FWD, IN_COLS)],
        out_shape=[jax.ShapeDtypeStruct((T, D), BF16), jax.ShapeDtypeStruct((T, IN_COLS), F32)],
        compiler_params=_cparams(32),
    )(x, g1, wg)


def _fill_shift_buffer(sh_ref, row0, value):
    for q in range(sh_ref.shape[1]):
        sh_ref[0, q, row0:row0 + value.shape[0], :] = value[:, q * LANES:(q + 1) * LANES]


def _build_shifts(sh_ref):
    rows = sh_ref.shape[2]
    for p in range(1, 8):
        for q in range(sh_ref.shape[1]):
            sh_ref[p, q, 0:rows - 8, :] = sh_ref[0, q, p:p + rows - 8, :]


def _shifted(sh_ref, q, base, off):
    start = base + (off - off % 8)
    if not isinstance(start, int):
        start = pl.multiple_of(start, 8)
    return sh_ref[off % 8, q, pl.ds(start, RB), :]


def _conv_taps(sh_ref, w_ref, q, base, first_tap_row, step):
    cols = slice(q * LANES, (q + 1) * LANES)
    acc = [jnp.zeros((RB, LANES), F32) for _ in range(CONV_CHAINS)]
    for k in range(CONV_W):
        term = _shifted(sh_ref, q, base, first_tap_row + step * k) * w_ref[k:k + 1, cols]
        acc[k % CONV_CHAINS] = acc[k % CONV_CHAINS] + term
    return functools.reduce(lambda a, b: a + b, acc)


def _mixer_fwd(proj, lg, lb, wm, bst, cw, cb, clg, clb):
    tm = TM_SGU
    hb = tm // HALO

    def body(p_ref, ph_ref, lg_ref, lb_ref, wm_ref, bs_ref, cw_ref, cb_ref, clg_ref, clb_ref,
             mix_ref, c_ref, g_ref, gbuf):
        i = pl.program_id(0)
        u, _ = _gelu(p_ref[:, 0:D_A])
        vg, _ = _gelu(p_ref[:, D_A:2 * D_A])
        xhat, _ = _ln(vg)
        v = (xhat * lg_ref[...] + lb_ref[...]).astype(BF16)
        mask = _tril_mask()
        for h in range(H_A):
            hc = slice(h * CHUNK, (h + 1) * CHUNK)
            wmh = jnp.where(mask, wm_ref[h], 0.0).astype(BF16)
            for c in range(tm // CHUNK):
                rc = slice(c * CHUNK, (c + 1) * CHUNK)
                mixed = _dot(wmh, v[rc, hc]) + bs_ref[:, h:h + 1]
                mix_ref[rc, hc] = (u[rc, hc] * mixed).astype(BF16)

        g = p_ref[:, 2 * D_A:2 * D_A + D_B] * _sigmoid(p_ref[:, 2 * D_A + D_B:IN_COLS])
        g_ref[...] = g
        gh = ph_ref[:, 0:D_B] * _sigmoid(ph_ref[:, D_B:2 * D_B])
        _fill_shift_buffer(gbuf, 0, jnp.where(i > 0, gh, 0.0))
        _fill_shift_buffer(gbuf, HALO, g)
        _build_shifts(gbuf)
        for q in range(H_B):
            cols = slice(q * LANES, (q + 1) * LANES)

            def conv_block(rb, carry, q=q, cols=cols):
                base = pl.multiple_of(rb * RB, RB)
                acc = _conv_taps(gbuf, cw_ref, q, base, HALO - (CONV_W - 1), 1)
                c_ref[pl.ds(base, RB), cols] = acc + cb_ref[:, cols]
                return carry

            lax.fori_loop(0, tm // RB, conv_block, 0)
        for q in range(H_B):
            cols = slice(q * LANES, (q + 1) * LANES)
            chat, _ = _ln(c_ref[:, cols])
            z = chat * clg_ref[:, cols] + clb_ref[:, cols]
            mix_ref[:, D_A + q * LANES:D_A + (q + 1) * LANES] = (z * _sigmoid(z)).astype(BF16)

    vec = _full((1, D_A))
    return pl.pallas_call(
        body, name="mixer_fwd", grid=(T // tm,),
        in_specs=[_rows(tm, IN_COLS),
                  pl.BlockSpec((HALO, 2 * D_B), lambda i: (jnp.maximum(i * hb - 1, 0), 1)),
                  vec, vec, _full((H_A, CHUNK, CHUNK)), _full((CHUNK, H_A)),
                  _full((CONV_W, D_B)), vec, vec, vec],
        out_specs=[_rows(tm, D), _rows(tm, D_B), _rows(tm, D_B)],
        out_shape=[jax.ShapeDtypeStruct((T, D), BF16), jax.ShapeDtypeStruct((T, D_B), F32),
                   jax.ShapeDtypeStruct((T, D_B), F32)],
        scratch_shapes=[pltpu.VMEM((8, H_B, HALO + tm, LANES), F32)],
        compiler_params=_cparams(32),
    )(proj, proj, lg, lb, wm, bst, cw, cb, clg, clb)


def _fwd_out(mix, wout, x, g2):
    def body(m_ref, w_ref, x_ref, g_ref, x1_ref, h2_ref):
        x1 = x_ref[...] + _dot(m_ref[...], w_ref[...])
        x1_ref[...] = x1
        h2_ref[...] = (x1 * _rms(x1) * g_ref[...]).astype(BF16)

    return pl.pallas_call(
        body, name="fwd_out", grid=(T // TM_FWD,),
        in_specs=[_rows(TM_FWD, D), _full((D, D)), _rows(TM_FWD, D), _full((1, D))],
        out_specs=[_rows(TM_FWD, D), _rows(TM_FWD, D)],
        out_shape=[jax.ShapeDtypeStruct((T, D), F32), jax.ShapeDtypeStruct((T, D), BF16)],
        compiler_params=_cparams(32),
    )(mix, wout, x, g2)


def _fwd_ff1(h2, wg):
    bn = wg.shape[2]

    def body(h_ref, w_ref, r_ref):
        h = h_ref[...]
        for j in range(N_DEV):
            r_ref[:, j * bn:(j + 1) * bn] = jnp.maximum(_dot(h, w_ref[j]), 0.0).astype(BF16)

    return pl.pallas_call(
        body, name="fwd_ff1", grid=(T // TM_FWD,),
        in_specs=[_rows(TM_FWD, D), _full(wg.shape)],
        out_specs=_rows(TM_FWD, D_FF),
        out_shape=jax.ShapeDtypeStruct((T, D_FF), BF16),
        compiler_params=_cparams(48),
    )(h2, wg)


def _fwd_ff2(r, w2, x1):
    def body(r_ref, w_ref, x_ref, o_ref):
        rv = r_ref[...]
        o_ref[...] = x_ref[...] + _dot(rv * rv, w_ref[...])

    return pl.pallas_call(
        body, name="fwd_ff2", grid=(T // TM_FWD,),
        in_specs=[_rows(TM_FWD, D_FF), _full((D_FF, D)), _rows(TM_FWD, D)],
        out_specs=_rows(TM_FWD, D),
        out_shape=jax.ShapeDtypeStruct((T, D), F32),
        compiler_params=_cparams(48),
    )(r, w2, x1)


def _loss_bwd(xf, gf, tgt):
    def body(x_ref, g_ref, t_ref, dx_ref, dxb_ref, loss_ref, dg_ref):
        i = pl.program_id(0)

        @pl.when(i == 0)
        def _():
            loss_ref[...] = jnp.zeros(loss_ref.shape, F32)
            dg_ref[...] = jnp.zeros(dg_ref.shape, F32)

        xv = x_ref[...]
        r = _rms(xv)
        diff = xv * r * g_ref[...] - t_ref[...]
        loss_ref[...] += 0.5 * jnp.sum(jnp.mean(diff * diff, axis=-1, keepdims=True), axis=0, keepdims=True)
        dx, dg = _rms_bwd(diff * (1.0 / D), xv, r, g_ref[...])
        dx_ref[...] = dx
        dxb_ref[...] = dx.astype(BF16)
        dg_ref[...] += dg

    return pl.pallas_call(
        body, name="loss_bwd", grid=(T // TM,),
        in_specs=[_rows(TM, D), _full((1, D)), _rows(TM, D)],
        out_specs=[_rows(TM, D), _rows(TM, D), _full((8, LANES)), _full((1, D))],
        out_shape=[jax.ShapeDtypeStruct((T, D), F32), jax.ShapeDtypeStruct((T, D), BF16),
                   jax.ShapeDtypeStruct((8, LANES), F32), jax.ShapeDtypeStruct((1, D), F32)],
        compiler_params=_cparams(32),
    )(xf, gf, tgt)


def _bwd_norm_matmul(name, dy, wg, x, g, dres, dep):
    bn = wg.shape[2]
    ncol = dy.shape[1]

    def body(dy_ref, w_ref, x_ref, g_ref, dres_ref, dep_ref, dx_ref, dxb_ref, dg_ref):
        i = pl.program_id(0)

        @pl.when(i == 0)
        def _():
            dg_ref[...] = jnp.zeros(dg_ref.shape, F32)

        dh = jnp.zeros((TM, D), F32)
        for j in range(N_DEV):
            dh = dh + _dot_nt(dy_ref[:, j * bn:(j + 1) * bn], w_ref[j])
        xv = x_ref[...]
        dxn, dg = _rms_bwd(dh, xv, _rms(xv), g_ref[...])
        dx = dres_ref[...] + dxn
        dx_ref[...] = dx
        dxb_ref[...] = dx.astype(BF16)
        dg_ref[...] += dg

    return pl.pallas_call(
        body, name=name, grid=(T // TM,),
        in_specs=[_rows(TM, ncol), _full(wg.shape), _rows(TM, D), _full((1, D)), _rows(TM, D), ANY],
        out_specs=[_rows(TM, D), _rows(TM, D), _full((1, D))],
        out_shape=[jax.ShapeDtypeStruct((T, D), F32), jax.ShapeDtypeStruct((T, D), BF16),
                   jax.ShapeDtypeStruct((1, D), F32)],
        compiler_params=_cparams(48),
    )(dy, wg, x, g, dres, dep)


def _bwd_mlp(dxb, dres, w2, r, wg1, x1, g2, dep):
    bn = wg1.shape[2]

    def body(d_ref, dres_ref, w2_ref, r_ref, w1_ref, x_ref, g_ref, dep_ref, df1_ref, dx_ref, dxb_ref, dg_ref):
        i = pl.program_id(0)

        @pl.when(i == 0)
        def _():
            dg_ref[...] = jnp.zeros(dg_ref.shape, F32)

        d = d_ref[...]
        dh = jnp.zeros((TM, D), F32)
        for j in range(N_DEV):
            cols = slice(j * bn, (j + 1) * bn)
            df1 = (2.0 * r_ref[:, cols].astype(F32) * _dot_nt(d, w2_ref[cols, :])).astype(BF16)
            df1_ref[:, cols] = df1
            dh = dh + _dot_nt(df1, w1_ref[j])
        xv = x_ref[...]
        dxn, dg = _rms_bwd(dh, xv, _rms(xv), g_ref[...])
        dx = dres_ref[...] + dxn
        dx_ref[...] = dx
        dxb_ref[...] = dx.astype(BF16)
        dg_ref[...] += dg

    once = dict(pipeline_mode=pl.Buffered(1))
    return pl.pallas_call(
        body, name="bwd_mlp", grid=(T // TM,),
        in_specs=[_rows(TM, D), _rows(TM, D), pl.BlockSpec((D_FF, D), lambda i: (0, 0), **once), _rows(TM, D_FF),
                  pl.BlockSpec(wg1.shape, lambda i: (0, 0, 0), **once), _rows(TM, D), _full((1, D)), ANY],
        out_specs=[_rows(TM, D_FF), _rows(TM, D), _rows(TM, D), _full((1, D))],
        out_shape=[jax.ShapeDtypeStruct((T, D_FF), BF16), jax.ShapeDtypeStruct((T, D), F32),
                   jax.ShapeDtypeStruct((T, D), BF16), jax.ShapeDtypeStruct((1, D), F32)],
        compiler_params=_cparams(56),
    )(dxb, dres, w2, r, wg1, x1, g2, dep)


def _mixer_bwd_a(dxb, wout, proj, c, lg, lb, wm, bst, clg, clb, dep):
    tm = TM_SGU
    n_tiles = T // tm

    def body(dx_ref, wo_ref, p_ref, c_ref, lg_ref, lb_ref, wm_ref, bs_ref, clg_ref, clb_ref, dep_ref,
             dpa_ref, dc_ref, dlg_ref, dlb_ref, dwm_ref, dbs_ref, dcb_ref, dclg_ref, dclb_ref,
             dv_buf, db_acc):
        i = pl.program_id(0)

        @pl.when(i == 0)
        def _():
            for ref in (dlg_ref, dlb_ref, dwm_ref, dbs_ref, dcb_ref, dclg_ref, dclb_ref, db_acc):
                ref[...] = jnp.zeros(ref.shape, F32)

        dmix = _dot_nt(dx_ref[...], wo_ref[...])
        ua = p_ref[:, 0:D_A]
        va = p_ref[:, D_A:2 * D_A]
        u, cdf_u = _gelu(ua)
        vg, cdf_v = _gelu(va)
        xhat, rstd = _ln(vg)
        v = (xhat * lg_ref[...] + lb_ref[...]).astype(BF16)
        mask = _tril_mask()
        for h in range(H_A):
            hc = slice(h * CHUNK, (h + 1) * CHUNK)
            wmh = jnp.where(mask, wm_ref[h], 0.0).astype(BF16)
            for cidx in range(tm // CHUNK):
                rc = slice(cidx * CHUNK, (cidx + 1) * CHUNK)
                vb = v[rc, hc]
                mixed = _dot(wmh, vb) + bs_ref[:, h:h + 1]
                da = dmix[rc, hc]
                dpa_ref[rc, hc] = (da * mixed * _gelu_grad(ua[rc, hc], cdf_u[rc, hc])).astype(BF16)
                dmixed = da * u[rc, hc]
                dmb = dmixed.astype(BF16)
                dv_buf[rc, hc] = _dot_tn(wmh, dmb)
                dwm_ref[h] += _dot_nt(dmb, vb)
                db_acc[:, hc] += dmixed
        dv = dv_buf[...]
        dlb_ref[...] += jnp.sum(dv, axis=0, keepdims=True)
        dlg_ref[...] += jnp.sum(dv * xhat, axis=0, keepdims=True)
        dvg = _ln_bwd(dv * lg_ref[...], xhat, rstd)
        dpa_ref[:, D_A:2 * D_A] = (dvg * _gelu_grad(va, cdf_v)).astype(BF16)

        for q in range(H_B):
            cols = slice(q * LANES, (q + 1) * LANES)
            chat, crstd = _ln(c_ref[:, cols])
            z = chat * clg_ref[:, cols] + clb_ref[:, cols]
            sg = _sigmoid(z)
            dz = dmix[:, D_A + q * LANES:D_A + (q + 1) * LANES] * (sg * (1.0 + z * (1.0 - sg)))
            dclb_ref[:, cols] += jnp.sum(dz, axis=0, keepdims=True)
            dclg_ref[:, cols] += jnp.sum(dz * chat, axis=0, keepdims=True)
            dc = _ln_bwd(dz * clg_ref[:, cols], chat, crstd)
            dc_ref[:, cols] = dc
            dcb_ref[:, cols] += jnp.sum(dc, axis=0, keepdims=True)

        @pl.when(i == n_tiles - 1)
        def _():
            for h in range(H_A):
                hc = slice(h * CHUNK, (h + 1) * CHUNK)
                dwm_ref[h] = jnp.where(mask, dwm_ref[h], 0.0)
                dbs_ref[h:h + 1, :] = jnp.sum(db_acc[:, hc].T, axis=0, keepdims=True)

    vec = _full((1, D_A))
    vshape = jax.ShapeDtypeStruct((1, D_A), F32)
    return pl.pallas_call(
        body, name="mixer_bwd_a", grid=(n_tiles,),
        in_specs=[_rows(tm, D), _full((D, D)), _rows(tm, IN_COLS), _rows(tm, D_B), vec, vec,
                  _full((H_A, CHUNK, CHUNK)), _full((CHUNK, H_A)), vec, vec, ANY],
        out_specs=[_rows(tm, 2 * D_A), _rows(tm, D_B), vec, vec, _full((H_A, CHUNK, CHUNK)),
                   _full((H_A, CHUNK)), vec, vec, vec],
        out_shape=[jax.ShapeDtypeStruct((T, 2 * D_A), BF16), jax.ShapeDtypeStruct((T, D_B), F32), vshape, vshape,
                   jax.ShapeDtypeStruct((H_A, CHUNK, CHUNK), F32), jax.ShapeDtypeStruct((H_A, CHUNK), F32),
                   vshape, vshape, vshape],
        scratch_shapes=[pltpu.VMEM((tm, D_A), F32), pltpu.VMEM((CHUNK, D_A), F32)],
        compiler_params=_cparams(32),
    )(dxb, wout, proj, c, lg, lb, wm, bst, clg, clb, dep)


def _mixer_bwd_b(dc, g, proj, cw, dpa):
    tm = TM_MIX
    n_tiles = T // tm
    hb = tm // HALO

    def body(dc_ref, dch_ref, g_ref, gh_ref, p_ref, cw_ref, dpa_ref, dp_ref, dcw_ref, dcbuf, gbuf, dwacc):
        i = pl.program_id(0)

        @pl.when(i == 0)
        def _():
            dwacc[...] = jnp.zeros(dwacc.shape, F32)

        _fill_shift_buffer(dcbuf, 0, dc_ref[...])
        _fill_shift_buffer(dcbuf, tm, jnp.where(i < n_tiles - 1, dch_ref[...], 0.0))
        _fill_shift_buffer(gbuf, 0, jnp.where(i > 0, gh_ref[...], 0.0))
        _fill_shift_buffer(gbuf, HALO, g_ref[...])
        _build_shifts(dcbuf)
        _build_shifts(gbuf)
        dp_ref[:, 0:2 * D_A] = dpa_ref[...]
        for q in range(H_B):
            cols = slice(q * LANES, (q + 1) * LANES)

            def row_block(rb, carry, q=q, cols=cols):
                base = pl.multiple_of(rb * RB, RB)
                rows = pl.ds(base, RB)
                dg = _conv_taps(dcbuf, cw_ref, q, base, CONV_W - 1, -1)
                val = p_ref[rows, cols]
                sg = _sigmoid(p_ref[rows, D_B + q * LANES:D_B + (q + 1) * LANES])
                dp_ref[rows, 2 * D_A + q * LANES:2 * D_A + (q + 1) * LANES] = (dg * sg).astype(BF16)
                dp_ref[rows, 2 * D_A + D_B + q * LANES:2 * D_A + D_B + (q + 1) * LANES] = (
                    dg * val * sg * (1.0 - sg)).astype(BF16)
                dcv = dcbuf[0, q, rows, :]
                parts = []
                for k in range(CONV_W):
                    prod = dcv * _shifted(gbuf, q, base, HALO - (CONV_W - 1) + k)
                    parts.append(jnp.sum(prod.reshape(RB // 8, 8, LANES), axis=0))
                for k in range(CONV_W):
                    dwacc[k * 8:(k + 1) * 8, cols] += parts[k]
                return carry

            lax.fori_loop(0, tm // RB, row_block, 0)

        @pl.when(i == n_tiles - 1)
        def _():
            for k in range(CONV_W):
                dcw_ref[k:k + 1, :] = jnp.sum(dwacc[k * 8:(k + 1) * 8, :], axis=0, keepdims=True)

    return pl.pallas_call(
        body, name="mixer_bwd_b", grid=(n_tiles,),
        in_specs=[_rows(tm, D_B),
                  pl.BlockSpec((HALO, D_B), lambda i: (jnp.minimum((i + 1) * hb, T // HALO - 1), 0)),
                  _rows(tm, D_B),
                  pl.BlockSpec((HALO, D_B), lambda i: (jnp.maximum(i * hb - 1, 0), 0)),
                  pl.BlockSpec((tm, 2 * D_B), lambda i: (i, 1)),
                  _full((CONV_W, D_B)), _rows(tm, 2 * D_A)],
        out_specs=[_rows(tm, IN_COLS), _full((CONV_W, D_B))],
        out_shape=[jax.ShapeDtypeStruct((T, IN_COLS), BF16), jax.ShapeDtypeStruct((CONV_W, D_B), F32)],
        scratch_shapes=[pltpu.VMEM((8, H_B, tm + HALO, LANES), F32), pltpu.VMEM((8, H_B, HALO + tm, LANES), F32),
                        pltpu.VMEM((CONV_W * 8, D_B), F32)],
        compiler_params=_cparams(32),
    )(dc, dc, g, g, proj, cw, dpa)


def _wgrad(name, a, g, bn, square_a=False):
    k = a.shape[1]
    n = g.shape[1]
    tk = min(k, 1024)
    tn = min(n, max(bn, 1024))
    nsub = tn // bn
    tt = TT_WGRAD
    nt = T // tt

    def body(a_ref, g_ref, o_ref, ob_ref):
        t = pl.program_id(2)

        @pl.when(t == 0)
        def _():
            o_ref[...] = jnp.zeros(o_ref.shape, F32)

        av = a_ref[...]
        if square_a:
            av = av * av
        for s in range(nsub):
            o_ref[s] += _dot_tn(av, g_ref[:, s * bn:(s + 1) * bn])

        @pl.when(t == nt - 1)
        def _():
            ob_ref[...] = o_ref[...].astype(BF16)

    ospec = pl.BlockSpec((nsub, tk, bn), lambda ki, ni, ti: (ni, ki, 0))
    return pl.pallas_call(
        body, name=name, grid=(k // tk, n // tn, nt),
        in_specs=[pl.BlockSpec((tt, tk), lambda ki, ni, ti: (ti, ki)),
                  pl.BlockSpec((tt, tn), lambda ki, ni, ti: (ti, ni))],
        out_specs=[ospec, ospec],
        out_shape=[jax.ShapeDtypeStruct((n // bn, k, bn), F32), jax.ShapeDtypeStruct((n // bn, k, bn), BF16)],
        compiler_params=_cparams(40, ("arbitrary", "arbitrary", "arbitrary")),
    )(a, g)


def _coords():
    return lax.axis_index("x"), lax.axis_index("y"), lax.axis_index("c")


def _flat(x, y, c):
    return 4 * x + 2 * y + c


def _peer(k):
    x, y, c = _coords()
    return (x ^ ((k >> 2) & 1), y ^ ((k >> 1) & 1), c ^ (k & 1))


def _hbm(a):
    return pltpu.with_memory_space_constraint(a, pltpu.HBM)


def _hbm_like(a):
    return pltpu.HBM(a.shape, a.dtype)


def _peer_sems():
    return pltpu.SemaphoreType.DMA((N_DEV - 1,))


def _place_own(shards):
    n = len(shards)
    shapes = [(s.shape if l is None else s.shape[1:]) for s, l, _ in shards]

    def body(*refs):
        ins, outs = refs[:n], refs[n:2 * n]
        stage_in, stage_out = refs[2 * n:3 * n], refs[3 * n:4 * n]
        in_sems, out_sems = refs[4 * n], refs[4 * n + 1]
        me = _flat(*_coords())
        srcs = [ins[a] if shards[a][1] is None else ins[a].at[shards[a][1]] for a in range(n)]
        loads = [pltpu.make_async_copy(srcs[a], stage_in[a], in_sems.at[a]) for a in range(n)]
        stores = [pltpu.make_async_copy(stage_out[a], outs[a].at[me], out_sems.at[a]) for a in range(n)]
        for cp in loads:
            cp.start()
        for a in range(n):
            loads[a].wait()
            stage_out[a][...] = stage_in[a][...].astype(stage_out[a].dtype)
            stores[a].start()
        for cp in stores:
            cp.wait()

    return pl.pallas_call(
        body, name="place_own", in_specs=[ANY] * n, out_specs=[ANY] * n,
        out_shape=[jax.ShapeDtypeStruct((N_DEV,) + shapes[a], shards[a][2]) for a in range(n)],
        scratch_shapes=[*[pltpu.VMEM(shapes[a], shards[a][0].dtype) for a in range(n)],
                        *[pltpu.VMEM(shapes[a], shards[a][2]) for a in range(n)],
                        pltpu.SemaphoreType.DMA((n,)), pltpu.SemaphoreType.DMA((n,))],
        compiler_params=pltpu.CompilerParams(vmem_limit_bytes=40 * MIB),
    )(*[s for s, _, _ in shards])


SIBLING = 1
CHIP_PEERS = (2, 4, 6)
FIRST_PEERS = (SIBLING,) + CHIP_PEERS


def _gather_start(lands):
    n = len(lands)

    def body(*refs):
        lnd, send, recv = refs[:n], refs[n:2 * n], refs[2 * n:3 * n]
        me = _flat(*_coords())
        for a in range(n):
            for j, k in enumerate(FIRST_PEERS):
                pltpu.make_async_remote_copy(
                    src_ref=lnd[a].at[me], dst_ref=lnd[a].at[me], send_sem=send[a].at[j],
                    recv_sem=recv[a].at[j], device_id=_peer(k), device_id_type=MESH_ID).start()

    sems = pltpu.SemaphoreType.DMA((len(FIRST_PEERS),))
    outs = pl.pallas_call(
        body, name="gather_start",
        out_shape=(*[sems] * (2 * n), *[_hbm_like(l) for l in lands]),
        in_specs=[HBM] * n, out_specs=(*[SEM] * (2 * n), *[HBM] * n),
        input_output_aliases={i: 2 * n + i for i in range(n)},
        compiler_params=pltpu.CompilerParams(has_side_effects=EFFECT),
    )(*[_hbm(l) for l in lands])
    return outs[:n], outs[n:2 * n], outs[2 * n:]


def _gather_pass_on(name, land, send, recv, after):
    def body(l_ref, send_ref, recv_ref, after_ref, send2, recv2, l_out):
        sibling = _peer(SIBLING)
        for j, k in enumerate(CHIP_PEERS):
            cp = pltpu.make_async_remote_copy(
                src_ref=l_ref.at[0], dst_ref=l_ref.at[0], send_sem=send_ref.at[1 + j], recv_sem=recv_ref.at[1 + j],
                device_id=_peer(k), device_id_type=MESH_ID)
            cp.wait_send()
            cp.wait_recv()
            blk = _flat(*_peer(k))
            pltpu.make_async_remote_copy(
                src_ref=l_ref.at[blk], dst_ref=l_ref.at[blk], send_sem=send2.at[j], recv_sem=recv2.at[j],
                device_id=sibling, device_id_type=MESH_ID).start()

    sems = pltpu.SemaphoreType.DMA((len(CHIP_PEERS),))
    return pl.pallas_call(
        body, name=name, out_shape=(sems, sems, _hbm_like(land)),
        in_specs=(HBM, SEM, SEM, ANY), out_specs=(SEM, SEM, HBM), input_output_aliases={0: 2},
        compiler_params=pltpu.CompilerParams(has_side_effects=EFFECT),
    )(land, send, recv, after)


def _gather_wait(name, land, send, recv, send2, recv2, after):
    def body(l_ref, send_ref, recv_ref, send2_ref, recv2_ref, after_ref, l_out):
        sibling = _peer(SIBLING)
        own = pltpu.make_async_remote_copy(
            src_ref=l_ref.at[0], dst_ref=l_ref.at[0], send_sem=send_ref.at[0], recv_sem=recv_ref.at[0],
            device_id=sibling, device_id_type=MESH_ID)
        own.wait_send()
        own.wait_recv()
        for j in range(len(CHIP_PEERS)):
            cp = pltpu.make_async_remote_copy(
                src_ref=l_ref.at[0], dst_ref=l_ref.at[0], send_sem=send2_ref.at[j], recv_sem=recv2_ref.at[j],
                device_id=sibling, device_id_type=MESH_ID)
            cp.wait_send()
            cp.wait_recv()

    return pl.pallas_call(
        body, name=name, out_shape=_hbm_like(land),
        in_specs=(HBM, SEM, SEM, SEM, SEM, ANY), out_specs=HBM, input_output_aliases={0: 0},
        compiler_params=pltpu.CompilerParams(has_side_effects=EFFECT),
    )(land, send, recv, send2, recv2, after)


def _scatter_start(name, grads, lands, layer):
    n = len(grads)

    def body(*refs):
        g, lnd, send, recv = refs[:n], refs[n:2 * n], refs[2 * n:3 * n], refs[3 * n:4 * n]
        for a in range(n):
            for k in range(1, N_DEV):
                to = _peer(k)
                pltpu.make_async_remote_copy(
                    src_ref=g[a].at[_flat(*to)], dst_ref=lnd[a].at[layer, k - 1], send_sem=send[a].at[k - 1],
                    recv_sem=recv[a].at[k - 1], device_id=to, device_id_type=MESH_ID).start()

    outs = pl.pallas_call(
        body, name=name,
        out_shape=(*[_peer_sems()] * (2 * n), *[_hbm_like(g) for g in grads], *[_hbm_like(l) for l in lands]),
        in_specs=[HBM] * (2 * n), out_specs=(*[SEM] * (2 * n), *[HBM] * (2 * n)),
        input_output_aliases={i: 2 * n + i for i in range(2 * n)},
        compiler_params=pltpu.CompilerParams(has_side_effects=EFFECT),
    )(*[_hbm(g) for g in grads], *[_hbm(l) for l in lands])
    return outs[:n], outs[n:2 * n], outs[2 * n:3 * n], outs[3 * n:]


def _scatter_wait(grads, lands, sends, recvs, after):
    n, nw = len(grads), len(lands)

    def body(*refs):
        g, lnd = refs[:n], refs[n:n + nw]
        send, recv = refs[n + nw:2 * n + nw], refs[2 * n + nw:3 * n + nw]
        for a in range(n):
            for k in range(1, N_DEV):
                cp = pltpu.make_async_remote_copy(
                    src_ref=g[a].at[0], dst_ref=lnd[a // DEPTH].at[a % DEPTH, 0],
                    send_sem=send[a].at[k - 1], recv_sem=recv[a].at[k - 1],
                    device_id=_peer(k), device_id_type=MESH_ID)
                cp.wait_send()
                cp.wait_recv()

    outs = pl.pallas_call(
        body, name="scatter_wait",
        out_shape=(*[_hbm_like(g) for g in grads], *[_hbm_like(l) for l in lands]),
        in_specs=(*[HBM] * (n + nw), *[SEM] * (2 * n), ANY), out_specs=[HBM] * (n + nw),
        input_output_aliases={i: i for i in range(n + nw)},
        compiler_params=pltpu.CompilerParams(has_side_effects=EFFECT),
    )(*grads, *lands, *sends, *recvs, after)
    return outs[n:]


def _own_blocks(grads_f32):
    nw = len(grads_f32)
    flat32 = [g for w in grads_f32 for g in w]
    n = len(flat32)

    def body(*refs):
        g32, own, stage = refs[:n], refs[n:n + nw], refs[n + nw:2 * n + nw]
        in_sems, out_sems = refs[2 * n + nw], refs[2 * n + nw + 1]
        me = _flat(*_coords())
        loads = [pltpu.make_async_copy(g32[a].at[me], stage[a], in_sems.at[a]) for a in range(n)]
        stores = [pltpu.make_async_copy(stage[a], own[a // DEPTH].at[a % DEPTH], out_sems.at[a]) for a in range(n)]
        for cp in loads:
            cp.start()
        for a in range(n):
            loads[a].wait()
            stores[a].start()
        for cp in stores:
            cp.wait()

    return pl.pallas_call(
        body, name="own_blocks", in_specs=[ANY] * n, out_specs=[ANY] * nw,
        out_shape=[jax.ShapeDtypeStruct((DEPTH,) + w[0].shape[1:], F32) for w in grads_f32],
        scratch_shapes=[*[pltpu.VMEM(g.shape[1:], F32) for g in flat32],
                        pltpu.SemaphoreType.DMA((n,)), pltpu.SemaphoreType.DMA((n,))],
        compiler_params=pltpu.CompilerParams(vmem_limit_bytes=24 * MIB),
    )(*flat32)


def _all_reduce_small(part):
    rows = part.shape[0]
    br = rows // N_DEV
    assert br * N_DEV == rows and br % 8 == 0

    def body(p_ref, o_ref, slots, send1, recv1, send2, recv2):
        me = _flat(*_coords())

        def block(ref, d):
            return ref.at[pl.ds(pl.multiple_of(d * br, 8), br), :]

        slots[me] = p_ref[pl.ds(pl.multiple_of(me * br, 8), br), :]
        scatter, gather = [], []
        for k in range(1, N_DEV):
            to = _peer(k)
            scatter.append(pltpu.make_async_remote_copy(
                src_ref=block(p_ref, _flat(*to)), dst_ref=slots.at[me],
                send_sem=send1.at[k - 1], recv_sem=recv1.at[k - 1], device_id=to, device_id_type=MESH_ID))
            gather.append(pltpu.make_async_remote_copy(
                src_ref=block(o_ref, me), dst_ref=block(o_ref, me),
                send_sem=send2.at[k - 1], recv_sem=recv2.at[k - 1], device_id=to, device_id_type=MESH_ID))
        for cp in scatter:
            cp.start()
        for cp in scatter:
            cp.wait()
        acc = slots[0]
        for d in range(1, N_DEV):
            acc = acc + slots[d]
        o_ref[pl.ds(pl.multiple_of(me * br, 8), br), :] = acc
        for cp in gather:
            cp.start()
        for cp in gather:
            cp.wait()

    return pl.pallas_call(
        body, name="all_reduce_small",
        in_specs=[pl.BlockSpec(memory_space=pltpu.VMEM)], out_specs=pl.BlockSpec(memory_space=pltpu.VMEM),
        out_shape=jax.ShapeDtypeStruct(part.shape, F32),
        scratch_shapes=[pltpu.VMEM((N_DEV, br, LANES), F32)] + [pltpu.SemaphoreType.DMA((N_DEV - 1,))] * 4,
        compiler_params=pltpu.CompilerParams(vmem_limit_bytes=24 * MIB),
    )(part)


def _adam_math(w, g, m, v):
    m = ADAM_B1 * m + (1.0 - ADAM_B1) * g
    v = ADAM_B2 * v + (1.0 - ADAM_B2) * (g * g)
    m_hat = m / (1.0 - ADAM_B1 ** ADAM_STEP)
    v_hat = v / (1.0 - ADAM_B2 ** ADAM_STEP)
    delta = -ADAM_LR * (m_hat / (jnp.sqrt(v_hat) + ADAM_EPS) + ADAM_WD * w)
    return delta, m, v


def _adam_sharded(name, own, recv, w, m, v):
    _, r, c = w.shape
    tr = min(r, 256)

    def body(own_ref, recv_ref, w_ref, m_ref, v_ref, g_ref, d_ref, nm_ref, nv_ref):
        g = own_ref[...]
        for k in range(N_DEV - 1):
            g = g + recv_ref[k].astype(F32)
        delta, nm, nv = _adam_math(w_ref[...], g, m_ref[...], v_ref[...])
        g_ref[...] = g
        d_ref[...] = delta
        nm_ref[...] = nm
        nv_ref[...] = nv

    blk = pl.BlockSpec((None, tr, c), lambda l, i: (l, i, 0))
    shp = jax.ShapeDtypeStruct(w.shape, F32)
    return pl.pallas_call(
        body, name=name, grid=(DEPTH, r // tr),
        in_specs=[blk, pl.BlockSpec((None, N_DEV - 1, tr, c), lambda l, i: (l, 0, i, 0)), blk, blk, blk],
        out_specs=[blk] * 4, out_shape=[shp] * 4,
        compiler_params=_cparams(32, ("arbitrary", "arbitrary")),
    )(own, recv, w, m, v)


def _adam_small(g, w, m, v):
    rows = g.shape[0]

    def body(g_ref, w_ref, m_ref, v_ref, d_ref, nm_ref, nv_ref):
        delta, nm, nv = _adam_math(w_ref[...], g_ref[...], m_ref[...], v_ref[...])
        d_ref[...] = delta
        nm_ref[...] = nm
        nv_ref[...] = nv

    spec = _full((rows, LANES))
    shp = jax.ShapeDtypeStruct((rows, LANES), F32)
    return pl.pallas_call(
        body, name="adam_small", grid=(1,),
        in_specs=[spec] * 4, out_specs=[spec] * 3, out_shape=[shp] * 3,
        compiler_params=_cparams(24),
    )(g, w, m, v)


SMALL = ["norm1_g", "sgu_ln_g", "sgu_ln_b", "sgu_w", "sgu_b", "conv_b", "conv_ln_g", "conv_ln_b", "norm2_g",
         "final_g"]


def _pack(arrays):
    flat = jnp.concatenate([a.reshape(-1) for a in arrays])
    pad = (-flat.shape[0]) % (N_DEV * 8 * LANES)
    return jnp.pad(flat, (0, pad)).reshape(-1, LANES)


def _unpack(packed, shapes):
    flat = packed.reshape(-1)
    out, off = [], 0
    for s in shapes:
        size = 1
        for d in s:
            size *= d
        out.append(flat[off:off + size].reshape(s))
        off += size
    return out


def kernel(x, norm1_g, w_in, sgu_ln_g, sgu_ln_b, sgu_w, sgu_b, conv_w, conv_b, conv_ln_g, conv_ln_b, w_out, norm2_g, w_ff1, w_ff2, final_g, loss_target, m_norm1_g, m_w_in, m_sgu_ln_g, m_sgu_ln_b, m_sgu_w, m_sgu_b, m_conv_w, m_conv_b, m_conv_ln_g, m_conv_ln_b, m_w_out, m_norm2_g, m_w_ff1, m_w_ff2, m_final_g, v_norm1_g, v_w_in, v_sgu_ln_g, v_sgu_ln_b, v_sgu_w, v_sgu_b, v_conv_w, v_conv_b, v_conv_ln_g, v_conv_ln_b, v_w_out, v_norm2_g, v_w_ff1, v_w_ff2, v_final_g):
    x2d = x.reshape(T, D)
    tgt = loss_target.reshape(T, D)
    cw_shard = conv_w.reshape(CONV_W, LANES)

    gnames, shards = [], []
    for l in range(DEPTH):
        for k, w in (("w_in", w_in), ("w_out", w_out), ("w_ff1", w_ff1), ("w_ff2", w_ff2)):
            gnames.append(f"{k}{l}")
            shards.append((w, l, BF16))
        if l == 0:
            gnames.insert(1, "conv_w")
            shards.insert(1, (cw_shard, None, F32))
    sends, recvs, lands = _gather_start(_place_own(shards))
    gidx = {k: i for i, k in enumerate(gnames)}

    passed = {}

    def pass_on(k, after):
        i = gidx[k]
        passed[k] = _gather_pass_on("gather_pass_on_" + k, lands[i], sends[i], recvs[i], after)

    def gathered(k, after):
        i = gidx[k]
        send2, recv2, land = passed[k]
        return _gather_wait("gather_wait_" + k, land, sends[i], recvs[i], send2, recv2, after)

    saved = []
    xl = x2d
    cw_full = None
    for l in range(DEPTH):
        g1 = norm1_g[l].reshape(1, D)
        g2 = norm2_g[l].reshape(1, D)
        lg, lb = sgu_ln_g[l].reshape(1, D_A), sgu_ln_b[l].reshape(1, D_A)
        bst = sgu_b[l].T
        cb = conv_b[l].reshape(1, D_B)
        clg, clb = conv_ln_g[l].reshape(1, D_B), conv_ln_b[l].reshape(1, D_B)
        if l == 0:
            pass_on("w_in0", xl)
        wg_in = gathered(f"w_in{l}", xl)
        if l == 0:
            pass_on("conv_w", xl)
        h, proj = _fwd_in(xl, g1, wg_in)
        if l == 0:
            cw_full = gathered("conv_w", proj).reshape(N_DEV, DEPTH, CONV_W, D_B // N_DEV).transpose(
                1, 2, 0, 3).reshape(DEPTH, CONV_W, D_B)
        pass_on(f"w_out{l}", proj)
        mix, c, g = _mixer_fwd(proj, lg, lb, sgu_w[l], bst, cw_full[l], cb, clg, clb)
        pass_on(f"w_ff1{l}", mix)
        wout = gathered(f"w_out{l}", mix).reshape(D, D)
        x1, h2 = _fwd_out(mix, wout, xl, g2)
        pass_on(f"w_ff2{l}", h2)
        wg_ff1 = gathered(f"w_ff1{l}", h2)
        r = _fwd_ff1(h2, wg_ff1)
        if l + 1 < DEPTH:
            pass_on(f"w_in{l + 1}", r)
        w2 = gathered(f"w_ff2{l}", r).reshape(D_FF, D)
        x2 = _fwd_ff2(r, w2, x1)
        saved.append(dict(x=xl, h=h, proj=proj, mix=mix, c=c, g=g, x1=x1, h2=h2, r=r, wg_in=wg_in, wout=wout,
                          wg_ff1=wg_ff1, w2=w2, g1=g1, g2=g2, lg=lg, lb=lb, bst=bst, clg=clg, clb=clb))
        xl = x2

    dx, dxb, loss_part, d_final_g = _loss_bwd(xl, final_g.reshape(1, D), tgt)

    names = ["w_in", "w_out", "w_ff1", "w_ff2"]
    block = dict(w_in=(D, IN_COLS // N_DEV), w_out=(D // N_DEV, D), w_ff1=(D, D_FF // N_DEV), w_ff2=(D_FF // N_DEV, D))
    land = {k: lax.empty((DEPTH, N_DEV - 1) + block[k], BF16) for k in names}
    big32 = {k: [None] * DEPTH for k in names}
    big16 = {k: [None] * DEPTH for k in names}
    ssend = {k: [None] * DEPTH for k in names}
    srecv = {k: [None] * DEPTH for k in names}

    def send_grads(l, grads):
        ks = list(grads)
        for k in ks:
            big32[k][l] = grads[k][0].reshape((N_DEV,) + block[k])
        sends, recvs, g16s, lands = _scatter_start(
            "scatter_start_" + "_".join(ks) + str(l), [grads[k][1].reshape((N_DEV,) + block[k]) for k in ks],
            [land[k] for k in ks], l)
        for i, k in enumerate(ks):
            ssend[k][l], srecv[k][l], big16[k][l], land[k] = sends[i], recvs[i], g16s[i], lands[i]

    small = {}
    d_conv_w = [None] * DEPTH
    for l in reversed(range(DEPTH)):
        s = saved[l]
        g_ff2 = _wgrad("wgrad_ff2", s["r"], dxb, D, square_a=True)
        df1, dx1, dx1b, dg2 = _bwd_mlp(dxb, dx, s["w2"], s["r"], s["wg_ff1"], s["x1"], s["g2"], g_ff2[1])
        send_grads(l, dict(w_ff2=g_ff2, w_ff1=_wgrad("wgrad_ff1", s["h2"], df1, D_FF // N_DEV)))
        g_out = _wgrad("wgrad_out", s["mix"], dx1b, D)
        dpa, dc, dlg, dlb, dwm, dbs, dcb, dclg, dclb = _mixer_bwd_a(
            dx1b, s["wout"], s["proj"], s["c"], s["lg"], s["lb"], sgu_w[l], s["bst"], s["clg"], s["clb"],
            big16["w_ff1"][l])
        dproj, d_conv_w[l] = _mixer_bwd_b(dc, s["g"], s["proj"], cw_full[l], dpa)
        send_grads(l, dict(w_out=g_out, w_in=_wgrad("wgrad_in", s["h"], dproj, IN_COLS // N_DEV)))
        dx, dxb, dg1 = _bwd_norm_matmul("bwd_in", dproj, s["wg_in"], s["x"], s["g1"], dx1, big16["w_in"][l])
        small[l] = dict(norm1_g=dg1, sgu_ln_g=dlg, sgu_ln_b=dlb, sgu_w=dwm, sgu_b=dbs, conv_b=dcb, conv_ln_g=dclg,
                        conv_ln_b=dclb, norm2_g=dg2)
    grad_x = dx.reshape(1, T, D)

    own = _own_blocks([big32[k] for k in names])
    recv = _scatter_wait([g for k in names for g in big16[k]], [land[k] for k in names],
                         [q for k in names for q in ssend[k]], [q for k in names for q in srecv[k]], dx)
    wmv = dict(w_in=(w_in, m_w_in, v_w_in), w_out=(w_out, m_w_out, v_w_out), w_ff1=(w_ff1, m_w_ff1, v_w_ff1),
               w_ff2=(w_ff2, m_w_ff2, v_w_ff2))
    res = {}
    for idx, k in enumerate(names):
        res[k] = _adam_sharded("adam_" + k, own[idx], recv[idx], *wmv[k])

    rep = dict(norm1_g=(norm1_g, m_norm1_g, v_norm1_g), sgu_ln_g=(sgu_ln_g, m_sgu_ln_g, v_sgu_ln_g),
               sgu_ln_b=(sgu_ln_b, m_sgu_ln_b, v_sgu_ln_b), sgu_w=(sgu_w, m_sgu_w, v_sgu_w),
               sgu_b=(sgu_b, m_sgu_b, v_sgu_b), conv_b=(conv_b, m_conv_b, v_conv_b),
               conv_ln_g=(conv_ln_g, m_conv_ln_g, v_conv_ln_g), conv_ln_b=(conv_ln_b, m_conv_ln_b, v_conv_ln_b),
               norm2_g=(norm2_g, m_norm2_g, v_norm2_g), final_g=(final_g, m_final_g, v_final_g))
    parts = []
    for k in SMALL:
        if k == "final_g":
            parts.append(d_final_g.reshape(rep[k][0].shape))
        else:
            parts.append(jnp.stack([small[l][k].reshape(rep[k][0].shape[1:]) for l in range(DEPTH)]))
    parts.append(jnp.stack(d_conv_w))
    parts.append(loss_part[0, 0:1])
    shapes = [p.shape for p in parts]
    summed = _unpack(_all_reduce_small(_pack(parts)), shapes)
    loss = summed[-1][0]
    me = _flat(*_coords())
    g_conv_w = lax.dynamic_slice_in_dim(summed[-2], me * (D_B // N_DEV), D_B // N_DEV, axis=2)
    g_small = summed[:-2] + [g_conv_w]
    keys = SMALL + ["conv_w"]
    rep["conv_w"] = (conv_w, m_conv_w, v_conv_w)
    shapes = [a.shape for a in g_small]
    packed = [_pack(g_small)] + [_pack([rep[k][i] for k in keys]) for i in range(3)]
    d_s, nm_s, nv_s = (_unpack(a, shapes) for a in _adam_small(*packed))
    for i, k in enumerate(keys):
        res[k] = (g_small[i], d_s[i], nm_s[i], nv_s[i])

    order = ["norm1_g", "w_in", "sgu_ln_g", "sgu_ln_b", "sgu_w", "sgu_b", "conv_w", "conv_b", "conv_ln_g",
             "conv_ln_b", "w_out", "norm2_g", "w_ff1", "w_ff2", "final_g"]
    return (loss, grad_x, *[res[k][0] for k in order], *[res[k][1] for k in order],
            *[res[k][2] for k in order], *[res[k][3] for k in order])
```

```python
import functools

import jax
import jax.numpy as jnp
from jax import lax
from jax.experimental import pallas as pl
from jax.experimental.pallas import tpu as pltpu

F32 = jnp.float32
BF16 = jnp.bfloat16

N_DEV = 8
DEPTH = 2
T = 4096
D = 1024
D_A = 512
D_B = 512
CHUNK = 128
H_A = 4
H_B = 4
CONV_W = 31
HALO = 32
D_FF = 4096
IN_COLS = 2048
EPS = 1e-6

ADAM_LR = 0.001
ADAM_B1 = 0.9
ADAM_B2 = 0.999
ADAM_EPS = 1e-08
ADAM_WD = 0.01
ADAM_STEP = 10

TM = 512
TM_FWD = 1024
TT_WGRAD = 1024
TM_MIX = 256
TM_SGU = 512
RB = 64
CONV_CHAINS = 2
LANES = 128
MIB = 1024 * 1024
SCOPED_VMEM_MIB = 60

SQRT_HALF = 0.7071067811865476
INV_SQRT_2PI = 0.3989422804014327

MESH_ID = pl.DeviceIdType.MESH
ANY = pl.BlockSpec(memory_space=pl.ANY)
HBM = pl.BlockSpec(memory_space=pltpu.HBM)
SEM = pl.BlockSpec(memory_space=pltpu.SEMAPHORE)
EFFECT = pltpu.SideEffectType.DATAFLOW_SIDE_EFFECTING


def _cparams(vmem_mib, sem=("arbitrary",)):
    assert vmem_mib <= SCOPED_VMEM_MIB
    return pltpu.CompilerParams(dimension_semantics=sem, vmem_limit_bytes=SCOPED_VMEM_MIB * MIB)


def _full(shape):
    return pl.BlockSpec(shape, lambda *_: (0,) * len(shape))


def _rows(tm, cols):
    return pl.BlockSpec((tm, cols), lambda i: (i, 0))


def _gelu(x):
    cdf = 0.5 * (1.0 + lax.erf(x * SQRT_HALF))
    return x * cdf, cdf


def _gelu_grad(x, cdf):
    return cdf + x * (INV_SQRT_2PI * jnp.exp(-0.5 * x * x))


def _sigmoid(x):
    return 1.0 / (1.0 + jnp.exp(-x))


def _ln(x):
    mu = jnp.mean(x, axis=-1, keepdims=True)
    xc = x - mu
    rstd = lax.rsqrt(jnp.mean(xc * xc, axis=-1, keepdims=True) + EPS)
    return xc * rstd, rstd


def _ln_bwd(dyh, xhat, rstd):
    return rstd * (dyh - jnp.mean(dyh, axis=-1, keepdims=True) - xhat * jnp.mean(dyh * xhat, axis=-1, keepdims=True))


def _rms(x):
    return lax.rsqrt(jnp.mean(x * x, axis=-1, keepdims=True) + EPS)


def _rms_bwd(dh, x, r, g):
    n = x * r
    dn = dh * g
    dx = r * (dn - n * jnp.mean(dn * n, axis=-1, keepdims=True))
    return dx, jnp.sum(dh * n, axis=0, keepdims=True)


def _dot(a, b):
    return jnp.dot(a, b, preferred_element_type=F32)


def _dot_nt(a, b):
    return lax.dot_general(a, b, (((1,), (1,)), ((), ())), preferred_element_type=F32)


def _dot_tn(a, b):
    return lax.dot_general(a, b, (((0,), (0,)), ((), ())), preferred_element_type=F32)


def _tril_mask():
    r = lax.broadcasted_iota(jnp.int32, (CHUNK, CHUNK), 0)
    c = lax.broadcasted_iota(jnp.int32, (CHUNK, CHUNK), 1)
    return r >= c


def _fwd_in(x, g1, wg):
    bn = wg.shape[2]

    def body(x_ref, g_ref, w_ref, h_ref, p_ref):
        xv = x_ref[...]
        h = (xv * _rms(xv) * g_ref[...]).astype(BF16)
        h_ref[...] = h
        for j in range(N_DEV):
            p_ref[:, j * bn:(j + 1) * bn] = _dot(h, w_ref[j])

    return pl.pallas_call(
        body, name="fwd_in", grid=(T // TM_FWD,),
        in_specs=[_rows(TM_FWD, D), _full((1, D)), _full(wg.shape)],
        out_specs=[_rows(TM_FWD, D), _rows(TM_FWD, IN_COLS)],
        out_shape=[jax.ShapeDtypeStruct((T, D), BF16), jax.ShapeDtypeStruct((T, IN_COLS), F32)],
        compiler_params=_cparams(32),
    )(x, g1, wg)


def _fill_shift_buffer(sh_ref, row0, value):
    for q in range(sh_ref.shape[1]):
        sh_ref[0, q, row0:row0 + value.shape[0], :] = value[:, q * LANES:(q + 1) * LANES]


def _build_shifts(sh_ref):
    rows = sh_ref.shape[2]
    for p in range(1, 8):
        for q in range(sh_ref.shape[1]):
            sh_ref[p, q, 0:rows - 8, :] = sh_ref[0, q, p:p + rows - 8, :]


def _shifted(sh_ref, q, base, off):
    start = base + (off - off % 8)
    if not isinstance(start, int):
        start = pl.multiple_of(start, 8)
    return sh_ref[off % 8, q, pl.ds(start, RB), :]


def _conv_taps(sh_ref, w_ref, q, base, first_tap_row, step):
    cols = slice(q * LANES, (q + 1) * LANES)
    acc = [jnp.zeros((RB, LANES), F32) for _ in range(CONV_CHAINS)]
    for k in range(CONV_W):
        term = _shifted(sh_ref, q, base, first_tap_row + step * k) * w_ref[k:k + 1, cols]
        acc[k % CONV_CHAINS] = acc[k % CONV_CHAINS] + term
    return functools.reduce(lambda a, b: a + b, acc)


def _mixer_fwd(proj, lg, lb, wm, bst, cw, cb, clg, clb):
    tm = TM_MIX
    hb = tm // HALO

    def body(p_ref, ph_ref, lg_ref, lb_ref, wm_ref, bs_ref, cw_ref, cb_ref, clg_ref, clb_ref,
             mix_ref, c_ref, g_ref, gbuf):
        i = pl.program_id(0)
        u, _ = _gelu(p_ref[:, 0:D_A])
        vg, _ = _gelu(p_ref[:, D_A:2 * D_A])
        xhat, _ = _ln(vg)
        v = (xhat * lg_ref[...] + lb_ref[...]).astype(BF16)
        mask = _tril_mask()
        for h in range(H_A):
            hc = slice(h * CHUNK, (h + 1) * CHUNK)
            wmh = jnp.where(mask, wm_ref[h], 0.0).astype(BF16)
            for c in range(tm // CHUNK):
                rc = slice(c * CHUNK, (c + 1) * CHUNK)
                mixed = _dot(wmh, v[rc, hc]) + bs_ref[:, h:h + 1]
                mix_ref[rc, hc] = (u[rc, hc] * mixed).astype(BF16)

        g = p_ref[:, 2 * D_A:2 * D_A + D_B] * _sigmoid(p_ref[:, 2 * D_A + D_B:IN_COLS])
        g_ref[...] = g
        gh = ph_ref[:, 0:D_B] * _sigmoid(ph_ref[:, D_B:2 * D_B])
        _fill_shift_buffer(gbuf, 0, jnp.where(i > 0, gh, 0.0))
        _fill_shift_buffer(gbuf, HALO, g)
        _build_shifts(gbuf)
        for q in range(H_B):
            cols = slice(q * LANES, (q + 1) * LANES)
            for rb in range(tm // RB):
                acc = _conv_taps(gbuf, cw_ref, q, rb * RB, HALO - (CONV_W - 1), 1)
                c_ref[rb * RB:(rb + 1) * RB, cols] = acc + cb_ref[:, cols]
        for q in range(H_B):
            cols = slice(q * LANES, (q + 1) * LANES)
            chat, _ = _ln(c_ref[:, cols])
            z = chat * clg_ref[:, cols] + clb_ref[:, cols]
            mix_ref[:, D_A + q * LANES:D_A + (q + 1) * LANES] = (z * _sigmoid(z)).astype(BF16)

    vec = _full((1, D_A))
    return pl.pallas_call(
        body, name="mixer_fwd", grid=(T // tm,),
        in_specs=[_rows(tm, IN_COLS),
                  pl.BlockSpec((HALO, 2 * D_B), lambda i: (jnp.maximum(i * hb - 1, 0), 1)),
                  vec, vec, _full((H_A, CHUNK, CHUNK)), _full((CHUNK, H_A)),
                  _full((CONV_W, D_B)), vec, vec, vec],
        out_specs=[_rows(tm, D), _rows(tm, D_B), _rows(tm, D_B)],
        out_shape=[jax.ShapeDtypeStruct((T, D), BF16), jax.ShapeDtypeStruct((T, D_B), F32),
                   jax.ShapeDtypeStruct((T, D_B), F32)],
        scratch_shapes=[pltpu.VMEM((8, H_B, HALO + tm, LANES), F32)],
        compiler_params=_cparams(32),
    )(proj, proj, lg, lb, wm, bst, cw, cb, clg, clb)


def _fwd_out(mix, wout, x, g2):
    def body(m_ref, w_ref, x_ref, g_ref, x1_ref, h2_ref):
        x1 = x_ref[...] + _dot(m_ref[...], w_ref[...])
        x1_ref[...] = x1
        h2_ref[...] = (x1 * _rms(x1) * g_ref[...]).astype(BF16)

    return pl.pallas_call(
        body, name="fwd_out", grid=(T // TM_FWD,),
        in_specs=[_rows(TM_FWD, D), _full((D, D)), _rows(TM_FWD, D), _full((1, D))],
        out_specs=[_rows(TM_FWD, D), _rows(TM_FWD, D)],
        out_shape=[jax.ShapeDtypeStruct((T, D), F32), jax.ShapeDtypeStruct((T, D), BF16)],
        compiler_params=_cparams(32),
    )(mix, wout, x, g2)


def _fwd_ff1(h2, wg):
    bn = wg.shape[2]

    def body(h_ref, w_ref, r_ref):
        h = h_ref[...]
        for j in range(N_DEV):
            r_ref[:, j * bn:(j + 1) * bn] = jnp.maximum(_dot(h, w_ref[j]), 0.0).astype(BF16)

    return pl.pallas_call(
        body, name="fwd_ff1", grid=(T // TM_FWD,),
        in_specs=[_rows(TM_FWD, D), _full(wg.shape)],
        out_specs=_rows(TM_FWD, D_FF),
        out_shape=jax.ShapeDtypeStruct((T, D_FF), BF16),
        compiler_params=_cparams(48),
    )(h2, wg)


def _fwd_ff2(r, w2, x1):
    def body(r_ref, w_ref, x_ref, o_ref):
        rv = r_ref[...]
        o_ref[...] = x_ref[...] + _dot(rv * rv, w_ref[...])

    return pl.pallas_call(
        body, name="fwd_ff2", grid=(T // TM_FWD,),
        in_specs=[_rows(TM_FWD, D_FF), _full((D_FF, D)), _rows(TM_FWD, D)],
        out_specs=_rows(TM_FWD, D),
        out_shape=jax.ShapeDtypeStruct((T, D), F32),
        compiler_params=_cparams(48),
    )(r, w2, x1)


def _loss_bwd(xf, gf, tgt):
    def body(x_ref, g_ref, t_ref, dx_ref, dxb_ref, loss_ref, dg_ref):
        i = pl.program_id(0)

        @pl.when(i == 0)
        def _():
            loss_ref[...] = jnp.zeros(loss_ref.shape, F32)
            dg_ref[...] = jnp.zeros(dg_ref.shape, F32)

        xv = x_ref[...]
        r = _rms(xv)
        diff = xv * r * g_ref[...] - t_ref[...]
        loss_ref[...] += 0.5 * jnp.sum(jnp.mean(diff * diff, axis=-1, keepdims=True), axis=0, keepdims=True)
        dx, dg = _rms_bwd(diff * (1.0 / D), xv, r, g_ref[...])
        dx_ref[...] = dx
        dxb_ref[...] = dx.astype(BF16)
        dg_ref[...] += dg

    return pl.pallas_call(
        body, name="loss_bwd", grid=(T // TM,),
        in_specs=[_rows(TM, D), _full((1, D)), _rows(TM, D)],
        out_specs=[_rows(TM, D), _rows(TM, D), _full((8, LANES)), _full((1, D))],
        out_shape=[jax.ShapeDtypeStruct((T, D), F32), jax.ShapeDtypeStruct((T, D), BF16),
                   jax.ShapeDtypeStruct((8, LANES), F32), jax.ShapeDtypeStruct((1, D), F32)],
        compiler_params=_cparams(32),
    )(xf, gf, tgt)


def _bwd_norm_matmul(name, dy, wg, x, g, dres, dep):
    bn = wg.shape[2]
    ncol = dy.shape[1]

    def body(dy_ref, w_ref, x_ref, g_ref, dres_ref, dep_ref, dx_ref, dxb_ref, dg_ref):
        i = pl.program_id(0)

        @pl.when(i == 0)
        def _():
            dg_ref[...] = jnp.zeros(dg_ref.shape, F32)

        dh = jnp.zeros((TM, D), F32)
        for j in range(N_DEV):
            dh = dh + _dot_nt(dy_ref[:, j * bn:(j + 1) * bn], w_ref[j])
        xv = x_ref[...]
        dxn, dg = _rms_bwd(dh, xv, _rms(xv), g_ref[...])
        dx = dres_ref[...] + dxn
        dx_ref[...] = dx
        dxb_ref[...] = dx.astype(BF16)
        dg_ref[...] += dg

    return pl.pallas_call(
        body, name=name, grid=(T // TM,),
        in_specs=[_rows(TM, ncol), _full(wg.shape), _rows(TM, D), _full((1, D)), _rows(TM, D), ANY],
        out_specs=[_rows(TM, D), _rows(TM, D), _full((1, D))],
        out_shape=[jax.ShapeDtypeStruct((T, D), F32), jax.ShapeDtypeStruct((T, D), BF16),
                   jax.ShapeDtypeStruct((1, D), F32)],
        compiler_params=_cparams(48),
    )(dy, wg, x, g, dres, dep)


def _bwd_mlp(dxb, dres, w2, r, wg1, x1, g2, dep):
    bn = wg1.shape[2]

    def body(d_ref, dres_ref, w2_ref, r_ref, w1_ref, x_ref, g_ref, dep_ref, df1_ref, dx_ref, dxb_ref, dg_ref):
        i = pl.program_id(0)

        @pl.when(i == 0)
        def _():
            dg_ref[...] = jnp.zeros(dg_ref.shape, F32)

        d = d_ref[...]
        dh = jnp.zeros((TM, D), F32)
        for j in range(N_DEV):
            cols = slice(j * bn, (j + 1) * bn)
            df1 = (2.0 * r_ref[:, cols].astype(F32) * _dot_nt(d, w2_ref[cols, :])).astype(BF16)
            df1_ref[:, cols] = df1
            dh = dh + _dot_nt(df1, w1_ref[j])
        xv = x_ref[...]
        dxn, dg = _rms_bwd(dh, xv, _rms(xv), g_ref[...])
        dx = dres_ref[...] + dxn
        dx_ref[...] = dx
        dxb_ref[...] = dx.astype(BF16)
        dg_ref[...] += dg

    once = dict(pipeline_mode=pl.Buffered(1))
    return pl.pallas_call(
        body, name="bwd_mlp", grid=(T // TM,),
        in_specs=[_rows(TM, D), _rows(TM, D), pl.BlockSpec((D_FF, D), lambda i: (0, 0), **once), _rows(TM, D_FF),
                  pl.BlockSpec(wg1.shape, lambda i: (0, 0, 0), **once), _rows(TM, D), _full((1, D)), ANY],
        out_specs=[_rows(TM, D_FF), _rows(TM, D), _rows(TM, D), _full((1, D))],
        out_shape=[jax.ShapeDtypeStruct((T, D_FF), BF16), jax.ShapeDtypeStruct((T, D), F32),
                   jax.ShapeDtypeStruct((T, D), BF16), jax.ShapeDtypeStruct((1, D), F32)],
        compiler_params=_cparams(56),
    )(dxb, dres, w2, r, wg1, x1, g2, dep)


def _mixer_bwd_a(dxb, wout, proj, c, lg, lb, wm, bst, clg, clb, dep):
    tm = TM_SGU
    n_tiles = T // tm

    def body(dx_ref, wo_ref, p_ref, c_ref, lg_ref, lb_ref, wm_ref, bs_ref, clg_ref, clb_ref, dep_ref,
             dpa_ref, dc_ref, dlg_ref, dlb_ref, dwm_ref, dbs_ref, dcb_ref, dclg_ref, dclb_ref,
             dv_buf, db_acc):
        i = pl.program_id(0)

        @pl.when(i == 0)
        def _():
            for ref in (dlg_ref, dlb_ref, dwm_ref, dbs_ref, dcb_ref, dclg_ref, dclb_ref, db_acc):
                ref[...] = jnp.zeros(ref.shape, F32)

        dmix = _dot_nt(dx_ref[...], wo_ref[...])
        ua = p_ref[:, 0:D_A]
        va = p_ref[:, D_A:2 * D_A]
        u, cdf_u = _gelu(ua)
        vg, cdf_v = _gelu(va)
        xhat, rstd = _ln(vg)
        v = (xhat * lg_ref[...] + lb_ref[...]).astype(BF16)
        mask = _tril_mask()
        for h in range(H_A):
            hc = slice(h * CHUNK, (h + 1) * CHUNK)
            wmh = jnp.where(mask, wm_ref[h], 0.0).astype(BF16)
            for cidx in range(tm // CHUNK):
                rc = slice(cidx * CHUNK, (cidx + 1) * CHUNK)
                vb = v[rc, hc]
                mixed = _dot(wmh, vb) + bs_ref[:, h:h + 1]
                da = dmix[rc, hc]
                dpa_ref[rc, hc] = (da * mixed * _gelu_grad(ua[rc, hc], cdf_u[rc, hc])).astype(BF16)
                dmixed = da * u[rc, hc]
                dmb = dmixed.astype(BF16)
                dv_buf[rc, hc] = _dot_tn(wmh, dmb)
                dwm_ref[h] += _dot_nt(dmb, vb)
                db_acc[:, hc] += dmixed
        dv = dv_buf[...]
        dlb_ref[...] += jnp.sum(dv, axis=0, keepdims=True)
        dlg_ref[...] += jnp.sum(dv * xhat, axis=0, keepdims=True)
        dvg = _ln_bwd(dv * lg_ref[...], xhat, rstd)
        dpa_ref[:, D_A:2 * D_A] = (dvg * _gelu_grad(va, cdf_v)).astype(BF16)

        for q in range(H_B):
            cols = slice(q * LANES, (q + 1) * LANES)
            chat, crstd = _ln(c_ref[:, cols])
            z = chat * clg_ref[:, cols] + clb_ref[:, cols]
            sg = _sigmoid(z)
            dz = dmix[:, D_A + q * LANES:D_A + (q + 1) * LANES] * (sg * (1.0 + z * (1.0 - sg)))
            dclb_ref[:, cols] += jnp.sum(dz, axis=0, keepdims=True)
            dclg_ref[:, cols] += jnp.sum(dz * chat, axis=0, keepdims=True)
            dc = _ln_bwd(dz * clg_ref[:, cols], chat, crstd)
            dc_ref[:, cols] = dc
            dcb_ref[:, cols] += jnp.sum(dc, axis=0, keepdims=True)

        @pl.when(i == n_tiles - 1)
        def _():
            for h in range(H_A):
                hc = slice(h * CHUNK, (h + 1) * CHUNK)
                dwm_ref[h] = jnp.where(mask, dwm_ref[h], 0.0)
                dbs_ref[h:h + 1, :] = jnp.sum(db_acc[:, hc].T, axis=0, keepdims=True)

    vec = _full((1, D_A))
    vshape = jax.ShapeDtypeStruct((1, D_A), F32)
    return pl.pallas_call(
        body, name="mixer_bwd_a", grid=(n_tiles,),
        in_specs=[_rows(tm, D), _full((D, D)), _rows(tm, IN_COLS), _rows(tm, D_B), vec, vec,
                  _full((H_A, CHUNK, CHUNK)), _full((CHUNK, H_A)), vec, vec, ANY],
        out_specs=[_rows(tm, 2 * D_A), _rows(tm, D_B), vec, vec, _full((H_A, CHUNK, CHUNK)),
                   _full((H_A, CHUNK)), vec, vec, vec],
        out_shape=[jax.ShapeDtypeStruct((T, 2 * D_A), BF16), jax.ShapeDtypeStruct((T, D_B), F32), vshape, vshape,
                   jax.ShapeDtypeStruct((H_A, CHUNK, CHUNK), F32), jax.ShapeDtypeStruct((H_A, CHUNK), F32),
                   vshape, vshape, vshape],
        scratch_shapes=[pltpu.VMEM((tm, D_A), F32), pltpu.VMEM((CHUNK, D_A), F32)],
        compiler_params=_cparams(32),
    )(dxb, wout, proj, c, lg, lb, wm, bst, clg, clb, dep)


def _mixer_bwd_b(dc, g, proj, cw, dpa):
    tm = TM_MIX
    n_tiles = T // tm
    hb = tm // HALO

    def body(dc_ref, dch_ref, g_ref, gh_ref, p_ref, cw_ref, dpa_ref, dp_ref, dcw_ref, dcbuf, gbuf, dwacc):
        i = pl.program_id(0)

        @pl.when(i == 0)
        def _():
            dwacc[...] = jnp.zeros(dwacc.shape, F32)

        _fill_shift_buffer(dcbuf, 0, dc_ref[...])
        _fill_shift_buffer(dcbuf, tm, jnp.where(i < n_tiles - 1, dch_ref[...], 0.0))
        _fill_shift_buffer(gbuf, 0, jnp.where(i > 0, gh_ref[...], 0.0))
        _fill_shift_buffer(gbuf, HALO, g_ref[...])
        _build_shifts(dcbuf)
        _build_shifts(gbuf)
        dp_ref[:, 0:2 * D_A] = dpa_ref[...]
        for q in range(H_B):
            cols = slice(q * LANES, (q + 1) * LANES)

            def row_block(rb, carry, q=q, cols=cols):
                base = pl.multiple_of(rb * RB, RB)
                rows = pl.ds(base, RB)
                dg = _conv_taps(dcbuf, cw_ref, q, base, CONV_W - 1, -1)
                val = p_ref[rows, cols]
                sg = _sigmoid(p_ref[rows, D_B + q * LANES:D_B + (q + 1) * LANES])
                dp_ref[rows, 2 * D_A + q * LANES:2 * D_A + (q + 1) * LANES] = (dg * sg).astype(BF16)
                dp_ref[rows, 2 * D_A + D_B + q * LANES:2 * D_A + D_B + (q + 1) * LANES] = (
                    dg * val * sg * (1.0 - sg)).astype(BF16)
                dcv = dcbuf[0, q, rows, :]
                parts = []
                for k in range(CONV_W):
                    prod = dcv * _shifted(gbuf, q, base, HALO - (CONV_W - 1) + k)
                    parts.append(jnp.sum(prod.reshape(RB // 8, 8, LANES), axis=0))
                for k in range(CONV_W):
                    dwacc[k * 8:(k + 1) * 8, cols] += parts[k]
                return carry

            lax.fori_loop(0, tm // RB, row_block, 0)

        @pl.when(i == n_tiles - 1)
        def _():
            for k in range(CONV_W):
                dcw_ref[k:k + 1, :] = jnp.sum(dwacc[k * 8:(k + 1) * 8, :], axis=0, keepdims=True)

    return pl.pallas_call(
        body, name="mixer_bwd_b", grid=(n_tiles,),
        in_specs=[_rows(tm, D_B),
                  pl.BlockSpec((HALO, D_B), lambda i: (jnp.minimum((i + 1) * hb, T // HALO - 1), 0)),
                  _rows(tm, D_B),
                  pl.BlockSpec((HALO, D_B), lambda i: (jnp.maximum(i * hb - 1, 0), 0)),
                  pl.BlockSpec((tm, 2 * D_B), lambda i: (i, 1)),
                  _full((CONV_W, D_B)), _rows(tm, 2 * D_A)],
        out_specs=[_rows(tm, IN_COLS), _full((CONV_W, D_B))],
        out_shape=[jax.ShapeDtypeStruct((T, IN_COLS), BF16), jax.ShapeDtypeStruct((CONV_W, D_B), F32)],
        scratch_shapes=[pltpu.VMEM((8, H_B, tm + HALO, LANES), F32), pltpu.VMEM((8, H_B, HALO + tm, LANES), F32),
                        pltpu.VMEM((CONV_W * 8, D_B), F32)],
        compiler_params=_cparams(32),
    )(dc, dc, g, g, proj, cw, dpa)


def _wgrad(name, a, g, bn, square_a=False):
    k = a.shape[1]
    n = g.shape[1]
    tk = min(k, 1024)
    tn = min(n, max(bn, 1024))
    nsub = tn // bn
    tt = TT_WGRAD
    nt = T // tt

    def body(a_ref, g_ref, o_ref, ob_ref):
        t = pl.program_id(2)

        @pl.when(t == 0)
        def _():
            o_ref[...] = jnp.zeros(o_ref.shape, F32)

        av = a_ref[...]
        if square_a:
            av = av * av
        for s in range(nsub):
            o_ref[s] += _dot_tn(av, g_ref[:, s * bn:(s + 1) * bn])

        @pl.when(t == nt - 1)
        def _():
            ob_ref[...] = o_ref[...].astype(BF16)

    ospec = pl.BlockSpec((nsub, tk, bn), lambda ki, ni, ti: (ni, ki, 0))
    return pl.pallas_call(
        body, name=name, grid=(k // tk, n // tn, nt),
        in_specs=[pl.BlockSpec((tt, tk), lambda ki, ni, ti: (ti, ki)),
                  pl.BlockSpec((tt, tn), lambda ki, ni, ti: (ti, ni))],
        out_specs=[ospec, ospec],
        out_shape=[jax.ShapeDtypeStruct((n // bn, k, bn), F32), jax.ShapeDtypeStruct((n // bn, k, bn), BF16)],
        compiler_params=_cparams(40, ("arbitrary", "arbitrary", "arbitrary")),
    )(a, g)


def _coords():
    return lax.axis_index("x"), lax.axis_index("y"), lax.axis_index("c")


def _flat(x, y, c):
    return 4 * x + 2 * y + c


def _peer(k):
    x, y, c = _coords()
    return (x ^ ((k >> 2) & 1), y ^ ((k >> 1) & 1), c ^ (k & 1))


def _hbm(a):
    return pltpu.with_memory_space_constraint(a, pltpu.HBM)


def _hbm_like(a):
    return pltpu.HBM(a.shape, a.dtype)


def _peer_sems():
    return pltpu.SemaphoreType.DMA((N_DEV - 1,))


def _place_own(shards):
    n = len(shards)
    shapes = [(s.shape if l is None else s.shape[1:]) for s, l, _ in shards]

    def body(*refs):
        ins, outs = refs[:n], refs[n:2 * n]
        stage_in, stage_out = refs[2 * n:3 * n], refs[3 * n:4 * n]
        in_sems, out_sems = refs[4 * n], refs[4 * n + 1]
        me = _flat(*_coords())
        srcs = [ins[a] if shards[a][1] is None else ins[a].at[shards[a][1]] for a in range(n)]
        loads = [pltpu.make_async_copy(srcs[a], stage_in[a], in_sems.at[a]) for a in range(n)]
        stores = [pltpu.make_async_copy(stage_out[a], outs[a].at[me], out_sems.at[a]) for a in range(n)]
        for cp in loads:
            cp.start()
        for a in range(n):
            loads[a].wait()
            stage_out[a][...] = stage_in[a][...].astype(stage_out[a].dtype)
            stores[a].start()
        for cp in stores:
            cp.wait()

    return pl.pallas_call(
        body, name="place_own", in_specs=[ANY] * n, out_specs=[ANY] * n,
        out_shape=[jax.ShapeDtypeStruct((N_DEV,) + shapes[a], shards[a][2]) for a in range(n)],
        scratch_shapes=[*[pltpu.VMEM(shapes[a], shards[a][0].dtype) for a in range(n)],
                        *[pltpu.VMEM(shapes[a], shards[a][2]) for a in range(n)],
                        pltpu.SemaphoreType.DMA((n,)), pltpu.SemaphoreType.DMA((n,))],
        compiler_params=pltpu.CompilerParams(vmem_limit_bytes=40 * MIB),
    )(*[s for s, _, _ in shards])


SIBLING = 1
CHIP_PEERS = (2, 4, 6)
FIRST_PEERS = (SIBLING,) + CHIP_PEERS


def _gather_start(lands):
    n = len(lands)

    def body(*refs):
        lnd, send, recv = refs[:n], refs[n:2 * n], refs[2 * n:3 * n]
        me = _flat(*_coords())
        for a in range(n):
            for j, k in enumerate(FIRST_PEERS):
                pltpu.make_async_remote_copy(
                    src_ref=lnd[a].at[me], dst_ref=lnd[a].at[me], send_sem=send[a].at[j],
                    recv_sem=recv[a].at[j], device_id=_peer(k), device_id_type=MESH_ID).start()

    sems = pltpu.SemaphoreType.DMA((len(FIRST_PEERS),))
    outs = pl.pallas_call(
        body, name="gather_start",
        out_shape=(*[sems] * (2 * n), *[_hbm_like(l) for l in lands]),
        in_specs=[HBM] * n, out_specs=(*[SEM] * (2 * n), *[HBM] * n),
        input_output_aliases={i: 2 * n + i for i in range(n)},
        compiler_params=pltpu.CompilerParams(has_side_effects=EFFECT),
    )(*[_hbm(l) for l in lands])
    return outs[:n], outs[n:2 * n], outs[2 * n:]


def _gather_pass_on(name, land, send, recv, after):
    def body(l_ref, send_ref, recv_ref, after_ref, send2, recv2, l_out):
        sibling = _peer(SIBLING)
        for j, k in enumerate(CHIP_PEERS):
            cp = pltpu.make_async_remote_copy(
                src_ref=l_ref.at[0], dst_ref=l_ref.at[0], send_sem=send_ref.at[1 + j], recv_sem=recv_ref.at[1 + j],
                device_id=_peer(k), device_id_type=MESH_ID)
            cp.wait_send()
            cp.wait_recv()
            blk = _flat(*_peer(k))
            pltpu.make_async_remote_copy(
                src_ref=l_ref.at[blk], dst_ref=l_ref.at[blk], send_sem=send2.at[j], recv_sem=recv2.at[j],
                device_id=sibling, device_id_type=MESH_ID).start()

    sems = pltpu.SemaphoreType.DMA((len(CHIP_PEERS),))
    return pl.pallas_call(
        body, name=name, out_shape=(sems, sems, _hbm_like(land)),
        in_specs=(HBM, SEM, SEM, ANY), out_specs=(SEM, SEM, HBM), input_output_aliases={0: 2},
        compiler_params=pltpu.CompilerParams(has_side_effects=EFFECT),
    )(land, send, recv, after)


def _gather_wait(name, land, send, recv, send2, recv2, after):
    def body(l_ref, send_ref, recv_ref, send2_ref, recv2_ref, after_ref, l_out):
        sibling = _peer(SIBLING)
        own = pltpu.make_async_remote_copy(
            src_ref=l_ref.at[0], dst_ref=l_ref.at[0], send_sem=send_ref.at[0], recv_sem=recv_ref.at[0],
            device_id=sibling, device_id_type=MESH_ID)
        own.wait_send()
        own.wait_recv()
        for j in range(len(CHIP_PEERS)):
            cp = pltpu.make_async_remote_copy(
                src_ref=l_ref.at[0], dst_ref=l_ref.at[0], send_sem=send2_ref.at[j], recv_sem=recv2_ref.at[j],
                device_id=sibling, device_id_type=MESH_ID)
            cp.wait_send()
            cp.wait_recv()

    return pl.pallas_call(
        body, name=name, out_shape=_hbm_like(land),
        in_specs=(HBM, SEM, SEM, SEM, SEM, ANY), out_specs=HBM, input_output_aliases={0: 0},
        compiler_params=pltpu.CompilerParams(has_side_effects=EFFECT),
    )(land, send, recv, send2, recv2, after)


def _scatter_start(name, grads, lands, layer):
    n = len(grads)

    def body(*refs):
        g, lnd, send, recv = refs[:n], refs[n:2 * n], refs[2 * n:3 * n], refs[3 * n:4 * n]
        for a in range(n):
            for k in range(1, N_DEV):
                to = _peer(k)
                pltpu.make_async_remote_copy(
                    src_ref=g[a].at[_flat(*to)], dst_ref=lnd[a].at[layer, k - 1], send_sem=send[a].at[k - 1],
                    recv_sem=recv[a].at[k - 1], device_id=to, device_id_type=MESH_ID).start()

    outs = pl.pallas_call(
        body, name=name,
        out_shape=(*[_peer_sems()] * (2 * n), *[_hbm_like(g) for g in grads], *[_hbm_like(l) for l in lands]),
        in_specs=[HBM] * (2 * n), out_specs=(*[SEM] * (2 * n), *[HBM] * (2 * n)),
        input_output_aliases={i: 2 * n + i for i in range(2 * n)},
        compiler_params=pltpu.CompilerParams(has_side_effects=EFFECT),
    )(*[_hbm(g) for g in grads], *[_hbm(l) for l in lands])
    return outs[:n], outs[n:2 * n], outs[2 * n:3 * n], outs[3 * n:]


def _scatter_wait(grads, lands, sends, recvs, after):
    n, nw = len(grads), len(lands)

    def body(*refs):
        g, lnd = refs[:n], refs[n:n + nw]
        send, recv = refs[n + nw:2 * n + nw], refs[2 * n + nw:3 * n + nw]
        for a in range(n):
            for k in range(1, N_DEV):
                cp = pltpu.make_async_remote_copy(
                    src_ref=g[a].at[0], dst_ref=lnd[a // DEPTH].at[a % DEPTH, 0],
                    send_sem=send[a].at[k - 1], recv_sem=recv[a].at[k - 1],
                    device_id=_peer(k), device_id_type=MESH_ID)
                cp.wait_send()
                cp.wait_recv()

    outs = pl.pallas_call(
        body, name="scatter_wait",
        out_shape=(*[_hbm_like(g) for g in grads], *[_hbm_like(l) for l in lands]),
        in_specs=(*[HBM] * (n + nw), *[SEM] * (2 * n), ANY), out_specs=[HBM] * (n + nw),
        input_output_aliases={i: i for i in range(n + nw)},
        compiler_params=pltpu.CompilerParams(has_side_effects=EFFECT),
    )(*grads, *lands, *sends, *recvs, after)
    return outs[n:]


def _own_blocks(grads_f32):
    nw = len(grads_f32)
    flat32 = [g for w in grads_f32 for g in w]
    n = len(flat32)

    def body(*refs):
        g32, own, stage = refs[:n], refs[n:n + nw], refs[n + nw:2 * n + nw]
        in_sems, out_sems = refs[2 * n + nw], refs[2 * n + nw + 1]
        me = _flat(*_coords())
        loads = [pltpu.make_async_copy(g32[a].at[me], stage[a], in_sems.at[a]) for a in range(n)]
        stores = [pltpu.make_async_copy(stage[a], own[a // DEPTH].at[a % DEPTH], out_sems.at[a]) for a in range(n)]
        for cp in loads:
            cp.start()
        for a in range(n):
            loads[a].wait()
            stores[a].start()
        for cp in stores:
            cp.wait()

    return pl.pallas_call(
        body, name="own_blocks", in_specs=[ANY] * n, out_specs=[ANY] * nw,
        out_shape=[jax.ShapeDtypeStruct((DEPTH,) + w[0].shape[1:], F32) for w in grads_f32],
        scratch_shapes=[*[pltpu.VMEM(g.shape[1:], F32) for g in flat32],
                        pltpu.SemaphoreType.DMA((n,)), pltpu.SemaphoreType.DMA((n,))],
        compiler_params=pltpu.CompilerParams(vmem_limit_bytes=24 * MIB),
    )(*flat32)


def _all_reduce_small(part):
    rows = part.shape[0]
    br = rows // N_DEV
    assert br * N_DEV == rows and br % 8 == 0

    def body(p_ref, o_ref, slots, send1, recv1, send2, recv2):
        me = _flat(*_coords())

        def block(ref, d):
            return ref.at[pl.ds(pl.multiple_of(d * br, 8), br), :]

        slots[me] = p_ref[pl.ds(pl.multiple_of(me * br, 8), br), :]
        scatter, gather = [], []
        for k in range(1, N_DEV):
            to = _peer(k)
            scatter.append(pltpu.make_async_remote_copy(
                src_ref=block(p_ref, _flat(*to)), dst_ref=slots.at[me],
                send_sem=send1.at[k - 1], recv_sem=recv1.at[k - 1], device_id=to, device_id_type=MESH_ID))
            gather.append(pltpu.make_async_remote_copy(
                src_ref=block(o_ref, me), dst_ref=block(o_ref, me),
                send_sem=send2.at[k - 1], recv_sem=recv2.at[k - 1], device_id=to, device_id_type=MESH_ID))
        for cp in scatter:
            cp.start()
        for cp in scatter:
            cp.wait()
        acc = slots[0]
        for d in range(1, N_DEV):
            acc = acc + slots[d]
        o_ref[pl.ds(pl.multiple_of(me * br, 8), br), :] = acc
        for cp in gather:
            cp.start()
        for cp in gather:
            cp.wait()

    return pl.pallas_call(
        body, name="all_reduce_small",
        in_specs=[pl.BlockSpec(memory_space=pltpu.VMEM)], out_specs=pl.BlockSpec(memory_space=pltpu.VMEM),
        out_shape=jax.ShapeDtypeStruct(part.shape, F32),
        scratch_shapes=[pltpu.VMEM((N_DEV, br, LANES), F32)] + [pltpu.SemaphoreType.DMA((N_DEV - 1,))] * 4,
        compiler_params=pltpu.CompilerParams(vmem_limit_bytes=24 * MIB),
    )(part)


def _adam_math(w, g, m, v):
    m = ADAM_B1 * m + (1.0 - ADAM_B1) * g
    v = ADAM_B2 * v + (1.0 - ADAM_B2) * (g * g)
    m_hat = m / (1.0 - ADAM_B1 ** ADAM_STEP)
    v_hat = v / (1.0 - ADAM_B2 ** ADAM_STEP)
    delta = -ADAM_LR * (m_hat / (jnp.sqrt(v_hat) + ADAM_EPS) + ADAM_WD * w)
    return delta, m, v


def _adam_sharded(name, own, recv, w, m, v):
    _, r, c = w.shape
    tr = min(r, 256)

    def body(own_ref, recv_ref, w_ref, m_ref, v_ref, g_ref, d_ref, nm_ref, nv_ref):
        g = own_ref[...]
        for k in range(N_DEV - 1):
            g = g + recv_ref[k].astype(F32)
        delta, nm, nv = _adam_math(w_ref[...], g, m_ref[...], v_ref[...])
        g_ref[...] = g
        d_ref[...] = delta
        nm_ref[...] = nm
        nv_ref[...] = nv

    blk = pl.BlockSpec((None, tr, c), lambda l, i: (l, i, 0))
    shp = jax.ShapeDtypeStruct(w.shape, F32)
    return pl.pallas_call(
        body, name=name, grid=(DEPTH, r // tr),
        in_specs=[blk, pl.BlockSpec((None, N_DEV - 1, tr, c), lambda l, i: (l, 0, i, 0)), blk, blk, blk],
        out_specs=[blk] * 4, out_shape=[shp] * 4,
        compiler_params=_cparams(32, ("arbitrary", "arbitrary")),
    )(own, recv, w, m, v)


def _adam_small(g, w, m, v):
    rows = g.shape[0]

    def body(g_ref, w_ref, m_ref, v_ref, d_ref, nm_ref, nv_ref):
        delta, nm, nv = _adam_math(w_ref[...], g_ref[...], m_ref[...], v_ref[...])
        d_ref[...] = delta
        nm_ref[...] = nm
        nv_ref[...] = nv

    spec = _full((rows, LANES))
    shp = jax.ShapeDtypeStruct((rows, LANES), F32)
    return pl.pallas_call(
        body, name="adam_small", grid=(1,),
        in_specs=[spec] * 4, out_specs=[spec] * 3, out_shape=[shp] * 3,
        compiler_params=_cparams(24),
    )(g, w, m, v)


SMALL = ["norm1_g", "sgu_ln_g", "sgu_ln_b", "sgu_w", "sgu_b", "conv_b", "conv_ln_g", "conv_ln_b", "norm2_g",
         "final_g"]


def _pack(arrays):
    flat = jnp.concatenate([a.reshape(-1) for a in arrays])
    pad = (-flat.shape[0]) % (N_DEV * 8 * LANES)
    return jnp.pad(flat, (0, pad)).reshape(-1, LANES)


def _unpack(packed, shapes):
    flat = packed.reshape(-1)
    out, off = [], 0
    for s in shapes:
        size = 1
        for d in s:
            size *= d
        out.append(flat[off:off + size].reshape(s))
        off += size
    return out


def kernel(x, norm1_g, w_in, sgu_ln_g, sgu_ln_b, sgu_w, sgu_b, conv_w, conv_b, conv_ln_g, conv_ln_b, w_out, norm2_g, w_ff1, w_ff2, final_g, loss_target, m_norm1_g, m_w_in, m_sgu_ln_g, m_sgu_ln_b, m_sgu_w, m_sgu_b, m_conv_w, m_conv_b, m_conv_ln_g, m_conv_ln_b, m_w_out, m_norm2_g, m_w_ff1, m_w_ff2, m_final_g, v_norm1_g, v_w_in, v_sgu_ln_g, v_sgu_ln_b, v_sgu_w, v_sgu_b, v_conv_w, v_conv_b, v_conv_ln_g, v_conv_ln_b, v_w_out, v_norm2_g, v_w_ff1, v_w_ff2, v_final_g):
    x2d = x.reshape(T, D)
    tgt = loss_target.reshape(T, D)
    cw_shard = conv_w.reshape(CONV_W, LANES)

    gnames, shards = [], []
    for l in range(DEPTH):
        for k, w in (("w_in", w_in), ("w_out", w_out), ("w_ff1", w_ff1), ("w_ff2", w_ff2)):
            gnames.append(f"{k}{l}")
            shards.append((w, l, BF16))
        if l == 0:
            gnames.insert(1, "conv_w")
            shards.insert(1, (cw_shard, None, F32))
    sends, recvs, lands = _gather_start(_place_own(shards))
    gidx = {k: i for i, k in enumerate(gnames)}

    passed = {}

    def pass_on(k, after):
        i = gidx[k]
        passed[k] = _gather_pass_on("gather_pass_on_" + k, lands[i], sends[i], recvs[i], after)

    def gathered(k, after):
        i = gidx[k]
        send2, recv2, land = passed[k]
        return _gather_wait("gather_wait_" + k, land, sends[i], recvs[i], send2, recv2, after)

    saved = []
    xl = x2d
    cw_full = None
    for l in range(DEPTH):
        g1 = norm1_g[l].reshape(1, D)
        g2 = norm2_g[l].reshape(1, D)
        lg, lb = sgu_ln_g[l].reshape(1, D_A), sgu_ln_b[l].reshape(1, D_A)
        bst = sgu_b[l].T
        cb = conv_b[l].reshape(1, D_B)
        clg, clb = conv_ln_g[l].reshape(1, D_B), conv_ln_b[l].reshape(1, D_B)
        if l == 0:
            pass_on("w_in0", xl)
        wg_in = gathered(f"w_in{l}", xl)
        if l == 0:
            pass_on("conv_w", xl)
        h, proj = _fwd_in(xl, g1, wg_in)
        if l == 0:
            cw_full = gathered("conv_w", proj).reshape(N_DEV, DEPTH, CONV_W, D_B // N_DEV).transpose(
                1, 2, 0, 3).reshape(DEPTH, CONV_W, D_B)
        pass_on(f"w_out{l}", proj)
        mix, c, g = _mixer_fwd(proj, lg, lb, sgu_w[l], bst, cw_full[l], cb, clg, clb)
        pass_on(f"w_ff1{l}", mix)
        wout = gathered(f"w_out{l}", mix).reshape(D, D)
        x1, h2 = _fwd_out(mix, wout, xl, g2)
        pass_on(f"w_ff2{l}", h2)
        wg_ff1 = gathered(f"w_ff1{l}", h2)
        r = _fwd_ff1(h2, wg_ff1)
        if l + 1 < DEPTH:
            pass_on(f"w_in{l + 1}", r)
        w2 = gathered(f"w_ff2{l}", r).reshape(D_FF, D)
        x2 = _fwd_ff2(r, w2, x1)
        saved.append(dict(x=xl, h=h, proj=proj, mix=mix, c=c, g=g, x1=x1, h2=h2, r=r, wg_in=wg_in, wout=wout,
                          wg_ff1=wg_ff1, w2=w2, g1=g1, g2=g2, lg=lg, lb=lb, bst=bst, clg=clg, clb=clb))
        xl = x2

    dx, dxb, loss_part, d_final_g = _loss_bwd(xl, final_g.reshape(1, D), tgt)

    names = ["w_in", "w_out", "w_ff1", "w_ff2"]
    block = dict(w_in=(D, IN_COLS // N_DEV), w_out=(D // N_DEV, D), w_ff1=(D, D_FF // N_DEV), w_ff2=(D_FF // N_DEV, D))
    land = {k: lax.empty((DEPTH, N_DEV - 1) + block[k], BF16) for k in names}
    big32 = {k: [None] * DEPTH for k in names}
    big16 = {k: [None] * DEPTH for k in names}
    ssend = {k: [None] * DEPTH for k in names}
    srecv = {k: [None] * DEPTH for k in names}

    def send_grads(l, grads):
        ks = list(grads)
        for k in ks:
            big32[k][l] = grads[k][0].reshape((N_DEV,) + block[k])
        sends, recvs, g16s, lands = _scatter_start(
            "scatter_start_" + "_".join(ks) + str(l), [grads[k][1].reshape((N_DEV,) + block[k]) for k in ks],
            [land[k] for k in ks], l)
        for i, k in enumerate(ks):
            ssend[k][l], srecv[k][l], big16[k][l], land[k] = sends[i], recvs[i], g16s[i], lands[i]

    small = {}
    d_conv_w = [None] * DEPTH
    for l in reversed(range(DEPTH)):
        s = saved[l]
        send_grads(l, dict(w_ff2=_wgrad("wgrad_ff2", s["r"], dxb, D, square_a=True)))
        df1, dx1, dx1b, dg2 = _bwd_mlp(dxb, dx, s["w2"], s["r"], s["wg_ff1"], s["x1"], s["g2"], big16["w_ff2"][l])
        send_grads(l, dict(w_ff1=_wgrad("wgrad_ff1", s["h2"], df1, D_FF // N_DEV),
                           w_out=_wgrad("wgrad_out", s["mix"], dx1b, D)))
        dpa, dc, dlg, dlb, dwm, dbs, dcb, dclg, dclb = _mixer_bwd_a(
            dx1b, s["wout"], s["proj"], s["c"], s["lg"], s["lb"], sgu_w[l], s["bst"], s["clg"], s["clb"],
            big16["w_out"][l])
        dproj, d_conv_w[l] = _mixer_bwd_b(dc, s["g"], s["proj"], cw_full[l], dpa)
        send_grads(l, dict(w_in=_wgrad("wgrad_in", s["h"], dproj, IN_COLS // N_DEV)))
        dx, dxb, dg1 = _bwd_norm_matmul("bwd_in", dproj, s["wg_in"], s["x"], s["g1"], dx1, big16["w_in"][l])
        small[l] = dict(norm1_g=dg1, sgu_ln_g=dlg, sgu_ln_b=dlb, sgu_w=dwm, sgu_b=dbs, conv_b=dcb, conv_ln_g=dclg,
                        conv_ln_b=dclb, norm2_g=dg2)
    grad_x = dx.reshape(1, T, D)

    own = _own_blocks([big32[k] for k in names])
    recv = _scatter_wait([g for k in names for g in big16[k]], [land[k] for k in names],
                         [q for k in names for q in ssend[k]], [q for k in names for q in srecv[k]], dx)
    wmv = dict(w_in=(w_in, m_w_in, v_w_in), w_out=(w_out, m_w_out, v_w_out), w_ff1=(w_ff1, m_w_ff1, v_w_ff1),
               w_ff2=(w_ff2, m_w_ff2, v_w_ff2))
    res = {}
    for idx, k in enumerate(names):
        res[k] = _adam_sharded("adam_" + k, own[idx], recv[idx], *wmv[k])

    rep = dict(norm1_g=(norm1_g, m_norm1_g, v_norm1_g), sgu_ln_g=(sgu_ln_g, m_sgu_ln_g, v_sgu_ln_g),
               sgu_ln_b=(sgu_ln_b, m_sgu_ln_b, v_sgu_ln_b), sgu_w=(sgu_w, m_sgu_w, v_sgu_w),
               sgu_b=(sgu_b, m_sgu_b, v_sgu_b), conv_b=(conv_b, m_conv_b, v_conv_b),
               conv_ln_g=(conv_ln_g, m_conv_ln_g, v_conv_ln_g), conv_ln_b=(conv_ln_b, m_conv_ln_b, v_conv_ln_b),
               norm2_g=(norm2_g, m_norm2_g, v_norm2_g), final_g=(final_g, m_final_g, v_final_g))
    parts = []
    for k in SMALL:
        if k == "final_g":
            parts.append(d_final_g.reshape(rep[k][0].shape))
        else:
            parts.append(jnp.stack([small[l][k].reshape(rep[k][0].shape[1:]) for l in range(DEPTH)]))
    parts.append(jnp.stack(d_conv_w))
    parts.append(loss_part[0, 0:1])
    shapes = [p.shape for p in parts]
    summed = _unpack(_all_reduce_small(_pack(parts)), shapes)
    loss = summed[-1][0]
    me = _flat(*_coords())
    g_conv_w = lax.dynamic_slice_in_dim(summed[-2], me * (D_B // N_DEV), D_B // N_DEV, axis=2)
    g_small = summed[:-2] + [g_conv_w]
    keys = SMALL + ["conv_w"]
    rep["conv_w"] = (conv_w, m_conv_w, v_conv_w)
    shapes = [a.shape for a in g_small]
    packed = [_pack(g_small)] + [_pack([rep[k][i] for k in keys]) for i in range(3)]
    d_s, nm_s, nv_s = (_unpack(a, shapes) for a in _adam_small(*packed))
    for i, k in enumerate(keys):
        res[k] = (g_small[i], d_s[i], nm_s[i], nv_s[i])

    order = ["norm1_g", "w_in", "sgu_ln_g", "sgu_ln_b", "sgu_w", "sgu_b", "conv_w", "conv_b", "conv_ln_g",
             "conv_ln_b", "w_out", "norm2_g", "w_ff1", "w_ff2", "final_g"]
    return (loss, grad_x, *[res[k][0] for k in order], *[res[k][1] for k in order],
            *[res[k][2] for k in order], *[res[k][3] for k in order])
```

```python
import functools

import jax
import jax.numpy as jnp
from jax import lax
from jax.experimental import pallas as pl
from jax.experimental.pallas import tpu as pltpu

F32 = jnp.float32
BF16 = jnp.bfloat16

N_DEV = 8
DEPTH = 2
T = 4096
D = 1024
D_A = 512
D_B = 512
CHUNK = 128
H_A = 4
H_B = 4
CONV_W = 31
HALO = 32
D_FF = 4096
IN_COLS = 2048
EPS = 1e-6

ADAM_LR = 0.001
ADAM_B1 = 0.9
ADAM_B2 = 0.999
ADAM_EPS = 1e-08
ADAM_WD = 0.01
ADAM_STEP = 10

TM = 512
TM_FWD = 1024
TT_WGRAD = 1024
TM_MIX = 256
TM_SGU = 512
RB = 64
CONV_CHAINS = 2
LANES = 128
MIB = 1024 * 1024
SCOPED_VMEM_MIB = 60

SQRT_HALF = 0.7071067811865476
INV_SQRT_2PI = 0.3989422804014327

MESH_ID = pl.DeviceIdType.MESH
ANY = pl.BlockSpec(memory_space=pl.ANY)
HBM = pl.BlockSpec(memory_space=pltpu.HBM)
SEM = pl.BlockSpec(memory_space=pltpu.SEMAPHORE)
EFFECT = pltpu.SideEffectType.DATAFLOW_SIDE_EFFECTING


def _cparams(vmem_mib, sem=("arbitrary",)):
    assert vmem_mib <= SCOPED_VMEM_MIB
    return pltpu.CompilerParams(dimension_semantics=sem, vmem_limit_bytes=SCOPED_VMEM_MIB * MIB)


def _full(shape):
    return pl.BlockSpec(shape, lambda *_: (0,) * len(shape))


def _rows(tm, cols):
    return pl.BlockSpec((tm, cols), lambda i: (i, 0))


def _gelu(x):
    cdf = 0.5 * (1.0 + lax.erf(x * SQRT_HALF))
    return x * cdf, cdf


def _gelu_grad(x, cdf):
    return cdf + x * (INV_SQRT_2PI * jnp.exp(-0.5 * x * x))


def _sigmoid(x):
    return 1.0 / (1.0 + jnp.exp(-x))


def _ln(x):
    mu = jnp.mean(x, axis=-1, keepdims=True)
    xc = x - mu
    rstd = lax.rsqrt(jnp.mean(xc * xc, axis=-1, keepdims=True) + EPS)
    return xc * rstd, rstd


def _ln_bwd(dyh, xhat, rstd):
    return rstd * (dyh - jnp.mean(dyh, axis=-1, keepdims=True) - xhat * jnp.mean(dyh * xhat, axis=-1, keepdims=True))


def _rms(x):
    return lax.rsqrt(jnp.mean(x * x, axis=-1, keepdims=True) + EPS)


def _rms_bwd(dh, x, r, g):
    n = x * r
    dn = dh * g
    dx = r * (dn - n * jnp.mean(dn * n, axis=-1, keepdims=True))
    return dx, jnp.sum(dh * n, axis=0, keepdims=True)


def _dot(a, b):
    return jnp.dot(a, b, preferred_element_type=F32)


def _dot_nt(a, b):
    return lax.dot_general(a, b, (((1,), (1,)), ((), ())), preferred_element_type=F32)


def _dot_tn(a, b):
    return lax.dot_general(a, b, (((0,), (0,)), ((), ())), preferred_element_type=F32)


def _tril_mask():
    r = lax.broadcasted_iota(jnp.int32, (CHUNK, CHUNK), 0)
    c = lax.broadcasted_iota(jnp.int32, (CHUNK, CHUNK), 1)
    return r >= c


def _fwd_in(x, g1, wg):
    bn = wg.shape[2]

    def body(x_ref, g_ref, w_ref, h_ref, p_ref):
        xv = x_ref[...]
        h = (xv * _rms(xv) * g_ref[...]).astype(BF16)
        h_ref[...] = h
        for j in range(N_DEV):
            p_ref[:, j * bn:(j + 1) * bn] = _dot(h, w_ref[j])

    return pl.pallas_call(
        body, name="fwd_in", grid=(T // TM_FWD,),
        in_specs=[_rows(TM_FWD, D), _full((1, D)), _full(wg.shape)],
        out_specs=[_rows(TM_FWD, D), _rows(TM_FWD, IN_COLS)],
        out_shape=[jax.ShapeDtypeStruct((T, D), BF16), jax.ShapeDtypeStruct((T, IN_COLS), F32)],
        compiler_params=_cparams(32),
    )(x, g1, wg)


def _fill_shift_buffer(sh_ref, row0, value):
    for q in range(sh_ref.shape[1]):
        sh_ref[0, q, row0:row0 + value.shape[0], :] = value[:, q * LANES:(q + 1) * LANES]


def _build_shifts(sh_ref):
    rows = sh_ref.shape[2]
    for p in range(1, 8):
        for q in range(sh_ref.shape[1]):
            sh_ref[p, q, 0:rows - 8, :] = sh_ref[0, q, p:p + rows - 8, :]


def _shifted(sh_ref, q, base, off):
    start = base + (off - off % 8)
    if not isinstance(start, int):
        start = pl.multiple_of(start, 8)
    return sh_ref[off % 8, q, pl.ds(start, RB), :]


def _conv_taps(sh_ref, w_ref, q, base, first_tap_row, step):
    cols = slice(q * LANES, (q + 1) * LANES)
    acc = [jnp.zeros((RB, LANES), F32) for _ in range(CONV_CHAINS)]
    for k in range(CONV_W):
        term = _shifted(sh_ref, q, base, first_tap_row + step * k) * w_ref[k:k + 1, cols]
        acc[k % CONV_CHAINS] = acc[k % CONV_CHAINS] + term
    return functools.reduce(lambda a, b: a + b, acc)


def _mixer_fwd(proj, lg, lb, wm, bst, cw, cb, clg, clb):
    tm = TM_MIX
    hb = tm // HALO

    def body(p_ref, ph_ref, lg_ref, lb_ref, wm_ref, bs_ref, cw_ref, cb_ref, clg_ref, clb_ref,
             mix_ref, c_ref, g_ref, gbuf):
        i = pl.program_id(0)
        u, _ = _gelu(p_ref[:, 0:D_A])
        vg, _ = _gelu(p_ref[:, D_A:2 * D_A])
        xhat, _ = _ln(vg)
        v = (xhat * lg_ref[...] + lb_ref[...]).astype(BF16)
        mask = _tril_mask()
        for h in range(H_A):
            hc = slice(h * CHUNK, (h + 1) * CHUNK)
            wmh = jnp.where(mask, wm_ref[h], 0.0).astype(BF16)
            for c in range(tm // CHUNK):
                rc = slice(c * CHUNK, (c + 1) * CHUNK)
                mixed = _dot(wmh, v[rc, hc]) + bs_ref[:, h:h + 1]
                mix_ref[rc, hc] = (u[rc, hc] * mixed).astype(BF16)

        g = p_ref[:, 2 * D_A:2 * D_A + D_B] * _sigmoid(p_ref[:, 2 * D_A + D_B:IN_COLS])
        g_ref[...] = g
        gh = ph_ref[:, 0:D_B] * _sigmoid(ph_ref[:, D_B:2 * D_B])
        _fill_shift_buffer(gbuf, 0, jnp.where(i > 0, gh, 0.0))
        _fill_shift_buffer(gbuf, HALO, g)
        _build_shifts(gbuf)
        for q in range(H_B):
            cols = slice(q * LANES, (q + 1) * LANES)
            for rb in range(tm // RB):
                acc = _conv_taps(gbuf, cw_ref, q, rb * RB, HALO - (CONV_W - 1), 1)
                c_ref[rb * RB:(rb + 1) * RB, cols] = acc + cb_ref[:, cols]
        for q in range(H_B):
            cols = slice(q * LANES, (q + 1) * LANES)
            chat, _ = _ln(c_ref[:, cols])
            z = chat * clg_ref[:, cols] + clb_ref[:, cols]
            mix_ref[:, D_A + q * LANES:D_A + (q + 1) * LANES] = (z * _sigmoid(z)).astype(BF16)

    vec = _full((1, D_A))
    return pl.pallas_call(
        body, name="mixer_fwd", grid=(T // tm,),
        in_specs=[_rows(tm, IN_COLS),
                  pl.BlockSpec((HALO, 2 * D_B), lambda i: (jnp.maximum(i * hb - 1, 0), 1)),
                  vec, vec, _full((H_A, CHUNK, CHUNK)), _full((CHUNK, H_A)),
                  _full((CONV_W, D_B)), vec, vec, vec],
        out_specs=[_rows(tm, D), _rows(tm, D_B), _rows(tm, D_B)],
        out_shape=[jax.ShapeDtypeStruct((T, D), BF16), jax.ShapeDtypeStruct((T, D_B), F32),
                   jax.ShapeDtypeStruct((T, D_B), F32)],
        scratch_shapes=[pltpu.VMEM((8, H_B, HALO + tm, LANES), F32)],
        compiler_params=_cparams(32),
    )(proj, proj, lg, lb, wm, bst, cw, cb, clg, clb)


def _fwd_out(mix, wout, x, g2):
    def body(m_ref, w_ref, x_ref, g_ref, x1_ref, h2_ref):
        x1 = x_ref[...] + _dot(m_ref[...], w_ref[...])
        x1_ref[...] = x1
        h2_ref[...] = (x1 * _rms(x1) * g_ref[...]).astype(BF16)

    return pl.pallas_call(
        body, name="fwd_out", grid=(T // TM_FWD,),
        in_specs=[_rows(TM_FWD, D), _full((D, D)), _rows(TM_FWD, D), _full((1, D))],
        out_specs=[_rows(TM_FWD, D), _rows(TM_FWD, D)],
        out_shape=[jax.ShapeDtypeStruct((T, D), F32), jax.ShapeDtypeStruct((T, D), BF16)],
        compiler_params=_cparams(32),
    )(mix, wout, x, g2)


def _fwd_ff1(h2, wg):
    bn = wg.shape[2]

    def body(h_ref, w_ref, r_ref):
        h = h_ref[...]
        for j in range(N_DEV):
            r_ref[:, j * bn:(j + 1) * bn] = jnp.maximum(_dot(h, w_ref[j]), 0.0).astype(BF16)

    return pl.pallas_call(
        body, name="fwd_ff1", grid=(T // TM_FWD,),
        in_specs=[_rows(TM_FWD, D), _full(wg.shape)],
        out_specs=_rows(TM_FWD, D_FF),
        out_shape=jax.ShapeDtypeStruct((T, D_FF), BF16),
        compiler_params=_cparams(48),
    )(h2, wg)


def _fwd_ff2(r, w2, x1):
    def body(r_ref, w_ref, x_ref, o_ref):
        rv = r_ref[...]
        o_ref[...] = x_ref[...] + _dot(rv * rv, w_ref[...])

    return pl.pallas_call(
        body, name="fwd_ff2", grid=(T // TM_FWD,),
        in_specs=[_rows(TM_FWD, D_FF), _full((D_FF, D)), _rows(TM_FWD, D)],
        out_specs=_rows(TM_FWD, D),
        out_shape=jax.ShapeDtypeStruct((T, D), F32),
        compiler_params=_cparams(48),
    )(r, w2, x1)


def _fwd_ff2_loss(r, w2, x1, gf, tgt):
    def body(r_ref, w_ref, x_ref, g_ref, t_ref, dx_ref, dxb_ref, loss_ref, dg_ref):
        i = pl.program_id(0)

        @pl.when(i == 0)
        def _():
            loss_ref[...] = jnp.zeros(loss_ref.shape, F32)
            dg_ref[...] = jnp.zeros(dg_ref.shape, F32)

        rv = r_ref[...]
        xv = x_ref[...] + _dot(rv * rv, w_ref[...])
        rn = _rms(xv)
        diff = xv * rn * g_ref[...] - t_ref[...]
        loss_ref[...] += 0.5 * jnp.sum(jnp.mean(diff * diff, axis=-1, keepdims=True), axis=0, keepdims=True)
        dx, dg = _rms_bwd(diff * (1.0 / D), xv, rn, g_ref[...])
        dx_ref[...] = dx
        dxb_ref[...] = dx.astype(BF16)
        dg_ref[...] += dg

    return pl.pallas_call(
        body, name="fwd_ff2_loss", grid=(T // TM,),
        in_specs=[_rows(TM, D_FF), pl.BlockSpec((D_FF, D), lambda i: (0, 0), pipeline_mode=pl.Buffered(1)),
                  _rows(TM, D), _full((1, D)), _rows(TM, D)],
        out_specs=[_rows(TM, D), _rows(TM, D), _full((8, LANES)), _full((1, D))],
        out_shape=[jax.ShapeDtypeStruct((T, D), F32), jax.ShapeDtypeStruct((T, D), BF16),
                   jax.ShapeDtypeStruct((8, LANES), F32), jax.ShapeDtypeStruct((1, D), F32)],
        compiler_params=_cparams(40),
    )(r, w2, x1, gf, tgt)


def _bwd_norm_matmul(name, dy, wg, x, g, dres, dep):
    bn = wg.shape[2]
    ncol = dy.shape[1]

    def body(dy_ref, w_ref, x_ref, g_ref, dres_ref, dep_ref, dx_ref, dxb_ref, dg_ref):
        i = pl.program_id(0)

        @pl.when(i == 0)
        def _():
            dg_ref[...] = jnp.zeros(dg_ref.shape, F32)

        dh = jnp.zeros((TM, D), F32)
        for j in range(N_DEV):
            dh = dh + _dot_nt(dy_ref[:, j * bn:(j + 1) * bn], w_ref[j])
        xv = x_ref[...]
        dxn, dg = _rms_bwd(dh, xv, _rms(xv), g_ref[...])
        dx = dres_ref[...] + dxn
        dx_ref[...] = dx
        dxb_ref[...] = dx.astype(BF16)
        dg_ref[...] += dg

    return pl.pallas_call(
        body, name=name, grid=(T // TM,),
        in_specs=[_rows(TM, ncol), _full(wg.shape), _rows(TM, D), _full((1, D)), _rows(TM, D), ANY],
        out_specs=[_rows(TM, D), _rows(TM, D), _full((1, D))],
        out_shape=[jax.ShapeDtypeStruct((T, D), F32), jax.ShapeDtypeStruct((T, D), BF16),
                   jax.ShapeDtypeStruct((1, D), F32)],
        compiler_params=_cparams(48),
    )(dy, wg, x, g, dres, dep)


def _bwd_mlp(dxb, dres, w2, r, wg1, x1, g2, dep):
    bn = wg1.shape[2]

    def body(d_ref, dres_ref, w2_ref, r_ref, w1_ref, x_ref, g_ref, dep_ref, df1_ref, dx_ref, dxb_ref, dg_ref):
        i = pl.program_id(0)

        @pl.when(i == 0)
        def _():
            dg_ref[...] = jnp.zeros(dg_ref.shape, F32)

        d = d_ref[...]
        dh = jnp.zeros((TM, D), F32)
        for j in range(N_DEV):
            cols = slice(j * bn, (j + 1) * bn)
            df1 = (2.0 * r_ref[:, cols].astype(F32) * _dot_nt(d, w2_ref[cols, :])).astype(BF16)
            df1_ref[:, cols] = df1
            dh = dh + _dot_nt(df1, w1_ref[j])
        xv = x_ref[...]
        dxn, dg = _rms_bwd(dh, xv, _rms(xv), g_ref[...])
        dx = dres_ref[...] + dxn
        dx_ref[...] = dx
        dxb_ref[...] = dx.astype(BF16)
        dg_ref[...] += dg

    once = dict(pipeline_mode=pl.Buffered(1))
    return pl.pallas_call(
        body, name="bwd_mlp", grid=(T // TM,),
        in_specs=[_rows(TM, D), _rows(TM, D), pl.BlockSpec((D_FF, D), lambda i: (0, 0), **once), _rows(TM, D_FF),
                  pl.BlockSpec(wg1.shape, lambda i: (0, 0, 0), **once), _rows(TM, D), _full((1, D)), ANY],
        out_specs=[_rows(TM, D_FF), _rows(TM, D), _rows(TM, D), _full((1, D))],
        out_shape=[jax.ShapeDtypeStruct((T, D_FF), BF16), jax.ShapeDtypeStruct((T, D), F32),
                   jax.ShapeDtypeStruct((T, D), BF16), jax.ShapeDtypeStruct((1, D), F32)],
        compiler_params=_cparams(56),
    )(dxb, dres, w2, r, wg1, x1, g2, dep)


def _mixer_bwd_a(dxb, wout, proj, c, lg, lb, wm, bst, clg, clb, dep):
    tm = TM_SGU
    n_tiles = T // tm

    def body(dx_ref, wo_ref, p_ref, c_ref, lg_ref, lb_ref, wm_ref, bs_ref, clg_ref, clb_ref, dep_ref,
             dpa_ref, dc_ref, dlg_ref, dlb_ref, dwm_ref, dbs_ref, dcb_ref, dclg_ref, dclb_ref,
             dv_buf, db_acc):
        i = pl.program_id(0)

        @pl.when(i == 0)
        def _():
            for ref in (dlg_ref, dlb_ref, dwm_ref, dbs_ref, dcb_ref, dclg_ref, dclb_ref, db_acc):
                ref[...] = jnp.zeros(ref.shape, F32)

        dmix = _dot_nt(dx_ref[...], wo_ref[...])
        ua = p_ref[:, 0:D_A]
        va = p_ref[:, D_A:2 * D_A]
        u, cdf_u = _gelu(ua)
        vg, cdf_v = _gelu(va)
        xhat, rstd = _ln(vg)
        v = (xhat * lg_ref[...] + lb_ref[...]).astype(BF16)
        mask = _tril_mask()
        for h in range(H_A):
            hc = slice(h * CHUNK, (h + 1) * CHUNK)
            wmh = jnp.where(mask, wm_ref[h], 0.0).astype(BF16)
            for cidx in range(tm // CHUNK):
                rc = slice(cidx * CHUNK, (cidx + 1) * CHUNK)
                vb = v[rc, hc]
                mixed = _dot(wmh, vb) + bs_ref[:, h:h + 1]
                da = dmix[rc, hc]
                dpa_ref[rc, hc] = (da * mixed * _gelu_grad(ua[rc, hc], cdf_u[rc, hc])).astype(BF16)
                dmixed = da * u[rc, hc]
                dmb = dmixed.astype(BF16)
                dv_buf[rc, hc] = _dot_tn(wmh, dmb)
                dwm_ref[h] += _dot_nt(dmb, vb)
                db_acc[:, hc] += dmixed
        dv = dv_buf[...]
        dlb_ref[...] += jnp.sum(dv, axis=0, keepdims=True)
        dlg_ref[...] += jnp.sum(dv * xhat, axis=0, keepdims=True)
        dvg = _ln_bwd(dv * lg_ref[...], xhat, rstd)
        dpa_ref[:, D_A:2 * D_A] = (dvg * _gelu_grad(va, cdf_v)).astype(BF16)

        for q in range(H_B):
            cols = slice(q * LANES, (q + 1) * LANES)
            chat, crstd = _ln(c_ref[:, cols])
            z = chat * clg_ref[:, cols] + clb_ref[:, cols]
            sg = _sigmoid(z)
            dz = dmix[:, D_A + q * LANES:D_A + (q + 1) * LANES] * (sg * (1.0 + z * (1.0 - sg)))
            dclb_ref[:, cols] += jnp.sum(dz, axis=0, keepdims=True)
            dclg_ref[:, cols] += jnp.sum(dz * chat, axis=0, keepdims=True)
            dc = _ln_bwd(dz * clg_ref[:, cols], chat, crstd)
            dc_ref[:, cols] = dc
            dcb_ref[:, cols] += jnp.sum(dc, axis=0, keepdims=True)

        @pl.when(i == n_tiles - 1)
        def _():
            for h in range(H_A):
                hc = slice(h * CHUNK, (h + 1) * CHUNK)
                dwm_ref[h] = jnp.where(mask, dwm_ref[h], 0.0)
                dbs_ref[h:h + 1, :] = jnp.sum(db_acc[:, hc].T, axis=0, keepdims=True)

    vec = _full((1, D_A))
    vshape = jax.ShapeDtypeStruct((1, D_A), F32)
    return pl.pallas_call(
        body, name="mixer_bwd_a", grid=(n_tiles,),
        in_specs=[_rows(tm, D), _full((D, D)), _rows(tm, IN_COLS), _rows(tm, D_B), vec, vec,
                  _full((H_A, CHUNK, CHUNK)), _full((CHUNK, H_A)), vec, vec, ANY],
        out_specs=[_rows(tm, 2 * D_A), _rows(tm, D_B), vec, vec, _full((H_A, CHUNK, CHUNK)),
                   _full((H_A, CHUNK)), vec, vec, vec],
        out_shape=[jax.ShapeDtypeStruct((T, 2 * D_A), BF16), jax.ShapeDtypeStruct((T, D_B), F32), vshape, vshape,
                   jax.ShapeDtypeStruct((H_A, CHUNK, CHUNK), F32), jax.ShapeDtypeStruct((H_A, CHUNK), F32),
                   vshape, vshape, vshape],
        scratch_shapes=[pltpu.VMEM((tm, D_A), F32), pltpu.VMEM((CHUNK, D_A), F32)],
        compiler_params=_cparams(32),
    )(dxb, wout, proj, c, lg, lb, wm, bst, clg, clb, dep)


def _mixer_bwd_b(dc, g, proj, cw, dpa):
    tm = TM_MIX
    n_tiles = T // tm
    hb = tm // HALO

    def body(dc_ref, dch_ref, g_ref, gh_ref, p_ref, cw_ref, dpa_ref, dp_ref, dcw_ref, dcbuf, gbuf, dwacc):
        i = pl.program_id(0)

        @pl.when(i == 0)
        def _():
            dwacc[...] = jnp.zeros(dwacc.shape, F32)

        _fill_shift_buffer(dcbuf, 0, dc_ref[...])
        _fill_shift_buffer(dcbuf, tm, jnp.where(i < n_tiles - 1, dch_ref[...], 0.0))
        _fill_shift_buffer(gbuf, 0, jnp.where(i > 0, gh_ref[...], 0.0))
        _fill_shift_buffer(gbuf, HALO, g_ref[...])
        _build_shifts(dcbuf)
        _build_shifts(gbuf)
        dp_ref[:, 0:2 * D_A] = dpa_ref[...]
        for q in range(H_B):
            cols = slice(q * LANES, (q + 1) * LANES)

            def row_block(rb, carry, q=q, cols=cols):
                base = pl.multiple_of(rb * RB, RB)
                rows = pl.ds(base, RB)
                dg = _conv_taps(dcbuf, cw_ref, q, base, CONV_W - 1, -1)
                val = p_ref[rows, cols]
                sg = _sigmoid(p_ref[rows, D_B + q * LANES:D_B + (q + 1) * LANES])
                dp_ref[rows, 2 * D_A + q * LANES:2 * D_A + (q + 1) * LANES] = (dg * sg).astype(BF16)
                dp_ref[rows, 2 * D_A + D_B + q * LANES:2 * D_A + D_B + (q + 1) * LANES] = (
                    dg * val * sg * (1.0 - sg)).astype(BF16)
                dcv = dcbuf[0, q, rows, :]
                parts = []
                for k in range(CONV_W):
                    prod = dcv * _shifted(gbuf, q, base, HALO - (CONV_W - 1) + k)
                    parts.append(jnp.sum(prod.reshape(RB // 8, 8, LANES), axis=0))
                for k in range(CONV_W):
                    dwacc[k * 8:(k + 1) * 8, cols] += parts[k]
                return carry

            lax.fori_loop(0, tm // RB, row_block, 0)

        @pl.when(i == n_tiles - 1)
        def _():
            for k in range(CONV_W):
                dcw_ref[k:k + 1, :] = jnp.sum(dwacc[k * 8:(k + 1) * 8, :], axis=0, keepdims=True)

    return pl.pallas_call(
        body, name="mixer_bwd_b", grid=(n_tiles,),
        in_specs=[_rows(tm, D_B),
                  pl.BlockSpec((HALO, D_B), lambda i: (jnp.minimum((i + 1) * hb, T // HALO - 1), 0)),
                  _rows(tm, D_B),
                  pl.BlockSpec((HALO, D_B), lambda i: (jnp.maximum(i * hb - 1, 0), 0)),
                  pl.BlockSpec((tm, 2 * D_B), lambda i: (i, 1)),
                  _full((CONV_W, D_B)), _rows(tm, 2 * D_A)],
        out_specs=[_rows(tm, IN_COLS), _full((CONV_W, D_B))],
        out_shape=[jax.ShapeDtypeStruct((T, IN_COLS), BF16), jax.ShapeDtypeStruct((CONV_W, D_B), F32)],
        scratch_shapes=[pltpu.VMEM((8, H_B, tm + HALO, LANES), F32), pltpu.VMEM((8, H_B, HALO + tm, LANES), F32),
                        pltpu.VMEM((CONV_W * 8, D_B), F32)],
        compiler_params=_cparams(32),
    )(dc, dc, g, g, proj, cw, dpa)


def _wgrad(name, a, g, bn, square_a=False):
    k = a.shape[1]
    n = g.shape[1]
    tk = min(k, 1024)
    tn = min(n, max(bn, 1024))
    nsub = tn // bn
    tt = TT_WGRAD
    nt = T // tt

    def body(a_ref, g_ref, o_ref, ob_ref):
        t = pl.program_id(2)

        @pl.when(t == 0)
        def _():
            o_ref[...] = jnp.zeros(o_ref.shape, F32)

        av = a_ref[...]
        if square_a:
            av = av * av
        for s in range(nsub):
            o_ref[s] += _dot_tn(av, g_ref[:, s * bn:(s + 1) * bn])

        @pl.when(t == nt - 1)
        def _():
            ob_ref[...] = o_ref[...].astype(BF16)

    ospec = pl.BlockSpec((nsub, tk, bn), lambda ki, ni, ti: (ni, ki, 0))
    return pl.pallas_call(
        body, name=name, grid=(k // tk, n // tn, nt),
        in_specs=[pl.BlockSpec((tt, tk), lambda ki, ni, ti: (ti, ki)),
                  pl.BlockSpec((tt, tn), lambda ki, ni, ti: (ti, ni))],
        out_specs=[ospec, ospec],
        out_shape=[jax.ShapeDtypeStruct((n // bn, k, bn), F32), jax.ShapeDtypeStruct((n // bn, k, bn), BF16)],
        compiler_params=_cparams(40, ("arbitrary", "arbitrary", "arbitrary")),
    )(a, g)


def _coords():
    return lax.axis_index("x"), lax.axis_index("y"), lax.axis_index("c")


def _flat(x, y, c):
    return 4 * x + 2 * y + c


def _peer(k):
    x, y, c = _coords()
    return (x ^ ((k >> 2) & 1), y ^ ((k >> 1) & 1), c ^ (k & 1))


def _hbm(a):
    return pltpu.with_memory_space_constraint(a, pltpu.HBM)


def _hbm_like(a):
    return pltpu.HBM(a.shape, a.dtype)


def _peer_sems():
    return pltpu.SemaphoreType.DMA((N_DEV - 1,))


def _place_own(shards):
    n = len(shards)
    shapes = [(s.shape if l is None else s.shape[1:]) for s, l, _ in shards]

    def body(*refs):
        ins, outs = refs[:n], refs[n:2 * n]
        stage_in, stage_out = refs[2 * n:3 * n], refs[3 * n:4 * n]
        in_sems, out_sems = refs[4 * n], refs[4 * n + 1]
        me = _flat(*_coords())
        srcs = [ins[a] if shards[a][1] is None else ins[a].at[shards[a][1]] for a in range(n)]
        loads = [pltpu.make_async_copy(srcs[a], stage_in[a], in_sems.at[a]) for a in range(n)]
        stores = [pltpu.make_async_copy(stage_out[a], outs[a].at[me], out_sems.at[a]) for a in range(n)]
        for cp in loads:
            cp.start()
        for a in range(n):
            loads[a].wait()
            stage_out[a][...] = stage_in[a][...].astype(stage_out[a].dtype)
            stores[a].start()
        for cp in stores:
            cp.wait()

    return pl.pallas_call(
        body, name="place_own", in_specs=[ANY] * n, out_specs=[ANY] * n,
        out_shape=[jax.ShapeDtypeStruct((N_DEV,) + shapes[a], shards[a][2]) for a in range(n)],
        scratch_shapes=[*[pltpu.VMEM(shapes[a], shards[a][0].dtype) for a in range(n)],
                        *[pltpu.VMEM(shapes[a], shards[a][2]) for a in range(n)],
                        pltpu.SemaphoreType.DMA((n,)), pltpu.SemaphoreType.DMA((n,))],
        compiler_params=pltpu.CompilerParams(vmem_limit_bytes=40 * MIB),
    )(*[s for s, _, _ in shards])


SIBLING = 1
CHIP_PEERS = (2, 4, 6)
FIRST_PEERS = (SIBLING,) + CHIP_PEERS


def _gather_start(lands):
    n = len(lands)

    def body(*refs):
        lnd, send, recv = refs[:n], refs[n:2 * n], refs[2 * n:3 * n]
        me = _flat(*_coords())
        for a in range(n):
            for j, k in enumerate(FIRST_PEERS):
                pltpu.make_async_remote_copy(
                    src_ref=lnd[a].at[me], dst_ref=lnd[a].at[me], send_sem=send[a].at[j],
                    recv_sem=recv[a].at[j], device_id=_peer(k), device_id_type=MESH_ID).start()

    sems = pltpu.SemaphoreType.DMA((len(FIRST_PEERS),))
    outs = pl.pallas_call(
        body, name="gather_start",
        out_shape=(*[sems] * (2 * n), *[_hbm_like(l) for l in lands]),
        in_specs=[HBM] * n, out_specs=(*[SEM] * (2 * n), *[HBM] * n),
        input_output_aliases={i: 2 * n + i for i in range(n)},
        compiler_params=pltpu.CompilerParams(has_side_effects=EFFECT),
    )(*[_hbm(l) for l in lands])
    return outs[:n], outs[n:2 * n], outs[2 * n:]


def _gather_pass_on(name, land, send, recv, after):
    def body(l_ref, send_ref, recv_ref, after_ref, send2, recv2, l_out):
        sibling = _peer(SIBLING)
        for j, k in enumerate(CHIP_PEERS):
            cp = pltpu.make_async_remote_copy(
                src_ref=l_ref.at[0], dst_ref=l_ref.at[0], send_sem=send_ref.at[1 + j], recv_sem=recv_ref.at[1 + j],
                device_id=_peer(k), device_id_type=MESH_ID)
            cp.wait_send()
            cp.wait_recv()
            blk = _flat(*_peer(k))
            pltpu.make_async_remote_copy(
                src_ref=l_ref.at[blk], dst_ref=l_ref.at[blk], send_sem=send2.at[j], recv_sem=recv2.at[j],
                device_id=sibling, device_id_type=MESH_ID).start()

    sems = pltpu.SemaphoreType.DMA((len(CHIP_PEERS),))
    return pl.pallas_call(
        body, name=name, out_shape=(sems, sems, _hbm_like(land)),
        in_specs=(HBM, SEM, SEM, ANY), out_specs=(SEM, SEM, HBM), input_output_aliases={0: 2},
        compiler_params=pltpu.CompilerParams(has_side_effects=EFFECT),
    )(land, send, recv, after)


def _gather_wait(name, land, send, recv, send2, recv2, after):
    def body(l_ref, send_ref, recv_ref, send2_ref, recv2_ref, after_ref, l_out):
        sibling = _peer(SIBLING)
        own = pltpu.make_async_remote_copy(
            src_ref=l_ref.at[0], dst_ref=l_ref.at[0], send_sem=send_ref.at[0], recv_sem=recv_ref.at[0],
            device_id=sibling, device_id_type=MESH_ID)
        own.wait_send()
        own.wait_recv()
        for j in range(len(CHIP_PEERS)):
            cp = pltpu.make_async_remote_copy(
                src_ref=l_ref.at[0], dst_ref=l_ref.at[0], send_sem=send2_ref.at[j], recv_sem=recv2_ref.at[j],
                device_id=sibling, device_id_type=MESH_ID)
            cp.wait_send()
            cp.wait_recv()

    return pl.pallas_call(
        body, name=name, out_shape=_hbm_like(land),
        in_specs=(HBM, SEM, SEM, SEM, SEM, ANY), out_specs=HBM, input_output_aliases={0: 0},
        compiler_params=pltpu.CompilerParams(has_side_effects=EFFECT),
    )(land, send, recv, send2, recv2, after)


def _scatter_start(name, grads, lands, layer):
    n = len(grads)

    def body(*refs):
        g, lnd, send, recv = refs[:n], refs[n:2 * n], refs[2 * n:3 * n], refs[3 * n:4 * n]
        for a in range(n):
            for k in range(1, N_DEV):
                to = _peer(k)
                pltpu.make_async_remote_copy(
                    src_ref=g[a].at[_flat(*to)], dst_ref=lnd[a].at[layer, k - 1], send_sem=send[a].at[k - 1],
                    recv_sem=recv[a].at[k - 1], device_id=to, device_id_type=MESH_ID).start()

    outs = pl.pallas_call(
        body, name=name,
        out_shape=(*[_peer_sems()] * (2 * n), *[_hbm_like(g) for g in grads], *[_hbm_like(l) for l in lands]),
        in_specs=[HBM] * (2 * n), out_specs=(*[SEM] * (2 * n), *[HBM] * (2 * n)),
        input_output_aliases={i: 2 * n + i for i in range(2 * n)},
        compiler_params=pltpu.CompilerParams(has_side_effects=EFFECT),
    )(*[_hbm(g) for g in grads], *[_hbm(l) for l in lands])
    return outs[:n], outs[n:2 * n], outs[2 * n:3 * n], outs[3 * n:]


def _scatter_wait(grads, lands, sends, recvs, after):
    n, nw = len(grads), len(lands)

    def body(*refs):
        g, lnd = refs[:n], refs[n:n + nw]
        send, recv = refs[n + nw:2 * n + nw], refs[2 * n + nw:3 * n + nw]
        for a in range(n):
            for k in range(1, N_DEV):
                cp = pltpu.make_async_remote_copy(
                    src_ref=g[a].at[0], dst_ref=lnd[a // DEPTH].at[a % DEPTH, 0],
                    send_sem=send[a].at[k - 1], recv_sem=recv[a].at[k - 1],
                    device_id=_peer(k), device_id_type=MESH_ID)
                cp.wait_send()
                cp.wait_recv()

    outs = pl.pallas_call(
        body, name="scatter_wait",
        out_shape=(*[_hbm_like(g) for g in grads], *[_hbm_like(l) for l in lands]),
        in_specs=(*[HBM] * (n + nw), *[SEM] * (2 * n), ANY), out_specs=[HBM] * (n + nw),
        input_output_aliases={i: i for i in range(n + nw)},
        compiler_params=pltpu.CompilerParams(has_side_effects=EFFECT),
    )(*grads, *lands, *sends, *recvs, after)
    return outs[n:]


def _own_blocks(grads_f32):
    nw = len(grads_f32)
    flat32 = [g for w in grads_f32 for g in w]
    n = len(flat32)

    def body(*refs):
        g32, own, stage = refs[:n], refs[n:n + nw], refs[n + nw:2 * n + nw]
        in_sems, out_sems = refs[2 * n + nw], refs[2 * n + nw + 1]
        me = _flat(*_coords())
        loads = [pltpu.make_async_copy(g32[a].at[me], stage[a], in_sems.at[a]) for a in range(n)]
        stores = [pltpu.make_async_copy(stage[a], own[a // DEPTH].at[a % DEPTH], out_sems.at[a]) for a in range(n)]
        for cp in loads:
            cp.start()
        for a in range(n):
            loads[a].wait()
            stores[a].start()
        for cp in stores:
            cp.wait()

    return pl.pallas_call(
        body, name="own_blocks", in_specs=[ANY] * n, out_specs=[ANY] * nw,
        out_shape=[jax.ShapeDtypeStruct((DEPTH,) + w[0].shape[1:], F32) for w in grads_f32],
        scratch_shapes=[*[pltpu.VMEM(g.shape[1:], F32) for g in flat32],
                        pltpu.SemaphoreType.DMA((n,)), pltpu.SemaphoreType.DMA((n,))],
        compiler_params=pltpu.CompilerParams(vmem_limit_bytes=24 * MIB),
    )(*flat32)


def _all_reduce_small(part):
    rows = part.shape[0]
    br = rows // N_DEV
    assert br * N_DEV == rows and br % 8 == 0

    def body(p_ref, o_ref, slots, send1, recv1, send2, recv2):
        me = _flat(*_coords())

        def block(ref, d):
            return ref.at[pl.ds(pl.multiple_of(d * br, 8), br), :]

        slots[me] = p_ref[pl.ds(pl.multiple_of(me * br, 8), br), :]
        scatter, gather = [], []
        for k in range(1, N_DEV):
            to = _peer(k)
            scatter.append(pltpu.make_async_remote_copy(
                src_ref=block(p_ref, _flat(*to)), dst_ref=slots.at[me],
                send_sem=send1.at[k - 1], recv_sem=recv1.at[k - 1], device_id=to, device_id_type=MESH_ID))
            gather.append(pltpu.make_async_remote_copy(
                src_ref=block(o_ref, me), dst_ref=block(o_ref, me),
                send_sem=send2.at[k - 1], recv_sem=recv2.at[k - 1], device_id=to, device_id_type=MESH_ID))
        for cp in scatter:
            cp.start()
        for cp in scatter:
            cp.wait()
        acc = slots[0]
        for d in range(1, N_DEV):
            acc = acc + slots[d]
        o_ref[pl.ds(pl.multiple_of(me * br, 8), br), :] = acc
        for cp in gather:
            cp.start()
        for cp in gather:
            cp.wait()

    return pl.pallas_call(
        body, name="all_reduce_small",
        in_specs=[pl.BlockSpec(memory_space=pltpu.VMEM)], out_specs=pl.BlockSpec(memory_space=pltpu.VMEM),
        out_shape=jax.ShapeDtypeStruct(part.shape, F32),
        scratch_shapes=[pltpu.VMEM((N_DEV, br, LANES), F32)] + [pltpu.SemaphoreType.DMA((N_DEV - 1,))] * 4,
        compiler_params=pltpu.CompilerParams(vmem_limit_bytes=24 * MIB),
    )(part)


def _adam_math(w, g, m, v):
    m = ADAM_B1 * m + (1.0 - ADAM_B1) * g
    v = ADAM_B2 * v + (1.0 - ADAM_B2) * (g * g)
    m_hat = m / (1.0 - ADAM_B1 ** ADAM_STEP)
    v_hat = v / (1.0 - ADAM_B2 ** ADAM_STEP)
    delta = -ADAM_LR * (m_hat / (jnp.sqrt(v_hat) + ADAM_EPS) + ADAM_WD * w)
    return delta, m, v


def _adam_sharded(name, own, recv, w, m, v):
    _, r, c = w.shape
    tr = min(r, 256)

    def body(own_ref, recv_ref, w_ref, m_ref, v_ref, g_ref, d_ref, nm_ref, nv_ref):
        g = own_ref[...]
        for k in range(N_DEV - 1):
            g = g + recv_ref[k].astype(F32)
        delta, nm, nv = _adam_math(w_ref[...], g, m_ref[...], v_ref[...])
        g_ref[...] = g
        d_ref[...] = delta
        nm_ref[...] = nm
        nv_ref[...] = nv

    blk = pl.BlockSpec((None, tr, c), lambda l, i: (l, i, 0))
    shp = jax.ShapeDtypeStruct(w.shape, F32)
    return pl.pallas_call(
        body, name=name, grid=(DEPTH, r // tr),
        in_specs=[blk, pl.BlockSpec((None, N_DEV - 1, tr, c), lambda l, i: (l, 0, i, 0)), blk, blk, blk],
        out_specs=[blk] * 4, out_shape=[shp] * 4,
        compiler_params=_cparams(32, ("arbitrary", "arbitrary")),
    )(own, recv, w, m, v)


def _adam_small(g, w, m, v):
    rows = g.shape[0]

    def body(g_ref, w_ref, m_ref, v_ref, d_ref, nm_ref, nv_ref):
        delta, nm, nv = _adam_math(w_ref[...], g_ref[...], m_ref[...], v_ref[...])
        d_ref[...] = delta
        nm_ref[...] = nm
        nv_ref[...] = nv

    spec = _full((rows, LANES))
    shp = jax.ShapeDtypeStruct((rows, LANES), F32)
    return pl.pallas_call(
        body, name="adam_small", grid=(1,),
        in_specs=[spec] * 4, out_specs=[spec] * 3, out_shape=[shp] * 3,
        compiler_params=_cparams(24),
    )(g, w, m, v)


SMALL = ["norm1_g", "sgu_ln_g", "sgu_ln_b", "sgu_w", "sgu_b", "conv_b", "conv_ln_g", "conv_ln_b", "norm2_g",
         "final_g"]


def _pack(arrays):
    flat = jnp.concatenate([a.reshape(-1) for a in arrays])
    pad = (-flat.shape[0]) % (N_DEV * 8 * LANES)
    return jnp.pad(flat, (0, pad)).reshape(-1, LANES)


def _unpack(packed, shapes):
    flat = packed.reshape(-1)
    out, off = [], 0
    for s in shapes:
        size = 1
        for d in s:
            size *= d
        out.append(flat[off:off + size].reshape(s))
        off += size
    return out


def kernel(x, norm1_g, w_in, sgu_ln_g, sgu_ln_b, sgu_w, sgu_b, conv_w, conv_b, conv_ln_g, conv_ln_b, w_out, norm2_g, w_ff1, w_ff2, final_g, loss_target, m_norm1_g, m_w_in, m_sgu_ln_g, m_sgu_ln_b, m_sgu_w, m_sgu_b, m_conv_w, m_conv_b, m_conv_ln_g, m_conv_ln_b, m_w_out, m_norm2_g, m_w_ff1, m_w_ff2, m_final_g, v_norm1_g, v_w_in, v_sgu_ln_g, v_sgu_ln_b, v_sgu_w, v_sgu_b, v_conv_w, v_conv_b, v_conv_ln_g, v_conv_ln_b, v_w_out, v_norm2_g, v_w_ff1, v_w_ff2, v_final_g):
    x2d = x.reshape(T, D)
    tgt = loss_target.reshape(T, D)
    cw_shard = conv_w.reshape(CONV_W, LANES)

    gnames, shards = [], []
    for l in range(DEPTH):
        for k, w in (("w_in", w_in), ("w_out", w_out), ("w_ff1", w_ff1), ("w_ff2", w_ff2)):
            gnames.append(f"{k}{l}")
            shards.append((w, l, BF16))
        if l == 0:
            gnames.insert(1, "conv_w")
            shards.insert(1, (cw_shard, None, F32))
    sends, recvs, lands = _gather_start(_place_own(shards))
    gidx = {k: i for i, k in enumerate(gnames)}

    passed = {}

    def pass_on(k, after):
        i = gidx[k]
        passed[k] = _gather_pass_on("gather_pass_on_" + k, lands[i], sends[i], recvs[i], after)

    def gathered(k, after):
        i = gidx[k]
        send2, recv2, land = passed[k]
        return _gather_wait("gather_wait_" + k, land, sends[i], recvs[i], send2, recv2, after)

    saved = []
    xl = x2d
    cw_full = None
    for l in range(DEPTH):
        g1 = norm1_g[l].reshape(1, D)
        g2 = norm2_g[l].reshape(1, D)
        lg, lb = sgu_ln_g[l].reshape(1, D_A), sgu_ln_b[l].reshape(1, D_A)
        bst = sgu_b[l].T
        cb = conv_b[l].reshape(1, D_B)
        clg, clb = conv_ln_g[l].reshape(1, D_B), conv_ln_b[l].reshape(1, D_B)
        if l == 0:
            pass_on("w_in0", xl)
        wg_in = gathered(f"w_in{l}", xl)
        if l == 0:
            pass_on("conv_w", xl)
        h, proj = _fwd_in(xl, g1, wg_in)
        if l == 0:
            cw_full = gathered("conv_w", proj).reshape(N_DEV, DEPTH, CONV_W, D_B // N_DEV).transpose(
                1, 2, 0, 3).reshape(DEPTH, CONV_W, D_B)
        pass_on(f"w_out{l}", proj)
        mix, c, g = _mixer_fwd(proj, lg, lb, sgu_w[l], bst, cw_full[l], cb, clg, clb)
        pass_on(f"w_ff1{l}", mix)
        wout = gathered(f"w_out{l}", mix).reshape(D, D)
        x1, h2 = _fwd_out(mix, wout, xl, g2)
        pass_on(f"w_ff2{l}", h2)
        wg_ff1 = gathered(f"w_ff1{l}", h2)
        r = _fwd_ff1(h2, wg_ff1)
        if l + 1 < DEPTH:
            pass_on(f"w_in{l + 1}", r)
        w2 = gathered(f"w_ff2{l}", r).reshape(D_FF, D)
        saved.append(dict(x=xl, h=h, proj=proj, mix=mix, c=c, g=g, x1=x1, h2=h2, r=r, wg_in=wg_in, wout=wout,
                          wg_ff1=wg_ff1, w2=w2, g1=g1, g2=g2, lg=lg, lb=lb, bst=bst, clg=clg, clb=clb))
        if l + 1 < DEPTH:
            xl = _fwd_ff2(r, w2, x1)
        else:
            dx, dxb, loss_part, d_final_g = _fwd_ff2_loss(r, w2, x1, final_g.reshape(1, D), tgt)

    names = ["w_in", "w_out", "w_ff1", "w_ff2"]
    block = dict(w_in=(D, IN_COLS // N_DEV), w_out=(D // N_DEV, D), w_ff1=(D, D_FF // N_DEV), w_ff2=(D_FF // N_DEV, D))
    land = {k: lax.empty((DEPTH, N_DEV - 1) + block[k], BF16) for k in names}
    big32 = {k: [None] * DEPTH for k in names}
    big16 = {k: [None] * DEPTH for k in names}
    ssend = {k: [None] * DEPTH for k in names}
    srecv = {k: [None] * DEPTH for k in names}

    def send_grads(l, grads):
        ks = list(grads)
        for k in ks:
            big32[k][l] = grads[k][0].reshape((N_DEV,) + block[k])
        sends, recvs, g16s, lands = _scatter_start(
            "scatter_start_" + "_".join(ks) + str(l), [grads[k][1].reshape((N_DEV,) + block[k]) for k in ks],
            [land[k] for k in ks], l)
        for i, k in enumerate(ks):
            ssend[k][l], srecv[k][l], big16[k][l], land[k] = sends[i], recvs[i], g16s[i], lands[i]

    small = {}
    d_conv_w = [None] * DEPTH
    for l in reversed(range(DEPTH)):
        s = saved[l]
        send_grads(l, dict(w_ff2=_wgrad("wgrad_ff2", s["r"], dxb, D, square_a=True)))
        df1, dx1, dx1b, dg2 = _bwd_mlp(dxb, dx, s["w2"], s["r"], s["wg_ff1"], s["x1"], s["g2"], big16["w_ff2"][l])
        send_grads(l, dict(w_ff1=_wgrad("wgrad_ff1", s["h2"], df1, D_FF // N_DEV),
                           w_out=_wgrad("wgrad_out", s["mix"], dx1b, D)))
        dpa, dc, dlg, dlb, dwm, dbs, dcb, dclg, dclb = _mixer_bwd_a(
            dx1b, s["wout"], s["proj"], s["c"], s["lg"], s["lb"], sgu_w[l], s["bst"], s["clg"], s["clb"],
            big16["w_out"][l])
        dproj, d_conv_w[l] = _mixer_bwd_b(dc, s["g"], s["proj"], cw_full[l], dpa)
        send_grads(l, dict(w_in=_wgrad("wgrad_in", s["h"], dproj, IN_COLS // N_DEV)))
        dx, dxb, dg1 = _bwd_norm_matmul("bwd_in", dproj, s["wg_in"], s["x"], s["g1"], dx1, big16["w_in"][l])
        small[l] = dict(norm1_g=dg1, sgu_ln_g=dlg, sgu_ln_b=dlb, sgu_w=dwm, sgu_b=dbs, conv_b=dcb, conv_ln_g=dclg,
                        conv_ln_b=dclb, norm2_g=dg2)
    grad_x = dx.reshape(1, T, D)

    own = _own_blocks([big32[k] for k in names])
    recv = _scatter_wait([g for k in names for g in big16[k]], [land[k] for k in names],
                         [q for k in names for q in ssend[k]], [q for k in names for q in srecv[k]], dx)
    wmv = dict(w_in=(w_in, m_w_in, v_w_in), w_out=(w_out, m_w_out, v_w_out), w_ff1=(w_ff1, m_w_ff1, v_w_ff1),
               w_ff2=(w_ff2, m_w_ff2, v_w_ff2))
    res = {}
    for idx, k in enumerate(names):
        res[k] = _adam_sharded("adam_" + k, own[idx], recv[idx], *wmv[k])

    rep = dict(norm1_g=(norm1_g, m_norm1_g, v_norm1_g), sgu_ln_g=(sgu_ln_g, m_sgu_ln_g, v_sgu_ln_g),
               sgu_ln_b=(sgu_ln_b, m_sgu_ln_b, v_sgu_ln_b), sgu_w=(sgu_w, m_sgu_w, v_sgu_w),
               sgu_b=(sgu_b, m_sgu_b, v_sgu_b), conv_b=(conv_b, m_conv_b, v_conv_b),
               conv_ln_g=(conv_ln_g, m_conv_ln_g, v_conv_ln_g), conv_ln_b=(conv_ln_b, m_conv_ln_b, v_conv_ln_b),
               norm2_g=(norm2_g, m_norm2_g, v_norm2_g), final_g=(final_g, m_final_g, v_final_g))
    parts = []
    for k in SMALL:
        if k == "final_g":
            parts.append(d_final_g.reshape(rep[k][0].shape))
        else:
            parts.append(jnp.stack([small[l][k].reshape(rep[k][0].shape[1:]) for l in range(DEPTH)]))
    parts.append(jnp.stack(d_conv_w))
    parts.append(loss_part[0, 0:1])
    shapes = [p.shape for p in parts]
    summed = _unpack(_all_reduce_small(_pack(parts)), shapes)
    loss = summed[-1][0]
    me = _flat(*_coords())
    g_conv_w = lax.dynamic_slice_in_dim(summed[-2], me * (D_B // N_DEV), D_B // N_DEV, axis=2)
    g_small = summed[:-2] + [g_conv_w]
    keys = SMALL + ["conv_w"]
    rep["conv_w"] = (conv_w, m_conv_w, v_conv_w)
    shapes = [a.shape for a in g_small]
    packed = [_pack(g_small)] + [_pack([rep[k][i] for k in keys]) for i in range(3)]
    d_s, nm_s, nv_s = (_unpack(a, shapes) for a in _adam_small(*packed))
    for i, k in enumerate(keys):
        res[k] = (g_small[i], d_s[i], nm_s[i], nv_s[i])

    order = ["norm1_g", "w_in", "sgu_ln_g", "sgu_ln_b", "sgu_w", "sgu_b", "conv_w", "conv_b", "conv_ln_g",
             "conv_ln_b", "w_out", "norm2_g", "w_ff1", "w_ff2", "final_g"]
    return (loss, grad_x, *[res[k][0] for k in order], *[res[k][1] for k in order],
            *[res[k][2] for k in order], *[res[k][3] for k in order])
```

```python
import functools

import jax
import jax.numpy as jnp
from jax import lax
from jax.experimental import pallas as pl
from jax.experimental.pallas import tpu as pltpu

F32 = jnp.float32
BF16 = jnp.bfloat16

N_DEV = 8
DEPTH = 2
T = 4096
D = 1024
D_A = 512
D_B = 512
CHUNK = 128
H_A = 4
H_B = 4
CONV_W = 31
HALO = 32
D_FF = 4096
IN_COLS = 2048
EPS = 1e-6

ADAM_LR = 0.001
ADAM_B1 = 0.9
ADAM_B2 = 0.999
ADAM_EPS = 1e-08
ADAM_WD = 0.01
ADAM_STEP = 10

TM = 512
TM_FWD = 1024
TT_WGRAD = 1024
TM_MIX = 256
TM_SGU = 512
RB = 64
CONV_CHAINS = 2
LANES = 128
MIB = 1024 * 1024
SCOPED_VMEM_MIB = 60

SQRT_HALF = 0.7071067811865476
INV_SQRT_2PI = 0.3989422804014327

MESH_ID = pl.DeviceIdType.MESH
ANY = pl.BlockSpec(memory_space=pl.ANY)
HBM = pl.BlockSpec(memory_space=pltpu.HBM)
SEM = pl.BlockSpec(memory_space=pltpu.SEMAPHORE)
EFFECT = pltpu.SideEffectType.DATAFLOW_SIDE_EFFECTING


def _cparams(vmem_mib, sem=("arbitrary",)):
    assert vmem_mib <= SCOPED_VMEM_MIB
    return pltpu.CompilerParams(dimension_semantics=sem, vmem_limit_bytes=SCOPED_VMEM_MIB * MIB)


def _full(shape):
    return pl.BlockSpec(shape, lambda *_: (0,) * len(shape))


def _rows(tm, cols):
    return pl.BlockSpec((tm, cols), lambda i: (i, 0))


def _gelu(x):
    cdf = 0.5 * (1.0 + lax.erf(x * SQRT_HALF))
    return x * cdf, cdf


def _gelu_grad(x, cdf):
    return cdf + x * (INV_SQRT_2PI * jnp.exp(-0.5 * x * x))


def _sigmoid(x):
    return 1.0 / (1.0 + jnp.exp(-x))


def _ln(x):
    mu = jnp.mean(x, axis=-1, keepdims=True)
    xc = x - mu
    rstd = lax.rsqrt(jnp.mean(xc * xc, axis=-1, keepdims=True) + EPS)
    return xc * rstd, rstd


def _ln_bwd(dyh, xhat, rstd):
    return rstd * (dyh - jnp.mean(dyh, axis=-1, keepdims=True) - xhat * jnp.mean(dyh * xhat, axis=-1, keepdims=True))


def _rms(x):
    return lax.rsqrt(jnp.mean(x * x, axis=-1, keepdims=True) + EPS)


def _rms_bwd(dh, x, r, g):
    n = x * r
    dn = dh * g
    dx = r * (dn - n * jnp.mean(dn * n, axis=-1, keepdims=True))
    return dx, jnp.sum(dh * n, axis=0, keepdims=True)


def _dot(a, b):
    return jnp.dot(a, b, preferred_element_type=F32)


def _dot_nt(a, b):
    return lax.dot_general(a, b, (((1,), (1,)), ((), ())), preferred_element_type=F32)


def _dot_tn(a, b):
    return lax.dot_general(a, b, (((0,), (0,)), ((), ())), preferred_element_type=F32)


def _tril_mask():
    r = lax.broadcasted_iota(jnp.int32, (CHUNK, CHUNK), 0)
    c = lax.broadcasted_iota(jnp.int32, (CHUNK, CHUNK), 1)
    return r >= c


def _fwd_in(x, g1, wg):
    bn = wg.shape[2]

    def body(x_ref, g_ref, w_ref, h_ref, p_ref):
        xv = x_ref[...]
        h = (xv * _rms(xv) * g_ref[...]).astype(BF16)
        h_ref[...] = h
        for j in range(N_DEV):
            p_ref[:, j * bn:(j + 1) * bn] = _dot(h, w_ref[j])

    return pl.pallas_call(
        body, name="fwd_in", grid=(T // TM_FWD,),
        in_specs=[_rows(TM_FWD, D), _full((1, D)), _full(wg.shape)],
        out_specs=[_rows(TM_FWD, D), _rows(TM_FWD, IN_COLS)],
        out_shape=[jax.ShapeDtypeStruct((T, D), BF16), jax.ShapeDtypeStruct((T, IN_COLS), F32)],
        compiler_params=_cparams(32),
    )(x, g1, wg)


def _fill_shift_buffer(sh_ref, row0, value):
    for q in range(sh_ref.shape[1]):
        sh_ref[0, q, row0:row0 + value.shape[0], :] = value[:, q * LANES:(q + 1) * LANES]


def _build_shifts(sh_ref):
    rows = sh_ref.shape[2]
    for p in range(1, 8):
        for q in range(sh_ref.shape[1]):
            sh_ref[p, q, 0:rows - 8, :] = sh_ref[0, q, p:p + rows - 8, :]


def _shifted(sh_ref, q, base, off):
    start = base + (off - off % 8)
    if not isinstance(start, int):
        start = pl.multiple_of(start, 8)
    return sh_ref[off % 8, q, pl.ds(start, RB), :]


def _conv_taps(sh_ref, w_ref, q, base, first_tap_row, step):
    cols = slice(q * LANES, (q + 1) * LANES)
    acc = [jnp.zeros((RB, LANES), F32) for _ in range(CONV_CHAINS)]
    for k in range(CONV_W):
        term = _shifted(sh_ref, q, base, first_tap_row + step * k) * w_ref[k:k + 1, cols]
        acc[k % CONV_CHAINS] = acc[k % CONV_CHAINS] + term
    return functools.reduce(lambda a, b: a + b, acc)


def _mixer_fwd(proj, lg, lb, wm, bst, cw, cb, clg, clb, wout, x, g2):
    tm = TM_MIX
    hb = tm // HALO

    def body(p_ref, ph_ref, lg_ref, lb_ref, wm_ref, bs_ref, cw_ref, cb_ref, clg_ref, clb_ref, wo_ref, x_ref, g2_ref,
             mix_ref, c_ref, g_ref, x1_ref, h2_ref, gbuf):
        i = pl.program_id(0)
        u, _ = _gelu(p_ref[:, 0:D_A])
        vg, _ = _gelu(p_ref[:, D_A:2 * D_A])
        xhat, _ = _ln(vg)
        v = (xhat * lg_ref[...] + lb_ref[...]).astype(BF16)
        mask = _tril_mask()
        for h in range(H_A):
            hc = slice(h * CHUNK, (h + 1) * CHUNK)
            wmh = jnp.where(mask, wm_ref[h], 0.0).astype(BF16)
            for c in range(tm // CHUNK):
                rc = slice(c * CHUNK, (c + 1) * CHUNK)
                mixed = _dot(wmh, v[rc, hc]) + bs_ref[:, h:h + 1]
                mix_ref[rc, hc] = (u[rc, hc] * mixed).astype(BF16)

        g = p_ref[:, 2 * D_A:2 * D_A + D_B] * _sigmoid(p_ref[:, 2 * D_A + D_B:IN_COLS])
        g_ref[...] = g
        gh = ph_ref[:, 0:D_B] * _sigmoid(ph_ref[:, D_B:2 * D_B])
        _fill_shift_buffer(gbuf, 0, jnp.where(i > 0, gh, 0.0))
        _fill_shift_buffer(gbuf, HALO, g)
        _build_shifts(gbuf)
        for q in range(H_B):
            cols = slice(q * LANES, (q + 1) * LANES)
            for rb in range(tm // RB):
                acc = _conv_taps(gbuf, cw_ref, q, rb * RB, HALO - (CONV_W - 1), 1)
                c_ref[rb * RB:(rb + 1) * RB, cols] = acc + cb_ref[:, cols]
        for q in range(H_B):
            cols = slice(q * LANES, (q + 1) * LANES)
            chat, _ = _ln(c_ref[:, cols])
            z = chat * clg_ref[:, cols] + clb_ref[:, cols]
            mix_ref[:, D_A + q * LANES:D_A + (q + 1) * LANES] = (z * _sigmoid(z)).astype(BF16)

        x1 = x_ref[...] + _dot(mix_ref[...], wo_ref[...])
        x1_ref[...] = x1
        h2_ref[...] = (x1 * _rms(x1) * g2_ref[...]).astype(BF16)

    vec = _full((1, D_A))
    return pl.pallas_call(
        body, name="mixer_fwd", grid=(T // tm,),
        in_specs=[_rows(tm, IN_COLS),
                  pl.BlockSpec((HALO, 2 * D_B), lambda i: (jnp.maximum(i * hb - 1, 0), 1)),
                  vec, vec, _full((H_A, CHUNK, CHUNK)), _full((CHUNK, H_A)),
                  _full((CONV_W, D_B)), vec, vec, vec,
                  pl.BlockSpec((D, D), lambda i: (0, 0), pipeline_mode=pl.Buffered(1)), _rows(tm, D), _full((1, D))],
        out_specs=[_rows(tm, D), _rows(tm, D_B), _rows(tm, D_B), _rows(tm, D), _rows(tm, D)],
        out_shape=[jax.ShapeDtypeStruct((T, D), BF16), jax.ShapeDtypeStruct((T, D_B), F32),
                   jax.ShapeDtypeStruct((T, D_B), F32), jax.ShapeDtypeStruct((T, D), F32),
                   jax.ShapeDtypeStruct((T, D), BF16)],
        scratch_shapes=[pltpu.VMEM((8, H_B, HALO + tm, LANES), F32)],
        compiler_params=_cparams(40),
    )(proj, proj, lg, lb, wm, bst, cw, cb, clg, clb, wout, x, g2)


def _fwd_ff1(h2, wg):
    bn = wg.shape[2]

    def body(h_ref, w_ref, r_ref):
        h = h_ref[...]
        for j in range(N_DEV):
            r_ref[:, j * bn:(j + 1) * bn] = jnp.maximum(_dot(h, w_ref[j]), 0.0).astype(BF16)

    return pl.pallas_call(
        body, name="fwd_ff1", grid=(T // TM_FWD,),
        in_specs=[_rows(TM_FWD, D), _full(wg.shape)],
        out_specs=_rows(TM_FWD, D_FF),
        out_shape=jax.ShapeDtypeStruct((T, D_FF), BF16),
        compiler_params=_cparams(48),
    )(h2, wg)


def _fwd_ff2(r, w2, x1):
    def body(r_ref, w_ref, x_ref, o_ref):
        rv = r_ref[...]
        o_ref[...] = x_ref[...] + _dot(rv * rv, w_ref[...])

    return pl.pallas_call(
        body, name="fwd_ff2", grid=(T // TM_FWD,),
        in_specs=[_rows(TM_FWD, D_FF), _full((D_FF, D)), _rows(TM_FWD, D)],
        out_specs=_rows(TM_FWD, D),
        out_shape=jax.ShapeDtypeStruct((T, D), F32),
        compiler_params=_cparams(48),
    )(r, w2, x1)


def _fwd_ff2_loss(r, w2, x1, gf, tgt):
    def body(r_ref, w_ref, x_ref, g_ref, t_ref, dx_ref, dxb_ref, loss_ref, dg_ref):
        i = pl.program_id(0)

        @pl.when(i == 0)
        def _():
            loss_ref[...] = jnp.zeros(loss_ref.shape, F32)
            dg_ref[...] = jnp.zeros(dg_ref.shape, F32)

        rv = r_ref[...]
        xv = x_ref[...] + _dot(rv * rv, w_ref[...])
        rn = _rms(xv)
        diff = xv * rn * g_ref[...] - t_ref[...]
        loss_ref[...] += 0.5 * jnp.sum(jnp.mean(diff * diff, axis=-1, keepdims=True), axis=0, keepdims=True)
        dx, dg = _rms_bwd(diff * (1.0 / D), xv, rn, g_ref[...])
        dx_ref[...] = dx
        dxb_ref[...] = dx.astype(BF16)
        dg_ref[...] += dg

    return pl.pallas_call(
        body, name="fwd_ff2_loss", grid=(T // TM,),
        in_specs=[_rows(TM, D_FF), pl.BlockSpec((D_FF, D), lambda i: (0, 0), pipeline_mode=pl.Buffered(1)),
                  _rows(TM, D), _full((1, D)), _rows(TM, D)],
        out_specs=[_rows(TM, D), _rows(TM, D), _full((8, LANES)), _full((1, D))],
        out_shape=[jax.ShapeDtypeStruct((T, D), F32), jax.ShapeDtypeStruct((T, D), BF16),
                   jax.ShapeDtypeStruct((8, LANES), F32), jax.ShapeDtypeStruct((1, D), F32)],
        compiler_params=_cparams(40),
    )(r, w2, x1, gf, tgt)


def _bwd_mlp(dxb, dres, w2, r, wg1, x1, g2, dep):
    bn = wg1.shape[2]

    def body(d_ref, dres_ref, w2_ref, r_ref, w1_ref, x_ref, g_ref, dep_ref, df1_ref, dx_ref, dxb_ref, dg_ref):
        i = pl.program_id(0)

        @pl.when(i == 0)
        def _():
            dg_ref[...] = jnp.zeros(dg_ref.shape, F32)

        d = d_ref[...]
        dh = jnp.zeros((TM, D), F32)
        for j in range(N_DEV):
            cols = slice(j * bn, (j + 1) * bn)
            df1 = (2.0 * r_ref[:, cols].astype(F32) * _dot_nt(d, w2_ref[cols, :])).astype(BF16)
            df1_ref[:, cols] = df1
            dh = dh + _dot_nt(df1, w1_ref[j])
        xv = x_ref[...]
        dxn, dg = _rms_bwd(dh, xv, _rms(xv), g_ref[...])
        dx = dres_ref[...] + dxn
        dx_ref[...] = dx
        dxb_ref[...] = dx.astype(BF16)
        dg_ref[...] += dg

    once = dict(pipeline_mode=pl.Buffered(1))
    return pl.pallas_call(
        body, name="bwd_mlp", grid=(T // TM,),
        in_specs=[_rows(TM, D), _rows(TM, D), pl.BlockSpec((D_FF, D), lambda i: (0, 0), **once), _rows(TM, D_FF),
                  pl.BlockSpec(wg1.shape, lambda i: (0, 0, 0), **once), _rows(TM, D), _full((1, D)), ANY],
        out_specs=[_rows(TM, D_FF), _rows(TM, D), _rows(TM, D), _full((1, D))],
        out_shape=[jax.ShapeDtypeStruct((T, D_FF), BF16), jax.ShapeDtypeStruct((T, D), F32),
                   jax.ShapeDtypeStruct((T, D), BF16), jax.ShapeDtypeStruct((1, D), F32)],
        compiler_params=_cparams(56),
    )(dxb, dres, w2, r, wg1, x1, g2, dep)


def _mixer_bwd_a(dxb, wout, proj, c, lg, lb, wm, bst, clg, clb, dep):
    tm = TM_SGU
    n_tiles = T // tm

    def body(dx_ref, wo_ref, p_ref, c_ref, lg_ref, lb_ref, wm_ref, bs_ref, clg_ref, clb_ref, dep_ref,
             dpa_ref, dc_ref, dlg_ref, dlb_ref, dwm_ref, dbs_ref, dcb_ref, dclg_ref, dclb_ref,
             dv_buf, db_acc):
        i = pl.program_id(0)

        @pl.when(i == 0)
        def _():
            for ref in (dlg_ref, dlb_ref, dwm_ref, dbs_ref, dcb_ref, dclg_ref, dclb_ref, db_acc):
                ref[...] = jnp.zeros(ref.shape, F32)

        dmix = _dot_nt(dx_ref[...], wo_ref[...])
        ua = p_ref[:, 0:D_A]
        va = p_ref[:, D_A:2 * D_A]
        u, cdf_u = _gelu(ua)
        vg, cdf_v = _gelu(va)
        xhat, rstd = _ln(vg)
        v = (xhat * lg_ref[...] + lb_ref[...]).astype(BF16)
        mask = _tril_mask()
        for h in range(H_A):
            hc = slice(h * CHUNK, (h + 1) * CHUNK)
            wmh = jnp.where(mask, wm_ref[h], 0.0).astype(BF16)
            for cidx in range(tm // CHUNK):
                rc = slice(cidx * CHUNK, (cidx + 1) * CHUNK)
                vb = v[rc, hc]
                mixed = _dot(wmh, vb) + bs_ref[:, h:h + 1]
                da = dmix[rc, hc]
                dpa_ref[rc, hc] = (da * mixed * _gelu_grad(ua[rc, hc], cdf_u[rc, hc])).astype(BF16)
                dmixed = da * u[rc, hc]
                dmb = dmixed.astype(BF16)
                dv_buf[rc, hc] = _dot_tn(wmh, dmb)
                dwm_ref[h] += _dot_nt(dmb, vb)
                db_acc[:, hc] += dmixed
        dv = dv_buf[...]
        dlb_ref[...] += jnp.sum(dv, axis=0, keepdims=True)
        dlg_ref[...] += jnp.sum(dv * xhat, axis=0, keepdims=True)
        dvg = _ln_bwd(dv * lg_ref[...], xhat, rstd)
        dpa_ref[:, D_A:2 * D_A] = (dvg * _gelu_grad(va, cdf_v)).astype(BF16)

        for q in range(H_B):
            cols = slice(q * LANES, (q + 1) * LANES)
            chat, crstd = _ln(c_ref[:, cols])
            z = chat * clg_ref[:, cols] + clb_ref[:, cols]
            sg = _sigmoid(z)
            dz = dmix[:, D_A + q * LANES:D_A + (q + 1) * LANES] * (sg * (1.0 + z * (1.0 - sg)))
            dclb_ref[:, cols] += jnp.sum(dz, axis=0, keepdims=True)
            dclg_ref[:, cols] += jnp.sum(dz * chat, axis=0, keepdims=True)
            dc = _ln_bwd(dz * clg_ref[:, cols], chat, crstd)
            dc_ref[:, cols] = dc
            dcb_ref[:, cols] += jnp.sum(dc, axis=0, keepdims=True)

        @pl.when(i == n_tiles - 1)
        def _():
            for h in range(H_A):
                hc = slice(h * CHUNK, (h + 1) * CHUNK)
                dwm_ref[h] = jnp.where(mask, dwm_ref[h], 0.0)
                dbs_ref[h:h + 1, :] = jnp.sum(db_acc[:, hc].T, axis=0, keepdims=True)

    vec = _full((1, D_A))
    vshape = jax.ShapeDtypeStruct((1, D_A), F32)
    return pl.pallas_call(
        body, name="mixer_bwd_a", grid=(n_tiles,),
        in_specs=[_rows(tm, D), _full((D, D)), _rows(tm, IN_COLS), _rows(tm, D_B), vec, vec,
                  _full((H_A, CHUNK, CHUNK)), _full((CHUNK, H_A)), vec, vec, ANY],
        out_specs=[_rows(tm, 2 * D_A), _rows(tm, D_B), vec, vec, _full((H_A, CHUNK, CHUNK)),
                   _full((H_A, CHUNK)), vec, vec, vec],
        out_shape=[jax.ShapeDtypeStruct((T, 2 * D_A), BF16), jax.ShapeDtypeStruct((T, D_B), F32), vshape, vshape,
                   jax.ShapeDtypeStruct((H_A, CHUNK, CHUNK), F32), jax.ShapeDtypeStruct((H_A, CHUNK), F32),
                   vshape, vshape, vshape],
        scratch_shapes=[pltpu.VMEM((tm, D_A), F32), pltpu.VMEM((CHUNK, D_A), F32)],
        compiler_params=_cparams(32),
    )(dxb, wout, proj, c, lg, lb, wm, bst, clg, clb, dep)


def _mixer_bwd_b(dc, g, proj, cw, dpa, wg_in, x, g1, dres):
    tm = TM_MIX
    n_tiles = T // tm
    hb = tm // HALO
    bn = wg_in.shape[2]

    def body(dc_ref, dch_ref, g_ref, gh_ref, p_ref, cw_ref, dpa_ref, win_ref, x_ref, g1_ref, dres_ref,
             dp_ref, dcw_ref, dx_ref, dxb_ref, dg1_ref, dcbuf, gbuf, dwacc):
        i = pl.program_id(0)

        @pl.when(i == 0)
        def _():
            dwacc[...] = jnp.zeros(dwacc.shape, F32)
            dg1_ref[...] = jnp.zeros(dg1_ref.shape, F32)

        _fill_shift_buffer(dcbuf, 0, dc_ref[...])
        _fill_shift_buffer(dcbuf, tm, jnp.where(i < n_tiles - 1, dch_ref[...], 0.0))
        _fill_shift_buffer(gbuf, 0, jnp.where(i > 0, gh_ref[...], 0.0))
        _fill_shift_buffer(gbuf, HALO, g_ref[...])
        _build_shifts(dcbuf)
        _build_shifts(gbuf)
        dp_ref[:, 0:2 * D_A] = dpa_ref[...]
        for q in range(H_B):
            cols = slice(q * LANES, (q + 1) * LANES)

            def row_block(rb, carry, q=q, cols=cols):
                base = pl.multiple_of(rb * RB, RB)
                rows = pl.ds(base, RB)
                dg = _conv_taps(dcbuf, cw_ref, q, base, CONV_W - 1, -1)
                val = p_ref[rows, cols]
                sg = _sigmoid(p_ref[rows, D_B + q * LANES:D_B + (q + 1) * LANES])
                dp_ref[rows, 2 * D_A + q * LANES:2 * D_A + (q + 1) * LANES] = (dg * sg).astype(BF16)
                dp_ref[rows, 2 * D_A + D_B + q * LANES:2 * D_A + D_B + (q + 1) * LANES] = (
                    dg * val * sg * (1.0 - sg)).astype(BF16)
                dcv = dcbuf[0, q, rows, :]
                parts = []
                for k in range(CONV_W):
                    prod = dcv * _shifted(gbuf, q, base, HALO - (CONV_W - 1) + k)
                    parts.append(jnp.sum(prod.reshape(RB // 8, 8, LANES), axis=0))
                for k in range(CONV_W):
                    dwacc[k * 8:(k + 1) * 8, cols] += parts[k]
                return carry

            lax.fori_loop(0, tm // RB, row_block, 0)

        dh = jnp.zeros((tm, D), F32)
        for j in range(N_DEV):
            dh = dh + _dot_nt(dp_ref[:, j * bn:(j + 1) * bn], win_ref[j])
        xv = x_ref[...]
        dxn, dg = _rms_bwd(dh, xv, _rms(xv), g1_ref[...])
        dx = dres_ref[...] + dxn
        dx_ref[...] = dx
        dxb_ref[...] = dx.astype(BF16)
        dg1_ref[...] += dg

        @pl.when(i == n_tiles - 1)
        def _():
            for k in range(CONV_W):
                dcw_ref[k:k + 1, :] = jnp.sum(dwacc[k * 8:(k + 1) * 8, :], axis=0, keepdims=True)

    return pl.pallas_call(
        body, name="mixer_bwd_b", grid=(n_tiles,),
        in_specs=[_rows(tm, D_B),
                  pl.BlockSpec((HALO, D_B), lambda i: (jnp.minimum((i + 1) * hb, T // HALO - 1), 0)),
                  _rows(tm, D_B),
                  pl.BlockSpec((HALO, D_B), lambda i: (jnp.maximum(i * hb - 1, 0), 0)),
                  pl.BlockSpec((tm, 2 * D_B), lambda i: (i, 1)),
                  _full((CONV_W, D_B)), _rows(tm, 2 * D_A),
                  pl.BlockSpec(wg_in.shape, lambda i: (0, 0, 0), pipeline_mode=pl.Buffered(1)),
                  _rows(tm, D), _full((1, D)), _rows(tm, D)],
        out_specs=[_rows(tm, IN_COLS), _full((CONV_W, D_B)), _rows(tm, D), _rows(tm, D), _full((1, D))],
        out_shape=[jax.ShapeDtypeStruct((T, IN_COLS), BF16), jax.ShapeDtypeStruct((CONV_W, D_B), F32),
                   jax.ShapeDtypeStruct((T, D), F32), jax.ShapeDtypeStruct((T, D), BF16),
                   jax.ShapeDtypeStruct((1, D), F32)],
        scratch_shapes=[pltpu.VMEM((8, H_B, tm + HALO, LANES), F32), pltpu.VMEM((8, H_B, HALO + tm, LANES), F32),
                        pltpu.VMEM((CONV_W * 8, D_B), F32)],
        compiler_params=_cparams(40),
    )(dc, dc, g, g, proj, cw, dpa, wg_in, x, g1, dres)


def _wgrad(name, a, g, bn, square_a=False):
    k = a.shape[1]
    n = g.shape[1]
    tk = min(k, 1024)
    tn = min(n, max(bn, 1024))
    nsub = tn // bn
    tt = TT_WGRAD
    nt = T // tt

    def body(a_ref, g_ref, o_ref, ob_ref):
        t = pl.program_id(2)

        @pl.when(t == 0)
        def _():
            o_ref[...] = jnp.zeros(o_ref.shape, F32)

        av = a_ref[...]
        if square_a:
            av = av * av
        for s in range(nsub):
            o_ref[s] += _dot_tn(av, g_ref[:, s * bn:(s + 1) * bn])

        @pl.when(t == nt - 1)
        def _():
            ob_ref[...] = o_ref[...].astype(BF16)

    ospec = pl.BlockSpec((nsub, tk, bn), lambda ki, ni, ti: (ni, ki, 0))
    return pl.pallas_call(
        body, name=name, grid=(k // tk, n // tn, nt),
        in_specs=[pl.BlockSpec((tt, tk), lambda ki, ni, ti: (ti, ki)),
                  pl.BlockSpec((tt, tn), lambda ki, ni, ti: (ti, ni))],
        out_specs=[ospec, ospec],
        out_shape=[jax.ShapeDtypeStruct((n // bn, k, bn), F32), jax.ShapeDtypeStruct((n // bn, k, bn), BF16)],
        compiler_params=_cparams(40, ("arbitrary", "arbitrary", "arbitrary")),
    )(a, g)


def _coords():
    return lax.axis_index("x"), lax.axis_index("y"), lax.axis_index("c")


def _flat(x, y, c):
    return 4 * x + 2 * y + c


def _peer(k):
    x, y, c = _coords()
    return (x ^ ((k >> 2) & 1), y ^ ((k >> 1) & 1), c ^ (k & 1))


def _hbm(a):
    return pltpu.with_memory_space_constraint(a, pltpu.HBM)


def _hbm_like(a):
    return pltpu.HBM(a.shape, a.dtype)


def _peer_sems():
    return pltpu.SemaphoreType.DMA((N_DEV - 1,))


def _place_own(shards):
    n = len(shards)
    shapes = [(s.shape if l is None else s.shape[1:]) for s, l, _ in shards]

    def body(*refs):
        ins, outs = refs[:n], refs[n:2 * n]
        stage_in, stage_out = refs[2 * n:3 * n], refs[3 * n:4 * n]
        in_sems, out_sems = refs[4 * n], refs[4 * n + 1]
        me = _flat(*_coords())
        srcs = [ins[a] if shards[a][1] is None else ins[a].at[shards[a][1]] for a in range(n)]
        loads = [pltpu.make_async_copy(srcs[a], stage_in[a], in_sems.at[a]) for a in range(n)]
        stores = [pltpu.make_async_copy(stage_out[a], outs[a].at[me], out_sems.at[a]) for a in range(n)]
        for cp in loads:
            cp.start()
        for a in range(n):
            loads[a].wait()
            stage_out[a][...] = stage_in[a][...].astype(stage_out[a].dtype)
            stores[a].start()
        for cp in stores:
            cp.wait()

    return pl.pallas_call(
        body, name="place_own", in_specs=[ANY] * n, out_specs=[ANY] * n,
        out_shape=[jax.ShapeDtypeStruct((N_DEV,) + shapes[a], shards[a][2]) for a in range(n)],
        scratch_shapes=[*[pltpu.VMEM(shapes[a], shards[a][0].dtype) for a in range(n)],
                        *[pltpu.VMEM(shapes[a], shards[a][2]) for a in range(n)],
                        pltpu.SemaphoreType.DMA((n,)), pltpu.SemaphoreType.DMA((n,))],
        compiler_params=pltpu.CompilerParams(vmem_limit_bytes=40 * MIB),
    )(*[s for s, _, _ in shards])


SIBLING = 1
CHIP_PEERS = (2, 4, 6)
FIRST_PEERS = (SIBLING,) + CHIP_PEERS


def _gather_start(lands):
    n = len(lands)

    def body(*refs):
        lnd, send, recv = refs[:n], refs[n:2 * n], refs[2 * n:3 * n]
        me = _flat(*_coords())
        for a in range(n):
            for j, k in enumerate(FIRST_PEERS):
                pltpu.make_async_remote_copy(
                    src_ref=lnd[a].at[me], dst_ref=lnd[a].at[me], send_sem=send[a].at[j],
                    recv_sem=recv[a].at[j], device_id=_peer(k), device_id_type=MESH_ID).start()

    sems = pltpu.SemaphoreType.DMA((len(FIRST_PEERS),))
    outs = pl.pallas_call(
        body, name="gather_start",
        out_shape=(*[sems] * (2 * n), *[_hbm_like(l) for l in lands]),
        in_specs=[HBM] * n, out_specs=(*[SEM] * (2 * n), *[HBM] * n),
        input_output_aliases={i: 2 * n + i for i in range(n)},
        compiler_params=pltpu.CompilerParams(has_side_effects=EFFECT),
    )(*[_hbm(l) for l in lands])
    return outs[:n], outs[n:2 * n], outs[2 * n:]


def _gather_pass_on(name, land, send, recv, after):
    def body(l_ref, send_ref, recv_ref, after_ref, send2, recv2, l_out):
        sibling = _peer(SIBLING)
        for j, k in enumerate(CHIP_PEERS):
            cp = pltpu.make_async_remote_copy(
                src_ref=l_ref.at[0], dst_ref=l_ref.at[0], send_sem=send_ref.at[1 + j], recv_sem=recv_ref.at[1 + j],
                device_id=_peer(k), device_id_type=MESH_ID)
            cp.wait_send()
            cp.wait_recv()
            blk = _flat(*_peer(k))
            pltpu.make_async_remote_copy(
                src_ref=l_ref.at[blk], dst_ref=l_ref.at[blk], send_sem=send2.at[j], recv_sem=recv2.at[j],
                device_id=sibling, device_id_type=MESH_ID).start()

    sems = pltpu.SemaphoreType.DMA((len(CHIP_PEERS),))
    return pl.pallas_call(
        body, name=name, out_shape=(sems, sems, _hbm_like(land)),
        in_specs=(HBM, SEM, SEM, ANY), out_specs=(SEM, SEM, HBM), input_output_aliases={0: 2},
        compiler_params=pltpu.CompilerParams(has_side_effects=EFFECT),
    )(land, send, recv, after)


def _gather_wait(name, land, send, recv, send2, recv2, after):
    def body(l_ref, send_ref, recv_ref, send2_ref, recv2_ref, after_ref, l_out):
        sibling = _peer(SIBLING)
        own = pltpu.make_async_remote_copy(
            src_ref=l_ref.at[0], dst_ref=l_ref.at[0], send_sem=send_ref.at[0], recv_sem=recv_ref.at[0],
            device_id=sibling, device_id_type=MESH_ID)
        own.wait_send()
        own.wait_recv()
        for j in range(len(CHIP_PEERS)):
            cp = pltpu.make_async_remote_copy(
                src_ref=l_ref.at[0], dst_ref=l_ref.at[0], send_sem=send2_ref.at[j], recv_sem=recv2_ref.at[j],
                device_id=sibling, device_id_type=MESH_ID)
            cp.wait_send()
            cp.wait_recv()

    return pl.pallas_call(
        body, name=name, out_shape=_hbm_like(land),
        in_specs=(HBM, SEM, SEM, SEM, SEM, ANY), out_specs=HBM, input_output_aliases={0: 0},
        compiler_params=pltpu.CompilerParams(has_side_effects=EFFECT),
    )(land, send, recv, send2, recv2, after)


def _scatter_start(name, grads, lands, layer):
    n = len(grads)

    def body(*refs):
        g, lnd, send, recv = refs[:n], refs[n:2 * n], refs[2 * n:3 * n], refs[3 * n:4 * n]
        for a in range(n):
            for k in range(1, N_DEV):
                to = _peer(k)
                pltpu.make_async_remote_copy(
                    src_ref=g[a].at[_flat(*to)], dst_ref=lnd[a].at[layer, k - 1], send_sem=send[a].at[k - 1],
                    recv_sem=recv[a].at[k - 1], device_id=to, device_id_type=MESH_ID).start()

    outs = pl.pallas_call(
        body, name=name,
        out_shape=(*[_peer_sems()] * (2 * n), *[_hbm_like(g) for g in grads], *[_hbm_like(l) for l in lands]),
        in_specs=[HBM] * (2 * n), out_specs=(*[SEM] * (2 * n), *[HBM] * (2 * n)),
        input_output_aliases={i: 2 * n + i for i in range(2 * n)},
        compiler_params=pltpu.CompilerParams(has_side_effects=EFFECT),
    )(*[_hbm(g) for g in grads], *[_hbm(l) for l in lands])
    return outs[:n], outs[n:2 * n], outs[2 * n:3 * n], outs[3 * n:]


def _scatter_wait(name, grads, lands, sends, recvs, after):
    n, nw = len(grads), len(lands)

    def body(*refs):
        g, lnd = refs[:n], refs[n:n + nw]
        send, recv = refs[n + nw:2 * n + nw], refs[2 * n + nw:3 * n + nw]
        for a in range(n):
            for k in range(1, N_DEV):
                cp = pltpu.make_async_remote_copy(
                    src_ref=g[a].at[0], dst_ref=lnd[a // DEPTH].at[a % DEPTH, 0],
                    send_sem=send[a].at[k - 1], recv_sem=recv[a].at[k - 1],
                    device_id=_peer(k), device_id_type=MESH_ID)
                cp.wait_send()
                cp.wait_recv()

    outs = pl.pallas_call(
        body, name=name,
        out_shape=(*[_hbm_like(g) for g in grads], *[_hbm_like(l) for l in lands]),
        in_specs=(*[HBM] * (n + nw), *[SEM] * (2 * n), ANY), out_specs=[HBM] * (n + nw),
        input_output_aliases={i: i for i in range(n + nw)},
        compiler_params=pltpu.CompilerParams(has_side_effects=EFFECT),
    )(*grads, *lands, *sends, *recvs, after)
    return outs[n:]


def _own_blocks(grads_f32):
    nw = len(grads_f32)
    flat32 = [g for w in grads_f32 for g in w]
    n = len(flat32)

    def body(*refs):
        g32, own, stage = refs[:n], refs[n:n + nw], refs[n + nw:2 * n + nw]
        in_sems, out_sems = refs[2 * n + nw], refs[2 * n + nw + 1]
        me = _flat(*_coords())
        loads = [pltpu.make_async_copy(g32[a].at[me], stage[a], in_sems.at[a]) for a in range(n)]
        stores = [pltpu.make_async_copy(stage[a], own[a // DEPTH].at[a % DEPTH], out_sems.at[a]) for a in range(n)]
        for cp in loads:
            cp.start()
        for a in range(n):
            loads[a].wait()
            stores[a].start()
        for cp in stores:
            cp.wait()

    return pl.pallas_call(
        body, name="own_blocks", in_specs=[ANY] * n, out_specs=[ANY] * nw,
        out_shape=[jax.ShapeDtypeStruct((DEPTH,) + w[0].shape[1:], F32) for w in grads_f32],
        scratch_shapes=[*[pltpu.VMEM(g.shape[1:], F32) for g in flat32],
                        pltpu.SemaphoreType.DMA((n,)), pltpu.SemaphoreType.DMA((n,))],
        compiler_params=pltpu.CompilerParams(vmem_limit_bytes=24 * MIB),
    )(*flat32)


def _all_reduce_small(part, dep):
    rows = part.shape[0]
    br = rows // N_DEV
    assert br * N_DEV == rows and br % 8 == 0

    def body(p_ref, dep_ref, o_ref, slots, send1, recv1, send2, recv2):
        me = _flat(*_coords())

        def block(ref, d):
            return ref.at[pl.ds(pl.multiple_of(d * br, 8), br), :]

        slots[me] = p_ref[pl.ds(pl.multiple_of(me * br, 8), br), :]
        scatter, gather = [], []
        for k in range(1, N_DEV):
            to = _peer(k)
            scatter.append(pltpu.make_async_remote_copy(
                src_ref=block(p_ref, _flat(*to)), dst_ref=slots.at[me],
                send_sem=send1.at[k - 1], recv_sem=recv1.at[k - 1], device_id=to, device_id_type=MESH_ID))
            gather.append(pltpu.make_async_remote_copy(
                src_ref=block(o_ref, me), dst_ref=block(o_ref, me),
                send_sem=send2.at[k - 1], recv_sem=recv2.at[k - 1], device_id=to, device_id_type=MESH_ID))
        for cp in scatter:
            cp.start()
        for cp in scatter:
            cp.wait()
        acc = slots[0]
        for d in range(1, N_DEV):
            acc = acc + slots[d]
        o_ref[pl.ds(pl.multiple_of(me * br, 8), br), :] = acc
        for cp in gather:
            cp.start()
        for cp in gather:
            cp.wait()

    return pl.pallas_call(
        body, name="all_reduce_small",
        in_specs=[pl.BlockSpec(memory_space=pltpu.VMEM), ANY], out_specs=pl.BlockSpec(memory_space=pltpu.VMEM),
        out_shape=jax.ShapeDtypeStruct(part.shape, F32),
        scratch_shapes=[pltpu.VMEM((N_DEV, br, LANES), F32)] + [pltpu.SemaphoreType.DMA((N_DEV - 1,))] * 4,
        compiler_params=pltpu.CompilerParams(vmem_limit_bytes=24 * MIB),
    )(part, dep)


def _adam_math(w, g, m, v):
    m = ADAM_B1 * m + (1.0 - ADAM_B1) * g
    v = ADAM_B2 * v + (1.0 - ADAM_B2) * (g * g)
    m_hat = m / (1.0 - ADAM_B1 ** ADAM_STEP)
    v_hat = v / (1.0 - ADAM_B2 ** ADAM_STEP)
    delta = -ADAM_LR * (m_hat / (jnp.sqrt(v_hat) + ADAM_EPS) + ADAM_WD * w)
    return delta, m, v


def _adam_sharded(name, own, recv, w, m, v):
    _, r, c = w.shape
    tr = min(r, 256)

    def body(own_ref, recv_ref, w_ref, m_ref, v_ref, g_ref, d_ref, nm_ref, nv_ref):
        g = own_ref[...]
        for k in range(N_DEV - 1):
            g = g + recv_ref[k].astype(F32)
        delta, nm, nv = _adam_math(w_ref[...], g, m_ref[...], v_ref[...])
        g_ref[...] = g
        d_ref[...] = delta
        nm_ref[...] = nm
        nv_ref[...] = nv

    blk = pl.BlockSpec((None, tr, c), lambda l, i: (l, i, 0))
    shp = jax.ShapeDtypeStruct(w.shape, F32)
    return pl.pallas_call(
        body, name=name, grid=(DEPTH, r // tr),
        in_specs=[blk, pl.BlockSpec((None, N_DEV - 1, tr, c), lambda l, i: (l, 0, i, 0)), blk, blk, blk],
        out_specs=[blk] * 4, out_shape=[shp] * 4,
        compiler_params=_cparams(32, ("arbitrary", "arbitrary")),
    )(own, recv, w, m, v)


def _adam_small(g, w, m, v):
    rows = g.shape[0]

    def body(g_ref, w_ref, m_ref, v_ref, d_ref, nm_ref, nv_ref):
        delta, nm, nv = _adam_math(w_ref[...], g_ref[...], m_ref[...], v_ref[...])
        d_ref[...] = delta
        nm_ref[...] = nm
        nv_ref[...] = nv

    spec = _full((rows, LANES))
    shp = jax.ShapeDtypeStruct((rows, LANES), F32)
    return pl.pallas_call(
        body, name="adam_small", grid=(1,),
        in_specs=[spec] * 4, out_specs=[spec] * 3, out_shape=[shp] * 3,
        compiler_params=_cparams(24),
    )(g, w, m, v)


SMALL = ["norm1_g", "sgu_ln_g", "sgu_ln_b", "sgu_w", "sgu_b", "conv_b", "conv_ln_g", "conv_ln_b", "norm2_g",
         "final_g"]


def _pack(arrays):
    flat = jnp.concatenate([a.reshape(-1) for a in arrays])
    pad = (-flat.shape[0]) % (N_DEV * 8 * LANES)
    return jnp.pad(flat, (0, pad)).reshape(-1, LANES)


def _unpack(packed, shapes):
    flat = packed.reshape(-1)
    out, off = [], 0
    for s in shapes:
        size = 1
        for d in s:
            size *= d
        out.append(flat[off:off + size].reshape(s))
        off += size
    return out


def kernel(x, norm1_g, w_in, sgu_ln_g, sgu_ln_b, sgu_w, sgu_b, conv_w, conv_b, conv_ln_g, conv_ln_b, w_out, norm2_g, w_ff1, w_ff2, final_g, loss_target, m_norm1_g, m_w_in, m_sgu_ln_g, m_sgu_ln_b, m_sgu_w, m_sgu_b, m_conv_w, m_conv_b, m_conv_ln_g, m_conv_ln_b, m_w_out, m_norm2_g, m_w_ff1, m_w_ff2, m_final_g, v_norm1_g, v_w_in, v_sgu_ln_g, v_sgu_ln_b, v_sgu_w, v_sgu_b, v_conv_w, v_conv_b, v_conv_ln_g, v_conv_ln_b, v_w_out, v_norm2_g, v_w_ff1, v_w_ff2, v_final_g):
    x2d = x.reshape(T, D)
    tgt = loss_target.reshape(T, D)
    cw_shard = conv_w.reshape(CONV_W, LANES)

    gnames, shards = [], []
    for l in range(DEPTH):
        for k, w in (("w_in", w_in), ("w_out", w_out), ("w_ff1", w_ff1), ("w_ff2", w_ff2)):
            gnames.append(f"{k}{l}")
            shards.append((w, l, BF16))
        if l == 0:
            gnames.insert(1, "conv_w")
            shards.insert(1, (cw_shard, None, F32))
    sends, recvs, lands = _gather_start(_place_own(shards))
    gidx = {k: i for i, k in enumerate(gnames)}

    passed = {}

    def pass_on(k, after):
        i = gidx[k]
        passed[k] = _gather_pass_on("gather_pass_on_" + k, lands[i], sends[i], recvs[i], after)

    def gathered(k, after):
        i = gidx[k]
        send2, recv2, land = passed[k]
        return _gather_wait("gather_wait_" + k, land, sends[i], recvs[i], send2, recv2, after)

    saved = []
    xl = x2d
    cw_full = None
    for l in range(DEPTH):
        g1 = norm1_g[l].reshape(1, D)
        g2 = norm2_g[l].reshape(1, D)
        lg, lb = sgu_ln_g[l].reshape(1, D_A), sgu_ln_b[l].reshape(1, D_A)
        bst = sgu_b[l].T
        cb = conv_b[l].reshape(1, D_B)
        clg, clb = conv_ln_g[l].reshape(1, D_B), conv_ln_b[l].reshape(1, D_B)
        if l == 0:
            pass_on("w_in0", xl)
        wg_in = gathered(f"w_in{l}", xl)
        if l == 0:
            pass_on("conv_w", xl)
        pass_on(f"w_out{l}", xl)
        h, proj = _fwd_in(xl, g1, wg_in)
        if l == 0:
            cw_full = gathered("conv_w", proj).reshape(N_DEV, DEPTH, CONV_W, D_B // N_DEV).transpose(
                1, 2, 0, 3).reshape(DEPTH, CONV_W, D_B)
        wout = gathered(f"w_out{l}", proj).reshape(D, D)
        if l > 0:
            pass_on(f"w_ff1{l}", proj)
        mix, c, g, x1, h2 = _mixer_fwd(proj, lg, lb, sgu_w[l], bst, cw_full[l], cb, clg, clb, wout, xl, g2)
        if l == 0:
            pass_on(f"w_ff1{l}", h2)
        else:
            pass_on(f"w_ff2{l}", h2)
        wg_ff1 = gathered(f"w_ff1{l}", h2)
        r = _fwd_ff1(h2, wg_ff1)
        if l == 0:
            pass_on(f"w_ff2{l}", r)
        if l + 1 < DEPTH:
            pass_on(f"w_in{l + 1}", r)
        w2 = gathered(f"w_ff2{l}", r).reshape(D_FF, D)
        saved.append(dict(x=xl, h=h, proj=proj, mix=mix, c=c, g=g, x1=x1, h2=h2, r=r, wg_in=wg_in, wout=wout,
                          wg_ff1=wg_ff1, w2=w2, g1=g1, g2=g2, lg=lg, lb=lb, bst=bst, clg=clg, clb=clb))
        if l + 1 < DEPTH:
            xl = _fwd_ff2(r, w2, x1)
        else:
            dx, dxb, loss_part, d_final_g = _fwd_ff2_loss(r, w2, x1, final_g.reshape(1, D), tgt)

    names = ["w_in", "w_out", "w_ff1", "w_ff2"]
    block = dict(w_in=(D, IN_COLS // N_DEV), w_out=(D // N_DEV, D), w_ff1=(D, D_FF // N_DEV), w_ff2=(D_FF // N_DEV, D))
    land = {k: lax.empty((DEPTH, N_DEV - 1) + block[k], BF16) for k in names}
    big32 = {k: [None] * DEPTH for k in names}
    big16 = {k: [None] * DEPTH for k in names}
    ssend = {k: [None] * DEPTH for k in names}
    srecv = {k: [None] * DEPTH for k in names}

    def send_grads(l, grads):
        ks = list(grads)
        for k in ks:
            big32[k][l] = grads[k][0].reshape((N_DEV,) + block[k])
        sends, recvs, g16s, lands = _scatter_start(
            "scatter_start_" + "_".join(ks) + str(l), [grads[k][1].reshape((N_DEV,) + block[k]) for k in ks],
            [land[k] for k in ks], l)
        for i, k in enumerate(ks):
            ssend[k][l], srecv[k][l], big16[k][l], land[k] = sends[i], recvs[i], g16s[i], lands[i]

    small = {}
    d_conv_w = [None] * DEPTH
    for l in reversed(range(DEPTH)):
        s = saved[l]
        send_grads(l, dict(w_ff2=_wgrad("wgrad_ff2", s["r"], dxb, D, square_a=True)))
        df1, dx1, dx1b, dg2 = _bwd_mlp(dxb, dx, s["w2"], s["r"], s["wg_ff1"], s["x1"], s["g2"], big16["w_ff2"][l])
        send_grads(l, dict(w_ff1=_wgrad("wgrad_ff1", s["h2"], df1, D_FF // N_DEV),
                           w_out=_wgrad("wgrad_out", s["mix"], dx1b, D)))
        dpa, dc, dlg, dlb, dwm, dbs, dcb, dclg, dclb = _mixer_bwd_a(
            dx1b, s["wout"], s["proj"], s["c"], s["lg"], s["lb"], sgu_w[l], s["bst"], s["clg"], s["clb"],
            big16["w_out"][l])
        dproj, d_conv_w[l], dx, dxb, dg1 = _mixer_bwd_b(dc, s["g"], s["proj"], cw_full[l], dpa, s["wg_in"], s["x"],
                                                         s["g1"], dx1)
        send_grads(l, dict(w_in=_wgrad("wgrad_in", s["h"], dproj, IN_COLS // N_DEV)))
        small[l] = dict(norm1_g=dg1, sgu_ln_g=dlg, sgu_ln_b=dlb, sgu_w=dwm, sgu_b=dbs, conv_b=dcb, conv_ln_g=dclg,
                        conv_ln_b=dclb, norm2_g=dg2)
    grad_x = dx.reshape(1, T, D)

    own = dict(zip(names, _own_blocks([big32[k] for k in names])))
    wmv = dict(w_in=(w_in, m_w_in, v_w_in), w_out=(w_out, m_w_out, v_w_out), w_ff1=(w_ff1, m_w_ff1, v_w_ff1),
               w_ff2=(w_ff2, m_w_ff2, v_w_ff2))
    res = {}

    def update(tag, ks, after):
        recv = _scatter_wait("scatter_wait_" + tag, [g for k in ks for g in big16[k]], [land[k] for k in ks],
                             [q for k in ks for q in ssend[k]], [q for k in ks for q in srecv[k]], after)
        for k, rk in zip(ks, recv):
            res[k] = _adam_sharded("adam_" + k, own[k], rk, *wmv[k])

    update("early", ["w_ff2", "w_ff1", "w_out"], big16["w_in"][0])

    rep = dict(norm1_g=(norm1_g, m_norm1_g, v_norm1_g), sgu_ln_g=(sgu_ln_g, m_sgu_ln_g, v_sgu_ln_g),
               sgu_ln_b=(sgu_ln_b, m_sgu_ln_b, v_sgu_ln_b), sgu_w=(sgu_w, m_sgu_w, v_sgu_w),
               sgu_b=(sgu_b, m_sgu_b, v_sgu_b), conv_b=(conv_b, m_conv_b, v_conv_b),
               conv_ln_g=(conv_ln_g, m_conv_ln_g, v_conv_ln_g), conv_ln_b=(conv_ln_b, m_conv_ln_b, v_conv_ln_b),
               norm2_g=(norm2_g, m_norm2_g, v_norm2_g), final_g=(final_g, m_final_g, v_final_g))
    parts = []
    for k in SMALL:
        if k == "final_g":
            parts.append(d_final_g.reshape(rep[k][0].shape))
        else:
            parts.append(jnp.stack([small[l][k].reshape(rep[k][0].shape[1:]) for l in range(DEPTH)]))
    parts.append(jnp.stack(d_conv_w))
    parts.append(loss_part[0, 0:1])
    shapes = [p.shape for p in parts]
    packed_sum = _all_reduce_small(_pack(parts), res["w_out"][0])
    update("late", ["w_in"], packed_sum)
    summed = _unpack(packed_sum, shapes)
    loss = summed[-1][0]
    me = _flat(*_coords())
    g_conv_w = lax.dynamic_slice_in_dim(summed[-2], me * (D_B // N_DEV), D_B // N_DEV, axis=2)
    g_small = summed[:-2] + [g_conv_w]
    keys = SMALL + ["conv_w"]
    rep["conv_w"] = (conv_w, m_conv_w, v_conv_w)
    shapes = [a.shape for a in g_small]
    packed = [_pack(g_small)] + [_pack([rep[k][i] for k in keys]) for i in range(3)]
    d_s, nm_s, nv_s = (_unpack(a, shapes) for a in _adam_small(*packed))
    for i, k in enumerate(keys):
        res[k] = (g_small[i], d_s[i], nm_s[i], nv_s[i])

    order = ["norm1_g", "w_in", "sgu_ln_g", "sgu_ln_b", "sgu_w", "sgu_b", "conv_w", "conv_b", "conv_ln_g",
             "conv_ln_b", "w_out", "norm2_g", "w_ff1", "w_ff2", "final_g"]
    return (loss, grad_x, *[res[k][0] for k in order], *[res[k][1] for k in order],
            *[res[k][2] for k in order], *[res[k][3] for k in order])
```

```python
import functools

import jax
import jax.numpy as jnp
from jax import lax
from jax.experimental import pallas as pl
from jax.experimental.pallas import tpu as pltpu

F32 = jnp.float32
BF16 = jnp.bfloat16

N_DEV = 8
DEPTH = 2
T = 4096
D = 1024
D_A = 512
D_B = 512
CHUNK = 128
H_A = 4
H_B = 4
CONV_W = 31
HALO = 32
D_FF = 4096
IN_COLS = 2048
EPS = 1e-6

ADAM_LR = 0.001
ADAM_B1 = 0.9
ADAM_B2 = 0.999
ADAM_EPS = 1e-08
ADAM_WD = 0.01
ADAM_STEP = 10

TM = 512
TM_FWD = 1024
TT_WGRAD = 4096
TM_MIX = 256
TM_SGU = 512
RB = 64
CONV_CHAINS = 2
LANES = 128
MIB = 1024 * 1024
SCOPED_VMEM_MIB = 60

SQRT_HALF = 0.7071067811865476
INV_SQRT_2PI = 0.3989422804014327

MESH_ID = pl.DeviceIdType.MESH
ANY = pl.BlockSpec(memory_space=pl.ANY)
HBM = pl.BlockSpec(memory_space=pltpu.HBM)
SEM = pl.BlockSpec(memory_space=pltpu.SEMAPHORE)
EFFECT = pltpu.SideEffectType.DATAFLOW_SIDE_EFFECTING


def _cparams(vmem_mib, sem=("arbitrary",)):
    assert vmem_mib <= SCOPED_VMEM_MIB
    return pltpu.CompilerParams(dimension_semantics=sem, vmem_limit_bytes=SCOPED_VMEM_MIB * MIB)


def _full(shape):
    return pl.BlockSpec(shape, lambda *_: (0,) * len(shape))


def _rows(tm, cols):
    return pl.BlockSpec((tm, cols), lambda i: (i, 0))


def _gelu(x):
    cdf = 0.5 * (1.0 + lax.erf(x * SQRT_HALF))
    return x * cdf, cdf


def _gelu_grad(x, cdf):
    return cdf + x * (INV_SQRT_2PI * jnp.exp(-0.5 * x * x))


def _sigmoid(x):
    return 1.0 / (1.0 + jnp.exp(-x))


def _ln(x):
    mu = jnp.mean(x, axis=-1, keepdims=True)
    xc = x - mu
    rstd = lax.rsqrt(jnp.mean(xc * xc, axis=-1, keepdims=True) + EPS)
    return xc * rstd, rstd


def _ln_bwd(dyh, xhat, rstd):
    return rstd * (dyh - jnp.mean(dyh, axis=-1, keepdims=True) - xhat * jnp.mean(dyh * xhat, axis=-1, keepdims=True))


def _rms(x):
    return lax.rsqrt(jnp.mean(x * x, axis=-1, keepdims=True) + EPS)


def _rms_bwd(dh, x, r, g):
    n = x * r
    dn = dh * g
    dx = r * (dn - n * jnp.mean(dn * n, axis=-1, keepdims=True))
    return dx, jnp.sum(dh * n, axis=0, keepdims=True)


def _dot(a, b):
    return jnp.dot(a, b, preferred_element_type=F32)


def _dot_nt(a, b):
    return lax.dot_general(a, b, (((1,), (1,)), ((), ())), preferred_element_type=F32)


def _dot_tn(a, b):
    return lax.dot_general(a, b, (((0,), (0,)), ((), ())), preferred_element_type=F32)


def _tril_mask():
    r = lax.broadcasted_iota(jnp.int32, (CHUNK, CHUNK), 0)
    c = lax.broadcasted_iota(jnp.int32, (CHUNK, CHUNK), 1)
    return r >= c


def _fwd_in(x, g1, wg):
    bn = wg.shape[2]

    def body(x_ref, g_ref, w_ref, h_ref, p_ref):
        xv = x_ref[...]
        h = (xv * _rms(xv) * g_ref[...]).astype(BF16)
        h_ref[...] = h
        for j in range(N_DEV):
            p_ref[:, j * bn:(j + 1) * bn] = _dot(h, w_ref[j])

    return pl.pallas_call(
        body, name="fwd_in", grid=(T // TM_FWD,),
        in_specs=[_rows(TM_FWD, D), _full((1, D)), _full(wg.shape)],
        out_specs=[_rows(TM_FWD, D), _rows(TM_FWD, IN_COLS)],
        out_shape=[jax.ShapeDtypeStruct((T, D), BF16), jax.ShapeDtypeStruct((T, IN_COLS), F32)],
        compiler_params=_cparams(32),
    )(x, g1, wg)


def _fill_shift_buffer(sh_ref, row0, value):
    for q in range(sh_ref.shape[1]):
        sh_ref[0, q, row0:row0 + value.shape[0], :] = value[:, q * LANES:(q + 1) * LANES]


def _build_shifts(sh_ref):
    rows = sh_ref.shape[2]
    for p in range(1, 8):
        for q in range(sh_ref.shape[1]):
            sh_ref[p, q, 0:rows - 8, :] = sh_ref[0, q, p:p + rows - 8, :]


def _shifted(sh_ref, q, base, off):
    start = base + (off - off % 8)
    if not isinstance(start, int):
        start = pl.multiple_of(start, 8)
    return sh_ref[off % 8, q, pl.ds(start, RB), :]


def _conv_taps(sh_ref, w_ref, q, base, first_tap_row, step):
    cols = slice(q * LANES, (q + 1) * LANES)
    acc = [jnp.zeros((RB, LANES), F32) for _ in range(CONV_CHAINS)]
    for k in range(CONV_W):
        term = _shifted(sh_ref, q, base, first_tap_row + step * k) * w_ref[k:k + 1, cols]
        acc[k % CONV_CHAINS] = acc[k % CONV_CHAINS] + term
    return functools.reduce(lambda a, b: a + b, acc)


def _mixer_fwd(proj, lg, lb, wm, bst, cw, cb, clg, clb, wout, x, g2):
    tm = TM_MIX
    hb = tm // HALO

    def body(p_ref, ph_ref, lg_ref, lb_ref, wm_ref, bs_ref, cw_ref, cb_ref, clg_ref, clb_ref, wo_ref, x_ref, g2_ref,
             mix_ref, c_ref, g_ref, x1_ref, h2_ref, gbuf):
        i = pl.program_id(0)
        u, _ = _gelu(p_ref[:, 0:D_A])
        vg, _ = _gelu(p_ref[:, D_A:2 * D_A])
        xhat, _ = _ln(vg)
        v = (xhat * lg_ref[...] + lb_ref[...]).astype(BF16)
        mask = _tril_mask()
        for h in range(H_A):
            hc = slice(h * CHUNK, (h + 1) * CHUNK)
            wmh = jnp.where(mask, wm_ref[h], 0.0).astype(BF16)
            for c in range(tm // CHUNK):
                rc = slice(c * CHUNK, (c + 1) * CHUNK)
                mixed = _dot(wmh, v[rc, hc]) + bs_ref[:, h:h + 1]
                mix_ref[rc, hc] = (u[rc, hc] * mixed).astype(BF16)
        x1_ref[...] = x_ref[...] + _dot(mix_ref[:, 0:D_A], wo_ref[0:D_A, :])

        g = p_ref[:, 2 * D_A:2 * D_A + D_B] * _sigmoid(p_ref[:, 2 * D_A + D_B:IN_COLS])
        g_ref[...] = g
        gh = ph_ref[:, 0:D_B] * _sigmoid(ph_ref[:, D_B:2 * D_B])
        _fill_shift_buffer(gbuf, 0, jnp.where(i > 0, gh, 0.0))
        _fill_shift_buffer(gbuf, HALO, g)
        _build_shifts(gbuf)
        for q in range(H_B):
            cols = slice(q * LANES, (q + 1) * LANES)
            for rb in range(tm // RB):
                acc = _conv_taps(gbuf, cw_ref, q, rb * RB, HALO - (CONV_W - 1), 1)
                c_ref[rb * RB:(rb + 1) * RB, cols] = acc + cb_ref[:, cols]
        for q in range(H_B):
            cols = slice(q * LANES, (q + 1) * LANES)
            chat, _ = _ln(c_ref[:, cols])
            z = chat * clg_ref[:, cols] + clb_ref[:, cols]
            mix_ref[:, D_A + q * LANES:D_A + (q + 1) * LANES] = (z * _sigmoid(z)).astype(BF16)

        x1 = x1_ref[...] + _dot(mix_ref[:, D_A:D], wo_ref[D_A:D, :])
        x1_ref[...] = x1
        h2_ref[...] = (x1 * _rms(x1) * g2_ref[...]).astype(BF16)

    vec = _full((1, D_A))
    return pl.pallas_call(
        body, name="mixer_fwd", grid=(T // tm,),
        in_specs=[_rows(tm, IN_COLS),
                  pl.BlockSpec((HALO, 2 * D_B), lambda i: (jnp.maximum(i * hb - 1, 0), 1)),
                  vec, vec, _full((H_A, CHUNK, CHUNK)), _full((CHUNK, H_A)),
                  _full((CONV_W, D_B)), vec, vec, vec,
                  pl.BlockSpec((D, D), lambda i: (0, 0), pipeline_mode=pl.Buffered(1)), _rows(tm, D), _full((1, D))],
        out_specs=[_rows(tm, D), _rows(tm, D_B), _rows(tm, D_B), _rows(tm, D), _rows(tm, D)],
        out_shape=[jax.ShapeDtypeStruct((T, D), BF16), jax.ShapeDtypeStruct((T, D_B), F32),
                   jax.ShapeDtypeStruct((T, D_B), F32), jax.ShapeDtypeStruct((T, D), F32),
                   jax.ShapeDtypeStruct((T, D), BF16)],
        scratch_shapes=[pltpu.VMEM((8, H_B, HALO + tm, LANES), F32)],
        compiler_params=_cparams(40),
    )(proj, proj, lg, lb, wm, bst, cw, cb, clg, clb, wout, x, g2)


def _fwd_ff1(h2, wg):
    bn = wg.shape[2]

    def body(h_ref, w_ref, r_ref):
        h = h_ref[...]
        for j in range(N_DEV):
            r_ref[:, j * bn:(j + 1) * bn] = jnp.maximum(_dot(h, w_ref[j]), 0.0).astype(BF16)

    return pl.pallas_call(
        body, name="fwd_ff1", grid=(T // TM_FWD,),
        in_specs=[_rows(TM_FWD, D), _full(wg.shape)],
        out_specs=_rows(TM_FWD, D_FF),
        out_shape=jax.ShapeDtypeStruct((T, D_FF), BF16),
        compiler_params=_cparams(48),
    )(h2, wg)


def _fwd_ff2(r, w2, x1):
    def body(r_ref, w_ref, x_ref, o_ref):
        rv = r_ref[...]
        o_ref[...] = x_ref[...] + _dot(rv * rv, w_ref[...])

    return pl.pallas_call(
        body, name="fwd_ff2", grid=(T // TM_FWD,),
        in_specs=[_rows(TM_FWD, D_FF), _full((D_FF, D)), _rows(TM_FWD, D)],
        out_specs=_rows(TM_FWD, D),
        out_shape=jax.ShapeDtypeStruct((T, D), F32),
        compiler_params=_cparams(48),
    )(r, w2, x1)


def _fwd_ff2_loss(r, w2, x1, gf, tgt):
    def body(r_ref, w_ref, x_ref, g_ref, t_ref, dx_ref, dxb_ref, loss_ref, dg_ref):
        i = pl.program_id(0)

        @pl.when(i == 0)
        def _():
            loss_ref[...] = jnp.zeros(loss_ref.shape, F32)
            dg_ref[...] = jnp.zeros(dg_ref.shape, F32)

        rv = r_ref[...]
        xv = x_ref[...] + _dot(rv * rv, w_ref[...])
        rn = _rms(xv)
        diff = xv * rn * g_ref[...] - t_ref[...]
        loss_ref[...] += 0.5 * jnp.sum(jnp.mean(diff * diff, axis=-1, keepdims=True), axis=0, keepdims=True)
        dx, dg = _rms_bwd(diff * (1.0 / D), xv, rn, g_ref[...])
        dx_ref[...] = dx
        dxb_ref[...] = dx.astype(BF16)
        dg_ref[...] += dg

    return pl.pallas_call(
        body, name="fwd_ff2_loss", grid=(T // TM,),
        in_specs=[_rows(TM, D_FF), pl.BlockSpec((D_FF, D), lambda i: (0, 0), pipeline_mode=pl.Buffered(1)),
                  _rows(TM, D), _full((1, D)), _rows(TM, D)],
        out_specs=[_rows(TM, D), _rows(TM, D), _full((8, LANES)), _full((1, D))],
        out_shape=[jax.ShapeDtypeStruct((T, D), F32), jax.ShapeDtypeStruct((T, D), BF16),
                   jax.ShapeDtypeStruct((8, LANES), F32), jax.ShapeDtypeStruct((1, D), F32)],
        compiler_params=_cparams(40),
    )(r, w2, x1, gf, tgt)


def _bwd_mlp(dxb, dres, w2, r, wg1, x1, g2, dep):
    bn = wg1.shape[2]

    def body(d_ref, dres_ref, w2_ref, r_ref, w1_ref, x_ref, g_ref, dep_ref, df1_ref, dx_ref, dxb_ref, dg_ref):
        i = pl.program_id(0)

        @pl.when(i == 0)
        def _():
            dg_ref[...] = jnp.zeros(dg_ref.shape, F32)

        d = d_ref[...]
        dh = jnp.zeros((TM, D), F32)
        for j in range(N_DEV):
            cols = slice(j * bn, (j + 1) * bn)
            df1 = (2.0 * r_ref[:, cols].astype(F32) * _dot_nt(d, w2_ref[cols, :])).astype(BF16)
            df1_ref[:, cols] = df1
            dh = dh + _dot_nt(df1, w1_ref[j])
        xv = x_ref[...]
        dxn, dg = _rms_bwd(dh, xv, _rms(xv), g_ref[...])
        dx = dres_ref[...] + dxn
        dx_ref[...] = dx
        dxb_ref[...] = dx.astype(BF16)
        dg_ref[...] += dg

    once = dict(pipeline_mode=pl.Buffered(1))
    return pl.pallas_call(
        body, name="bwd_mlp", grid=(T // TM,),
        in_specs=[_rows(TM, D), _rows(TM, D), pl.BlockSpec((D_FF, D), lambda i: (0, 0), **once), _rows(TM, D_FF),
                  pl.BlockSpec(wg1.shape, lambda i: (0, 0, 0), **once), _rows(TM, D), _full((1, D)), ANY],
        out_specs=[_rows(TM, D_FF), _rows(TM, D), _rows(TM, D), _full((1, D))],
        out_shape=[jax.ShapeDtypeStruct((T, D_FF), BF16), jax.ShapeDtypeStruct((T, D), F32),
                   jax.ShapeDtypeStruct((T, D), BF16), jax.ShapeDtypeStruct((1, D), F32)],
        compiler_params=_cparams(56),
    )(dxb, dres, w2, r, wg1, x1, g2, dep)


def _mixer_bwd_a(dxb, wout, proj, c, lg, lb, wm, bst, clg, clb, dep):
    tm = TM_SGU
    n_tiles = T // tm

    def body(dx_ref, wo_ref, p_ref, c_ref, lg_ref, lb_ref, wm_ref, bs_ref, clg_ref, clb_ref, dep_ref,
             dpa_ref, dc_ref, dlg_ref, dlb_ref, dwm_ref, dbs_ref, dcb_ref, dclg_ref, dclb_ref,
             dv_buf, db_acc):
        i = pl.program_id(0)

        @pl.when(i == 0)
        def _():
            for ref in (dlg_ref, dlb_ref, dwm_ref, dbs_ref, dcb_ref, dclg_ref, dclb_ref, db_acc):
                ref[...] = jnp.zeros(ref.shape, F32)

        dmix = _dot_nt(dx_ref[...], wo_ref[...])
        ua = p_ref[:, 0:D_A]
        va = p_ref[:, D_A:2 * D_A]
        u, cdf_u = _gelu(ua)
        vg, cdf_v = _gelu(va)
        xhat, rstd = _ln(vg)
        v = (xhat * lg_ref[...] + lb_ref[...]).astype(BF16)
        mask = _tril_mask()
        for h in range(H_A):
            hc = slice(h * CHUNK, (h + 1) * CHUNK)
            wmh = jnp.where(mask, wm_ref[h], 0.0).astype(BF16)
            for cidx in range(tm // CHUNK):
                rc = slice(cidx * CHUNK, (cidx + 1) * CHUNK)
                vb = v[rc, hc]
                mixed = _dot(wmh, vb) + bs_ref[:, h:h + 1]
                da = dmix[rc, hc]
                dpa_ref[rc, hc] = (da * mixed * _gelu_grad(ua[rc, hc], cdf_u[rc, hc])).astype(BF16)
                dmixed = da * u[rc, hc]
                dmb = dmixed.astype(BF16)
                dv_buf[rc, hc] = _dot_tn(wmh, dmb)
                dwm_ref[h] += _dot_nt(dmb, vb)
                db_acc[:, hc] += dmixed
        dv = dv_buf[...]
        dlb_ref[...] += jnp.sum(dv, axis=0, keepdims=True)
        dlg_ref[...] += jnp.sum(dv * xhat, axis=0, keepdims=True)
        dvg = _ln_bwd(dv * lg_ref[...], xhat, rstd)
        dpa_ref[:, D_A:2 * D_A] = (dvg * _gelu_grad(va, cdf_v)).astype(BF16)

        for q in range(H_B):
            cols = slice(q * LANES, (q + 1) * LANES)
            chat, crstd = _ln(c_ref[:, cols])
            z = chat * clg_ref[:, cols] + clb_ref[:, cols]
            sg = _sigmoid(z)
            dz = dmix[:, D_A + q * LANES:D_A + (q + 1) * LANES] * (sg * (1.0 + z * (1.0 - sg)))
            dclb_ref[:, cols] += jnp.sum(dz, axis=0, keepdims=True)
            dclg_ref[:, cols] += jnp.sum(dz * chat, axis=0, keepdims=True)
            dc = _ln_bwd(dz * clg_ref[:, cols], chat, crstd)
            dc_ref[:, cols] = dc
            dcb_ref[:, cols] += jnp.sum(dc, axis=0, keepdims=True)

        @pl.when(i == n_tiles - 1)
        def _():
            for h in range(H_A):
                hc = slice(h * CHUNK, (h + 1) * CHUNK)
                dwm_ref[h] = jnp.where(mask, dwm_ref[h], 0.0)
                dbs_ref[h:h + 1, :] = jnp.sum(db_acc[:, hc].T, axis=0, keepdims=True)

    vec = _full((1, D_A))
    vshape = jax.ShapeDtypeStruct((1, D_A), F32)
    return pl.pallas_call(
        body, name="mixer_bwd_a", grid=(n_tiles,),
        in_specs=[_rows(tm, D), _full((D, D)), _rows(tm, IN_COLS), _rows(tm, D_B), vec, vec,
                  _full((H_A, CHUNK, CHUNK)), _full((CHUNK, H_A)), vec, vec, ANY],
        out_specs=[_rows(tm, 2 * D_A), _rows(tm, D_B), vec, vec, _full((H_A, CHUNK, CHUNK)),
                   _full((H_A, CHUNK)), vec, vec, vec],
        out_shape=[jax.ShapeDtypeStruct((T, 2 * D_A), BF16), jax.ShapeDtypeStruct((T, D_B), F32), vshape, vshape,
                   jax.ShapeDtypeStruct((H_A, CHUNK, CHUNK), F32), jax.ShapeDtypeStruct((H_A, CHUNK), F32),
                   vshape, vshape, vshape],
        scratch_shapes=[pltpu.VMEM((tm, D_A), F32), pltpu.VMEM((CHUNK, D_A), F32)],
        compiler_params=_cparams(32),
    )(dxb, wout, proj, c, lg, lb, wm, bst, clg, clb, dep)


def _mixer_bwd_b(dc, g, proj, cw, dpa, wg_in, x, g1, dres):
    tm = TM_MIX
    n_tiles = T // tm
    hb = tm // HALO
    bn = wg_in.shape[2]

    def body(dc_ref, dch_ref, g_ref, gh_ref, p_ref, cw_ref, dpa_ref, win_ref, x_ref, g1_ref, dres_ref,
             dp_ref, dcw_ref, dx_ref, dxb_ref, dg1_ref, dcbuf, gbuf, dwacc):
        i = pl.program_id(0)

        @pl.when(i == 0)
        def _():
            dwacc[...] = jnp.zeros(dwacc.shape, F32)
            dg1_ref[...] = jnp.zeros(dg1_ref.shape, F32)

        _fill_shift_buffer(dcbuf, 0, dc_ref[...])
        _fill_shift_buffer(dcbuf, tm, jnp.where(i < n_tiles - 1, dch_ref[...], 0.0))
        _fill_shift_buffer(gbuf, 0, jnp.where(i > 0, gh_ref[...], 0.0))
        _fill_shift_buffer(gbuf, HALO, g_ref[...])
        _build_shifts(dcbuf)
        _build_shifts(gbuf)
        dp_ref[:, 0:2 * D_A] = dpa_ref[...]
        for q in range(H_B):
            cols = slice(q * LANES, (q + 1) * LANES)

            def row_block(rb, carry, q=q, cols=cols):
                base = pl.multiple_of(rb * RB, RB)
                rows = pl.ds(base, RB)
                dg = _conv_taps(dcbuf, cw_ref, q, base, CONV_W - 1, -1)
                val = p_ref[rows, cols]
                sg = _sigmoid(p_ref[rows, D_B + q * LANES:D_B + (q + 1) * LANES])
                dp_ref[rows, 2 * D_A + q * LANES:2 * D_A + (q + 1) * LANES] = (dg * sg).astype(BF16)
                dp_ref[rows, 2 * D_A + D_B + q * LANES:2 * D_A + D_B + (q + 1) * LANES] = (
                    dg * val * sg * (1.0 - sg)).astype(BF16)
                dcv = dcbuf[0, q, rows, :]
                parts = []
                for k in range(CONV_W):
                    prod = dcv * _shifted(gbuf, q, base, HALO - (CONV_W - 1) + k)
                    parts.append(jnp.sum(prod.reshape(RB // 8, 8, LANES), axis=0))
                for k in range(CONV_W):
                    dwacc[k * 8:(k + 1) * 8, cols] += parts[k]
                return carry

            lax.fori_loop(0, tm // RB, row_block, 0)

        dh = jnp.zeros((tm, D), F32)
        for j in range(N_DEV):
            dh = dh + _dot_nt(dp_ref[:, j * bn:(j + 1) * bn], win_ref[j])
        xv = x_ref[...]
        dxn, dg = _rms_bwd(dh, xv, _rms(xv), g1_ref[...])
        dx = dres_ref[...] + dxn
        dx_ref[...] = dx
        dxb_ref[...] = dx.astype(BF16)
        dg1_ref[...] += dg

        @pl.when(i == n_tiles - 1)
        def _():
            for k in range(CONV_W):
                dcw_ref[k:k + 1, :] = jnp.sum(dwacc[k * 8:(k + 1) * 8, :], axis=0, keepdims=True)

    return pl.pallas_call(
        body, name="mixer_bwd_b", grid=(n_tiles,),
        in_specs=[_rows(tm, D_B),
                  pl.BlockSpec((HALO, D_B), lambda i: (jnp.minimum((i + 1) * hb, T // HALO - 1), 0)),
                  _rows(tm, D_B),
                  pl.BlockSpec((HALO, D_B), lambda i: (jnp.maximum(i * hb - 1, 0), 0)),
                  pl.BlockSpec((tm, 2 * D_B), lambda i: (i, 1)),
                  _full((CONV_W, D_B)), _rows(tm, 2 * D_A),
                  pl.BlockSpec(wg_in.shape, lambda i: (0, 0, 0), pipeline_mode=pl.Buffered(1)),
                  _rows(tm, D), _full((1, D)), _rows(tm, D)],
        out_specs=[_rows(tm, IN_COLS), _full((CONV_W, D_B)), _rows(tm, D), _rows(tm, D), _full((1, D))],
        out_shape=[jax.ShapeDtypeStruct((T, IN_COLS), BF16), jax.ShapeDtypeStruct((CONV_W, D_B), F32),
                   jax.ShapeDtypeStruct((T, D), F32), jax.ShapeDtypeStruct((T, D), BF16),
                   jax.ShapeDtypeStruct((1, D), F32)],
        scratch_shapes=[pltpu.VMEM((8, H_B, tm + HALO, LANES), F32), pltpu.VMEM((8, H_B, HALO + tm, LANES), F32),
                        pltpu.VMEM((CONV_W * 8, D_B), F32)],
        compiler_params=_cparams(40),
    )(dc, dc, g, g, proj, cw, dpa, wg_in, x, g1, dres)


def _wgrad(name, a, g, bn, square_a=False):
    k = a.shape[1]
    n = g.shape[1]
    tk = min(k, 1024)
    tn = min(n, max(bn, 1024))
    nsub = tn // bn
    tt = TT_WGRAD
    nt = T // tt

    def body(a_ref, g_ref, o_ref, ob_ref):
        t = pl.program_id(2)

        @pl.when(t == 0)
        def _():
            o_ref[...] = jnp.zeros(o_ref.shape, F32)

        av = a_ref[...]
        if square_a:
            av = av * av
        for s in range(nsub):
            o_ref[s] += _dot_tn(av, g_ref[:, s * bn:(s + 1) * bn])

        @pl.when(t == nt - 1)
        def _():
            ob_ref[...] = o_ref[...].astype(BF16)

    ospec = pl.BlockSpec((nsub, tk, bn), lambda ki, ni, ti: (ni, ki, 0))
    return pl.pallas_call(
        body, name=name, grid=(k // tk, n // tn, nt),
        in_specs=[pl.BlockSpec((tt, tk), lambda ki, ni, ti: (ti, ki)),
                  pl.BlockSpec((tt, tn), lambda ki, ni, ti: (ti, ni))],
        out_specs=[ospec, ospec],
        out_shape=[jax.ShapeDtypeStruct((n // bn, k, bn), F32), jax.ShapeDtypeStruct((n // bn, k, bn), BF16)],
        compiler_params=_cparams(40, ("arbitrary", "arbitrary", "arbitrary")),
    )(a, g)


def _coords():
    return lax.axis_index("x"), lax.axis_index("y"), lax.axis_index("c")


def _flat(x, y, c):
    return 4 * x + 2 * y + c


def _peer(k):
    x, y, c = _coords()
    return (x ^ ((k >> 2) & 1), y ^ ((k >> 1) & 1), c ^ (k & 1))


def _hbm(a):
    return pltpu.with_memory_space_constraint(a, pltpu.HBM)


def _hbm_like(a):
    return pltpu.HBM(a.shape, a.dtype)


def _peer_sems():
    return pltpu.SemaphoreType.DMA((N_DEV - 1,))


def _place_own(shards):
    n = len(shards)
    shapes = [(s.shape if l is None else s.shape[1:]) for s, l, _ in shards]

    def body(*refs):
        ins, outs = refs[:n], refs[n:2 * n]
        stage_in, stage_out = refs[2 * n:3 * n], refs[3 * n:4 * n]
        in_sems, out_sems = refs[4 * n], refs[4 * n + 1]
        me = _flat(*_coords())
        srcs = [ins[a] if shards[a][1] is None else ins[a].at[shards[a][1]] for a in range(n)]
        loads = [pltpu.make_async_copy(srcs[a], stage_in[a], in_sems.at[a]) for a in range(n)]
        stores = [pltpu.make_async_copy(stage_out[a], outs[a].at[me], out_sems.at[a]) for a in range(n)]
        for cp in loads:
            cp.start()
        for a in range(n):
            loads[a].wait()
            stage_out[a][...] = stage_in[a][...].astype(stage_out[a].dtype)
            stores[a].start()
        for cp in stores:
            cp.wait()

    return pl.pallas_call(
        body, name="place_own", in_specs=[ANY] * n, out_specs=[ANY] * n,
        out_shape=[jax.ShapeDtypeStruct((N_DEV,) + shapes[a], shards[a][2]) for a in range(n)],
        scratch_shapes=[*[pltpu.VMEM(shapes[a], shards[a][0].dtype) for a in range(n)],
                        *[pltpu.VMEM(shapes[a], shards[a][2]) for a in range(n)],
                        pltpu.SemaphoreType.DMA((n,)), pltpu.SemaphoreType.DMA((n,))],
        compiler_params=pltpu.CompilerParams(vmem_limit_bytes=40 * MIB),
    )(*[s for s, _, _ in shards])


SIBLING = 1
CHIP_PEERS = (2, 4, 6)
FIRST_PEERS = (SIBLING,) + CHIP_PEERS


def _gather_start(lands):
    n = len(lands)

    def body(*refs):
        lnd, send, recv = refs[:n], refs[n:2 * n], refs[2 * n:3 * n]
        me = _flat(*_coords())
        for a in range(n):
            for j, k in enumerate(FIRST_PEERS):
                pltpu.make_async_remote_copy(
                    src_ref=lnd[a].at[me], dst_ref=lnd[a].at[me], send_sem=send[a].at[j],
                    recv_sem=recv[a].at[j], device_id=_peer(k), device_id_type=MESH_ID).start()

    sems = pltpu.SemaphoreType.DMA((len(FIRST_PEERS),))
    outs = pl.pallas_call(
        body, name="gather_start",
        out_shape=(*[sems] * (2 * n), *[_hbm_like(l) for l in lands]),
        in_specs=[HBM] * n, out_specs=(*[SEM] * (2 * n), *[HBM] * n),
        input_output_aliases={i: 2 * n + i for i in range(n)},
        compiler_params=pltpu.CompilerParams(has_side_effects=EFFECT),
    )(*[_hbm(l) for l in lands])
    return outs[:n], outs[n:2 * n], outs[2 * n:]


def _gather_pass_on(name, land, send, recv, after):
    def body(l_ref, send_ref, recv_ref, after_ref, send2, recv2, l_out):
        sibling = _peer(SIBLING)
        for j, k in enumerate(CHIP_PEERS):
            cp = pltpu.make_async_remote_copy(
                src_ref=l_ref.at[0], dst_ref=l_ref.at[0], send_sem=send_ref.at[1 + j], recv_sem=recv_ref.at[1 + j],
                device_id=_peer(k), device_id_type=MESH_ID)
            cp.wait_send()
            cp.wait_recv()
            blk = _flat(*_peer(k))
            pltpu.make_async_remote_copy(
                src_ref=l_ref.at[blk], dst_ref=l_ref.at[blk], send_sem=send2.at[j], recv_sem=recv2.at[j],
                device_id=sibling, device_id_type=MESH_ID).start()

    sems = pltpu.SemaphoreType.DMA((len(CHIP_PEERS),))
    return pl.pallas_call(
        body, name=name, out_shape=(sems, sems, _hbm_like(land)),
        in_specs=(HBM, SEM, SEM, ANY), out_specs=(SEM, SEM, HBM), input_output_aliases={0: 2},
        compiler_params=pltpu.CompilerParams(has_side_effects=EFFECT),
    )(land, send, recv, after)


def _gather_wait(name, land, send, recv, send2, recv2, after):
    def body(l_ref, send_ref, recv_ref, send2_ref, recv2_ref, after_ref, l_out):
        sibling = _peer(SIBLING)
        own = pltpu.make_async_remote_copy(
            src_ref=l_ref.at[0], dst_ref=l_ref.at[0], send_sem=send_ref.at[0], recv_sem=recv_ref.at[0],
            device_id=sibling, device_id_type=MESH_ID)
        own.wait_send()
        own.wait_recv()
        for j in range(len(CHIP_PEERS)):
            cp = pltpu.make_async_remote_copy(
                src_ref=l_ref.at[0], dst_ref=l_ref.at[0], send_sem=send2_ref.at[j], recv_sem=recv2_ref.at[j],
                device_id=sibling, device_id_type=MESH_ID)
            cp.wait_send()
            cp.wait_recv()

    return pl.pallas_call(
        body, name=name, out_shape=_hbm_like(land),
        in_specs=(HBM, SEM, SEM, SEM, SEM, ANY), out_specs=HBM, input_output_aliases={0: 0},
        compiler_params=pltpu.CompilerParams(has_side_effects=EFFECT),
    )(land, send, recv, send2, recv2, after)


def _scatter_start(name, grads, lands, layer):
    n = len(grads)

    def body(*refs):
        g, lnd, send, recv = refs[:n], refs[n:2 * n], refs[2 * n:3 * n], refs[3 * n:4 * n]
        for a in range(n):
            for k in range(1, N_DEV):
                to = _peer(k)
                pltpu.make_async_remote_copy(
                    src_ref=g[a].at[_flat(*to)], dst_ref=lnd[a].at[layer, k - 1], send_sem=send[a].at[k - 1],
                    recv_sem=recv[a].at[k - 1], device_id=to, device_id_type=MESH_ID).start()

    outs = pl.pallas_call(
        body, name=name,
        out_shape=(*[_peer_sems()] * (2 * n), *[_hbm_like(g) for g in grads], *[_hbm_like(l) for l in lands]),
        in_specs=[HBM] * (2 * n), out_specs=(*[SEM] * (2 * n), *[HBM] * (2 * n)),
        input_output_aliases={i: 2 * n + i for i in range(2 * n)},
        compiler_params=pltpu.CompilerParams(has_side_effects=EFFECT),
    )(*[_hbm(g) for g in grads], *[_hbm(l) for l in lands])
    return outs[:n], outs[n:2 * n], outs[2 * n:3 * n], outs[3 * n:]


def _scatter_wait(name, grads, lands, sends, recvs, after):
    n, nw = len(grads), len(lands)

    def body(*refs):
        g, lnd = refs[:n], refs[n:n + nw]
        send, recv = refs[n + nw:2 * n + nw], refs[2 * n + nw:3 * n + nw]
        for a in range(n):
            for k in range(1, N_DEV):
                cp = pltpu.make_async_remote_copy(
                    src_ref=g[a].at[0], dst_ref=lnd[a // DEPTH].at[a % DEPTH, 0],
                    send_sem=send[a].at[k - 1], recv_sem=recv[a].at[k - 1],
                    device_id=_peer(k), device_id_type=MESH_ID)
                cp.wait_send()
                cp.wait_recv()

    outs = pl.pallas_call(
        body, name=name,
        out_shape=(*[_hbm_like(g) for g in grads], *[_hbm_like(l) for l in lands]),
        in_specs=(*[HBM] * (n + nw), *[SEM] * (2 * n), ANY), out_specs=[HBM] * (n + nw),
        input_output_aliases={i: i for i in range(n + nw)},
        compiler_params=pltpu.CompilerParams(has_side_effects=EFFECT),
    )(*grads, *lands, *sends, *recvs, after)
    return outs[n:]


def _own_blocks(grads_f32):
    nw = len(grads_f32)
    flat32 = [g for w in grads_f32 for g in w]
    n = len(flat32)

    def body(*refs):
        g32, own, stage = refs[:n], refs[n:n + nw], refs[n + nw:2 * n + nw]
        in_sems, out_sems = refs[2 * n + nw], refs[2 * n + nw + 1]
        me = _flat(*_coords())
        loads = [pltpu.make_async_copy(g32[a].at[me], stage[a], in_sems.at[a]) for a in range(n)]
        stores = [pltpu.make_async_copy(stage[a], own[a // DEPTH].at[a % DEPTH], out_sems.at[a]) for a in range(n)]
        for cp in loads:
            cp.start()
        for a in range(n):
            loads[a].wait()
            stores[a].start()
        for cp in stores:
            cp.wait()

    return pl.pallas_call(
        body, name="own_blocks", in_specs=[ANY] * n, out_specs=[ANY] * nw,
        out_shape=[jax.ShapeDtypeStruct((DEPTH,) + w[0].shape[1:], F32) for w in grads_f32],
        scratch_shapes=[*[pltpu.VMEM(g.shape[1:], F32) for g in flat32],
                        pltpu.SemaphoreType.DMA((n,)), pltpu.SemaphoreType.DMA((n,))],
        compiler_params=pltpu.CompilerParams(vmem_limit_bytes=24 * MIB),
    )(*flat32)


def _all_reduce_small(part, dep):
    rows = part.shape[0]
    br = rows // N_DEV
    assert br * N_DEV == rows and br % 8 == 0

    def body(p_ref, dep_ref, o_ref, slots, send1, recv1, send2, recv2):
        me = _flat(*_coords())

        def block(ref, d):
            return ref.at[pl.ds(pl.multiple_of(d * br, 8), br), :]

        slots[me] = p_ref[pl.ds(pl.multiple_of(me * br, 8), br), :]
        scatter, gather = [], []
        for k in range(1, N_DEV):
            to = _peer(k)
            scatter.append(pltpu.make_async_remote_copy(
                src_ref=block(p_ref, _flat(*to)), dst_ref=slots.at[me],
                send_sem=send1.at[k - 1], recv_sem=recv1.at[k - 1], device_id=to, device_id_type=MESH_ID))
            gather.append(pltpu.make_async_remote_copy(
                src_ref=block(o_ref, me), dst_ref=block(o_ref, me),
                send_sem=send2.at[k - 1], recv_sem=recv2.at[k - 1], device_id=to, device_id_type=MESH_ID))
        for cp in scatter:
            cp.start()
        for cp in scatter:
            cp.wait()
        acc = slots[0]
        for d in range(1, N_DEV):
            acc = acc + slots[d]
        o_ref[pl.ds(pl.multiple_of(me * br, 8), br), :] = acc
        for cp in gather:
            cp.start()
        for cp in gather:
            cp.wait()

    return pl.pallas_call(
        body, name="all_reduce_small",
        in_specs=[pl.BlockSpec(memory_space=pltpu.VMEM), ANY], out_specs=pl.BlockSpec(memory_space=pltpu.VMEM),
        out_shape=jax.ShapeDtypeStruct(part.shape, F32),
        scratch_shapes=[pltpu.VMEM((N_DEV, br, LANES), F32)] + [pltpu.SemaphoreType.DMA((N_DEV - 1,))] * 4,
        compiler_params=pltpu.CompilerParams(vmem_limit_bytes=24 * MIB),
    )(part, dep)


def _adam_math(w, g, m, v):
    m = ADAM_B1 * m + (1.0 - ADAM_B1) * g
    v = ADAM_B2 * v + (1.0 - ADAM_B2) * (g * g)
    m_hat = m / (1.0 - ADAM_B1 ** ADAM_STEP)
    v_hat = v / (1.0 - ADAM_B2 ** ADAM_STEP)
    delta = -ADAM_LR * (m_hat / (jnp.sqrt(v_hat) + ADAM_EPS) + ADAM_WD * w)
    return delta, m, v


def _adam_sharded(name, own, recv, w, m, v):
    _, r, c = w.shape
    tr = min(r, 256)

    def body(own_ref, recv_ref, w_ref, m_ref, v_ref, g_ref, d_ref, nm_ref, nv_ref):
        g = own_ref[...]
        for k in range(N_DEV - 1):
            g = g + recv_ref[k].astype(F32)
        delta, nm, nv = _adam_math(w_ref[...], g, m_ref[...], v_ref[...])
        g_ref[...] = g
        d_ref[...] = delta
        nm_ref[...] = nm
        nv_ref[...] = nv

    blk = pl.BlockSpec((None, tr, c), lambda l, i: (l, i, 0))
    shp = jax.ShapeDtypeStruct(w.shape, F32)
    return pl.pallas_call(
        body, name=name, grid=(DEPTH, r // tr),
        in_specs=[blk, pl.BlockSpec((None, N_DEV - 1, tr, c), lambda l, i: (l, 0, i, 0)), blk, blk, blk],
        out_specs=[blk] * 4, out_shape=[shp] * 4,
        compiler_params=_cparams(32, ("arbitrary", "arbitrary")),
    )(own, recv, w, m, v)


def _adam_small(g, w, m, v):
    rows = g.shape[0]

    def body(g_ref, w_ref, m_ref, v_ref, d_ref, nm_ref, nv_ref):
        delta, nm, nv = _adam_math(w_ref[...], g_ref[...], m_ref[...], v_ref[...])
        d_ref[...] = delta
        nm_ref[...] = nm
        nv_ref[...] = nv

    spec = _full((rows, LANES))
    shp = jax.ShapeDtypeStruct((rows, LANES), F32)
    return pl.pallas_call(
        body, name="adam_small", grid=(1,),
        in_specs=[spec] * 4, out_specs=[spec] * 3, out_shape=[shp] * 3,
        compiler_params=_cparams(24),
    )(g, w, m, v)


SMALL = ["norm1_g", "sgu_ln_g", "sgu_ln_b", "sgu_w", "sgu_b", "conv_b", "conv_ln_g", "conv_ln_b", "norm2_g",
         "final_g"]


def _pack(arrays):
    flat = jnp.concatenate([a.reshape(-1) for a in arrays])
    pad = (-flat.shape[0]) % (N_DEV * 8 * LANES)
    return jnp.pad(flat, (0, pad)).reshape(-1, LANES)


def _unpack(packed, shapes):
    flat = packed.reshape(-1)
    out, off = [], 0
    for s in shapes:
        size = 1
        for d in s:
            size *= d
        out.append(flat[off:off + size].reshape(s))
        off += size
    return out


def kernel(x, norm1_g, w_in, sgu_ln_g, sgu_ln_b, sgu_w, sgu_b, conv_w, conv_b, conv_ln_g, conv_ln_b, w_out, norm2_g, w_ff1, w_ff2, final_g, loss_target, m_norm1_g, m_w_in, m_sgu_ln_g, m_sgu_ln_b, m_sgu_w, m_sgu_b, m_conv_w, m_conv_b, m_conv_ln_g, m_conv_ln_b, m_w_out, m_norm2_g, m_w_ff1, m_w_ff2, m_final_g, v_norm1_g, v_w_in, v_sgu_ln_g, v_sgu_ln_b, v_sgu_w, v_sgu_b, v_conv_w, v_conv_b, v_conv_ln_g, v_conv_ln_b, v_w_out, v_norm2_g, v_w_ff1, v_w_ff2, v_final_g):
    x2d = x.reshape(T, D)
    tgt = loss_target.reshape(T, D)
    cw_shard = conv_w.reshape(CONV_W, LANES)

    gnames, shards = [], []
    for l in range(DEPTH):
        for k, w in (("w_in", w_in), ("w_out", w_out), ("w_ff1", w_ff1), ("w_ff2", w_ff2)):
            gnames.append(f"{k}{l}")
            shards.append((w, l, BF16))
        if l == 0:
            gnames.insert(1, "conv_w")
            shards.insert(1, (cw_shard, None, F32))
    sends, recvs, lands = _gather_start(_place_own(shards))
    gidx = {k: i for i, k in enumerate(gnames)}

    passed = {}

    def pass_on(k, after):
        i = gidx[k]
        passed[k] = _gather_pass_on("gather_pass_on_" + k, lands[i], sends[i], recvs[i], after)

    def gathered(k, after):
        i = gidx[k]
        send2, recv2, land = passed[k]
        return _gather_wait("gather_wait_" + k, land, sends[i], recvs[i], send2, recv2, after)

    saved = []
    xl = x2d
    cw_full = None
    for l in range(DEPTH):
        g1 = norm1_g[l].reshape(1, D)
        g2 = norm2_g[l].reshape(1, D)
        lg, lb = sgu_ln_g[l].reshape(1, D_A), sgu_ln_b[l].reshape(1, D_A)
        bst = sgu_b[l].T
        cb = conv_b[l].reshape(1, D_B)
        clg, clb = conv_ln_g[l].reshape(1, D_B), conv_ln_b[l].reshape(1, D_B)
        if l == 0:
            pass_on("w_in0", xl)
        wg_in = gathered(f"w_in{l}", xl)
        if l == 0:
            pass_on("conv_w", xl)
        pass_on(f"w_out{l}", xl)
        h, proj = _fwd_in(xl, g1, wg_in)
        if l == 0:
            cw_full = gathered("conv_w", proj).reshape(N_DEV, DEPTH, CONV_W, D_B // N_DEV).transpose(
                1, 2, 0, 3).reshape(DEPTH, CONV_W, D_B)
        wout = gathered(f"w_out{l}", proj).reshape(D, D)
        if l > 0:
            pass_on(f"w_ff1{l}", proj)
        mix, c, g, x1, h2 = _mixer_fwd(proj, lg, lb, sgu_w[l], bst, cw_full[l], cb, clg, clb, wout, xl, g2)
        if l == 0:
            pass_on(f"w_ff1{l}", h2)
        else:
            pass_on(f"w_ff2{l}", h2)
        wg_ff1 = gathered(f"w_ff1{l}", h2)
        r = _fwd_ff1(h2, wg_ff1)
        if l == 0:
            pass_on(f"w_ff2{l}", r)
        if l + 1 < DEPTH:
            pass_on(f"w_in{l + 1}", r)
        w2 = gathered(f"w_ff2{l}", r).reshape(D_FF, D)
        saved.append(dict(x=xl, h=h, proj=proj, mix=mix, c=c, g=g, x1=x1, h2=h2, r=r, wg_in=wg_in, wout=wout,
                          wg_ff1=wg_ff1, w2=w2, g1=g1, g2=g2, lg=lg, lb=lb, bst=bst, clg=clg, clb=clb))
        if l + 1 < DEPTH:
            xl = _fwd_ff2(r, w2, x1)
        else:
            dx, dxb, loss_part, d_final_g = _fwd_ff2_loss(r, w2, x1, final_g.reshape(1, D), tgt)

    names = ["w_in", "w_out", "w_ff1", "w_ff2"]
    block = dict(w_in=(D, IN_COLS // N_DEV), w_out=(D // N_DEV, D), w_ff1=(D, D_FF // N_DEV), w_ff2=(D_FF // N_DEV, D))
    land = {k: lax.empty((DEPTH, N_DEV - 1) + block[k], BF16) for k in names}
    big32 = {k: [None] * DEPTH for k in names}
    big16 = {k: [None] * DEPTH for k in names}
    ssend = {k: [None] * DEPTH for k in names}
    srecv = {k: [None] * DEPTH for k in names}

    def send_grads(l, grads):
        ks = list(grads)
        for k in ks:
            big32[k][l] = grads[k][0].reshape((N_DEV,) + block[k])
        sends, recvs, g16s, lands = _scatter_start(
            "scatter_start_" + "_".join(ks) + str(l), [grads[k][1].reshape((N_DEV,) + block[k]) for k in ks],
            [land[k] for k in ks], l)
        for i, k in enumerate(ks):
            ssend[k][l], srecv[k][l], big16[k][l], land[k] = sends[i], recvs[i], g16s[i], lands[i]

    small = {}
    d_conv_w = [None] * DEPTH
    for l in reversed(range(DEPTH)):
        s = saved[l]
        send_grads(l, dict(w_ff2=_wgrad("wgrad_ff2", s["r"], dxb, D, square_a=True)))
        df1, dx1, dx1b, dg2 = _bwd_mlp(dxb, dx, s["w2"], s["r"], s["wg_ff1"], s["x1"], s["g2"], big16["w_ff2"][l])
        send_grads(l, dict(w_ff1=_wgrad("wgrad_ff1", s["h2"], df1, D_FF // N_DEV),
                           w_out=_wgrad("wgrad_out", s["mix"], dx1b, D)))
        dpa, dc, dlg, dlb, dwm, dbs, dcb, dclg, dclb = _mixer_bwd_a(
            dx1b, s["wout"], s["proj"], s["c"], s["lg"], s["lb"], sgu_w[l], s["bst"], s["clg"], s["clb"],
            big16["w_out"][l])
        dproj, d_conv_w[l], dx, dxb, dg1 = _mixer_bwd_b(dc, s["g"], s["proj"], cw_full[l], dpa, s["wg_in"], s["x"],
                                                         s["g1"], dx1)
        send_grads(l, dict(w_in=_wgrad("wgrad_in", s["h"], dproj, IN_COLS // N_DEV)))
        small[l] = dict(norm1_g=dg1, sgu_ln_g=dlg, sgu_ln_b=dlb, sgu_w=dwm, sgu_b=dbs, conv_b=dcb, conv_ln_g=dclg,
                        conv_ln_b=dclb, norm2_g=dg2)
    grad_x = dx.reshape(1, T, D)

    own = dict(zip(names, _own_blocks([big32[k] for k in names])))
    wmv = dict(w_in=(w_in, m_w_in, v_w_in), w_out=(w_out, m_w_out, v_w_out), w_ff1=(w_ff1, m_w_ff1, v_w_ff1),
               w_ff2=(w_ff2, m_w_ff2, v_w_ff2))
    res = {}

    def update(tag, ks, after):
        recv = _scatter_wait("scatter_wait_" + tag, [g for k in ks for g in big16[k]], [land[k] for k in ks],
                             [q for k in ks for q in ssend[k]], [q for k in ks for q in srecv[k]], after)
        for k, rk in zip(ks, recv):
            res[k] = _adam_sharded("adam_" + k, own[k], rk, *wmv[k])

    update("early", ["w_ff2", "w_ff1", "w_out"], big16["w_in"][0])

    rep = dict(norm1_g=(norm1_g, m_norm1_g, v_norm1_g), sgu_ln_g=(sgu_ln_g, m_sgu_ln_g, v_sgu_ln_g),
               sgu_ln_b=(sgu_ln_b, m_sgu_ln_b, v_sgu_ln_b), sgu_w=(sgu_w, m_sgu_w, v_sgu_w),
               sgu_b=(sgu_b, m_sgu_b, v_sgu_b), conv_b=(conv_b, m_conv_b, v_conv_b),
               conv_ln_g=(conv_ln_g, m_conv_ln_g, v_conv_ln_g), conv_ln_b=(conv_ln_b, m_conv_ln_b, v_conv_ln_b),
               norm2_g=(norm2_g, m_norm2_g, v_norm2_g), final_g=(final_g, m_final_g, v_final_g))
    parts = []
    for k in SMALL:
        if k == "final_g":
            parts.append(d_final_g.reshape(rep[k][0].shape))
        else:
            parts.append(jnp.stack([small[l][k].reshape(rep[k][0].shape[1:]) for l in range(DEPTH)]))
    parts.append(jnp.stack(d_conv_w))
    parts.append(loss_part[0, 0:1])
    shapes = [p.shape for p in parts]
    packed_sum = _all_reduce_small(_pack(parts), res["w_out"][0])
    update("late", ["w_in"], packed_sum)
    summed = _unpack(packed_sum, shapes)
    loss = summed[-1][0]
    me = _flat(*_coords())
    g_conv_w = lax.dynamic_slice_in_dim(summed[-2], me * (D_B // N_DEV), D_B // N_DEV, axis=2)
    g_small = summed[:-2] + [g_conv_w]
    keys = SMALL + ["conv_w"]
    rep["conv_w"] = (conv_w, m_conv_w, v_conv_w)
    shapes = [a.shape for a in g_small]
    packed = [_pack(g_small)] + [_pack([rep[k][i] for k in keys]) for i in range(3)]
    d_s, nm_s, nv_s = (_unpack(a, shapes) for a in _adam_small(*packed))
    for i, k in enumerate(keys):
        res[k] = (g_small[i], d_s[i], nm_s[i], nv_s[i])

    order = ["norm1_g", "w_in", "sgu_ln_g", "sgu_ln_b", "sgu_w", "sgu_b", "conv_w", "conv_b", "conv_ln_g",
             "conv_ln_b", "w_out", "norm2_g", "w_ff1", "w_ff2", "final_g"]
    return (loss, grad_x, *[res[k][0] for k in order], *[res[k][1] for k in order],
            *[res[k][2] for k in order], *[res[k][3] for k in order])
```

```python
import functools

import jax
import jax.numpy as jnp
from jax import lax
from jax.experimental import pallas as pl
from jax.experimental.pallas import tpu as pltpu

F32 = jnp.float32
BF16 = jnp.bfloat16

N_DEV = 8
DEPTH = 2
T = 4096
D = 1024
D_A = 512
D_B = 512
CHUNK = 128
H_A = 4
H_B = 4
CONV_W = 31
HALO = 32
D_FF = 4096
IN_COLS = 2048
EPS = 1e-6

ADAM_LR = 0.001
ADAM_B1 = 0.9
ADAM_B2 = 0.999
ADAM_EPS = 1e-08
ADAM_WD = 0.01
ADAM_STEP = 10

TM = 512
TM_FWD = 1024
TT_WGRAD = 4096
TM_MIX = 256
TM_SGU = 512
RB = 64
CONV_CHAINS = 2
LANES = 128
MIB = 1024 * 1024
SCOPED_VMEM_MIB = 60

SQRT_HALF = 0.7071067811865476
INV_SQRT_2PI = 0.3989422804014327

MESH_ID = pl.DeviceIdType.MESH
ANY = pl.BlockSpec(memory_space=pl.ANY)
HBM = pl.BlockSpec(memory_space=pltpu.HBM)
SEM = pl.BlockSpec(memory_space=pltpu.SEMAPHORE)
EFFECT = pltpu.SideEffectType.DATAFLOW_SIDE_EFFECTING


def _cparams(vmem_mib, sem=("arbitrary",)):
    assert vmem_mib <= SCOPED_VMEM_MIB
    return pltpu.CompilerParams(dimension_semantics=sem, vmem_limit_bytes=SCOPED_VMEM_MIB * MIB)


def _full(shape):
    return pl.BlockSpec(shape, lambda *_: (0,) * len(shape))


def _rows(tm, cols):
    return pl.BlockSpec((tm, cols), lambda i: (i, 0))


def _gelu(x):
    cdf = 0.5 * (1.0 + lax.erf(x * SQRT_HALF))
    return x * cdf, cdf


def _gelu_grad(x, cdf):
    return cdf + x * (INV_SQRT_2PI * jnp.exp(-0.5 * x * x))


def _sigmoid(x):
    return 1.0 / (1.0 + jnp.exp(-x))


def _ln(x):
    mu = jnp.mean(x, axis=-1, keepdims=True)
    xc = x - mu
    rstd = lax.rsqrt(jnp.mean(xc * xc, axis=-1, keepdims=True) + EPS)
    return xc * rstd, rstd


def _ln_bwd(dyh, xhat, rstd):
    return rstd * (dyh - jnp.mean(dyh, axis=-1, keepdims=True) - xhat * jnp.mean(dyh * xhat, axis=-1, keepdims=True))


def _rms(x):
    return lax.rsqrt(jnp.mean(x * x, axis=-1, keepdims=True) + EPS)


def _rms_bwd(dh, x, r, g):
    n = x * r
    dn = dh * g
    dx = r * (dn - n * jnp.mean(dn * n, axis=-1, keepdims=True))
    return dx, jnp.sum(dh * n, axis=0, keepdims=True)


def _dot(a, b):
    return jnp.dot(a, b, preferred_element_type=F32)


def _dot_nt(a, b):
    return lax.dot_general(a, b, (((1,), (1,)), ((), ())), preferred_element_type=F32)


def _dot_tn(a, b):
    return lax.dot_general(a, b, (((0,), (0,)), ((), ())), preferred_element_type=F32)


def _tril_mask():
    r = lax.broadcasted_iota(jnp.int32, (CHUNK, CHUNK), 0)
    c = lax.broadcasted_iota(jnp.int32, (CHUNK, CHUNK), 1)
    return r >= c


def _fwd_in(x, g1, wg):
    bn = wg.shape[2]

    def body(x_ref, g_ref, w_ref, h_ref, p_ref):
        xv = x_ref[...]
        h = (xv * _rms(xv) * g_ref[...]).astype(BF16)
        h_ref[...] = h
        for j in range(N_DEV):
            p_ref[:, j * bn:(j + 1) * bn] = _dot(h, w_ref[j])

    return pl.pallas_call(
        body, name="fwd_in", grid=(T // TM_FWD,),
        in_specs=[_rows(TM_FWD, D), _full((1, D)), _full(wg.shape)],
        out_specs=[_rows(TM_FWD, D), _rows(TM_FWD, IN_COLS)],
        out_shape=[jax.ShapeDtypeStruct((T, D), BF16), jax.ShapeDtypeStruct((T, IN_COLS), F32)],
        compiler_params=_cparams(32),
    )(x, g1, wg)


def _fill_shift_buffer(sh_ref, row0, value):
    for q in range(sh_ref.shape[1]):
        sh_ref[0, q, row0:row0 + value.shape[0], :] = value[:, q * LANES:(q + 1) * LANES]


def _build_shifts(sh_ref):
    rows = sh_ref.shape[2]
    for p in range(1, 8):
        for q in range(sh_ref.shape[1]):
            sh_ref[p, q, 0:rows - 8, :] = sh_ref[0, q, p:p + rows - 8, :]


def _shifted(sh_ref, q, base, off):
    start = base + (off - off % 8)
    if not isinstance(start, int):
        start = pl.multiple_of(start, 8)
    return sh_ref[off % 8, q, pl.ds(start, RB), :]


def _conv_taps(sh_ref, w_ref, q, base, first_tap_row, step):
    cols = slice(q * LANES, (q + 1) * LANES)
    acc = [jnp.zeros((RB, LANES), F32) for _ in range(CONV_CHAINS)]
    for k in range(CONV_W):
        term = _shifted(sh_ref, q, base, first_tap_row + step * k) * w_ref[k:k + 1, cols]
        acc[k % CONV_CHAINS] = acc[k % CONV_CHAINS] + term
    return functools.reduce(lambda a, b: a + b, acc)


def _mixer_fwd(proj, lg, lb, wm, bst, cw, cb, clg, clb, wout, x, g2):
    tm = TM_MIX
    hb = tm // HALO

    def body(p_ref, ph_ref, lg_ref, lb_ref, wm_ref, bs_ref, cw_ref, cb_ref, clg_ref, clb_ref, wo_ref, x_ref, g2_ref,
             mix_ref, c_ref, g_ref, x1_ref, h2_ref, gbuf):
        i = pl.program_id(0)
        u, _ = _gelu(p_ref[:, 0:D_A])
        vg, _ = _gelu(p_ref[:, D_A:2 * D_A])
        xhat, _ = _ln(vg)
        v = (xhat * lg_ref[...] + lb_ref[...]).astype(BF16)
        mask = _tril_mask()
        for h in range(H_A):
            hc = slice(h * CHUNK, (h + 1) * CHUNK)
            wmh = jnp.where(mask, wm_ref[h], 0.0).astype(BF16)
            for c in range(tm // CHUNK):
                rc = slice(c * CHUNK, (c + 1) * CHUNK)
                mixed = _dot(wmh, v[rc, hc]) + bs_ref[:, h:h + 1]
                mix_ref[rc, hc] = (u[rc, hc] * mixed).astype(BF16)
        x1_ref[...] = x_ref[...] + _dot(mix_ref[:, 0:D_A], wo_ref[0:D_A, :])

        g = p_ref[:, 2 * D_A:2 * D_A + D_B] * _sigmoid(p_ref[:, 2 * D_A + D_B:IN_COLS])
        g_ref[...] = g
        gh = ph_ref[:, 0:D_B] * _sigmoid(ph_ref[:, D_B:2 * D_B])
        _fill_shift_buffer(gbuf, 0, jnp.where(i > 0, gh, 0.0))
        _fill_shift_buffer(gbuf, HALO, g)
        _build_shifts(gbuf)
        for q in range(H_B):
            cols = slice(q * LANES, (q + 1) * LANES)
            for rb in range(tm // RB):
                acc = _conv_taps(gbuf, cw_ref, q, rb * RB, HALO - (CONV_W - 1), 1)
                c_ref[rb * RB:(rb + 1) * RB, cols] = acc + cb_ref[:, cols]
        for q in range(H_B):
            cols = slice(q * LANES, (q + 1) * LANES)
            chat, _ = _ln(c_ref[:, cols])
            z = chat * clg_ref[:, cols] + clb_ref[:, cols]
            mix_ref[:, D_A + q * LANES:D_A + (q + 1) * LANES] = (z * _sigmoid(z)).astype(BF16)

        x1 = x1_ref[...] + _dot(mix_ref[:, D_A:D], wo_ref[D_A:D, :])
        x1_ref[...] = x1
        h2_ref[...] = (x1 * _rms(x1) * g2_ref[...]).astype(BF16)

    vec = _full((1, D_A))
    return pl.pallas_call(
        body, name="mixer_fwd", grid=(T // tm,),
        in_specs=[_rows(tm, IN_COLS),
                  pl.BlockSpec((HALO, 2 * D_B), lambda i: (jnp.maximum(i * hb - 1, 0), 1)),
                  vec, vec, _full((H_A, CHUNK, CHUNK)), _full((CHUNK, H_A)),
                  _full((CONV_W, D_B)), vec, vec, vec,
                  pl.BlockSpec((D, D), lambda i: (0, 0), pipeline_mode=pl.Buffered(1)), _rows(tm, D), _full((1, D))],
        out_specs=[_rows(tm, D), _rows(tm, D_B), _rows(tm, D_B), _rows(tm, D), _rows(tm, D)],
        out_shape=[jax.ShapeDtypeStruct((T, D), BF16), jax.ShapeDtypeStruct((T, D_B), F32),
                   jax.ShapeDtypeStruct((T, D_B), F32), jax.ShapeDtypeStruct((T, D), F32),
                   jax.ShapeDtypeStruct((T, D), BF16)],
        scratch_shapes=[pltpu.VMEM((8, H_B, HALO + tm, LANES), F32)],
        compiler_params=_cparams(40),
    )(proj, proj, lg, lb, wm, bst, cw, cb, clg, clb, wout, x, g2)


def _fwd_ff1(h2, wg):
    bn = wg.shape[2]

    def body(h_ref, w_ref, r_ref):
        h = h_ref[...]
        for j in range(N_DEV):
            r_ref[:, j * bn:(j + 1) * bn] = jnp.maximum(_dot(h, w_ref[j]), 0.0).astype(BF16)

    return pl.pallas_call(
        body, name="fwd_ff1", grid=(T // TM_FWD,),
        in_specs=[_rows(TM_FWD, D), _full(wg.shape)],
        out_specs=_rows(TM_FWD, D_FF),
        out_shape=jax.ShapeDtypeStruct((T, D_FF), BF16),
        compiler_params=_cparams(48),
    )(h2, wg)


def _fwd_ff2(r, w2, x1):
    def body(r_ref, w_ref, x_ref, o_ref):
        rv = r_ref[...]
        o_ref[...] = x_ref[...] + _dot(rv * rv, w_ref[...])

    return pl.pallas_call(
        body, name="fwd_ff2", grid=(T // TM_FWD,),
        in_specs=[_rows(TM_FWD, D_FF), _full((D_FF, D)), _rows(TM_FWD, D)],
        out_specs=_rows(TM_FWD, D),
        out_shape=jax.ShapeDtypeStruct((T, D), F32),
        compiler_params=_cparams(48),
    )(r, w2, x1)


def _fwd_ff2_loss(r, w2, x1, gf, tgt):
    def body(r_ref, w_ref, x_ref, g_ref, t_ref, dx_ref, dxb_ref, loss_ref, dg_ref):
        i = pl.program_id(0)

        @pl.when(i == 0)
        def _():
            loss_ref[...] = jnp.zeros(loss_ref.shape, F32)
            dg_ref[...] = jnp.zeros(dg_ref.shape, F32)

        rv = r_ref[...]
        xv = x_ref[...] + _dot(rv * rv, w_ref[...])
        rn = _rms(xv)
        diff = xv * rn * g_ref[...] - t_ref[...]
        loss_ref[...] += 0.5 * jnp.sum(jnp.mean(diff * diff, axis=-1, keepdims=True), axis=0, keepdims=True)
        dx, dg = _rms_bwd(diff * (1.0 / D), xv, rn, g_ref[...])
        dx_ref[...] = dx
        dxb_ref[...] = dx.astype(BF16)
        dg_ref[...] += dg

    return pl.pallas_call(
        body, name="fwd_ff2_loss", grid=(T // TM,),
        in_specs=[_rows(TM, D_FF), pl.BlockSpec((D_FF, D), lambda i: (0, 0), pipeline_mode=pl.Buffered(1)),
                  _rows(TM, D), _full((1, D)), _rows(TM, D)],
        out_specs=[_rows(TM, D), _rows(TM, D), _full((8, LANES)), _full((1, D))],
        out_shape=[jax.ShapeDtypeStruct((T, D), F32), jax.ShapeDtypeStruct((T, D), BF16),
                   jax.ShapeDtypeStruct((8, LANES), F32), jax.ShapeDtypeStruct((1, D), F32)],
        compiler_params=_cparams(40),
    )(r, w2, x1, gf, tgt)


def _bwd_mlp(dxb, dres, w2, r, wg1, x1, g2, dep):
    bn = wg1.shape[2]

    def body(d_ref, dres_ref, w2_ref, r_ref, w1_ref, x_ref, g_ref, dep_ref, df1_ref, dx_ref, dxb_ref, dg_ref):
        i = pl.program_id(0)

        @pl.when(i == 0)
        def _():
            dg_ref[...] = jnp.zeros(dg_ref.shape, F32)

        d = d_ref[...]
        dh = jnp.zeros((TM, D), F32)
        for j in range(N_DEV):
            cols = slice(j * bn, (j + 1) * bn)
            df1 = (2.0 * r_ref[:, cols].astype(F32) * _dot_nt(d, w2_ref[cols, :])).astype(BF16)
            df1_ref[:, cols] = df1
            dh = dh + _dot_nt(df1, w1_ref[j])
        xv = x_ref[...]
        dxn, dg = _rms_bwd(dh, xv, _rms(xv), g_ref[...])
        dx = dres_ref[...] + dxn
        dx_ref[...] = dx
        dxb_ref[...] = dx.astype(BF16)
        dg_ref[...] += dg

    once = dict(pipeline_mode=pl.Buffered(1))
    return pl.pallas_call(
        body, name="bwd_mlp", grid=(T // TM,),
        in_specs=[_rows(TM, D), _rows(TM, D), pl.BlockSpec((D_FF, D), lambda i: (0, 0), **once), _rows(TM, D_FF),
                  pl.BlockSpec(wg1.shape, lambda i: (0, 0, 0), **once), _rows(TM, D), _full((1, D)), ANY],
        out_specs=[_rows(TM, D_FF), _rows(TM, D), _rows(TM, D), _full((1, D))],
        out_shape=[jax.ShapeDtypeStruct((T, D_FF), BF16), jax.ShapeDtypeStruct((T, D), F32),
                   jax.ShapeDtypeStruct((T, D), BF16), jax.ShapeDtypeStruct((1, D), F32)],
        compiler_params=_cparams(56),
    )(dxb, dres, w2, r, wg1, x1, g2, dep)


def _mixer_bwd_a(dxb, wout, proj, c, lg, lb, wm, bst, clg, clb, dep):
    tm = TM_SGU
    n_tiles = T // tm

    def body(dx_ref, wo_ref, p_ref, c_ref, lg_ref, lb_ref, wm_ref, bs_ref, clg_ref, clb_ref, dep_ref,
             dpa_ref, dc_ref, dlg_ref, dlb_ref, dwm_ref, dbs_ref, dcb_ref, dclg_ref, dclb_ref,
             dv_buf, db_acc):
        i = pl.program_id(0)

        @pl.when(i == 0)
        def _():
            for ref in (dlg_ref, dlb_ref, dwm_ref, dbs_ref, dcb_ref, dclg_ref, dclb_ref, db_acc):
                ref[...] = jnp.zeros(ref.shape, F32)

        dmix = _dot_nt(dx_ref[...], wo_ref[...])
        ua = p_ref[:, 0:D_A]
        va = p_ref[:, D_A:2 * D_A]
        u, cdf_u = _gelu(ua)
        vg, cdf_v = _gelu(va)
        xhat, rstd = _ln(vg)
        v = (xhat * lg_ref[...] + lb_ref[...]).astype(BF16)
        mask = _tril_mask()
        for h in range(H_A):
            hc = slice(h * CHUNK, (h + 1) * CHUNK)
            wmh = jnp.where(mask, wm_ref[h], 0.0).astype(BF16)
            for cidx in range(tm // CHUNK):
                rc = slice(cidx * CHUNK, (cidx + 1) * CHUNK)
                vb = v[rc, hc]
                mixed = _dot(wmh, vb) + bs_ref[:, h:h + 1]
                da = dmix[rc, hc]
                dpa_ref[rc, hc] = (da * mixed * _gelu_grad(ua[rc, hc], cdf_u[rc, hc])).astype(BF16)
                dmixed = da * u[rc, hc]
                dmb = dmixed.astype(BF16)
                dv_buf[rc, hc] = _dot_tn(wmh, dmb)
                dwm_ref[h] += _dot_nt(dmb, vb)
                db_acc[:, hc] += dmixed
        dv = dv_buf[...]
        dlb_ref[...] += jnp.sum(dv, axis=0, keepdims=True)
        dlg_ref[...] += jnp.sum(dv * xhat, axis=0, keepdims=True)
        dvg = _ln_bwd(dv * lg_ref[...], xhat, rstd)
        dpa_ref[:, D_A:2 * D_A] = (dvg * _gelu_grad(va, cdf_v)).astype(BF16)

        for q in range(H_B):
            cols = slice(q * LANES, (q + 1) * LANES)
            chat, crstd = _ln(c_ref[:, cols])
            z = chat * clg_ref[:, cols] + clb_ref[:, cols]
            sg = _sigmoid(z)
            dz = dmix[:, D_A + q * LANES:D_A + (q + 1) * LANES] * (sg * (1.0 + z * (1.0 - sg)))
            dclb_ref[:, cols] += jnp.sum(dz, axis=0, keepdims=True)
            dclg_ref[:, cols] += jnp.sum(dz * chat, axis=0, keepdims=True)
            dc = _ln_bwd(dz * clg_ref[:, cols], chat, crstd)
            dc_ref[:, cols] = dc
            dcb_ref[:, cols] += jnp.sum(dc, axis=0, keepdims=True)

        @pl.when(i == n_tiles - 1)
        def _():
            for h in range(H_A):
                hc = slice(h * CHUNK, (h + 1) * CHUNK)
                dwm_ref[h] = jnp.where(mask, dwm_ref[h], 0.0)
                dbs_ref[h:h + 1, :] = jnp.sum(db_acc[:, hc].T, axis=0, keepdims=True)

    vec = _full((1, D_A))
    vshape = jax.ShapeDtypeStruct((1, D_A), F32)
    return pl.pallas_call(
        body, name="mixer_bwd_a", grid=(n_tiles,),
        in_specs=[_rows(tm, D), _full((D, D)), _rows(tm, IN_COLS), _rows(tm, D_B), vec, vec,
                  _full((H_A, CHUNK, CHUNK)), _full((CHUNK, H_A)), vec, vec, ANY],
        out_specs=[_rows(tm, 2 * D_A), _rows(tm, D_B), vec, vec, _full((H_A, CHUNK, CHUNK)),
                   _full((H_A, CHUNK)), vec, vec, vec],
        out_shape=[jax.ShapeDtypeStruct((T, 2 * D_A), BF16), jax.ShapeDtypeStruct((T, D_B), F32), vshape, vshape,
                   jax.ShapeDtypeStruct((H_A, CHUNK, CHUNK), F32), jax.ShapeDtypeStruct((H_A, CHUNK), F32),
                   vshape, vshape, vshape],
        scratch_shapes=[pltpu.VMEM((tm, D_A), F32), pltpu.VMEM((CHUNK, D_A), F32)],
        compiler_params=_cparams(32),
    )(dxb, wout, proj, c, lg, lb, wm, bst, clg, clb, dep)


def _mixer_bwd_b(dc, g, proj, cw, dpa, wg_in, x, g1, dres):
    tm = TM_MIX
    n_tiles = T // tm
    hb = tm // HALO
    bn = wg_in.shape[2]

    def body(dc_ref, dch_ref, g_ref, gh_ref, p_ref, cw_ref, dpa_ref, win_ref, x_ref, g1_ref, dres_ref,
             dp_ref, dcw_ref, dx_ref, dxb_ref, dg1_ref, dcbuf, gbuf, dwacc):
        i = pl.program_id(0)

        @pl.when(i == 0)
        def _():
            dwacc[...] = jnp.zeros(dwacc.shape, F32)
            dg1_ref[...] = jnp.zeros(dg1_ref.shape, F32)

        _fill_shift_buffer(dcbuf, 0, dc_ref[...])
        _fill_shift_buffer(dcbuf, tm, jnp.where(i < n_tiles - 1, dch_ref[...], 0.0))
        _fill_shift_buffer(gbuf, 0, jnp.where(i > 0, gh_ref[...], 0.0))
        _fill_shift_buffer(gbuf, HALO, g_ref[...])
        _build_shifts(dcbuf)
        _build_shifts(gbuf)
        dp_ref[:, 0:2 * D_A] = dpa_ref[...]
        for q in range(H_B):
            cols = slice(q * LANES, (q + 1) * LANES)

            def row_block(rb, carry, q=q, cols=cols):
                base = pl.multiple_of(rb * RB, RB)
                rows = pl.ds(base, RB)
                dg = _conv_taps(dcbuf, cw_ref, q, base, CONV_W - 1, -1)
                val = p_ref[rows, cols]
                sg = _sigmoid(p_ref[rows, D_B + q * LANES:D_B + (q + 1) * LANES])
                dp_ref[rows, 2 * D_A + q * LANES:2 * D_A + (q + 1) * LANES] = (dg * sg).astype(BF16)
                dp_ref[rows, 2 * D_A + D_B + q * LANES:2 * D_A + D_B + (q + 1) * LANES] = (
                    dg * val * sg * (1.0 - sg)).astype(BF16)
                dcv = dcbuf[0, q, rows, :]
                parts = []
                for k in range(CONV_W):
                    prod = dcv * _shifted(gbuf, q, base, HALO - (CONV_W - 1) + k)
                    parts.append(jnp.sum(prod.reshape(RB // 8, 8, LANES), axis=0))
                for k in range(CONV_W):
                    dwacc[k * 8:(k + 1) * 8, cols] += parts[k]
                return carry

            lax.fori_loop(0, tm // RB, row_block, 0)

        dh = jnp.zeros((tm, D), F32)
        for j in range(N_DEV):
            dh = dh + _dot_nt(dp_ref[:, j * bn:(j + 1) * bn], win_ref[j])
        xv = x_ref[...]
        dxn, dg = _rms_bwd(dh, xv, _rms(xv), g1_ref[...])
        dx = dres_ref[...] + dxn
        dx_ref[...] = dx
        dxb_ref[...] = dx.astype(BF16)
        dg1_ref[...] += dg

        @pl.when(i == n_tiles - 1)
        def _():
            for k in range(CONV_W):
                dcw_ref[k:k + 1, :] = jnp.sum(dwacc[k * 8:(k + 1) * 8, :], axis=0, keepdims=True)

    return pl.pallas_call(
        body, name="mixer_bwd_b", grid=(n_tiles,),
        in_specs=[_rows(tm, D_B),
                  pl.BlockSpec((HALO, D_B), lambda i: (jnp.minimum((i + 1) * hb, T // HALO - 1), 0)),
                  _rows(tm, D_B),
                  pl.BlockSpec((HALO, D_B), lambda i: (jnp.maximum(i * hb - 1, 0), 0)),
                  pl.BlockSpec((tm, 2 * D_B), lambda i: (i, 1)),
                  _full((CONV_W, D_B)), _rows(tm, 2 * D_A),
                  pl.BlockSpec(wg_in.shape, lambda i: (0, 0, 0), pipeline_mode=pl.Buffered(1)),
                  _rows(tm, D), _full((1, D)), _rows(tm, D)],
        out_specs=[_rows(tm, IN_COLS), _full((CONV_W, D_B)), _rows(tm, D), _rows(tm, D), _full((1, D))],
        out_shape=[jax.ShapeDtypeStruct((T, IN_COLS), BF16), jax.ShapeDtypeStruct((CONV_W, D_B), F32),
                   jax.ShapeDtypeStruct((T, D), F32), jax.ShapeDtypeStruct((T, D), BF16),
                   jax.ShapeDtypeStruct((1, D), F32)],
        scratch_shapes=[pltpu.VMEM((8, H_B, tm + HALO, LANES), F32), pltpu.VMEM((8, H_B, HALO + tm, LANES), F32),
                        pltpu.VMEM((CONV_W * 8, D_B), F32)],
        compiler_params=_cparams(40),
    )(dc, dc, g, g, proj, cw, dpa, wg_in, x, g1, dres)


def _wgrad(name, a, g, bn, square_a=False):
    k = a.shape[1]
    n = g.shape[1]
    tk = min(k, 1024)
    tn = min(n, max(bn, 1024))
    nsub = tn // bn
    tt = TT_WGRAD
    nt = T // tt

    def body(a_ref, g_ref, o_ref, ob_ref):
        t = pl.program_id(2)

        @pl.when(t == 0)
        def _():
            o_ref[...] = jnp.zeros(o_ref.shape, F32)

        av = a_ref[...]
        if square_a:
            av = av * av
        for s in range(nsub):
            o_ref[s] += _dot_tn(av, g_ref[:, s * bn:(s + 1) * bn])

        @pl.when(t == nt - 1)
        def _():
            ob_ref[...] = o_ref[...].astype(BF16)

    ospec = pl.BlockSpec((nsub, tk, bn), lambda ki, ni, ti: (ni, ki, 0))
    return pl.pallas_call(
        body, name=name, grid=(k // tk, n // tn, nt),
        in_specs=[pl.BlockSpec((tt, tk), lambda ki, ni, ti: (ti, ki)),
                  pl.BlockSpec((tt, tn), lambda ki, ni, ti: (ti, ni))],
        out_specs=[ospec, ospec],
        out_shape=[jax.ShapeDtypeStruct((n // bn, k, bn), F32), jax.ShapeDtypeStruct((n // bn, k, bn), BF16)],
        compiler_params=_cparams(40, ("arbitrary", "arbitrary", "arbitrary")),
    )(a, g)


def _coords():
    return lax.axis_index("x"), lax.axis_index("y"), lax.axis_index("c")


def _flat(x, y, c):
    return 4 * x + 2 * y + c


def _peer(k):
    x, y, c = _coords()
    return (x ^ ((k >> 2) & 1), y ^ ((k >> 1) & 1), c ^ (k & 1))


def _hbm(a):
    return pltpu.with_memory_space_constraint(a, pltpu.HBM)


def _hbm_like(a):
    return pltpu.HBM(a.shape, a.dtype)


def _peer_sems():
    return pltpu.SemaphoreType.DMA((N_DEV - 1,))


def _place_own(shards):
    n = len(shards)
    shapes = [(s.shape if l is None else s.shape[1:]) for s, l, _ in shards]

    def body(*refs):
        ins, outs = refs[:n], refs[n:2 * n]
        stage_in, stage_out = refs[2 * n:3 * n], refs[3 * n:4 * n]
        in_sems, out_sems = refs[4 * n], refs[4 * n + 1]
        me = _flat(*_coords())
        srcs = [ins[a] if shards[a][1] is None else ins[a].at[shards[a][1]] for a in range(n)]
        loads = [pltpu.make_async_copy(srcs[a], stage_in[a], in_sems.at[a]) for a in range(n)]
        stores = [pltpu.make_async_copy(stage_out[a], outs[a].at[me], out_sems.at[a]) for a in range(n)]
        for cp in loads:
            cp.start()
        for a in range(n):
            loads[a].wait()
            stage_out[a][...] = stage_in[a][...].astype(stage_out[a].dtype)
            stores[a].start()
        for cp in stores:
            cp.wait()

    return pl.pallas_call(
        body, name="place_own", in_specs=[ANY] * n, out_specs=[ANY] * n,
        out_shape=[jax.ShapeDtypeStruct((N_DEV,) + shapes[a], shards[a][2]) for a in range(n)],
        scratch_shapes=[*[pltpu.VMEM(shapes[a], shards[a][0].dtype) for a in range(n)],
                        *[pltpu.VMEM(shapes[a], shards[a][2]) for a in range(n)],
                        pltpu.SemaphoreType.DMA((n,)), pltpu.SemaphoreType.DMA((n,))],
        compiler_params=pltpu.CompilerParams(vmem_limit_bytes=40 * MIB),
    )(*[s for s, _, _ in shards])


SIBLING = 1
CHIP_PEERS = (2, 4, 6)
FIRST_PEERS = (SIBLING,) + CHIP_PEERS


def _gather_start(lands):
    n = len(lands)

    def body(*refs):
        lnd, send, recv = refs[:n], refs[n:2 * n], refs[2 * n:3 * n]
        me = _flat(*_coords())
        for a in range(n):
            for j, k in enumerate(FIRST_PEERS):
                pltpu.make_async_remote_copy(
                    src_ref=lnd[a].at[me], dst_ref=lnd[a].at[me], send_sem=send[a].at[j],
                    recv_sem=recv[a].at[j], device_id=_peer(k), device_id_type=MESH_ID).start()

    sems = pltpu.SemaphoreType.DMA((len(FIRST_PEERS),))
    outs = pl.pallas_call(
        body, name="gather_start",
        out_shape=(*[sems] * (2 * n), *[_hbm_like(l) for l in lands]),
        in_specs=[HBM] * n, out_specs=(*[SEM] * (2 * n), *[HBM] * n),
        input_output_aliases={i: 2 * n + i for i in range(n)},
        compiler_params=pltpu.CompilerParams(has_side_effects=EFFECT),
    )(*[_hbm(l) for l in lands])
    return outs[:n], outs[n:2 * n], outs[2 * n:]


def _gather_pass_on(name, land, send, recv, after):
    def body(l_ref, send_ref, recv_ref, after_ref, send2, recv2, l_out):
        sibling = _peer(SIBLING)
        for j, k in enumerate(CHIP_PEERS):
            cp = pltpu.make_async_remote_copy(
                src_ref=l_ref.at[0], dst_ref=l_ref.at[0], send_sem=send_ref.at[1 + j], recv_sem=recv_ref.at[1 + j],
                device_id=_peer(k), device_id_type=MESH_ID)
            cp.wait_send()
            cp.wait_recv()
            blk = _flat(*_peer(k))
            pltpu.make_async_remote_copy(
                src_ref=l_ref.at[blk], dst_ref=l_ref.at[blk], send_sem=send2.at[j], recv_sem=recv2.at[j],
                device_id=sibling, device_id_type=MESH_ID).start()

    sems = pltpu.SemaphoreType.DMA((len(CHIP_PEERS),))
    return pl.pallas_call(
        body, name=name, out_shape=(sems, sems, _hbm_like(land)),
        in_specs=(HBM, SEM, SEM, ANY), out_specs=(SEM, SEM, HBM), input_output_aliases={0: 2},
        compiler_params=pltpu.CompilerParams(has_side_effects=EFFECT),
    )(land, send, recv, after)


def _gather_wait(name, land, send, recv, send2, recv2, after):
    def body(l_ref, send_ref, recv_ref, send2_ref, recv2_ref, after_ref, l_out):
        sibling = _peer(SIBLING)
        own = pltpu.make_async_remote_copy(
            src_ref=l_ref.at[0], dst_ref=l_ref.at[0], send_sem=send_ref.at[0], recv_sem=recv_ref.at[0],
            device_id=sibling, device_id_type=MESH_ID)
        own.wait_send()
        own.wait_recv()
        for j in range(len(CHIP_PEERS)):
            cp = pltpu.make_async_remote_copy(
                src_ref=l_ref.at[0], dst_ref=l_ref.at[0], send_sem=send2_ref.at[j], recv_sem=recv2_ref.at[j],
                device_id=sibling, device_id_type=MESH_ID)
            cp.wait_send()
            cp.wait_recv()

    return pl.pallas_call(
        body, name=name, out_shape=_hbm_like(land),
        in_specs=(HBM, SEM, SEM, SEM, SEM, ANY), out_specs=HBM, input_output_aliases={0: 0},
        compiler_params=pltpu.CompilerParams(has_side_effects=EFFECT),
    )(land, send, recv, send2, recv2, after)


def _scatter_start(name, grads, lands, layer):
    n = len(grads)

    def body(*refs):
        g, lnd, send, recv = refs[:n], refs[n:2 * n], refs[2 * n:3 * n], refs[3 * n:4 * n]
        for a in range(n):
            for k in range(1, N_DEV):
                to = _peer(k)
                pltpu.make_async_remote_copy(
                    src_ref=g[a].at[_flat(*to)], dst_ref=lnd[a].at[layer, k - 1], send_sem=send[a].at[k - 1],
                    recv_sem=recv[a].at[k - 1], device_id=to, device_id_type=MESH_ID).start()

    outs = pl.pallas_call(
        body, name=name,
        out_shape=(*[_peer_sems()] * (2 * n), *[_hbm_like(g) for g in grads], *[_hbm_like(l) for l in lands]),
        in_specs=[HBM] * (2 * n), out_specs=(*[SEM] * (2 * n), *[HBM] * (2 * n)),
        input_output_aliases={i: 2 * n + i for i in range(2 * n)},
        compiler_params=pltpu.CompilerParams(has_side_effects=EFFECT),
    )(*[_hbm(g) for g in grads], *[_hbm(l) for l in lands])
    return outs[:n], outs[n:2 * n], outs[2 * n:3 * n], outs[3 * n:]


def _scatter_wait(name, grads, lands, sends, recvs, after):
    n, nw = len(grads), len(lands)

    def body(*refs):
        g, lnd = refs[:n], refs[n:n + nw]
        send, recv = refs[n + nw:2 * n + nw], refs[2 * n + nw:3 * n + nw]
        for a in range(n):
            for k in range(1, N_DEV):
                cp = pltpu.make_async_remote_copy(
                    src_ref=g[a].at[0], dst_ref=lnd[a // DEPTH].at[a % DEPTH, 0],
                    send_sem=send[a].at[k - 1], recv_sem=recv[a].at[k - 1],
                    device_id=_peer(k), device_id_type=MESH_ID)
                cp.wait_send()
                cp.wait_recv()

    outs = pl.pallas_call(
        body, name=name,
        out_shape=(*[_hbm_like(g) for g in grads], *[_hbm_like(l) for l in lands]),
        in_specs=(*[HBM] * (n + nw), *[SEM] * (2 * n), ANY), out_specs=[HBM] * (n + nw),
        input_output_aliases={i: i for i in range(n + nw)},
        compiler_params=pltpu.CompilerParams(has_side_effects=EFFECT),
    )(*grads, *lands, *sends, *recvs, after)
    return outs[n:]


def _own_blocks(grads_f32):
    nw = len(grads_f32)
    flat32 = [g for w in grads_f32 for g in w]
    n = len(flat32)

    def body(*refs):
        g32, own, stage = refs[:n], refs[n:n + nw], refs[n + nw:2 * n + nw]
        in_sems, out_sems = refs[2 * n + nw], refs[2 * n + nw + 1]
        me = _flat(*_coords())
        loads = [pltpu.make_async_copy(g32[a].at[me], stage[a], in_sems.at[a]) for a in range(n)]
        stores = [pltpu.make_async_copy(stage[a], own[a // DEPTH].at[a % DEPTH], out_sems.at[a]) for a in range(n)]
        for cp in loads:
            cp.start()
        for a in range(n):
            loads[a].wait()
            stores[a].start()
        for cp in stores:
            cp.wait()

    return pl.pallas_call(
        body, name="own_blocks", in_specs=[ANY] * n, out_specs=[ANY] * nw,
        out_shape=[jax.ShapeDtypeStruct((DEPTH,) + w[0].shape[1:], F32) for w in grads_f32],
        scratch_shapes=[*[pltpu.VMEM(g.shape[1:], F32) for g in flat32],
                        pltpu.SemaphoreType.DMA((n,)), pltpu.SemaphoreType.DMA((n,))],
        compiler_params=pltpu.CompilerParams(vmem_limit_bytes=24 * MIB),
    )(*flat32)


def _all_reduce_small(part, dep):
    rows = part.shape[0]
    br = rows // N_DEV
    assert br * N_DEV == rows and br % 8 == 0

    def body(p_ref, dep_ref, o_ref, slots, send1, recv1, send2, recv2):
        me = _flat(*_coords())

        def block(ref, d):
            return ref.at[pl.ds(pl.multiple_of(d * br, 8), br), :]

        slots[me] = p_ref[pl.ds(pl.multiple_of(me * br, 8), br), :]
        scatter, gather = [], []
        for k in range(1, N_DEV):
            to = _peer(k)
            scatter.append(pltpu.make_async_remote_copy(
                src_ref=block(p_ref, _flat(*to)), dst_ref=slots.at[me],
                send_sem=send1.at[k - 1], recv_sem=recv1.at[k - 1], device_id=to, device_id_type=MESH_ID))
            gather.append(pltpu.make_async_remote_copy(
                src_ref=block(o_ref, me), dst_ref=block(o_ref, me),
                send_sem=send2.at[k - 1], recv_sem=recv2.at[k - 1], device_id=to, device_id_type=MESH_ID))
        for cp in scatter:
            cp.start()
        for cp in scatter:
            cp.wait()
        acc = slots[0]
        for d in range(1, N_DEV):
            acc = acc + slots[d]
        o_ref[pl.ds(pl.multiple_of(me * br, 8), br), :] = acc
        for cp in gather:
            cp.start()
        for cp in gather:
            cp.wait()

    return pl.pallas_call(
        body, name="all_reduce_small",
        in_specs=[pl.BlockSpec(memory_space=pltpu.VMEM), ANY], out_specs=pl.BlockSpec(memory_space=pltpu.VMEM),
        out_shape=jax.ShapeDtypeStruct(part.shape, F32),
        scratch_shapes=[pltpu.VMEM((N_DEV, br, LANES), F32)] + [pltpu.SemaphoreType.DMA((N_DEV - 1,))] * 4,
        compiler_params=pltpu.CompilerParams(vmem_limit_bytes=24 * MIB),
    )(part, dep)


def _adam_math(w, g, m, v):
    m = ADAM_B1 * m + (1.0 - ADAM_B1) * g
    v = ADAM_B2 * v + (1.0 - ADAM_B2) * (g * g)
    m_hat = m / (1.0 - ADAM_B1 ** ADAM_STEP)
    v_hat = v / (1.0 - ADAM_B2 ** ADAM_STEP)
    delta = -ADAM_LR * (m_hat / (jnp.sqrt(v_hat) + ADAM_EPS) + ADAM_WD * w)
    return delta, m, v


def _adam_sharded(name, own, recv, w, m, v):
    _, r, c = w.shape
    tr = min(r, 256)

    def body(own_ref, recv_ref, w_ref, m_ref, v_ref, g_ref, d_ref, nm_ref, nv_ref):
        g = own_ref[...]
        for k in range(N_DEV - 1):
            g = g + recv_ref[k].astype(F32)
        delta, nm, nv = _adam_math(w_ref[...], g, m_ref[...], v_ref[...])
        g_ref[...] = g
        d_ref[...] = delta
        nm_ref[...] = nm
        nv_ref[...] = nv

    blk = pl.BlockSpec((None, tr, c), lambda l, i: (l, i, 0))
    shp = jax.ShapeDtypeStruct(w.shape, F32)
    return pl.pallas_call(
        body, name=name, grid=(DEPTH, r // tr),
        in_specs=[blk, pl.BlockSpec((None, N_DEV - 1, tr, c), lambda l, i: (l, 0, i, 0)), blk, blk, blk],
        out_specs=[blk] * 4, out_shape=[shp] * 4,
        compiler_params=_cparams(32, ("arbitrary", "arbitrary")),
    )(own, recv, w, m, v)


def _adam_small(gs, ws, ms, vs):
    n = len(gs)

    def body(*refs):
        g, w, m, v = refs[:n], refs[n:2 * n], refs[2 * n:3 * n], refs[3 * n:4 * n]
        d, nm, nv = refs[4 * n:5 * n], refs[5 * n:6 * n], refs[6 * n:7 * n]
        for a in range(n):
            delta, new_m, new_v = _adam_math(w[a][...], g[a][...], m[a][...], v[a][...])
            d[a][...] = delta
            nm[a][...] = new_m
            nv[a][...] = new_v

    specs = [_full(a.shape) for a in gs]
    shapes = [jax.ShapeDtypeStruct(a.shape, F32) for a in gs]
    outs = pl.pallas_call(
        body, name="adam_small", grid=(1,),
        in_specs=specs * 4, out_specs=specs * 3, out_shape=shapes * 3,
        compiler_params=_cparams(24),
    )(*gs, *ws, *ms, *vs)
    return outs[:n], outs[n:2 * n], outs[2 * n:]


SMALL = ["norm1_g", "sgu_ln_g", "sgu_ln_b", "sgu_w", "sgu_b", "conv_b", "conv_ln_g", "conv_ln_b", "norm2_g",
         "final_g"]


def _pack(arrays):
    flat = jnp.concatenate([a.reshape(-1) for a in arrays])
    pad = (-flat.shape[0]) % (N_DEV * 8 * LANES)
    return jnp.pad(flat, (0, pad)).reshape(-1, LANES)


def _unpack(packed, shapes):
    flat = packed.reshape(-1)
    out, off = [], 0
    for s in shapes:
        size = 1
        for d in s:
            size *= d
        out.append(flat[off:off + size].reshape(s))
        off += size
    return out


def kernel(x, norm1_g, w_in, sgu_ln_g, sgu_ln_b, sgu_w, sgu_b, conv_w, conv_b, conv_ln_g, conv_ln_b, w_out, norm2_g, w_ff1, w_ff2, final_g, loss_target, m_norm1_g, m_w_in, m_sgu_ln_g, m_sgu_ln_b, m_sgu_w, m_sgu_b, m_conv_w, m_conv_b, m_conv_ln_g, m_conv_ln_b, m_w_out, m_norm2_g, m_w_ff1, m_w_ff2, m_final_g, v_norm1_g, v_w_in, v_sgu_ln_g, v_sgu_ln_b, v_sgu_w, v_sgu_b, v_conv_w, v_conv_b, v_conv_ln_g, v_conv_ln_b, v_w_out, v_norm2_g, v_w_ff1, v_w_ff2, v_final_g):
    x2d = x.reshape(T, D)
    tgt = loss_target.reshape(T, D)
    cw_shard = conv_w.reshape(CONV_W, LANES)

    gnames, shards = [], []
    for l in range(DEPTH):
        for k, w in (("w_in", w_in), ("w_out", w_out), ("w_ff1", w_ff1), ("w_ff2", w_ff2)):
            gnames.append(f"{k}{l}")
            shards.append((w, l, BF16))
        if l == 0:
            gnames.insert(1, "conv_w")
            shards.insert(1, (cw_shard, None, F32))
    sends, recvs, lands = _gather_start(_place_own(shards))
    gidx = {k: i for i, k in enumerate(gnames)}

    passed = {}

    def pass_on(k, after):
        i = gidx[k]
        passed[k] = _gather_pass_on("gather_pass_on_" + k, lands[i], sends[i], recvs[i], after)

    def gathered(k, after):
        i = gidx[k]
        send2, recv2, land = passed[k]
        return _gather_wait("gather_wait_" + k, land, sends[i], recvs[i], send2, recv2, after)

    saved = []
    xl = x2d
    cw_full = None
    for l in range(DEPTH):
        g1 = norm1_g[l].reshape(1, D)
        g2 = norm2_g[l].reshape(1, D)
        lg, lb = sgu_ln_g[l].reshape(1, D_A), sgu_ln_b[l].reshape(1, D_A)
        bst = sgu_b[l].T
        cb = conv_b[l].reshape(1, D_B)
        clg, clb = conv_ln_g[l].reshape(1, D_B), conv_ln_b[l].reshape(1, D_B)
        if l == 0:
            pass_on("w_in0", xl)
        wg_in = gathered(f"w_in{l}", xl)
        if l == 0:
            pass_on("conv_w", xl)
        pass_on(f"w_out{l}", xl)
        h, proj = _fwd_in(xl, g1, wg_in)
        if l == 0:
            cw_full = gathered("conv_w", proj).reshape(N_DEV, DEPTH, CONV_W, D_B // N_DEV).transpose(
                1, 2, 0, 3).reshape(DEPTH, CONV_W, D_B)
        wout = gathered(f"w_out{l}", proj).reshape(D, D)
        if l > 0:
            pass_on(f"w_ff1{l}", proj)
        mix, c, g, x1, h2 = _mixer_fwd(proj, lg, lb, sgu_w[l], bst, cw_full[l], cb, clg, clb, wout, xl, g2)
        if l == 0:
            pass_on(f"w_ff1{l}", h2)
        else:
            pass_on(f"w_ff2{l}", h2)
        wg_ff1 = gathered(f"w_ff1{l}", h2)
        r = _fwd_ff1(h2, wg_ff1)
        if l == 0:
            pass_on(f"w_ff2{l}", r)
        if l + 1 < DEPTH:
            pass_on(f"w_in{l + 1}", r)
        w2 = gathered(f"w_ff2{l}", r).reshape(D_FF, D)
        saved.append(dict(x=xl, h=h, proj=proj, mix=mix, c=c, g=g, x1=x1, h2=h2, r=r, wg_in=wg_in, wout=wout,
                          wg_ff1=wg_ff1, w2=w2, g1=g1, g2=g2, lg=lg, lb=lb, bst=bst, clg=clg, clb=clb))
        if l + 1 < DEPTH:
            xl = _fwd_ff2(r, w2, x1)
        else:
            dx, dxb, loss_part, d_final_g = _fwd_ff2_loss(r, w2, x1, final_g.reshape(1, D), tgt)

    names = ["w_in", "w_out", "w_ff1", "w_ff2"]
    block = dict(w_in=(D, IN_COLS // N_DEV), w_out=(D // N_DEV, D), w_ff1=(D, D_FF // N_DEV), w_ff2=(D_FF // N_DEV, D))
    land = {k: lax.empty((DEPTH, N_DEV - 1) + block[k], BF16) for k in names}
    big32 = {k: [None] * DEPTH for k in names}
    big16 = {k: [None] * DEPTH for k in names}
    ssend = {k: [None] * DEPTH for k in names}
    srecv = {k: [None] * DEPTH for k in names}

    def send_grads(l, grads):
        ks = list(grads)
        for k in ks:
            big32[k][l] = grads[k][0].reshape((N_DEV,) + block[k])
        sends, recvs, g16s, lands = _scatter_start(
            "scatter_start_" + "_".join(ks) + str(l), [grads[k][1].reshape((N_DEV,) + block[k]) for k in ks],
            [land[k] for k in ks], l)
        for i, k in enumerate(ks):
            ssend[k][l], srecv[k][l], big16[k][l], land[k] = sends[i], recvs[i], g16s[i], lands[i]

    small = {}
    d_conv_w = [None] * DEPTH
    for l in reversed(range(DEPTH)):
        s = saved[l]
        send_grads(l, dict(w_ff2=_wgrad("wgrad_ff2", s["r"], dxb, D, square_a=True)))
        df1, dx1, dx1b, dg2 = _bwd_mlp(dxb, dx, s["w2"], s["r"], s["wg_ff1"], s["x1"], s["g2"], big16["w_ff2"][l])
        send_grads(l, dict(w_ff1=_wgrad("wgrad_ff1", s["h2"], df1, D_FF // N_DEV),
                           w_out=_wgrad("wgrad_out", s["mix"], dx1b, D)))
        dpa, dc, dlg, dlb, dwm, dbs, dcb, dclg, dclb = _mixer_bwd_a(
            dx1b, s["wout"], s["proj"], s["c"], s["lg"], s["lb"], sgu_w[l], s["bst"], s["clg"], s["clb"],
            big16["w_out"][l])
        dproj, d_conv_w[l], dx, dxb, dg1 = _mixer_bwd_b(dc, s["g"], s["proj"], cw_full[l], dpa, s["wg_in"], s["x"],
                                                         s["g1"], dx1)
        send_grads(l, dict(w_in=_wgrad("wgrad_in", s["h"], dproj, IN_COLS // N_DEV)))
        small[l] = dict(norm1_g=dg1, sgu_ln_g=dlg, sgu_ln_b=dlb, sgu_w=dwm, sgu_b=dbs, conv_b=dcb, conv_ln_g=dclg,
                        conv_ln_b=dclb, norm2_g=dg2)
    grad_x = dx.reshape(1, T, D)

    own = dict(zip(names, _own_blocks([big32[k] for k in names])))
    wmv = dict(w_in=(w_in, m_w_in, v_w_in), w_out=(w_out, m_w_out, v_w_out), w_ff1=(w_ff1, m_w_ff1, v_w_ff1),
               w_ff2=(w_ff2, m_w_ff2, v_w_ff2))
    res = {}

    def update(tag, ks, after):
        recv = _scatter_wait("scatter_wait_" + tag, [g for k in ks for g in big16[k]], [land[k] for k in ks],
                             [q for k in ks for q in ssend[k]], [q for k in ks for q in srecv[k]], after)
        for k, rk in zip(ks, recv):
            res[k] = _adam_sharded("adam_" + k, own[k], rk, *wmv[k])

    update("early", ["w_ff2", "w_ff1", "w_out"], big16["w_in"][0])

    rep = dict(norm1_g=(norm1_g, m_norm1_g, v_norm1_g), sgu_ln_g=(sgu_ln_g, m_sgu_ln_g, v_sgu_ln_g),
               sgu_ln_b=(sgu_ln_b, m_sgu_ln_b, v_sgu_ln_b), sgu_w=(sgu_w, m_sgu_w, v_sgu_w),
               sgu_b=(sgu_b, m_sgu_b, v_sgu_b), conv_b=(conv_b, m_conv_b, v_conv_b),
               conv_ln_g=(conv_ln_g, m_conv_ln_g, v_conv_ln_g), conv_ln_b=(conv_ln_b, m_conv_ln_b, v_conv_ln_b),
               norm2_g=(norm2_g, m_norm2_g, v_norm2_g), final_g=(final_g, m_final_g, v_final_g))
    parts = []
    for k in SMALL:
        if k == "final_g":
            parts.append(d_final_g.reshape(rep[k][0].shape))
        else:
            parts.append(jnp.stack([small[l][k].reshape(rep[k][0].shape[1:]) for l in range(DEPTH)]))
    parts.append(jnp.stack(d_conv_w))
    parts.append(loss_part[0, 0:1])
    shapes = [p.shape for p in parts]
    packed_sum = _all_reduce_small(_pack(parts), res["w_out"][0])
    update("late", ["w_in"], packed_sum)
    summed = _unpack(packed_sum, shapes)
    loss = summed[-1][0]
    me = _flat(*_coords())
    g_conv_w = lax.dynamic_slice_in_dim(summed[-2], me * (D_B // N_DEV), D_B // N_DEV, axis=2)
    g_small = summed[:-2] + [g_conv_w]
    keys = SMALL + ["conv_w"]
    rep["conv_w"] = (conv_w, m_conv_w, v_conv_w)
    def two_d(a):
        return a.reshape(1, -1) if a.ndim == 1 else a

    d_s, nm_s, nv_s = _adam_small([two_d(g) for g in g_small], *[[two_d(rep[k][i]) for k in keys] for i in range(3)])
    for i, k in enumerate(keys):
        shape = rep[k][0].shape
        res[k] = (g_small[i], d_s[i].reshape(shape), nm_s[i].reshape(shape), nv_s[i].reshape(shape))

    order = ["norm1_g", "w_in", "sgu_ln_g", "sgu_ln_b", "sgu_w", "sgu_b", "conv_w", "conv_b", "conv_ln_g",
             "conv_ln_b", "w_out", "norm2_g", "w_ff1", "w_ff2", "final_g"]
    return (loss, grad_x, *[res[k][0] for k in order], *[res[k][1] for k in order],
            *[res[k][2] for k in order], *[res[k][3] for k in order])
```

```python
import functools

import jax
import jax.numpy as jnp
from jax import lax
from jax.experimental import pallas as pl
from jax.experimental.pallas import tpu as pltpu

F32 = jnp.float32
BF16 = jnp.bfloat16

N_DEV = 8
DEPTH = 2
T = 4096
D = 1024
D_A = 512
D_B = 512
CHUNK = 128
H_A = 4
H_B = 4
CONV_W = 31
HALO = 32
D_FF = 4096
IN_COLS = 2048
EPS = 1e-6

ADAM_LR = 0.001
ADAM_B1 = 0.9
ADAM_B2 = 0.999
ADAM_EPS = 1e-08
ADAM_WD = 0.01
ADAM_STEP = 10

TM = 512
TM_FWD = 1024
TT_WGRAD = 4096
TM_MIX = 256
TM_SGU = 512
RB = 64
CONV_CHAINS = 2
LANES = 128
MIB = 1024 * 1024
SCOPED_VMEM_MIB = 60

SQRT_HALF = 0.7071067811865476
INV_SQRT_2PI = 0.3989422804014327

MESH_ID = pl.DeviceIdType.MESH
HBM = pl.BlockSpec(memory_space=pltpu.HBM)
SEM = pl.BlockSpec(memory_space=pltpu.SEMAPHORE)
EFFECT = pltpu.SideEffectType.DATAFLOW_SIDE_EFFECTING


def _cparams(vmem_mib, sem=("arbitrary",)):
    assert vmem_mib <= SCOPED_VMEM_MIB
    return pltpu.CompilerParams(dimension_semantics=sem, vmem_limit_bytes=SCOPED_VMEM_MIB * MIB)


def _full(shape):
    return pl.BlockSpec(shape, lambda *_: (0,) * len(shape))


def _rows(tm, cols):
    return pl.BlockSpec((tm, cols), lambda i: (i, 0))


def _gelu(x):
    cdf = 0.5 * (1.0 + lax.erf(x * SQRT_HALF))
    return x * cdf, cdf


def _gelu_grad(x, cdf):
    return cdf + x * (INV_SQRT_2PI * jnp.exp(-0.5 * x * x))


def _sigmoid(x):
    return 1.0 / (1.0 + jnp.exp(-x))


def _ln(x):
    mu = jnp.mean(x, axis=-1, keepdims=True)
    xc = x - mu
    rstd = lax.rsqrt(jnp.mean(xc * xc, axis=-1, keepdims=True) + EPS)
    return xc * rstd, rstd


def _ln_bwd(dyh, xhat, rstd):
    return rstd * (dyh - jnp.mean(dyh, axis=-1, keepdims=True) - xhat * jnp.mean(dyh * xhat, axis=-1, keepdims=True))


def _rms(x):
    return lax.rsqrt(jnp.mean(x * x, axis=-1, keepdims=True) + EPS)


def _rms_bwd(dh, x, r, g):
    n = x * r
    dn = dh * g
    dx = r * (dn - n * jnp.mean(dn * n, axis=-1, keepdims=True))
    return dx, jnp.sum(dh * n, axis=0, keepdims=True)


def _dot(a, b):
    return jnp.dot(a, b, preferred_element_type=F32)


def _dot_nt(a, b):
    return lax.dot_general(a, b, (((1,), (1,)), ((), ())), preferred_element_type=F32)


def _dot_tn(a, b):
    return lax.dot_general(a, b, (((0,), (0,)), ((), ())), preferred_element_type=F32)


def _tril_mask():
    r = lax.broadcasted_iota(jnp.int32, (CHUNK, CHUNK), 0)
    c = lax.broadcasted_iota(jnp.int32, (CHUNK, CHUNK), 1)
    return r >= c


def _fwd_in(x, g1, wg):
    bn = wg.shape[2]

    def body(x_ref, g_ref, w_ref, h_ref, p_ref):
        xv = x_ref[...]
        h = (xv * _rms(xv) * g_ref[...]).astype(BF16)
        h_ref[...] = h
        for j in range(N_DEV):
            p_ref[:, j * bn:(j + 1) * bn] = _dot(h, w_ref[j])

    return pl.pallas_call(
        body, name="fwd_in", grid=(T // TM_FWD,),
        in_specs=[_rows(TM_FWD, D), _full((1, D)), _full(wg.shape)],
        out_specs=[_rows(TM_FWD, D), _rows(TM_FWD, IN_COLS)],
        out_shape=[jax.ShapeDtypeStruct((T, D), BF16), jax.ShapeDtypeStruct((T, IN_COLS), F32)],
        compiler_params=_cparams(32),
    )(*map(_hbm, (x, g1, wg)))


def _fill_shift_buffer(sh_ref, row0, value):
    for q in range(sh_ref.shape[1]):
        sh_ref[0, q, row0:row0 + value.shape[0], :] = value[:, q * LANES:(q + 1) * LANES]


def _build_shifts(sh_ref):
    rows = sh_ref.shape[2]
    for p in range(1, 8):
        for q in range(sh_ref.shape[1]):
            sh_ref[p, q, 0:rows - 8, :] = sh_ref[0, q, p:p + rows - 8, :]


def _shifted(sh_ref, q, base, off):
    start = base + (off - off % 8)
    if not isinstance(start, int):
        start = pl.multiple_of(start, 8)
    return sh_ref[off % 8, q, pl.ds(start, RB), :]


def _conv_taps(sh_ref, w_ref, q, base, first_tap_row, step):
    cols = slice(q * LANES, (q + 1) * LANES)
    acc = [jnp.zeros((RB, LANES), F32) for _ in range(CONV_CHAINS)]
    for k in range(CONV_W):
        term = _shifted(sh_ref, q, base, first_tap_row + step * k) * w_ref[k:k + 1, cols]
        acc[k % CONV_CHAINS] = acc[k % CONV_CHAINS] + term
    return functools.reduce(lambda a, b: a + b, acc)


def _mixer_fwd(proj, lg, lb, wm, bst, cw, cb, clg, clb, wout, x, g2):
    tm = TM_MIX
    hb = tm // HALO

    def body(p_ref, ph_ref, lg_ref, lb_ref, wm_ref, bs_ref, cw_ref, cb_ref, clg_ref, clb_ref, wo_ref, x_ref, g2_ref,
             mix_ref, c_ref, g_ref, x1_ref, h2_ref, gbuf):
        i = pl.program_id(0)
        u, _ = _gelu(p_ref[:, 0:D_A])
        vg, _ = _gelu(p_ref[:, D_A:2 * D_A])
        xhat, _ = _ln(vg)
        v = (xhat * lg_ref[...] + lb_ref[...]).astype(BF16)
        mask = _tril_mask()
        for h in range(H_A):
            hc = slice(h * CHUNK, (h + 1) * CHUNK)
            wmh = jnp.where(mask, wm_ref[h], 0.0).astype(BF16)
            for c in range(tm // CHUNK):
                rc = slice(c * CHUNK, (c + 1) * CHUNK)
                mixed = _dot(wmh, v[rc, hc]) + bs_ref[:, h:h + 1]
                mix_ref[rc, hc] = (u[rc, hc] * mixed).astype(BF16)
        x1_ref[...] = x_ref[...] + _dot(mix_ref[:, 0:D_A], wo_ref[0:D_A, :])

        g = p_ref[:, 2 * D_A:2 * D_A + D_B] * _sigmoid(p_ref[:, 2 * D_A + D_B:IN_COLS])
        g_ref[...] = g
        gh = ph_ref[:, 0:D_B] * _sigmoid(ph_ref[:, D_B:2 * D_B])
        _fill_shift_buffer(gbuf, 0, jnp.where(i > 0, gh, 0.0))
        _fill_shift_buffer(gbuf, HALO, g)
        _build_shifts(gbuf)
        for q in range(H_B):
            cols = slice(q * LANES, (q + 1) * LANES)
            for rb in range(tm // RB):
                acc = _conv_taps(gbuf, cw_ref, q, rb * RB, HALO - (CONV_W - 1), 1)
                c_ref[rb * RB:(rb + 1) * RB, cols] = acc + cb_ref[:, cols]
        for q in range(H_B):
            cols = slice(q * LANES, (q + 1) * LANES)
            chat, _ = _ln(c_ref[:, cols])
            z = chat * clg_ref[:, cols] + clb_ref[:, cols]
            mix_ref[:, D_A + q * LANES:D_A + (q + 1) * LANES] = (z * _sigmoid(z)).astype(BF16)

        x1 = x1_ref[...] + _dot(mix_ref[:, D_A:D], wo_ref[D_A:D, :])
        x1_ref[...] = x1
        h2_ref[...] = (x1 * _rms(x1) * g2_ref[...]).astype(BF16)

    vec = _full((1, D_A))
    return pl.pallas_call(
        body, name="mixer_fwd", grid=(T // tm,),
        in_specs=[_rows(tm, IN_COLS),
                  pl.BlockSpec((HALO, 2 * D_B), lambda i: (jnp.maximum(i * hb - 1, 0), 1)),
                  vec, vec, _full((H_A, CHUNK, CHUNK)), _full((CHUNK, H_A)),
                  _full((CONV_W, D_B)), vec, vec, vec,
                  pl.BlockSpec((D, D), lambda i: (0, 0), pipeline_mode=pl.Buffered(1)), _rows(tm, D), _full((1, D))],
        out_specs=[_rows(tm, D), _rows(tm, D_B), _rows(tm, D_B), _rows(tm, D), _rows(tm, D)],
        out_shape=[jax.ShapeDtypeStruct((T, D), BF16), jax.ShapeDtypeStruct((T, D_B), F32),
                   jax.ShapeDtypeStruct((T, D_B), F32), jax.ShapeDtypeStruct((T, D), F32),
                   jax.ShapeDtypeStruct((T, D), BF16)],
        scratch_shapes=[pltpu.VMEM((8, H_B, HALO + tm, LANES), F32)],
        compiler_params=_cparams(40),
    )(*map(_hbm, (proj, proj, lg, lb, wm, bst, cw, cb, clg, clb, wout, x, g2)))


def _fwd_ff1(h2, wg):
    bn = wg.shape[2]

    def body(h_ref, w_ref, r_ref):
        h = h_ref[...]
        for j in range(N_DEV):
            r_ref[:, j * bn:(j + 1) * bn] = jnp.maximum(_dot(h, w_ref[j]), 0.0).astype(BF16)

    return pl.pallas_call(
        body, name="fwd_ff1", grid=(T // TM_FWD,),
        in_specs=[_rows(TM_FWD, D), _full(wg.shape)],
        out_specs=_rows(TM_FWD, D_FF),
        out_shape=jax.ShapeDtypeStruct((T, D_FF), BF16),
        compiler_params=_cparams(48),
    )(h2, wg)


def _fwd_ff2(r, w2, x1):
    def body(r_ref, w_ref, x_ref, o_ref):
        rv = r_ref[...]
        o_ref[...] = x_ref[...] + _dot(rv * rv, w_ref[...])

    return pl.pallas_call(
        body, name="fwd_ff2", grid=(T // TM_FWD,),
        in_specs=[_rows(TM_FWD, D_FF), _full((D_FF, D)), _rows(TM_FWD, D)],
        out_specs=_rows(TM_FWD, D),
        out_shape=jax.ShapeDtypeStruct((T, D), F32),
        compiler_params=_cparams(48),
    )(r, w2, x1)


def _fwd_ff2_loss(r, w2, x1, gf, tgt):
    def body(r_ref, w_ref, x_ref, g_ref, t_ref, dx_ref, dxb_ref, loss_ref, dg_ref):
        i = pl.program_id(0)

        @pl.when(i == 0)
        def _():
            loss_ref[...] = jnp.zeros(loss_ref.shape, F32)
            dg_ref[...] = jnp.zeros(dg_ref.shape, F32)

        rv = r_ref[...]
        xv = x_ref[...] + _dot(rv * rv, w_ref[...])
        rn = _rms(xv)
        diff = xv * rn * g_ref[...] - t_ref[...]
        loss_ref[...] += 0.5 * jnp.sum(jnp.mean(diff * diff, axis=-1, keepdims=True), axis=0, keepdims=True)
        dx, dg = _rms_bwd(diff * (1.0 / D), xv, rn, g_ref[...])
        dx_ref[...] = dx
        dxb_ref[...] = dx.astype(BF16)
        dg_ref[...] += dg

    return pl.pallas_call(
        body, name="fwd_ff2_loss", grid=(T // TM,),
        in_specs=[_rows(TM, D_FF), pl.BlockSpec((D_FF, D), lambda i: (0, 0), pipeline_mode=pl.Buffered(1)),
                  _rows(TM, D), _full((1, D)), _rows(TM, D)],
        out_specs=[_rows(TM, D), _rows(TM, D), _full((8, LANES)), _full((1, D))],
        out_shape=[jax.ShapeDtypeStruct((T, D), F32), jax.ShapeDtypeStruct((T, D), BF16),
                   jax.ShapeDtypeStruct((8, LANES), F32), jax.ShapeDtypeStruct((1, D), F32)],
        compiler_params=_cparams(40),
    )(*map(_hbm, (r, w2, x1, gf, tgt)))


def _bwd_mlp(dxb, dres, w2, r, wg1, x1, g2, dep):
    bn = wg1.shape[2]

    def body(d_ref, dres_ref, w2_ref, r_ref, w1_ref, x_ref, g_ref, dep_ref, df1_ref, dx_ref, dxb_ref, dg_ref):
        i = pl.program_id(0)

        @pl.when(i == 0)
        def _():
            dg_ref[...] = jnp.zeros(dg_ref.shape, F32)

        d = d_ref[...]
        dh = jnp.zeros((TM, D), F32)
        for j in range(N_DEV):
            cols = slice(j * bn, (j + 1) * bn)
            df1 = (2.0 * r_ref[:, cols].astype(F32) * _dot_nt(d, w2_ref[cols, :])).astype(BF16)
            df1_ref[:, cols] = df1
            dh = dh + _dot_nt(df1, w1_ref[j])
        xv = x_ref[...]
        dxn, dg = _rms_bwd(dh, xv, _rms(xv), g_ref[...])
        dx = dres_ref[...] + dxn
        dx_ref[...] = dx
        dxb_ref[...] = dx.astype(BF16)
        dg_ref[...] += dg

    once = dict(pipeline_mode=pl.Buffered(1))
    return pl.pallas_call(
        body, name="bwd_mlp", grid=(T // TM,),
        in_specs=[_rows(TM, D), _rows(TM, D), pl.BlockSpec((D_FF, D), lambda i: (0, 0), **once), _rows(TM, D_FF),
                  pl.BlockSpec(wg1.shape, lambda i: (0, 0, 0), **once), _rows(TM, D), _full((1, D)), HBM],
        out_specs=[_rows(TM, D_FF), _rows(TM, D), _rows(TM, D), _full((1, D))],
        out_shape=[jax.ShapeDtypeStruct((T, D_FF), BF16), jax.ShapeDtypeStruct((T, D), F32),
                   jax.ShapeDtypeStruct((T, D), BF16), jax.ShapeDtypeStruct((1, D), F32)],
        compiler_params=_cparams(56),
    )(*map(_hbm, (dxb, dres, w2, r, wg1, x1, g2, dep)))


def _mixer_bwd_a(dxb, wout, proj, c, lg, lb, wm, bst, clg, clb, dep):
    tm = TM_SGU
    n_tiles = T // tm

    def body(dx_ref, wo_ref, p_ref, c_ref, lg_ref, lb_ref, wm_ref, bs_ref, clg_ref, clb_ref, dep_ref,
             dpa_ref, dc_ref, dlg_ref, dlb_ref, dwm_ref, dbs_ref, dcb_ref, dclg_ref, dclb_ref,
             dv_buf, db_acc):
        i = pl.program_id(0)

        @pl.when(i == 0)
        def _():
            for ref in (dlg_ref, dlb_ref, dwm_ref, dbs_ref, dcb_ref, dclg_ref, dclb_ref, db_acc):
                ref[...] = jnp.zeros(ref.shape, F32)

        dmix = _dot_nt(dx_ref[...], wo_ref[...])
        ua = p_ref[:, 0:D_A]
        va = p_ref[:, D_A:2 * D_A]
        u, cdf_u = _gelu(ua)
        vg, cdf_v = _gelu(va)
        xhat, rstd = _ln(vg)
        v = (xhat * lg_ref[...] + lb_ref[...]).astype(BF16)
        mask = _tril_mask()
        for h in range(H_A):
            hc = slice(h * CHUNK, (h + 1) * CHUNK)
            wmh = jnp.where(mask, wm_ref[h], 0.0).astype(BF16)
            for cidx in range(tm // CHUNK):
                rc = slice(cidx * CHUNK, (cidx + 1) * CHUNK)
                vb = v[rc, hc]
                mixed = _dot(wmh, vb) + bs_ref[:, h:h + 1]
                da = dmix[rc, hc]
                dpa_ref[rc, hc] = (da * mixed * _gelu_grad(ua[rc, hc], cdf_u[rc, hc])).astype(BF16)
                dmixed = da * u[rc, hc]
                dmb = dmixed.astype(BF16)
                dv_buf[rc, hc] = _dot_tn(wmh, dmb)
                dwm_ref[h] += _dot_nt(dmb, vb)
                db_acc[:, hc] += dmixed
        dv = dv_buf[...]
        dlb_ref[...] += jnp.sum(dv, axis=0, keepdims=True)
        dlg_ref[...] += jnp.sum(dv * xhat, axis=0, keepdims=True)
        dvg = _ln_bwd(dv * lg_ref[...], xhat, rstd)
        dpa_ref[:, D_A:2 * D_A] = (dvg * _gelu_grad(va, cdf_v)).astype(BF16)

        for q in range(H_B):
            cols = slice(q * LANES, (q + 1) * LANES)
            chat, crstd = _ln(c_ref[:, cols])
            z = chat * clg_ref[:, cols] + clb_ref[:, cols]
            sg = _sigmoid(z)
            dz = dmix[:, D_A + q * LANES:D_A + (q + 1) * LANES] * (sg * (1.0 + z * (1.0 - sg)))
            dclb_ref[:, cols] += jnp.sum(dz, axis=0, keepdims=True)
            dclg_ref[:, cols] += jnp.sum(dz * chat, axis=0, keepdims=True)
            dc = _ln_bwd(dz * clg_ref[:, cols], chat, crstd)
            dc_ref[:, cols] = dc
            dcb_ref[:, cols] += jnp.sum(dc, axis=0, keepdims=True)

        @pl.when(i == n_tiles - 1)
        def _():
            for h in range(H_A):
                hc = slice(h * CHUNK, (h + 1) * CHUNK)
                dwm_ref[h] = jnp.where(mask, dwm_ref[h], 0.0)
                dbs_ref[h:h + 1, :] = jnp.sum(db_acc[:, hc].T, axis=0, keepdims=True)

    vec = _full((1, D_A))
    vshape = jax.ShapeDtypeStruct((1, D_A), F32)
    return pl.pallas_call(
        body, name="mixer_bwd_a", grid=(n_tiles,),
        in_specs=[_rows(tm, D), _full((D, D)), _rows(tm, IN_COLS), _rows(tm, D_B), vec, vec,
                  _full((H_A, CHUNK, CHUNK)), _full((CHUNK, H_A)), vec, vec, HBM],
        out_specs=[_rows(tm, 2 * D_A), _rows(tm, D_B), vec, vec, _full((H_A, CHUNK, CHUNK)),
                   _full((H_A, CHUNK)), vec, vec, vec],
        out_shape=[jax.ShapeDtypeStruct((T, 2 * D_A), BF16), jax.ShapeDtypeStruct((T, D_B), F32), vshape, vshape,
                   jax.ShapeDtypeStruct((H_A, CHUNK, CHUNK), F32), jax.ShapeDtypeStruct((H_A, CHUNK), F32),
                   vshape, vshape, vshape],
        scratch_shapes=[pltpu.VMEM((tm, D_A), F32), pltpu.VMEM((CHUNK, D_A), F32)],
        compiler_params=_cparams(32),
    )(*map(_hbm, (dxb, wout, proj, c, lg, lb, wm, bst, clg, clb, dep)))


def _mixer_bwd_b(dc, g, proj, cw, dpa, wg_in, x, g1, dres):
    tm = TM_MIX
    n_tiles = T // tm
    hb = tm // HALO
    bn = wg_in.shape[2]

    def body(dc_ref, dch_ref, g_ref, gh_ref, p_ref, cw_ref, dpa_ref, win_ref, x_ref, g1_ref, dres_ref,
             dp_ref, dcw_ref, dx_ref, dxb_ref, dg1_ref, dcbuf, gbuf, dwacc):
        i = pl.program_id(0)

        @pl.when(i == 0)
        def _():
            dwacc[...] = jnp.zeros(dwacc.shape, F32)
            dg1_ref[...] = jnp.zeros(dg1_ref.shape, F32)

        _fill_shift_buffer(dcbuf, 0, dc_ref[...])
        _fill_shift_buffer(dcbuf, tm, jnp.where(i < n_tiles - 1, dch_ref[...], 0.0))
        _fill_shift_buffer(gbuf, 0, jnp.where(i > 0, gh_ref[...], 0.0))
        _fill_shift_buffer(gbuf, HALO, g_ref[...])
        _build_shifts(dcbuf)
        _build_shifts(gbuf)
        dp_ref[:, 0:2 * D_A] = dpa_ref[...]
        for q in range(H_B):
            cols = slice(q * LANES, (q + 1) * LANES)

            def row_block(rb, carry, q=q, cols=cols):
                base = pl.multiple_of(rb * RB, RB)
                rows = pl.ds(base, RB)
                dg = _conv_taps(dcbuf, cw_ref, q, base, CONV_W - 1, -1)
                val = p_ref[rows, cols]
                sg = _sigmoid(p_ref[rows, D_B + q * LANES:D_B + (q + 1) * LANES])
                dp_ref[rows, 2 * D_A + q * LANES:2 * D_A + (q + 1) * LANES] = (dg * sg).astype(BF16)
                dp_ref[rows, 2 * D_A + D_B + q * LANES:2 * D_A + D_B + (q + 1) * LANES] = (
                    dg * val * sg * (1.0 - sg)).astype(BF16)
                dcv = dcbuf[0, q, rows, :]
                parts = []
                for k in range(CONV_W):
                    prod = dcv * _shifted(gbuf, q, base, HALO - (CONV_W - 1) + k)
                    parts.append(jnp.sum(prod.reshape(RB // 8, 8, LANES), axis=0))
                for k in range(CONV_W):
                    dwacc[k * 8:(k + 1) * 8, cols] += parts[k]
                return carry

            lax.fori_loop(0, tm // RB, row_block, 0)

        dh = jnp.zeros((tm, D), F32)
        for j in range(N_DEV):
            dh = dh + _dot_nt(dp_ref[:, j * bn:(j + 1) * bn], win_ref[j])
        xv = x_ref[...]
        dxn, dg = _rms_bwd(dh, xv, _rms(xv), g1_ref[...])
        dx = dres_ref[...] + dxn
        dx_ref[...] = dx
        dxb_ref[...] = dx.astype(BF16)
        dg1_ref[...] += dg

        @pl.when(i == n_tiles - 1)
        def _():
            for k in range(CONV_W):
                dcw_ref[k:k + 1, :] = jnp.sum(dwacc[k * 8:(k + 1) * 8, :], axis=0, keepdims=True)

    return pl.pallas_call(
        body, name="mixer_bwd_b", grid=(n_tiles,),
        in_specs=[_rows(tm, D_B),
                  pl.BlockSpec((HALO, D_B), lambda i: (jnp.minimum((i + 1) * hb, T // HALO - 1), 0)),
                  _rows(tm, D_B),
                  pl.BlockSpec((HALO, D_B), lambda i: (jnp.maximum(i * hb - 1, 0), 0)),
                  pl.BlockSpec((tm, 2 * D_B), lambda i: (i, 1)),
                  _full((CONV_W, D_B)), _rows(tm, 2 * D_A),
                  pl.BlockSpec(wg_in.shape, lambda i: (0, 0, 0), pipeline_mode=pl.Buffered(1)),
                  _rows(tm, D), _full((1, D)), _rows(tm, D)],
        out_specs=[_rows(tm, IN_COLS), _full((CONV_W, D_B)), _rows(tm, D), _rows(tm, D), _full((1, D))],
        out_shape=[jax.ShapeDtypeStruct((T, IN_COLS), BF16), jax.ShapeDtypeStruct((CONV_W, D_B), F32),
                   jax.ShapeDtypeStruct((T, D), F32), jax.ShapeDtypeStruct((T, D), BF16),
                   jax.ShapeDtypeStruct((1, D), F32)],
        scratch_shapes=[pltpu.VMEM((8, H_B, tm + HALO, LANES), F32), pltpu.VMEM((8, H_B, HALO + tm, LANES), F32),
                        pltpu.VMEM((CONV_W * 8, D_B), F32)],
        compiler_params=_cparams(40),
    )(*map(_hbm, (dc, dc, g, g, proj, cw, dpa, wg_in, x, g1, dres)))


def _wgrad(name, a, g, bn, square_a=False):
    k = a.shape[1]
    n = g.shape[1]
    tk = min(k, 1024)
    tn = min(n, max(bn, 1024))
    nsub = tn // bn
    tt = TT_WGRAD
    nt = T // tt

    def body(a_ref, g_ref, o_ref, ob_ref):
        t = pl.program_id(2)

        @pl.when(t == 0)
        def _():
            o_ref[...] = jnp.zeros(o_ref.shape, F32)

        av = a_ref[...]
        if square_a:
            av = av * av
        for s in range(nsub):
            o_ref[s] += _dot_tn(av, g_ref[:, s * bn:(s + 1) * bn])

        @pl.when(t == nt - 1)
        def _():
            ob_ref[...] = o_ref[...].astype(BF16)

    ospec = pl.BlockSpec((nsub, tk, bn), lambda ki, ni, ti: (ni, ki, 0))
    return pl.pallas_call(
        body, name=name, grid=(k // tk, n // tn, nt),
        in_specs=[pl.BlockSpec((tt, tk), lambda ki, ni, ti: (ti, ki)),
                  pl.BlockSpec((tt, tn), lambda ki, ni, ti: (ti, ni))],
        out_specs=[ospec, ospec],
        out_shape=[jax.ShapeDtypeStruct((n // bn, k, bn), F32), jax.ShapeDtypeStruct((n // bn, k, bn), BF16)],
        compiler_params=_cparams(40, ("arbitrary", "arbitrary", "arbitrary")),
    )(a, g)


def _coords():
    return lax.axis_index("x"), lax.axis_index("y"), lax.axis_index("c")


def _flat(x, y, c):
    return 4 * x + 2 * y + c


def _peer(k):
    x, y, c = _coords()
    return (x ^ ((k >> 2) & 1), y ^ ((k >> 1) & 1), c ^ (k & 1))


def _hbm(a):
    return pltpu.with_memory_space_constraint(a, pltpu.HBM)


def _hbm_like(a):
    return pltpu.HBM(a.shape, a.dtype)


def _peer_sems():
    return pltpu.SemaphoreType.DMA((N_DEV - 1,))


def _place_own(shards):
    n = len(shards)
    shapes = [(s.shape if l is None else s.shape[1:]) for s, l, _ in shards]

    def body(*refs):
        ins, outs = refs[:n], refs[n:2 * n]
        stage_in, stage_out = refs[2 * n:3 * n], refs[3 * n:4 * n]
        in_sems, out_sems = refs[4 * n], refs[4 * n + 1]
        me = _flat(*_coords())
        srcs = [ins[a] if shards[a][1] is None else ins[a].at[shards[a][1]] for a in range(n)]
        loads = [pltpu.make_async_copy(srcs[a], stage_in[a], in_sems.at[a]) for a in range(n)]
        stores = [pltpu.make_async_copy(stage_out[a], outs[a].at[me], out_sems.at[a]) for a in range(n)]
        for cp in loads:
            cp.start()
        for a in range(n):
            loads[a].wait()
            stage_out[a][...] = stage_in[a][...].astype(stage_out[a].dtype)
            stores[a].start()
        for cp in stores:
            cp.wait()

    return pl.pallas_call(
        body, name="place_own", in_specs=[HBM] * n, out_specs=[HBM] * n,
        out_shape=[pltpu.HBM((N_DEV,) + shapes[a], shards[a][2]) for a in range(n)],
        scratch_shapes=[*[pltpu.VMEM(shapes[a], shards[a][0].dtype) for a in range(n)],
                        *[pltpu.VMEM(shapes[a], shards[a][2]) for a in range(n)],
                        pltpu.SemaphoreType.DMA((n,)), pltpu.SemaphoreType.DMA((n,))],
        compiler_params=pltpu.CompilerParams(vmem_limit_bytes=40 * MIB),
    )(*[_hbm(s) for s, _, _ in shards])


SIBLING = 1
CHIP_PEERS = (2, 4, 6)
FIRST_PEERS = (SIBLING,) + CHIP_PEERS


def _gather_start(lands):
    n = len(lands)

    def body(*refs):
        lnd, send, recv = refs[:n], refs[n:2 * n], refs[2 * n:3 * n]
        me = _flat(*_coords())
        for a in range(n):
            for j, k in enumerate(FIRST_PEERS):
                pltpu.make_async_remote_copy(
                    src_ref=lnd[a].at[me], dst_ref=lnd[a].at[me], send_sem=send[a].at[j],
                    recv_sem=recv[a].at[j], device_id=_peer(k), device_id_type=MESH_ID).start()

    sems = pltpu.SemaphoreType.DMA((len(FIRST_PEERS),))
    outs = pl.pallas_call(
        body, name="gather_start",
        out_shape=(*[sems] * (2 * n), *[_hbm_like(l) for l in lands]),
        in_specs=[HBM] * n, out_specs=(*[SEM] * (2 * n), *[HBM] * n),
        input_output_aliases={i: 2 * n + i for i in range(n)},
        compiler_params=pltpu.CompilerParams(has_side_effects=EFFECT),
    )(*[_hbm(l) for l in lands])
    return outs[:n], outs[n:2 * n], outs[2 * n:]


def _gather_pass_on(name, land, send, recv, after):
    def body(l_ref, send_ref, recv_ref, after_ref, send2, recv2, l_out):
        sibling = _peer(SIBLING)
        for j, k in enumerate(CHIP_PEERS):
            cp = pltpu.make_async_remote_copy(
                src_ref=l_ref.at[0], dst_ref=l_ref.at[0], send_sem=send_ref.at[1 + j], recv_sem=recv_ref.at[1 + j],
                device_id=_peer(k), device_id_type=MESH_ID)
            cp.wait_send()
            cp.wait_recv()
            blk = _flat(*_peer(k))
            pltpu.make_async_remote_copy(
                src_ref=l_ref.at[blk], dst_ref=l_ref.at[blk], send_sem=send2.at[j], recv_sem=recv2.at[j],
                device_id=sibling, device_id_type=MESH_ID).start()

    sems = pltpu.SemaphoreType.DMA((len(CHIP_PEERS),))
    return pl.pallas_call(
        body, name=name, out_shape=(sems, sems, _hbm_like(land)),
        in_specs=(HBM, SEM, SEM, HBM), out_specs=(SEM, SEM, HBM), input_output_aliases={0: 2},
        compiler_params=pltpu.CompilerParams(has_side_effects=EFFECT),
    )(land, send, recv, after)


def _gather_wait(name, land, send, recv, send2, recv2, after):
    def body(l_ref, send_ref, recv_ref, send2_ref, recv2_ref, after_ref, l_out):
        sibling = _peer(SIBLING)
        own = pltpu.make_async_remote_copy(
            src_ref=l_ref.at[0], dst_ref=l_ref.at[0], send_sem=send_ref.at[0], recv_sem=recv_ref.at[0],
            device_id=sibling, device_id_type=MESH_ID)
        own.wait_send()
        own.wait_recv()
        for j in range(len(CHIP_PEERS)):
            cp = pltpu.make_async_remote_copy(
                src_ref=l_ref.at[0], dst_ref=l_ref.at[0], send_sem=send2_ref.at[j], recv_sem=recv2_ref.at[j],
                device_id=sibling, device_id_type=MESH_ID)
            cp.wait_send()
            cp.wait_recv()

    return pl.pallas_call(
        body, name=name, out_shape=_hbm_like(land),
        in_specs=(HBM, SEM, SEM, SEM, SEM, HBM), out_specs=HBM, input_output_aliases={0: 0},
        compiler_params=pltpu.CompilerParams(has_side_effects=EFFECT),
    )(land, send, recv, send2, recv2, after)


def _scatter_start(name, grads, lands, layer):
    n = len(grads)

    def body(*refs):
        g, lnd, send, recv = refs[:n], refs[n:2 * n], refs[2 * n:3 * n], refs[3 * n:4 * n]
        for a in range(n):
            for k in range(1, N_DEV):
                to = _peer(k)
                pltpu.make_async_remote_copy(
                    src_ref=g[a].at[_flat(*to)], dst_ref=lnd[a].at[layer, k - 1], send_sem=send[a].at[k - 1],
                    recv_sem=recv[a].at[k - 1], device_id=to, device_id_type=MESH_ID).start()

    outs = pl.pallas_call(
        body, name=name,
        out_shape=(*[_peer_sems()] * (2 * n), *[_hbm_like(g) for g in grads], *[_hbm_like(l) for l in lands]),
        in_specs=[HBM] * (2 * n), out_specs=(*[SEM] * (2 * n), *[HBM] * (2 * n)),
        input_output_aliases={i: 2 * n + i for i in range(2 * n)},
        compiler_params=pltpu.CompilerParams(has_side_effects=EFFECT),
    )(*[_hbm(g) for g in grads], *[_hbm(l) for l in lands])
    return outs[:n], outs[n:2 * n], outs[2 * n:3 * n], outs[3 * n:]


def _scatter_wait(name, grads, lands, sends, recvs, after):
    n, nw = len(grads), len(lands)

    def body(*refs):
        g, lnd = refs[:n], refs[n:n + nw]
        send, recv = refs[n + nw:2 * n + nw], refs[2 * n + nw:3 * n + nw]
        for a in range(n):
            for k in range(1, N_DEV):
                cp = pltpu.make_async_remote_copy(
                    src_ref=g[a].at[0], dst_ref=lnd[a // DEPTH].at[a % DEPTH, 0],
                    send_sem=send[a].at[k - 1], recv_sem=recv[a].at[k - 1],
                    device_id=_peer(k), device_id_type=MESH_ID)
                cp.wait_send()
                cp.wait_recv()

    outs = pl.pallas_call(
        body, name=name,
        out_shape=(*[_hbm_like(g) for g in grads], *[_hbm_like(l) for l in lands]),
        in_specs=(*[HBM] * (n + nw), *[SEM] * (2 * n), HBM), out_specs=[HBM] * (n + nw),
        input_output_aliases={i: i for i in range(n + nw)},
        compiler_params=pltpu.CompilerParams(has_side_effects=EFFECT),
    )(*grads, *lands, *sends, *recvs, after)
    return outs[n:]


def _own_blocks(grads_f32):
    nw = len(grads_f32)
    flat32 = [g for w in grads_f32 for g in w]
    n = len(flat32)

    def body(*refs):
        g32, own, stage = refs[:n], refs[n:n + nw], refs[n + nw:2 * n + nw]
        in_sems, out_sems = refs[2 * n + nw], refs[2 * n + nw + 1]
        me = _flat(*_coords())
        loads = [pltpu.make_async_copy(g32[a].at[me], stage[a], in_sems.at[a]) for a in range(n)]
        stores = [pltpu.make_async_copy(stage[a], own[a // DEPTH].at[a % DEPTH], out_sems.at[a]) for a in range(n)]
        for cp in loads:
            cp.start()
        for a in range(n):
            loads[a].wait()
            stores[a].start()
        for cp in stores:
            cp.wait()

    return pl.pallas_call(
        body, name="own_blocks", in_specs=[HBM] * n, out_specs=[HBM] * nw,
        out_shape=[pltpu.HBM((DEPTH,) + w[0].shape[1:], F32) for w in grads_f32],
        scratch_shapes=[*[pltpu.VMEM(g.shape[1:], F32) for g in flat32],
                        pltpu.SemaphoreType.DMA((n,)), pltpu.SemaphoreType.DMA((n,))],
        compiler_params=pltpu.CompilerParams(vmem_limit_bytes=24 * MIB),
    )(*[_hbm(g) for g in flat32])


def _all_reduce_small(part, dep):
    rows = part.shape[0]
    br = rows // N_DEV
    assert br * N_DEV == rows and br % 8 == 0

    def body(p_ref, dep_ref, o_ref, slots, send1, recv1, send2, recv2):
        me = _flat(*_coords())

        def block(ref, d):
            return ref.at[pl.ds(pl.multiple_of(d * br, 8), br), :]

        slots[me] = p_ref[pl.ds(pl.multiple_of(me * br, 8), br), :]
        scatter, gather = [], []
        for k in range(1, N_DEV):
            to = _peer(k)
            scatter.append(pltpu.make_async_remote_copy(
                src_ref=block(p_ref, _flat(*to)), dst_ref=slots.at[me],
                send_sem=send1.at[k - 1], recv_sem=recv1.at[k - 1], device_id=to, device_id_type=MESH_ID))
            gather.append(pltpu.make_async_remote_copy(
                src_ref=block(o_ref, me), dst_ref=block(o_ref, me),
                send_sem=send2.at[k - 1], recv_sem=recv2.at[k - 1], device_id=to, device_id_type=MESH_ID))
        for cp in scatter:
            cp.start()
        for cp in scatter:
            cp.wait()
        acc = slots[0]
        for d in range(1, N_DEV):
            acc = acc + slots[d]
        o_ref[pl.ds(pl.multiple_of(me * br, 8), br), :] = acc
        for cp in gather:
            cp.start()
        for cp in gather:
            cp.wait()

    return pl.pallas_call(
        body, name="all_reduce_small",
        in_specs=[pl.BlockSpec(memory_space=pltpu.VMEM), HBM], out_specs=pl.BlockSpec(memory_space=pltpu.VMEM),
        out_shape=jax.ShapeDtypeStruct(part.shape, F32),
        scratch_shapes=[pltpu.VMEM((N_DEV, br, LANES), F32)] + [pltpu.SemaphoreType.DMA((N_DEV - 1,))] * 4,
        compiler_params=pltpu.CompilerParams(vmem_limit_bytes=24 * MIB),
    )(part, _hbm(dep))


def _adam_math(w, g, m, v):
    m = ADAM_B1 * m + (1.0 - ADAM_B1) * g
    v = ADAM_B2 * v + (1.0 - ADAM_B2) * (g * g)
    m_hat = m / (1.0 - ADAM_B1 ** ADAM_STEP)
    v_hat = v / (1.0 - ADAM_B2 ** ADAM_STEP)
    delta = -ADAM_LR * (m_hat / (jnp.sqrt(v_hat) + ADAM_EPS) + ADAM_WD * w)
    return delta, m, v


def _adam_sharded(name, own, recv, w, m, v):
    _, r, c = w.shape
    tr = min(r, 256)

    def body(own_ref, recv_ref, w_ref, m_ref, v_ref, g_ref, d_ref, nm_ref, nv_ref):
        g = own_ref[...]
        for k in range(N_DEV - 1):
            g = g + recv_ref[k].astype(F32)
        delta, nm, nv = _adam_math(w_ref[...], g, m_ref[...], v_ref[...])
        g_ref[...] = g
        d_ref[...] = delta
        nm_ref[...] = nm
        nv_ref[...] = nv

    blk = pl.BlockSpec((None, tr, c), lambda l, i: (l, i, 0))
    shp = jax.ShapeDtypeStruct(w.shape, F32)
    return pl.pallas_call(
        body, name=name, grid=(DEPTH, r // tr),
        in_specs=[blk, pl.BlockSpec((None, N_DEV - 1, tr, c), lambda l, i: (l, 0, i, 0)), blk, blk, blk],
        out_specs=[blk] * 4, out_shape=[shp] * 4,
        compiler_params=_cparams(32, ("arbitrary", "arbitrary")),
    )(*[_hbm(a) for a in (own, recv, w, m, v)])


def _adam_small(gs, ws, ms, vs):
    n = len(gs)

    def body(*refs):
        g, w, m, v = refs[:n], refs[n:2 * n], refs[2 * n:3 * n], refs[3 * n:4 * n]
        d, nm, nv = refs[4 * n:5 * n], refs[5 * n:6 * n], refs[6 * n:7 * n]
        for a in range(n):
            delta, new_m, new_v = _adam_math(w[a][...], g[a][...], m[a][...], v[a][...])
            d[a][...] = delta
            nm[a][...] = new_m
            nv[a][...] = new_v

    specs = [_full(a.shape) for a in gs]
    shapes = [jax.ShapeDtypeStruct(a.shape, F32) for a in gs]
    outs = pl.pallas_call(
        body, name="adam_small", grid=(1,),
        in_specs=specs * 4, out_specs=specs * 3, out_shape=shapes * 3,
        compiler_params=_cparams(24),
    )(*gs, *ws, *ms, *vs)
    return outs[:n], outs[n:2 * n], outs[2 * n:]


SMALL = ["norm1_g", "sgu_ln_g", "sgu_ln_b", "sgu_w", "sgu_b", "conv_b", "conv_ln_g", "conv_ln_b", "norm2_g",
         "final_g"]


def _pack(arrays):
    flat = jnp.concatenate([a.reshape(-1) for a in arrays])
    pad = (-flat.shape[0]) % (N_DEV * 8 * LANES)
    return jnp.pad(flat, (0, pad)).reshape(-1, LANES)


def _unpack(packed, shapes):
    flat = packed.reshape(-1)
    out, off = [], 0
    for s in shapes:
        size = 1
        for d in s:
            size *= d
        out.append(flat[off:off + size].reshape(s))
        off += size
    return out


def kernel(x, norm1_g, w_in, sgu_ln_g, sgu_ln_b, sgu_w, sgu_b, conv_w, conv_b, conv_ln_g, conv_ln_b, w_out, norm2_g, w_ff1, w_ff2, final_g, loss_target, m_norm1_g, m_w_in, m_sgu_ln_g, m_sgu_ln_b, m_sgu_w, m_sgu_b, m_conv_w, m_conv_b, m_conv_ln_g, m_conv_ln_b, m_w_out, m_norm2_g, m_w_ff1, m_w_ff2, m_final_g, v_norm1_g, v_w_in, v_sgu_ln_g, v_sgu_ln_b, v_sgu_w, v_sgu_b, v_conv_w, v_conv_b, v_conv_ln_g, v_conv_ln_b, v_w_out, v_norm2_g, v_w_ff1, v_w_ff2, v_final_g):
    x2d = x.reshape(T, D)
    tgt = loss_target.reshape(T, D)
    cw_shard = conv_w.reshape(CONV_W, LANES)

    gnames, shards = [], []
    for l in range(DEPTH):
        for k, w in (("w_in", w_in), ("w_out", w_out), ("w_ff1", w_ff1), ("w_ff2", w_ff2)):
            gnames.append(f"{k}{l}")
            shards.append((w, l, BF16))
        if l == 0:
            gnames.insert(1, "conv_w")
            shards.insert(1, (cw_shard, None, F32))
    sends, recvs, lands = _gather_start(_place_own(shards))
    gidx = {k: i for i, k in enumerate(gnames)}

    passed = {}

    def pass_on(k, after):
        i = gidx[k]
        passed[k] = _gather_pass_on("gather_pass_on_" + k, lands[i], sends[i], recvs[i], after)

    def gathered(k, after):
        i = gidx[k]
        send2, recv2, land = passed[k]
        return _gather_wait("gather_wait_" + k, land, sends[i], recvs[i], send2, recv2, after)

    saved = []
    xl = x2d
    cw_full = None
    for l in range(DEPTH):
        g1 = norm1_g[l].reshape(1, D)
        g2 = norm2_g[l].reshape(1, D)
        lg, lb = sgu_ln_g[l].reshape(1, D_A), sgu_ln_b[l].reshape(1, D_A)
        bst = sgu_b[l].T
        cb = conv_b[l].reshape(1, D_B)
        clg, clb = conv_ln_g[l].reshape(1, D_B), conv_ln_b[l].reshape(1, D_B)
        if l == 0:
            pass_on("w_in0", xl)
        wg_in = gathered(f"w_in{l}", xl)
        if l == 0:
            pass_on("conv_w", xl)
        pass_on(f"w_out{l}", xl)
        h, proj = _fwd_in(xl, g1, wg_in)
        if l == 0:
            cw_full = gathered("conv_w", proj).reshape(N_DEV, DEPTH, CONV_W, D_B // N_DEV).transpose(
                1, 2, 0, 3).reshape(DEPTH, CONV_W, D_B)
        wout = gathered(f"w_out{l}", proj).reshape(D, D)
        if l > 0:
            pass_on(f"w_ff1{l}", proj)
        mix, c, g, x1, h2 = _mixer_fwd(proj, lg, lb, sgu_w[l], bst, cw_full[l], cb, clg, clb, wout, xl, g2)
        if l == 0:
            pass_on(f"w_ff1{l}", h2)
        else:
            pass_on(f"w_ff2{l}", h2)
        wg_ff1 = gathered(f"w_ff1{l}", h2)
        r = _fwd_ff1(h2, wg_ff1)
        if l == 0:
            pass_on(f"w_ff2{l}", r)
        if l + 1 < DEPTH:
            pass_on(f"w_in{l + 1}", r)
        w2 = gathered(f"w_ff2{l}", r).reshape(D_FF, D)
        saved.append(dict(x=xl, h=h, proj=proj, mix=mix, c=c, g=g, x1=x1, h2=h2, r=r, wg_in=wg_in, wout=wout,
                          wg_ff1=wg_ff1, w2=w2, g1=g1, g2=g2, lg=lg, lb=lb, bst=bst, clg=clg, clb=clb))
        if l + 1 < DEPTH:
            xl = _fwd_ff2(r, w2, x1)
        else:
            dx, dxb, loss_part, d_final_g = _fwd_ff2_loss(r, w2, x1, final_g.reshape(1, D), tgt)

    names = ["w_in", "w_out", "w_ff1", "w_ff2"]
    block = dict(w_in=(D, IN_COLS // N_DEV), w_out=(D // N_DEV, D), w_ff1=(D, D_FF // N_DEV), w_ff2=(D_FF // N_DEV, D))
    land = {k: lax.empty((DEPTH, N_DEV - 1) + block[k], BF16) for k in names}
    big32 = {k: [None] * DEPTH for k in names}
    big16 = {k: [None] * DEPTH for k in names}
    ssend = {k: [None] * DEPTH for k in names}
    srecv = {k: [None] * DEPTH for k in names}

    def send_grads(l, grads):
        ks = list(grads)
        for k in ks:
            big32[k][l] = grads[k][0].reshape((N_DEV,) + block[k])
        sends, recvs, g16s, lands = _scatter_start(
            "scatter_start_" + "_".join(ks) + str(l), [grads[k][1].reshape((N_DEV,) + block[k]) for k in ks],
            [land[k] for k in ks], l)
        for i, k in enumerate(ks):
            ssend[k][l], srecv[k][l], big16[k][l], land[k] = sends[i], recvs[i], g16s[i], lands[i]

    small = {}
    d_conv_w = [None] * DEPTH
    for l in reversed(range(DEPTH)):
        s = saved[l]
        send_grads(l, dict(w_ff2=_wgrad("wgrad_ff2", s["r"], dxb, D, square_a=True)))
        df1, dx1, dx1b, dg2 = _bwd_mlp(dxb, dx, s["w2"], s["r"], s["wg_ff1"], s["x1"], s["g2"], big16["w_ff2"][l])
        send_grads(l, dict(w_ff1=_wgrad("wgrad_ff1", s["h2"], df1, D_FF // N_DEV),
                           w_out=_wgrad("wgrad_out", s["mix"], dx1b, D)))
        dpa, dc, dlg, dlb, dwm, dbs, dcb, dclg, dclb = _mixer_bwd_a(
            dx1b, s["wout"], s["proj"], s["c"], s["lg"], s["lb"], sgu_w[l], s["bst"], s["clg"], s["clb"],
            big16["w_out"][l])
        dproj, d_conv_w[l], dx, dxb, dg1 = _mixer_bwd_b(dc, s["g"], s["proj"], cw_full[l], dpa, s["wg_in"], s["x"],
                                                         s["g1"], dx1)
        send_grads(l, dict(w_in=_wgrad("wgrad_in", s["h"], dproj, IN_COLS // N_DEV)))
        small[l] = dict(norm1_g=dg1, sgu_ln_g=dlg, sgu_ln_b=dlb, sgu_w=dwm, sgu_b=dbs, conv_b=dcb, conv_ln_g=dclg,
                        conv_ln_b=dclb, norm2_g=dg2)
    grad_x = dx.reshape(1, T, D)

    own = dict(zip(names, _own_blocks([big32[k] for k in names])))
    wmv = dict(w_in=(w_in, m_w_in, v_w_in), w_out=(w_out, m_w_out, v_w_out), w_ff1=(w_ff1, m_w_ff1, v_w_ff1),
               w_ff2=(w_ff2, m_w_ff2, v_w_ff2))
    res = {}

    def update(tag, ks, after):
        recv = _scatter_wait("scatter_wait_" + tag, [g for k in ks for g in big16[k]], [land[k] for k in ks],
                             [q for k in ks for q in ssend[k]], [q for k in ks for q in srecv[k]], after)
        for k, rk in zip(ks, recv):
            res[k] = _adam_sharded("adam_" + k, own[k], rk, *wmv[k])

    update("early", ["w_ff2", "w_ff1", "w_out"], big16["w_in"][0])

    rep = dict(norm1_g=(norm1_g, m_norm1_g, v_norm1_g), sgu_ln_g=(sgu_ln_g, m_sgu_ln_g, v_sgu_ln_g),
               sgu_ln_b=(sgu_ln_b, m_sgu_ln_b, v_sgu_ln_b), sgu_w=(sgu_w, m_sgu_w, v_sgu_w),
               sgu_b=(sgu_b, m_sgu_b, v_sgu_b), conv_b=(conv_b, m_conv_b, v_conv_b),
               conv_ln_g=(conv_ln_g, m_conv_ln_g, v_conv_ln_g), conv_ln_b=(conv_ln_b, m_conv_ln_b, v_conv_ln_b),
               norm2_g=(norm2_g, m_norm2_g, v_norm2_g), final_g=(final_g, m_final_g, v_final_g))
    parts = []
    for k in SMALL:
        if k == "final_g":
            parts.append(d_final_g.reshape(rep[k][0].shape))
        else:
            parts.append(jnp.stack([small[l][k].reshape(rep[k][0].shape[1:]) for l in range(DEPTH)]))
    parts.append(jnp.stack(d_conv_w))
    parts.append(loss_part[0, 0:1])
    shapes = [p.shape for p in parts]
    packed_sum = _all_reduce_small(_pack(parts), res["w_out"][0])
    update("late", ["w_in"], packed_sum)
    summed = _unpack(packed_sum, shapes)
    loss = summed[-1][0]
    me = _flat(*_coords())
    g_conv_w = lax.dynamic_slice_in_dim(summed[-2], me * (D_B // N_DEV), D_B // N_DEV, axis=2)
    g_small = summed[:-2] + [g_conv_w]
    keys = SMALL + ["conv_w"]
    rep["conv_w"] = (conv_w, m_conv_w, v_conv_w)
    def two_d(a):
        return a.reshape(1, -1) if a.ndim == 1 else a

    d_s, nm_s, nv_s = _adam_small([two_d(g) for g in g_small], *[[two_d(rep[k][i]) for k in keys] for i in range(3)])
    for i, k in enumerate(keys):
        shape = rep[k][0].shape
        res[k] = (g_small[i], d_s[i].reshape(shape), nm_s[i].reshape(shape), nv_s[i].reshape(shape))

    order = ["norm1_g", "w_in", "sgu_ln_g", "sgu_ln_b", "sgu_w", "sgu_b", "conv_w", "conv_b", "conv_ln_g",
             "conv_ln_b", "w_out", "norm2_g", "w_ff1", "w_ff2", "final_g"]
    return (loss, grad_x, *[res[k][0] for k in order], *[res[k][1] for k in order],
            *[res[k][2] for k in order], *[res[k][3] for k in order])
```

```python
import functools

import jax
import jax.numpy as jnp
from jax import lax
from jax.experimental import pallas as pl
from jax.experimental.pallas import tpu as pltpu

F32 = jnp.float32
BF16 = jnp.bfloat16

N_DEV = 8
DEPTH = 2
T = 4096
D = 1024
D_A = 512
D_B = 512
CHUNK = 128
H_A = 4
H_B = 4
CONV_W = 31
HALO = 32
D_FF = 4096
IN_COLS = 2048
EPS = 1e-6

ADAM_LR = 0.001
ADAM_B1 = 0.9
ADAM_B2 = 0.999
ADAM_EPS = 1e-08
ADAM_WD = 0.01
ADAM_STEP = 10

TM = 512
TM_FWD = 1024
TT_WGRAD = 4096
TM_MIX = 256
TM_SGU = 512
RB = 64
CONV_CHAINS = 2
LANES = 128
MIB = 1024 * 1024
SCOPED_VMEM_MIB = 60

SQRT_HALF = 0.7071067811865476
INV_SQRT_2PI = 0.3989422804014327

MESH_ID = pl.DeviceIdType.MESH
HBM = pl.BlockSpec(memory_space=pltpu.HBM)
SEM = pl.BlockSpec(memory_space=pltpu.SEMAPHORE)
EFFECT = pltpu.SideEffectType.DATAFLOW_SIDE_EFFECTING


def _cparams(vmem_mib, sem=("arbitrary",)):
    assert vmem_mib <= SCOPED_VMEM_MIB
    return pltpu.CompilerParams(dimension_semantics=sem, vmem_limit_bytes=SCOPED_VMEM_MIB * MIB)


def _full(shape):
    return pl.BlockSpec(shape, lambda *_: (0,) * len(shape))


def _rows(tm, cols):
    return pl.BlockSpec((tm, cols), lambda i: (i, 0))


def _gelu(x):
    cdf = 0.5 * (1.0 + lax.erf(x * SQRT_HALF))
    return x * cdf, cdf


def _gelu_grad(x, cdf):
    return cdf + x * (INV_SQRT_2PI * jnp.exp(-0.5 * x * x))


def _sigmoid(x):
    return 1.0 / (1.0 + jnp.exp(-x))


def _ln(x):
    mu = jnp.mean(x, axis=-1, keepdims=True)
    xc = x - mu
    rstd = lax.rsqrt(jnp.mean(xc * xc, axis=-1, keepdims=True) + EPS)
    return xc * rstd, rstd


def _ln_bwd(dyh, xhat, rstd):
    return rstd * (dyh - jnp.mean(dyh, axis=-1, keepdims=True) - xhat * jnp.mean(dyh * xhat, axis=-1, keepdims=True))


def _rms(x):
    return lax.rsqrt(jnp.mean(x * x, axis=-1, keepdims=True) + EPS)


def _rms_bwd(dh, x, r, g):
    n = x * r
    dn = dh * g
    dx = r * (dn - n * jnp.mean(dn * n, axis=-1, keepdims=True))
    return dx, jnp.sum(dh * n, axis=0, keepdims=True)


def _dot(a, b):
    return jnp.dot(a, b, preferred_element_type=F32)


def _dot_nt(a, b):
    return lax.dot_general(a, b, (((1,), (1,)), ((), ())), preferred_element_type=F32)


def _dot_tn(a, b):
    return lax.dot_general(a, b, (((0,), (0,)), ((), ())), preferred_element_type=F32)


def _tril_mask():
    r = lax.broadcasted_iota(jnp.int32, (CHUNK, CHUNK), 0)
    c = lax.broadcasted_iota(jnp.int32, (CHUNK, CHUNK), 1)
    return r >= c


def _fwd_in(x, g1, wg):
    bn = wg.shape[2]

    def body(x_ref, g_ref, w_ref, h_ref, p_ref):
        xv = x_ref[...]
        h = (xv * _rms(xv) * g_ref[...]).astype(BF16)
        h_ref[...] = h
        for j in range(N_DEV):
            p_ref[:, j * bn:(j + 1) * bn] = _dot(h, w_ref[j])

    return pl.pallas_call(
        body, name="fwd_in", grid=(T // TM_FWD,),
        in_specs=[_rows(TM_FWD, D), _full((1, D)), _full(wg.shape)],
        out_specs=[_rows(TM_FWD, D), _rows(TM_FWD, IN_COLS)],
        out_shape=[jax.ShapeDtypeStruct((T, D), BF16), jax.ShapeDtypeStruct((T, IN_COLS), F32)],
        compiler_params=_cparams(32),
    )(*map(_hbm, (x, g1, wg)))


def _fill_shift_buffer(sh_ref, row0, value):
    for q in range(sh_ref.shape[1]):
        sh_ref[0, q, row0:row0 + value.shape[0], :] = value[:, q * LANES:(q + 1) * LANES]


def _build_shifts(sh_ref):
    rows = sh_ref.shape[2]
    for p in range(1, 8):
        for q in range(sh_ref.shape[1]):
            sh_ref[p, q, 0:rows - 8, :] = sh_ref[0, q, p:p + rows - 8, :]


def _shifted(sh_ref, q, base, off):
    start = base + (off - off % 8)
    if not isinstance(start, int):
        start = pl.multiple_of(start, 8)
    return sh_ref[off % 8, q, pl.ds(start, RB), :]


def _conv_taps(sh_ref, w_ref, q, base, first_tap_row, step):
    cols = slice(q * LANES, (q + 1) * LANES)
    acc = [jnp.zeros((RB, LANES), F32) for _ in range(CONV_CHAINS)]
    for k in range(CONV_W):
        term = _shifted(sh_ref, q, base, first_tap_row + step * k) * w_ref[k:k + 1, cols]
        acc[k % CONV_CHAINS] = acc[k % CONV_CHAINS] + term
    return functools.reduce(lambda a, b: a + b, acc)


def _mixer_fwd(proj, lg, lb, wm, bst, cw, cb, clg, clb, wout, x, g2):
    tm = TM_MIX
    hb = tm // HALO

    def body(p_ref, ph_ref, lg_ref, lb_ref, wm_ref, bs_ref, cw_ref, cb_ref, clg_ref, clb_ref, wo_ref, x_ref, g2_ref,
             mix_ref, c_ref, g_ref, x1_ref, h2_ref, gbuf):
        i = pl.program_id(0)
        u, _ = _gelu(p_ref[:, 0:D_A])
        vg, _ = _gelu(p_ref[:, D_A:2 * D_A])
        xhat, _ = _ln(vg)
        v = (xhat * lg_ref[...] + lb_ref[...]).astype(BF16)
        mask = _tril_mask()
        for h in range(H_A):
            hc = slice(h * CHUNK, (h + 1) * CHUNK)
            wmh = jnp.where(mask, wm_ref[h], 0.0).astype(BF16)
            for c in range(tm // CHUNK):
                rc = slice(c * CHUNK, (c + 1) * CHUNK)
                mixed = _dot(wmh, v[rc, hc]) + bs_ref[:, h:h + 1]
                mix_ref[rc, hc] = (u[rc, hc] * mixed).astype(BF16)
        x1_ref[...] = x_ref[...] + _dot(mix_ref[:, 0:D_A], wo_ref[0:D_A, :])

        g = p_ref[:, 2 * D_A:2 * D_A + D_B] * _sigmoid(p_ref[:, 2 * D_A + D_B:IN_COLS])
        g_ref[...] = g
        gh = ph_ref[:, 0:D_B] * _sigmoid(ph_ref[:, D_B:2 * D_B])
        _fill_shift_buffer(gbuf, 0, jnp.where(i > 0, gh, 0.0))
        _fill_shift_buffer(gbuf, HALO, g)
        _build_shifts(gbuf)
        for q in range(H_B):
            cols = slice(q * LANES, (q + 1) * LANES)
            for rb in range(tm // RB):
                acc = _conv_taps(gbuf, cw_ref, q, rb * RB, HALO - (CONV_W - 1), 1)
                c_ref[rb * RB:(rb + 1) * RB, cols] = acc + cb_ref[:, cols]
        for q in range(H_B):
            cols = slice(q * LANES, (q + 1) * LANES)
            chat, _ = _ln(c_ref[:, cols])
            z = chat * clg_ref[:, cols] + clb_ref[:, cols]
            mix_ref[:, D_A + q * LANES:D_A + (q + 1) * LANES] = (z * _sigmoid(z)).astype(BF16)

        x1 = x1_ref[...] + _dot(mix_ref[:, D_A:D], wo_ref[D_A:D, :])
        x1_ref[...] = x1
        h2_ref[...] = (x1 * _rms(x1) * g2_ref[...]).astype(BF16)

    vec = _full((1, D_A))
    return pl.pallas_call(
        body, name="mixer_fwd", grid=(T // tm,),
        in_specs=[_rows(tm, IN_COLS),
                  pl.BlockSpec((HALO, 2 * D_B), lambda i: (jnp.maximum(i * hb - 1, 0), 1)),
                  vec, vec, _full((H_A, CHUNK, CHUNK)), _full((CHUNK, H_A)),
                  _full((CONV_W, D_B)), vec, vec, vec,
                  pl.BlockSpec((D, D), lambda i: (0, 0), pipeline_mode=pl.Buffered(1)), _rows(tm, D), _full((1, D))],
        out_specs=[_rows(tm, D), _rows(tm, D_B), _rows(tm, D_B), _rows(tm, D), _rows(tm, D)],
        out_shape=[jax.ShapeDtypeStruct((T, D), BF16), jax.ShapeDtypeStruct((T, D_B), F32),
                   jax.ShapeDtypeStruct((T, D_B), F32), jax.ShapeDtypeStruct((T, D), F32),
                   jax.ShapeDtypeStruct((T, D), BF16)],
        scratch_shapes=[pltpu.VMEM((8, H_B, HALO + tm, LANES), F32)],
        compiler_params=_cparams(40),
    )(*map(_hbm, (proj, proj, lg, lb, wm, bst, cw, cb, clg, clb, wout, x, g2)))


def _fwd_ff1(h2, wg):
    bn = wg.shape[2]

    def body(h_ref, w_ref, r_ref):
        h = h_ref[...]
        for j in range(N_DEV):
            r_ref[:, j * bn:(j + 1) * bn] = jnp.maximum(_dot(h, w_ref[j]), 0.0).astype(BF16)

    return pl.pallas_call(
        body, name="fwd_ff1", grid=(T // TM_FWD,),
        in_specs=[_rows(TM_FWD, D), _full(wg.shape)],
        out_specs=_rows(TM_FWD, D_FF),
        out_shape=jax.ShapeDtypeStruct((T, D_FF), BF16),
        compiler_params=_cparams(48),
    )(h2, wg)


def _fwd_ff2(r, w2, x1):
    def body(r_ref, w_ref, x_ref, o_ref):
        rv = r_ref[...]
        o_ref[...] = x_ref[...] + _dot(rv * rv, w_ref[...])

    return pl.pallas_call(
        body, name="fwd_ff2", grid=(T // TM_FWD,),
        in_specs=[_rows(TM_FWD, D_FF), _full((D_FF, D)), _rows(TM_FWD, D)],
        out_specs=_rows(TM_FWD, D),
        out_shape=jax.ShapeDtypeStruct((T, D), F32),
        compiler_params=_cparams(48),
    )(r, w2, x1)


def _fwd_ff2_loss(r, w2, x1, gf, tgt):
    def body(r_ref, w_ref, x_ref, g_ref, t_ref, dx_ref, dxb_ref, loss_ref, dg_ref):
        i = pl.program_id(0)

        @pl.when(i == 0)
        def _():
            loss_ref[...] = jnp.zeros(loss_ref.shape, F32)
            dg_ref[...] = jnp.zeros(dg_ref.shape, F32)

        rv = r_ref[...]
        xv = x_ref[...] + _dot(rv * rv, w_ref[...])
        rn = _rms(xv)
        diff = xv * rn * g_ref[...] - t_ref[...]
        loss_ref[...] += 0.5 * jnp.sum(jnp.mean(diff * diff, axis=-1, keepdims=True), axis=0, keepdims=True)
        dx, dg = _rms_bwd(diff * (1.0 / D), xv, rn, g_ref[...])
        dx_ref[...] = dx
        dxb_ref[...] = dx.astype(BF16)
        dg_ref[...] += dg

    return pl.pallas_call(
        body, name="fwd_ff2_loss", grid=(T // TM,),
        in_specs=[_rows(TM, D_FF), pl.BlockSpec((D_FF, D), lambda i: (0, 0), pipeline_mode=pl.Buffered(1)),
                  _rows(TM, D), _full((1, D)), _rows(TM, D)],
        out_specs=[_rows(TM, D), _rows(TM, D), _full((8, LANES)), _full((1, D))],
        out_shape=[jax.ShapeDtypeStruct((T, D), F32), jax.ShapeDtypeStruct((T, D), BF16),
                   jax.ShapeDtypeStruct((8, LANES), F32), jax.ShapeDtypeStruct((1, D), F32)],
        compiler_params=_cparams(40),
    )(*map(_hbm, (r, w2, x1, gf, tgt)))


def _bwd_mlp(dxb, dres, w2, r, wg1, x1, g2, dep):
    bn = wg1.shape[2]

    def body(d_ref, dres_ref, w2_ref, r_ref, w1_ref, x_ref, g_ref, dep_ref, df1_ref, dx_ref, dxb_ref, dg_ref):
        i = pl.program_id(0)

        @pl.when(i == 0)
        def _():
            dg_ref[...] = jnp.zeros(dg_ref.shape, F32)

        d = d_ref[...]
        dh = jnp.zeros((TM, D), F32)
        for j in range(N_DEV):
            cols = slice(j * bn, (j + 1) * bn)
            df1 = (2.0 * r_ref[:, cols].astype(F32) * _dot_nt(d, w2_ref[cols, :])).astype(BF16)
            df1_ref[:, cols] = df1
            dh = dh + _dot_nt(df1, w1_ref[j])
        xv = x_ref[...]
        dxn, dg = _rms_bwd(dh, xv, _rms(xv), g_ref[...])
        dx = dres_ref[...] + dxn
        dx_ref[...] = dx
        dxb_ref[...] = dx.astype(BF16)
        dg_ref[...] += dg

    once = dict(pipeline_mode=pl.Buffered(1))
    return pl.pallas_call(
        body, name="bwd_mlp", grid=(T // TM,),
        in_specs=[_rows(TM, D), _rows(TM, D), pl.BlockSpec((D_FF, D), lambda i: (0, 0), **once), _rows(TM, D_FF),
                  pl.BlockSpec(wg1.shape, lambda i: (0, 0, 0), **once), _rows(TM, D), _full((1, D)), HBM],
        out_specs=[_rows(TM, D_FF), _rows(TM, D), _rows(TM, D), _full((1, D))],
        out_shape=[jax.ShapeDtypeStruct((T, D_FF), BF16), jax.ShapeDtypeStruct((T, D), F32),
                   jax.ShapeDtypeStruct((T, D), BF16), jax.ShapeDtypeStruct((1, D), F32)],
        compiler_params=_cparams(56),
    )(*map(_hbm, (dxb, dres, w2, r, wg1, x1, g2, dep)))


def _mixer_bwd_a(dxb, wout, proj, c, lg, lb, wm, bst, clg, clb, dep):
    tm = TM_SGU
    n_tiles = T // tm

    def body(dx_ref, wo_ref, p_ref, c_ref, lg_ref, lb_ref, wm_ref, bs_ref, clg_ref, clb_ref, dep_ref,
             dpa_ref, dc_ref, dlg_ref, dlb_ref, dwm_ref, dbs_ref, dcb_ref, dclg_ref, dclb_ref,
             dv_buf, db_acc):
        i = pl.program_id(0)

        @pl.when(i == 0)
        def _():
            for ref in (dlg_ref, dlb_ref, dwm_ref, dbs_ref, dcb_ref, dclg_ref, dclb_ref, db_acc):
                ref[...] = jnp.zeros(ref.shape, F32)

        dmix = _dot_nt(dx_ref[...], wo_ref[...])
        ua = p_ref[:, 0:D_A]
        va = p_ref[:, D_A:2 * D_A]
        u, cdf_u = _gelu(ua)
        vg, cdf_v = _gelu(va)
        xhat, rstd = _ln(vg)
        v = (xhat * lg_ref[...] + lb_ref[...]).astype(BF16)
        mask = _tril_mask()
        for h in range(H_A):
            hc = slice(h * CHUNK, (h + 1) * CHUNK)
            wmh = jnp.where(mask, wm_ref[h], 0.0).astype(BF16)
            for cidx in range(tm // CHUNK):
                rc = slice(cidx * CHUNK, (cidx + 1) * CHUNK)
                vb = v[rc, hc]
                mixed = _dot(wmh, vb) + bs_ref[:, h:h + 1]
                da = dmix[rc, hc]
                dpa_ref[rc, hc] = (da * mixed * _gelu_grad(ua[rc, hc], cdf_u[rc, hc])).astype(BF16)
                dmixed = da * u[rc, hc]
                dmb = dmixed.astype(BF16)
                dv_buf[rc, hc] = _dot_tn(wmh, dmb)
                dwm_ref[h] += _dot_nt(dmb, vb)
                db_acc[:, hc] += dmixed
        dv = dv_buf[...]
        dlb_ref[...] += jnp.sum(dv, axis=0, keepdims=True)
        dlg_ref[...] += jnp.sum(dv * xhat, axis=0, keepdims=True)
        dvg = _ln_bwd(dv * lg_ref[...], xhat, rstd)
        dpa_ref[:, D_A:2 * D_A] = (dvg * _gelu_grad(va, cdf_v)).astype(BF16)

        for q in range(H_B):
            cols = slice(q * LANES, (q + 1) * LANES)
            chat, crstd = _ln(c_ref[:, cols])
            z = chat * clg_ref[:, cols] + clb_ref[:, cols]
            sg = _sigmoid(z)
            dz = dmix[:, D_A + q * LANES:D_A + (q + 1) * LANES] * (sg * (1.0 + z * (1.0 - sg)))
            dclb_ref[:, cols] += jnp.sum(dz, axis=0, keepdims=True)
            dclg_ref[:, cols] += jnp.sum(dz * chat, axis=0, keepdims=True)
            dc = _ln_bwd(dz * clg_ref[:, cols], chat, crstd)
            dc_ref[:, cols] = dc
            dcb_ref[:, cols] += jnp.sum(dc, axis=0, keepdims=True)

        @pl.when(i == n_tiles - 1)
        def _():
            for h in range(H_A):
                hc = slice(h * CHUNK, (h + 1) * CHUNK)
                dwm_ref[h] = jnp.where(mask, dwm_ref[h], 0.0)
                dbs_ref[h:h + 1, :] = jnp.sum(db_acc[:, hc].T, axis=0, keepdims=True)

    vec = _full((1, D_A))
    vshape = jax.ShapeDtypeStruct((1, D_A), F32)
    return pl.pallas_call(
        body, name="mixer_bwd_a", grid=(n_tiles,),
        in_specs=[_rows(tm, D), _full((D, D)), _rows(tm, IN_COLS), _rows(tm, D_B), vec, vec,
                  _full((H_A, CHUNK, CHUNK)), _full((CHUNK, H_A)), vec, vec, HBM],
        out_specs=[_rows(tm, 2 * D_A), _rows(tm, D_B), vec, vec, _full((H_A, CHUNK, CHUNK)),
                   _full((H_A, CHUNK)), vec, vec, vec],
        out_shape=[jax.ShapeDtypeStruct((T, 2 * D_A), BF16), jax.ShapeDtypeStruct((T, D_B), F32), vshape, vshape,
                   jax.ShapeDtypeStruct((H_A, CHUNK, CHUNK), F32), jax.ShapeDtypeStruct((H_A, CHUNK), F32),
                   vshape, vshape, vshape],
        scratch_shapes=[pltpu.VMEM((tm, D_A), F32), pltpu.VMEM((CHUNK, D_A), F32)],
        compiler_params=_cparams(32),
    )(*map(_hbm, (dxb, wout, proj, c, lg, lb, wm, bst, clg, clb, dep)))


def _mixer_bwd_b(dc, g, proj, cw, dpa, wg_in, x, g1, dres):
    tm = TM_MIX
    n_tiles = T // tm
    hb = tm // HALO
    bn = wg_in.shape[2]

    def body(dc_ref, dch_ref, g_ref, gh_ref, p_ref, cw_ref, dpa_ref, win_ref, x_ref, g1_ref, dres_ref,
             dp_ref, dcw_ref, dx_ref, dxb_ref, dg1_ref, dcbuf, gbuf, dwacc):
        i = pl.program_id(0)

        @pl.when(i == 0)
        def _():
            dwacc[...] = jnp.zeros(dwacc.shape, F32)
            dg1_ref[...] = jnp.zeros(dg1_ref.shape, F32)

        _fill_shift_buffer(dcbuf, 0, dc_ref[...])
        _fill_shift_buffer(dcbuf, tm, jnp.where(i < n_tiles - 1, dch_ref[...], 0.0))
        _fill_shift_buffer(gbuf, 0, jnp.where(i > 0, gh_ref[...], 0.0))
        _fill_shift_buffer(gbuf, HALO, g_ref[...])
        _build_shifts(dcbuf)
        _build_shifts(gbuf)
        dp_ref[:, 0:2 * D_A] = dpa_ref[...]
        for q in range(H_B):
            cols = slice(q * LANES, (q + 1) * LANES)

            def row_block(rb, carry, q=q, cols=cols):
                base = pl.multiple_of(rb * RB, RB)
                rows = pl.ds(base, RB)
                dg = _conv_taps(dcbuf, cw_ref, q, base, CONV_W - 1, -1)
                val = p_ref[rows, cols]
                sg = _sigmoid(p_ref[rows, D_B + q * LANES:D_B + (q + 1) * LANES])
                dp_ref[rows, 2 * D_A + q * LANES:2 * D_A + (q + 1) * LANES] = (dg * sg).astype(BF16)
                dp_ref[rows, 2 * D_A + D_B + q * LANES:2 * D_A + D_B + (q + 1) * LANES] = (
                    dg * val * sg * (1.0 - sg)).astype(BF16)
                dcv = dcbuf[0, q, rows, :]
                parts = []
                for k in range(CONV_W):
                    prod = dcv * _shifted(gbuf, q, base, HALO - (CONV_W - 1) + k)
                    parts.append(jnp.sum(prod.reshape(RB // 8, 8, LANES), axis=0))
                for k in range(CONV_W):
                    dwacc[k * 8:(k + 1) * 8, cols] += parts[k]
                return carry

            lax.fori_loop(0, tm // RB, row_block, 0)

        dh = jnp.zeros((tm, D), F32)
        for j in range(N_DEV):
            dh = dh + _dot_nt(dp_ref[:, j * bn:(j + 1) * bn], win_ref[j])
        xv = x_ref[...]
        dxn, dg = _rms_bwd(dh, xv, _rms(xv), g1_ref[...])
        dx = dres_ref[...] + dxn
        dx_ref[...] = dx
        dxb_ref[...] = dx.astype(BF16)
        dg1_ref[...] += dg

        @pl.when(i == n_tiles - 1)
        def _():
            for k in range(CONV_W):
                dcw_ref[k:k + 1, :] = jnp.sum(dwacc[k * 8:(k + 1) * 8, :], axis=0, keepdims=True)

    return pl.pallas_call(
        body, name="mixer_bwd_b", grid=(n_tiles,),
        in_specs=[_rows(tm, D_B),
                  pl.BlockSpec((HALO, D_B), lambda i: (jnp.minimum((i + 1) * hb, T // HALO - 1), 0)),
                  _rows(tm, D_B),
                  pl.BlockSpec((HALO, D_B), lambda i: (jnp.maximum(i * hb - 1, 0), 0)),
                  pl.BlockSpec((tm, 2 * D_B), lambda i: (i, 1)),
                  _full((CONV_W, D_B)), _rows(tm, 2 * D_A),
                  pl.BlockSpec(wg_in.shape, lambda i: (0, 0, 0), pipeline_mode=pl.Buffered(1)),
                  _rows(tm, D), _full((1, D)), _rows(tm, D)],
        out_specs=[_rows(tm, IN_COLS), _full((CONV_W, D_B)), _rows(tm, D), _rows(tm, D), _full((1, D))],
        out_shape=[jax.ShapeDtypeStruct((T, IN_COLS), BF16), jax.ShapeDtypeStruct((CONV_W, D_B), F32),
                   jax.ShapeDtypeStruct((T, D), F32), jax.ShapeDtypeStruct((T, D), BF16),
                   jax.ShapeDtypeStruct((1, D), F32)],
        scratch_shapes=[pltpu.VMEM((8, H_B, tm + HALO, LANES), F32), pltpu.VMEM((8, H_B, HALO + tm, LANES), F32),
                        pltpu.VMEM((CONV_W * 8, D_B), F32)],
        compiler_params=_cparams(40),
    )(*map(_hbm, (dc, dc, g, g, proj, cw, dpa, wg_in, x, g1, dres)))


def _wgrad(name, a, g, bn, square_a=False):
    k = a.shape[1]
    n = g.shape[1]
    tk = min(k, 1024)
    tn = min(n, max(bn, 1024))
    nsub = tn // bn
    tt = TT_WGRAD
    nt = T // tt

    def body(a_ref, g_ref, o_ref, ob_ref):
        t = pl.program_id(2)

        @pl.when(t == 0)
        def _():
            o_ref[...] = jnp.zeros(o_ref.shape, F32)

        av = a_ref[...]
        if square_a:
            av = av * av
        for s in range(nsub):
            o_ref[s] += _dot_tn(av, g_ref[:, s * bn:(s + 1) * bn])

        @pl.when(t == nt - 1)
        def _():
            ob_ref[...] = o_ref[...].astype(BF16)

    ospec = pl.BlockSpec((nsub, tk, bn), lambda ki, ni, ti: (ni, ki, 0))
    return pl.pallas_call(
        body, name=name, grid=(k // tk, n // tn, nt),
        in_specs=[pl.BlockSpec((tt, tk), lambda ki, ni, ti: (ti, ki)),
                  pl.BlockSpec((tt, tn), lambda ki, ni, ti: (ti, ni))],
        out_specs=[ospec, ospec],
        out_shape=[jax.ShapeDtypeStruct((n // bn, k, bn), F32), jax.ShapeDtypeStruct((n // bn, k, bn), BF16)],
        compiler_params=_cparams(40, ("arbitrary", "arbitrary", "arbitrary")),
    )(a, g)


def _coords():
    return lax.axis_index("x"), lax.axis_index("y"), lax.axis_index("c")


def _flat(x, y, c):
    return 4 * x + 2 * y + c


def _peer(k):
    x, y, c = _coords()
    return (x ^ ((k >> 2) & 1), y ^ ((k >> 1) & 1), c ^ (k & 1))


def _hbm(a):
    return pltpu.with_memory_space_constraint(a, pltpu.HBM)


def _hbm_like(a):
    return pltpu.HBM(a.shape, a.dtype)


def _peer_sems():
    return pltpu.SemaphoreType.DMA((N_DEV - 1,))


def _place_own(shards):
    n = len(shards)
    shapes = [(s.shape if l is None else s.shape[1:]) for s, l, _ in shards]

    def body(*refs):
        ins, outs = refs[:n], refs[n:2 * n]
        stage_in, stage_out = refs[2 * n:3 * n], refs[3 * n:4 * n]
        in_sems, out_sems = refs[4 * n], refs[4 * n + 1]
        me = _flat(*_coords())
        srcs = [ins[a] if shards[a][1] is None else ins[a].at[shards[a][1]] for a in range(n)]
        loads = [pltpu.make_async_copy(srcs[a], stage_in[a], in_sems.at[a]) for a in range(n)]
        stores = [pltpu.make_async_copy(stage_out[a], outs[a].at[me], out_sems.at[a]) for a in range(n)]
        for cp in loads:
            cp.start()
        for a in range(n):
            loads[a].wait()
            stage_out[a][...] = stage_in[a][...].astype(stage_out[a].dtype)
            stores[a].start()
        for cp in stores:
            cp.wait()

    return pl.pallas_call(
        body, name="place_own", in_specs=[HBM] * n, out_specs=[HBM] * n,
        out_shape=[pltpu.HBM((N_DEV,) + shapes[a], shards[a][2]) for a in range(n)],
        scratch_shapes=[*[pltpu.VMEM(shapes[a], shards[a][0].dtype) for a in range(n)],
                        *[pltpu.VMEM(shapes[a], shards[a][2]) for a in range(n)],
                        pltpu.SemaphoreType.DMA((n,)), pltpu.SemaphoreType.DMA((n,))],
        compiler_params=pltpu.CompilerParams(vmem_limit_bytes=40 * MIB),
    )(*[_hbm(s) for s, _, _ in shards])


SIBLING = 1
CHIP_PEERS = (2, 4, 6)
FIRST_PEERS = (SIBLING,) + CHIP_PEERS


def _gather_start(lands):
    n = len(lands)

    def body(*refs):
        lnd, send, recv = refs[:n], refs[n:2 * n], refs[2 * n:3 * n]
        me = _flat(*_coords())
        for a in range(n):
            for j, k in enumerate(FIRST_PEERS):
                pltpu.make_async_remote_copy(
                    src_ref=lnd[a].at[me], dst_ref=lnd[a].at[me], send_sem=send[a].at[j],
                    recv_sem=recv[a].at[j], device_id=_peer(k), device_id_type=MESH_ID).start()

    sems = pltpu.SemaphoreType.DMA((len(FIRST_PEERS),))
    outs = pl.pallas_call(
        body, name="gather_start",
        out_shape=(*[sems] * (2 * n), *[_hbm_like(l) for l in lands]),
        in_specs=[HBM] * n, out_specs=(*[SEM] * (2 * n), *[HBM] * n),
        input_output_aliases={i: 2 * n + i for i in range(n)},
        compiler_params=pltpu.CompilerParams(has_side_effects=EFFECT),
    )(*[_hbm(l) for l in lands])
    return outs[:n], outs[n:2 * n], outs[2 * n:]


def _gather_pass_on(name, land, send, recv, after):
    def body(l_ref, send_ref, recv_ref, after_ref, send2, recv2, l_out):
        sibling = _peer(SIBLING)
        for j, k in enumerate(CHIP_PEERS):
            cp = pltpu.make_async_remote_copy(
                src_ref=l_ref.at[0], dst_ref=l_ref.at[0], send_sem=send_ref.at[1 + j], recv_sem=recv_ref.at[1 + j],
                device_id=_peer(k), device_id_type=MESH_ID)
            cp.wait_send()
            cp.wait_recv()
            blk = _flat(*_peer(k))
            pltpu.make_async_remote_copy(
                src_ref=l_ref.at[blk], dst_ref=l_ref.at[blk], send_sem=send2.at[j], recv_sem=recv2.at[j],
                device_id=sibling, device_id_type=MESH_ID).start()

    sems = pltpu.SemaphoreType.DMA((len(CHIP_PEERS),))
    return pl.pallas_call(
        body, name=name, out_shape=(sems, sems, _hbm_like(land)),
        in_specs=(HBM, SEM, SEM, HBM), out_specs=(SEM, SEM, HBM), input_output_aliases={0: 2},
        compiler_params=pltpu.CompilerParams(has_side_effects=EFFECT),
    )(land, send, recv, after)


def _gather_wait(name, land, send, recv, send2, recv2, after):
    def body(l_ref, send_ref, recv_ref, send2_ref, recv2_ref, after_ref, l_out):
        sibling = _peer(SIBLING)
        own = pltpu.make_async_remote_copy(
            src_ref=l_ref.at[0], dst_ref=l_ref.at[0], send_sem=send_ref.at[0], recv_sem=recv_ref.at[0],
            device_id=sibling, device_id_type=MESH_ID)
        own.wait_send()
        own.wait_recv()
        for j in range(len(CHIP_PEERS)):
            cp = pltpu.make_async_remote_copy(
                src_ref=l_ref.at[0], dst_ref=l_ref.at[0], send_sem=send2_ref.at[j], recv_sem=recv2_ref.at[j],
                device_id=sibling, device_id_type=MESH_ID)
            cp.wait_send()
            cp.wait_recv()

    return pl.pallas_call(
        body, name=name, out_shape=_hbm_like(land),
        in_specs=(HBM, SEM, SEM, SEM, SEM, HBM), out_specs=HBM, input_output_aliases={0: 0},
        compiler_params=pltpu.CompilerParams(has_side_effects=EFFECT),
    )(land, send, recv, send2, recv2, after)


def _scatter_start(name, grads, lands, layer):
    n = len(grads)

    def body(*refs):
        g, lnd, send, recv = refs[:n], refs[n:2 * n], refs[2 * n:3 * n], refs[3 * n:4 * n]
        for a in range(n):
            for k in range(1, N_DEV):
                to = _peer(k)
                pltpu.make_async_remote_copy(
                    src_ref=g[a].at[_flat(*to)], dst_ref=lnd[a].at[layer, k - 1], send_sem=send[a].at[k - 1],
                    recv_sem=recv[a].at[k - 1], device_id=to, device_id_type=MESH_ID).start()

    outs = pl.pallas_call(
        body, name=name,
        out_shape=(*[_peer_sems()] * (2 * n), *[_hbm_like(g) for g in grads], *[_hbm_like(l) for l in lands]),
        in_specs=[HBM] * (2 * n), out_specs=(*[SEM] * (2 * n), *[HBM] * (2 * n)),
        input_output_aliases={i: 2 * n + i for i in range(2 * n)},
        compiler_params=pltpu.CompilerParams(has_side_effects=EFFECT),
    )(*[_hbm(g) for g in grads], *[_hbm(l) for l in lands])
    return outs[:n], outs[n:2 * n], outs[2 * n:3 * n], outs[3 * n:]


def _scatter_wait(name, grads, lands, sends, recvs, after):
    n, nw = len(grads), len(lands)

    def body(*refs):
        g, lnd = refs[:n], refs[n:n + nw]
        send, recv = refs[n + nw:2 * n + nw], refs[2 * n + nw:3 * n + nw]
        for a in range(n):
            for k in range(1, N_DEV):
                cp = pltpu.make_async_remote_copy(
                    src_ref=g[a].at[0], dst_ref=lnd[a // DEPTH].at[a % DEPTH, 0],
                    send_sem=send[a].at[k - 1], recv_sem=recv[a].at[k - 1],
                    device_id=_peer(k), device_id_type=MESH_ID)
                cp.wait_send()
                cp.wait_recv()

    outs = pl.pallas_call(
        body, name=name,
        out_shape=(*[_hbm_like(g) for g in grads], *[_hbm_like(l) for l in lands]),
        in_specs=(*[HBM] * (n + nw), *[SEM] * (2 * n), HBM), out_specs=[HBM] * (n + nw),
        input_output_aliases={i: i for i in range(n + nw)},
        compiler_params=pltpu.CompilerParams(has_side_effects=EFFECT),
    )(*grads, *lands, *sends, *recvs, after)
    return outs[n:]


def _own_blocks(grads_f32):
    nw = len(grads_f32)
    flat32 = [g for w in grads_f32 for g in w]
    n = len(flat32)

    def body(*refs):
        g32, own, stage = refs[:n], refs[n:n + nw], refs[n + nw:2 * n + nw]
        in_sems, out_sems = refs[2 * n + nw], refs[2 * n + nw + 1]
        me = _flat(*_coords())
        loads = [pltpu.make_async_copy(g32[a].at[me], stage[a], in_sems.at[a]) for a in range(n)]
        stores = [pltpu.make_async_copy(stage[a], own[a // DEPTH].at[a % DEPTH], out_sems.at[a]) for a in range(n)]
        for cp in loads:
            cp.start()
        for a in range(n):
            loads[a].wait()
            stores[a].start()
        for cp in stores:
            cp.wait()

    return pl.pallas_call(
        body, name="own_blocks", in_specs=[HBM] * n, out_specs=[HBM] * nw,
        out_shape=[pltpu.HBM((DEPTH,) + w[0].shape[1:], F32) for w in grads_f32],
        scratch_shapes=[*[pltpu.VMEM(g.shape[1:], F32) for g in flat32],
                        pltpu.SemaphoreType.DMA((n,)), pltpu.SemaphoreType.DMA((n,))],
        compiler_params=pltpu.CompilerParams(vmem_limit_bytes=24 * MIB),
    )(*[_hbm(g) for g in flat32])


def _all_reduce_small(part, deps):
    rows = part.shape[0]
    br = rows // N_DEV
    assert br * N_DEV == rows and br % 8 == 0
    nd = len(deps)

    def body(p_ref, *refs):
        o_ref, slots, send1, recv1, send2, recv2 = refs[nd:]
        me = _flat(*_coords())

        def block(ref, d):
            return ref.at[pl.ds(pl.multiple_of(d * br, 8), br), :]

        slots[me] = p_ref[pl.ds(pl.multiple_of(me * br, 8), br), :]
        scatter, gather = [], []
        for k in range(1, N_DEV):
            to = _peer(k)
            scatter.append(pltpu.make_async_remote_copy(
                src_ref=block(p_ref, _flat(*to)), dst_ref=slots.at[me],
                send_sem=send1.at[k - 1], recv_sem=recv1.at[k - 1], device_id=to, device_id_type=MESH_ID))
            gather.append(pltpu.make_async_remote_copy(
                src_ref=block(o_ref, me), dst_ref=block(o_ref, me),
                send_sem=send2.at[k - 1], recv_sem=recv2.at[k - 1], device_id=to, device_id_type=MESH_ID))
        for cp in scatter:
            cp.start()
        for cp in scatter:
            cp.wait()
        acc = slots[0]
        for d in range(1, N_DEV):
            acc = acc + slots[d]
        o_ref[pl.ds(pl.multiple_of(me * br, 8), br), :] = acc
        for cp in gather:
            cp.start()
        for cp in gather:
            cp.wait()

    return pl.pallas_call(
        body, name="all_reduce_small",
        in_specs=[pl.BlockSpec(memory_space=pltpu.VMEM)] + [HBM] * nd, out_specs=pl.BlockSpec(memory_space=pltpu.VMEM),
        out_shape=jax.ShapeDtypeStruct(part.shape, F32),
        scratch_shapes=[pltpu.VMEM((N_DEV, br, LANES), F32)] + [pltpu.SemaphoreType.DMA((N_DEV - 1,))] * 4,
        compiler_params=pltpu.CompilerParams(vmem_limit_bytes=24 * MIB),
    )(part, *map(_hbm, deps))


def _adam_math(w, g, m, v):
    m = ADAM_B1 * m + (1.0 - ADAM_B1) * g
    v = ADAM_B2 * v + (1.0 - ADAM_B2) * (g * g)
    m_hat = m / (1.0 - ADAM_B1 ** ADAM_STEP)
    v_hat = v / (1.0 - ADAM_B2 ** ADAM_STEP)
    delta = -ADAM_LR * (m_hat / (jnp.sqrt(v_hat) + ADAM_EPS) + ADAM_WD * w)
    return delta, m, v


def _adam_sharded(name, own, recv, w, m, v):
    _, r, c = w.shape
    tr = min(r, 256)

    def body(own_ref, recv_ref, w_ref, m_ref, v_ref, g_ref, d_ref, nm_ref, nv_ref):
        g = own_ref[...]
        for k in range(N_DEV - 1):
            g = g + recv_ref[k].astype(F32)
        delta, nm, nv = _adam_math(w_ref[...], g, m_ref[...], v_ref[...])
        g_ref[...] = g
        d_ref[...] = delta
        nm_ref[...] = nm
        nv_ref[...] = nv

    blk = pl.BlockSpec((None, tr, c), lambda l, i: (l, i, 0))
    shp = jax.ShapeDtypeStruct(w.shape, F32)
    return pl.pallas_call(
        body, name=name, grid=(DEPTH, r // tr),
        in_specs=[blk, pl.BlockSpec((None, N_DEV - 1, tr, c), lambda l, i: (l, 0, i, 0)), blk, blk, blk],
        out_specs=[blk] * 4, out_shape=[shp] * 4,
        compiler_params=_cparams(32, ("arbitrary", "arbitrary")),
    )(*[_hbm(a) for a in (own, recv, w, m, v)])


def _adam_small(gs, ws, ms, vs):
    n = len(gs)

    def body(*refs):
        g, w, m, v = refs[:n], refs[n:2 * n], refs[2 * n:3 * n], refs[3 * n:4 * n]
        d, nm, nv = refs[4 * n:5 * n], refs[5 * n:6 * n], refs[6 * n:7 * n]
        for a in range(n):
            delta, new_m, new_v = _adam_math(w[a][...], g[a][...], m[a][...], v[a][...])
            d[a][...] = delta
            nm[a][...] = new_m
            nv[a][...] = new_v

    specs = [_full(a.shape) for a in gs]
    shapes = [jax.ShapeDtypeStruct(a.shape, F32) for a in gs]
    outs = pl.pallas_call(
        body, name="adam_small", grid=(1,),
        in_specs=specs * 4, out_specs=specs * 3, out_shape=shapes * 3,
        compiler_params=_cparams(24),
    )(*gs, *ws, *ms, *vs)
    return outs[:n], outs[n:2 * n], outs[2 * n:]


SMALL = ["norm1_g", "sgu_ln_g", "sgu_ln_b", "sgu_w", "sgu_b", "conv_b", "conv_ln_g", "conv_ln_b", "norm2_g",
         "final_g"]


def _pack(arrays):
    flat = jnp.concatenate([a.reshape(-1) for a in arrays])
    pad = (-flat.shape[0]) % (N_DEV * 8 * LANES)
    return jnp.pad(flat, (0, pad)).reshape(-1, LANES)


def _unpack(packed, shapes):
    flat = packed.reshape(-1)
    out, off = [], 0
    for s in shapes:
        size = 1
        for d in s:
            size *= d
        out.append(flat[off:off + size].reshape(s))
        off += size
    return out


def kernel(x, norm1_g, w_in, sgu_ln_g, sgu_ln_b, sgu_w, sgu_b, conv_w, conv_b, conv_ln_g, conv_ln_b, w_out, norm2_g, w_ff1, w_ff2, final_g, loss_target, m_norm1_g, m_w_in, m_sgu_ln_g, m_sgu_ln_b, m_sgu_w, m_sgu_b, m_conv_w, m_conv_b, m_conv_ln_g, m_conv_ln_b, m_w_out, m_norm2_g, m_w_ff1, m_w_ff2, m_final_g, v_norm1_g, v_w_in, v_sgu_ln_g, v_sgu_ln_b, v_sgu_w, v_sgu_b, v_conv_w, v_conv_b, v_conv_ln_g, v_conv_ln_b, v_w_out, v_norm2_g, v_w_ff1, v_w_ff2, v_final_g):
    x2d = x.reshape(T, D)
    tgt = loss_target.reshape(T, D)
    cw_shard = conv_w.reshape(CONV_W, LANES)

    gnames, shards = [], []
    for l in range(DEPTH):
        for k, w in (("w_in", w_in), ("w_out", w_out), ("w_ff1", w_ff1), ("w_ff2", w_ff2)):
            gnames.append(f"{k}{l}")
            shards.append((w, l, BF16))
        if l == 0:
            gnames.insert(1, "conv_w")
            shards.insert(1, (cw_shard, None, F32))
    sends, recvs, lands = _gather_start(_place_own(shards))
    gidx = {k: i for i, k in enumerate(gnames)}

    passed = {}

    def pass_on(k, after):
        i = gidx[k]
        passed[k] = _gather_pass_on("gather_pass_on_" + k, lands[i], sends[i], recvs[i], after)

    def gathered(k, after):
        i = gidx[k]
        send2, recv2, land = passed[k]
        return _gather_wait("gather_wait_" + k, land, sends[i], recvs[i], send2, recv2, after)

    saved = []
    xl = x2d
    cw_full = None
    for l in range(DEPTH):
        g1 = norm1_g[l].reshape(1, D)
        g2 = norm2_g[l].reshape(1, D)
        lg, lb = sgu_ln_g[l].reshape(1, D_A), sgu_ln_b[l].reshape(1, D_A)
        bst = sgu_b[l].T
        cb = conv_b[l].reshape(1, D_B)
        clg, clb = conv_ln_g[l].reshape(1, D_B), conv_ln_b[l].reshape(1, D_B)
        if l == 0:
            pass_on("w_in0", xl)
        wg_in = gathered(f"w_in{l}", xl)
        if l == 0:
            pass_on("conv_w", xl)
        pass_on(f"w_out{l}", xl)
        h, proj = _fwd_in(xl, g1, wg_in)
        if l == 0:
            cw_full = gathered("conv_w", proj).reshape(N_DEV, DEPTH, CONV_W, D_B // N_DEV).transpose(
                1, 2, 0, 3).reshape(DEPTH, CONV_W, D_B)
        wout = gathered(f"w_out{l}", proj).reshape(D, D)
        if l > 0:
            pass_on(f"w_ff1{l}", proj)
        mix, c, g, x1, h2 = _mixer_fwd(proj, lg, lb, sgu_w[l], bst, cw_full[l], cb, clg, clb, wout, xl, g2)
        if l == 0:
            pass_on(f"w_ff1{l}", h2)
        else:
            pass_on(f"w_ff2{l}", h2)
        wg_ff1 = gathered(f"w_ff1{l}", h2)
        r = _fwd_ff1(h2, wg_ff1)
        if l == 0:
            pass_on(f"w_ff2{l}", r)
        if l + 1 < DEPTH:
            pass_on(f"w_in{l + 1}", r)
        w2 = gathered(f"w_ff2{l}", r).reshape(D_FF, D)
        saved.append(dict(x=xl, h=h, proj=proj, mix=mix, c=c, g=g, x1=x1, h2=h2, r=r, wg_in=wg_in, wout=wout,
                          wg_ff1=wg_ff1, w2=w2, g1=g1, g2=g2, lg=lg, lb=lb, bst=bst, clg=clg, clb=clb))
        if l + 1 < DEPTH:
            xl = _fwd_ff2(r, w2, x1)
        else:
            dx, dxb, loss_part, d_final_g = _fwd_ff2_loss(r, w2, x1, final_g.reshape(1, D), tgt)

    names = ["w_in", "w_out", "w_ff1", "w_ff2"]
    block = dict(w_in=(D, IN_COLS // N_DEV), w_out=(D // N_DEV, D), w_ff1=(D, D_FF // N_DEV), w_ff2=(D_FF // N_DEV, D))
    land = {k: lax.empty((DEPTH, N_DEV - 1) + block[k], BF16) for k in names}
    big32 = {k: [None] * DEPTH for k in names}
    big16 = {k: [None] * DEPTH for k in names}
    ssend = {k: [None] * DEPTH for k in names}
    srecv = {k: [None] * DEPTH for k in names}

    def send_grads(l, grads):
        ks = list(grads)
        for k in ks:
            big32[k][l] = grads[k][0].reshape((N_DEV,) + block[k])
        sends, recvs, g16s, lands = _scatter_start(
            "scatter_start_" + "_".join(ks) + str(l), [grads[k][1].reshape((N_DEV,) + block[k]) for k in ks],
            [land[k] for k in ks], l)
        for i, k in enumerate(ks):
            ssend[k][l], srecv[k][l], big16[k][l], land[k] = sends[i], recvs[i], g16s[i], lands[i]

    small = {}
    d_conv_w = [None] * DEPTH
    for l in reversed(range(DEPTH)):
        s = saved[l]
        send_grads(l, dict(w_ff2=_wgrad("wgrad_ff2", s["r"], dxb, D, square_a=True)))
        df1, dx1, dx1b, dg2 = _bwd_mlp(dxb, dx, s["w2"], s["r"], s["wg_ff1"], s["x1"], s["g2"], big16["w_ff2"][l])
        send_grads(l, dict(w_ff1=_wgrad("wgrad_ff1", s["h2"], df1, D_FF // N_DEV),
                           w_out=_wgrad("wgrad_out", s["mix"], dx1b, D)))
        dpa, dc, dlg, dlb, dwm, dbs, dcb, dclg, dclb = _mixer_bwd_a(
            dx1b, s["wout"], s["proj"], s["c"], s["lg"], s["lb"], sgu_w[l], s["bst"], s["clg"], s["clb"],
            big16["w_out"][l])
        dproj, d_conv_w[l], dx, dxb, dg1 = _mixer_bwd_b(dc, s["g"], s["proj"], cw_full[l], dpa, s["wg_in"], s["x"],
                                                         s["g1"], dx1)
        send_grads(l, dict(w_in=_wgrad("wgrad_in", s["h"], dproj, IN_COLS // N_DEV)))
        small[l] = dict(norm1_g=dg1, sgu_ln_g=dlg, sgu_ln_b=dlb, sgu_w=dwm, sgu_b=dbs, conv_b=dcb, conv_ln_g=dclg,
                        conv_ln_b=dclb, norm2_g=dg2)
    grad_x = dx.reshape(1, T, D)

    own = dict(zip(names, _own_blocks([big32[k] for k in names])))
    wmv = dict(w_in=(w_in, m_w_in, v_w_in), w_out=(w_out, m_w_out, v_w_out), w_ff1=(w_ff1, m_w_ff1, v_w_ff1),
               w_ff2=(w_ff2, m_w_ff2, v_w_ff2))
    res = {}

    def update(tag, ks, after):
        recv = _scatter_wait("scatter_wait_" + tag, [g for k in ks for g in big16[k]], [land[k] for k in ks],
                             [q for k in ks for q in ssend[k]], [q for k in ks for q in srecv[k]], after)
        for k, rk in zip(ks, recv):
            res[k] = _adam_sharded("adam_" + k, own[k], rk, *wmv[k])

    update("early", ["w_ff2", "w_ff1", "w_out"], big16["w_in"][0])

    rep = dict(norm1_g=(norm1_g, m_norm1_g, v_norm1_g), sgu_ln_g=(sgu_ln_g, m_sgu_ln_g, v_sgu_ln_g),
               sgu_ln_b=(sgu_ln_b, m_sgu_ln_b, v_sgu_ln_b), sgu_w=(sgu_w, m_sgu_w, v_sgu_w),
               sgu_b=(sgu_b, m_sgu_b, v_sgu_b), conv_b=(conv_b, m_conv_b, v_conv_b),
               conv_ln_g=(conv_ln_g, m_conv_ln_g, v_conv_ln_g), conv_ln_b=(conv_ln_b, m_conv_ln_b, v_conv_ln_b),
               norm2_g=(norm2_g, m_norm2_g, v_norm2_g), final_g=(final_g, m_final_g, v_final_g))
    parts = []
    for k in SMALL:
        if k == "final_g":
            parts.append(d_final_g.reshape(rep[k][0].shape))
        else:
            parts.append(jnp.stack([small[l][k].reshape(rep[k][0].shape[1:]) for l in range(DEPTH)]))
    parts.append(jnp.stack(d_conv_w))
    parts.append(loss_part[0, 0:1])
    shapes = [p.shape for p in parts]
    packed_sum = _all_reduce_small(_pack(parts), [res[k][0] for k in ("w_ff2", "w_ff1", "w_out")])
    update("late", ["w_in"], packed_sum)
    summed = _unpack(packed_sum, shapes)
    loss = summed[-1][0]
    me = _flat(*_coords())
    g_conv_w = lax.dynamic_slice_in_dim(summed[-2], me * (D_B // N_DEV), D_B // N_DEV, axis=2)
    g_small = summed[:-2] + [g_conv_w]
    keys = SMALL + ["conv_w"]
    rep["conv_w"] = (conv_w, m_conv_w, v_conv_w)
    def two_d(a):
        return a.reshape(1, -1) if a.ndim == 1 else a

    d_s, nm_s, nv_s = _adam_small([two_d(g) for g in g_small], *[[two_d(rep[k][i]) for k in keys] for i in range(3)])
    for i, k in enumerate(keys):
        shape = rep[k][0].shape
        res[k] = (g_small[i], d_s[i].reshape(shape), nm_s[i].reshape(shape), nv_s[i].reshape(shape))

    order = ["norm1_g", "w_in", "sgu_ln_g", "sgu_ln_b", "sgu_w", "sgu_b", "conv_w", "conv_b", "conv_ln_g",
             "conv_ln_b", "w_out", "norm2_g", "w_ff1", "w_ff2", "final_g"]
    return (loss, grad_x, *[res[k][0] for k in order], *[res[k][1] for k in order],
            *[res[k][2] for k in order], *[res[k][3] for k in order])
```

```python
import functools

import jax
import jax.numpy as jnp
from jax import lax
from jax.experimental import pallas as pl
from jax.experimental.pallas import tpu as pltpu

F32 = jnp.float32
BF16 = jnp.bfloat16

N_DEV = 8
DEPTH = 2
T = 4096
D = 1024
D_A = 512
D_B = 512
CHUNK = 128
H_A = 4
H_B = 4
CONV_W = 31
HALO = 32
D_FF = 4096
IN_COLS = 2048
EPS = 1e-6

ADAM_LR = 0.001
ADAM_B1 = 0.9
ADAM_B2 = 0.999
ADAM_EPS = 1e-08
ADAM_WD = 0.01
ADAM_STEP = 10

TM = 512
TM_FWD = 1024
TT_WGRAD = 4096
TM_MIX = 256
TM_SGU = 512
RB = 64
CONV_CHAINS = 2
LANES = 128
MIB = 1024 * 1024
SCOPED_VMEM_MIB = 60

SQRT_HALF = 0.7071067811865476
INV_SQRT_2PI = 0.3989422804014327

MESH_ID = pl.DeviceIdType.MESH
HBM = pl.BlockSpec(memory_space=pltpu.HBM)
SEM = pl.BlockSpec(memory_space=pltpu.SEMAPHORE)
EFFECT = pltpu.SideEffectType.DATAFLOW_SIDE_EFFECTING


def _cparams(vmem_mib, sem=("arbitrary",)):
    assert vmem_mib <= SCOPED_VMEM_MIB
    return pltpu.CompilerParams(dimension_semantics=sem, vmem_limit_bytes=SCOPED_VMEM_MIB * MIB)


def _full(shape):
    return pl.BlockSpec(shape, lambda *_: (0,) * len(shape))


def _rows(tm, cols):
    return pl.BlockSpec((tm, cols), lambda i: (i, 0))


def _gelu(x):
    cdf = 0.5 * (1.0 + lax.erf(x * SQRT_HALF))
    return x * cdf, cdf


def _gelu_grad(x, cdf):
    return cdf + x * (INV_SQRT_2PI * jnp.exp(-0.5 * x * x))


def _sigmoid(x):
    return 1.0 / (1.0 + jnp.exp(-x))


def _ln(x):
    mu = jnp.mean(x, axis=-1, keepdims=True)
    xc = x - mu
    rstd = lax.rsqrt(jnp.mean(xc * xc, axis=-1, keepdims=True) + EPS)
    return xc * rstd, rstd


def _ln_bwd(dyh, xhat, rstd):
    return rstd * (dyh - jnp.mean(dyh, axis=-1, keepdims=True) - xhat * jnp.mean(dyh * xhat, axis=-1, keepdims=True))


def _rms(x):
    return lax.rsqrt(jnp.mean(x * x, axis=-1, keepdims=True) + EPS)


def _rms_bwd(dh, x, r, g):
    n = x * r
    dn = dh * g
    dx = r * (dn - n * jnp.mean(dn * n, axis=-1, keepdims=True))
    return dx, jnp.sum(dh * n, axis=0, keepdims=True)


def _dot(a, b):
    return jnp.dot(a, b, preferred_element_type=F32)


def _dot_nt(a, b):
    return lax.dot_general(a, b, (((1,), (1,)), ((), ())), preferred_element_type=F32)


def _dot_tn(a, b):
    return lax.dot_general(a, b, (((0,), (0,)), ((), ())), preferred_element_type=F32)


def _tril_mask():
    r = lax.broadcasted_iota(jnp.int32, (CHUNK, CHUNK), 0)
    c = lax.broadcasted_iota(jnp.int32, (CHUNK, CHUNK), 1)
    return r >= c


def _fwd_in(x, g1, wg):
    bn = wg.shape[2]

    def body(x_ref, g_ref, w_ref, h_ref, p_ref):
        xv = x_ref[...]
        h = (xv * _rms(xv) * g_ref[...]).astype(BF16)
        h_ref[...] = h
        for j in range(N_DEV):
            p_ref[:, j * bn:(j + 1) * bn] = _dot(h, w_ref[j])

    return pl.pallas_call(
        body, name="fwd_in", grid=(T // TM_FWD,),
        in_specs=[_rows(TM_FWD, D), _full((1, D)), _full(wg.shape)],
        out_specs=[_rows(TM_FWD, D), _rows(TM_FWD, IN_COLS)],
        out_shape=[jax.ShapeDtypeStruct((T, D), BF16), jax.ShapeDtypeStruct((T, IN_COLS), F32)],
        compiler_params=_cparams(32),
    )(*map(_hbm, (x, g1, wg)))


def _fill_shift_buffer(sh_ref, row0, value):
    for q in range(sh_ref.shape[1]):
        sh_ref[0, q, row0:row0 + value.shape[0], :] = value[:, q * LANES:(q + 1) * LANES]


def _build_shifts(sh_ref):
    rows = sh_ref.shape[2]
    for p in range(1, 8):
        for q in range(sh_ref.shape[1]):
            sh_ref[p, q, 0:rows - 8, :] = sh_ref[0, q, p:p + rows - 8, :]


def _shifted(sh_ref, q, base, off):
    start = base + (off - off % 8)
    if not isinstance(start, int):
        start = pl.multiple_of(start, 8)
    return sh_ref[off % 8, q, pl.ds(start, RB), :]


def _conv_taps(sh_ref, w_ref, q, base, first_tap_row, step):
    cols = slice(q * LANES, (q + 1) * LANES)
    acc = [jnp.zeros((RB, LANES), F32) for _ in range(CONV_CHAINS)]
    for k in range(CONV_W):
        term = _shifted(sh_ref, q, base, first_tap_row + step * k) * w_ref[k:k + 1, cols]
        acc[k % CONV_CHAINS] = acc[k % CONV_CHAINS] + term
    return functools.reduce(lambda a, b: a + b, acc)


def _mixer_fwd(proj, lg, lb, wm, bst, cw, cb, clg, clb, wout, x, g2):
    tm = TM_MIX
    hb = tm // HALO

    def body(p_ref, ph_ref, lg_ref, lb_ref, wm_ref, bs_ref, cw_ref, cb_ref, clg_ref, clb_ref, wo_ref, x_ref, g2_ref,
             mix_ref, c_ref, g_ref, x1_ref, h2_ref, gbuf):
        i = pl.program_id(0)
        u, _ = _gelu(p_ref[:, 0:D_A])
        vg, _ = _gelu(p_ref[:, D_A:2 * D_A])
        xhat, _ = _ln(vg)
        v = (xhat * lg_ref[...] + lb_ref[...]).astype(BF16)
        mask = _tril_mask()
        for h in range(H_A):
            hc = slice(h * CHUNK, (h + 1) * CHUNK)
            wmh = jnp.where(mask, wm_ref[h], 0.0).astype(BF16)
            for c in range(tm // CHUNK):
                rc = slice(c * CHUNK, (c + 1) * CHUNK)
                mixed = _dot(wmh, v[rc, hc]) + bs_ref[:, h:h + 1]
                mix_ref[rc, hc] = (u[rc, hc] * mixed).astype(BF16)
        x1_ref[...] = x_ref[...] + _dot(mix_ref[:, 0:D_A], wo_ref[0:D_A, :])

        g = p_ref[:, 2 * D_A:2 * D_A + D_B] * _sigmoid(p_ref[:, 2 * D_A + D_B:IN_COLS])
        g_ref[...] = g
        gh = ph_ref[:, 0:D_B] * _sigmoid(ph_ref[:, D_B:2 * D_B])
        _fill_shift_buffer(gbuf, 0, jnp.where(i > 0, gh, 0.0))
        _fill_shift_buffer(gbuf, HALO, g)
        _build_shifts(gbuf)
        for q in range(H_B):
            cols = slice(q * LANES, (q + 1) * LANES)
            for rb in range(tm // RB):
                acc = _conv_taps(gbuf, cw_ref, q, rb * RB, HALO - (CONV_W - 1), 1)
                c_ref[rb * RB:(rb + 1) * RB, cols] = acc + cb_ref[:, cols]
        for q in range(H_B):
            cols = slice(q * LANES, (q + 1) * LANES)
            chat, _ = _ln(c_ref[:, cols])
            z = chat * clg_ref[:, cols] + clb_ref[:, cols]
            mix_ref[:, D_A + q * LANES:D_A + (q + 1) * LANES] = (z * _sigmoid(z)).astype(BF16)

        x1 = x1_ref[...] + _dot(mix_ref[:, D_A:D], wo_ref[D_A:D, :])
        x1_ref[...] = x1
        h2_ref[...] = (x1 * _rms(x1) * g2_ref[...]).astype(BF16)

    vec = _full((1, D_A))
    return pl.pallas_call(
        body, name="mixer_fwd", grid=(T // tm,),
        in_specs=[_rows(tm, IN_COLS),
                  pl.BlockSpec((HALO, 2 * D_B), lambda i: (jnp.maximum(i * hb - 1, 0), 1)),
                  vec, vec, _full((H_A, CHUNK, CHUNK)), _full((CHUNK, H_A)),
                  _full((CONV_W, D_B)), vec, vec, vec,
                  pl.BlockSpec((D, D), lambda i: (0, 0), pipeline_mode=pl.Buffered(1)), _rows(tm, D), _full((1, D))],
        out_specs=[_rows(tm, D), _rows(tm, D_B), _rows(tm, D_B), _rows(tm, D), _rows(tm, D)],
        out_shape=[jax.ShapeDtypeStruct((T, D), BF16), jax.ShapeDtypeStruct((T, D_B), F32),
                   jax.ShapeDtypeStruct((T, D_B), F32), jax.ShapeDtypeStruct((T, D), F32),
                   jax.ShapeDtypeStruct((T, D), BF16)],
        scratch_shapes=[pltpu.VMEM((8, H_B, HALO + tm, LANES), F32)],
        compiler_params=_cparams(40),
    )(*map(_hbm, (proj, proj, lg, lb, wm, bst, cw, cb, clg, clb, wout, x, g2)))


def _fwd_ff1(h2, wg):
    bn = wg.shape[2]

    def body(h_ref, w_ref, r_ref):
        h = h_ref[...]
        for j in range(N_DEV):
            r_ref[:, j * bn:(j + 1) * bn] = jnp.maximum(_dot(h, w_ref[j]), 0.0).astype(BF16)

    return pl.pallas_call(
        body, name="fwd_ff1", grid=(T // TM_FWD,),
        in_specs=[_rows(TM_FWD, D), _full(wg.shape)],
        out_specs=_rows(TM_FWD, D_FF),
        out_shape=jax.ShapeDtypeStruct((T, D_FF), BF16),
        compiler_params=_cparams(48),
    )(h2, wg)


def _fwd_ff2(r, w2, x1):
    def body(r_ref, w_ref, x_ref, o_ref):
        rv = r_ref[...]
        o_ref[...] = x_ref[...] + _dot(rv * rv, w_ref[...])

    return pl.pallas_call(
        body, name="fwd_ff2", grid=(T // TM_FWD,),
        in_specs=[_rows(TM_FWD, D_FF), _full((D_FF, D)), _rows(TM_FWD, D)],
        out_specs=_rows(TM_FWD, D),
        out_shape=jax.ShapeDtypeStruct((T, D), F32),
        compiler_params=_cparams(48),
    )(r, w2, x1)


def _fwd_ff2_loss(r, w2, x1, gf, tgt):
    def body(r_ref, w_ref, x_ref, g_ref, t_ref, dx_ref, dxb_ref, loss_ref, dg_ref):
        i = pl.program_id(0)

        @pl.when(i == 0)
        def _():
            loss_ref[...] = jnp.zeros(loss_ref.shape, F32)
            dg_ref[...] = jnp.zeros(dg_ref.shape, F32)

        rv = r_ref[...]
        xv = x_ref[...] + _dot(rv * rv, w_ref[...])
        rn = _rms(xv)
        diff = xv * rn * g_ref[...] - t_ref[...]
        loss_ref[...] += 0.5 * jnp.sum(jnp.mean(diff * diff, axis=-1, keepdims=True), axis=0, keepdims=True)
        dx, dg = _rms_bwd(diff * (1.0 / D), xv, rn, g_ref[...])
        dx_ref[...] = dx
        dxb_ref[...] = dx.astype(BF16)
        dg_ref[...] += dg

    return pl.pallas_call(
        body, name="fwd_ff2_loss", grid=(T // TM,),
        in_specs=[_rows(TM, D_FF), pl.BlockSpec((D_FF, D), lambda i: (0, 0), pipeline_mode=pl.Buffered(1)),
                  _rows(TM, D), _full((1, D)), _rows(TM, D)],
        out_specs=[_rows(TM, D), _rows(TM, D), _full((8, LANES)), _full((1, D))],
        out_shape=[jax.ShapeDtypeStruct((T, D), F32), jax.ShapeDtypeStruct((T, D), BF16),
                   jax.ShapeDtypeStruct((8, LANES), F32), jax.ShapeDtypeStruct((1, D), F32)],
        compiler_params=_cparams(40),
    )(*map(_hbm, (r, w2, x1, gf, tgt)))


def _bwd_mlp(dxb, dres, w2, r, wg1, x1, g2, dep):
    bn = wg1.shape[2]

    def body(d_ref, dres_ref, w2_ref, r_ref, w1_ref, x_ref, g_ref, dep_ref, df1_ref, dx_ref, dxb_ref, dg_ref):
        i = pl.program_id(0)

        @pl.when(i == 0)
        def _():
            dg_ref[...] = jnp.zeros(dg_ref.shape, F32)

        d = d_ref[...]
        dh = jnp.zeros((TM, D), F32)
        for j in range(N_DEV):
            cols = slice(j * bn, (j + 1) * bn)
            df1 = (2.0 * r_ref[:, cols].astype(F32) * _dot_nt(d, w2_ref[cols, :])).astype(BF16)
            df1_ref[:, cols] = df1
            dh = dh + _dot_nt(df1, w1_ref[j])
        xv = x_ref[...]
        dxn, dg = _rms_bwd(dh, xv, _rms(xv), g_ref[...])
        dx = dres_ref[...] + dxn
        dx_ref[...] = dx
        dxb_ref[...] = dx.astype(BF16)
        dg_ref[...] += dg

    once = dict(pipeline_mode=pl.Buffered(1))
    return pl.pallas_call(
        body, name="bwd_mlp", grid=(T // TM,),
        in_specs=[_rows(TM, D), _rows(TM, D), pl.BlockSpec((D_FF, D), lambda i: (0, 0), **once), _rows(TM, D_FF),
                  pl.BlockSpec(wg1.shape, lambda i: (0, 0, 0), **once), _rows(TM, D), _full((1, D)), HBM],
        out_specs=[_rows(TM, D_FF), _rows(TM, D), _rows(TM, D), _full((1, D))],
        out_shape=[jax.ShapeDtypeStruct((T, D_FF), BF16), jax.ShapeDtypeStruct((T, D), F32),
                   jax.ShapeDtypeStruct((T, D), BF16), jax.ShapeDtypeStruct((1, D), F32)],
        compiler_params=_cparams(56),
    )(*map(_hbm, (dxb, dres, w2, r, wg1, x1, g2, dep)))


def _mixer_bwd_a(dxb, wout, proj, c, lg, lb, wm, bst, clg, clb, dep):
    tm = TM_SGU
    n_tiles = T // tm

    def body(dx_ref, wo_ref, p_ref, c_ref, lg_ref, lb_ref, wm_ref, bs_ref, clg_ref, clb_ref, dep_ref,
             dpa_ref, dc_ref, dlg_ref, dlb_ref, dwm_ref, dbs_ref, dcb_ref, dclg_ref, dclb_ref,
             dv_buf, db_acc):
        i = pl.program_id(0)

        @pl.when(i == 0)
        def _():
            for ref in (dlg_ref, dlb_ref, dwm_ref, dbs_ref, dcb_ref, dclg_ref, dclb_ref, db_acc):
                ref[...] = jnp.zeros(ref.shape, F32)

        dmix = _dot_nt(dx_ref[...], wo_ref[...])
        ua = p_ref[:, 0:D_A]
        va = p_ref[:, D_A:2 * D_A]
        u, cdf_u = _gelu(ua)
        vg, cdf_v = _gelu(va)
        xhat, rstd = _ln(vg)
        v = (xhat * lg_ref[...] + lb_ref[...]).astype(BF16)
        mask = _tril_mask()
        for h in range(H_A):
            hc = slice(h * CHUNK, (h + 1) * CHUNK)
            wmh = jnp.where(mask, wm_ref[h], 0.0).astype(BF16)
            for cidx in range(tm // CHUNK):
                rc = slice(cidx * CHUNK, (cidx + 1) * CHUNK)
                vb = v[rc, hc]
                mixed = _dot(wmh, vb) + bs_ref[:, h:h + 1]
                da = dmix[rc, hc]
                dpa_ref[rc, hc] = (da * mixed * _gelu_grad(ua[rc, hc], cdf_u[rc, hc])).astype(BF16)
                dmixed = da * u[rc, hc]
                dmb = dmixed.astype(BF16)
                dv_buf[rc, hc] = _dot_tn(wmh, dmb)
                dwm_ref[h] += _dot_nt(dmb, vb)
                db_acc[:, hc] += dmixed
        dv = dv_buf[...]
        dlb_ref[...] += jnp.sum(dv, axis=0, keepdims=True)
        dlg_ref[...] += jnp.sum(dv * xhat, axis=0, keepdims=True)
        dvg = _ln_bwd(dv * lg_ref[...], xhat, rstd)
        dpa_ref[:, D_A:2 * D_A] = (dvg * _gelu_grad(va, cdf_v)).astype(BF16)

        for q in range(H_B):
            cols = slice(q * LANES, (q + 1) * LANES)
            chat, crstd = _ln(c_ref[:, cols])
            z = chat * clg_ref[:, cols] + clb_ref[:, cols]
            sg = _sigmoid(z)
            dz = dmix[:, D_A + q * LANES:D_A + (q + 1) * LANES] * (sg * (1.0 + z * (1.0 - sg)))
            dclb_ref[:, cols] += jnp.sum(dz, axis=0, keepdims=True)
            dclg_ref[:, cols] += jnp.sum(dz * chat, axis=0, keepdims=True)
            dc = _ln_bwd(dz * clg_ref[:, cols], chat, crstd)
            dc_ref[:, cols] = dc
            dcb_ref[:, cols] += jnp.sum(dc, axis=0, keepdims=True)

        @pl.when(i == n_tiles - 1)
        def _():
            for h in range(H_A):
                hc = slice(h * CHUNK, (h + 1) * CHUNK)
                dwm_ref[h] = jnp.where(mask, dwm_ref[h], 0.0)
                dbs_ref[h:h + 1, :] = jnp.sum(db_acc[:, hc].T, axis=0, keepdims=True)

    vec = _full((1, D_A))
    vshape = jax.ShapeDtypeStruct((1, D_A), F32)
    return pl.pallas_call(
        body, name="mixer_bwd_a", grid=(n_tiles,),
        in_specs=[_rows(tm, D), _full((D, D)), _rows(tm, IN_COLS), _rows(tm, D_B), vec, vec,
                  _full((H_A, CHUNK, CHUNK)), _full((CHUNK, H_A)), vec, vec, HBM],
        out_specs=[_rows(tm, 2 * D_A), _rows(tm, D_B), vec, vec, _full((H_A, CHUNK, CHUNK)),
                   _full((H_A, CHUNK)), vec, vec, vec],
        out_shape=[jax.ShapeDtypeStruct((T, 2 * D_A), BF16), jax.ShapeDtypeStruct((T, D_B), F32), vshape, vshape,
                   jax.ShapeDtypeStruct((H_A, CHUNK, CHUNK), F32), jax.ShapeDtypeStruct((H_A, CHUNK), F32),
                   vshape, vshape, vshape],
        scratch_shapes=[pltpu.VMEM((tm, D_A), F32), pltpu.VMEM((CHUNK, D_A), F32)],
        compiler_params=_cparams(32),
    )(*map(_hbm, (dxb, wout, proj, c, lg, lb, wm, bst, clg, clb, dep)))


def _mixer_bwd_b(dc, g, proj, cw, dpa, wg_in, x, g1, dres):
    tm = TM_MIX
    n_tiles = T // tm
    hb = tm // HALO
    bn = wg_in.shape[2]

    def body(dc_ref, dch_ref, g_ref, gh_ref, p_ref, cw_ref, dpa_ref, win_ref, x_ref, g1_ref, dres_ref,
             dp_ref, dcw_ref, dx_ref, dxb_ref, dg1_ref, dcbuf, gbuf, dwacc):
        i = pl.program_id(0)

        @pl.when(i == 0)
        def _():
            dwacc[...] = jnp.zeros(dwacc.shape, F32)
            dg1_ref[...] = jnp.zeros(dg1_ref.shape, F32)

        _fill_shift_buffer(dcbuf, 0, dc_ref[...])
        _fill_shift_buffer(dcbuf, tm, jnp.where(i < n_tiles - 1, dch_ref[...], 0.0))
        _fill_shift_buffer(gbuf, 0, jnp.where(i > 0, gh_ref[...], 0.0))
        _fill_shift_buffer(gbuf, HALO, g_ref[...])
        _build_shifts(dcbuf)
        _build_shifts(gbuf)
        dp_ref[:, 0:2 * D_A] = dpa_ref[...]
        for q in range(H_B):
            cols = slice(q * LANES, (q + 1) * LANES)

            def row_block(rb, carry, q=q, cols=cols):
                base = pl.multiple_of(rb * RB, RB)
                rows = pl.ds(base, RB)
                dg = _conv_taps(dcbuf, cw_ref, q, base, CONV_W - 1, -1)
                val = p_ref[rows, cols]
                sg = _sigmoid(p_ref[rows, D_B + q * LANES:D_B + (q + 1) * LANES])
                dp_ref[rows, 2 * D_A + q * LANES:2 * D_A + (q + 1) * LANES] = (dg * sg).astype(BF16)
                dp_ref[rows, 2 * D_A + D_B + q * LANES:2 * D_A + D_B + (q + 1) * LANES] = (
                    dg * val * sg * (1.0 - sg)).astype(BF16)
                dcv = dcbuf[0, q, rows, :]
                parts = []
                for k in range(CONV_W):
                    prod = dcv * _shifted(gbuf, q, base, HALO - (CONV_W - 1) + k)
                    parts.append(jnp.sum(prod.reshape(RB // 8, 8, LANES), axis=0))
                for k in range(CONV_W):
                    dwacc[k * 8:(k + 1) * 8, cols] += parts[k]
                return carry

            lax.fori_loop(0, tm // RB, row_block, 0)

        dh = jnp.zeros((tm, D), F32)
        for j in range(N_DEV):
            dh = dh + _dot_nt(dp_ref[:, j * bn:(j + 1) * bn], win_ref[j])
        xv = x_ref[...]
        dxn, dg = _rms_bwd(dh, xv, _rms(xv), g1_ref[...])
        dx = dres_ref[...] + dxn
        dx_ref[...] = dx
        dxb_ref[...] = dx.astype(BF16)
        dg1_ref[...] += dg

        @pl.when(i == n_tiles - 1)
        def _():
            for k in range(CONV_W):
                dcw_ref[k:k + 1, :] = jnp.sum(dwacc[k * 8:(k + 1) * 8, :], axis=0, keepdims=True)

    return pl.pallas_call(
        body, name="mixer_bwd_b", grid=(n_tiles,),
        in_specs=[_rows(tm, D_B),
                  pl.BlockSpec((HALO, D_B), lambda i: (jnp.minimum((i + 1) * hb, T // HALO - 1), 0)),
                  _rows(tm, D_B),
                  pl.BlockSpec((HALO, D_B), lambda i: (jnp.maximum(i * hb - 1, 0), 0)),
                  pl.BlockSpec((tm, 2 * D_B), lambda i: (i, 1)),
                  _full((CONV_W, D_B)), _rows(tm, 2 * D_A),
                  pl.BlockSpec(wg_in.shape, lambda i: (0, 0, 0), pipeline_mode=pl.Buffered(1)),
                  _rows(tm, D), _full((1, D)), _rows(tm, D)],
        out_specs=[_rows(tm, IN_COLS), _full((CONV_W, D_B)), _rows(tm, D), _rows(tm, D), _full((1, D))],
        out_shape=[jax.ShapeDtypeStruct((T, IN_COLS), BF16), jax.ShapeDtypeStruct((CONV_W, D_B), F32),
                   jax.ShapeDtypeStruct((T, D), F32), jax.ShapeDtypeStruct((T, D), BF16),
                   jax.ShapeDtypeStruct((1, D), F32)],
        scratch_shapes=[pltpu.VMEM((8, H_B, tm + HALO, LANES), F32), pltpu.VMEM((8, H_B, HALO + tm, LANES), F32),
                        pltpu.VMEM((CONV_W * 8, D_B), F32)],
        compiler_params=_cparams(40),
    )(*map(_hbm, (dc, dc, g, g, proj, cw, dpa, wg_in, x, g1, dres)))


def _wgrad(name, a, g, bn, square_a=False):
    k = a.shape[1]
    n = g.shape[1]
    tk = min(k, 1024)
    tn = min(n, max(bn, 1024))
    nsub = tn // bn
    tt = TT_WGRAD
    nt = T // tt

    def body(a_ref, g_ref, o_ref, ob_ref):
        t = pl.program_id(2)

        @pl.when(t == 0)
        def _():
            o_ref[...] = jnp.zeros(o_ref.shape, F32)

        av = a_ref[...]
        if square_a:
            av = av * av
        for s in range(nsub):
            o_ref[s] += _dot_tn(av, g_ref[:, s * bn:(s + 1) * bn])

        @pl.when(t == nt - 1)
        def _():
            ob_ref[...] = o_ref[...].astype(BF16)

    ospec = pl.BlockSpec((nsub, tk, bn), lambda ki, ni, ti: (ni, ki, 0))
    return pl.pallas_call(
        body, name=name, grid=(k // tk, n // tn, nt),
        in_specs=[pl.BlockSpec((tt, tk), lambda ki, ni, ti: (ti, ki)),
                  pl.BlockSpec((tt, tn), lambda ki, ni, ti: (ti, ni))],
        out_specs=[ospec, ospec],
        out_shape=[jax.ShapeDtypeStruct((n // bn, k, bn), F32), jax.ShapeDtypeStruct((n // bn, k, bn), BF16)],
        compiler_params=_cparams(40, ("arbitrary", "arbitrary", "arbitrary")),
    )(a, g)


def _coords():
    return lax.axis_index("x"), lax.axis_index("y"), lax.axis_index("c")


def _flat(x, y, c):
    return 4 * x + 2 * y + c


def _peer(k):
    x, y, c = _coords()
    return (x ^ ((k >> 2) & 1), y ^ ((k >> 1) & 1), c ^ (k & 1))


def _hbm(a):
    return pltpu.with_memory_space_constraint(a, pltpu.HBM)


def _hbm_like(a):
    return pltpu.HBM(a.shape, a.dtype)


def _peer_sems():
    return pltpu.SemaphoreType.DMA((N_DEV - 1,))


def _place_own(shards):
    n = len(shards)
    shapes = [(s.shape if l is None else s.shape[1:]) for s, l, _ in shards]

    def body(*refs):
        ins, outs = refs[:n], refs[n:2 * n]
        stage_in, stage_out = refs[2 * n:3 * n], refs[3 * n:4 * n]
        in_sems, out_sems = refs[4 * n], refs[4 * n + 1]
        me = _flat(*_coords())
        srcs = [ins[a] if shards[a][1] is None else ins[a].at[shards[a][1]] for a in range(n)]
        loads = [pltpu.make_async_copy(srcs[a], stage_in[a], in_sems.at[a]) for a in range(n)]
        stores = [pltpu.make_async_copy(stage_out[a], outs[a].at[me], out_sems.at[a]) for a in range(n)]
        for cp in loads:
            cp.start()
        for a in range(n):
            loads[a].wait()
            stage_out[a][...] = stage_in[a][...].astype(stage_out[a].dtype)
            stores[a].start()
        for cp in stores:
            cp.wait()

    return pl.pallas_call(
        body, name="place_own", in_specs=[HBM] * n, out_specs=[HBM] * n,
        out_shape=[pltpu.HBM((N_DEV,) + shapes[a], shards[a][2]) for a in range(n)],
        scratch_shapes=[*[pltpu.VMEM(shapes[a], shards[a][0].dtype) for a in range(n)],
                        *[pltpu.VMEM(shapes[a], shards[a][2]) for a in range(n)],
                        pltpu.SemaphoreType.DMA((n,)), pltpu.SemaphoreType.DMA((n,))],
        compiler_params=pltpu.CompilerParams(vmem_limit_bytes=40 * MIB),
    )(*[_hbm(s) for s, _, _ in shards])


SIBLING = 1
CHIP_PEERS = (2, 4, 6)
FIRST_PEERS = (SIBLING,) + CHIP_PEERS


def _gather_start(lands):
    n = len(lands)

    def body(*refs):
        lnd, send, recv = refs[:n], refs[n:2 * n], refs[2 * n:3 * n]
        me = _flat(*_coords())
        for a in range(n):
            for j, k in enumerate(FIRST_PEERS):
                pltpu.make_async_remote_copy(
                    src_ref=lnd[a].at[me], dst_ref=lnd[a].at[me], send_sem=send[a].at[j],
                    recv_sem=recv[a].at[j], device_id=_peer(k), device_id_type=MESH_ID).start()

    sems = pltpu.SemaphoreType.DMA((len(FIRST_PEERS),))
    outs = pl.pallas_call(
        body, name="gather_start",
        out_shape=(*[sems] * (2 * n), *[_hbm_like(l) for l in lands]),
        in_specs=[HBM] * n, out_specs=(*[SEM] * (2 * n), *[HBM] * n),
        input_output_aliases={i: 2 * n + i for i in range(n)},
        compiler_params=pltpu.CompilerParams(has_side_effects=EFFECT),
    )(*[_hbm(l) for l in lands])
    return outs[:n], outs[n:2 * n], outs[2 * n:]


def _gather_pass_on(name, land, send, recv, after):
    def body(l_ref, send_ref, recv_ref, after_ref, send2, recv2, l_out):
        sibling = _peer(SIBLING)
        for j, k in enumerate(CHIP_PEERS):
            cp = pltpu.make_async_remote_copy(
                src_ref=l_ref.at[0], dst_ref=l_ref.at[0], send_sem=send_ref.at[1 + j], recv_sem=recv_ref.at[1 + j],
                device_id=_peer(k), device_id_type=MESH_ID)
            cp.wait_send()
            cp.wait_recv()
            blk = _flat(*_peer(k))
            pltpu.make_async_remote_copy(
                src_ref=l_ref.at[blk], dst_ref=l_ref.at[blk], send_sem=send2.at[j], recv_sem=recv2.at[j],
                device_id=sibling, device_id_type=MESH_ID).start()

    sems = pltpu.SemaphoreType.DMA((len(CHIP_PEERS),))
    return pl.pallas_call(
        body, name=name, out_shape=(sems, sems, _hbm_like(land)),
        in_specs=(HBM, SEM, SEM, HBM), out_specs=(SEM, SEM, HBM), input_output_aliases={0: 2},
        compiler_params=pltpu.CompilerParams(has_side_effects=EFFECT),
    )(land, send, recv, after)


def _gather_wait(name, land, send, recv, send2, recv2, after):
    def body(l_ref, send_ref, recv_ref, send2_ref, recv2_ref, after_ref, l_out):
        sibling = _peer(SIBLING)
        own = pltpu.make_async_remote_copy(
            src_ref=l_ref.at[0], dst_ref=l_ref.at[0], send_sem=send_ref.at[0], recv_sem=recv_ref.at[0],
            device_id=sibling, device_id_type=MESH_ID)
        own.wait_send()
        own.wait_recv()
        for j in range(len(CHIP_PEERS)):
            cp = pltpu.make_async_remote_copy(
                src_ref=l_ref.at[0], dst_ref=l_ref.at[0], send_sem=send2_ref.at[j], recv_sem=recv2_ref.at[j],
                device_id=sibling, device_id_type=MESH_ID)
            cp.wait_send()
            cp.wait_recv()

    return pl.pallas_call(
        body, name=name, out_shape=_hbm_like(land),
        in_specs=(HBM, SEM, SEM, SEM, SEM, HBM), out_specs=HBM, input_output_aliases={0: 0},
        compiler_params=pltpu.CompilerParams(has_side_effects=EFFECT),
    )(land, send, recv, send2, recv2, after)


def _scatter_start(name, grads, lands, layer):
    n = len(grads)

    def body(*refs):
        g, lnd, send, recv = refs[:n], refs[n:2 * n], refs[2 * n:3 * n], refs[3 * n:4 * n]
        for a in range(n):
            for k in range(1, N_DEV):
                to = _peer(k)
                pltpu.make_async_remote_copy(
                    src_ref=g[a].at[_flat(*to)], dst_ref=lnd[a].at[layer, k - 1], send_sem=send[a].at[k - 1],
                    recv_sem=recv[a].at[k - 1], device_id=to, device_id_type=MESH_ID).start()

    outs = pl.pallas_call(
        body, name=name,
        out_shape=(*[_peer_sems()] * (2 * n), *[_hbm_like(g) for g in grads], *[_hbm_like(l) for l in lands]),
        in_specs=[HBM] * (2 * n), out_specs=(*[SEM] * (2 * n), *[HBM] * (2 * n)),
        input_output_aliases={i: 2 * n + i for i in range(2 * n)},
        compiler_params=pltpu.CompilerParams(has_side_effects=EFFECT),
    )(*[_hbm(g) for g in grads], *[_hbm(l) for l in lands])
    return outs[:n], outs[n:2 * n], outs[2 * n:3 * n], outs[3 * n:]


def _scatter_wait(name, grads, lands, sends, recvs, after):
    n, nw = len(grads), len(lands)

    def body(*refs):
        g, lnd = refs[:n], refs[n:n + nw]
        send, recv = refs[n + nw:2 * n + nw], refs[2 * n + nw:3 * n + nw]
        for a in range(n):
            for k in range(1, N_DEV):
                cp = pltpu.make_async_remote_copy(
                    src_ref=g[a].at[0], dst_ref=lnd[a // DEPTH].at[a % DEPTH, 0],
                    send_sem=send[a].at[k - 1], recv_sem=recv[a].at[k - 1],
                    device_id=_peer(k), device_id_type=MESH_ID)
                cp.wait_send()
                cp.wait_recv()

    outs = pl.pallas_call(
        body, name=name,
        out_shape=(*[_hbm_like(g) for g in grads], *[_hbm_like(l) for l in lands]),
        in_specs=(*[HBM] * (n + nw), *[SEM] * (2 * n), HBM), out_specs=[HBM] * (n + nw),
        input_output_aliases={i: i for i in range(n + nw)},
        compiler_params=pltpu.CompilerParams(has_side_effects=EFFECT),
    )(*grads, *lands, *sends, *recvs, after)
    return outs[n:]


def _all_reduce_small(part, deps):
    rows = part.shape[0]
    br = rows // N_DEV
    assert br * N_DEV == rows and br % 8 == 0
    nd = len(deps)

    def body(p_ref, *refs):
        o_ref, slots, send1, recv1, send2, recv2 = refs[nd:]
        me = _flat(*_coords())

        def block(ref, d):
            return ref.at[pl.ds(pl.multiple_of(d * br, 8), br), :]

        slots[me] = p_ref[pl.ds(pl.multiple_of(me * br, 8), br), :]
        scatter, gather = [], []
        for k in range(1, N_DEV):
            to = _peer(k)
            scatter.append(pltpu.make_async_remote_copy(
                src_ref=block(p_ref, _flat(*to)), dst_ref=slots.at[me],
                send_sem=send1.at[k - 1], recv_sem=recv1.at[k - 1], device_id=to, device_id_type=MESH_ID))
            gather.append(pltpu.make_async_remote_copy(
                src_ref=block(o_ref, me), dst_ref=block(o_ref, me),
                send_sem=send2.at[k - 1], recv_sem=recv2.at[k - 1], device_id=to, device_id_type=MESH_ID))
        for cp in scatter:
            cp.start()
        for cp in scatter:
            cp.wait()
        acc = slots[0]
        for d in range(1, N_DEV):
            acc = acc + slots[d]
        o_ref[pl.ds(pl.multiple_of(me * br, 8), br), :] = acc
        for cp in gather:
            cp.start()
        for cp in gather:
            cp.wait()

    return pl.pallas_call(
        body, name="all_reduce_small",
        in_specs=[pl.BlockSpec(memory_space=pltpu.VMEM)] + [HBM] * nd, out_specs=pl.BlockSpec(memory_space=pltpu.VMEM),
        out_shape=jax.ShapeDtypeStruct(part.shape, F32),
        scratch_shapes=[pltpu.VMEM((N_DEV, br, LANES), F32)] + [pltpu.SemaphoreType.DMA((N_DEV - 1,))] * 4,
        compiler_params=pltpu.CompilerParams(vmem_limit_bytes=24 * MIB),
    )(part, *map(_hbm, deps))


def _adam_math(w, g, m, v):
    m = ADAM_B1 * m + (1.0 - ADAM_B1) * g
    v = ADAM_B2 * v + (1.0 - ADAM_B2) * (g * g)
    m_hat = m / (1.0 - ADAM_B1 ** ADAM_STEP)
    v_hat = v / (1.0 - ADAM_B2 ** ADAM_STEP)
    delta = -ADAM_LR * (m_hat / (jnp.sqrt(v_hat) + ADAM_EPS) + ADAM_WD * w)
    return delta, m, v


def _adam_sharded(name, me, g32, recv, w, m, v):
    _, r, c = w.shape
    tr = min(r, 256)
    nblk = r // tr
    assert len(g32) == DEPTH == 2

    def body(me_ref, g0_ref, g1_ref, recv_ref, w_ref, m_ref, v_ref, g_ref, d_ref, nm_ref, nv_ref):
        g = jnp.where(pl.program_id(0) == 0, g0_ref[...], g1_ref[...])
        for k in range(N_DEV - 1):
            g = g + recv_ref[k].astype(F32)
        delta, nm, nv = _adam_math(w_ref[...], g, m_ref[...], v_ref[...])
        g_ref[...] = g
        d_ref[...] = delta
        nm_ref[...] = nm
        nv_ref[...] = nv

    blk = pl.BlockSpec((None, tr, c), lambda l, i, me_ref: (l, i, 0))
    own0 = pl.BlockSpec((None, tr, c), lambda l, i, me_ref: (me_ref[0], jnp.where(l == 0, i, nblk - 1), 0))
    own1 = pl.BlockSpec((None, tr, c), lambda l, i, me_ref: (me_ref[0], jnp.where(l == 1, i, 0), 0))
    shp = jax.ShapeDtypeStruct(w.shape, F32)
    return pl.pallas_call(
        body, name=name,
        grid_spec=pltpu.PrefetchScalarGridSpec(
            num_scalar_prefetch=1, grid=(DEPTH, nblk),
            in_specs=[own0, own1, pl.BlockSpec((None, N_DEV - 1, tr, c), lambda l, i, me_ref: (l, 0, i, 0)),
                      blk, blk, blk],
            out_specs=[blk] * 4),
        out_shape=[shp] * 4,
        compiler_params=_cparams(32, ("arbitrary", "arbitrary")),
    )(me, *[_hbm(a) for a in (*g32, recv, w, m, v)])


def _adam_small(gs, ws, ms, vs):
    n = len(gs)

    def body(*refs):
        g, w, m, v = refs[:n], refs[n:2 * n], refs[2 * n:3 * n], refs[3 * n:4 * n]
        d, nm, nv = refs[4 * n:5 * n], refs[5 * n:6 * n], refs[6 * n:7 * n]
        for a in range(n):
            delta, new_m, new_v = _adam_math(w[a][...], g[a][...], m[a][...], v[a][...])
            d[a][...] = delta
            nm[a][...] = new_m
            nv[a][...] = new_v

    specs = [_full(a.shape) for a in gs]
    shapes = [jax.ShapeDtypeStruct(a.shape, F32) for a in gs]
    outs = pl.pallas_call(
        body, name="adam_small", grid=(1,),
        in_specs=specs * 4, out_specs=specs * 3, out_shape=shapes * 3,
        compiler_params=_cparams(24),
    )(*gs, *ws, *ms, *vs)
    return outs[:n], outs[n:2 * n], outs[2 * n:]


SMALL = ["norm1_g", "sgu_ln_g", "sgu_ln_b", "sgu_w", "sgu_b", "conv_b", "conv_ln_g", "conv_ln_b", "norm2_g",
         "final_g"]


def _pack(arrays):
    flat = jnp.concatenate([a.reshape(-1) for a in arrays])
    pad = (-flat.shape[0]) % (N_DEV * 8 * LANES)
    return jnp.pad(flat, (0, pad)).reshape(-1, LANES)


def _unpack(packed, shapes):
    flat = packed.reshape(-1)
    out, off = [], 0
    for s in shapes:
        size = 1
        for d in s:
            size *= d
        out.append(flat[off:off + size].reshape(s))
        off += size
    return out


def kernel(x, norm1_g, w_in, sgu_ln_g, sgu_ln_b, sgu_w, sgu_b, conv_w, conv_b, conv_ln_g, conv_ln_b, w_out, norm2_g, w_ff1, w_ff2, final_g, loss_target, m_norm1_g, m_w_in, m_sgu_ln_g, m_sgu_ln_b, m_sgu_w, m_sgu_b, m_conv_w, m_conv_b, m_conv_ln_g, m_conv_ln_b, m_w_out, m_norm2_g, m_w_ff1, m_w_ff2, m_final_g, v_norm1_g, v_w_in, v_sgu_ln_g, v_sgu_ln_b, v_sgu_w, v_sgu_b, v_conv_w, v_conv_b, v_conv_ln_g, v_conv_ln_b, v_w_out, v_norm2_g, v_w_ff1, v_w_ff2, v_final_g):
    x2d = x.reshape(T, D)
    tgt = loss_target.reshape(T, D)
    cw_shard = conv_w.reshape(CONV_W, LANES)

    gnames, shards = [], []
    for l in range(DEPTH):
        for k, w in (("w_in", w_in), ("w_out", w_out), ("w_ff1", w_ff1), ("w_ff2", w_ff2)):
            gnames.append(f"{k}{l}")
            shards.append((w, l, BF16))
        if l == 0:
            gnames.insert(1, "conv_w")
            shards.insert(1, (cw_shard, None, F32))
    sends, recvs, lands = _gather_start(_place_own(shards))
    gidx = {k: i for i, k in enumerate(gnames)}

    passed = {}

    def pass_on(k, after):
        i = gidx[k]
        passed[k] = _gather_pass_on("gather_pass_on_" + k, lands[i], sends[i], recvs[i], after)

    def gathered(k, after):
        i = gidx[k]
        send2, recv2, land = passed[k]
        return _gather_wait("gather_wait_" + k, land, sends[i], recvs[i], send2, recv2, after)

    saved = []
    xl = x2d
    cw_full = None
    for l in range(DEPTH):
        g1 = norm1_g[l].reshape(1, D)
        g2 = norm2_g[l].reshape(1, D)
        lg, lb = sgu_ln_g[l].reshape(1, D_A), sgu_ln_b[l].reshape(1, D_A)
        bst = sgu_b[l].T
        cb = conv_b[l].reshape(1, D_B)
        clg, clb = conv_ln_g[l].reshape(1, D_B), conv_ln_b[l].reshape(1, D_B)
        if l == 0:
            pass_on("w_in0", xl)
        wg_in = gathered(f"w_in{l}", xl)
        if l == 0:
            pass_on("conv_w", xl)
        pass_on(f"w_out{l}", xl)
        h, proj = _fwd_in(xl, g1, wg_in)
        if l == 0:
            cw_full = gathered("conv_w", proj).reshape(N_DEV, DEPTH, CONV_W, D_B // N_DEV).transpose(
                1, 2, 0, 3).reshape(DEPTH, CONV_W, D_B)
        wout = gathered(f"w_out{l}", proj).reshape(D, D)
        if l > 0:
            pass_on(f"w_ff1{l}", proj)
        mix, c, g, x1, h2 = _mixer_fwd(proj, lg, lb, sgu_w[l], bst, cw_full[l], cb, clg, clb, wout, xl, g2)
        if l == 0:
            pass_on(f"w_ff1{l}", h2)
        else:
            pass_on(f"w_ff2{l}", h2)
        wg_ff1 = gathered(f"w_ff1{l}", h2)
        r = _fwd_ff1(h2, wg_ff1)
        if l == 0:
            pass_on(f"w_ff2{l}", r)
        if l + 1 < DEPTH:
            pass_on(f"w_in{l + 1}", r)
        w2 = gathered(f"w_ff2{l}", r).reshape(D_FF, D)
        saved.append(dict(x=xl, h=h, proj=proj, mix=mix, c=c, g=g, x1=x1, h2=h2, r=r, wg_in=wg_in, wout=wout,
                          wg_ff1=wg_ff1, w2=w2, g1=g1, g2=g2, lg=lg, lb=lb, bst=bst, clg=clg, clb=clb))
        if l + 1 < DEPTH:
            xl = _fwd_ff2(r, w2, x1)
        else:
            dx, dxb, loss_part, d_final_g = _fwd_ff2_loss(r, w2, x1, final_g.reshape(1, D), tgt)

    names = ["w_in", "w_out", "w_ff1", "w_ff2"]
    block = dict(w_in=(D, IN_COLS // N_DEV), w_out=(D // N_DEV, D), w_ff1=(D, D_FF // N_DEV), w_ff2=(D_FF // N_DEV, D))
    land = {k: lax.empty((DEPTH, N_DEV - 1) + block[k], BF16) for k in names}
    big32 = {k: [None] * DEPTH for k in names}
    big16 = {k: [None] * DEPTH for k in names}
    ssend = {k: [None] * DEPTH for k in names}
    srecv = {k: [None] * DEPTH for k in names}

    def send_grads(l, grads):
        ks = list(grads)
        for k in ks:
            big32[k][l] = grads[k][0].reshape((N_DEV,) + block[k])
        sends, recvs, g16s, lands = _scatter_start(
            "scatter_start_" + "_".join(ks) + str(l), [grads[k][1].reshape((N_DEV,) + block[k]) for k in ks],
            [land[k] for k in ks], l)
        for i, k in enumerate(ks):
            ssend[k][l], srecv[k][l], big16[k][l], land[k] = sends[i], recvs[i], g16s[i], lands[i]

    small = {}
    d_conv_w = [None] * DEPTH
    for l in reversed(range(DEPTH)):
        s = saved[l]
        send_grads(l, dict(w_ff2=_wgrad("wgrad_ff2", s["r"], dxb, D, square_a=True)))
        df1, dx1, dx1b, dg2 = _bwd_mlp(dxb, dx, s["w2"], s["r"], s["wg_ff1"], s["x1"], s["g2"], big16["w_ff2"][l])
        send_grads(l, dict(w_ff1=_wgrad("wgrad_ff1", s["h2"], df1, D_FF // N_DEV),
                           w_out=_wgrad("wgrad_out", s["mix"], dx1b, D)))
        dpa, dc, dlg, dlb, dwm, dbs, dcb, dclg, dclb = _mixer_bwd_a(
            dx1b, s["wout"], s["proj"], s["c"], s["lg"], s["lb"], sgu_w[l], s["bst"], s["clg"], s["clb"],
            big16["w_out"][l])
        dproj, d_conv_w[l], dx, dxb, dg1 = _mixer_bwd_b(dc, s["g"], s["proj"], cw_full[l], dpa, s["wg_in"], s["x"],
                                                         s["g1"], dx1)
        send_grads(l, dict(w_in=_wgrad("wgrad_in", s["h"], dproj, IN_COLS // N_DEV)))
        small[l] = dict(norm1_g=dg1, sgu_ln_g=dlg, sgu_ln_b=dlb, sgu_w=dwm, sgu_b=dbs, conv_b=dcb, conv_ln_g=dclg,
                        conv_ln_b=dclb, norm2_g=dg2)
    grad_x = dx.reshape(1, T, D)

    me_block = _flat(*_coords()).astype(jnp.int32).reshape(1)
    wmv = dict(w_in=(w_in, m_w_in, v_w_in), w_out=(w_out, m_w_out, v_w_out), w_ff1=(w_ff1, m_w_ff1, v_w_ff1),
               w_ff2=(w_ff2, m_w_ff2, v_w_ff2))
    res = {}

    def update(tag, ks, after):
        recv = _scatter_wait("scatter_wait_" + tag, [g for k in ks for g in big16[k]], [land[k] for k in ks],
                             [q for k in ks for q in ssend[k]], [q for k in ks for q in srecv[k]], after)
        for k, rk in zip(ks, recv):
            res[k] = _adam_sharded("adam_" + k, me_block, big32[k], rk, *wmv[k])

    update("early", ["w_ff2", "w_ff1", "w_out"], big16["w_in"][0])

    rep = dict(norm1_g=(norm1_g, m_norm1_g, v_norm1_g), sgu_ln_g=(sgu_ln_g, m_sgu_ln_g, v_sgu_ln_g),
               sgu_ln_b=(sgu_ln_b, m_sgu_ln_b, v_sgu_ln_b), sgu_w=(sgu_w, m_sgu_w, v_sgu_w),
               sgu_b=(sgu_b, m_sgu_b, v_sgu_b), conv_b=(conv_b, m_conv_b, v_conv_b),
               conv_ln_g=(conv_ln_g, m_conv_ln_g, v_conv_ln_g), conv_ln_b=(conv_ln_b, m_conv_ln_b, v_conv_ln_b),
               norm2_g=(norm2_g, m_norm2_g, v_norm2_g), final_g=(final_g, m_final_g, v_final_g))
    parts = []
    for k in SMALL:
        if k == "final_g":
            parts.append(d_final_g.reshape(rep[k][0].shape))
        else:
            parts.append(jnp.stack([small[l][k].reshape(rep[k][0].shape[1:]) for l in range(DEPTH)]))
    parts.append(jnp.stack(d_conv_w))
    parts.append(loss_part[0, 0:1])
    shapes = [p.shape for p in parts]
    packed_sum = _all_reduce_small(_pack(parts), [res[k][0] for k in ("w_ff2", "w_ff1", "w_out")])
    update("late", ["w_in"], packed_sum)
    summed = _unpack(packed_sum, shapes)
    loss = summed[-1][0]
    me = _flat(*_coords())
    g_conv_w = lax.dynamic_slice_in_dim(summed[-2], me * (D_B // N_DEV), D_B // N_DEV, axis=2)
    g_small = summed[:-2] + [g_conv_w]
    keys = SMALL + ["conv_w"]
    rep["conv_w"] = (conv_w, m_conv_w, v_conv_w)
    def two_d(a):
        return a.reshape(1, -1) if a.ndim == 1 else a

    d_s, nm_s, nv_s = _adam_small([two_d(g) for g in g_small], *[[two_d(rep[k][i]) for k in keys] for i in range(3)])
    for i, k in enumerate(keys):
        shape = rep[k][0].shape
        res[k] = (g_small[i], d_s[i].reshape(shape), nm_s[i].reshape(shape), nv_s[i].reshape(shape))

    order = ["norm1_g", "w_in", "sgu_ln_g", "sgu_ln_b", "sgu_w", "sgu_b", "conv_w", "conv_b", "conv_ln_g",
             "conv_ln_b", "w_out", "norm2_g", "w_ff1", "w_ff2", "final_g"]
    return (loss, grad_x, *[res[k][0] for k in order], *[res[k][1] for k in order],
            *[res[k][2] for k in order], *[res[k][3] for k in order])
```

```python
import functools

import jax
import jax.numpy as jnp
from jax import lax
from jax.experimental import pallas as pl
from jax.experimental.pallas import tpu as pltpu

F32 = jnp.float32
BF16 = jnp.bfloat16

N_DEV = 8
DEPTH = 2
T = 4096
D = 1024
D_A = 512
D_B = 512
CHUNK = 128
H_A = 4
H_B = 4
CONV_W = 31
HALO = 32
D_FF = 4096
IN_COLS = 2048
EPS = 1e-6

ADAM_LR = 0.001
ADAM_B1 = 0.9
ADAM_B2 = 0.999
ADAM_EPS = 1e-08
ADAM_WD = 0.01
ADAM_STEP = 10

TM = 512
TM_FWD = 1024
TT_WGRAD = 4096
TM_MIX = 256
TM_SGU = 512
RB = 64
CONV_CHAINS = 2
LANES = 128
MIB = 1024 * 1024
SCOPED_VMEM_MIB = 60

SQRT_HALF = 0.7071067811865476
INV_SQRT_2PI = 0.3989422804014327

MESH_ID = pl.DeviceIdType.MESH
HBM = pl.BlockSpec(memory_space=pltpu.HBM)
SEM = pl.BlockSpec(memory_space=pltpu.SEMAPHORE)
EFFECT = pltpu.SideEffectType.DATAFLOW_SIDE_EFFECTING


def _cparams(vmem_mib, sem=("arbitrary",)):
    assert vmem_mib <= SCOPED_VMEM_MIB
    return pltpu.CompilerParams(dimension_semantics=sem, vmem_limit_bytes=SCOPED_VMEM_MIB * MIB)


def _full(shape):
    return pl.BlockSpec(shape, lambda *_: (0,) * len(shape))


def _rows(tm, cols):
    return pl.BlockSpec((tm, cols), lambda i: (i, 0))


def _gelu(x):
    cdf = 0.5 * (1.0 + lax.erf(x * SQRT_HALF))
    return x * cdf, cdf


def _gelu_grad(x, cdf):
    return cdf + x * (INV_SQRT_2PI * jnp.exp(-0.5 * x * x))


def _sigmoid(x):
    return 1.0 / (1.0 + jnp.exp(-x))


def _ln(x):
    mu = jnp.mean(x, axis=-1, keepdims=True)
    xc = x - mu
    rstd = lax.rsqrt(jnp.mean(xc * xc, axis=-1, keepdims=True) + EPS)
    return xc * rstd, rstd


def _ln_bwd(dyh, xhat, rstd):
    return rstd * (dyh - jnp.mean(dyh, axis=-1, keepdims=True) - xhat * jnp.mean(dyh * xhat, axis=-1, keepdims=True))


def _rms(x):
    return lax.rsqrt(jnp.mean(x * x, axis=-1, keepdims=True) + EPS)


def _rms_bwd(dh, x, r, g):
    n = x * r
    dn = dh * g
    dx = r * (dn - n * jnp.mean(dn * n, axis=-1, keepdims=True))
    return dx, jnp.sum(dh * n, axis=0, keepdims=True)


def _dot(a, b):
    return jnp.dot(a, b, preferred_element_type=F32)


def _dot_nt(a, b):
    return lax.dot_general(a, b, (((1,), (1,)), ((), ())), preferred_element_type=F32)


def _dot_tn(a, b):
    return lax.dot_general(a, b, (((0,), (0,)), ((), ())), preferred_element_type=F32)


def _tril_mask():
    r = lax.broadcasted_iota(jnp.int32, (CHUNK, CHUNK), 0)
    c = lax.broadcasted_iota(jnp.int32, (CHUNK, CHUNK), 1)
    return r >= c


def _fwd_in(x, g1, wg):
    bn = wg.shape[2]

    def body(x_ref, g_ref, w_ref, h_ref, p_ref):
        xv = x_ref[...]
        h = (xv * _rms(xv) * g_ref[...]).astype(BF16)
        h_ref[...] = h
        for j in range(N_DEV):
            p_ref[:, j * bn:(j + 1) * bn] = _dot(h, w_ref[j])

    return pl.pallas_call(
        body, name="fwd_in", grid=(T // TM_FWD,),
        in_specs=[_rows(TM_FWD, D), _full((1, D)), _full(wg.shape)],
        out_specs=[_rows(TM_FWD, D), _rows(TM_FWD, IN_COLS)],
        out_shape=[jax.ShapeDtypeStruct((T, D), BF16), jax.ShapeDtypeStruct((T, IN_COLS), F32)],
        compiler_params=_cparams(32),
    )(*map(_hbm, (x, g1, wg)))


def _fill_shift_buffer(sh_ref, row0, value):
    for q in range(sh_ref.shape[1]):
        sh_ref[0, q, row0:row0 + value.shape[0], :] = value[:, q * LANES:(q + 1) * LANES]


def _build_shifts(sh_ref):
    rows = sh_ref.shape[2]
    for p in range(1, 8):
        for q in range(sh_ref.shape[1]):
            sh_ref[p, q, 0:rows - 8, :] = sh_ref[0, q, p:p + rows - 8, :]


def _shifted(sh_ref, q, base, off):
    start = base + (off - off % 8)
    if not isinstance(start, int):
        start = pl.multiple_of(start, 8)
    return sh_ref[off % 8, q, pl.ds(start, RB), :]


def _conv_taps(sh_ref, w_ref, q, base, first_tap_row, step):
    cols = slice(q * LANES, (q + 1) * LANES)
    acc = [jnp.zeros((RB, LANES), F32) for _ in range(CONV_CHAINS)]
    for k in range(CONV_W):
        term = _shifted(sh_ref, q, base, first_tap_row + step * k) * w_ref[k:k + 1, cols]
        acc[k % CONV_CHAINS] = acc[k % CONV_CHAINS] + term
    return functools.reduce(lambda a, b: a + b, acc)


def _mixer_fwd(proj, lg, lb, wm, bst, cw, cb, clg, clb, wout, x, g2):
    tm = TM_MIX
    hb = tm // HALO

    def body(p_ref, ph_ref, lg_ref, lb_ref, wm_ref, bs_ref, cw_ref, cb_ref, clg_ref, clb_ref, wo_ref, x_ref, g2_ref,
             mix_ref, c_ref, g_ref, x1_ref, h2_ref, gbuf):
        i = pl.program_id(0)
        u, _ = _gelu(p_ref[:, 0:D_A])
        vg, _ = _gelu(p_ref[:, D_A:2 * D_A])
        xhat, _ = _ln(vg)
        v = (xhat * lg_ref[...] + lb_ref[...]).astype(BF16)
        mask = _tril_mask()
        for h in range(H_A):
            hc = slice(h * CHUNK, (h + 1) * CHUNK)
            wmh = jnp.where(mask, wm_ref[h], 0.0).astype(BF16)
            for c in range(tm // CHUNK):
                rc = slice(c * CHUNK, (c + 1) * CHUNK)
                mixed = _dot(wmh, v[rc, hc]) + bs_ref[:, h:h + 1]
                mix_ref[rc, hc] = (u[rc, hc] * mixed).astype(BF16)
        x1_ref[...] = x_ref[...] + _dot(mix_ref[:, 0:D_A], wo_ref[0:D_A, :])

        g = p_ref[:, 2 * D_A:2 * D_A + D_B] * _sigmoid(p_ref[:, 2 * D_A + D_B:IN_COLS])
        g_ref[...] = g
        gh = ph_ref[:, 0:D_B] * _sigmoid(ph_ref[:, D_B:2 * D_B])
        _fill_shift_buffer(gbuf, 0, jnp.where(i > 0, gh, 0.0))
        _fill_shift_buffer(gbuf, HALO, g)
        _build_shifts(gbuf)
        for q in range(H_B):
            cols = slice(q * LANES, (q + 1) * LANES)
            for rb in range(tm // RB):
                acc = _conv_taps(gbuf, cw_ref, q, rb * RB, HALO - (CONV_W - 1), 1)
                c_ref[rb * RB:(rb + 1) * RB, cols] = acc + cb_ref[:, cols]
        for q in range(H_B):
            cols = slice(q * LANES, (q + 1) * LANES)
            chat, _ = _ln(c_ref[:, cols])
            z = chat * clg_ref[:, cols] + clb_ref[:, cols]
            mix_ref[:, D_A + q * LANES:D_A + (q + 1) * LANES] = (z * _sigmoid(z)).astype(BF16)

        x1 = x1_ref[...] + _dot(mix_ref[:, D_A:D], wo_ref[D_A:D, :])
        x1_ref[...] = x1
        h2_ref[...] = (x1 * _rms(x1) * g2_ref[...]).astype(BF16)

    vec = _full((1, D_A))
    return pl.pallas_call(
        body, name="mixer_fwd", grid=(T // tm,),
        in_specs=[_rows(tm, IN_COLS),
                  pl.BlockSpec((HALO, 2 * D_B), lambda i: (jnp.maximum(i * hb - 1, 0), 1)),
                  vec, vec, _full((H_A, CHUNK, CHUNK)), _full((CHUNK, H_A)),
                  _full((CONV_W, D_B)), vec, vec, vec,
                  pl.BlockSpec((D, D), lambda i: (0, 0), pipeline_mode=pl.Buffered(1)), _rows(tm, D), _full((1, D))],
        out_specs=[_rows(tm, D), _rows(tm, D_B), _rows(tm, D_B), _rows(tm, D), _rows(tm, D)],
        out_shape=[jax.ShapeDtypeStruct((T, D), BF16), jax.ShapeDtypeStruct((T, D_B), F32),
                   jax.ShapeDtypeStruct((T, D_B), F32), jax.ShapeDtypeStruct((T, D), F32),
                   jax.ShapeDtypeStruct((T, D), BF16)],
        scratch_shapes=[pltpu.VMEM((8, H_B, HALO + tm, LANES), F32)],
        compiler_params=_cparams(40),
    )(*map(_hbm, (proj, proj, lg, lb, wm, bst, cw, cb, clg, clb, wout, x, g2)))


def _fwd_ff1(h2, wg):
    bn = wg.shape[2]

    def body(h_ref, w_ref, r_ref):
        h = h_ref[...]
        for j in range(N_DEV):
            r_ref[:, j * bn:(j + 1) * bn] = jnp.maximum(_dot(h, w_ref[j]), 0.0).astype(BF16)

    return pl.pallas_call(
        body, name="fwd_ff1", grid=(T // TM_FWD,),
        in_specs=[_rows(TM_FWD, D), _full(wg.shape)],
        out_specs=_rows(TM_FWD, D_FF),
        out_shape=jax.ShapeDtypeStruct((T, D_FF), BF16),
        compiler_params=_cparams(48),
    )(h2, wg)


def _fwd_ff2(r, w2, x1):
    def body(r_ref, w_ref, x_ref, o_ref):
        rv = r_ref[...]
        o_ref[...] = x_ref[...] + _dot(rv * rv, w_ref[...])

    return pl.pallas_call(
        body, name="fwd_ff2", grid=(T // TM_FWD,),
        in_specs=[_rows(TM_FWD, D_FF), _full((D_FF, D)), _rows(TM_FWD, D)],
        out_specs=_rows(TM_FWD, D),
        out_shape=jax.ShapeDtypeStruct((T, D), F32),
        compiler_params=_cparams(48),
    )(r, w2, x1)


def _fwd_ff2_loss(r, w2, x1, gf, tgt):
    def body(r_ref, w_ref, x_ref, g_ref, t_ref, dx_ref, dxb_ref, loss_ref, dg_ref):
        i = pl.program_id(0)

        @pl.when(i == 0)
        def _():
            loss_ref[...] = jnp.zeros(loss_ref.shape, F32)
            dg_ref[...] = jnp.zeros(dg_ref.shape, F32)

        rv = r_ref[...]
        xv = x_ref[...] + _dot(rv * rv, w_ref[...])
        rn = _rms(xv)
        diff = xv * rn * g_ref[...] - t_ref[...]
        loss_ref[...] += 0.5 * jnp.sum(jnp.mean(diff * diff, axis=-1, keepdims=True), axis=0, keepdims=True)
        dx, dg = _rms_bwd(diff * (1.0 / D), xv, rn, g_ref[...])
        dx_ref[...] = dx
        dxb_ref[...] = dx.astype(BF16)
        dg_ref[...] += dg

    return pl.pallas_call(
        body, name="fwd_ff2_loss", grid=(T // TM,),
        in_specs=[_rows(TM, D_FF), pl.BlockSpec((D_FF, D), lambda i: (0, 0), pipeline_mode=pl.Buffered(1)),
                  _rows(TM, D), _full((1, D)), _rows(TM, D)],
        out_specs=[_rows(TM, D), _rows(TM, D), _full((8, LANES)), _full((1, D))],
        out_shape=[jax.ShapeDtypeStruct((T, D), F32), jax.ShapeDtypeStruct((T, D), BF16),
                   jax.ShapeDtypeStruct((8, LANES), F32), jax.ShapeDtypeStruct((1, D), F32)],
        compiler_params=_cparams(40),
    )(*map(_hbm, (r, w2, x1, gf, tgt)))


def _bwd_mlp(dxb, dres, w2, r, wg1, x1, g2, dep):
    bn = wg1.shape[2]

    def body(d_ref, dres_ref, w2_ref, r_ref, w1_ref, x_ref, g_ref, dep_ref, df1_ref, dx_ref, dxb_ref, dg_ref):
        i = pl.program_id(0)

        @pl.when(i == 0)
        def _():
            dg_ref[...] = jnp.zeros(dg_ref.shape, F32)

        d = d_ref[...]
        dh = jnp.zeros((TM, D), F32)
        for j in range(N_DEV):
            cols = slice(j * bn, (j + 1) * bn)
            df1 = (2.0 * r_ref[:, cols].astype(F32) * _dot_nt(d, w2_ref[cols, :])).astype(BF16)
            df1_ref[:, cols] = df1
            dh = dh + _dot_nt(df1, w1_ref[j])
        xv = x_ref[...]
        dxn, dg = _rms_bwd(dh, xv, _rms(xv), g_ref[...])
        dx = dres_ref[...] + dxn
        dx_ref[...] = dx
        dxb_ref[...] = dx.astype(BF16)
        dg_ref[...] += dg

    once = dict(pipeline_mode=pl.Buffered(1))
    return pl.pallas_call(
        body, name="bwd_mlp", grid=(T // TM,),
        in_specs=[_rows(TM, D), _rows(TM, D), pl.BlockSpec((D_FF, D), lambda i: (0, 0), **once), _rows(TM, D_FF),
                  pl.BlockSpec(wg1.shape, lambda i: (0, 0, 0), **once), _rows(TM, D), _full((1, D)), HBM],
        out_specs=[_rows(TM, D_FF), _rows(TM, D), _rows(TM, D), _full((1, D))],
        out_shape=[jax.ShapeDtypeStruct((T, D_FF), BF16), jax.ShapeDtypeStruct((T, D), F32),
                   jax.ShapeDtypeStruct((T, D), BF16), jax.ShapeDtypeStruct((1, D), F32)],
        compiler_params=_cparams(56),
    )(*map(_hbm, (dxb, dres, w2, r, wg1, x1, g2, dep)))


def _mixer_bwd_a(dxb, wout, proj, c, lg, lb, wm, bst, clg, clb, dep):
    tm = TM_SGU
    n_tiles = T // tm

    def body(dx_ref, wo_ref, p_ref, c_ref, lg_ref, lb_ref, wm_ref, bs_ref, clg_ref, clb_ref, dep_ref,
             dpa_ref, dc_ref, dlg_ref, dlb_ref, dwm_ref, dbs_ref, dcb_ref, dclg_ref, dclb_ref,
             dv_buf, db_acc):
        i = pl.program_id(0)

        @pl.when(i == 0)
        def _():
            for ref in (dlg_ref, dlb_ref, dwm_ref, dbs_ref, dcb_ref, dclg_ref, dclb_ref, db_acc):
                ref[...] = jnp.zeros(ref.shape, F32)

        dmix = _dot_nt(dx_ref[...], wo_ref[...])
        ua = p_ref[:, 0:D_A]
        va = p_ref[:, D_A:2 * D_A]
        u, cdf_u = _gelu(ua)
        vg, cdf_v = _gelu(va)
        xhat, rstd = _ln(vg)
        v = (xhat * lg_ref[...] + lb_ref[...]).astype(BF16)
        mask = _tril_mask()
        for h in range(H_A):
            hc = slice(h * CHUNK, (h + 1) * CHUNK)
            wmh = jnp.where(mask, wm_ref[h], 0.0).astype(BF16)
            for cidx in range(tm // CHUNK):
                rc = slice(cidx * CHUNK, (cidx + 1) * CHUNK)
                vb = v[rc, hc]
                mixed = _dot(wmh, vb) + bs_ref[:, h:h + 1]
                da = dmix[rc, hc]
                dpa_ref[rc, hc] = (da * mixed * _gelu_grad(ua[rc, hc], cdf_u[rc, hc])).astype(BF16)
                dmixed = da * u[rc, hc]
                dmb = dmixed.astype(BF16)
                dv_buf[rc, hc] = _dot_tn(wmh, dmb)
                dwm_ref[h] += _dot_nt(dmb, vb)
                db_acc[:, hc] += dmixed
        dv = dv_buf[...]
        dlb_ref[...] += jnp.sum(dv, axis=0, keepdims=True)
        dlg_ref[...] += jnp.sum(dv * xhat, axis=0, keepdims=True)
        dvg = _ln_bwd(dv * lg_ref[...], xhat, rstd)
        dpa_ref[:, D_A:2 * D_A] = (dvg * _gelu_grad(va, cdf_v)).astype(BF16)

        for q in range(H_B):
            cols = slice(q * LANES, (q + 1) * LANES)
            chat, crstd = _ln(c_ref[:, cols])
            z = chat * clg_ref[:, cols] + clb_ref[:, cols]
            sg = _sigmoid(z)
            dz = dmix[:, D_A + q * LANES:D_A + (q + 1) * LANES] * (sg * (1.0 + z * (1.0 - sg)))
            dclb_ref[:, cols] += jnp.sum(dz, axis=0, keepdims=True)
            dclg_ref[:, cols] += jnp.sum(dz * chat, axis=0, keepdims=True)
            dc = _ln_bwd(dz * clg_ref[:, cols], chat, crstd)
            dc_ref[:, cols] = dc
            dcb_ref[:, cols] += jnp.sum(dc, axis=0, keepdims=True)

        @pl.when(i == n_tiles - 1)
        def _():
            for h in range(H_A):
                hc = slice(h * CHUNK, (h + 1) * CHUNK)
                dwm_ref[h] = jnp.where(mask, dwm_ref[h], 0.0)
                dbs_ref[h:h + 1, :] = jnp.sum(db_acc[:, hc].T, axis=0, keepdims=True)

    vec = _full((1, D_A))
    vshape = jax.ShapeDtypeStruct((1, D_A), F32)
    return pl.pallas_call(
        body, name="mixer_bwd_a", grid=(n_tiles,),
        in_specs=[_rows(tm, D), _full((D, D)), _rows(tm, IN_COLS), _rows(tm, D_B), vec, vec,
                  _full((H_A, CHUNK, CHUNK)), _full((CHUNK, H_A)), vec, vec, HBM],
        out_specs=[_rows(tm, 2 * D_A), _rows(tm, D_B), vec, vec, _full((H_A, CHUNK, CHUNK)),
                   _full((H_A, CHUNK)), vec, vec, vec],
        out_shape=[jax.ShapeDtypeStruct((T, 2 * D_A), BF16), jax.ShapeDtypeStruct((T, D_B), F32), vshape, vshape,
                   jax.ShapeDtypeStruct((H_A, CHUNK, CHUNK), F32), jax.ShapeDtypeStruct((H_A, CHUNK), F32),
                   vshape, vshape, vshape],
        scratch_shapes=[pltpu.VMEM((tm, D_A), F32), pltpu.VMEM((CHUNK, D_A), F32)],
        compiler_params=_cparams(32),
    )(*map(_hbm, (dxb, wout, proj, c, lg, lb, wm, bst, clg, clb, dep)))


def _mixer_bwd_b(dc, g, proj, cw, dpa, wg_in, x, g1, dres):
    tm = TM_MIX
    n_tiles = T // tm
    hb = tm // HALO
    bn = wg_in.shape[2]

    def body(dc_ref, dch_ref, g_ref, p_ref, cw_ref, dpa_ref, win_ref, x_ref, g1_ref, dres_ref,
             dp_ref, dcw_ref, dx_ref, dxb_ref, dg1_ref, dcbuf, dwacc):
        i = pl.program_id(0)

        @pl.when(i == 0)
        def _():
            dwacc[...] = jnp.zeros(dwacc.shape, F32)
            dg1_ref[...] = jnp.zeros(dg1_ref.shape, F32)

        _fill_shift_buffer(dcbuf, 0, dc_ref[...])
        _fill_shift_buffer(dcbuf, tm, jnp.where(i < n_tiles - 1, dch_ref[...], 0.0))
        _build_shifts(dcbuf)
        dp_ref[:, 0:2 * D_A] = dpa_ref[...]
        for q in range(H_B):
            cols = slice(q * LANES, (q + 1) * LANES)

            def row_block(rb, carry, q=q, cols=cols):
                base = pl.multiple_of(rb * RB, RB)
                rows = pl.ds(base, RB)
                gv = g_ref[rows, cols]
                acc = [jnp.zeros((RB, LANES), F32) for _ in range(CONV_CHAINS)]
                parts = []
                for k in range(CONV_W):
                    xk = _shifted(dcbuf, q, base, CONV_W - 1 - k)
                    acc[k % CONV_CHAINS] = acc[k % CONV_CHAINS] + xk * cw_ref[k:k + 1, cols]
                    parts.append(jnp.sum((gv * xk).reshape(RB // 8, 8, LANES), axis=0))
                dg = functools.reduce(lambda a, b: a + b, acc)
                val = p_ref[rows, cols]
                sg = _sigmoid(p_ref[rows, D_B + q * LANES:D_B + (q + 1) * LANES])
                dp_ref[rows, 2 * D_A + q * LANES:2 * D_A + (q + 1) * LANES] = (dg * sg).astype(BF16)
                dp_ref[rows, 2 * D_A + D_B + q * LANES:2 * D_A + D_B + (q + 1) * LANES] = (
                    dg * val * sg * (1.0 - sg)).astype(BF16)
                for k in range(CONV_W):
                    dwacc[k * 8:(k + 1) * 8, cols] += parts[k]
                return carry

            lax.fori_loop(0, tm // RB, row_block, 0)

        dh = jnp.zeros((tm, D), F32)
        for j in range(N_DEV):
            dh = dh + _dot_nt(dp_ref[:, j * bn:(j + 1) * bn], win_ref[j])
        xv = x_ref[...]
        dxn, dg = _rms_bwd(dh, xv, _rms(xv), g1_ref[...])
        dx = dres_ref[...] + dxn
        dx_ref[...] = dx
        dxb_ref[...] = dx.astype(BF16)
        dg1_ref[...] += dg

        @pl.when(i == n_tiles - 1)
        def _():
            for k in range(CONV_W):
                dcw_ref[k:k + 1, :] = jnp.sum(dwacc[k * 8:(k + 1) * 8, :], axis=0, keepdims=True)

    return pl.pallas_call(
        body, name="mixer_bwd_b", grid=(n_tiles,),
        in_specs=[_rows(tm, D_B),
                  pl.BlockSpec((HALO, D_B), lambda i: (jnp.minimum((i + 1) * hb, T // HALO - 1), 0)),
                  _rows(tm, D_B),
                  pl.BlockSpec((tm, 2 * D_B), lambda i: (i, 1)),
                  _full((CONV_W, D_B)), _rows(tm, 2 * D_A),
                  pl.BlockSpec(wg_in.shape, lambda i: (0, 0, 0), pipeline_mode=pl.Buffered(1)),
                  _rows(tm, D), _full((1, D)), _rows(tm, D)],
        out_specs=[_rows(tm, IN_COLS), _full((CONV_W, D_B)), _rows(tm, D), _rows(tm, D), _full((1, D))],
        out_shape=[jax.ShapeDtypeStruct((T, IN_COLS), BF16), jax.ShapeDtypeStruct((CONV_W, D_B), F32),
                   jax.ShapeDtypeStruct((T, D), F32), jax.ShapeDtypeStruct((T, D), BF16),
                   jax.ShapeDtypeStruct((1, D), F32)],
        scratch_shapes=[pltpu.VMEM((8, H_B, tm + HALO, LANES), F32), pltpu.VMEM((CONV_W * 8, D_B), F32)],
        compiler_params=_cparams(40),
    )(*map(_hbm, (dc, dc, g, proj, cw, dpa, wg_in, x, g1, dres)))


def _wgrad(name, a, g, bn, square_a=False):
    k = a.shape[1]
    n = g.shape[1]
    tk = min(k, 1024)
    tn = min(n, max(bn, 1024))
    nsub = tn // bn
    tt = TT_WGRAD
    nt = T // tt

    def body(a_ref, g_ref, o_ref, ob_ref):
        t = pl.program_id(2)

        @pl.when(t == 0)
        def _():
            o_ref[...] = jnp.zeros(o_ref.shape, F32)

        av = a_ref[...]
        if square_a:
            av = av * av
        for s in range(nsub):
            o_ref[s] += _dot_tn(av, g_ref[:, s * bn:(s + 1) * bn])

        @pl.when(t == nt - 1)
        def _():
            ob_ref[...] = o_ref[...].astype(BF16)

    ospec = pl.BlockSpec((nsub, tk, bn), lambda ki, ni, ti: (ni, ki, 0))
    return pl.pallas_call(
        body, name=name, grid=(k // tk, n // tn, nt),
        in_specs=[pl.BlockSpec((tt, tk), lambda ki, ni, ti: (ti, ki)),
                  pl.BlockSpec((tt, tn), lambda ki, ni, ti: (ti, ni))],
        out_specs=[ospec, ospec],
        out_shape=[jax.ShapeDtypeStruct((n // bn, k, bn), F32), jax.ShapeDtypeStruct((n // bn, k, bn), BF16)],
        compiler_params=_cparams(40, ("arbitrary", "arbitrary", "arbitrary")),
    )(a, g)


def _coords():
    return lax.axis_index("x"), lax.axis_index("y"), lax.axis_index("c")


def _flat(x, y, c):
    return 4 * x + 2 * y + c


def _peer(k):
    x, y, c = _coords()
    return (x ^ ((k >> 2) & 1), y ^ ((k >> 1) & 1), c ^ (k & 1))


def _hbm(a):
    return pltpu.with_memory_space_constraint(a, pltpu.HBM)


def _hbm_like(a):
    return pltpu.HBM(a.shape, a.dtype)


def _peer_sems():
    return pltpu.SemaphoreType.DMA((N_DEV - 1,))


def _place_own(shards):
    n = len(shards)
    shapes = [(s.shape if l is None else s.shape[1:]) for s, l, _ in shards]

    def body(*refs):
        ins, outs = refs[:n], refs[n:2 * n]
        stage_in, stage_out = refs[2 * n:3 * n], refs[3 * n:4 * n]
        in_sems, out_sems = refs[4 * n], refs[4 * n + 1]
        me = _flat(*_coords())
        srcs = [ins[a] if shards[a][1] is None else ins[a].at[shards[a][1]] for a in range(n)]
        loads = [pltpu.make_async_copy(srcs[a], stage_in[a], in_sems.at[a]) for a in range(n)]
        stores = [pltpu.make_async_copy(stage_out[a], outs[a].at[me], out_sems.at[a]) for a in range(n)]
        for cp in loads:
            cp.start()
        for a in range(n):
            loads[a].wait()
            stage_out[a][...] = stage_in[a][...].astype(stage_out[a].dtype)
            stores[a].start()
        for cp in stores:
            cp.wait()

    return pl.pallas_call(
        body, name="place_own", in_specs=[HBM] * n, out_specs=[HBM] * n,
        out_shape=[pltpu.HBM((N_DEV,) + shapes[a], shards[a][2]) for a in range(n)],
        scratch_shapes=[*[pltpu.VMEM(shapes[a], shards[a][0].dtype) for a in range(n)],
                        *[pltpu.VMEM(shapes[a], shards[a][2]) for a in range(n)],
                        pltpu.SemaphoreType.DMA((n,)), pltpu.SemaphoreType.DMA((n,))],
        compiler_params=pltpu.CompilerParams(vmem_limit_bytes=40 * MIB),
    )(*[_hbm(s) for s, _, _ in shards])


SIBLING = 1
CHIP_PEERS = (2, 4, 6)
FIRST_PEERS = (SIBLING,) + CHIP_PEERS


def _gather_start(lands):
    n = len(lands)

    def body(*refs):
        lnd, send, recv = refs[:n], refs[n:2 * n], refs[2 * n:3 * n]
        me = _flat(*_coords())
        for a in range(n):
            for j, k in enumerate(FIRST_PEERS):
                pltpu.make_async_remote_copy(
                    src_ref=lnd[a].at[me], dst_ref=lnd[a].at[me], send_sem=send[a].at[j],
                    recv_sem=recv[a].at[j], device_id=_peer(k), device_id_type=MESH_ID).start()

    sems = pltpu.SemaphoreType.DMA((len(FIRST_PEERS),))
    outs = pl.pallas_call(
        body, name="gather_start",
        out_shape=(*[sems] * (2 * n), *[_hbm_like(l) for l in lands]),
        in_specs=[HBM] * n, out_specs=(*[SEM] * (2 * n), *[HBM] * n),
        input_output_aliases={i: 2 * n + i for i in range(n)},
        compiler_params=pltpu.CompilerParams(has_side_effects=EFFECT),
    )(*[_hbm(l) for l in lands])
    return outs[:n], outs[n:2 * n], outs[2 * n:]


def _gather_pass_on(name, land, send, recv, after):
    def body(l_ref, send_ref, recv_ref, after_ref, send2, recv2, l_out):
        sibling = _peer(SIBLING)
        for j, k in enumerate(CHIP_PEERS):
            cp = pltpu.make_async_remote_copy(
                src_ref=l_ref.at[0], dst_ref=l_ref.at[0], send_sem=send_ref.at[1 + j], recv_sem=recv_ref.at[1 + j],
                device_id=_peer(k), device_id_type=MESH_ID)
            cp.wait_send()
            cp.wait_recv()
            blk = _flat(*_peer(k))
            pltpu.make_async_remote_copy(
                src_ref=l_ref.at[blk], dst_ref=l_ref.at[blk], send_sem=send2.at[j], recv_sem=recv2.at[j],
                device_id=sibling, device_id_type=MESH_ID).start()

    sems = pltpu.SemaphoreType.DMA((len(CHIP_PEERS),))
    return pl.pallas_call(
        body, name=name, out_shape=(sems, sems, _hbm_like(land)),
        in_specs=(HBM, SEM, SEM, HBM), out_specs=(SEM, SEM, HBM), input_output_aliases={0: 2},
        compiler_params=pltpu.CompilerParams(has_side_effects=EFFECT),
    )(land, send, recv, after)


def _gather_wait(name, land, send, recv, send2, recv2, after):
    def body(l_ref, send_ref, recv_ref, send2_ref, recv2_ref, after_ref, l_out):
        sibling = _peer(SIBLING)
        own = pltpu.make_async_remote_copy(
            src_ref=l_ref.at[0], dst_ref=l_ref.at[0], send_sem=send_ref.at[0], recv_sem=recv_ref.at[0],
            device_id=sibling, device_id_type=MESH_ID)
        own.wait_send()
        own.wait_recv()
        for j in range(len(CHIP_PEERS)):
            cp = pltpu.make_async_remote_copy(
                src_ref=l_ref.at[0], dst_ref=l_ref.at[0], send_sem=send2_ref.at[j], recv_sem=recv2_ref.at[j],
                device_id=sibling, device_id_type=MESH_ID)
            cp.wait_send()
            cp.wait_recv()

    return pl.pallas_call(
        body, name=name, out_shape=_hbm_like(land),
        in_specs=(HBM, SEM, SEM, SEM, SEM, HBM), out_specs=HBM, input_output_aliases={0: 0},
        compiler_params=pltpu.CompilerParams(has_side_effects=EFFECT),
    )(land, send, recv, send2, recv2, after)


def _scatter_start(name, grads, lands, layer):
    n = len(grads)

    def body(*refs):
        g, lnd, send, recv = refs[:n], refs[n:2 * n], refs[2 * n:3 * n], refs[3 * n:4 * n]
        for a in range(n):
            for k in range(1, N_DEV):
                to = _peer(k)
                pltpu.make_async_remote_copy(
                    src_ref=g[a].at[_flat(*to)], dst_ref=lnd[a].at[layer, k - 1], send_sem=send[a].at[k - 1],
                    recv_sem=recv[a].at[k - 1], device_id=to, device_id_type=MESH_ID).start()

    outs = pl.pallas_call(
        body, name=name,
        out_shape=(*[_peer_sems()] * (2 * n), *[_hbm_like(g) for g in grads], *[_hbm_like(l) for l in lands]),
        in_specs=[HBM] * (2 * n), out_specs=(*[SEM] * (2 * n), *[HBM] * (2 * n)),
        input_output_aliases={i: 2 * n + i for i in range(2 * n)},
        compiler_params=pltpu.CompilerParams(has_side_effects=EFFECT),
    )(*[_hbm(g) for g in grads], *[_hbm(l) for l in lands])
    return outs[:n], outs[n:2 * n], outs[2 * n:3 * n], outs[3 * n:]


def _scatter_wait(name, grads, lands, sends, recvs, after):
    n, nw = len(grads), len(lands)

    def body(*refs):
        g, lnd = refs[:n], refs[n:n + nw]
        send, recv = refs[n + nw:2 * n + nw], refs[2 * n + nw:3 * n + nw]
        for a in range(n):
            for k in range(1, N_DEV):
                cp = pltpu.make_async_remote_copy(
                    src_ref=g[a].at[0], dst_ref=lnd[a // DEPTH].at[a % DEPTH, 0],
                    send_sem=send[a].at[k - 1], recv_sem=recv[a].at[k - 1],
                    device_id=_peer(k), device_id_type=MESH_ID)
                cp.wait_send()
                cp.wait_recv()

    outs = pl.pallas_call(
        body, name=name,
        out_shape=(*[_hbm_like(g) for g in grads], *[_hbm_like(l) for l in lands]),
        in_specs=(*[HBM] * (n + nw), *[SEM] * (2 * n), HBM), out_specs=[HBM] * (n + nw),
        input_output_aliases={i: i for i in range(n + nw)},
        compiler_params=pltpu.CompilerParams(has_side_effects=EFFECT),
    )(*grads, *lands, *sends, *recvs, after)
    return outs[n:]


def _all_reduce_small(part, deps):
    rows = part.shape[0]
    br = rows // N_DEV
    assert br * N_DEV == rows and br % 8 == 0
    nd = len(deps)

    def body(p_ref, *refs):
        o_ref, slots, send1, recv1, send2, recv2 = refs[nd:]
        me = _flat(*_coords())

        def block(ref, d):
            return ref.at[pl.ds(pl.multiple_of(d * br, 8), br), :]

        slots[me] = p_ref[pl.ds(pl.multiple_of(me * br, 8), br), :]
        scatter, gather = [], []
        for k in range(1, N_DEV):
            to = _peer(k)
            scatter.append(pltpu.make_async_remote_copy(
                src_ref=block(p_ref, _flat(*to)), dst_ref=slots.at[me],
                send_sem=send1.at[k - 1], recv_sem=recv1.at[k - 1], device_id=to, device_id_type=MESH_ID))
            gather.append(pltpu.make_async_remote_copy(
                src_ref=block(o_ref, me), dst_ref=block(o_ref, me),
                send_sem=send2.at[k - 1], recv_sem=recv2.at[k - 1], device_id=to, device_id_type=MESH_ID))
        for cp in scatter:
            cp.start()
        for cp in scatter:
            cp.wait()
        acc = slots[0]
        for d in range(1, N_DEV):
            acc = acc + slots[d]
        o_ref[pl.ds(pl.multiple_of(me * br, 8), br), :] = acc
        for cp in gather:
            cp.start()
        for cp in gather:
            cp.wait()

    return pl.pallas_call(
        body, name="all_reduce_small",
        in_specs=[pl.BlockSpec(memory_space=pltpu.VMEM)] + [HBM] * nd, out_specs=pl.BlockSpec(memory_space=pltpu.VMEM),
        out_shape=jax.ShapeDtypeStruct(part.shape, F32),
        scratch_shapes=[pltpu.VMEM((N_DEV, br, LANES), F32)] + [pltpu.SemaphoreType.DMA((N_DEV - 1,))] * 4,
        compiler_params=pltpu.CompilerParams(vmem_limit_bytes=24 * MIB),
    )(part, *map(_hbm, deps))


def _adam_math(w, g, m, v):
    m = ADAM_B1 * m + (1.0 - ADAM_B1) * g
    v = ADAM_B2 * v + (1.0 - ADAM_B2) * (g * g)
    m_hat = m / (1.0 - ADAM_B1 ** ADAM_STEP)
    v_hat = v / (1.0 - ADAM_B2 ** ADAM_STEP)
    delta = -ADAM_LR * (m_hat / (jnp.sqrt(v_hat) + ADAM_EPS) + ADAM_WD * w)
    return delta, m, v


def _adam_sharded(name, me, g32, recv, w, m, v):
    _, r, c = w.shape
    tr = min(r, 256)
    nblk = r // tr
    assert len(g32) == DEPTH == 2

    def body(me_ref, g0_ref, g1_ref, recv_ref, w_ref, m_ref, v_ref, g_ref, d_ref, nm_ref, nv_ref):
        g = jnp.where(pl.program_id(0) == 0, g0_ref[...], g1_ref[...])
        for k in range(N_DEV - 1):
            g = g + recv_ref[k].astype(F32)
        delta, nm, nv = _adam_math(w_ref[...], g, m_ref[...], v_ref[...])
        g_ref[...] = g
        d_ref[...] = delta
        nm_ref[...] = nm
        nv_ref[...] = nv

    blk = pl.BlockSpec((None, tr, c), lambda l, i, me_ref: (l, i, 0))
    own0 = pl.BlockSpec((None, tr, c), lambda l, i, me_ref: (me_ref[0], jnp.where(l == 0, i, nblk - 1), 0))
    own1 = pl.BlockSpec((None, tr, c), lambda l, i, me_ref: (me_ref[0], jnp.where(l == 1, i, 0), 0))
    shp = jax.ShapeDtypeStruct(w.shape, F32)
    return pl.pallas_call(
        body, name=name,
        grid_spec=pltpu.PrefetchScalarGridSpec(
            num_scalar_prefetch=1, grid=(DEPTH, nblk),
            in_specs=[own0, own1, pl.BlockSpec((None, N_DEV - 1, tr, c), lambda l, i, me_ref: (l, 0, i, 0)),
                      blk, blk, blk],
            out_specs=[blk] * 4),
        out_shape=[shp] * 4,
        compiler_params=_cparams(32, ("arbitrary", "arbitrary")),
    )(me, *[_hbm(a) for a in (*g32, recv, w, m, v)])


def _adam_small(gs, ws, ms, vs):
    n = len(gs)

    def body(*refs):
        g, w, m, v = refs[:n], refs[n:2 * n], refs[2 * n:3 * n], refs[3 * n:4 * n]
        d, nm, nv = refs[4 * n:5 * n], refs[5 * n:6 * n], refs[6 * n:7 * n]
        for a in range(n):
            delta, new_m, new_v = _adam_math(w[a][...], g[a][...], m[a][...], v[a][...])
            d[a][...] = delta
            nm[a][...] = new_m
            nv[a][...] = new_v

    specs = [_full(a.shape) for a in gs]
    shapes = [jax.ShapeDtypeStruct(a.shape, F32) for a in gs]
    outs = pl.pallas_call(
        body, name="adam_small", grid=(1,),
        in_specs=specs * 4, out_specs=specs * 3, out_shape=shapes * 3,
        compiler_params=_cparams(24),
    )(*gs, *ws, *ms, *vs)
    return outs[:n], outs[n:2 * n], outs[2 * n:]


SMALL = ["norm1_g", "sgu_ln_g", "sgu_ln_b", "sgu_w", "sgu_b", "conv_b", "conv_ln_g", "conv_ln_b", "norm2_g",
         "final_g"]


def _pack(arrays):
    flat = jnp.concatenate([a.reshape(-1) for a in arrays])
    pad = (-flat.shape[0]) % (N_DEV * 8 * LANES)
    return jnp.pad(flat, (0, pad)).reshape(-1, LANES)


def _unpack(packed, shapes):
    flat = packed.reshape(-1)
    out, off = [], 0
    for s in shapes:
        size = 1
        for d in s:
            size *= d
        out.append(flat[off:off + size].reshape(s))
        off += size
    return out


def kernel(x, norm1_g, w_in, sgu_ln_g, sgu_ln_b, sgu_w, sgu_b, conv_w, conv_b, conv_ln_g, conv_ln_b, w_out, norm2_g, w_ff1, w_ff2, final_g, loss_target, m_norm1_g, m_w_in, m_sgu_ln_g, m_sgu_ln_b, m_sgu_w, m_sgu_b, m_conv_w, m_conv_b, m_conv_ln_g, m_conv_ln_b, m_w_out, m_norm2_g, m_w_ff1, m_w_ff2, m_final_g, v_norm1_g, v_w_in, v_sgu_ln_g, v_sgu_ln_b, v_sgu_w, v_sgu_b, v_conv_w, v_conv_b, v_conv_ln_g, v_conv_ln_b, v_w_out, v_norm2_g, v_w_ff1, v_w_ff2, v_final_g):
    x2d = x.reshape(T, D)
    tgt = loss_target.reshape(T, D)
    cw_shard = conv_w.reshape(CONV_W, LANES)

    gnames, shards = [], []
    for l in range(DEPTH):
        for k, w in (("w_in", w_in), ("w_out", w_out), ("w_ff1", w_ff1), ("w_ff2", w_ff2)):
            gnames.append(f"{k}{l}")
            shards.append((w, l, BF16))
        if l == 0:
            gnames.insert(1, "conv_w")
            shards.insert(1, (cw_shard, None, F32))
    sends, recvs, lands = _gather_start(_place_own(shards))
    gidx = {k: i for i, k in enumerate(gnames)}

    passed = {}

    def pass_on(k, after):
        i = gidx[k]
        passed[k] = _gather_pass_on("gather_pass_on_" + k, lands[i], sends[i], recvs[i], after)

    def gathered(k, after):
        i = gidx[k]
        send2, recv2, land = passed[k]
        return _gather_wait("gather_wait_" + k, land, sends[i], recvs[i], send2, recv2, after)

    saved = []
    xl = x2d
    cw_full = None
    for l in range(DEPTH):
        g1 = norm1_g[l].reshape(1, D)
        g2 = norm2_g[l].reshape(1, D)
        lg, lb = sgu_ln_g[l].reshape(1, D_A), sgu_ln_b[l].reshape(1, D_A)
        bst = sgu_b[l].T
        cb = conv_b[l].reshape(1, D_B)
        clg, clb = conv_ln_g[l].reshape(1, D_B), conv_ln_b[l].reshape(1, D_B)
        if l == 0:
            pass_on("w_in0", xl)
        wg_in = gathered(f"w_in{l}", xl)
        if l == 0:
            pass_on("conv_w", xl)
        pass_on(f"w_out{l}", xl)
        h, proj = _fwd_in(xl, g1, wg_in)
        if l == 0:
            cw_full = gathered("conv_w", proj).reshape(N_DEV, DEPTH, CONV_W, D_B // N_DEV).transpose(
                1, 2, 0, 3).reshape(DEPTH, CONV_W, D_B)
        wout = gathered(f"w_out{l}", proj).reshape(D, D)
        if l > 0:
            pass_on(f"w_ff1{l}", proj)
        mix, c, g, x1, h2 = _mixer_fwd(proj, lg, lb, sgu_w[l], bst, cw_full[l], cb, clg, clb, wout, xl, g2)
        if l == 0:
            pass_on(f"w_ff1{l}", h2)
        else:
            pass_on(f"w_ff2{l}", h2)
        wg_ff1 = gathered(f"w_ff1{l}", h2)
        r = _fwd_ff1(h2, wg_ff1)
        if l == 0:
            pass_on(f"w_ff2{l}", r)
        if l + 1 < DEPTH:
            pass_on(f"w_in{l + 1}", r)
        w2 = gathered(f"w_ff2{l}", r).reshape(D_FF, D)
        saved.append(dict(x=xl, h=h, proj=proj, mix=mix, c=c, g=g, x1=x1, h2=h2, r=r, wg_in=wg_in, wout=wout,
                          wg_ff1=wg_ff1, w2=w2, g1=g1, g2=g2, lg=lg, lb=lb, bst=bst, clg=clg, clb=clb))
        if l + 1 < DEPTH:
            xl = _fwd_ff2(r, w2, x1)
        else:
            dx, dxb, loss_part, d_final_g = _fwd_ff2_loss(r, w2, x1, final_g.reshape(1, D), tgt)

    names = ["w_in", "w_out", "w_ff1", "w_ff2"]
    block = dict(w_in=(D, IN_COLS // N_DEV), w_out=(D // N_DEV, D), w_ff1=(D, D_FF // N_DEV), w_ff2=(D_FF // N_DEV, D))
    land = {k: lax.empty((DEPTH, N_DEV - 1) + block[k], BF16) for k in names}
    big32 = {k: [None] * DEPTH for k in names}
    big16 = {k: [None] * DEPTH for k in names}
    ssend = {k: [None] * DEPTH for k in names}
    srecv = {k: [None] * DEPTH for k in names}

    def send_grads(l, grads):
        ks = list(grads)
        for k in ks:
            big32[k][l] = grads[k][0].reshape((N_DEV,) + block[k])
        sends, recvs, g16s, lands = _scatter_start(
            "scatter_start_" + "_".join(ks) + str(l), [grads[k][1].reshape((N_DEV,) + block[k]) for k in ks],
            [land[k] for k in ks], l)
        for i, k in enumerate(ks):
            ssend[k][l], srecv[k][l], big16[k][l], land[k] = sends[i], recvs[i], g16s[i], lands[i]

    small = {}
    d_conv_w = [None] * DEPTH
    for l in reversed(range(DEPTH)):
        s = saved[l]
        send_grads(l, dict(w_ff2=_wgrad("wgrad_ff2", s["r"], dxb, D, square_a=True)))
        df1, dx1, dx1b, dg2 = _bwd_mlp(dxb, dx, s["w2"], s["r"], s["wg_ff1"], s["x1"], s["g2"], big16["w_ff2"][l])
        send_grads(l, dict(w_ff1=_wgrad("wgrad_ff1", s["h2"], df1, D_FF // N_DEV),
                           w_out=_wgrad("wgrad_out", s["mix"], dx1b, D)))
        dpa, dc, dlg, dlb, dwm, dbs, dcb, dclg, dclb = _mixer_bwd_a(
            dx1b, s["wout"], s["proj"], s["c"], s["lg"], s["lb"], sgu_w[l], s["bst"], s["clg"], s["clb"],
            big16["w_out"][l])
        dproj, d_conv_w[l], dx, dxb, dg1 = _mixer_bwd_b(dc, s["g"], s["proj"], cw_full[l], dpa, s["wg_in"], s["x"],
                                                         s["g1"], dx1)
        send_grads(l, dict(w_in=_wgrad("wgrad_in", s["h"], dproj, IN_COLS // N_DEV)))
        small[l] = dict(norm1_g=dg1, sgu_ln_g=dlg, sgu_ln_b=dlb, sgu_w=dwm, sgu_b=dbs, conv_b=dcb, conv_ln_g=dclg,
                        conv_ln_b=dclb, norm2_g=dg2)
    grad_x = dx.reshape(1, T, D)

    me_block = _flat(*_coords()).astype(jnp.int32).reshape(1)
    wmv = dict(w_in=(w_in, m_w_in, v_w_in), w_out=(w_out, m_w_out, v_w_out), w_ff1=(w_ff1, m_w_ff1, v_w_ff1),
               w_ff2=(w_ff2, m_w_ff2, v_w_ff2))
    res = {}

    def update(tag, ks, after):
        recv = _scatter_wait("scatter_wait_" + tag, [g for k in ks for g in big16[k]], [land[k] for k in ks],
                             [q for k in ks for q in ssend[k]], [q for k in ks for q in srecv[k]], after)
        for k, rk in zip(ks, recv):
            res[k] = _adam_sharded("adam_" + k, me_block, big32[k], rk, *wmv[k])

    update("early", ["w_ff2", "w_ff1", "w_out"], big16["w_in"][0])

    rep = dict(norm1_g=(norm1_g, m_norm1_g, v_norm1_g), sgu_ln_g=(sgu_ln_g, m_sgu_ln_g, v_sgu_ln_g),
               sgu_ln_b=(sgu_ln_b, m_sgu_ln_b, v_sgu_ln_b), sgu_w=(sgu_w, m_sgu_w, v_sgu_w),
               sgu_b=(sgu_b, m_sgu_b, v_sgu_b), conv_b=(conv_b, m_conv_b, v_conv_b),
               conv_ln_g=(conv_ln_g, m_conv_ln_g, v_conv_ln_g), conv_ln_b=(conv_ln_b, m_conv_ln_b, v_conv_ln_b),
               norm2_g=(norm2_g, m_norm2_g, v_norm2_g), final_g=(final_g, m_final_g, v_final_g))
    parts = []
    for k in SMALL:
        if k == "final_g":
            parts.append(d_final_g.reshape(rep[k][0].shape))
        else:
            parts.append(jnp.stack([small[l][k].reshape(rep[k][0].shape[1:]) for l in range(DEPTH)]))
    parts.append(jnp.stack(d_conv_w))
    parts.append(loss_part[0, 0:1])
    shapes = [p.shape for p in parts]
    packed_sum = _all_reduce_small(_pack(parts), [res[k][0] for k in ("w_ff2", "w_ff1", "w_out")])
    update("late", ["w_in"], packed_sum)
    summed = _unpack(packed_sum, shapes)
    loss = summed[-1][0]
    me = _flat(*_coords())
    g_conv_w = lax.dynamic_slice_in_dim(summed[-2], me * (D_B // N_DEV), D_B // N_DEV, axis=2)
    g_small = summed[:-2] + [g_conv_w]
    keys = SMALL + ["conv_w"]
    rep["conv_w"] = (conv_w, m_conv_w, v_conv_w)
    def two_d(a):
        return a.reshape(1, -1) if a.ndim == 1 else a

    d_s, nm_s, nv_s = _adam_small([two_d(g) for g in g_small], *[[two_d(rep[k][i]) for k in keys] for i in range(3)])
    for i, k in enumerate(keys):
        shape = rep[k][0].shape
        res[k] = (g_small[i], d_s[i].reshape(shape), nm_s[i].reshape(shape), nv_s[i].reshape(shape))

    order = ["norm1_g", "w_in", "sgu_ln_g", "sgu_ln_b", "sgu_w", "sgu_b", "conv_w", "conv_b", "conv_ln_g",
             "conv_ln_b", "w_out", "norm2_g", "w_ff1", "w_ff2", "final_g"]
    return (loss, grad_x, *[res[k][0] for k in order], *[res[k][1] for k in order],
            *[res[k][2] for k in order], *[res[k][3] for k in order])
```

```python
import functools

import jax
import jax.numpy as jnp
from jax import lax
from jax.experimental import pallas as pl
from jax.experimental.pallas import tpu as pltpu

F32 = jnp.float32
BF16 = jnp.bfloat16

N_DEV = 8
DEPTH = 2
T = 4096
D = 1024
D_A = 512
D_B = 512
CHUNK = 128
H_A = 4
H_B = 4
CONV_W = 31
HALO = 32
D_FF = 4096
IN_COLS = 2048
EPS = 1e-6

ADAM_LR = 0.001
ADAM_B1 = 0.9
ADAM_B2 = 0.999
ADAM_EPS = 1e-08
ADAM_WD = 0.01
ADAM_STEP = 10

TM = 512
TM_FWD = 1024
TT_WGRAD = 4096
TM_MIX = 256
TM_SGU = 512
RB = 64
CONV_CHAINS = 2
LANES = 128
MIB = 1024 * 1024
SCOPED_VMEM_MIB = 60

SQRT_HALF = 0.7071067811865476
INV_SQRT_2PI = 0.3989422804014327

MESH_ID = pl.DeviceIdType.MESH
HBM = pl.BlockSpec(memory_space=pltpu.HBM)
SEM = pl.BlockSpec(memory_space=pltpu.SEMAPHORE)
EFFECT = pltpu.SideEffectType.DATAFLOW_SIDE_EFFECTING


def _cparams(vmem_mib, sem=("arbitrary",)):
    assert vmem_mib <= SCOPED_VMEM_MIB
    return pltpu.CompilerParams(dimension_semantics=sem, vmem_limit_bytes=SCOPED_VMEM_MIB * MIB)


def _full(shape):
    return pl.BlockSpec(shape, lambda *_: (0,) * len(shape))


def _rows(tm, cols):
    return pl.BlockSpec((tm, cols), lambda i: (i, 0))


def _gelu(x):
    cdf = 0.5 * (1.0 + lax.erf(x * SQRT_HALF))
    return x * cdf, cdf


def _gelu_grad(x, cdf):
    return cdf + x * (INV_SQRT_2PI * jnp.exp(-0.5 * x * x))


def _sigmoid(x):
    return 1.0 / (1.0 + jnp.exp(-x))


def _ln(x):
    mu = jnp.mean(x, axis=-1, keepdims=True)
    xc = x - mu
    rstd = lax.rsqrt(jnp.mean(xc * xc, axis=-1, keepdims=True) + EPS)
    return xc * rstd, rstd


def _ln_bwd(dyh, xhat, rstd):
    return rstd * (dyh - jnp.mean(dyh, axis=-1, keepdims=True) - xhat * jnp.mean(dyh * xhat, axis=-1, keepdims=True))


def _rms(x):
    return lax.rsqrt(jnp.mean(x * x, axis=-1, keepdims=True) + EPS)


def _rms_bwd(dh, x, r, g):
    n = x * r
    dn = dh * g
    dx = r * (dn - n * jnp.mean(dn * n, axis=-1, keepdims=True))
    return dx, jnp.sum(dh * n, axis=0, keepdims=True)


def _dot(a, b):
    return jnp.dot(a, b, preferred_element_type=F32)


def _dot_nt(a, b):
    return lax.dot_general(a, b, (((1,), (1,)), ((), ())), preferred_element_type=F32)


def _dot_tn(a, b):
    return lax.dot_general(a, b, (((0,), (0,)), ((), ())), preferred_element_type=F32)


def _tril_mask():
    r = lax.broadcasted_iota(jnp.int32, (CHUNK, CHUNK), 0)
    c = lax.broadcasted_iota(jnp.int32, (CHUNK, CHUNK), 1)
    return r >= c


def _fwd_in(x, g1, wg):
    bn = wg.shape[2]

    def body(x_ref, g_ref, w_ref, h_ref, p_ref):
        xv = x_ref[...]
        h = (xv * _rms(xv) * g_ref[...]).astype(BF16)
        h_ref[...] = h
        for j in range(N_DEV):
            p_ref[:, j * bn:(j + 1) * bn] = _dot(h, w_ref[j])

    return pl.pallas_call(
        body, name="fwd_in", grid=(T // TM_FWD,),
        in_specs=[_rows(TM_FWD, D), _full((1, D)), _full(wg.shape)],
        out_specs=[_rows(TM_FWD, D), _rows(TM_FWD, IN_COLS)],
        out_shape=[jax.ShapeDtypeStruct((T, D), BF16), jax.ShapeDtypeStruct((T, IN_COLS), F32)],
        compiler_params=_cparams(32),
    )(*map(_hbm, (x, g1, wg)))


def _fill_shift_buffer(sh_ref, row0, value):
    for q in range(sh_ref.shape[1]):
        sh_ref[0, q, row0:row0 + value.shape[0], :] = value[:, q * LANES:(q + 1) * LANES]


def _build_shifts(sh_ref):
    rows = sh_ref.shape[2]
    for p in range(1, 8):
        for q in range(sh_ref.shape[1]):
            sh_ref[p, q, 0:rows - 8, :] = sh_ref[0, q, p:p + rows - 8, :]


def _shifted(sh_ref, q, base, off):
    start = base + (off - off % 8)
    if not isinstance(start, int):
        start = pl.multiple_of(start, 8)
    return sh_ref[off % 8, q, pl.ds(start, RB), :]


def _conv_taps(sh_ref, w_ref, q, base, first_tap_row, step):
    cols = slice(q * LANES, (q + 1) * LANES)
    acc = [jnp.zeros((RB, LANES), F32) for _ in range(CONV_CHAINS)]
    for k in range(CONV_W):
        term = _shifted(sh_ref, q, base, first_tap_row + step * k) * w_ref[k:k + 1, cols]
        acc[k % CONV_CHAINS] = acc[k % CONV_CHAINS] + term
    return functools.reduce(lambda a, b: a + b, acc)


def _mixer_fwd(proj, lg, lb, wm, bst, cw, cb, clg, clb, wout, x, g2):
    tm = TM_MIX
    hb = tm // HALO

    def body(p_ref, ph_ref, lg_ref, lb_ref, wm_ref, bs_ref, cw_ref, cb_ref, clg_ref, clb_ref, wo_ref, x_ref, g2_ref,
             mix_ref, c_ref, g_ref, x1_ref, h2_ref, gbuf):
        i = pl.program_id(0)
        u, _ = _gelu(p_ref[:, 0:D_A])
        vg, _ = _gelu(p_ref[:, D_A:2 * D_A])
        xhat, _ = _ln(vg)
        v = (xhat * lg_ref[...] + lb_ref[...]).astype(BF16)
        mask = _tril_mask()
        for h in range(H_A):
            hc = slice(h * CHUNK, (h + 1) * CHUNK)
            wmh = jnp.where(mask, wm_ref[h], 0.0).astype(BF16)
            for c in range(tm // CHUNK):
                rc = slice(c * CHUNK, (c + 1) * CHUNK)
                mixed = _dot(wmh, v[rc, hc]) + bs_ref[:, h:h + 1]
                mix_ref[rc, hc] = (u[rc, hc] * mixed).astype(BF16)
        x1_ref[...] = x_ref[...] + _dot(mix_ref[:, 0:D_A], wo_ref[0:D_A, :])

        g = p_ref[:, 2 * D_A:2 * D_A + D_B] * _sigmoid(p_ref[:, 2 * D_A + D_B:IN_COLS])
        g_ref[...] = g
        gh = ph_ref[:, 0:D_B] * _sigmoid(ph_ref[:, D_B:2 * D_B])
        _fill_shift_buffer(gbuf, 0, jnp.where(i > 0, gh, 0.0))
        _fill_shift_buffer(gbuf, HALO, g)
        _build_shifts(gbuf)
        for q in range(H_B):
            cols = slice(q * LANES, (q + 1) * LANES)
            for rb in range(tm // RB):
                acc = _conv_taps(gbuf, cw_ref, q, rb * RB, HALO - (CONV_W - 1), 1)
                c_ref[rb * RB:(rb + 1) * RB, cols] = acc + cb_ref[:, cols]
        for q in range(H_B):
            cols = slice(q * LANES, (q + 1) * LANES)
            chat, _ = _ln(c_ref[:, cols])
            z = chat * clg_ref[:, cols] + clb_ref[:, cols]
            mix_ref[:, D_A + q * LANES:D_A + (q + 1) * LANES] = (z * _sigmoid(z)).astype(BF16)

        x1 = x1_ref[...] + _dot(mix_ref[:, D_A:D], wo_ref[D_A:D, :])
        x1_ref[...] = x1
        h2_ref[...] = (x1 * _rms(x1) * g2_ref[...]).astype(BF16)

    vec = _full((1, D_A))
    return pl.pallas_call(
        body, name="mixer_fwd", grid=(T // tm,),
        in_specs=[_rows(tm, IN_COLS),
                  pl.BlockSpec((HALO, 2 * D_B), lambda i: (jnp.maximum(i * hb - 1, 0), 1)),
                  vec, vec, _full((H_A, CHUNK, CHUNK)), _full((CHUNK, H_A)),
                  _full((CONV_W, D_B)), vec, vec, vec,
                  pl.BlockSpec((D, D), lambda i: (0, 0), pipeline_mode=pl.Buffered(1)), _rows(tm, D), _full((1, D))],
        out_specs=[_rows(tm, D), _rows(tm, D_B), _rows(tm, D_B), _rows(tm, D), _rows(tm, D)],
        out_shape=[jax.ShapeDtypeStruct((T, D), BF16), jax.ShapeDtypeStruct((T, D_B), F32),
                   jax.ShapeDtypeStruct((T, D_B), F32), jax.ShapeDtypeStruct((T, D), F32),
                   jax.ShapeDtypeStruct((T, D), BF16)],
        scratch_shapes=[pltpu.VMEM((8, H_B, HALO + tm, LANES), F32)],
        compiler_params=_cparams(40),
    )(*map(_hbm, (proj, proj, lg, lb, wm, bst, cw, cb, clg, clb, wout, x, g2)))


def _fwd_ff1(h2, wg):
    bn = wg.shape[2]

    def body(h_ref, w_ref, r_ref):
        h = h_ref[...]
        for j in range(N_DEV):
            r_ref[:, j * bn:(j + 1) * bn] = jnp.maximum(_dot(h, w_ref[j]), 0.0).astype(BF16)

    return pl.pallas_call(
        body, name="fwd_ff1", grid=(T // TM_FWD,),
        in_specs=[_rows(TM_FWD, D), _full(wg.shape)],
        out_specs=_rows(TM_FWD, D_FF),
        out_shape=jax.ShapeDtypeStruct((T, D_FF), BF16),
        compiler_params=_cparams(48),
    )(h2, wg)


def _fwd_ff2(r, w2, x1):
    def body(r_ref, w_ref, x_ref, o_ref):
        rv = r_ref[...]
        o_ref[...] = x_ref[...] + _dot(rv * rv, w_ref[...])

    return pl.pallas_call(
        body, name="fwd_ff2", grid=(T // TM_FWD,),
        in_specs=[_rows(TM_FWD, D_FF), _full((D_FF, D)), _rows(TM_FWD, D)],
        out_specs=_rows(TM_FWD, D),
        out_shape=jax.ShapeDtypeStruct((T, D), F32),
        compiler_params=_cparams(48),
    )(r, w2, x1)


def _fwd_ff2_loss(r, w2, x1, gf, tgt):
    def body(r_ref, w_ref, x_ref, g_ref, t_ref, dx_ref, dxb_ref, loss_ref, dg_ref):
        i = pl.program_id(0)

        @pl.when(i == 0)
        def _():
            loss_ref[...] = jnp.zeros(loss_ref.shape, F32)
            dg_ref[...] = jnp.zeros(dg_ref.shape, F32)

        rv = r_ref[...]
        xv = x_ref[...] + _dot(rv * rv, w_ref[...])
        rn = _rms(xv)
        diff = xv * rn * g_ref[...] - t_ref[...]
        loss_ref[...] += 0.5 * jnp.sum(jnp.mean(diff * diff, axis=-1, keepdims=True), axis=0, keepdims=True)
        dx, dg = _rms_bwd(diff * (1.0 / D), xv, rn, g_ref[...])
        dx_ref[...] = dx
        dxb_ref[...] = dx.astype(BF16)
        dg_ref[...] += dg

    return pl.pallas_call(
        body, name="fwd_ff2_loss", grid=(T // TM,),
        in_specs=[_rows(TM, D_FF), pl.BlockSpec((D_FF, D), lambda i: (0, 0), pipeline_mode=pl.Buffered(1)),
                  _rows(TM, D), _full((1, D)), _rows(TM, D)],
        out_specs=[_rows(TM, D), _rows(TM, D), _full((8, LANES)), _full((1, D))],
        out_shape=[jax.ShapeDtypeStruct((T, D), F32), jax.ShapeDtypeStruct((T, D), BF16),
                   jax.ShapeDtypeStruct((8, LANES), F32), jax.ShapeDtypeStruct((1, D), F32)],
        compiler_params=_cparams(40),
    )(*map(_hbm, (r, w2, x1, gf, tgt)))


def _bwd_mlp(dxb, dres, w2, r, wg1, x1, g2, dep):
    bn = wg1.shape[2]

    def body(d_ref, dres_ref, w2_ref, r_ref, w1_ref, x_ref, g_ref, dep_ref, df1_ref, dx_ref, dxb_ref, dg_ref):
        i = pl.program_id(0)

        @pl.when(i == 0)
        def _():
            dg_ref[...] = jnp.zeros(dg_ref.shape, F32)

        d = d_ref[...]
        dh = jnp.zeros((TM, D), F32)
        for j in range(N_DEV):
            cols = slice(j * bn, (j + 1) * bn)
            df1 = (2.0 * r_ref[:, cols].astype(F32) * _dot_nt(d, w2_ref[cols, :])).astype(BF16)
            df1_ref[:, cols] = df1
            dh = dh + _dot_nt(df1, w1_ref[j])
        xv = x_ref[...]
        dxn, dg = _rms_bwd(dh, xv, _rms(xv), g_ref[...])
        dx = dres_ref[...] + dxn
        dx_ref[...] = dx
        dxb_ref[...] = dx.astype(BF16)
        dg_ref[...] += dg

    once = dict(pipeline_mode=pl.Buffered(1))
    return pl.pallas_call(
        body, name="bwd_mlp", grid=(T // TM,),
        in_specs=[_rows(TM, D), _rows(TM, D), pl.BlockSpec((D_FF, D), lambda i: (0, 0), **once), _rows(TM, D_FF),
                  pl.BlockSpec(wg1.shape, lambda i: (0, 0, 0), **once), _rows(TM, D), _full((1, D)), HBM],
        out_specs=[_rows(TM, D_FF), _rows(TM, D), _rows(TM, D), _full((1, D))],
        out_shape=[jax.ShapeDtypeStruct((T, D_FF), BF16), jax.ShapeDtypeStruct((T, D), F32),
                   jax.ShapeDtypeStruct((T, D), BF16), jax.ShapeDtypeStruct((1, D), F32)],
        compiler_params=_cparams(56),
    )(*map(_hbm, (dxb, dres, w2, r, wg1, x1, g2, dep)))


def _mixer_bwd_a(dxb, wout, proj, c, lg, lb, wm, bst, clg, clb, dep):
    tm = TM_SGU
    n_tiles = T // tm

    def body(dx_ref, wo_ref, p_ref, c_ref, lg_ref, lb_ref, wm_ref, bs_ref, clg_ref, clb_ref, dep_ref,
             dpa_ref, dc_ref, dlg_ref, dlb_ref, dwm_ref, dbs_ref, dcb_ref, dclg_ref, dclb_ref,
             dv_buf, db_acc):
        i = pl.program_id(0)

        @pl.when(i == 0)
        def _():
            for ref in (dlg_ref, dlb_ref, dwm_ref, dbs_ref, dcb_ref, dclg_ref, dclb_ref, db_acc):
                ref[...] = jnp.zeros(ref.shape, F32)

        dmix = _dot_nt(dx_ref[...], wo_ref[...])
        ua = p_ref[:, 0:D_A]
        va = p_ref[:, D_A:2 * D_A]
        u, cdf_u = _gelu(ua)
        vg, cdf_v = _gelu(va)
        xhat, rstd = _ln(vg)
        v = (xhat * lg_ref[...] + lb_ref[...]).astype(BF16)
        mask = _tril_mask()
        for h in range(H_A):
            hc = slice(h * CHUNK, (h + 1) * CHUNK)
            wmh = jnp.where(mask, wm_ref[h], 0.0).astype(BF16)
            for cidx in range(tm // CHUNK):
                rc = slice(cidx * CHUNK, (cidx + 1) * CHUNK)
                vb = v[rc, hc]
                mixed = _dot(wmh, vb) + bs_ref[:, h:h + 1]
                da = dmix[rc, hc]
                dpa_ref[rc, hc] = (da * mixed * _gelu_grad(ua[rc, hc], cdf_u[rc, hc])).astype(BF16)
                dmixed = da * u[rc, hc]
                dmb = dmixed.astype(BF16)
                dv_buf[rc, hc] = _dot_tn(wmh, dmb)
                dwm_ref[h] += _dot_nt(dmb, vb)
                db_acc[:, hc] += dmixed
        dv = dv_buf[...]
        dlb_ref[...] += jnp.sum(dv, axis=0, keepdims=True)
        dlg_ref[...] += jnp.sum(dv * xhat, axis=0, keepdims=True)
        dvg = _ln_bwd(dv * lg_ref[...], xhat, rstd)
        dpa_ref[:, D_A:2 * D_A] = (dvg * _gelu_grad(va, cdf_v)).astype(BF16)

        for q in range(H_B):
            cols = slice(q * LANES, (q + 1) * LANES)
            chat, crstd = _ln(c_ref[:, cols])
            z = chat * clg_ref[:, cols] + clb_ref[:, cols]
            sg = _sigmoid(z)
            dz = dmix[:, D_A + q * LANES:D_A + (q + 1) * LANES] * (sg * (1.0 + z * (1.0 - sg)))
            dclb_ref[:, cols] += jnp.sum(dz, axis=0, keepdims=True)
            dclg_ref[:, cols] += jnp.sum(dz * chat, axis=0, keepdims=True)
            dc = _ln_bwd(dz * clg_ref[:, cols], chat, crstd)
            dc_ref[:, cols] = dc
            dcb_ref[:, cols] += jnp.sum(dc, axis=0, keepdims=True)

        @pl.when(i == n_tiles - 1)
        def _():
            for h in range(H_A):
                hc = slice(h * CHUNK, (h + 1) * CHUNK)
                dwm_ref[h] = jnp.where(mask, dwm_ref[h], 0.0)
                dbs_ref[h:h + 1, :] = jnp.sum(db_acc[:, hc].T, axis=0, keepdims=True)

    vec = _full((1, D_A))
    vshape = jax.ShapeDtypeStruct((1, D_A), F32)
    return pl.pallas_call(
        body, name="mixer_bwd_a", grid=(n_tiles,),
        in_specs=[_rows(tm, D), _full((D, D)), _rows(tm, IN_COLS), _rows(tm, D_B), vec, vec,
                  _full((H_A, CHUNK, CHUNK)), _full((CHUNK, H_A)), vec, vec, HBM],
        out_specs=[_rows(tm, 2 * D_A), _rows(tm, D_B), vec, vec, _full((H_A, CHUNK, CHUNK)),
                   _full((H_A, CHUNK)), vec, vec, vec],
        out_shape=[jax.ShapeDtypeStruct((T, 2 * D_A), BF16), jax.ShapeDtypeStruct((T, D_B), F32), vshape, vshape,
                   jax.ShapeDtypeStruct((H_A, CHUNK, CHUNK), F32), jax.ShapeDtypeStruct((H_A, CHUNK), F32),
                   vshape, vshape, vshape],
        scratch_shapes=[pltpu.VMEM((tm, D_A), F32), pltpu.VMEM((CHUNK, D_A), F32)],
        compiler_params=_cparams(32),
    )(*map(_hbm, (dxb, wout, proj, c, lg, lb, wm, bst, clg, clb, dep)))


def _mixer_bwd_b(dc, g, proj, cw, dpa, wg_in, x, g1, dres):
    tm = TM_MIX
    n_tiles = T // tm
    hb = tm // HALO
    bn = wg_in.shape[2]

    def body(dc_ref, dch_ref, g_ref, p_ref, cw_ref, dpa_ref, win_ref, x_ref, g1_ref, dres_ref,
             dp_ref, dcw_ref, dx_ref, dxb_ref, dg1_ref, dcbuf, dwacc):
        i = pl.program_id(0)

        @pl.when(i == 0)
        def _():
            dwacc[...] = jnp.zeros(dwacc.shape, F32)
            dg1_ref[...] = jnp.zeros(dg1_ref.shape, F32)

        _fill_shift_buffer(dcbuf, 0, dc_ref[...])
        _fill_shift_buffer(dcbuf, tm, jnp.where(i < n_tiles - 1, dch_ref[...], 0.0))
        _build_shifts(dcbuf)
        dp_ref[:, 0:2 * D_A] = dpa_ref[...]
        for q in range(H_B):
            cols = slice(q * LANES, (q + 1) * LANES)

            def row_block(rb, carry, q=q, cols=cols):
                base = pl.multiple_of(rb * RB, RB)
                rows = pl.ds(base, RB)
                gv = g_ref[rows, cols]
                acc = [jnp.zeros((RB, LANES), F32) for _ in range(CONV_CHAINS)]
                parts = []
                for k in range(CONV_W):
                    xk = _shifted(dcbuf, q, base, CONV_W - 1 - k)
                    acc[k % CONV_CHAINS] = acc[k % CONV_CHAINS] + xk * cw_ref[k:k + 1, cols]
                    parts.append(jnp.sum((gv * xk).reshape(RB // 8, 8, LANES), axis=0))
                dg = functools.reduce(lambda a, b: a + b, acc)
                val = p_ref[rows, cols]
                sg = _sigmoid(p_ref[rows, D_B + q * LANES:D_B + (q + 1) * LANES])
                dp_ref[rows, 2 * D_A + q * LANES:2 * D_A + (q + 1) * LANES] = (dg * sg).astype(BF16)
                dp_ref[rows, 2 * D_A + D_B + q * LANES:2 * D_A + D_B + (q + 1) * LANES] = (
                    dg * val * sg * (1.0 - sg)).astype(BF16)
                for k in range(CONV_W):
                    dwacc[k * 8:(k + 1) * 8, cols] += parts[k]
                return carry

            lax.fori_loop(0, tm // RB, row_block, 0)

        dh = jnp.zeros((tm, D), F32)
        for j in range(N_DEV):
            dh = dh + _dot_nt(dp_ref[:, j * bn:(j + 1) * bn], win_ref[j])
        xv = x_ref[...]
        dxn, dg = _rms_bwd(dh, xv, _rms(xv), g1_ref[...])
        dx = dres_ref[...] + dxn
        dx_ref[...] = dx
        dxb_ref[...] = dx.astype(BF16)
        dg1_ref[...] += dg

        @pl.when(i == n_tiles - 1)
        def _():
            for k in range(CONV_W):
                dcw_ref[k:k + 1, :] = jnp.sum(dwacc[k * 8:(k + 1) * 8, :], axis=0, keepdims=True)

    return pl.pallas_call(
        body, name="mixer_bwd_b", grid=(n_tiles,),
        in_specs=[_rows(tm, D_B),
                  pl.BlockSpec((HALO, D_B), lambda i: (jnp.minimum((i + 1) * hb, T // HALO - 1), 0)),
                  _rows(tm, D_B),
                  pl.BlockSpec((tm, 2 * D_B), lambda i: (i, 1)),
                  _full((CONV_W, D_B)), _rows(tm, 2 * D_A),
                  pl.BlockSpec(wg_in.shape, lambda i: (0, 0, 0), pipeline_mode=pl.Buffered(1)),
                  _rows(tm, D), _full((1, D)), _rows(tm, D)],
        out_specs=[_rows(tm, IN_COLS), _full((CONV_W, D_B)), _rows(tm, D), _rows(tm, D), _full((1, D))],
        out_shape=[jax.ShapeDtypeStruct((T, IN_COLS), BF16), jax.ShapeDtypeStruct((CONV_W, D_B), F32),
                   jax.ShapeDtypeStruct((T, D), F32), jax.ShapeDtypeStruct((T, D), BF16),
                   jax.ShapeDtypeStruct((1, D), F32)],
        scratch_shapes=[pltpu.VMEM((8, H_B, tm + HALO, LANES), F32), pltpu.VMEM((CONV_W * 8, D_B), F32)],
        compiler_params=_cparams(40),
    )(*map(_hbm, (dc, dc, g, proj, cw, dpa, wg_in, x, g1, dres)))


def _wgrad(name, a, g, bn, square_a=False):
    k = a.shape[1]
    n = g.shape[1]
    tk = min(k, 1024)
    tn = min(n, max(bn, 1024))
    nsub = tn // bn
    tt = TT_WGRAD
    nt = T // tt

    def body(a_ref, g_ref, o_ref, ob_ref):
        t = pl.program_id(2)

        @pl.when(t == 0)
        def _():
            o_ref[...] = jnp.zeros(o_ref.shape, F32)

        av = a_ref[...]
        if square_a:
            av = av * av
        for s in range(nsub):
            o_ref[s] += _dot_tn(av, g_ref[:, s * bn:(s + 1) * bn])

        @pl.when(t == nt - 1)
        def _():
            ob_ref[...] = o_ref[...].astype(BF16)

    ospec = pl.BlockSpec((nsub, tk, bn), lambda ki, ni, ti: (ni, ki, 0))
    return pl.pallas_call(
        body, name=name, grid=(k // tk, n // tn, nt),
        in_specs=[pl.BlockSpec((tt, tk), lambda ki, ni, ti: (ti, ki)),
                  pl.BlockSpec((tt, tn), lambda ki, ni, ti: (ti, ni))],
        out_specs=[ospec, ospec],
        out_shape=[jax.ShapeDtypeStruct((n // bn, k, bn), F32), jax.ShapeDtypeStruct((n // bn, k, bn), BF16)],
        compiler_params=_cparams(40, ("arbitrary", "arbitrary", "arbitrary")),
    )(a, g)


def _coords():
    return lax.axis_index("x"), lax.axis_index("y"), lax.axis_index("c")


def _flat(x, y, c):
    return 4 * x + 2 * y + c


def _peer(k):
    x, y, c = _coords()
    return (x ^ ((k >> 2) & 1), y ^ ((k >> 1) & 1), c ^ (k & 1))


def _hbm(a):
    return pltpu.with_memory_space_constraint(a, pltpu.HBM)


def _hbm_like(a):
    return pltpu.HBM(a.shape, a.dtype)


def _peer_sems():
    return pltpu.SemaphoreType.DMA((N_DEV - 1,))


def _place_own(shards):
    n = len(shards)
    shapes = [(s.shape if l is None else s.shape[1:]) for s, l, _ in shards]

    def body(*refs):
        ins, outs = refs[:n], refs[n:2 * n]
        stage_in, stage_out = refs[2 * n:3 * n], refs[3 * n:4 * n]
        in_sems, out_sems = refs[4 * n], refs[4 * n + 1]
        me = _flat(*_coords())
        srcs = [ins[a] if shards[a][1] is None else ins[a].at[shards[a][1]] for a in range(n)]
        loads = [pltpu.make_async_copy(srcs[a], stage_in[a], in_sems.at[a]) for a in range(n)]
        stores = [pltpu.make_async_copy(stage_out[a], outs[a].at[me], out_sems.at[a]) for a in range(n)]
        for cp in loads:
            cp.start()
        for a in range(n):
            loads[a].wait()
            stage_out[a][...] = stage_in[a][...].astype(stage_out[a].dtype)
            stores[a].start()
        for cp in stores:
            cp.wait()

    return pl.pallas_call(
        body, name="place_own", in_specs=[HBM] * n, out_specs=[HBM] * n,
        out_shape=[pltpu.HBM((N_DEV,) + shapes[a], shards[a][2]) for a in range(n)],
        scratch_shapes=[*[pltpu.VMEM(shapes[a], shards[a][0].dtype) for a in range(n)],
                        *[pltpu.VMEM(shapes[a], shards[a][2]) for a in range(n)],
                        pltpu.SemaphoreType.DMA((n,)), pltpu.SemaphoreType.DMA((n,))],
        compiler_params=pltpu.CompilerParams(vmem_limit_bytes=40 * MIB),
    )(*[_hbm(s) for s, _, _ in shards])


SIBLING = 1
CHIP_PEERS = (2, 4, 6)
FIRST_PEERS = (SIBLING,) + CHIP_PEERS


def _gather_start(lands):
    n = len(lands)

    def body(*refs):
        lnd, send, recv = refs[:n], refs[n:2 * n], refs[2 * n:3 * n]
        me = _flat(*_coords())
        for a in range(n):
            for j, k in enumerate(FIRST_PEERS):
                pltpu.make_async_remote_copy(
                    src_ref=lnd[a].at[me], dst_ref=lnd[a].at[me], send_sem=send[a].at[j],
                    recv_sem=recv[a].at[j], device_id=_peer(k), device_id_type=MESH_ID).start()

    sems = pltpu.SemaphoreType.DMA((len(FIRST_PEERS),))
    outs = pl.pallas_call(
        body, name="gather_start",
        out_shape=(*[sems] * (2 * n), *[_hbm_like(l) for l in lands]),
        in_specs=[HBM] * n, out_specs=(*[SEM] * (2 * n), *[HBM] * n),
        input_output_aliases={i: 2 * n + i for i in range(n)},
        compiler_params=pltpu.CompilerParams(has_side_effects=EFFECT),
    )(*[_hbm(l) for l in lands])
    return outs[:n], outs[n:2 * n], outs[2 * n:]


def _gather_pass_on(name, lands, sends, recvs, after):
    n = len(lands)

    def body(*refs):
        lnd, send, recv = refs[:n], refs[n:2 * n], refs[2 * n:3 * n]
        send2, recv2 = refs[3 * n + 1:4 * n + 1], refs[4 * n + 1:5 * n + 1]
        sibling = _peer(SIBLING)
        for a in range(n):
            for j, k in enumerate(CHIP_PEERS):
                cp = pltpu.make_async_remote_copy(
                    src_ref=lnd[a].at[0], dst_ref=lnd[a].at[0], send_sem=send[a].at[1 + j],
                    recv_sem=recv[a].at[1 + j], device_id=_peer(k), device_id_type=MESH_ID)
                cp.wait_send()
                cp.wait_recv()
                blk = _flat(*_peer(k))
                pltpu.make_async_remote_copy(
                    src_ref=lnd[a].at[blk], dst_ref=lnd[a].at[blk], send_sem=send2[a].at[j],
                    recv_sem=recv2[a].at[j], device_id=sibling, device_id_type=MESH_ID).start()

    sems = pltpu.SemaphoreType.DMA((len(CHIP_PEERS),))
    outs = pl.pallas_call(
        body, name=name, out_shape=(*[sems] * (2 * n), *[_hbm_like(l) for l in lands]),
        in_specs=(*[HBM] * n, *[SEM] * (2 * n), HBM), out_specs=(*[SEM] * (2 * n), *[HBM] * n),
        input_output_aliases={i: 2 * n + i for i in range(n)},
        compiler_params=pltpu.CompilerParams(has_side_effects=EFFECT),
    )(*lands, *sends, *recvs, after)
    return outs[:n], outs[n:2 * n], outs[2 * n:]


def _gather_wait(name, lands, sends, recvs, sends2, recvs2, after):
    n = len(lands)

    def body(*refs):
        lnd, send, recv = refs[:n], refs[n:2 * n], refs[2 * n:3 * n]
        send2, recv2 = refs[3 * n:4 * n], refs[4 * n:5 * n]
        sibling = _peer(SIBLING)
        for a in range(n):
            own = pltpu.make_async_remote_copy(
                src_ref=lnd[a].at[0], dst_ref=lnd[a].at[0], send_sem=send[a].at[0], recv_sem=recv[a].at[0],
                device_id=sibling, device_id_type=MESH_ID)
            own.wait_send()
            own.wait_recv()
            for j in range(len(CHIP_PEERS)):
                cp = pltpu.make_async_remote_copy(
                    src_ref=lnd[a].at[0], dst_ref=lnd[a].at[0], send_sem=send2[a].at[j], recv_sem=recv2[a].at[j],
                    device_id=sibling, device_id_type=MESH_ID)
                cp.wait_send()
                cp.wait_recv()

    return pl.pallas_call(
        body, name=name, out_shape=[_hbm_like(l) for l in lands],
        in_specs=(*[HBM] * n, *[SEM] * (4 * n), HBM), out_specs=[HBM] * n,
        input_output_aliases={i: i for i in range(n)},
        compiler_params=pltpu.CompilerParams(has_side_effects=EFFECT),
    )(*lands, *sends, *recvs, *sends2, *recvs2, after)


def _scatter_start(name, grads, lands, layer):
    n = len(grads)

    def body(*refs):
        g, lnd, send, recv = refs[:n], refs[n:2 * n], refs[2 * n:3 * n], refs[3 * n:4 * n]
        for a in range(n):
            for k in range(1, N_DEV):
                to = _peer(k)
                pltpu.make_async_remote_copy(
                    src_ref=g[a].at[_flat(*to)], dst_ref=lnd[a].at[layer, k - 1], send_sem=send[a].at[k - 1],
                    recv_sem=recv[a].at[k - 1], device_id=to, device_id_type=MESH_ID).start()

    outs = pl.pallas_call(
        body, name=name,
        out_shape=(*[_peer_sems()] * (2 * n), *[_hbm_like(g) for g in grads], *[_hbm_like(l) for l in lands]),
        in_specs=[HBM] * (2 * n), out_specs=(*[SEM] * (2 * n), *[HBM] * (2 * n)),
        input_output_aliases={i: 2 * n + i for i in range(2 * n)},
        compiler_params=pltpu.CompilerParams(has_side_effects=EFFECT),
    )(*[_hbm(g) for g in grads], *[_hbm(l) for l in lands])
    return outs[:n], outs[n:2 * n], outs[2 * n:3 * n], outs[3 * n:]


def _scatter_wait(name, grads, lands, sends, recvs, after):
    n, nw = len(grads), len(lands)

    def body(*refs):
        g, lnd = refs[:n], refs[n:n + nw]
        send, recv = refs[n + nw:2 * n + nw], refs[2 * n + nw:3 * n + nw]
        for a in range(n):
            for k in range(1, N_DEV):
                cp = pltpu.make_async_remote_copy(
                    src_ref=g[a].at[0], dst_ref=lnd[a // DEPTH].at[a % DEPTH, 0],
                    send_sem=send[a].at[k - 1], recv_sem=recv[a].at[k - 1],
                    device_id=_peer(k), device_id_type=MESH_ID)
                cp.wait_send()
                cp.wait_recv()

    outs = pl.pallas_call(
        body, name=name,
        out_shape=(*[_hbm_like(g) for g in grads], *[_hbm_like(l) for l in lands]),
        in_specs=(*[HBM] * (n + nw), *[SEM] * (2 * n), HBM), out_specs=[HBM] * (n + nw),
        input_output_aliases={i: i for i in range(n + nw)},
        compiler_params=pltpu.CompilerParams(has_side_effects=EFFECT),
    )(*grads, *lands, *sends, *recvs, after)
    return outs[n:]


def _all_reduce_small(part, deps):
    rows = part.shape[0]
    br = rows // N_DEV
    assert br * N_DEV == rows and br % 8 == 0
    nd = len(deps)

    def body(p_ref, *refs):
        o_ref, slots, send1, recv1, send2, recv2 = refs[nd:]
        me = _flat(*_coords())

        def block(ref, d):
            return ref.at[pl.ds(pl.multiple_of(d * br, 8), br), :]

        slots[me] = p_ref[pl.ds(pl.multiple_of(me * br, 8), br), :]
        scatter, gather = [], []
        for k in range(1, N_DEV):
            to = _peer(k)
            scatter.append(pltpu.make_async_remote_copy(
                src_ref=block(p_ref, _flat(*to)), dst_ref=slots.at[me],
                send_sem=send1.at[k - 1], recv_sem=recv1.at[k - 1], device_id=to, device_id_type=MESH_ID))
            gather.append(pltpu.make_async_remote_copy(
                src_ref=block(o_ref, me), dst_ref=block(o_ref, me),
                send_sem=send2.at[k - 1], recv_sem=recv2.at[k - 1], device_id=to, device_id_type=MESH_ID))
        for cp in scatter:
            cp.start()
        for cp in scatter:
            cp.wait()
        acc = slots[0]
        for d in range(1, N_DEV):
            acc = acc + slots[d]
        o_ref[pl.ds(pl.multiple_of(me * br, 8), br), :] = acc
        for cp in gather:
            cp.start()
        for cp in gather:
            cp.wait()

    return pl.pallas_call(
        body, name="all_reduce_small",
        in_specs=[pl.BlockSpec(memory_space=pltpu.VMEM)] + [HBM] * nd, out_specs=pl.BlockSpec(memory_space=pltpu.VMEM),
        out_shape=jax.ShapeDtypeStruct(part.shape, F32),
        scratch_shapes=[pltpu.VMEM((N_DEV, br, LANES), F32)] + [pltpu.SemaphoreType.DMA((N_DEV - 1,))] * 4,
        compiler_params=pltpu.CompilerParams(vmem_limit_bytes=24 * MIB),
    )(part, *map(_hbm, deps))


def _adam_math(w, g, m, v):
    m = ADAM_B1 * m + (1.0 - ADAM_B1) * g
    v = ADAM_B2 * v + (1.0 - ADAM_B2) * (g * g)
    m_hat = m / (1.0 - ADAM_B1 ** ADAM_STEP)
    v_hat = v / (1.0 - ADAM_B2 ** ADAM_STEP)
    delta = -ADAM_LR * (m_hat / (jnp.sqrt(v_hat) + ADAM_EPS) + ADAM_WD * w)
    return delta, m, v


def _adam_sharded(name, me, g32, recv, w, m, v):
    _, r, c = w.shape
    tr = min(r, 256)
    nblk = r // tr
    assert len(g32) == DEPTH == 2

    def body(me_ref, g0_ref, g1_ref, recv_ref, w_ref, m_ref, v_ref, g_ref, d_ref, nm_ref, nv_ref):
        g = jnp.where(pl.program_id(0) == 0, g0_ref[...], g1_ref[...])
        for k in range(N_DEV - 1):
            g = g + recv_ref[k].astype(F32)
        delta, nm, nv = _adam_math(w_ref[...], g, m_ref[...], v_ref[...])
        g_ref[...] = g
        d_ref[...] = delta
        nm_ref[...] = nm
        nv_ref[...] = nv

    blk = pl.BlockSpec((None, tr, c), lambda l, i, me_ref: (l, i, 0))
    own0 = pl.BlockSpec((None, tr, c), lambda l, i, me_ref: (me_ref[0], jnp.where(l == 0, i, nblk - 1), 0))
    own1 = pl.BlockSpec((None, tr, c), lambda l, i, me_ref: (me_ref[0], jnp.where(l == 1, i, 0), 0))
    shp = jax.ShapeDtypeStruct(w.shape, F32)
    return pl.pallas_call(
        body, name=name,
        grid_spec=pltpu.PrefetchScalarGridSpec(
            num_scalar_prefetch=1, grid=(DEPTH, nblk),
            in_specs=[own0, own1, pl.BlockSpec((None, N_DEV - 1, tr, c), lambda l, i, me_ref: (l, 0, i, 0)),
                      blk, blk, blk],
            out_specs=[blk] * 4),
        out_shape=[shp] * 4,
        compiler_params=_cparams(32, ("arbitrary", "arbitrary")),
    )(me, *[_hbm(a) for a in (*g32, recv, w, m, v)])


def _adam_small(gs, ws, ms, vs):
    n = len(gs)

    def body(*refs):
        g, w, m, v = refs[:n], refs[n:2 * n], refs[2 * n:3 * n], refs[3 * n:4 * n]
        d, nm, nv = refs[4 * n:5 * n], refs[5 * n:6 * n], refs[6 * n:7 * n]
        for a in range(n):
            delta, new_m, new_v = _adam_math(w[a][...], g[a][...], m[a][...], v[a][...])
            d[a][...] = delta
            nm[a][...] = new_m
            nv[a][...] = new_v

    specs = [_full(a.shape) for a in gs]
    shapes = [jax.ShapeDtypeStruct(a.shape, F32) for a in gs]
    outs = pl.pallas_call(
        body, name="adam_small", grid=(1,),
        in_specs=specs * 4, out_specs=specs * 3, out_shape=shapes * 3,
        compiler_params=_cparams(24),
    )(*gs, *ws, *ms, *vs)
    return outs[:n], outs[n:2 * n], outs[2 * n:]


SMALL = ["norm1_g", "sgu_ln_g", "sgu_ln_b", "sgu_w", "sgu_b", "conv_b", "conv_ln_g", "conv_ln_b", "norm2_g",
         "final_g"]


def _pack(arrays):
    flat = jnp.concatenate([a.reshape(-1) for a in arrays])
    pad = (-flat.shape[0]) % (N_DEV * 8 * LANES)
    return jnp.pad(flat, (0, pad)).reshape(-1, LANES)


def _unpack(packed, shapes):
    flat = packed.reshape(-1)
    out, off = [], 0
    for s in shapes:
        size = 1
        for d in s:
            size *= d
        out.append(flat[off:off + size].reshape(s))
        off += size
    return out


def kernel(x, norm1_g, w_in, sgu_ln_g, sgu_ln_b, sgu_w, sgu_b, conv_w, conv_b, conv_ln_g, conv_ln_b, w_out, norm2_g, w_ff1, w_ff2, final_g, loss_target, m_norm1_g, m_w_in, m_sgu_ln_g, m_sgu_ln_b, m_sgu_w, m_sgu_b, m_conv_w, m_conv_b, m_conv_ln_g, m_conv_ln_b, m_w_out, m_norm2_g, m_w_ff1, m_w_ff2, m_final_g, v_norm1_g, v_w_in, v_sgu_ln_g, v_sgu_ln_b, v_sgu_w, v_sgu_b, v_conv_w, v_conv_b, v_conv_ln_g, v_conv_ln_b, v_w_out, v_norm2_g, v_w_ff1, v_w_ff2, v_final_g):
    x2d = x.reshape(T, D)
    tgt = loss_target.reshape(T, D)
    cw_shard = conv_w.reshape(CONV_W, LANES)

    gnames, shards = [], []
    for l in range(DEPTH):
        for k, w in (("w_in", w_in), ("w_out", w_out), ("w_ff1", w_ff1), ("w_ff2", w_ff2)):
            gnames.append(f"{k}{l}")
            shards.append((w, l, BF16))
        if l == 0:
            gnames.insert(1, "conv_w")
            shards.insert(1, (cw_shard, None, F32))
    sends, recvs, lands = _gather_start(_place_own(shards))
    gidx = {k: i for i, k in enumerate(gnames)}

    passed = {}

    def pass_on(ks, after):
        idx = [gidx[k] for k in ks]
        s2, r2, ld = _gather_pass_on("gather_pass_on_" + "_".join(ks), [lands[i] for i in idx],
                                     [sends[i] for i in idx], [recvs[i] for i in idx], after)
        for j, k in enumerate(ks):
            passed[k] = (s2[j], r2[j], ld[j])

    def gathered(ks, after):
        idx = [gidx[k] for k in ks]
        return _gather_wait("gather_wait_" + "_".join(ks), [passed[k][2] for k in ks], [sends[i] for i in idx],
                            [recvs[i] for i in idx], [passed[k][0] for k in ks], [passed[k][1] for k in ks], after)

    saved = []
    xl = x2d
    cw_full = None
    for l in range(DEPTH):
        g1 = norm1_g[l].reshape(1, D)
        g2 = norm2_g[l].reshape(1, D)
        lg, lb = sgu_ln_g[l].reshape(1, D_A), sgu_ln_b[l].reshape(1, D_A)
        bst = sgu_b[l].T
        cb = conv_b[l].reshape(1, D_B)
        clg, clb = conv_ln_g[l].reshape(1, D_B), conv_ln_b[l].reshape(1, D_B)
        pass_on(["w_in0"] if l == 0 else [f"w_in{l}", f"w_out{l}"], xl)
        (wg_in,) = gathered([f"w_in{l}"], xl)
        if l == 0:
            pass_on(["conv_w", "w_out0"], xl)
        h, proj = _fwd_in(xl, g1, wg_in)
        if l == 0:
            cw_g, wout = gathered(["conv_w", "w_out0"], proj)
            cw_full = cw_g.reshape(N_DEV, DEPTH, CONV_W, D_B // N_DEV).transpose(1, 2, 0, 3).reshape(
                DEPTH, CONV_W, D_B)
        else:
            (wout,) = gathered([f"w_out{l}"], proj)
            pass_on([f"w_ff1{l}"], proj)
        wout = wout.reshape(D, D)
        mix, c, g, x1, h2 = _mixer_fwd(proj, lg, lb, sgu_w[l], bst, cw_full[l], cb, clg, clb, wout, xl, g2)
        pass_on([f"w_ff1{l}"] if l == 0 else [f"w_ff2{l}"], h2)
        (wg_ff1,) = gathered([f"w_ff1{l}"], h2)
        r = _fwd_ff1(h2, wg_ff1)
        if l == 0:
            pass_on(["w_ff20"], r)
        (w2,) = gathered([f"w_ff2{l}"], r)
        w2 = w2.reshape(D_FF, D)
        saved.append(dict(x=xl, h=h, proj=proj, mix=mix, c=c, g=g, x1=x1, h2=h2, r=r, wg_in=wg_in, wout=wout,
                          wg_ff1=wg_ff1, w2=w2, g1=g1, g2=g2, lg=lg, lb=lb, bst=bst, clg=clg, clb=clb))
        if l + 1 < DEPTH:
            xl = _fwd_ff2(r, w2, x1)
        else:
            dx, dxb, loss_part, d_final_g = _fwd_ff2_loss(r, w2, x1, final_g.reshape(1, D), tgt)

    names = ["w_in", "w_out", "w_ff1", "w_ff2"]
    block = dict(w_in=(D, IN_COLS // N_DEV), w_out=(D // N_DEV, D), w_ff1=(D, D_FF // N_DEV), w_ff2=(D_FF // N_DEV, D))
    land = {k: lax.empty((DEPTH, N_DEV - 1) + block[k], BF16) for k in names}
    big32 = {k: [None] * DEPTH for k in names}
    big16 = {k: [None] * DEPTH for k in names}
    ssend = {k: [None] * DEPTH for k in names}
    srecv = {k: [None] * DEPTH for k in names}

    def send_grads(l, grads):
        ks = list(grads)
        for k in ks:
            big32[k][l] = grads[k][0].reshape((N_DEV,) + block[k])
        sends, recvs, g16s, lands = _scatter_start(
            "scatter_start_" + "_".join(ks) + str(l), [grads[k][1].reshape((N_DEV,) + block[k]) for k in ks],
            [land[k] for k in ks], l)
        for i, k in enumerate(ks):
            ssend[k][l], srecv[k][l], big16[k][l], land[k] = sends[i], recvs[i], g16s[i], lands[i]

    small = {}
    d_conv_w = [None] * DEPTH
    for l in reversed(range(DEPTH)):
        s = saved[l]
        send_grads(l, dict(w_ff2=_wgrad("wgrad_ff2", s["r"], dxb, D, square_a=True)))
        df1, dx1, dx1b, dg2 = _bwd_mlp(dxb, dx, s["w2"], s["r"], s["wg_ff1"], s["x1"], s["g2"], big16["w_ff2"][l])
        send_grads(l, dict(w_ff1=_wgrad("wgrad_ff1", s["h2"], df1, D_FF // N_DEV),
                           w_out=_wgrad("wgrad_out", s["mix"], dx1b, D)))
        dpa, dc, dlg, dlb, dwm, dbs, dcb, dclg, dclb = _mixer_bwd_a(
            dx1b, s["wout"], s["proj"], s["c"], s["lg"], s["lb"], sgu_w[l], s["bst"], s["clg"], s["clb"],
            big16["w_out"][l])
        dproj, d_conv_w[l], dx, dxb, dg1 = _mixer_bwd_b(dc, s["g"], s["proj"], cw_full[l], dpa, s["wg_in"], s["x"],
                                                         s["g1"], dx1)
        send_grads(l, dict(w_in=_wgrad("wgrad_in", s["h"], dproj, IN_COLS // N_DEV)))
        small[l] = dict(norm1_g=dg1, sgu_ln_g=dlg, sgu_ln_b=dlb, sgu_w=dwm, sgu_b=dbs, conv_b=dcb, conv_ln_g=dclg,
                        conv_ln_b=dclb, norm2_g=dg2)
    grad_x = dx.reshape(1, T, D)

    me_block = _flat(*_coords()).astype(jnp.int32).reshape(1)
    wmv = dict(w_in=(w_in, m_w_in, v_w_in), w_out=(w_out, m_w_out, v_w_out), w_ff1=(w_ff1, m_w_ff1, v_w_ff1),
               w_ff2=(w_ff2, m_w_ff2, v_w_ff2))
    res = {}

    def update(tag, ks, after):
        recv = _scatter_wait("scatter_wait_" + tag, [g for k in ks for g in big16[k]], [land[k] for k in ks],
                             [q for k in ks for q in ssend[k]], [q for k in ks for q in srecv[k]], after)
        for k, rk in zip(ks, recv):
            res[k] = _adam_sharded("adam_" + k, me_block, big32[k], rk, *wmv[k])

    update("early", ["w_ff2", "w_ff1", "w_out"], big16["w_in"][0])

    rep = dict(norm1_g=(norm1_g, m_norm1_g, v_norm1_g), sgu_ln_g=(sgu_ln_g, m_sgu_ln_g, v_sgu_ln_g),
               sgu_ln_b=(sgu_ln_b, m_sgu_ln_b, v_sgu_ln_b), sgu_w=(sgu_w, m_sgu_w, v_sgu_w),
               sgu_b=(sgu_b, m_sgu_b, v_sgu_b), conv_b=(conv_b, m_conv_b, v_conv_b),
               conv_ln_g=(conv_ln_g, m_conv_ln_g, v_conv_ln_g), conv_ln_b=(conv_ln_b, m_conv_ln_b, v_conv_ln_b),
               norm2_g=(norm2_g, m_norm2_g, v_norm2_g), final_g=(final_g, m_final_g, v_final_g))
    parts = []
    for k in SMALL:
        if k == "final_g":
            parts.append(d_final_g.reshape(rep[k][0].shape))
        else:
            parts.append(jnp.stack([small[l][k].reshape(rep[k][0].shape[1:]) for l in range(DEPTH)]))
    parts.append(jnp.stack(d_conv_w))
    parts.append(loss_part[0, 0:1])
    shapes = [p.shape for p in parts]
    packed_sum = _all_reduce_small(_pack(parts), [res[k][0] for k in ("w_ff2", "w_ff1", "w_out")])
    update("late", ["w_in"], packed_sum)
    summed = _unpack(packed_sum, shapes)
    loss = summed[-1][0]
    me = _flat(*_coords())
    g_conv_w = lax.dynamic_slice_in_dim(summed[-2], me * (D_B // N_DEV), D_B // N_DEV, axis=2)
    g_small = summed[:-2] + [g_conv_w]
    keys = SMALL + ["conv_w"]
    rep["conv_w"] = (conv_w, m_conv_w, v_conv_w)
    def two_d(a):
        return a.reshape(1, -1) if a.ndim == 1 else a

    d_s, nm_s, nv_s = _adam_small([two_d(g) for g in g_small], *[[two_d(rep[k][i]) for k in keys] for i in range(3)])
    for i, k in enumerate(keys):
        shape = rep[k][0].shape
        res[k] = (g_small[i], d_s[i].reshape(shape), nm_s[i].reshape(shape), nv_s[i].reshape(shape))

    order = ["norm1_g", "w_in", "sgu_ln_g", "sgu_ln_b", "sgu_w", "sgu_b", "conv_w", "conv_b", "conv_ln_g",
             "conv_ln_b", "w_out", "norm2_g", "w_ff1", "w_ff2", "final_g"]
    return (loss, grad_x, *[res[k][0] for k in order], *[res[k][1] for k in order],
            *[res[k][2] for k in order], *[res[k][3] for k in order])
```

```python
import functools

import jax
import jax.numpy as jnp
from jax import lax
from jax.experimental import pallas as pl
from jax.experimental.pallas import tpu as pltpu

F32 = jnp.float32
BF16 = jnp.bfloat16

N_DEV = 8
DEPTH = 2
T = 4096
D = 1024
D_A = 512
D_B = 512
CHUNK = 128
H_A = 4
H_B = 4
CONV_W = 31
HALO = 32
D_FF = 4096
IN_COLS = 2048
EPS = 1e-6

ADAM_LR = 0.001
ADAM_B1 = 0.9
ADAM_B2 = 0.999
ADAM_EPS = 1e-08
ADAM_WD = 0.01
ADAM_STEP = 10

TM = 512
TM_FWD = 1024
TT_WGRAD = 4096
TM_MIX = 256
TM_SGU = 512
RB = 64
CONV_CHAINS = 2
LANES = 128
MIB = 1024 * 1024
SCOPED_VMEM_MIB = 60

SQRT_HALF = 0.7071067811865476
INV_SQRT_2PI = 0.3989422804014327

MESH_ID = pl.DeviceIdType.MESH
HBM = pl.BlockSpec(memory_space=pltpu.HBM)
SEM = pl.BlockSpec(memory_space=pltpu.SEMAPHORE)
EFFECT = pltpu.SideEffectType.DATAFLOW_SIDE_EFFECTING


def _cparams(vmem_mib, sem=("arbitrary",)):
    assert vmem_mib <= SCOPED_VMEM_MIB
    return pltpu.CompilerParams(dimension_semantics=sem, vmem_limit_bytes=SCOPED_VMEM_MIB * MIB)


def _full(shape):
    return pl.BlockSpec(shape, lambda *_: (0,) * len(shape))


def _rows(tm, cols):
    return pl.BlockSpec((tm, cols), lambda i: (i, 0))


def _gelu(x):
    cdf = 0.5 * (1.0 + lax.erf(x * SQRT_HALF))
    return x * cdf, cdf


def _gelu_grad(x, cdf):
    return cdf + x * (INV_SQRT_2PI * jnp.exp(-0.5 * x * x))


def _sigmoid(x):
    return 1.0 / (1.0 + jnp.exp(-x))


def _ln(x):
    mu = jnp.mean(x, axis=-1, keepdims=True)
    xc = x - mu
    rstd = lax.rsqrt(jnp.mean(xc * xc, axis=-1, keepdims=True) + EPS)
    return xc * rstd, rstd


def _ln_bwd(dyh, xhat, rstd):
    return rstd * (dyh - jnp.mean(dyh, axis=-1, keepdims=True) - xhat * jnp.mean(dyh * xhat, axis=-1, keepdims=True))


def _rms(x):
    return lax.rsqrt(jnp.mean(x * x, axis=-1, keepdims=True) + EPS)


def _rms_bwd(dh, x, r, g):
    n = x * r
    dn = dh * g
    dx = r * (dn - n * jnp.mean(dn * n, axis=-1, keepdims=True))
    return dx, jnp.sum(dh * n, axis=0, keepdims=True)


def _dot(a, b):
    return jnp.dot(a, b, preferred_element_type=F32)


def _dot_nt(a, b):
    return lax.dot_general(a, b, (((1,), (1,)), ((), ())), preferred_element_type=F32)


def _dot_tn(a, b):
    return lax.dot_general(a, b, (((0,), (0,)), ((), ())), preferred_element_type=F32)


def _tril_mask():
    r = lax.broadcasted_iota(jnp.int32, (CHUNK, CHUNK), 0)
    c = lax.broadcasted_iota(jnp.int32, (CHUNK, CHUNK), 1)
    return r >= c


def _fwd_in(x, g1, wg):
    bn = wg.shape[2]

    def body(x_ref, g_ref, w_ref, h_ref, p_ref):
        xv = x_ref[...]
        h = (xv * _rms(xv) * g_ref[...]).astype(BF16)
        h_ref[...] = h
        for j in range(N_DEV):
            p_ref[:, j * bn:(j + 1) * bn] = _dot(h, w_ref[j])

    return pl.pallas_call(
        body, name="fwd_in", grid=(T // TM_FWD,),
        in_specs=[_rows(TM_FWD, D), _full((1, D)), _full(wg.shape)],
        out_specs=[_rows(TM_FWD, D), _rows(TM_FWD, IN_COLS)],
        out_shape=[jax.ShapeDtypeStruct((T, D), BF16), jax.ShapeDtypeStruct((T, IN_COLS), F32)],
        compiler_params=_cparams(32),
    )(*map(_hbm, (x, g1, wg)))


def _fill_shift_buffer(sh_ref, row0, value):
    for q in range(sh_ref.shape[1]):
        sh_ref[0, q, row0:row0 + value.shape[0], :] = value[:, q * LANES:(q + 1) * LANES]


def _build_shifts(sh_ref):
    rows = sh_ref.shape[2]
    for p in range(1, 8):
        for q in range(sh_ref.shape[1]):
            sh_ref[p, q, 0:rows - 8, :] = sh_ref[0, q, p:p + rows - 8, :]


def _shifted(sh_ref, q, base, off):
    start = base + (off - off % 8)
    if not isinstance(start, int):
        start = pl.multiple_of(start, 8)
    return sh_ref[off % 8, q, pl.ds(start, RB), :]


def _conv_taps(sh_ref, w_ref, q, base, first_tap_row, step):
    cols = slice(q * LANES, (q + 1) * LANES)
    acc = [jnp.zeros((RB, LANES), F32) for _ in range(CONV_CHAINS)]
    for k in range(CONV_W):
        term = _shifted(sh_ref, q, base, first_tap_row + step * k) * w_ref[k:k + 1, cols]
        acc[k % CONV_CHAINS] = acc[k % CONV_CHAINS] + term
    return functools.reduce(lambda a, b: a + b, acc)


def _mixer_fwd(proj, lg, lb, wm, bst, cw, cb, clg, clb, wout, x, g2):
    tm = TM_MIX
    hb = tm // HALO

    def body(p_ref, ph_ref, lg_ref, lb_ref, wm_ref, bs_ref, cw_ref, cb_ref, clg_ref, clb_ref, wo_ref, x_ref, g2_ref,
             mix_ref, c_ref, g_ref, x1_ref, h2_ref, gbuf):
        i = pl.program_id(0)
        u, _ = _gelu(p_ref[:, 0:D_A])
        vg, _ = _gelu(p_ref[:, D_A:2 * D_A])
        xhat, _ = _ln(vg)
        v = (xhat * lg_ref[...] + lb_ref[...]).astype(BF16)
        mask = _tril_mask()
        for h in range(H_A):
            hc = slice(h * CHUNK, (h + 1) * CHUNK)
            wmh = jnp.where(mask, wm_ref[h], 0.0).astype(BF16)
            for c in range(tm // CHUNK):
                rc = slice(c * CHUNK, (c + 1) * CHUNK)
                mixed = _dot(wmh, v[rc, hc]) + bs_ref[:, h:h + 1]
                mix_ref[rc, hc] = (u[rc, hc] * mixed).astype(BF16)
        x1_ref[...] = x_ref[...] + _dot(mix_ref[:, 0:D_A], wo_ref[0:D_A, :])

        g = p_ref[:, 2 * D_A:2 * D_A + D_B] * _sigmoid(p_ref[:, 2 * D_A + D_B:IN_COLS])
        g_ref[...] = g
        gh = ph_ref[:, 0:D_B] * _sigmoid(ph_ref[:, D_B:2 * D_B])
        _fill_shift_buffer(gbuf, 0, jnp.where(i > 0, gh, 0.0))
        _fill_shift_buffer(gbuf, HALO, g)
        _build_shifts(gbuf)
        for q in range(H_B):
            cols = slice(q * LANES, (q + 1) * LANES)
            for rb in range(tm // RB):
                acc = _conv_taps(gbuf, cw_ref, q, rb * RB, HALO - (CONV_W - 1), 1)
                c_ref[rb * RB:(rb + 1) * RB, cols] = acc + cb_ref[:, cols]
        for q in range(H_B):
            cols = slice(q * LANES, (q + 1) * LANES)
            chat, _ = _ln(c_ref[:, cols])
            z = chat * clg_ref[:, cols] + clb_ref[:, cols]
            mix_ref[:, D_A + q * LANES:D_A + (q + 1) * LANES] = (z * _sigmoid(z)).astype(BF16)

        x1 = x1_ref[...] + _dot(mix_ref[:, D_A:D], wo_ref[D_A:D, :])
        x1_ref[...] = x1
        h2_ref[...] = (x1 * _rms(x1) * g2_ref[...]).astype(BF16)

    vec = _full((1, D_A))
    return pl.pallas_call(
        body, name="mixer_fwd", grid=(T // tm,),
        in_specs=[_rows(tm, IN_COLS),
                  pl.BlockSpec((HALO, 2 * D_B), lambda i: (jnp.maximum(i * hb - 1, 0), 1)),
                  vec, vec, _full((H_A, CHUNK, CHUNK)), _full((CHUNK, H_A)),
                  _full((CONV_W, D_B)), vec, vec, vec,
                  pl.BlockSpec((D, D), lambda i: (0, 0), pipeline_mode=pl.Buffered(1)), _rows(tm, D), _full((1, D))],
        out_specs=[_rows(tm, D), _rows(tm, D_B), _rows(tm, D_B), _rows(tm, D), _rows(tm, D)],
        out_shape=[jax.ShapeDtypeStruct((T, D), BF16), jax.ShapeDtypeStruct((T, D_B), F32),
                   jax.ShapeDtypeStruct((T, D_B), F32), jax.ShapeDtypeStruct((T, D), F32),
                   jax.ShapeDtypeStruct((T, D), BF16)],
        scratch_shapes=[pltpu.VMEM((8, H_B, HALO + tm, LANES), F32)],
        compiler_params=_cparams(40),
    )(*map(_hbm, (proj, proj, lg, lb, wm, bst, cw, cb, clg, clb, wout, x, g2)))


def _fwd_ff1(h2, wg):
    bn = wg.shape[2]

    def body(h_ref, w_ref, r_ref):
        h = h_ref[...]
        for j in range(N_DEV):
            r_ref[:, j * bn:(j + 1) * bn] = jnp.maximum(_dot(h, w_ref[j]), 0.0).astype(BF16)

    return pl.pallas_call(
        body, name="fwd_ff1", grid=(T // TM_FWD,),
        in_specs=[_rows(TM_FWD, D), _full(wg.shape)],
        out_specs=_rows(TM_FWD, D_FF),
        out_shape=jax.ShapeDtypeStruct((T, D_FF), BF16),
        compiler_params=_cparams(48),
    )(h2, wg)


def _fwd_ff2(r, w2, x1):
    def body(r_ref, w_ref, x_ref, o_ref):
        rv = r_ref[...]
        o_ref[...] = x_ref[...] + _dot(rv * rv, w_ref[...])

    return pl.pallas_call(
        body, name="fwd_ff2", grid=(T // TM_FWD,),
        in_specs=[_rows(TM_FWD, D_FF), _full((D_FF, D)), _rows(TM_FWD, D)],
        out_specs=_rows(TM_FWD, D),
        out_shape=jax.ShapeDtypeStruct((T, D), F32),
        compiler_params=_cparams(48),
    )(r, w2, x1)


def _fwd_ff2_loss(r, w2, x1, gf, tgt):
    def body(r_ref, w_ref, x_ref, g_ref, t_ref, dx_ref, dxb_ref, loss_ref, dg_ref):
        i = pl.program_id(0)

        @pl.when(i == 0)
        def _():
            loss_ref[...] = jnp.zeros(loss_ref.shape, F32)
            dg_ref[...] = jnp.zeros(dg_ref.shape, F32)

        rv = r_ref[...]
        xv = x_ref[...] + _dot(rv * rv, w_ref[...])
        rn = _rms(xv)
        diff = xv * rn * g_ref[...] - t_ref[...]
        loss_ref[...] += 0.5 * jnp.sum(jnp.mean(diff * diff, axis=-1, keepdims=True), axis=0, keepdims=True)
        dx, dg = _rms_bwd(diff * (1.0 / D), xv, rn, g_ref[...])
        dx_ref[...] = dx
        dxb_ref[...] = dx.astype(BF16)
        dg_ref[...] += dg

    return pl.pallas_call(
        body, name="fwd_ff2_loss", grid=(T // TM,),
        in_specs=[_rows(TM, D_FF), pl.BlockSpec((D_FF, D), lambda i: (0, 0), pipeline_mode=pl.Buffered(1)),
                  _rows(TM, D), _full((1, D)), _rows(TM, D)],
        out_specs=[_rows(TM, D), _rows(TM, D), _full((8, LANES)), _full((1, D))],
        out_shape=[jax.ShapeDtypeStruct((T, D), F32), jax.ShapeDtypeStruct((T, D), BF16),
                   jax.ShapeDtypeStruct((8, LANES), F32), jax.ShapeDtypeStruct((1, D), F32)],
        compiler_params=_cparams(40),
    )(*map(_hbm, (r, w2, x1, gf, tgt)))


def _bwd_mlp(dxb, dres, w2, r, wg1, x1, g2, dep):
    bn = wg1.shape[2]

    def body(d_ref, dres_ref, w2_ref, r_ref, w1_ref, x_ref, g_ref, dep_ref, df1_ref, dx_ref, dxb_ref, dg_ref):
        i = pl.program_id(0)

        @pl.when(i == 0)
        def _():
            dg_ref[...] = jnp.zeros(dg_ref.shape, F32)

        d = d_ref[...]
        dh = jnp.zeros((TM, D), F32)
        for j in range(N_DEV):
            cols = slice(j * bn, (j + 1) * bn)
            df1 = (2.0 * r_ref[:, cols].astype(F32) * _dot_nt(d, w2_ref[cols, :])).astype(BF16)
            df1_ref[:, cols] = df1
            dh = dh + _dot_nt(df1, w1_ref[j])
        xv = x_ref[...]
        dxn, dg = _rms_bwd(dh, xv, _rms(xv), g_ref[...])
        dx = dres_ref[...] + dxn
        dx_ref[...] = dx
        dxb_ref[...] = dx.astype(BF16)
        dg_ref[...] += dg

    once = dict(pipeline_mode=pl.Buffered(1))
    return pl.pallas_call(
        body, name="bwd_mlp", grid=(T // TM,),
        in_specs=[_rows(TM, D), _rows(TM, D), pl.BlockSpec((D_FF, D), lambda i: (0, 0), **once), _rows(TM, D_FF),
                  pl.BlockSpec(wg1.shape, lambda i: (0, 0, 0), **once), _rows(TM, D), _full((1, D)), HBM],
        out_specs=[_rows(TM, D_FF), _rows(TM, D), _rows(TM, D), _full((1, D))],
        out_shape=[jax.ShapeDtypeStruct((T, D_FF), BF16), jax.ShapeDtypeStruct((T, D), F32),
                   jax.ShapeDtypeStruct((T, D), BF16), jax.ShapeDtypeStruct((1, D), F32)],
        compiler_params=_cparams(56),
    )(*map(_hbm, (dxb, dres, w2, r, wg1, x1, g2, dep)))


def _mixer_bwd_a(dxb, wout, proj, c, lg, lb, wm, bst, clg, clb, dep):
    tm = TM_SGU
    n_tiles = T // tm

    def body(dx_ref, wo_ref, p_ref, c_ref, lg_ref, lb_ref, wm_ref, bs_ref, clg_ref, clb_ref, dep_ref,
             dpa_ref, dc_ref, dlg_ref, dlb_ref, dwm_ref, dbs_ref, dcb_ref, dclg_ref, dclb_ref,
             dv_buf, db_acc):
        i = pl.program_id(0)

        @pl.when(i == 0)
        def _():
            for ref in (dlg_ref, dlb_ref, dwm_ref, dbs_ref, dcb_ref, dclg_ref, dclb_ref, db_acc):
                ref[...] = jnp.zeros(ref.shape, F32)

        dmix = _dot_nt(dx_ref[...], wo_ref[...])
        ua = p_ref[:, 0:D_A]
        va = p_ref[:, D_A:2 * D_A]
        u, cdf_u = _gelu(ua)
        vg, cdf_v = _gelu(va)
        xhat, rstd = _ln(vg)
        v = (xhat * lg_ref[...] + lb_ref[...]).astype(BF16)
        mask = _tril_mask()
        for h in range(H_A):
            hc = slice(h * CHUNK, (h + 1) * CHUNK)
            wmh = jnp.where(mask, wm_ref[h], 0.0).astype(BF16)
            for cidx in range(tm // CHUNK):
                rc = slice(cidx * CHUNK, (cidx + 1) * CHUNK)
                vb = v[rc, hc]
                mixed = _dot(wmh, vb) + bs_ref[:, h:h + 1]
                da = dmix[rc, hc]
                dpa_ref[rc, hc] = (da * mixed * _gelu_grad(ua[rc, hc], cdf_u[rc, hc])).astype(BF16)
                dmixed = da * u[rc, hc]
                dmb = dmixed.astype(BF16)
                dv_buf[rc, hc] = _dot_tn(wmh, dmb)
                dwm_ref[h] += _dot_nt(dmb, vb)
                db_acc[:, hc] += dmixed
        dv = dv_buf[...]
        dlb_ref[...] += jnp.sum(dv, axis=0, keepdims=True)
        dlg_ref[...] += jnp.sum(dv * xhat, axis=0, keepdims=True)
        dvg = _ln_bwd(dv * lg_ref[...], xhat, rstd)
        dpa_ref[:, D_A:2 * D_A] = (dvg * _gelu_grad(va, cdf_v)).astype(BF16)

        for q in range(H_B):
            cols = slice(q * LANES, (q + 1) * LANES)
            chat, crstd = _ln(c_ref[:, cols])
            z = chat * clg_ref[:, cols] + clb_ref[:, cols]
            sg = _sigmoid(z)
            dz = dmix[:, D_A + q * LANES:D_A + (q + 1) * LANES] * (sg * (1.0 + z * (1.0 - sg)))
            dclb_ref[:, cols] += jnp.sum(dz, axis=0, keepdims=True)
            dclg_ref[:, cols] += jnp.sum(dz * chat, axis=0, keepdims=True)
            dc = _ln_bwd(dz * clg_ref[:, cols], chat, crstd)
            dc_ref[:, cols] = dc
            dcb_ref[:, cols] += jnp.sum(dc, axis=0, keepdims=True)

        @pl.when(i == n_tiles - 1)
        def _():
            for h in range(H_A):
                hc = slice(h * CHUNK, (h + 1) * CHUNK)
                dwm_ref[h] = jnp.where(mask, dwm_ref[h], 0.0)
                dbs_ref[h:h + 1, :] = jnp.sum(db_acc[:, hc].T, axis=0, keepdims=True)

    vec = _full((1, D_A))
    vshape = jax.ShapeDtypeStruct((1, D_A), F32)
    return pl.pallas_call(
        body, name="mixer_bwd_a", grid=(n_tiles,),
        in_specs=[_rows(tm, D), _full((D, D)), _rows(tm, IN_COLS), _rows(tm, D_B), vec, vec,
                  _full((H_A, CHUNK, CHUNK)), _full((CHUNK, H_A)), vec, vec, HBM],
        out_specs=[_rows(tm, 2 * D_A), _rows(tm, D_B), vec, vec, _full((H_A, CHUNK, CHUNK)),
                   _full((H_A, CHUNK)), vec, vec, vec],
        out_shape=[jax.ShapeDtypeStruct((T, 2 * D_A), BF16), jax.ShapeDtypeStruct((T, D_B), F32), vshape, vshape,
                   jax.ShapeDtypeStruct((H_A, CHUNK, CHUNK), F32), jax.ShapeDtypeStruct((H_A, CHUNK), F32),
                   vshape, vshape, vshape],
        scratch_shapes=[pltpu.VMEM((tm, D_A), F32), pltpu.VMEM((CHUNK, D_A), F32)],
        compiler_params=_cparams(32),
    )(*map(_hbm, (dxb, wout, proj, c, lg, lb, wm, bst, clg, clb, dep)))


def _mixer_bwd_b(dc, g, proj, cw, dpa, wg_in, x, g1, dres):
    tm = TM_MIX
    n_tiles = T // tm
    hb = tm // HALO
    bn = wg_in.shape[2]

    def body(dc_ref, dch_ref, g_ref, p_ref, cw_ref, dpa_ref, win_ref, x_ref, g1_ref, dres_ref,
             dp_ref, dcw_ref, dx_ref, dxb_ref, dg1_ref, dcbuf, dwacc):
        i = pl.program_id(0)

        @pl.when(i == 0)
        def _():
            dwacc[...] = jnp.zeros(dwacc.shape, F32)
            dg1_ref[...] = jnp.zeros(dg1_ref.shape, F32)

        _fill_shift_buffer(dcbuf, 0, dc_ref[...])
        _fill_shift_buffer(dcbuf, tm, jnp.where(i < n_tiles - 1, dch_ref[...], 0.0))
        _build_shifts(dcbuf)
        dp_ref[:, 0:2 * D_A] = dpa_ref[...]
        for q in range(H_B):
            cols = slice(q * LANES, (q + 1) * LANES)

            def row_block(rb, carry, q=q, cols=cols):
                base = pl.multiple_of(rb * RB, RB)
                rows = pl.ds(base, RB)
                gv = g_ref[rows, cols]
                acc = [jnp.zeros((RB, LANES), F32) for _ in range(CONV_CHAINS)]
                parts = []
                for k in range(CONV_W):
                    xk = _shifted(dcbuf, q, base, CONV_W - 1 - k)
                    acc[k % CONV_CHAINS] = acc[k % CONV_CHAINS] + xk * cw_ref[k:k + 1, cols]
                    parts.append(jnp.sum((gv * xk).reshape(RB // 8, 8, LANES), axis=0))
                dg = functools.reduce(lambda a, b: a + b, acc)
                val = p_ref[rows, cols]
                sg = _sigmoid(p_ref[rows, D_B + q * LANES:D_B + (q + 1) * LANES])
                dp_ref[rows, 2 * D_A + q * LANES:2 * D_A + (q + 1) * LANES] = (dg * sg).astype(BF16)
                dp_ref[rows, 2 * D_A + D_B + q * LANES:2 * D_A + D_B + (q + 1) * LANES] = (
                    dg * val * sg * (1.0 - sg)).astype(BF16)
                for k in range(CONV_W):
                    dwacc[k * 8:(k + 1) * 8, cols] += parts[k]
                return carry

            lax.fori_loop(0, tm // RB, row_block, 0)

        dh = jnp.zeros((tm, D), F32)
        for j in range(N_DEV):
            dh = dh + _dot_nt(dp_ref[:, j * bn:(j + 1) * bn], win_ref[j])
        xv = x_ref[...]
        dxn, dg = _rms_bwd(dh, xv, _rms(xv), g1_ref[...])
        dx = dres_ref[...] + dxn
        dx_ref[...] = dx
        dxb_ref[...] = dx.astype(BF16)
        dg1_ref[...] += dg

        @pl.when(i == n_tiles - 1)
        def _():
            for k in range(CONV_W):
                dcw_ref[k:k + 1, :] = jnp.sum(dwacc[k * 8:(k + 1) * 8, :], axis=0, keepdims=True)

    return pl.pallas_call(
        body, name="mixer_bwd_b", grid=(n_tiles,),
        in_specs=[_rows(tm, D_B),
                  pl.BlockSpec((HALO, D_B), lambda i: (jnp.minimum((i + 1) * hb, T // HALO - 1), 0)),
                  _rows(tm, D_B),
                  pl.BlockSpec((tm, 2 * D_B), lambda i: (i, 1)),
                  _full((CONV_W, D_B)), _rows(tm, 2 * D_A),
                  pl.BlockSpec(wg_in.shape, lambda i: (0, 0, 0), pipeline_mode=pl.Buffered(1)),
                  _rows(tm, D), _full((1, D)), _rows(tm, D)],
        out_specs=[_rows(tm, IN_COLS), _full((CONV_W, D_B)), _rows(tm, D), _rows(tm, D), _full((1, D))],
        out_shape=[jax.ShapeDtypeStruct((T, IN_COLS), BF16), jax.ShapeDtypeStruct((CONV_W, D_B), F32),
                   jax.ShapeDtypeStruct((T, D), F32), jax.ShapeDtypeStruct((T, D), BF16),
                   jax.ShapeDtypeStruct((1, D), F32)],
        scratch_shapes=[pltpu.VMEM((8, H_B, tm + HALO, LANES), F32), pltpu.VMEM((CONV_W * 8, D_B), F32)],
        compiler_params=_cparams(40),
    )(*map(_hbm, (dc, dc, g, proj, cw, dpa, wg_in, x, g1, dres)))


def _wgrad(name, a, g, bn, square_a=False):
    k = a.shape[1]
    n = g.shape[1]
    tk = min(k, 1024)
    tn = min(n, max(bn, 1024))
    nsub = tn // bn
    tt = TT_WGRAD
    nt = T // tt

    def body(a_ref, g_ref, o_ref, ob_ref):
        t = pl.program_id(2)

        @pl.when(t == 0)
        def _():
            o_ref[...] = jnp.zeros(o_ref.shape, F32)

        av = a_ref[...]
        if square_a:
            av = av * av
        for s in range(nsub):
            o_ref[s] += _dot_tn(av, g_ref[:, s * bn:(s + 1) * bn])

        @pl.when(t == nt - 1)
        def _():
            ob_ref[...] = o_ref[...].astype(BF16)

    ospec = pl.BlockSpec((nsub, tk, bn), lambda ki, ni, ti: (ni, ki, 0))
    return pl.pallas_call(
        body, name=name, grid=(k // tk, n // tn, nt),
        in_specs=[pl.BlockSpec((tt, tk), lambda ki, ni, ti: (ti, ki)),
                  pl.BlockSpec((tt, tn), lambda ki, ni, ti: (ti, ni))],
        out_specs=[ospec, ospec],
        out_shape=[jax.ShapeDtypeStruct((n // bn, k, bn), F32), jax.ShapeDtypeStruct((n // bn, k, bn), BF16)],
        compiler_params=_cparams(40, ("arbitrary", "arbitrary", "arbitrary")),
    )(a, g)


def _coords():
    return lax.axis_index("x"), lax.axis_index("y"), lax.axis_index("c")


def _flat(x, y, c):
    return 4 * x + 2 * y + c


def _peer(k):
    x, y, c = _coords()
    return (x ^ ((k >> 2) & 1), y ^ ((k >> 1) & 1), c ^ (k & 1))


def _hbm(a):
    return pltpu.with_memory_space_constraint(a, pltpu.HBM)


def _hbm_like(a):
    return pltpu.HBM(a.shape, a.dtype)


def _peer_sems():
    return pltpu.SemaphoreType.DMA((N_DEV - 1,))


def _place_own(shards):
    n = len(shards)
    shapes = [(s.shape if l is None else s.shape[1:]) for s, l, _ in shards]

    def body(*refs):
        ins, outs = refs[:n], refs[n:2 * n]
        stage_in, stage_out = refs[2 * n:3 * n], refs[3 * n:4 * n]
        in_sems, out_sems = refs[4 * n], refs[4 * n + 1]
        me = _flat(*_coords())
        srcs = [ins[a] if shards[a][1] is None else ins[a].at[shards[a][1]] for a in range(n)]
        loads = [pltpu.make_async_copy(srcs[a], stage_in[a], in_sems.at[a]) for a in range(n)]
        stores = [pltpu.make_async_copy(stage_out[a], outs[a].at[me], out_sems.at[a]) for a in range(n)]
        for cp in loads:
            cp.start()
        for a in range(n):
            loads[a].wait()
            stage_out[a][...] = stage_in[a][...].astype(stage_out[a].dtype)
            stores[a].start()
        for cp in stores:
            cp.wait()

    return pl.pallas_call(
        body, name="place_own", in_specs=[HBM] * n, out_specs=[HBM] * n,
        out_shape=[pltpu.HBM((N_DEV,) + shapes[a], shards[a][2]) for a in range(n)],
        scratch_shapes=[*[pltpu.VMEM(shapes[a], shards[a][0].dtype) for a in range(n)],
                        *[pltpu.VMEM(shapes[a], shards[a][2]) for a in range(n)],
                        pltpu.SemaphoreType.DMA((n,)), pltpu.SemaphoreType.DMA((n,))],
        compiler_params=pltpu.CompilerParams(vmem_limit_bytes=40 * MIB),
    )(*[_hbm(s) for s, _, _ in shards])


SIBLING = 1
CHIP_PEERS = (2, 4, 6)
FIRST_PEERS = (SIBLING,) + CHIP_PEERS


def _gather_start(lands):
    n = len(lands)

    def body(*refs):
        lnd, send, recv = refs[:n], refs[n:2 * n], refs[2 * n:3 * n]
        me = _flat(*_coords())
        for a in range(n):
            for j, k in enumerate(FIRST_PEERS):
                pltpu.make_async_remote_copy(
                    src_ref=lnd[a].at[me], dst_ref=lnd[a].at[me], send_sem=send[a].at[j],
                    recv_sem=recv[a].at[j], device_id=_peer(k), device_id_type=MESH_ID).start()

    sems = pltpu.SemaphoreType.DMA((len(FIRST_PEERS),))
    outs = pl.pallas_call(
        body, name="gather_start",
        out_shape=(*[sems] * (2 * n), *[_hbm_like(l) for l in lands]),
        in_specs=[HBM] * n, out_specs=(*[SEM] * (2 * n), *[HBM] * n),
        input_output_aliases={i: 2 * n + i for i in range(n)},
        compiler_params=pltpu.CompilerParams(has_side_effects=EFFECT),
    )(*[_hbm(l) for l in lands])
    return outs[:n], outs[n:2 * n], outs[2 * n:]


def _gather_pass_on(name, lands, sends, recvs, after):
    n = len(lands)

    def body(*refs):
        lnd, send, recv = refs[:n], refs[n:2 * n], refs[2 * n:3 * n]
        send2, recv2 = refs[3 * n + 1:4 * n + 1], refs[4 * n + 1:5 * n + 1]
        sibling = _peer(SIBLING)
        for a in range(n):
            for j, k in enumerate(CHIP_PEERS):
                cp = pltpu.make_async_remote_copy(
                    src_ref=lnd[a].at[0], dst_ref=lnd[a].at[0], send_sem=send[a].at[1 + j],
                    recv_sem=recv[a].at[1 + j], device_id=_peer(k), device_id_type=MESH_ID)
                cp.wait_send()
                cp.wait_recv()
                blk = _flat(*_peer(k))
                pltpu.make_async_remote_copy(
                    src_ref=lnd[a].at[blk], dst_ref=lnd[a].at[blk], send_sem=send2[a].at[j],
                    recv_sem=recv2[a].at[j], device_id=sibling, device_id_type=MESH_ID).start()

    sems = pltpu.SemaphoreType.DMA((len(CHIP_PEERS),))
    outs = pl.pallas_call(
        body, name=name, out_shape=(*[sems] * (2 * n), *[_hbm_like(l) for l in lands]),
        in_specs=(*[HBM] * n, *[SEM] * (2 * n), HBM), out_specs=(*[SEM] * (2 * n), *[HBM] * n),
        input_output_aliases={i: 2 * n + i for i in range(n)},
        compiler_params=pltpu.CompilerParams(has_side_effects=EFFECT),
    )(*lands, *sends, *recvs, after)
    return outs[:n], outs[n:2 * n], outs[2 * n:]


def _gather_wait(name, lands, sends, recvs, sends2, recvs2, after):
    n = len(lands)

    def body(*refs):
        lnd, send, recv = refs[:n], refs[n:2 * n], refs[2 * n:3 * n]
        send2, recv2 = refs[3 * n:4 * n], refs[4 * n:5 * n]
        sibling = _peer(SIBLING)
        for a in range(n):
            own = pltpu.make_async_remote_copy(
                src_ref=lnd[a].at[0], dst_ref=lnd[a].at[0], send_sem=send[a].at[0], recv_sem=recv[a].at[0],
                device_id=sibling, device_id_type=MESH_ID)
            own.wait_send()
            own.wait_recv()
            for j in range(len(CHIP_PEERS)):
                cp = pltpu.make_async_remote_copy(
                    src_ref=lnd[a].at[0], dst_ref=lnd[a].at[0], send_sem=send2[a].at[j], recv_sem=recv2[a].at[j],
                    device_id=sibling, device_id_type=MESH_ID)
                cp.wait_send()
                cp.wait_recv()

    return pl.pallas_call(
        body, name=name, out_shape=[_hbm_like(l) for l in lands],
        in_specs=(*[HBM] * n, *[SEM] * (4 * n), HBM), out_specs=[HBM] * n,
        input_output_aliases={i: i for i in range(n)},
        compiler_params=pltpu.CompilerParams(has_side_effects=EFFECT),
    )(*lands, *sends, *recvs, *sends2, *recvs2, after)


def _scatter_start(name, grads, lands, layer):
    n = len(grads)

    def body(*refs):
        g, lnd, send, recv = refs[:n], refs[n:2 * n], refs[2 * n:3 * n], refs[3 * n:4 * n]
        for a in range(n):
            for k in range(1, N_DEV):
                to = _peer(k)
                pltpu.make_async_remote_copy(
                    src_ref=g[a].at[_flat(*to)], dst_ref=lnd[a].at[layer, k - 1], send_sem=send[a].at[k - 1],
                    recv_sem=recv[a].at[k - 1], device_id=to, device_id_type=MESH_ID).start()

    outs = pl.pallas_call(
        body, name=name,
        out_shape=(*[_peer_sems()] * (2 * n), *[_hbm_like(g) for g in grads], *[_hbm_like(l) for l in lands]),
        in_specs=[HBM] * (2 * n), out_specs=(*[SEM] * (2 * n), *[HBM] * (2 * n)),
        input_output_aliases={i: 2 * n + i for i in range(2 * n)},
        compiler_params=pltpu.CompilerParams(has_side_effects=EFFECT),
    )(*[_hbm(g) for g in grads], *[_hbm(l) for l in lands])
    return outs[:n], outs[n:2 * n], outs[2 * n:3 * n], outs[3 * n:]


def _scatter_wait(name, grads, lands, sends, recvs, after):
    n, nw = len(grads), len(lands)

    def body(*refs):
        g, lnd = refs[:n], refs[n:n + nw]
        send, recv = refs[n + nw:2 * n + nw], refs[2 * n + nw:3 * n + nw]
        for a in range(n):
            for k in range(1, N_DEV):
                cp = pltpu.make_async_remote_copy(
                    src_ref=g[a].at[0], dst_ref=lnd[a // DEPTH].at[a % DEPTH, 0],
                    send_sem=send[a].at[k - 1], recv_sem=recv[a].at[k - 1],
                    device_id=_peer(k), device_id_type=MESH_ID)
                cp.wait_send()
                cp.wait_recv()

    outs = pl.pallas_call(
        body, name=name,
        out_shape=(*[_hbm_like(g) for g in grads], *[_hbm_like(l) for l in lands]),
        in_specs=(*[HBM] * (n + nw), *[SEM] * (2 * n), HBM), out_specs=[HBM] * (n + nw),
        input_output_aliases={i: i for i in range(n + nw)},
        compiler_params=pltpu.CompilerParams(has_side_effects=EFFECT),
    )(*grads, *lands, *sends, *recvs, after)
    return outs[n:]


def _all_reduce_small(part, deps):
    rows = part.shape[0]
    br = rows // N_DEV
    assert br * N_DEV == rows and br % 8 == 0
    nd = len(deps)

    def body(p_ref, *refs):
        o_ref, slots, send1, recv1, send2, recv2 = refs[nd:]
        me = _flat(*_coords())

        def block(ref, d):
            return ref.at[pl.ds(pl.multiple_of(d * br, 8), br), :]

        slots[me] = p_ref[pl.ds(pl.multiple_of(me * br, 8), br), :]
        scatter, gather = [], []
        for k in range(1, N_DEV):
            to = _peer(k)
            scatter.append(pltpu.make_async_remote_copy(
                src_ref=block(p_ref, _flat(*to)), dst_ref=slots.at[me],
                send_sem=send1.at[k - 1], recv_sem=recv1.at[k - 1], device_id=to, device_id_type=MESH_ID))
            gather.append(pltpu.make_async_remote_copy(
                src_ref=block(o_ref, me), dst_ref=block(o_ref, me),
                send_sem=send2.at[k - 1], recv_sem=recv2.at[k - 1], device_id=to, device_id_type=MESH_ID))
        for cp in scatter:
            cp.start()
        for cp in scatter:
            cp.wait()
        acc = slots[0]
        for d in range(1, N_DEV):
            acc = acc + slots[d]
        o_ref[pl.ds(pl.multiple_of(me * br, 8), br), :] = acc
        for cp in gather:
            cp.start()
        for cp in gather:
            cp.wait()

    return pl.pallas_call(
        body, name="all_reduce_small",
        in_specs=[pl.BlockSpec(memory_space=pltpu.VMEM)] + [HBM] * nd, out_specs=pl.BlockSpec(memory_space=pltpu.VMEM),
        out_shape=jax.ShapeDtypeStruct(part.shape, F32),
        scratch_shapes=[pltpu.VMEM((N_DEV, br, LANES), F32)] + [pltpu.SemaphoreType.DMA((N_DEV - 1,))] * 4,
        compiler_params=pltpu.CompilerParams(vmem_limit_bytes=24 * MIB),
    )(part, *map(_hbm, deps))


def _adam_math(w, g, m, v):
    m = ADAM_B1 * m + (1.0 - ADAM_B1) * g
    v = ADAM_B2 * v + (1.0 - ADAM_B2) * (g * g)
    m_hat = m / (1.0 - ADAM_B1 ** ADAM_STEP)
    v_hat = v / (1.0 - ADAM_B2 ** ADAM_STEP)
    delta = -ADAM_LR * (m_hat / (jnp.sqrt(v_hat) + ADAM_EPS) + ADAM_WD * w)
    return delta, m, v


def _adam_sharded(name, me, g32, recv, w, m, v):
    _, r, c = w.shape
    tr = min(r, 256)
    nblk = r // tr
    assert len(g32) == DEPTH == 2

    def body(me_ref, g0_ref, g1_ref, recv_ref, w_ref, m_ref, v_ref, g_ref, d_ref, nm_ref, nv_ref):
        g = jnp.where(pl.program_id(0) == 0, g0_ref[...], g1_ref[...])
        for k in range(N_DEV - 1):
            g = g + recv_ref[k].astype(F32)
        delta, nm, nv = _adam_math(w_ref[...], g, m_ref[...], v_ref[...])
        g_ref[...] = g
        d_ref[...] = delta
        nm_ref[...] = nm
        nv_ref[...] = nv

    blk = pl.BlockSpec((None, tr, c), lambda l, i, me_ref: (l, i, 0))
    own0 = pl.BlockSpec((None, tr, c), lambda l, i, me_ref: (me_ref[0], jnp.where(l == 0, i, nblk - 1), 0))
    own1 = pl.BlockSpec((None, tr, c), lambda l, i, me_ref: (me_ref[0], jnp.where(l == 1, i, 0), 0))
    shp = jax.ShapeDtypeStruct(w.shape, F32)
    return pl.pallas_call(
        body, name=name,
        grid_spec=pltpu.PrefetchScalarGridSpec(
            num_scalar_prefetch=1, grid=(DEPTH, nblk),
            in_specs=[own0, own1, pl.BlockSpec((None, N_DEV - 1, tr, c), lambda l, i, me_ref: (l, 0, i, 0)),
                      blk, blk, blk],
            out_specs=[blk] * 4),
        out_shape=[shp] * 4,
        compiler_params=_cparams(32, ("arbitrary", "arbitrary")),
    )(me, *[_hbm(a) for a in (*g32, recv, w, m, v)])


def _adam_small(gs, ws, ms, vs):
    n = len(gs)

    def body(*refs):
        g, w, m, v = refs[:n], refs[n:2 * n], refs[2 * n:3 * n], refs[3 * n:4 * n]
        d, nm, nv = refs[4 * n:5 * n], refs[5 * n:6 * n], refs[6 * n:7 * n]
        for a in range(n):
            delta, new_m, new_v = _adam_math(w[a][...], g[a][...], m[a][...], v[a][...])
            d[a][...] = delta
            nm[a][...] = new_m
            nv[a][...] = new_v

    specs = [_full(a.shape) for a in gs]
    shapes = [jax.ShapeDtypeStruct(a.shape, F32) for a in gs]
    outs = pl.pallas_call(
        body, name="adam_small", grid=(1,),
        in_specs=specs * 4, out_specs=specs * 3, out_shape=shapes * 3,
        compiler_params=_cparams(24),
    )(*gs, *ws, *ms, *vs)
    return outs[:n], outs[n:2 * n], outs[2 * n:]


SMALL = ["norm1_g", "sgu_ln_g", "sgu_ln_b", "sgu_w", "sgu_b", "conv_b", "conv_ln_g", "conv_ln_b", "norm2_g",
         "final_g"]


def _pack(arrays):
    flat = jnp.concatenate([a.reshape(-1) for a in arrays])
    pad = (-flat.shape[0]) % (N_DEV * 8 * LANES)
    return jnp.pad(flat, (0, pad)).reshape(-1, LANES)


def _unpack(packed, shapes):
    flat = packed.reshape(-1)
    out, off = [], 0
    for s in shapes:
        size = 1
        for d in s:
            size *= d
        out.append(flat[off:off + size].reshape(s))
        off += size
    return out


def kernel(x, norm1_g, w_in, sgu_ln_g, sgu_ln_b, sgu_w, sgu_b, conv_w, conv_b, conv_ln_g, conv_ln_b, w_out, norm2_g, w_ff1, w_ff2, final_g, loss_target, m_norm1_g, m_w_in, m_sgu_ln_g, m_sgu_ln_b, m_sgu_w, m_sgu_b, m_conv_w, m_conv_b, m_conv_ln_g, m_conv_ln_b, m_w_out, m_norm2_g, m_w_ff1, m_w_ff2, m_final_g, v_norm1_g, v_w_in, v_sgu_ln_g, v_sgu_ln_b, v_sgu_w, v_sgu_b, v_conv_w, v_conv_b, v_conv_ln_g, v_conv_ln_b, v_w_out, v_norm2_g, v_w_ff1, v_w_ff2, v_final_g):
    x2d = x.reshape(T, D)
    tgt = loss_target.reshape(T, D)
    cw_shard = conv_w.reshape(CONV_W, LANES)

    gnames, shards = [], []
    for l in range(DEPTH):
        for k, w in (("w_in", w_in), ("w_out", w_out), ("w_ff1", w_ff1), ("w_ff2", w_ff2)):
            gnames.append(f"{k}{l}")
            shards.append((w, l, BF16))
        if l == 0:
            gnames.insert(1, "conv_w")
            shards.insert(1, (cw_shard, None, F32))
    sends, recvs, lands = _gather_start(_place_own(shards))
    gidx = {k: i for i, k in enumerate(gnames)}

    passed = {}

    def pass_on(ks, after):
        idx = [gidx[k] for k in ks]
        s2, r2, ld = _gather_pass_on("gather_pass_on_" + "_".join(ks), [lands[i] for i in idx],
                                     [sends[i] for i in idx], [recvs[i] for i in idx], after)
        for j, k in enumerate(ks):
            passed[k] = (s2[j], r2[j], ld[j])

    def gathered(ks, after):
        idx = [gidx[k] for k in ks]
        return _gather_wait("gather_wait_" + "_".join(ks), [passed[k][2] for k in ks], [sends[i] for i in idx],
                            [recvs[i] for i in idx], [passed[k][0] for k in ks], [passed[k][1] for k in ks], after)

    saved = []
    xl = x2d
    cw_full = None
    for l in range(DEPTH):
        g1 = norm1_g[l].reshape(1, D)
        g2 = norm2_g[l].reshape(1, D)
        lg, lb = sgu_ln_g[l].reshape(1, D_A), sgu_ln_b[l].reshape(1, D_A)
        bst = sgu_b[l].T
        cb = conv_b[l].reshape(1, D_B)
        clg, clb = conv_ln_g[l].reshape(1, D_B), conv_ln_b[l].reshape(1, D_B)
        pass_on([f"w_in{l}"], xl)
        (wg_in,) = gathered([f"w_in{l}"], xl)
        h, proj = _fwd_in(xl, g1, wg_in)
        if l == 0:
            pass_on(["conv_w", "w_out0"], h)
            cw_g, wout = gathered(["conv_w", "w_out0"], proj)
            cw_full = cw_g.reshape(N_DEV, DEPTH, CONV_W, D_B // N_DEV).transpose(1, 2, 0, 3).reshape(
                DEPTH, CONV_W, D_B)
        else:
            pass_on([f"w_out{l}"], h)
            (wout,) = gathered([f"w_out{l}"], proj)
            pass_on([f"w_ff1{l}"], proj)
        wout = wout.reshape(D, D)
        mix, c, g, x1, h2 = _mixer_fwd(proj, lg, lb, sgu_w[l], bst, cw_full[l], cb, clg, clb, wout, xl, g2)
        pass_on([f"w_ff1{l}"] if l == 0 else [f"w_ff2{l}"], h2)
        (wg_ff1,) = gathered([f"w_ff1{l}"], h2)
        r = _fwd_ff1(h2, wg_ff1)
        if l == 0:
            pass_on(["w_ff20"], r)
        (w2,) = gathered([f"w_ff2{l}"], r)
        w2 = w2.reshape(D_FF, D)
        saved.append(dict(x=xl, h=h, proj=proj, mix=mix, c=c, g=g, x1=x1, h2=h2, r=r, wg_in=wg_in, wout=wout,
                          wg_ff1=wg_ff1, w2=w2, g1=g1, g2=g2, lg=lg, lb=lb, bst=bst, clg=clg, clb=clb))
        if l + 1 < DEPTH:
            xl = _fwd_ff2(r, w2, x1)
        else:
            dx, dxb, loss_part, d_final_g = _fwd_ff2_loss(r, w2, x1, final_g.reshape(1, D), tgt)

    names = ["w_in", "w_out", "w_ff1", "w_ff2"]
    block = dict(w_in=(D, IN_COLS // N_DEV), w_out=(D // N_DEV, D), w_ff1=(D, D_FF // N_DEV), w_ff2=(D_FF // N_DEV, D))
    land = {k: lax.empty((DEPTH, N_DEV - 1) + block[k], BF16) for k in names}
    big32 = {k: [None] * DEPTH for k in names}
    big16 = {k: [None] * DEPTH for k in names}
    ssend = {k: [None] * DEPTH for k in names}
    srecv = {k: [None] * DEPTH for k in names}

    def send_grads(l, grads):
        ks = list(grads)
        for k in ks:
            big32[k][l] = grads[k][0].reshape((N_DEV,) + block[k])
        sends, recvs, g16s, lands = _scatter_start(
            "scatter_start_" + "_".join(ks) + str(l), [grads[k][1].reshape((N_DEV,) + block[k]) for k in ks],
            [land[k] for k in ks], l)
        for i, k in enumerate(ks):
            ssend[k][l], srecv[k][l], big16[k][l], land[k] = sends[i], recvs[i], g16s[i], lands[i]

    small = {}
    d_conv_w = [None] * DEPTH
    for l in reversed(range(DEPTH)):
        s = saved[l]
        send_grads(l, dict(w_ff2=_wgrad("wgrad_ff2", s["r"], dxb, D, square_a=True)))
        df1, dx1, dx1b, dg2 = _bwd_mlp(dxb, dx, s["w2"], s["r"], s["wg_ff1"], s["x1"], s["g2"], big16["w_ff2"][l])
        send_grads(l, dict(w_ff1=_wgrad("wgrad_ff1", s["h2"], df1, D_FF // N_DEV),
                           w_out=_wgrad("wgrad_out", s["mix"], dx1b, D)))
        dpa, dc, dlg, dlb, dwm, dbs, dcb, dclg, dclb = _mixer_bwd_a(
            dx1b, s["wout"], s["proj"], s["c"], s["lg"], s["lb"], sgu_w[l], s["bst"], s["clg"], s["clb"],
            big16["w_out"][l])
        dproj, d_conv_w[l], dx, dxb, dg1 = _mixer_bwd_b(dc, s["g"], s["proj"], cw_full[l], dpa, s["wg_in"], s["x"],
                                                         s["g1"], dx1)
        send_grads(l, dict(w_in=_wgrad("wgrad_in", s["h"], dproj, IN_COLS // N_DEV)))
        small[l] = dict(norm1_g=dg1, sgu_ln_g=dlg, sgu_ln_b=dlb, sgu_w=dwm, sgu_b=dbs, conv_b=dcb, conv_ln_g=dclg,
                        conv_ln_b=dclb, norm2_g=dg2)
    grad_x = dx.reshape(1, T, D)

    me_block = _flat(*_coords()).astype(jnp.int32).reshape(1)
    wmv = dict(w_in=(w_in, m_w_in, v_w_in), w_out=(w_out, m_w_out, v_w_out), w_ff1=(w_ff1, m_w_ff1, v_w_ff1),
               w_ff2=(w_ff2, m_w_ff2, v_w_ff2))
    res = {}

    def update(tag, ks, after):
        recv = _scatter_wait("scatter_wait_" + tag, [g for k in ks for g in big16[k]], [land[k] for k in ks],
                             [q for k in ks for q in ssend[k]], [q for k in ks for q in srecv[k]], after)
        for k, rk in zip(ks, recv):
            res[k] = _adam_sharded("adam_" + k, me_block, big32[k], rk, *wmv[k])

    update("early", ["w_ff2", "w_ff1", "w_out"], big16["w_in"][0])

    rep = dict(norm1_g=(norm1_g, m_norm1_g, v_norm1_g), sgu_ln_g=(sgu_ln_g, m_sgu_ln_g, v_sgu_ln_g),
               sgu_ln_b=(sgu_ln_b, m_sgu_ln_b, v_sgu_ln_b), sgu_w=(sgu_w, m_sgu_w, v_sgu_w),
               sgu_b=(sgu_b, m_sgu_b, v_sgu_b), conv_b=(conv_b, m_conv_b, v_conv_b),
               conv_ln_g=(conv_ln_g, m_conv_ln_g, v_conv_ln_g), conv_ln_b=(conv_ln_b, m_conv_ln_b, v_conv_ln_b),
               norm2_g=(norm2_g, m_norm2_g, v_norm2_g), final_g=(final_g, m_final_g, v_final_g))
    parts = []
    for k in SMALL:
        if k == "final_g":
            parts.append(d_final_g.reshape(rep[k][0].shape))
        else:
            parts.append(jnp.stack([small[l][k].reshape(rep[k][0].shape[1:]) for l in range(DEPTH)]))
    parts.append(jnp.stack(d_conv_w))
    parts.append(loss_part[0, 0:1])
    shapes = [p.shape for p in parts]
    packed_sum = _all_reduce_small(_pack(parts), [res[k][0] for k in ("w_ff2", "w_ff1", "w_out")])
    update("late", ["w_in"], packed_sum)
    summed = _unpack(packed_sum, shapes)
    loss = summed[-1][0]
    me = _flat(*_coords())
    g_conv_w = lax.dynamic_slice_in_dim(summed[-2], me * (D_B // N_DEV), D_B // N_DEV, axis=2)
    g_small = summed[:-2] + [g_conv_w]
    keys = SMALL + ["conv_w"]
    rep["conv_w"] = (conv_w, m_conv_w, v_conv_w)
    def two_d(a):
        return a.reshape(1, -1) if a.ndim == 1 else a

    d_s, nm_s, nv_s = _adam_small([two_d(g) for g in g_small], *[[two_d(rep[k][i]) for k in keys] for i in range(3)])
    for i, k in enumerate(keys):
        shape = rep[k][0].shape
        res[k] = (g_small[i], d_s[i].reshape(shape), nm_s[i].reshape(shape), nv_s[i].reshape(shape))

    order = ["norm1_g", "w_in", "sgu_ln_g", "sgu_ln_b", "sgu_w", "sgu_b", "conv_w", "conv_b", "conv_ln_g",
             "conv_ln_b", "w_out", "norm2_g", "w_ff1", "w_ff2", "final_g"]
    return (loss, grad_x, *[res[k][0] for k in order], *[res[k][1] for k in order],
            *[res[k][2] for k in order], *[res[k][3] for k in order])
```

```python
import functools

import jax
import jax.numpy as jnp
from jax import lax
from jax.experimental import pallas as pl
from jax.experimental.pallas import tpu as pltpu

F32 = jnp.float32
BF16 = jnp.bfloat16

N_DEV = 8
DEPTH = 2
T = 4096
D = 1024
D_A = 512
D_B = 512
CHUNK = 128
H_A = 4
H_B = 4
CONV_W = 31
HALO = 32
D_FF = 4096
IN_COLS = 2048
EPS = 1e-6

ADAM_LR = 0.001
ADAM_B1 = 0.9
ADAM_B2 = 0.999
ADAM_EPS = 1e-08
ADAM_WD = 0.01
ADAM_STEP = 10

TM = 512
TM_FWD = 1024
TT_WGRAD = 4096
TM_MIX = 256
TM_SGU = 512
RB = 64
CONV_CHAINS = 2
LANES = 128
MIB = 1024 * 1024
SCOPED_VMEM_MIB = 60

SQRT_HALF = 0.7071067811865476
INV_SQRT_2PI = 0.3989422804014327

MESH_ID = pl.DeviceIdType.MESH
HBM = pl.BlockSpec(memory_space=pltpu.HBM)
SEM = pl.BlockSpec(memory_space=pltpu.SEMAPHORE)
EFFECT = pltpu.SideEffectType.DATAFLOW_SIDE_EFFECTING


def _cparams(vmem_mib, sem=("arbitrary",)):
    assert vmem_mib <= SCOPED_VMEM_MIB
    return pltpu.CompilerParams(dimension_semantics=sem, vmem_limit_bytes=SCOPED_VMEM_MIB * MIB)


def _full(shape):
    return pl.BlockSpec(shape, lambda *_: (0,) * len(shape))


def _rows(tm, cols):
    return pl.BlockSpec((tm, cols), lambda i: (i, 0))


def _gelu(x):
    cdf = 0.5 * (1.0 + lax.erf(x * SQRT_HALF))
    return x * cdf, cdf


def _gelu_grad(x, cdf):
    return cdf + x * (INV_SQRT_2PI * jnp.exp(-0.5 * x * x))


def _sigmoid(x):
    return 1.0 / (1.0 + jnp.exp(-x))


def _ln(x):
    mu = jnp.mean(x, axis=-1, keepdims=True)
    xc = x - mu
    rstd = lax.rsqrt(jnp.mean(xc * xc, axis=-1, keepdims=True) + EPS)
    return xc * rstd, rstd


def _ln_bwd(dyh, xhat, rstd):
    return rstd * (dyh - jnp.mean(dyh, axis=-1, keepdims=True) - xhat * jnp.mean(dyh * xhat, axis=-1, keepdims=True))


def _rms(x):
    return lax.rsqrt(jnp.mean(x * x, axis=-1, keepdims=True) + EPS)


def _rms_bwd(dh, x, r, g):
    n = x * r
    dn = dh * g
    dx = r * (dn - n * jnp.mean(dn * n, axis=-1, keepdims=True))
    return dx, jnp.sum(dh * n, axis=0, keepdims=True)


def _dot(a, b):
    return jnp.dot(a, b, preferred_element_type=F32)


def _dot_nt(a, b):
    return lax.dot_general(a, b, (((1,), (1,)), ((), ())), preferred_element_type=F32)


def _dot_tn(a, b):
    return lax.dot_general(a, b, (((0,), (0,)), ((), ())), preferred_element_type=F32)


def _tril_mask():
    r = lax.broadcasted_iota(jnp.int32, (CHUNK, CHUNK), 0)
    c = lax.broadcasted_iota(jnp.int32, (CHUNK, CHUNK), 1)
    return r >= c


def _fwd_in(x, g1, wg):
    bn = wg.shape[2]

    def body(x_ref, g_ref, w_ref, h_ref, p_ref):
        xv = x_ref[...]
        h = (xv * _rms(xv) * g_ref[...]).astype(BF16)
        h_ref[...] = h
        for j in range(N_DEV):
            p_ref[:, j * bn:(j + 1) * bn] = _dot(h, w_ref[j])

    return pl.pallas_call(
        body, name="fwd_in", grid=(T // TM_FWD,),
        in_specs=[_rows(TM_FWD, D), _full((1, D)), _full(wg.shape)],
        out_specs=[_rows(TM_FWD, D), _rows(TM_FWD, IN_COLS)],
        out_shape=[jax.ShapeDtypeStruct((T, D), BF16), jax.ShapeDtypeStruct((T, IN_COLS), F32)],
        compiler_params=_cparams(32),
    )(*map(_hbm, (x, g1, wg)))


def _fill_shift_buffer(sh_ref, row0, value):
    for q in range(sh_ref.shape[1]):
        sh_ref[0, q, row0:row0 + value.shape[0], :] = value[:, q * LANES:(q + 1) * LANES]


def _build_shifts(sh_ref):
    rows = sh_ref.shape[2]
    for p in range(1, 8):
        for q in range(sh_ref.shape[1]):
            sh_ref[p, q, 0:rows - 8, :] = sh_ref[0, q, p:p + rows - 8, :]


def _shifted(sh_ref, q, base, off):
    start = base + (off - off % 8)
    if not isinstance(start, int):
        start = pl.multiple_of(start, 8)
    return sh_ref[off % 8, q, pl.ds(start, RB), :]


def _conv_taps(sh_ref, w_ref, q, base, first_tap_row, step):
    cols = slice(q * LANES, (q + 1) * LANES)
    acc = [jnp.zeros((RB, LANES), F32) for _ in range(CONV_CHAINS)]
    for k in range(CONV_W):
        term = _shifted(sh_ref, q, base, first_tap_row + step * k) * w_ref[k:k + 1, cols]
        acc[k % CONV_CHAINS] = acc[k % CONV_CHAINS] + term
    return functools.reduce(lambda a, b: a + b, acc)


def _mixer_fwd(proj, lg, lb, wm, bst, cw, cb, clg, clb, wout, x, g2):
    tm = TM_SGU
    hb = tm // HALO

    def body(p_ref, ph_ref, lg_ref, lb_ref, wm_ref, bs_ref, cw_ref, cb_ref, clg_ref, clb_ref, wo_ref, x_ref, g2_ref,
             mix_ref, c_ref, g_ref, x1_ref, h2_ref, gbuf):
        i = pl.program_id(0)
        u, _ = _gelu(p_ref[:, 0:D_A])
        vg, _ = _gelu(p_ref[:, D_A:2 * D_A])
        xhat, _ = _ln(vg)
        v = (xhat * lg_ref[...] + lb_ref[...]).astype(BF16)
        mask = _tril_mask()
        for h in range(H_A):
            hc = slice(h * CHUNK, (h + 1) * CHUNK)
            wmh = jnp.where(mask, wm_ref[h], 0.0).astype(BF16)
            for c in range(tm // CHUNK):
                rc = slice(c * CHUNK, (c + 1) * CHUNK)
                mixed = _dot(wmh, v[rc, hc]) + bs_ref[:, h:h + 1]
                mix_ref[rc, hc] = (u[rc, hc] * mixed).astype(BF16)
        x1_ref[...] = x_ref[...] + _dot(mix_ref[:, 0:D_A], wo_ref[0:D_A, :])

        g = p_ref[:, 2 * D_A:2 * D_A + D_B] * _sigmoid(p_ref[:, 2 * D_A + D_B:IN_COLS])
        g_ref[...] = g
        gh = ph_ref[:, 0:D_B] * _sigmoid(ph_ref[:, D_B:2 * D_B])
        _fill_shift_buffer(gbuf, 0, jnp.where(i > 0, gh, 0.0))
        _fill_shift_buffer(gbuf, HALO, g)
        _build_shifts(gbuf)
        for q in range(H_B):
            cols = slice(q * LANES, (q + 1) * LANES)
            for rb in range(tm // RB):
                acc = _conv_taps(gbuf, cw_ref, q, rb * RB, HALO - (CONV_W - 1), 1)
                c_ref[rb * RB:(rb + 1) * RB, cols] = acc + cb_ref[:, cols]
        for q in range(H_B):
            cols = slice(q * LANES, (q + 1) * LANES)
            chat, _ = _ln(c_ref[:, cols])
            z = chat * clg_ref[:, cols] + clb_ref[:, cols]
            mix_ref[:, D_A + q * LANES:D_A + (q + 1) * LANES] = (z * _sigmoid(z)).astype(BF16)

        x1 = x1_ref[...] + _dot(mix_ref[:, D_A:D], wo_ref[D_A:D, :])
        x1_ref[...] = x1
        h2_ref[...] = (x1 * _rms(x1) * g2_ref[...]).astype(BF16)

    vec = _full((1, D_A))
    return pl.pallas_call(
        body, name="mixer_fwd", grid=(T // tm,),
        in_specs=[_rows(tm, IN_COLS),
                  pl.BlockSpec((HALO, 2 * D_B), lambda i: (jnp.maximum(i * hb - 1, 0), 1)),
                  vec, vec, _full((H_A, CHUNK, CHUNK)), _full((CHUNK, H_A)),
                  _full((CONV_W, D_B)), vec, vec, vec,
                  pl.BlockSpec((D, D), lambda i: (0, 0), pipeline_mode=pl.Buffered(1)), _rows(tm, D), _full((1, D))],
        out_specs=[_rows(tm, D), _rows(tm, D_B), _rows(tm, D_B), _rows(tm, D), _rows(tm, D)],
        out_shape=[jax.ShapeDtypeStruct((T, D), BF16), jax.ShapeDtypeStruct((T, D_B), F32),
                   jax.ShapeDtypeStruct((T, D_B), F32), jax.ShapeDtypeStruct((T, D), F32),
                   jax.ShapeDtypeStruct((T, D), BF16)],
        scratch_shapes=[pltpu.VMEM((8, H_B, HALO + tm, LANES), F32)],
        compiler_params=_cparams(40),
    )(*map(_hbm, (proj, proj, lg, lb, wm, bst, cw, cb, clg, clb, wout, x, g2)))


def _fwd_ff1(h2, wg):
    bn = wg.shape[2]

    def body(h_ref, w_ref, r_ref):
        h = h_ref[...]
        for j in range(N_DEV):
            r_ref[:, j * bn:(j + 1) * bn] = jnp.maximum(_dot(h, w_ref[j]), 0.0).astype(BF16)

    return pl.pallas_call(
        body, name="fwd_ff1", grid=(T // TM_FWD,),
        in_specs=[_rows(TM_FWD, D), _full(wg.shape)],
        out_specs=_rows(TM_FWD, D_FF),
        out_shape=jax.ShapeDtypeStruct((T, D_FF), BF16),
        compiler_params=_cparams(48),
    )(h2, wg)


def _fwd_ff2(r, w2, x1):
    def body(r_ref, w_ref, x_ref, o_ref):
        rv = r_ref[...]
        o_ref[...] = x_ref[...] + _dot(rv * rv, w_ref[...])

    return pl.pallas_call(
        body, name="fwd_ff2", grid=(T // TM_FWD,),
        in_specs=[_rows(TM_FWD, D_FF), _full((D_FF, D)), _rows(TM_FWD, D)],
        out_specs=_rows(TM_FWD, D),
        out_shape=jax.ShapeDtypeStruct((T, D), F32),
        compiler_params=_cparams(48),
    )(r, w2, x1)


def _fwd_ff2_loss(r, w2, x1, gf, tgt):
    def body(r_ref, w_ref, x_ref, g_ref, t_ref, dx_ref, dxb_ref, loss_ref, dg_ref):
        i = pl.program_id(0)

        @pl.when(i == 0)
        def _():
            loss_ref[...] = jnp.zeros(loss_ref.shape, F32)
            dg_ref[...] = jnp.zeros(dg_ref.shape, F32)

        rv = r_ref[...]
        xv = x_ref[...] + _dot(rv * rv, w_ref[...])
        rn = _rms(xv)
        diff = xv * rn * g_ref[...] - t_ref[...]
        loss_ref[...] += 0.5 * jnp.sum(jnp.mean(diff * diff, axis=-1, keepdims=True), axis=0, keepdims=True)
        dx, dg = _rms_bwd(diff * (1.0 / D), xv, rn, g_ref[...])
        dx_ref[...] = dx
        dxb_ref[...] = dx.astype(BF16)
        dg_ref[...] += dg

    return pl.pallas_call(
        body, name="fwd_ff2_loss", grid=(T // TM,),
        in_specs=[_rows(TM, D_FF), pl.BlockSpec((D_FF, D), lambda i: (0, 0), pipeline_mode=pl.Buffered(1)),
                  _rows(TM, D), _full((1, D)), _rows(TM, D)],
        out_specs=[_rows(TM, D), _rows(TM, D), _full((8, LANES)), _full((1, D))],
        out_shape=[jax.ShapeDtypeStruct((T, D), F32), jax.ShapeDtypeStruct((T, D), BF16),
                   jax.ShapeDtypeStruct((8, LANES), F32), jax.ShapeDtypeStruct((1, D), F32)],
        compiler_params=_cparams(40),
    )(*map(_hbm, (r, w2, x1, gf, tgt)))


def _bwd_mlp(dxb, dres, w2, r, wg1, x1, g2, dep):
    bn = wg1.shape[2]

    def body(d_ref, dres_ref, w2_ref, r_ref, w1_ref, x_ref, g_ref, dep_ref, df1_ref, dx_ref, dxb_ref, dg_ref):
        i = pl.program_id(0)

        @pl.when(i == 0)
        def _():
            dg_ref[...] = jnp.zeros(dg_ref.shape, F32)

        d = d_ref[...]
        dh = jnp.zeros((TM, D), F32)
        for j in range(N_DEV):
            cols = slice(j * bn, (j + 1) * bn)
            df1 = (2.0 * r_ref[:, cols].astype(F32) * _dot_nt(d, w2_ref[cols, :])).astype(BF16)
            df1_ref[:, cols] = df1
            dh = dh + _dot_nt(df1, w1_ref[j])
        xv = x_ref[...]
        dxn, dg = _rms_bwd(dh, xv, _rms(xv), g_ref[...])
        dx = dres_ref[...] + dxn
        dx_ref[...] = dx
        dxb_ref[...] = dx.astype(BF16)
        dg_ref[...] += dg

    once = dict(pipeline_mode=pl.Buffered(1))
    return pl.pallas_call(
        body, name="bwd_mlp", grid=(T // TM,),
        in_specs=[_rows(TM, D), _rows(TM, D), pl.BlockSpec((D_FF, D), lambda i: (0, 0), **once), _rows(TM, D_FF),
                  pl.BlockSpec(wg1.shape, lambda i: (0, 0, 0), **once), _rows(TM, D), _full((1, D)), HBM],
        out_specs=[_rows(TM, D_FF), _rows(TM, D), _rows(TM, D), _full((1, D))],
        out_shape=[jax.ShapeDtypeStruct((T, D_FF), BF16), jax.ShapeDtypeStruct((T, D), F32),
                   jax.ShapeDtypeStruct((T, D), BF16), jax.ShapeDtypeStruct((1, D), F32)],
        compiler_params=_cparams(56),
    )(*map(_hbm, (dxb, dres, w2, r, wg1, x1, g2, dep)))


def _mixer_bwd_a(dxb, wout, proj, c, lg, lb, wm, bst, clg, clb, dep):
    tm = TM_SGU
    n_tiles = T // tm

    def body(dx_ref, wo_ref, p_ref, c_ref, lg_ref, lb_ref, wm_ref, bs_ref, clg_ref, clb_ref, dep_ref,
             dpa_ref, dc_ref, dlg_ref, dlb_ref, dwm_ref, dbs_ref, dcb_ref, dclg_ref, dclb_ref,
             dv_buf, db_acc):
        i = pl.program_id(0)

        @pl.when(i == 0)
        def _():
            for ref in (dlg_ref, dlb_ref, dwm_ref, dbs_ref, dcb_ref, dclg_ref, dclb_ref, db_acc):
                ref[...] = jnp.zeros(ref.shape, F32)

        dmix = _dot_nt(dx_ref[...], wo_ref[...])
        ua = p_ref[:, 0:D_A]
        va = p_ref[:, D_A:2 * D_A]
        u, cdf_u = _gelu(ua)
        vg, cdf_v = _gelu(va)
        xhat, rstd = _ln(vg)
        v = (xhat * lg_ref[...] + lb_ref[...]).astype(BF16)
        mask = _tril_mask()
        for h in range(H_A):
            hc = slice(h * CHUNK, (h + 1) * CHUNK)
            wmh = jnp.where(mask, wm_ref[h], 0.0).astype(BF16)
            for cidx in range(tm // CHUNK):
                rc = slice(cidx * CHUNK, (cidx + 1) * CHUNK)
                vb = v[rc, hc]
                mixed = _dot(wmh, vb) + bs_ref[:, h:h + 1]
                da = dmix[rc, hc]
                dpa_ref[rc, hc] = (da * mixed * _gelu_grad(ua[rc, hc], cdf_u[rc, hc])).astype(BF16)
                dmixed = da * u[rc, hc]
                dmb = dmixed.astype(BF16)
                dv_buf[rc, hc] = _dot_tn(wmh, dmb)
                dwm_ref[h] += _dot_nt(dmb, vb)
                db_acc[:, hc] += dmixed
        dv = dv_buf[...]
        dlb_ref[...] += jnp.sum(dv, axis=0, keepdims=True)
        dlg_ref[...] += jnp.sum(dv * xhat, axis=0, keepdims=True)
        dvg = _ln_bwd(dv * lg_ref[...], xhat, rstd)
        dpa_ref[:, D_A:2 * D_A] = (dvg * _gelu_grad(va, cdf_v)).astype(BF16)

        for q in range(H_B):
            cols = slice(q * LANES, (q + 1) * LANES)
            chat, crstd = _ln(c_ref[:, cols])
            z = chat * clg_ref[:, cols] + clb_ref[:, cols]
            sg = _sigmoid(z)
            dz = dmix[:, D_A + q * LANES:D_A + (q + 1) * LANES] * (sg * (1.0 + z * (1.0 - sg)))
            dclb_ref[:, cols] += jnp.sum(dz, axis=0, keepdims=True)
            dclg_ref[:, cols] += jnp.sum(dz * chat, axis=0, keepdims=True)
            dc = _ln_bwd(dz * clg_ref[:, cols], chat, crstd)
            dc_ref[:, cols] = dc
            dcb_ref[:, cols] += jnp.sum(dc, axis=0, keepdims=True)

        @pl.when(i == n_tiles - 1)
        def _():
            for h in range(H_A):
                hc = slice(h * CHUNK, (h + 1) * CHUNK)
                dwm_ref[h] = jnp.where(mask, dwm_ref[h], 0.0)
                dbs_ref[h:h + 1, :] = jnp.sum(db_acc[:, hc].T, axis=0, keepdims=True)

    vec = _full((1, D_A))
    vshape = jax.ShapeDtypeStruct((1, D_A), F32)
    return pl.pallas_call(
        body, name="mixer_bwd_a", grid=(n_tiles,),
        in_specs=[_rows(tm, D), _full((D, D)), _rows(tm, IN_COLS), _rows(tm, D_B), vec, vec,
                  _full((H_A, CHUNK, CHUNK)), _full((CHUNK, H_A)), vec, vec, HBM],
        out_specs=[_rows(tm, 2 * D_A), _rows(tm, D_B), vec, vec, _full((H_A, CHUNK, CHUNK)),
                   _full((H_A, CHUNK)), vec, vec, vec],
        out_shape=[jax.ShapeDtypeStruct((T, 2 * D_A), BF16), jax.ShapeDtypeStruct((T, D_B), F32), vshape, vshape,
                   jax.ShapeDtypeStruct((H_A, CHUNK, CHUNK), F32), jax.ShapeDtypeStruct((H_A, CHUNK), F32),
                   vshape, vshape, vshape],
        scratch_shapes=[pltpu.VMEM((tm, D_A), F32), pltpu.VMEM((CHUNK, D_A), F32)],
        compiler_params=_cparams(32),
    )(*map(_hbm, (dxb, wout, proj, c, lg, lb, wm, bst, clg, clb, dep)))


def _mixer_bwd_b(dc, g, proj, cw, dpa, wg_in, x, g1, dres):
    tm = TM_MIX
    n_tiles = T // tm
    hb = tm // HALO
    bn = wg_in.shape[2]

    def body(dc_ref, dch_ref, g_ref, p_ref, cw_ref, dpa_ref, win_ref, x_ref, g1_ref, dres_ref,
             dp_ref, dcw_ref, dx_ref, dxb_ref, dg1_ref, dcbuf, dwacc):
        i = pl.program_id(0)

        @pl.when(i == 0)
        def _():
            dwacc[...] = jnp.zeros(dwacc.shape, F32)
            dg1_ref[...] = jnp.zeros(dg1_ref.shape, F32)

        _fill_shift_buffer(dcbuf, 0, dc_ref[...])
        _fill_shift_buffer(dcbuf, tm, jnp.where(i < n_tiles - 1, dch_ref[...], 0.0))
        _build_shifts(dcbuf)
        dp_ref[:, 0:2 * D_A] = dpa_ref[...]
        for q in range(H_B):
            cols = slice(q * LANES, (q + 1) * LANES)

            def row_block(rb, carry, q=q, cols=cols):
                base = pl.multiple_of(rb * RB, RB)
                rows = pl.ds(base, RB)
                gv = g_ref[rows, cols]
                acc = [jnp.zeros((RB, LANES), F32) for _ in range(CONV_CHAINS)]
                parts = []
                for k in range(CONV_W):
                    xk = _shifted(dcbuf, q, base, CONV_W - 1 - k)
                    acc[k % CONV_CHAINS] = acc[k % CONV_CHAINS] + xk * cw_ref[k:k + 1, cols]
                    parts.append(jnp.sum((gv * xk).reshape(RB // 8, 8, LANES), axis=0))
                dg = functools.reduce(lambda a, b: a + b, acc)
                val = p_ref[rows, cols]
                sg = _sigmoid(p_ref[rows, D_B + q * LANES:D_B + (q + 1) * LANES])
                dp_ref[rows, 2 * D_A + q * LANES:2 * D_A + (q + 1) * LANES] = (dg * sg).astype(BF16)
                dp_ref[rows, 2 * D_A + D_B + q * LANES:2 * D_A + D_B + (q + 1) * LANES] = (
                    dg * val * sg * (1.0 - sg)).astype(BF16)
                for k in range(CONV_W):
                    dwacc[k * 8:(k + 1) * 8, cols] += parts[k]
                return carry

            lax.fori_loop(0, tm // RB, row_block, 0)

        dh = jnp.zeros((tm, D), F32)
        for j in range(N_DEV):
            dh = dh + _dot_nt(dp_ref[:, j * bn:(j + 1) * bn], win_ref[j])
        xv = x_ref[...]
        dxn, dg = _rms_bwd(dh, xv, _rms(xv), g1_ref[...])
        dx = dres_ref[...] + dxn
        dx_ref[...] = dx
        dxb_ref[...] = dx.astype(BF16)
        dg1_ref[...] += dg

        @pl.when(i == n_tiles - 1)
        def _():
            for k in range(CONV_W):
                dcw_ref[k:k + 1, :] = jnp.sum(dwacc[k * 8:(k + 1) * 8, :], axis=0, keepdims=True)

    return pl.pallas_call(
        body, name="mixer_bwd_b", grid=(n_tiles,),
        in_specs=[_rows(tm, D_B),
                  pl.BlockSpec((HALO, D_B), lambda i: (jnp.minimum((i + 1) * hb, T // HALO - 1), 0)),
                  _rows(tm, D_B),
                  pl.BlockSpec((tm, 2 * D_B), lambda i: (i, 1)),
                  _full((CONV_W, D_B)), _rows(tm, 2 * D_A),
                  pl.BlockSpec(wg_in.shape, lambda i: (0, 0, 0), pipeline_mode=pl.Buffered(1)),
                  _rows(tm, D), _full((1, D)), _rows(tm, D)],
        out_specs=[_rows(tm, IN_COLS), _full((CONV_W, D_B)), _rows(tm, D), _rows(tm, D), _full((1, D))],
        out_shape=[jax.ShapeDtypeStruct((T, IN_COLS), BF16), jax.ShapeDtypeStruct((CONV_W, D_B), F32),
                   jax.ShapeDtypeStruct((T, D), F32), jax.ShapeDtypeStruct((T, D), BF16),
                   jax.ShapeDtypeStruct((1, D), F32)],
        scratch_shapes=[pltpu.VMEM((8, H_B, tm + HALO, LANES), F32), pltpu.VMEM((CONV_W * 8, D_B), F32)],
        compiler_params=_cparams(40),
    )(*map(_hbm, (dc, dc, g, proj, cw, dpa, wg_in, x, g1, dres)))


def _wgrad(name, a, g, bn, square_a=False):
    k = a.shape[1]
    n = g.shape[1]
    tk = min(k, 1024)
    tn = min(n, max(bn, 1024))
    nsub = tn // bn
    tt = TT_WGRAD
    nt = T // tt

    def body(a_ref, g_ref, o_ref, ob_ref):
        t = pl.program_id(2)

        @pl.when(t == 0)
        def _():
            o_ref[...] = jnp.zeros(o_ref.shape, F32)

        av = a_ref[...]
        if square_a:
            av = av * av
        for s in range(nsub):
            o_ref[s] += _dot_tn(av, g_ref[:, s * bn:(s + 1) * bn])

        @pl.when(t == nt - 1)
        def _():
            ob_ref[...] = o_ref[...].astype(BF16)

    ospec = pl.BlockSpec((nsub, tk, bn), lambda ki, ni, ti: (ni, ki, 0))
    return pl.pallas_call(
        body, name=name, grid=(k // tk, n // tn, nt),
        in_specs=[pl.BlockSpec((tt, tk), lambda ki, ni, ti: (ti, ki)),
                  pl.BlockSpec((tt, tn), lambda ki, ni, ti: (ti, ni))],
        out_specs=[ospec, ospec],
        out_shape=[jax.ShapeDtypeStruct((n // bn, k, bn), F32), jax.ShapeDtypeStruct((n // bn, k, bn), BF16)],
        compiler_params=_cparams(40, ("arbitrary", "arbitrary", "arbitrary")),
    )(a, g)


def _coords():
    return lax.axis_index("x"), lax.axis_index("y"), lax.axis_index("c")


def _flat(x, y, c):
    return 4 * x + 2 * y + c


def _peer(k):
    x, y, c = _coords()
    return (x ^ ((k >> 2) & 1), y ^ ((k >> 1) & 1), c ^ (k & 1))


def _hbm(a):
    return pltpu.with_memory_space_constraint(a, pltpu.HBM)


def _hbm_like(a):
    return pltpu.HBM(a.shape, a.dtype)


def _peer_sems():
    return pltpu.SemaphoreType.DMA((N_DEV - 1,))


def _place_own(shards):
    n = len(shards)
    shapes = [(s.shape if l is None else s.shape[1:]) for s, l, _ in shards]

    def body(*refs):
        ins, outs = refs[:n], refs[n:2 * n]
        stage_in, stage_out = refs[2 * n:3 * n], refs[3 * n:4 * n]
        in_sems, out_sems = refs[4 * n], refs[4 * n + 1]
        me = _flat(*_coords())
        srcs = [ins[a] if shards[a][1] is None else ins[a].at[shards[a][1]] for a in range(n)]
        loads = [pltpu.make_async_copy(srcs[a], stage_in[a], in_sems.at[a]) for a in range(n)]
        stores = [pltpu.make_async_copy(stage_out[a], outs[a].at[me], out_sems.at[a]) for a in range(n)]
        for cp in loads:
            cp.start()
        for a in range(n):
            loads[a].wait()
            stage_out[a][...] = stage_in[a][...].astype(stage_out[a].dtype)
            stores[a].start()
        for cp in stores:
            cp.wait()

    return pl.pallas_call(
        body, name="place_own", in_specs=[HBM] * n, out_specs=[HBM] * n,
        out_shape=[pltpu.HBM((N_DEV,) + shapes[a], shards[a][2]) for a in range(n)],
        scratch_shapes=[*[pltpu.VMEM(shapes[a], shards[a][0].dtype) for a in range(n)],
                        *[pltpu.VMEM(shapes[a], shards[a][2]) for a in range(n)],
                        pltpu.SemaphoreType.DMA((n,)), pltpu.SemaphoreType.DMA((n,))],
        compiler_params=pltpu.CompilerParams(vmem_limit_bytes=40 * MIB),
    )(*[_hbm(s) for s, _, _ in shards])


SIBLING = 1
CHIP_PEERS = (2, 4, 6)
FIRST_PEERS = (SIBLING,) + CHIP_PEERS


def _gather_start(lands):
    n = len(lands)

    def body(*refs):
        lnd, send, recv = refs[:n], refs[n:2 * n], refs[2 * n:3 * n]
        me = _flat(*_coords())
        for a in range(n):
            for j, k in enumerate(FIRST_PEERS):
                pltpu.make_async_remote_copy(
                    src_ref=lnd[a].at[me], dst_ref=lnd[a].at[me], send_sem=send[a].at[j],
                    recv_sem=recv[a].at[j], device_id=_peer(k), device_id_type=MESH_ID).start()

    sems = pltpu.SemaphoreType.DMA((len(FIRST_PEERS),))
    outs = pl.pallas_call(
        body, name="gather_start",
        out_shape=(*[sems] * (2 * n), *[_hbm_like(l) for l in lands]),
        in_specs=[HBM] * n, out_specs=(*[SEM] * (2 * n), *[HBM] * n),
        input_output_aliases={i: 2 * n + i for i in range(n)},
        compiler_params=pltpu.CompilerParams(has_side_effects=EFFECT),
    )(*[_hbm(l) for l in lands])
    return outs[:n], outs[n:2 * n], outs[2 * n:]


def _gather_pass_on(name, lands, sends, recvs, after):
    n = len(lands)

    def body(*refs):
        lnd, send, recv = refs[:n], refs[n:2 * n], refs[2 * n:3 * n]
        send2, recv2 = refs[3 * n + 1:4 * n + 1], refs[4 * n + 1:5 * n + 1]
        sibling = _peer(SIBLING)
        for a in range(n):
            for j, k in enumerate(CHIP_PEERS):
                cp = pltpu.make_async_remote_copy(
                    src_ref=lnd[a].at[0], dst_ref=lnd[a].at[0], send_sem=send[a].at[1 + j],
                    recv_sem=recv[a].at[1 + j], device_id=_peer(k), device_id_type=MESH_ID)
                cp.wait_send()
                cp.wait_recv()
                blk = _flat(*_peer(k))
                pltpu.make_async_remote_copy(
                    src_ref=lnd[a].at[blk], dst_ref=lnd[a].at[blk], send_sem=send2[a].at[j],
                    recv_sem=recv2[a].at[j], device_id=sibling, device_id_type=MESH_ID).start()

    sems = pltpu.SemaphoreType.DMA((len(CHIP_PEERS),))
    outs = pl.pallas_call(
        body, name=name, out_shape=(*[sems] * (2 * n), *[_hbm_like(l) for l in lands]),
        in_specs=(*[HBM] * n, *[SEM] * (2 * n), HBM), out_specs=(*[SEM] * (2 * n), *[HBM] * n),
        input_output_aliases={i: 2 * n + i for i in range(n)},
        compiler_params=pltpu.CompilerParams(has_side_effects=EFFECT),
    )(*lands, *sends, *recvs, after)
    return outs[:n], outs[n:2 * n], outs[2 * n:]


def _gather_wait(name, lands, sends, recvs, sends2, recvs2, after):
    n = len(lands)

    def body(*refs):
        lnd, send, recv = refs[:n], refs[n:2 * n], refs[2 * n:3 * n]
        send2, recv2 = refs[3 * n:4 * n], refs[4 * n:5 * n]
        sibling = _peer(SIBLING)
        for a in range(n):
            own = pltpu.make_async_remote_copy(
                src_ref=lnd[a].at[0], dst_ref=lnd[a].at[0], send_sem=send[a].at[0], recv_sem=recv[a].at[0],
                device_id=sibling, device_id_type=MESH_ID)
            own.wait_send()
            own.wait_recv()
            for j in range(len(CHIP_PEERS)):
                cp = pltpu.make_async_remote_copy(
                    src_ref=lnd[a].at[0], dst_ref=lnd[a].at[0], send_sem=send2[a].at[j], recv_sem=recv2[a].at[j],
                    device_id=sibling, device_id_type=MESH_ID)
                cp.wait_send()
                cp.wait_recv()

    return pl.pallas_call(
        body, name=name, out_shape=[_hbm_like(l) for l in lands],
        in_specs=(*[HBM] * n, *[SEM] * (4 * n), HBM), out_specs=[HBM] * n,
        input_output_aliases={i: i for i in range(n)},
        compiler_params=pltpu.CompilerParams(has_side_effects=EFFECT),
    )(*lands, *sends, *recvs, *sends2, *recvs2, after)


def _scatter_start(name, grads, lands, layer):
    n = len(grads)

    def body(*refs):
        g, lnd, send, recv = refs[:n], refs[n:2 * n], refs[2 * n:3 * n], refs[3 * n:4 * n]
        for a in range(n):
            for k in range(1, N_DEV):
                to = _peer(k)
                pltpu.make_async_remote_copy(
                    src_ref=g[a].at[_flat(*to)], dst_ref=lnd[a].at[layer, k - 1], send_sem=send[a].at[k - 1],
                    recv_sem=recv[a].at[k - 1], device_id=to, device_id_type=MESH_ID).start()

    outs = pl.pallas_call(
        body, name=name,
        out_shape=(*[_peer_sems()] * (2 * n), *[_hbm_like(g) for g in grads], *[_hbm_like(l) for l in lands]),
        in_specs=[HBM] * (2 * n), out_specs=(*[SEM] * (2 * n), *[HBM] * (2 * n)),
        input_output_aliases={i: 2 * n + i for i in range(2 * n)},
        compiler_params=pltpu.CompilerParams(has_side_effects=EFFECT),
    )(*[_hbm(g) for g in grads], *[_hbm(l) for l in lands])
    return outs[:n], outs[n:2 * n], outs[2 * n:3 * n], outs[3 * n:]


def _scatter_wait(name, grads, lands, sends, recvs, after):
    n, nw = len(grads), len(lands)

    def body(*refs):
        g, lnd = refs[:n], refs[n:n + nw]
        send, recv = refs[n + nw:2 * n + nw], refs[2 * n + nw:3 * n + nw]
        for a in range(n):
            for k in range(1, N_DEV):
                cp = pltpu.make_async_remote_copy(
                    src_ref=g[a].at[0], dst_ref=lnd[a // DEPTH].at[a % DEPTH, 0],
                    send_sem=send[a].at[k - 1], recv_sem=recv[a].at[k - 1],
                    device_id=_peer(k), device_id_type=MESH_ID)
                cp.wait_send()
                cp.wait_recv()

    outs = pl.pallas_call(
        body, name=name,
        out_shape=(*[_hbm_like(g) for g in grads], *[_hbm_like(l) for l in lands]),
        in_specs=(*[HBM] * (n + nw), *[SEM] * (2 * n), HBM), out_specs=[HBM] * (n + nw),
        input_output_aliases={i: i for i in range(n + nw)},
        compiler_params=pltpu.CompilerParams(has_side_effects=EFFECT),
    )(*grads, *lands, *sends, *recvs, after)
    return outs[n:]


def _all_reduce_small(part, deps):
    rows = part.shape[0]
    br = rows // N_DEV
    assert br * N_DEV == rows and br % 8 == 0
    nd = len(deps)

    def body(p_ref, *refs):
        o_ref, slots, send1, recv1, send2, recv2 = refs[nd:]
        me = _flat(*_coords())

        def block(ref, d):
            return ref.at[pl.ds(pl.multiple_of(d * br, 8), br), :]

        slots[me] = p_ref[pl.ds(pl.multiple_of(me * br, 8), br), :]
        scatter, gather = [], []
        for k in range(1, N_DEV):
            to = _peer(k)
            scatter.append(pltpu.make_async_remote_copy(
                src_ref=block(p_ref, _flat(*to)), dst_ref=slots.at[me],
                send_sem=send1.at[k - 1], recv_sem=recv1.at[k - 1], device_id=to, device_id_type=MESH_ID))
            gather.append(pltpu.make_async_remote_copy(
                src_ref=block(o_ref, me), dst_ref=block(o_ref, me),
                send_sem=send2.at[k - 1], recv_sem=recv2.at[k - 1], device_id=to, device_id_type=MESH_ID))
        for cp in scatter:
            cp.start()
        for cp in scatter:
            cp.wait()
        acc = slots[0]
        for d in range(1, N_DEV):
            acc = acc + slots[d]
        o_ref[pl.ds(pl.multiple_of(me * br, 8), br), :] = acc
        for cp in gather:
            cp.start()
        for cp in gather:
            cp.wait()

    return pl.pallas_call(
        body, name="all_reduce_small",
        in_specs=[pl.BlockSpec(memory_space=pltpu.VMEM)] + [HBM] * nd, out_specs=pl.BlockSpec(memory_space=pltpu.VMEM),
        out_shape=jax.ShapeDtypeStruct(part.shape, F32),
        scratch_shapes=[pltpu.VMEM((N_DEV, br, LANES), F32)] + [pltpu.SemaphoreType.DMA((N_DEV - 1,))] * 4,
        compiler_params=pltpu.CompilerParams(vmem_limit_bytes=24 * MIB),
    )(part, *map(_hbm, deps))


def _adam_math(w, g, m, v):
    m = ADAM_B1 * m + (1.0 - ADAM_B1) * g
    v = ADAM_B2 * v + (1.0 - ADAM_B2) * (g * g)
    m_hat = m / (1.0 - ADAM_B1 ** ADAM_STEP)
    v_hat = v / (1.0 - ADAM_B2 ** ADAM_STEP)
    delta = -ADAM_LR * (m_hat / (jnp.sqrt(v_hat) + ADAM_EPS) + ADAM_WD * w)
    return delta, m, v


def _adam_sharded(name, me, g32, recv, w, m, v):
    _, r, c = w.shape
    tr = min(r, 256)
    nblk = r // tr
    assert len(g32) == DEPTH == 2

    def body(me_ref, g0_ref, g1_ref, recv_ref, w_ref, m_ref, v_ref, g_ref, d_ref, nm_ref, nv_ref):
        g = jnp.where(pl.program_id(0) == 0, g0_ref[...], g1_ref[...])
        for k in range(N_DEV - 1):
            g = g + recv_ref[k].astype(F32)
        delta, nm, nv = _adam_math(w_ref[...], g, m_ref[...], v_ref[...])
        g_ref[...] = g
        d_ref[...] = delta
        nm_ref[...] = nm
        nv_ref[...] = nv

    blk = pl.BlockSpec((None, tr, c), lambda l, i, me_ref: (l, i, 0))
    own0 = pl.BlockSpec((None, tr, c), lambda l, i, me_ref: (me_ref[0], jnp.where(l == 0, i, nblk - 1), 0))
    own1 = pl.BlockSpec((None, tr, c), lambda l, i, me_ref: (me_ref[0], jnp.where(l == 1, i, 0), 0))
    shp = jax.ShapeDtypeStruct(w.shape, F32)
    return pl.pallas_call(
        body, name=name,
        grid_spec=pltpu.PrefetchScalarGridSpec(
            num_scalar_prefetch=1, grid=(DEPTH, nblk),
            in_specs=[own0, own1, pl.BlockSpec((None, N_DEV - 1, tr, c), lambda l, i, me_ref: (l, 0, i, 0)),
                      blk, blk, blk],
            out_specs=[blk] * 4),
        out_shape=[shp] * 4,
        compiler_params=_cparams(32, ("arbitrary", "arbitrary")),
    )(me, *[_hbm(a) for a in (*g32, recv, w, m, v)])


def _adam_small(gs, ws, ms, vs):
    n = len(gs)

    def body(*refs):
        g, w, m, v = refs[:n], refs[n:2 * n], refs[2 * n:3 * n], refs[3 * n:4 * n]
        d, nm, nv = refs[4 * n:5 * n], refs[5 * n:6 * n], refs[6 * n:7 * n]
        for a in range(n):
            delta, new_m, new_v = _adam_math(w[a][...], g[a][...], m[a][...], v[a][...])
            d[a][...] = delta
            nm[a][...] = new_m
            nv[a][...] = new_v

    specs = [_full(a.shape) for a in gs]
    shapes = [jax.ShapeDtypeStruct(a.shape, F32) for a in gs]
    outs = pl.pallas_call(
        body, name="adam_small", grid=(1,),
        in_specs=specs * 4, out_specs=specs * 3, out_shape=shapes * 3,
        compiler_params=_cparams(24),
    )(*gs, *ws, *ms, *vs)
    return outs[:n], outs[n:2 * n], outs[2 * n:]


SMALL = ["norm1_g", "sgu_ln_g", "sgu_ln_b", "sgu_w", "sgu_b", "conv_b", "conv_ln_g", "conv_ln_b", "norm2_g",
         "final_g"]


def _pack(arrays):
    flat = jnp.concatenate([a.reshape(-1) for a in arrays])
    pad = (-flat.shape[0]) % (N_DEV * 8 * LANES)
    return jnp.pad(flat, (0, pad)).reshape(-1, LANES)


def _unpack(packed, shapes):
    flat = packed.reshape(-1)
    out, off = [], 0
    for s in shapes:
        size = 1
        for d in s:
            size *= d
        out.append(flat[off:off + size].reshape(s))
        off += size
    return out


def kernel(x, norm1_g, w_in, sgu_ln_g, sgu_ln_b, sgu_w, sgu_b, conv_w, conv_b, conv_ln_g, conv_ln_b, w_out, norm2_g, w_ff1, w_ff2, final_g, loss_target, m_norm1_g, m_w_in, m_sgu_ln_g, m_sgu_ln_b, m_sgu_w, m_sgu_b, m_conv_w, m_conv_b, m_conv_ln_g, m_conv_ln_b, m_w_out, m_norm2_g, m_w_ff1, m_w_ff2, m_final_g, v_norm1_g, v_w_in, v_sgu_ln_g, v_sgu_ln_b, v_sgu_w, v_sgu_b, v_conv_w, v_conv_b, v_conv_ln_g, v_conv_ln_b, v_w_out, v_norm2_g, v_w_ff1, v_w_ff2, v_final_g):
    x2d = x.reshape(T, D)
    tgt = loss_target.reshape(T, D)
    cw_shard = conv_w.reshape(CONV_W, LANES)

    gnames, shards = [], []
    for l in range(DEPTH):
        for k, w in (("w_in", w_in), ("w_out", w_out), ("w_ff1", w_ff1), ("w_ff2", w_ff2)):
            gnames.append(f"{k}{l}")
            shards.append((w, l, BF16))
        if l == 0:
            gnames.insert(1, "conv_w")
            shards.insert(1, (cw_shard, None, F32))
    sends, recvs, lands = _gather_start(_place_own(shards))
    gidx = {k: i for i, k in enumerate(gnames)}

    passed = {}

    def pass_on(ks, after):
        idx = [gidx[k] for k in ks]
        s2, r2, ld = _gather_pass_on("gather_pass_on_" + "_".join(ks), [lands[i] for i in idx],
                                     [sends[i] for i in idx], [recvs[i] for i in idx], after)
        for j, k in enumerate(ks):
            passed[k] = (s2[j], r2[j], ld[j])

    def gathered(ks, after):
        idx = [gidx[k] for k in ks]
        return _gather_wait("gather_wait_" + "_".join(ks), [passed[k][2] for k in ks], [sends[i] for i in idx],
                            [recvs[i] for i in idx], [passed[k][0] for k in ks], [passed[k][1] for k in ks], after)

    saved = []
    xl = x2d
    cw_full = None
    for l in range(DEPTH):
        g1 = norm1_g[l].reshape(1, D)
        g2 = norm2_g[l].reshape(1, D)
        lg, lb = sgu_ln_g[l].reshape(1, D_A), sgu_ln_b[l].reshape(1, D_A)
        bst = sgu_b[l].T
        cb = conv_b[l].reshape(1, D_B)
        clg, clb = conv_ln_g[l].reshape(1, D_B), conv_ln_b[l].reshape(1, D_B)
        first = [f"w_in{l}"] if l == 0 else [f"w_in{l}", f"w_out{l}"]
        pass_on(first, xl)
        wg_in, *rest = gathered(first, xl)
        h, proj = _fwd_in(xl, g1, wg_in)
        if l == 0:
            pass_on(["conv_w", "w_out0"], h)
            cw_g, wout = gathered(["conv_w", "w_out0"], proj)
            cw_full = cw_g.reshape(N_DEV, DEPTH, CONV_W, D_B // N_DEV).transpose(1, 2, 0, 3).reshape(
                DEPTH, CONV_W, D_B)
        else:
            (wout,) = rest
            pass_on([f"w_ff1{l}"], proj)
        wout = wout.reshape(D, D)
        mix, c, g, x1, h2 = _mixer_fwd(proj, lg, lb, sgu_w[l], bst, cw_full[l], cb, clg, clb, wout, xl, g2)
        pass_on([f"w_ff1{l}"] if l == 0 else [f"w_ff2{l}"], h2)
        (wg_ff1,) = gathered([f"w_ff1{l}"], h2)
        r = _fwd_ff1(h2, wg_ff1)
        if l == 0:
            pass_on(["w_ff20"], r)
        (w2,) = gathered([f"w_ff2{l}"], r)
        w2 = w2.reshape(D_FF, D)
        saved.append(dict(x=xl, h=h, proj=proj, mix=mix, c=c, g=g, x1=x1, h2=h2, r=r, wg_in=wg_in, wout=wout,
                          wg_ff1=wg_ff1, w2=w2, g1=g1, g2=g2, lg=lg, lb=lb, bst=bst, clg=clg, clb=clb))
        if l + 1 < DEPTH:
            xl = _fwd_ff2(r, w2, x1)
        else:
            dx, dxb, loss_part, d_final_g = _fwd_ff2_loss(r, w2, x1, final_g.reshape(1, D), tgt)

    names = ["w_in", "w_out", "w_ff1", "w_ff2"]
    block = dict(w_in=(D, IN_COLS // N_DEV), w_out=(D // N_DEV, D), w_ff1=(D, D_FF // N_DEV), w_ff2=(D_FF // N_DEV, D))
    land = {k: lax.empty((DEPTH, N_DEV - 1) + block[k], BF16) for k in names}
    big32 = {k: [None] * DEPTH for k in names}
    big16 = {k: [None] * DEPTH for k in names}
    ssend = {k: [None] * DEPTH for k in names}
    srecv = {k: [None] * DEPTH for k in names}

    def send_grads(l, grads):
        ks = list(grads)
        for k in ks:
            big32[k][l] = grads[k][0].reshape((N_DEV,) + block[k])
        sends, recvs, g16s, lands = _scatter_start(
            "scatter_start_" + "_".join(ks) + str(l), [grads[k][1].reshape((N_DEV,) + block[k]) for k in ks],
            [land[k] for k in ks], l)
        for i, k in enumerate(ks):
            ssend[k][l], srecv[k][l], big16[k][l], land[k] = sends[i], recvs[i], g16s[i], lands[i]

    small = {}
    d_conv_w = [None] * DEPTH
    for l in reversed(range(DEPTH)):
        s = saved[l]
        send_grads(l, dict(w_ff2=_wgrad("wgrad_ff2", s["r"], dxb, D, square_a=True)))
        df1, dx1, dx1b, dg2 = _bwd_mlp(dxb, dx, s["w2"], s["r"], s["wg_ff1"], s["x1"], s["g2"], big16["w_ff2"][l])
        send_grads(l, dict(w_ff1=_wgrad("wgrad_ff1", s["h2"], df1, D_FF // N_DEV),
                           w_out=_wgrad("wgrad_out", s["mix"], dx1b, D)))
        dpa, dc, dlg, dlb, dwm, dbs, dcb, dclg, dclb = _mixer_bwd_a(
            dx1b, s["wout"], s["proj"], s["c"], s["lg"], s["lb"], sgu_w[l], s["bst"], s["clg"], s["clb"],
            big16["w_out"][l])
        dproj, d_conv_w[l], dx, dxb, dg1 = _mixer_bwd_b(dc, s["g"], s["proj"], cw_full[l], dpa, s["wg_in"], s["x"],
                                                         s["g1"], dx1)
        send_grads(l, dict(w_in=_wgrad("wgrad_in", s["h"], dproj, IN_COLS // N_DEV)))
        small[l] = dict(norm1_g=dg1, sgu_ln_g=dlg, sgu_ln_b=dlb, sgu_w=dwm, sgu_b=dbs, conv_b=dcb, conv_ln_g=dclg,
                        conv_ln_b=dclb, norm2_g=dg2)
    grad_x = dx.reshape(1, T, D)

    me_block = _flat(*_coords()).astype(jnp.int32).reshape(1)
    wmv = dict(w_in=(w_in, m_w_in, v_w_in), w_out=(w_out, m_w_out, v_w_out), w_ff1=(w_ff1, m_w_ff1, v_w_ff1),
               w_ff2=(w_ff2, m_w_ff2, v_w_ff2))
    res = {}

    def update(tag, ks, after):
        recv = _scatter_wait("scatter_wait_" + tag, [g for k in ks for g in big16[k]], [land[k] for k in ks],
                             [q for k in ks for q in ssend[k]], [q for k in ks for q in srecv[k]], after)
        for k, rk in zip(ks, recv):
            res[k] = _adam_sharded("adam_" + k, me_block, big32[k], rk, *wmv[k])

    update("early", ["w_ff2", "w_ff1", "w_out"], big16["w_in"][0])

    rep = dict(norm1_g=(norm1_g, m_norm1_g, v_norm1_g), sgu_ln_g=(sgu_ln_g, m_sgu_ln_g, v_sgu_ln_g),
               sgu_ln_b=(sgu_ln_b, m_sgu_ln_b, v_sgu_ln_b), sgu_w=(sgu_w, m_sgu_w, v_sgu_w),
               sgu_b=(sgu_b, m_sgu_b, v_sgu_b), conv_b=(conv_b, m_conv_b, v_conv_b),
               conv_ln_g=(conv_ln_g, m_conv_ln_g, v_conv_ln_g), conv_ln_b=(conv_ln_b, m_conv_ln_b, v_conv_ln_b),
               norm2_g=(norm2_g, m_norm2_g, v_norm2_g), final_g=(final_g, m_final_g, v_final_g))
    parts = []
    for k in SMALL:
        if k == "final_g":
            parts.append(d_final_g.reshape(rep[k][0].shape))
        else:
            parts.append(jnp.stack([small[l][k].reshape(rep[k][0].shape[1:]) for l in range(DEPTH)]))
    parts.append(jnp.stack(d_conv_w))
    parts.append(loss_part[0, 0:1])
    shapes = [p.shape for p in parts]
    packed_sum = _all_reduce_small(_pack(parts), [res[k][0] for k in ("w_ff2", "w_ff1", "w_out")])
    update("late", ["w_in"], packed_sum)
    summed = _unpack(packed_sum, shapes)
    loss = summed[-1][0]
    me = _flat(*_coords())
    g_conv_w = lax.dynamic_slice_in_dim(summed[-2], me * (D_B // N_DEV), D_B // N_DEV, axis=2)
    g_small = summed[:-2] + [g_conv_w]
    keys = SMALL + ["conv_w"]
    rep["conv_w"] = (conv_w, m_conv_w, v_conv_w)
    def two_d(a):
        return a.reshape(1, -1) if a.ndim == 1 else a

    d_s, nm_s, nv_s = _adam_small([two_d(g) for g in g_small], *[[two_d(rep[k][i]) for k in keys] for i in range(3)])
    for i, k in enumerate(keys):
        shape = rep[k][0].shape
        res[k] = (g_small[i], d_s[i].reshape(shape), nm_s[i].reshape(shape), nv_s[i].reshape(shape))

    order = ["norm1_g", "w_in", "sgu_ln_g", "sgu_ln_b", "sgu_w", "sgu_b", "conv_w", "conv_b", "conv_ln_g",
             "conv_ln_b", "w_out", "norm2_g", "w_ff1", "w_ff2", "final_g"]
    return (loss, grad_x, *[res[k][0] for k in order], *[res[k][1] for k in order],
            *[res[k][2] for k in order], *[res[k][3] for k in order])
```

```python
import functools

import jax
import jax.numpy as jnp
from jax import lax
from jax.experimental import pallas as pl
from jax.experimental.pallas import tpu as pltpu

F32 = jnp.float32
BF16 = jnp.bfloat16

N_DEV = 8
DEPTH = 2
T = 4096
D = 1024
D_A = 512
D_B = 512
CHUNK = 128
H_A = 4
H_B = 4
CONV_W = 31
HALO = 32
D_FF = 4096
IN_COLS = 2048
EPS = 1e-6

ADAM_LR = 0.001
ADAM_B1 = 0.9
ADAM_B2 = 0.999
ADAM_EPS = 1e-08
ADAM_WD = 0.01
ADAM_STEP = 10

TM = 512
TM_FWD = 1024
TT_WGRAD = 4096
TM_SGU = 512
RB = 64
CONV_CHAINS = 2
LANES = 128
MIB = 1024 * 1024
SCOPED_VMEM_MIB = 60

SQRT_HALF = 0.7071067811865476
INV_SQRT_2PI = 0.3989422804014327

MESH_ID = pl.DeviceIdType.MESH
HBM = pl.BlockSpec(memory_space=pltpu.HBM)
SEM = pl.BlockSpec(memory_space=pltpu.SEMAPHORE)
EFFECT = pltpu.SideEffectType.DATAFLOW_SIDE_EFFECTING


def _cparams(vmem_mib, sem=("arbitrary",)):
    assert vmem_mib <= SCOPED_VMEM_MIB
    return pltpu.CompilerParams(dimension_semantics=sem, vmem_limit_bytes=SCOPED_VMEM_MIB * MIB)


def _full(shape):
    return pl.BlockSpec(shape, lambda *_: (0,) * len(shape))


def _rows(tm, cols):
    return pl.BlockSpec((tm, cols), lambda i: (i, 0))


def _gelu(x):
    cdf = 0.5 * (1.0 + lax.erf(x * SQRT_HALF))
    return x * cdf, cdf


def _gelu_grad(x, cdf):
    return cdf + x * (INV_SQRT_2PI * jnp.exp(-0.5 * x * x))


def _sigmoid(x):
    return 1.0 / (1.0 + jnp.exp(-x))


def _ln(x):
    mu = jnp.mean(x, axis=-1, keepdims=True)
    xc = x - mu
    rstd = lax.rsqrt(jnp.mean(xc * xc, axis=-1, keepdims=True) + EPS)
    return xc * rstd, rstd


def _ln_bwd(dyh, xhat, rstd):
    return rstd * (dyh - jnp.mean(dyh, axis=-1, keepdims=True) - xhat * jnp.mean(dyh * xhat, axis=-1, keepdims=True))


def _rms(x):
    return lax.rsqrt(jnp.mean(x * x, axis=-1, keepdims=True) + EPS)


def _rms_bwd(dh, x, r, g):
    n = x * r
    dn = dh * g
    dx = r * (dn - n * jnp.mean(dn * n, axis=-1, keepdims=True))
    return dx, jnp.sum(dh * n, axis=0, keepdims=True)


def _dot(a, b):
    return jnp.dot(a, b, preferred_element_type=F32)


def _dot_nt(a, b):
    return lax.dot_general(a, b, (((1,), (1,)), ((), ())), preferred_element_type=F32)


def _dot_tn(a, b):
    return lax.dot_general(a, b, (((0,), (0,)), ((), ())), preferred_element_type=F32)


def _tril_mask():
    r = lax.broadcasted_iota(jnp.int32, (CHUNK, CHUNK), 0)
    c = lax.broadcasted_iota(jnp.int32, (CHUNK, CHUNK), 1)
    return r >= c


def _fwd_in(x, g1, wg):
    bn = wg.shape[2]

    def body(x_ref, g_ref, w_ref, h_ref, p_ref):
        xv = x_ref[...]
        h = (xv * _rms(xv) * g_ref[...]).astype(BF16)
        h_ref[...] = h
        for j in range(N_DEV):
            p_ref[:, j * bn:(j + 1) * bn] = _dot(h, w_ref[j])

    return pl.pallas_call(
        body, name="fwd_in", grid=(T // TM_FWD,),
        in_specs=[_rows(TM_FWD, D), _full((1, D)), _full(wg.shape)],
        out_specs=[_rows(TM_FWD, D), _rows(TM_FWD, IN_COLS)],
        out_shape=[jax.ShapeDtypeStruct((T, D), BF16), jax.ShapeDtypeStruct((T, IN_COLS), F32)],
        compiler_params=_cparams(32),
    )(*map(_hbm, (x, g1, wg)))


def _fill_shift_buffer(sh_ref, row0, value):
    for q in range(sh_ref.shape[1]):
        sh_ref[0, q, row0:row0 + value.shape[0], :] = value[:, q * LANES:(q + 1) * LANES]


def _build_shifts(sh_ref):
    rows = sh_ref.shape[2]
    for p in range(1, 8):
        for q in range(sh_ref.shape[1]):
            sh_ref[p, q, 0:rows - 8, :] = sh_ref[0, q, p:p + rows - 8, :]


def _shifted(sh_ref, q, base, off):
    start = base + (off - off % 8)
    if not isinstance(start, int):
        start = pl.multiple_of(start, 8)
    return sh_ref[off % 8, q, pl.ds(start, RB), :]


def _conv_taps(sh_ref, w_ref, q, base, first_tap_row, step):
    cols = slice(q * LANES, (q + 1) * LANES)
    acc = [jnp.zeros((RB, LANES), F32) for _ in range(CONV_CHAINS)]
    for k in range(CONV_W):
        term = _shifted(sh_ref, q, base, first_tap_row + step * k) * w_ref[k:k + 1, cols]
        acc[k % CONV_CHAINS] = acc[k % CONV_CHAINS] + term
    return functools.reduce(lambda a, b: a + b, acc)


def _mixer_fwd(proj, lg, lb, wm, bst, cw, cb, clg, clb, wout, x, g2):
    tm = TM_SGU
    hb = tm // HALO

    def body(p_ref, ph_ref, lg_ref, lb_ref, wm_ref, bs_ref, cw_ref, cb_ref, clg_ref, clb_ref, wo_ref, x_ref, g2_ref,
             mix_ref, c_ref, g_ref, x1_ref, h2_ref, gbuf):
        i = pl.program_id(0)
        u, _ = _gelu(p_ref[:, 0:D_A])
        vg, _ = _gelu(p_ref[:, D_A:2 * D_A])
        xhat, _ = _ln(vg)
        v = (xhat * lg_ref[...] + lb_ref[...]).astype(BF16)
        mask = _tril_mask()
        for h in range(H_A):
            hc = slice(h * CHUNK, (h + 1) * CHUNK)
            wmh = jnp.where(mask, wm_ref[h], 0.0).astype(BF16)
            for c in range(tm // CHUNK):
                rc = slice(c * CHUNK, (c + 1) * CHUNK)
                mixed = _dot(wmh, v[rc, hc]) + bs_ref[:, h:h + 1]
                mix_ref[rc, hc] = (u[rc, hc] * mixed).astype(BF16)
        x1_ref[...] = x_ref[...] + _dot(mix_ref[:, 0:D_A], wo_ref[0:D_A, :])

        g = p_ref[:, 2 * D_A:2 * D_A + D_B] * _sigmoid(p_ref[:, 2 * D_A + D_B:IN_COLS])
        g_ref[...] = g
        gh = ph_ref[:, 0:D_B] * _sigmoid(ph_ref[:, D_B:2 * D_B])
        _fill_shift_buffer(gbuf, 0, jnp.where(i > 0, gh, 0.0))
        _fill_shift_buffer(gbuf, HALO, g)
        _build_shifts(gbuf)
        for q in range(H_B):
            cols = slice(q * LANES, (q + 1) * LANES)
            for rb in range(tm // RB):
                acc = _conv_taps(gbuf, cw_ref, q, rb * RB, HALO - (CONV_W - 1), 1)
                c_ref[rb * RB:(rb + 1) * RB, cols] = acc + cb_ref[:, cols]
        for q in range(H_B):
            cols = slice(q * LANES, (q + 1) * LANES)
            chat, _ = _ln(c_ref[:, cols])
            z = chat * clg_ref[:, cols] + clb_ref[:, cols]
            mix_ref[:, D_A + q * LANES:D_A + (q + 1) * LANES] = (z * _sigmoid(z)).astype(BF16)

        x1 = x1_ref[...] + _dot(mix_ref[:, D_A:D], wo_ref[D_A:D, :])
        x1_ref[...] = x1
        h2_ref[...] = (x1 * _rms(x1) * g2_ref[...]).astype(BF16)

    vec = _full((1, D_A))
    return pl.pallas_call(
        body, name="mixer_fwd", grid=(T // tm,),
        in_specs=[_rows(tm, IN_COLS),
                  pl.BlockSpec((HALO, 2 * D_B), lambda i: (jnp.maximum(i * hb - 1, 0), 1)),
                  vec, vec, _full((H_A, CHUNK, CHUNK)), _full((CHUNK, H_A)),
                  _full((CONV_W, D_B)), vec, vec, vec,
                  pl.BlockSpec((D, D), lambda i: (0, 0), pipeline_mode=pl.Buffered(1)), _rows(tm, D), _full((1, D))],
        out_specs=[_rows(tm, D), _rows(tm, D_B), _rows(tm, D_B), _rows(tm, D), _rows(tm, D)],
        out_shape=[jax.ShapeDtypeStruct((T, D), BF16), jax.ShapeDtypeStruct((T, D_B), F32),
                   jax.ShapeDtypeStruct((T, D_B), F32), jax.ShapeDtypeStruct((T, D), F32),
                   jax.ShapeDtypeStruct((T, D), BF16)],
        scratch_shapes=[pltpu.VMEM((8, H_B, HALO + tm, LANES), F32)],
        compiler_params=_cparams(40),
    )(*map(_hbm, (proj, proj, lg, lb, wm, bst, cw, cb, clg, clb, wout, x, g2)))


def _fwd_ff1(h2, wg):
    bn = wg.shape[2]

    def body(h_ref, w_ref, r_ref):
        h = h_ref[...]
        for j in range(N_DEV):
            r_ref[:, j * bn:(j + 1) * bn] = jnp.maximum(_dot(h, w_ref[j]), 0.0).astype(BF16)

    return pl.pallas_call(
        body, name="fwd_ff1", grid=(T // TM_FWD,),
        in_specs=[_rows(TM_FWD, D), _full(wg.shape)],
        out_specs=_rows(TM_FWD, D_FF),
        out_shape=jax.ShapeDtypeStruct((T, D_FF), BF16),
        compiler_params=_cparams(48),
    )(h2, wg)


def _fwd_ff2(r, w2, x1):
    def body(r_ref, w_ref, x_ref, o_ref):
        rv = r_ref[...]
        o_ref[...] = x_ref[...] + _dot(rv * rv, w_ref[...])

    return pl.pallas_call(
        body, name="fwd_ff2", grid=(T // TM_FWD,),
        in_specs=[_rows(TM_FWD, D_FF), _full((D_FF, D)), _rows(TM_FWD, D)],
        out_specs=_rows(TM_FWD, D),
        out_shape=jax.ShapeDtypeStruct((T, D), F32),
        compiler_params=_cparams(48),
    )(r, w2, x1)


def _fwd_ff2_loss(r, w2, x1, gf, tgt):
    def body(r_ref, w_ref, x_ref, g_ref, t_ref, dx_ref, dxb_ref, loss_ref, dg_ref):
        i = pl.program_id(0)

        @pl.when(i == 0)
        def _():
            loss_ref[...] = jnp.zeros(loss_ref.shape, F32)
            dg_ref[...] = jnp.zeros(dg_ref.shape, F32)

        rv = r_ref[...]
        xv = x_ref[...] + _dot(rv * rv, w_ref[...])
        rn = _rms(xv)
        diff = xv * rn * g_ref[...] - t_ref[...]
        loss_ref[...] += 0.5 * jnp.sum(jnp.mean(diff * diff, axis=-1, keepdims=True), axis=0, keepdims=True)
        dx, dg = _rms_bwd(diff * (1.0 / D), xv, rn, g_ref[...])
        dx_ref[...] = dx
        dxb_ref[...] = dx.astype(BF16)
        dg_ref[...] += dg

    return pl.pallas_call(
        body, name="fwd_ff2_loss", grid=(T // TM,),
        in_specs=[_rows(TM, D_FF), pl.BlockSpec((D_FF, D), lambda i: (0, 0), pipeline_mode=pl.Buffered(1)),
                  _rows(TM, D), _full((1, D)), _rows(TM, D)],
        out_specs=[_rows(TM, D), _rows(TM, D), _full((8, LANES)), _full((1, D))],
        out_shape=[jax.ShapeDtypeStruct((T, D), F32), jax.ShapeDtypeStruct((T, D), BF16),
                   jax.ShapeDtypeStruct((8, LANES), F32), jax.ShapeDtypeStruct((1, D), F32)],
        compiler_params=_cparams(40),
    )(*map(_hbm, (r, w2, x1, gf, tgt)))


def _bwd_mlp(dxb, dres, w2, r, wg1, x1, g2, dep):
    bn = wg1.shape[2]

    def body(d_ref, dres_ref, w2_ref, r_ref, w1_ref, x_ref, g_ref, dep_ref, df1_ref, dx_ref, dxb_ref, dg_ref):
        i = pl.program_id(0)

        @pl.when(i == 0)
        def _():
            dg_ref[...] = jnp.zeros(dg_ref.shape, F32)

        d = d_ref[...]
        dh = jnp.zeros((TM, D), F32)
        for j in range(N_DEV):
            cols = slice(j * bn, (j + 1) * bn)
            df1 = (2.0 * r_ref[:, cols].astype(F32) * _dot_nt(d, w2_ref[cols, :])).astype(BF16)
            df1_ref[:, cols] = df1
            dh = dh + _dot_nt(df1, w1_ref[j])
        xv = x_ref[...]
        dxn, dg = _rms_bwd(dh, xv, _rms(xv), g_ref[...])
        dx = dres_ref[...] + dxn
        dx_ref[...] = dx
        dxb_ref[...] = dx.astype(BF16)
        dg_ref[...] += dg

    once = dict(pipeline_mode=pl.Buffered(1))
    return pl.pallas_call(
        body, name="bwd_mlp", grid=(T // TM,),
        in_specs=[_rows(TM, D), _rows(TM, D), pl.BlockSpec((D_FF, D), lambda i: (0, 0), **once), _rows(TM, D_FF),
                  pl.BlockSpec(wg1.shape, lambda i: (0, 0, 0), **once), _rows(TM, D), _full((1, D)), HBM],
        out_specs=[_rows(TM, D_FF), _rows(TM, D), _rows(TM, D), _full((1, D))],
        out_shape=[jax.ShapeDtypeStruct((T, D_FF), BF16), jax.ShapeDtypeStruct((T, D), F32),
                   jax.ShapeDtypeStruct((T, D), BF16), jax.ShapeDtypeStruct((1, D), F32)],
        compiler_params=_cparams(56),
    )(*map(_hbm, (dxb, dres, w2, r, wg1, x1, g2, dep)))


def _mixer_bwd_a(dxb, wout, proj, c, lg, lb, wm, bst, clg, clb, dep):
    tm = TM_SGU
    n_tiles = T // tm

    def body(dx_ref, wo_ref, p_ref, c_ref, lg_ref, lb_ref, wm_ref, bs_ref, clg_ref, clb_ref, dep_ref,
             dpa_ref, dc_ref, dlg_ref, dlb_ref, dwm_ref, dbs_ref, dcb_ref, dclg_ref, dclb_ref,
             dv_buf, db_acc):
        i = pl.program_id(0)

        @pl.when(i == 0)
        def _():
            for ref in (dlg_ref, dlb_ref, dwm_ref, dbs_ref, dcb_ref, dclg_ref, dclb_ref, db_acc):
                ref[...] = jnp.zeros(ref.shape, F32)

        dmix = _dot_nt(dx_ref[...], wo_ref[...])
        ua = p_ref[:, 0:D_A]
        va = p_ref[:, D_A:2 * D_A]
        u, cdf_u = _gelu(ua)
        vg, cdf_v = _gelu(va)
        xhat, rstd = _ln(vg)
        v = (xhat * lg_ref[...] + lb_ref[...]).astype(BF16)
        mask = _tril_mask()
        for h in range(H_A):
            hc = slice(h * CHUNK, (h + 1) * CHUNK)
            wmh = jnp.where(mask, wm_ref[h], 0.0).astype(BF16)
            for cidx in range(tm // CHUNK):
                rc = slice(cidx * CHUNK, (cidx + 1) * CHUNK)
                vb = v[rc, hc]
                mixed = _dot(wmh, vb) + bs_ref[:, h:h + 1]
                da = dmix[rc, hc]
                dpa_ref[rc, hc] = (da * mixed * _gelu_grad(ua[rc, hc], cdf_u[rc, hc])).astype(BF16)
                dmixed = da * u[rc, hc]
                dmb = dmixed.astype(BF16)
                dv_buf[rc, hc] = _dot_tn(wmh, dmb)
                dwm_ref[h] += _dot_nt(dmb, vb)
                db_acc[:, hc] += dmixed
        dv = dv_buf[...]
        dlb_ref[...] += jnp.sum(dv, axis=0, keepdims=True)
        dlg_ref[...] += jnp.sum(dv * xhat, axis=0, keepdims=True)
        dvg = _ln_bwd(dv * lg_ref[...], xhat, rstd)
        dpa_ref[:, D_A:2 * D_A] = (dvg * _gelu_grad(va, cdf_v)).astype(BF16)

        for q in range(H_B):
            cols = slice(q * LANES, (q + 1) * LANES)
            chat, crstd = _ln(c_ref[:, cols])
            z = chat * clg_ref[:, cols] + clb_ref[:, cols]
            sg = _sigmoid(z)
            dz = dmix[:, D_A + q * LANES:D_A + (q + 1) * LANES] * (sg * (1.0 + z * (1.0 - sg)))
            dclb_ref[:, cols] += jnp.sum(dz, axis=0, keepdims=True)
            dclg_ref[:, cols] += jnp.sum(dz * chat, axis=0, keepdims=True)
            dc = _ln_bwd(dz * clg_ref[:, cols], chat, crstd)
            dc_ref[:, cols] = dc
            dcb_ref[:, cols] += jnp.sum(dc, axis=0, keepdims=True)

        @pl.when(i == n_tiles - 1)
        def _():
            for h in range(H_A):
                hc = slice(h * CHUNK, (h + 1) * CHUNK)
                dwm_ref[h] = jnp.where(mask, dwm_ref[h], 0.0)
                dbs_ref[h:h + 1, :] = jnp.sum(db_acc[:, hc].T, axis=0, keepdims=True)

    vec = _full((1, D_A))
    vshape = jax.ShapeDtypeStruct((1, D_A), F32)
    return pl.pallas_call(
        body, name="mixer_bwd_a", grid=(n_tiles,),
        in_specs=[_rows(tm, D), _full((D, D)), _rows(tm, IN_COLS), _rows(tm, D_B), vec, vec,
                  _full((H_A, CHUNK, CHUNK)), _full((CHUNK, H_A)), vec, vec, HBM],
        out_specs=[_rows(tm, 2 * D_A), _rows(tm, D_B), vec, vec, _full((H_A, CHUNK, CHUNK)),
                   _full((H_A, CHUNK)), vec, vec, vec],
        out_shape=[jax.ShapeDtypeStruct((T, 2 * D_A), BF16), jax.ShapeDtypeStruct((T, D_B), F32), vshape, vshape,
                   jax.ShapeDtypeStruct((H_A, CHUNK, CHUNK), F32), jax.ShapeDtypeStruct((H_A, CHUNK), F32),
                   vshape, vshape, vshape],
        scratch_shapes=[pltpu.VMEM((tm, D_A), F32), pltpu.VMEM((CHUNK, D_A), F32)],
        compiler_params=_cparams(32),
    )(*map(_hbm, (dxb, wout, proj, c, lg, lb, wm, bst, clg, clb, dep)))


def _mixer_bwd_b(dc, g, proj, cw, dpa, wg_in, x, g1, dres):
    tm = TM_SGU
    n_tiles = T // tm
    hb = tm // HALO
    bn = wg_in.shape[2]

    def body(dc_ref, dch_ref, g_ref, p_ref, cw_ref, dpa_ref, win_ref, x_ref, g1_ref, dres_ref,
             dp_ref, dcw_ref, dx_ref, dxb_ref, dg1_ref, dcbuf, dwacc):
        i = pl.program_id(0)

        @pl.when(i == 0)
        def _():
            dwacc[...] = jnp.zeros(dwacc.shape, F32)
            dg1_ref[...] = jnp.zeros(dg1_ref.shape, F32)

        _fill_shift_buffer(dcbuf, 0, dc_ref[...])
        _fill_shift_buffer(dcbuf, tm, jnp.where(i < n_tiles - 1, dch_ref[...], 0.0))
        _build_shifts(dcbuf)
        dp_ref[:, 0:2 * D_A] = dpa_ref[...]
        for q in range(H_B):
            cols = slice(q * LANES, (q + 1) * LANES)

            def row_block(rb, carry, q=q, cols=cols):
                base = pl.multiple_of(rb * RB, RB)
                rows = pl.ds(base, RB)
                gv = g_ref[rows, cols]
                acc = [jnp.zeros((RB, LANES), F32) for _ in range(CONV_CHAINS)]
                parts = []
                for k in range(CONV_W):
                    xk = _shifted(dcbuf, q, base, CONV_W - 1 - k)
                    acc[k % CONV_CHAINS] = acc[k % CONV_CHAINS] + xk * cw_ref[k:k + 1, cols]
                    parts.append(jnp.sum((gv * xk).reshape(RB // 8, 8, LANES), axis=0))
                dg = functools.reduce(lambda a, b: a + b, acc)
                val = p_ref[rows, cols]
                sg = _sigmoid(p_ref[rows, D_B + q * LANES:D_B + (q + 1) * LANES])
                dp_ref[rows, 2 * D_A + q * LANES:2 * D_A + (q + 1) * LANES] = (dg * sg).astype(BF16)
                dp_ref[rows, 2 * D_A + D_B + q * LANES:2 * D_A + D_B + (q + 1) * LANES] = (
                    dg * val * sg * (1.0 - sg)).astype(BF16)
                for k in range(CONV_W):
                    dwacc[k * 8:(k + 1) * 8, cols] += parts[k]
                return carry

            lax.fori_loop(0, tm // RB, row_block, 0)

        dh = jnp.zeros((tm, D), F32)
        for j in range(N_DEV):
            dh = dh + _dot_nt(dp_ref[:, j * bn:(j + 1) * bn], win_ref[j])
        xv = x_ref[...]
        dxn, dg = _rms_bwd(dh, xv, _rms(xv), g1_ref[...])
        dx = dres_ref[...] + dxn
        dx_ref[...] = dx
        dxb_ref[...] = dx.astype(BF16)
        dg1_ref[...] += dg

        @pl.when(i == n_tiles - 1)
        def _():
            for k in range(CONV_W):
                dcw_ref[k:k + 1, :] = jnp.sum(dwacc[k * 8:(k + 1) * 8, :], axis=0, keepdims=True)

    return pl.pallas_call(
        body, name="mixer_bwd_b", grid=(n_tiles,),
        in_specs=[_rows(tm, D_B),
                  pl.BlockSpec((HALO, D_B), lambda i: (jnp.minimum((i + 1) * hb, T // HALO - 1), 0)),
                  _rows(tm, D_B),
                  pl.BlockSpec((tm, 2 * D_B), lambda i: (i, 1)),
                  _full((CONV_W, D_B)), _rows(tm, 2 * D_A),
                  pl.BlockSpec(wg_in.shape, lambda i: (0, 0, 0), pipeline_mode=pl.Buffered(1)),
                  _rows(tm, D), _full((1, D)), _rows(tm, D)],
        out_specs=[_rows(tm, IN_COLS), _full((CONV_W, D_B)), _rows(tm, D), _rows(tm, D), _full((1, D))],
        out_shape=[jax.ShapeDtypeStruct((T, IN_COLS), BF16), jax.ShapeDtypeStruct((CONV_W, D_B), F32),
                   jax.ShapeDtypeStruct((T, D), F32), jax.ShapeDtypeStruct((T, D), BF16),
                   jax.ShapeDtypeStruct((1, D), F32)],
        scratch_shapes=[pltpu.VMEM((8, H_B, tm + HALO, LANES), F32), pltpu.VMEM((CONV_W * 8, D_B), F32)],
        compiler_params=_cparams(40),
    )(*map(_hbm, (dc, dc, g, proj, cw, dpa, wg_in, x, g1, dres)))


def _wgrad(name, a, g, bn, square_a=False):
    k = a.shape[1]
    n = g.shape[1]
    tk = min(k, 1024)
    tn = min(n, max(bn, 1024))
    nsub = tn // bn
    tt = TT_WGRAD
    nt = T // tt

    def body(a_ref, g_ref, o_ref, ob_ref):
        t = pl.program_id(2)

        @pl.when(t == 0)
        def _():
            o_ref[...] = jnp.zeros(o_ref.shape, F32)

        av = a_ref[...]
        if square_a:
            av = av * av
        for s in range(nsub):
            o_ref[s] += _dot_tn(av, g_ref[:, s * bn:(s + 1) * bn])

        @pl.when(t == nt - 1)
        def _():
            ob_ref[...] = o_ref[...].astype(BF16)

    ospec = pl.BlockSpec((nsub, tk, bn), lambda ki, ni, ti: (ni, ki, 0))
    return pl.pallas_call(
        body, name=name, grid=(k // tk, n // tn, nt),
        in_specs=[pl.BlockSpec((tt, tk), lambda ki, ni, ti: (ti, ki)),
                  pl.BlockSpec((tt, tn), lambda ki, ni, ti: (ti, ni))],
        out_specs=[ospec, ospec],
        out_shape=[jax.ShapeDtypeStruct((n // bn, k, bn), F32), jax.ShapeDtypeStruct((n // bn, k, bn), BF16)],
        compiler_params=_cparams(40, ("arbitrary", "arbitrary", "arbitrary")),
    )(a, g)


def _coords():
    return lax.axis_index("x"), lax.axis_index("y"), lax.axis_index("c")


def _flat(x, y, c):
    return 4 * x + 2 * y + c


def _peer(k):
    x, y, c = _coords()
    return (x ^ ((k >> 2) & 1), y ^ ((k >> 1) & 1), c ^ (k & 1))


def _hbm(a):
    return pltpu.with_memory_space_constraint(a, pltpu.HBM)


def _hbm_like(a):
    return pltpu.HBM(a.shape, a.dtype)


def _peer_sems():
    return pltpu.SemaphoreType.DMA((N_DEV - 1,))


def _place_own(shards):
    n = len(shards)
    shapes = [(s.shape if l is None else s.shape[1:]) for s, l, _ in shards]

    def body(*refs):
        ins, outs = refs[:n], refs[n:2 * n]
        stage_in, stage_out = refs[2 * n:3 * n], refs[3 * n:4 * n]
        in_sems, out_sems = refs[4 * n], refs[4 * n + 1]
        me = _flat(*_coords())
        srcs = [ins[a] if shards[a][1] is None else ins[a].at[shards[a][1]] for a in range(n)]
        loads = [pltpu.make_async_copy(srcs[a], stage_in[a], in_sems.at[a]) for a in range(n)]
        stores = [pltpu.make_async_copy(stage_out[a], outs[a].at[me], out_sems.at[a]) for a in range(n)]
        for cp in loads:
            cp.start()
        for a in range(n):
            loads[a].wait()
            stage_out[a][...] = stage_in[a][...].astype(stage_out[a].dtype)
            stores[a].start()
        for cp in stores:
            cp.wait()

    return pl.pallas_call(
        body, name="place_own", in_specs=[HBM] * n, out_specs=[HBM] * n,
        out_shape=[pltpu.HBM((N_DEV,) + shapes[a], shards[a][2]) for a in range(n)],
        scratch_shapes=[*[pltpu.VMEM(shapes[a], shards[a][0].dtype) for a in range(n)],
                        *[pltpu.VMEM(shapes[a], shards[a][2]) for a in range(n)],
                        pltpu.SemaphoreType.DMA((n,)), pltpu.SemaphoreType.DMA((n,))],
        compiler_params=pltpu.CompilerParams(vmem_limit_bytes=40 * MIB),
    )(*[_hbm(s) for s, _, _ in shards])


SIBLING = 1
CHIP_PEERS = (2, 4, 6)
FIRST_PEERS = (SIBLING,) + CHIP_PEERS


def _gather_start(lands):
    n = len(lands)

    def body(*refs):
        lnd, send, recv = refs[:n], refs[n:2 * n], refs[2 * n:3 * n]
        me = _flat(*_coords())
        for a in range(n):
            for j, k in enumerate(FIRST_PEERS):
                pltpu.make_async_remote_copy(
                    src_ref=lnd[a].at[me], dst_ref=lnd[a].at[me], send_sem=send[a].at[j],
                    recv_sem=recv[a].at[j], device_id=_peer(k), device_id_type=MESH_ID).start()

    sems = pltpu.SemaphoreType.DMA((len(FIRST_PEERS),))
    outs = pl.pallas_call(
        body, name="gather_start",
        out_shape=(*[sems] * (2 * n), *[_hbm_like(l) for l in lands]),
        in_specs=[HBM] * n, out_specs=(*[SEM] * (2 * n), *[HBM] * n),
        input_output_aliases={i: 2 * n + i for i in range(n)},
        compiler_params=pltpu.CompilerParams(has_side_effects=EFFECT),
    )(*[_hbm(l) for l in lands])
    return outs[:n], outs[n:2 * n], outs[2 * n:]


def _gather_pass_on(name, lands, sends, recvs, after):
    n = len(lands)

    def body(*refs):
        lnd, send, recv = refs[:n], refs[n:2 * n], refs[2 * n:3 * n]
        send2, recv2 = refs[3 * n + 1:4 * n + 1], refs[4 * n + 1:5 * n + 1]
        sibling = _peer(SIBLING)
        for a in range(n):
            for j, k in enumerate(CHIP_PEERS):
                cp = pltpu.make_async_remote_copy(
                    src_ref=lnd[a].at[0], dst_ref=lnd[a].at[0], send_sem=send[a].at[1 + j],
                    recv_sem=recv[a].at[1 + j], device_id=_peer(k), device_id_type=MESH_ID)
                cp.wait_send()
                cp.wait_recv()
                blk = _flat(*_peer(k))
                pltpu.make_async_remote_copy(
                    src_ref=lnd[a].at[blk], dst_ref=lnd[a].at[blk], send_sem=send2[a].at[j],
                    recv_sem=recv2[a].at[j], device_id=sibling, device_id_type=MESH_ID).start()

    sems = pltpu.SemaphoreType.DMA((len(CHIP_PEERS),))
    outs = pl.pallas_call(
        body, name=name, out_shape=(*[sems] * (2 * n), *[_hbm_like(l) for l in lands]),
        in_specs=(*[HBM] * n, *[SEM] * (2 * n), HBM), out_specs=(*[SEM] * (2 * n), *[HBM] * n),
        input_output_aliases={i: 2 * n + i for i in range(n)},
        compiler_params=pltpu.CompilerParams(has_side_effects=EFFECT),
    )(*lands, *sends, *recvs, after)
    return outs[:n], outs[n:2 * n], outs[2 * n:]


def _gather_wait(name, lands, sends, recvs, sends2, recvs2, after):
    n = len(lands)

    def body(*refs):
        lnd, send, recv = refs[:n], refs[n:2 * n], refs[2 * n:3 * n]
        send2, recv2 = refs[3 * n:4 * n], refs[4 * n:5 * n]
        sibling = _peer(SIBLING)
        for a in range(n):
            own = pltpu.make_async_remote_copy(
                src_ref=lnd[a].at[0], dst_ref=lnd[a].at[0], send_sem=send[a].at[0], recv_sem=recv[a].at[0],
                device_id=sibling, device_id_type=MESH_ID)
            own.wait_send()
            own.wait_recv()
            for j in range(len(CHIP_PEERS)):
                cp = pltpu.make_async_remote_copy(
                    src_ref=lnd[a].at[0], dst_ref=lnd[a].at[0], send_sem=send2[a].at[j], recv_sem=recv2[a].at[j],
                    device_id=sibling, device_id_type=MESH_ID)
                cp.wait_send()
                cp.wait_recv()

    return pl.pallas_call(
        body, name=name, out_shape=[_hbm_like(l) for l in lands],
        in_specs=(*[HBM] * n, *[SEM] * (4 * n), HBM), out_specs=[HBM] * n,
        input_output_aliases={i: i for i in range(n)},
        compiler_params=pltpu.CompilerParams(has_side_effects=EFFECT),
    )(*lands, *sends, *recvs, *sends2, *recvs2, after)


def _scatter_start(name, grads, lands, layer):
    n = len(grads)

    def body(*refs):
        g, lnd, send, recv = refs[:n], refs[n:2 * n], refs[2 * n:3 * n], refs[3 * n:4 * n]
        for a in range(n):
            for k in range(1, N_DEV):
                to = _peer(k)
                pltpu.make_async_remote_copy(
                    src_ref=g[a].at[_flat(*to)], dst_ref=lnd[a].at[layer, k - 1], send_sem=send[a].at[k - 1],
                    recv_sem=recv[a].at[k - 1], device_id=to, device_id_type=MESH_ID).start()

    outs = pl.pallas_call(
        body, name=name,
        out_shape=(*[_peer_sems()] * (2 * n), *[_hbm_like(g) for g in grads], *[_hbm_like(l) for l in lands]),
        in_specs=[HBM] * (2 * n), out_specs=(*[SEM] * (2 * n), *[HBM] * (2 * n)),
        input_output_aliases={i: 2 * n + i for i in range(2 * n)},
        compiler_params=pltpu.CompilerParams(has_side_effects=EFFECT),
    )(*[_hbm(g) for g in grads], *[_hbm(l) for l in lands])
    return outs[:n], outs[n:2 * n], outs[2 * n:3 * n], outs[3 * n:]


def _scatter_wait(name, grads, lands, sends, recvs, after):
    n, nw = len(grads), len(lands)

    def body(*refs):
        g, lnd = refs[:n], refs[n:n + nw]
        send, recv = refs[n + nw:2 * n + nw], refs[2 * n + nw:3 * n + nw]
        for a in range(n):
            for k in range(1, N_DEV):
                cp = pltpu.make_async_remote_copy(
                    src_ref=g[a].at[0], dst_ref=lnd[a // DEPTH].at[a % DEPTH, 0],
                    send_sem=send[a].at[k - 1], recv_sem=recv[a].at[k - 1],
                    device_id=_peer(k), device_id_type=MESH_ID)
                cp.wait_send()
                cp.wait_recv()

    outs = pl.pallas_call(
        body, name=name,
        out_shape=(*[_hbm_like(g) for g in grads], *[_hbm_like(l) for l in lands]),
        in_specs=(*[HBM] * (n + nw), *[SEM] * (2 * n), HBM), out_specs=[HBM] * (n + nw),
        input_output_aliases={i: i for i in range(n + nw)},
        compiler_params=pltpu.CompilerParams(has_side_effects=EFFECT),
    )(*grads, *lands, *sends, *recvs, after)
    return outs[n:]


def _all_reduce_small(part, deps):
    rows = part.shape[0]
    br = rows // N_DEV
    assert br * N_DEV == rows and br % 8 == 0
    nd = len(deps)

    def body(p_ref, *refs):
        o_ref, slots, send1, recv1, send2, recv2 = refs[nd:]
        me = _flat(*_coords())

        def block(ref, d):
            return ref.at[pl.ds(pl.multiple_of(d * br, 8), br), :]

        slots[me] = p_ref[pl.ds(pl.multiple_of(me * br, 8), br), :]
        scatter, gather = [], []
        for k in range(1, N_DEV):
            to = _peer(k)
            scatter.append(pltpu.make_async_remote_copy(
                src_ref=block(p_ref, _flat(*to)), dst_ref=slots.at[me],
                send_sem=send1.at[k - 1], recv_sem=recv1.at[k - 1], device_id=to, device_id_type=MESH_ID))
            gather.append(pltpu.make_async_remote_copy(
                src_ref=block(o_ref, me), dst_ref=block(o_ref, me),
                send_sem=send2.at[k - 1], recv_sem=recv2.at[k - 1], device_id=to, device_id_type=MESH_ID))
        for cp in scatter:
            cp.start()
        for cp in scatter:
            cp.wait()
        acc = slots[0]
        for d in range(1, N_DEV):
            acc = acc + slots[d]
        o_ref[pl.ds(pl.multiple_of(me * br, 8), br), :] = acc
        for cp in gather:
            cp.start()
        for cp in gather:
            cp.wait()

    return pl.pallas_call(
        body, name="all_reduce_small",
        in_specs=[pl.BlockSpec(memory_space=pltpu.VMEM)] + [HBM] * nd, out_specs=pl.BlockSpec(memory_space=pltpu.VMEM),
        out_shape=jax.ShapeDtypeStruct(part.shape, F32),
        scratch_shapes=[pltpu.VMEM((N_DEV, br, LANES), F32)] + [pltpu.SemaphoreType.DMA((N_DEV - 1,))] * 4,
        compiler_params=pltpu.CompilerParams(vmem_limit_bytes=24 * MIB),
    )(part, *map(_hbm, deps))


def _adam_math(w, g, m, v):
    m = ADAM_B1 * m + (1.0 - ADAM_B1) * g
    v = ADAM_B2 * v + (1.0 - ADAM_B2) * (g * g)
    m_hat = m / (1.0 - ADAM_B1 ** ADAM_STEP)
    v_hat = v / (1.0 - ADAM_B2 ** ADAM_STEP)
    delta = -ADAM_LR * (m_hat / (jnp.sqrt(v_hat) + ADAM_EPS) + ADAM_WD * w)
    return delta, m, v


def _adam_sharded(name, me, g32, recv, w, m, v):
    _, r, c = w.shape
    tr = min(r, 256)
    nblk = r // tr
    assert len(g32) == DEPTH == 2

    def body(me_ref, g0_ref, g1_ref, recv_ref, w_ref, m_ref, v_ref, g_ref, d_ref, nm_ref, nv_ref):
        g = jnp.where(pl.program_id(0) == 0, g0_ref[...], g1_ref[...])
        for k in range(N_DEV - 1):
            g = g + recv_ref[k].astype(F32)
        delta, nm, nv = _adam_math(w_ref[...], g, m_ref[...], v_ref[...])
        g_ref[...] = g
        d_ref[...] = delta
        nm_ref[...] = nm
        nv_ref[...] = nv

    blk = pl.BlockSpec((None, tr, c), lambda l, i, me_ref: (l, i, 0))
    own0 = pl.BlockSpec((None, tr, c), lambda l, i, me_ref: (me_ref[0], jnp.where(l == 0, i, nblk - 1), 0))
    own1 = pl.BlockSpec((None, tr, c), lambda l, i, me_ref: (me_ref[0], jnp.where(l == 1, i, 0), 0))
    shp = jax.ShapeDtypeStruct(w.shape, F32)
    return pl.pallas_call(
        body, name=name,
        grid_spec=pltpu.PrefetchScalarGridSpec(
            num_scalar_prefetch=1, grid=(DEPTH, nblk),
            in_specs=[own0, own1, pl.BlockSpec((None, N_DEV - 1, tr, c), lambda l, i, me_ref: (l, 0, i, 0)),
                      blk, blk, blk],
            out_specs=[blk] * 4),
        out_shape=[shp] * 4,
        compiler_params=_cparams(32, ("arbitrary", "arbitrary")),
    )(me, *[_hbm(a) for a in (*g32, recv, w, m, v)])


def _adam_small(gs, ws, ms, vs):
    n = len(gs)

    def body(*refs):
        g, w, m, v = refs[:n], refs[n:2 * n], refs[2 * n:3 * n], refs[3 * n:4 * n]
        d, nm, nv = refs[4 * n:5 * n], refs[5 * n:6 * n], refs[6 * n:7 * n]
        for a in range(n):
            delta, new_m, new_v = _adam_math(w[a][...], g[a][...], m[a][...], v[a][...])
            d[a][...] = delta
            nm[a][...] = new_m
            nv[a][...] = new_v

    specs = [_full(a.shape) for a in gs]
    shapes = [jax.ShapeDtypeStruct(a.shape, F32) for a in gs]
    outs = pl.pallas_call(
        body, name="adam_small", grid=(1,),
        in_specs=specs * 4, out_specs=specs * 3, out_shape=shapes * 3,
        compiler_params=_cparams(24),
    )(*gs, *ws, *ms, *vs)
    return outs[:n], outs[n:2 * n], outs[2 * n:]


SMALL = ["norm1_g", "sgu_ln_g", "sgu_ln_b", "sgu_w", "sgu_b", "conv_b", "conv_ln_g", "conv_ln_b", "norm2_g",
         "final_g"]


def _pack(arrays):
    flat = jnp.concatenate([a.reshape(-1) for a in arrays])
    pad = (-flat.shape[0]) % (N_DEV * 8 * LANES)
    return jnp.pad(flat, (0, pad)).reshape(-1, LANES)


def _unpack(packed, shapes):
    flat = packed.reshape(-1)
    out, off = [], 0
    for s in shapes:
        size = 1
        for d in s:
            size *= d
        out.append(flat[off:off + size].reshape(s))
        off += size
    return out


def kernel(x, norm1_g, w_in, sgu_ln_g, sgu_ln_b, sgu_w, sgu_b, conv_w, conv_b, conv_ln_g, conv_ln_b, w_out, norm2_g, w_ff1, w_ff2, final_g, loss_target, m_norm1_g, m_w_in, m_sgu_ln_g, m_sgu_ln_b, m_sgu_w, m_sgu_b, m_conv_w, m_conv_b, m_conv_ln_g, m_conv_ln_b, m_w_out, m_norm2_g, m_w_ff1, m_w_ff2, m_final_g, v_norm1_g, v_w_in, v_sgu_ln_g, v_sgu_ln_b, v_sgu_w, v_sgu_b, v_conv_w, v_conv_b, v_conv_ln_g, v_conv_ln_b, v_w_out, v_norm2_g, v_w_ff1, v_w_ff2, v_final_g):
    x2d = x.reshape(T, D)
    tgt = loss_target.reshape(T, D)
    cw_shard = conv_w.reshape(CONV_W, LANES)

    gnames, shards = [], []
    for l in range(DEPTH):
        for k, w in (("w_in", w_in), ("w_out", w_out), ("w_ff1", w_ff1), ("w_ff2", w_ff2)):
            gnames.append(f"{k}{l}")
            shards.append((w, l, BF16))
        if l == 0:
            gnames.insert(1, "conv_w")
            shards.insert(1, (cw_shard, None, F32))
    sends, recvs, lands = _gather_start(_place_own(shards))
    gidx = {k: i for i, k in enumerate(gnames)}

    passed = {}

    def pass_on(ks, after):
        idx = [gidx[k] for k in ks]
        s2, r2, ld = _gather_pass_on("gather_pass_on_" + "_".join(ks), [lands[i] for i in idx],
                                     [sends[i] for i in idx], [recvs[i] for i in idx], after)
        for j, k in enumerate(ks):
            passed[k] = (s2[j], r2[j], ld[j])

    def gathered(ks, after):
        idx = [gidx[k] for k in ks]
        return _gather_wait("gather_wait_" + "_".join(ks), [passed[k][2] for k in ks], [sends[i] for i in idx],
                            [recvs[i] for i in idx], [passed[k][0] for k in ks], [passed[k][1] for k in ks], after)

    saved = []
    xl = x2d
    cw_full = None
    for l in range(DEPTH):
        g1 = norm1_g[l].reshape(1, D)
        g2 = norm2_g[l].reshape(1, D)
        lg, lb = sgu_ln_g[l].reshape(1, D_A), sgu_ln_b[l].reshape(1, D_A)
        bst = sgu_b[l].T
        cb = conv_b[l].reshape(1, D_B)
        clg, clb = conv_ln_g[l].reshape(1, D_B), conv_ln_b[l].reshape(1, D_B)
        first = [f"w_in{l}"] if l == 0 else [f"w_in{l}", f"w_out{l}"]
        pass_on(first, xl)
        wg_in, *rest = gathered(first, xl)
        h, proj = _fwd_in(xl, g1, wg_in)
        if l == 0:
            pass_on(["conv_w", "w_out0"], h)
            cw_g, wout = gathered(["conv_w", "w_out0"], proj)
            cw_full = cw_g.reshape(N_DEV, DEPTH, CONV_W, D_B // N_DEV).transpose(1, 2, 0, 3).reshape(
                DEPTH, CONV_W, D_B)
        else:
            (wout,) = rest
            pass_on([f"w_ff1{l}"], proj)
        wout = wout.reshape(D, D)
        mix, c, g, x1, h2 = _mixer_fwd(proj, lg, lb, sgu_w[l], bst, cw_full[l], cb, clg, clb, wout, xl, g2)
        pass_on([f"w_ff1{l}"] if l == 0 else [f"w_ff2{l}"], h2)
        (wg_ff1,) = gathered([f"w_ff1{l}"], h2)
        r = _fwd_ff1(h2, wg_ff1)
        if l == 0:
            pass_on(["w_ff20"], r)
        (w2,) = gathered([f"w_ff2{l}"], r)
        w2 = w2.reshape(D_FF, D)
        saved.append(dict(x=xl, h=h, proj=proj, mix=mix, c=c, g=g, x1=x1, h2=h2, r=r, wg_in=wg_in, wout=wout,
                          wg_ff1=wg_ff1, w2=w2, g1=g1, g2=g2, lg=lg, lb=lb, bst=bst, clg=clg, clb=clb))
        if l + 1 < DEPTH:
            xl = _fwd_ff2(r, w2, x1)
        else:
            dx, dxb, loss_part, d_final_g = _fwd_ff2_loss(r, w2, x1, final_g.reshape(1, D), tgt)

    names = ["w_in", "w_out", "w_ff1", "w_ff2"]
    block = dict(w_in=(D, IN_COLS // N_DEV), w_out=(D // N_DEV, D), w_ff1=(D, D_FF // N_DEV), w_ff2=(D_FF // N_DEV, D))
    land = {k: lax.empty((DEPTH, N_DEV - 1) + block[k], BF16) for k in names}
    big32 = {k: [None] * DEPTH for k in names}
    big16 = {k: [None] * DEPTH for k in names}
    ssend = {k: [None] * DEPTH for k in names}
    srecv = {k: [None] * DEPTH for k in names}

    def send_grads(l, grads):
        ks = list(grads)
        for k in ks:
            big32[k][l] = grads[k][0].reshape((N_DEV,) + block[k])
        sends, recvs, g16s, lands = _scatter_start(
            "scatter_start_" + "_".join(ks) + str(l), [grads[k][1].reshape((N_DEV,) + block[k]) for k in ks],
            [land[k] for k in ks], l)
        for i, k in enumerate(ks):
            ssend[k][l], srecv[k][l], big16[k][l], land[k] = sends[i], recvs[i], g16s[i], lands[i]

    small = {}
    d_conv_w = [None] * DEPTH
    for l in reversed(range(DEPTH)):
        s = saved[l]
        send_grads(l, dict(w_ff2=_wgrad("wgrad_ff2", s["r"], dxb, D, square_a=True)))
        df1, dx1, dx1b, dg2 = _bwd_mlp(dxb, dx, s["w2"], s["r"], s["wg_ff1"], s["x1"], s["g2"], big16["w_ff2"][l])
        send_grads(l, dict(w_ff1=_wgrad("wgrad_ff1", s["h2"], df1, D_FF // N_DEV),
                           w_out=_wgrad("wgrad_out", s["mix"], dx1b, D)))
        dpa, dc, dlg, dlb, dwm, dbs, dcb, dclg, dclb = _mixer_bwd_a(
            dx1b, s["wout"], s["proj"], s["c"], s["lg"], s["lb"], sgu_w[l], s["bst"], s["clg"], s["clb"],
            big16["w_out"][l])
        dproj, d_conv_w[l], dx, dxb, dg1 = _mixer_bwd_b(dc, s["g"], s["proj"], cw_full[l], dpa, s["wg_in"], s["x"],
                                                         s["g1"], dx1)
        send_grads(l, dict(w_in=_wgrad("wgrad_in", s["h"], dproj, IN_COLS // N_DEV)))
        small[l] = dict(norm1_g=dg1, sgu_ln_g=dlg, sgu_ln_b=dlb, sgu_w=dwm, sgu_b=dbs, conv_b=dcb, conv_ln_g=dclg,
                        conv_ln_b=dclb, norm2_g=dg2)
    grad_x = dx.reshape(1, T, D)

    me_block = _flat(*_coords()).astype(jnp.int32).reshape(1)
    wmv = dict(w_in=(w_in, m_w_in, v_w_in), w_out=(w_out, m_w_out, v_w_out), w_ff1=(w_ff1, m_w_ff1, v_w_ff1),
               w_ff2=(w_ff2, m_w_ff2, v_w_ff2))
    res = {}

    def update(tag, ks, after):
        recv = _scatter_wait("scatter_wait_" + tag, [g for k in ks for g in big16[k]], [land[k] for k in ks],
                             [q for k in ks for q in ssend[k]], [q for k in ks for q in srecv[k]], after)
        for k, rk in zip(ks, recv):
            res[k] = _adam_sharded("adam_" + k, me_block, big32[k], rk, *wmv[k])

    update("early", ["w_ff2", "w_ff1", "w_out"], big16["w_in"][0])

    rep = dict(norm1_g=(norm1_g, m_norm1_g, v_norm1_g), sgu_ln_g=(sgu_ln_g, m_sgu_ln_g, v_sgu_ln_g),
               sgu_ln_b=(sgu_ln_b, m_sgu_ln_b, v_sgu_ln_b), sgu_w=(sgu_w, m_sgu_w, v_sgu_w),
               sgu_b=(sgu_b, m_sgu_b, v_sgu_b), conv_b=(conv_b, m_conv_b, v_conv_b),
               conv_ln_g=(conv_ln_g, m_conv_ln_g, v_conv_ln_g), conv_ln_b=(conv_ln_b, m_conv_ln_b, v_conv_ln_b),
               norm2_g=(norm2_g, m_norm2_g, v_norm2_g), final_g=(final_g, m_final_g, v_final_g))
    parts = []
    for k in SMALL:
        if k == "final_g":
            parts.append(d_final_g.reshape(rep[k][0].shape))
        else:
            parts.append(jnp.stack([small[l][k].reshape(rep[k][0].shape[1:]) for l in range(DEPTH)]))
    parts.append(jnp.stack(d_conv_w))
    parts.append(loss_part[0, 0:1])
    shapes = [p.shape for p in parts]
    packed_sum = _all_reduce_small(_pack(parts), [res[k][0] for k in ("w_ff2", "w_ff1", "w_out")])
    update("late", ["w_in"], packed_sum)
    summed = _unpack(packed_sum, shapes)
    loss = summed[-1][0]
    me = _flat(*_coords())
    g_conv_w = lax.dynamic_slice_in_dim(summed[-2], me * (D_B // N_DEV), D_B // N_DEV, axis=2)
    g_small = summed[:-2] + [g_conv_w]
    keys = SMALL + ["conv_w"]
    rep["conv_w"] = (conv_w, m_conv_w, v_conv_w)
    def two_d(a):
        return a.reshape(1, -1) if a.ndim == 1 else a

    d_s, nm_s, nv_s = _adam_small([two_d(g) for g in g_small], *[[two_d(rep[k][i]) for k in keys] for i in range(3)])
    for i, k in enumerate(keys):
        shape = rep[k][0].shape
        res[k] = (g_small[i], d_s[i].reshape(shape), nm_s[i].reshape(shape), nv_s[i].reshape(shape))

    order = ["norm1_g", "w_in", "sgu_ln_g", "sgu_ln_b", "sgu_w", "sgu_b", "conv_w", "conv_b", "conv_ln_g",
             "conv_ln_b", "w_out", "norm2_g", "w_ff1", "w_ff2", "final_g"]
    return (loss, grad_x, *[res[k][0] for k in order], *[res[k][1] for k in order],
            *[res[k][2] for k in order], *[res[k][3] for k in order])
```

```python
import functools

import jax
import jax.numpy as jnp
from jax import lax
from jax.experimental import pallas as pl
from jax.experimental.pallas import tpu as pltpu

F32 = jnp.float32
BF16 = jnp.bfloat16

N_DEV = 8
DEPTH = 2
T = 4096
D = 1024
D_A = 512
D_B = 512
CHUNK = 128
H_A = 4
H_B = 4
CONV_W = 31
HALO = 32
D_FF = 4096
IN_COLS = 2048
EPS = 1e-6

ADAM_LR = 0.001
ADAM_B1 = 0.9
ADAM_B2 = 0.999
ADAM_EPS = 1e-08
ADAM_WD = 0.01
ADAM_STEP = 10

TM = 512
TM_FWD = 1024
TT_WGRAD = 4096
TM_SGU = 512
RB = 64
CONV_CHAINS = 2
LANES = 128
MIB = 1024 * 1024
SCOPED_VMEM_MIB = 60

SQRT_HALF = 0.7071067811865476
INV_SQRT_2PI = 0.3989422804014327

MESH_ID = pl.DeviceIdType.MESH
HBM = pl.BlockSpec(memory_space=pltpu.HBM)
SEM = pl.BlockSpec(memory_space=pltpu.SEMAPHORE)
EFFECT = pltpu.SideEffectType.DATAFLOW_SIDE_EFFECTING


def _cparams(vmem_mib, sem=("arbitrary",)):
    assert vmem_mib <= SCOPED_VMEM_MIB
    return pltpu.CompilerParams(dimension_semantics=sem, vmem_limit_bytes=SCOPED_VMEM_MIB * MIB)


def _full(shape):
    return pl.BlockSpec(shape, lambda *_: (0,) * len(shape))


def _rows(tm, cols):
    return pl.BlockSpec((tm, cols), lambda i: (i, 0))


def _gelu(x):
    cdf = 0.5 * (1.0 + lax.erf(x * SQRT_HALF))
    return x * cdf, cdf


def _gelu_grad(x, cdf):
    return cdf + x * (INV_SQRT_2PI * jnp.exp(-0.5 * x * x))


def _sigmoid(x):
    return 1.0 / (1.0 + jnp.exp(-x))


def _ln(x):
    mu = jnp.mean(x, axis=-1, keepdims=True)
    xc = x - mu
    rstd = lax.rsqrt(jnp.mean(xc * xc, axis=-1, keepdims=True) + EPS)
    return xc * rstd, rstd


def _ln_bwd(dyh, xhat, rstd):
    return rstd * (dyh - jnp.mean(dyh, axis=-1, keepdims=True) - xhat * jnp.mean(dyh * xhat, axis=-1, keepdims=True))


def _rms(x):
    return lax.rsqrt(jnp.mean(x * x, axis=-1, keepdims=True) + EPS)


def _rms_bwd(dh, x, r, g):
    n = x * r
    dn = dh * g
    dx = r * (dn - n * jnp.mean(dn * n, axis=-1, keepdims=True))
    return dx, jnp.sum(dh * n, axis=0, keepdims=True)


def _dot(a, b):
    return jnp.dot(a, b, preferred_element_type=F32)


def _dot_nt(a, b):
    return lax.dot_general(a, b, (((1,), (1,)), ((), ())), preferred_element_type=F32)


def _dot_tn(a, b):
    return lax.dot_general(a, b, (((0,), (0,)), ((), ())), preferred_element_type=F32)


def _tril_mask():
    r = lax.broadcasted_iota(jnp.int32, (CHUNK, CHUNK), 0)
    c = lax.broadcasted_iota(jnp.int32, (CHUNK, CHUNK), 1)
    return r >= c


def _fwd_in(x, g1, wg):
    bn = wg.shape[2]

    def body(x_ref, g_ref, w_ref, h_ref, p_ref):
        xv = x_ref[...]
        h = (xv * _rms(xv) * g_ref[...]).astype(BF16)
        h_ref[...] = h
        for j in range(N_DEV):
            p_ref[:, j * bn:(j + 1) * bn] = _dot(h, w_ref[j])

    return pl.pallas_call(
        body, name="fwd_in", grid=(T // TM_FWD,),
        in_specs=[_rows(TM_FWD, D), _full((1, D)), _full(wg.shape)],
        out_specs=[_rows(TM_FWD, D), _rows(TM_FWD, IN_COLS)],
        out_shape=[jax.ShapeDtypeStruct((T, D), BF16), jax.ShapeDtypeStruct((T, IN_COLS), F32)],
        compiler_params=_cparams(32),
    )(*map(_hbm, (x, g1, wg)))


def _fill_shift_buffer(sh_ref, row0, value):
    for q in range(sh_ref.shape[1]):
        sh_ref[0, q, row0:row0 + value.shape[0], :] = value[:, q * LANES:(q + 1) * LANES]


def _build_shifts(sh_ref):
    rows = sh_ref.shape[2]
    for p in range(1, 8):
        for q in range(sh_ref.shape[1]):
            sh_ref[p, q, 0:rows - 8, :] = sh_ref[0, q, p:p + rows - 8, :]


def _shifted(sh_ref, q, base, off):
    start = base + (off - off % 8)
    if not isinstance(start, int):
        start = pl.multiple_of(start, 8)
    return sh_ref[off % 8, q, pl.ds(start, RB), :]


def _conv_taps(sh_ref, w_ref, q, base, first_tap_row, step):
    cols = slice(q * LANES, (q + 1) * LANES)
    acc = [jnp.zeros((RB, LANES), F32) for _ in range(CONV_CHAINS)]
    for k in range(CONV_W):
        term = _shifted(sh_ref, q, base, first_tap_row + step * k) * w_ref[k:k + 1, cols]
        acc[k % CONV_CHAINS] = acc[k % CONV_CHAINS] + term
    return functools.reduce(lambda a, b: a + b, acc)


def _mixer_fwd(proj, lg, lb, wm, bst, cw, cb, clg, clb, wout, x, g2):
    tm = TM_SGU
    hb = tm // HALO

    def body(p_ref, ph_ref, lg_ref, lb_ref, wm_ref, bs_ref, cw_ref, cb_ref, clg_ref, clb_ref, wo_ref, x_ref, g2_ref,
             mix_ref, c_ref, g_ref, x1_ref, h2_ref, gbuf):
        i = pl.program_id(0)
        u, _ = _gelu(p_ref[:, 0:D_A])
        vg, _ = _gelu(p_ref[:, D_A:2 * D_A])
        xhat, _ = _ln(vg)
        v = (xhat * lg_ref[...] + lb_ref[...]).astype(BF16)
        mask = _tril_mask()
        for h in range(H_A):
            hc = slice(h * CHUNK, (h + 1) * CHUNK)
            wmh = jnp.where(mask, wm_ref[h], 0.0).astype(BF16)
            for c in range(tm // CHUNK):
                rc = slice(c * CHUNK, (c + 1) * CHUNK)
                mixed = _dot(wmh, v[rc, hc]) + bs_ref[:, h:h + 1]
                mix_ref[rc, hc] = (u[rc, hc] * mixed).astype(BF16)
        x1_ref[...] = x_ref[...] + _dot(mix_ref[:, 0:D_A], wo_ref[0:D_A, :])

        g = p_ref[:, 2 * D_A:2 * D_A + D_B] * _sigmoid(p_ref[:, 2 * D_A + D_B:IN_COLS])
        g_ref[...] = g
        gh = ph_ref[:, 0:D_B] * _sigmoid(ph_ref[:, D_B:2 * D_B])
        _fill_shift_buffer(gbuf, 0, jnp.where(i > 0, gh, 0.0))
        _fill_shift_buffer(gbuf, HALO, g)
        _build_shifts(gbuf)
        for q in range(H_B):
            cols = slice(q * LANES, (q + 1) * LANES)
            for rb in range(tm // RB):
                acc = _conv_taps(gbuf, cw_ref, q, rb * RB, HALO - (CONV_W - 1), 1)
                c_ref[rb * RB:(rb + 1) * RB, cols] = acc + cb_ref[:, cols]
        for q in range(H_B):
            cols = slice(q * LANES, (q + 1) * LANES)
            chat, _ = _ln(c_ref[:, cols])
            z = chat * clg_ref[:, cols] + clb_ref[:, cols]
            mix_ref[:, D_A + q * LANES:D_A + (q + 1) * LANES] = (z * _sigmoid(z)).astype(BF16)

        x1 = x1_ref[...] + _dot(mix_ref[:, D_A:D], wo_ref[D_A:D, :])
        x1_ref[...] = x1
        h2_ref[...] = (x1 * _rms(x1) * g2_ref[...]).astype(BF16)

    vec = _full((1, D_A))
    return pl.pallas_call(
        body, name="mixer_fwd", grid=(T // tm,),
        in_specs=[_rows(tm, IN_COLS),
                  pl.BlockSpec((HALO, 2 * D_B), lambda i: (jnp.maximum(i * hb - 1, 0), 1)),
                  vec, vec, _full((H_A, CHUNK, CHUNK)), _full((CHUNK, H_A)),
                  _full((CONV_W, D_B)), vec, vec, vec,
                  pl.BlockSpec((D, D), lambda i: (0, 0), pipeline_mode=pl.Buffered(1)), _rows(tm, D), _full((1, D))],
        out_specs=[_rows(tm, D), _rows(tm, D_B), _rows(tm, D_B), _rows(tm, D), _rows(tm, D)],
        out_shape=[jax.ShapeDtypeStruct((T, D), BF16), jax.ShapeDtypeStruct((T, D_B), F32),
                   jax.ShapeDtypeStruct((T, D_B), F32), jax.ShapeDtypeStruct((T, D), F32),
                   jax.ShapeDtypeStruct((T, D), BF16)],
        scratch_shapes=[pltpu.VMEM((8, H_B, HALO + tm, LANES), F32)],
        compiler_params=_cparams(40),
    )(*map(_hbm, (proj, proj, lg, lb, wm, bst, cw, cb, clg, clb, wout, x, g2)))


def _fwd_ff1(h2, wg):
    bn = wg.shape[2]

    def body(h_ref, w_ref, r_ref):
        h = h_ref[...]
        for j in range(N_DEV):
            r_ref[:, j * bn:(j + 1) * bn] = jnp.maximum(_dot(h, w_ref[j]), 0.0).astype(BF16)

    return pl.pallas_call(
        body, name="fwd_ff1", grid=(T // TM_FWD,),
        in_specs=[_rows(TM_FWD, D), _full(wg.shape)],
        out_specs=_rows(TM_FWD, D_FF),
        out_shape=jax.ShapeDtypeStruct((T, D_FF), BF16),
        compiler_params=_cparams(48),
    )(h2, wg)


def _fwd_ff2(r, w2, x1):
    def body(r_ref, w_ref, x_ref, o_ref):
        rv = r_ref[...]
        o_ref[...] = x_ref[...] + _dot(rv * rv, w_ref[...])

    return pl.pallas_call(
        body, name="fwd_ff2", grid=(T // TM_FWD,),
        in_specs=[_rows(TM_FWD, D_FF), _full((D_FF, D)), _rows(TM_FWD, D)],
        out_specs=_rows(TM_FWD, D),
        out_shape=jax.ShapeDtypeStruct((T, D), F32),
        compiler_params=_cparams(48),
    )(r, w2, x1)


def _fwd_ff2_loss(r, w2, x1, gf, tgt):
    def body(r_ref, w_ref, x_ref, g_ref, t_ref, dx_ref, dxb_ref, loss_ref, dg_ref):
        i = pl.program_id(0)

        @pl.when(i == 0)
        def _():
            loss_ref[...] = jnp.zeros(loss_ref.shape, F32)
            dg_ref[...] = jnp.zeros(dg_ref.shape, F32)

        rv = r_ref[...]
        xv = x_ref[...] + _dot(rv * rv, w_ref[...])
        rn = _rms(xv)
        diff = xv * rn * g_ref[...] - t_ref[...]
        loss_ref[...] += 0.5 * jnp.sum(jnp.mean(diff * diff, axis=-1, keepdims=True), axis=0, keepdims=True)
        dx, dg = _rms_bwd(diff * (1.0 / D), xv, rn, g_ref[...])
        dx_ref[...] = dx
        dxb_ref[...] = dx.astype(BF16)
        dg_ref[...] += dg

    return pl.pallas_call(
        body, name="fwd_ff2_loss", grid=(T // TM,),
        in_specs=[_rows(TM, D_FF), pl.BlockSpec((D_FF, D), lambda i: (0, 0), pipeline_mode=pl.Buffered(1)),
                  _rows(TM, D), _full((1, D)), _rows(TM, D)],
        out_specs=[_rows(TM, D), _rows(TM, D), _full((8, LANES)), _full((1, D))],
        out_shape=[jax.ShapeDtypeStruct((T, D), F32), jax.ShapeDtypeStruct((T, D), BF16),
                   jax.ShapeDtypeStruct((8, LANES), F32), jax.ShapeDtypeStruct((1, D), F32)],
        compiler_params=_cparams(40),
    )(*map(_hbm, (r, w2, x1, gf, tgt)))


def _bwd_mlp(dxb, dres, w2, r, wg1, x1, g2, dep):
    bn = wg1.shape[2]

    def body(d_ref, dres_ref, w2_ref, r_ref, w1_ref, x_ref, g_ref, dep_ref, df1_ref, dx_ref, dxb_ref, dg_ref):
        i = pl.program_id(0)

        @pl.when(i == 0)
        def _():
            dg_ref[...] = jnp.zeros(dg_ref.shape, F32)

        d = d_ref[...]
        dh = jnp.zeros((TM, D), F32)
        for j in range(N_DEV):
            cols = slice(j * bn, (j + 1) * bn)
            df1 = (2.0 * r_ref[:, cols].astype(F32) * _dot_nt(d, w2_ref[cols, :])).astype(BF16)
            df1_ref[:, cols] = df1
            dh = dh + _dot_nt(df1, w1_ref[j])
        xv = x_ref[...]
        dxn, dg = _rms_bwd(dh, xv, _rms(xv), g_ref[...])
        dx = dres_ref[...] + dxn
        dx_ref[...] = dx
        dxb_ref[...] = dx.astype(BF16)
        dg_ref[...] += dg

    once = dict(pipeline_mode=pl.Buffered(1))
    return pl.pallas_call(
        body, name="bwd_mlp", grid=(T // TM,),
        in_specs=[_rows(TM, D), _rows(TM, D), pl.BlockSpec((D_FF, D), lambda i: (0, 0), **once), _rows(TM, D_FF),
                  pl.BlockSpec(wg1.shape, lambda i: (0, 0, 0), **once), _rows(TM, D), _full((1, D)), HBM],
        out_specs=[_rows(TM, D_FF), _rows(TM, D), _rows(TM, D), _full((1, D))],
        out_shape=[jax.ShapeDtypeStruct((T, D_FF), BF16), jax.ShapeDtypeStruct((T, D), F32),
                   jax.ShapeDtypeStruct((T, D), BF16), jax.ShapeDtypeStruct((1, D), F32)],
        compiler_params=_cparams(56),
    )(*map(_hbm, (dxb, dres, w2, r, wg1, x1, g2, dep)))


def _mixer_bwd_a(dxb, wout, proj, c, lg, lb, wm, bst, clg, clb, dep):
    tm = TM_SGU
    n_tiles = T // tm

    def body(dx_ref, wo_ref, p_ref, c_ref, lg_ref, lb_ref, wm_ref, bs_ref, clg_ref, clb_ref, dep_ref,
             dpa_ref, dc_ref, dlg_ref, dlb_ref, dwm_ref, dbs_ref, dcb_ref, dclg_ref, dclb_ref,
             dv_buf, db_acc):
        i = pl.program_id(0)

        @pl.when(i == 0)
        def _():
            for ref in (dlg_ref, dlb_ref, dwm_ref, dbs_ref, dcb_ref, dclg_ref, dclb_ref, db_acc):
                ref[...] = jnp.zeros(ref.shape, F32)

        dmix = _dot_nt(dx_ref[...], wo_ref[...])
        ua = p_ref[:, 0:D_A]
        va = p_ref[:, D_A:2 * D_A]
        u, cdf_u = _gelu(ua)
        vg, cdf_v = _gelu(va)
        xhat, rstd = _ln(vg)
        v = (xhat * lg_ref[...] + lb_ref[...]).astype(BF16)
        mask = _tril_mask()
        for h in range(H_A):
            hc = slice(h * CHUNK, (h + 1) * CHUNK)
            wmh = jnp.where(mask, wm_ref[h], 0.0).astype(BF16)
            for cidx in range(tm // CHUNK):
                rc = slice(cidx * CHUNK, (cidx + 1) * CHUNK)
                vb = v[rc, hc]
                mixed = _dot(wmh, vb) + bs_ref[:, h:h + 1]
                da = dmix[rc, hc]
                dpa_ref[rc, hc] = (da * mixed * _gelu_grad(ua[rc, hc], cdf_u[rc, hc])).astype(BF16)
                dmixed = da * u[rc, hc]
                dmb = dmixed.astype(BF16)
                dv_buf[rc, hc] = _dot_tn(wmh, dmb)
                dwm_ref[h] += _dot_nt(dmb, vb)
                db_acc[:, hc] += dmixed
        dv = dv_buf[...]
        dlb_ref[...] += jnp.sum(dv, axis=0, keepdims=True)
        dlg_ref[...] += jnp.sum(dv * xhat, axis=0, keepdims=True)
        dvg = _ln_bwd(dv * lg_ref[...], xhat, rstd)
        dpa_ref[:, D_A:2 * D_A] = (dvg * _gelu_grad(va, cdf_v)).astype(BF16)

        for q in range(H_B):
            cols = slice(q * LANES, (q + 1) * LANES)
            chat, crstd = _ln(c_ref[:, cols])
            z = chat * clg_ref[:, cols] + clb_ref[:, cols]
            sg = _sigmoid(z)
            dz = dmix[:, D_A + q * LANES:D_A + (q + 1) * LANES] * (sg * (1.0 + z * (1.0 - sg)))
            dclb_ref[:, cols] += jnp.sum(dz, axis=0, keepdims=True)
            dclg_ref[:, cols] += jnp.sum(dz * chat, axis=0, keepdims=True)
            dc = _ln_bwd(dz * clg_ref[:, cols], chat, crstd)
            dc_ref[:, cols] = dc
            dcb_ref[:, cols] += jnp.sum(dc, axis=0, keepdims=True)

        @pl.when(i == n_tiles - 1)
        def _():
            for h in range(H_A):
                hc = slice(h * CHUNK, (h + 1) * CHUNK)
                dwm_ref[h] = jnp.where(mask, dwm_ref[h], 0.0)
                dbs_ref[h:h + 1, :] = jnp.sum(db_acc[:, hc].T, axis=0, keepdims=True)

    vec = _full((1, D_A))
    vshape = jax.ShapeDtypeStruct((1, D_A), F32)
    return pl.pallas_call(
        body, name="mixer_bwd_a", grid=(n_tiles,),
        in_specs=[_rows(tm, D), _full((D, D)), _rows(tm, IN_COLS), _rows(tm, D_B), vec, vec,
                  _full((H_A, CHUNK, CHUNK)), _full((CHUNK, H_A)), vec, vec, HBM],
        out_specs=[_rows(tm, 2 * D_A), _rows(tm, D_B), vec, vec, _full((H_A, CHUNK, CHUNK)),
                   _full((H_A, CHUNK)), vec, vec, vec],
        out_shape=[jax.ShapeDtypeStruct((T, 2 * D_A), BF16), jax.ShapeDtypeStruct((T, D_B), F32), vshape, vshape,
                   jax.ShapeDtypeStruct((H_A, CHUNK, CHUNK), F32), jax.ShapeDtypeStruct((H_A, CHUNK), F32),
                   vshape, vshape, vshape],
        scratch_shapes=[pltpu.VMEM((tm, D_A), F32), pltpu.VMEM((CHUNK, D_A), F32)],
        compiler_params=_cparams(32),
    )(*map(_hbm, (dxb, wout, proj, c, lg, lb, wm, bst, clg, clb, dep)))


def _mixer_bwd_b(dc, g, proj, cw, dpa, wg_in, x, g1, dres):
    tm = TM_SGU
    n_tiles = T // tm
    hb = tm // HALO
    bn = wg_in.shape[2]

    def body(dc_ref, dch_ref, g_ref, p_ref, cw_ref, dpa_ref, win_ref, x_ref, g1_ref, dres_ref,
             dp_ref, dcw_ref, dx_ref, dxb_ref, dg1_ref, dcbuf, dwacc):
        i = pl.program_id(0)

        @pl.when(i == 0)
        def _():
            dwacc[...] = jnp.zeros(dwacc.shape, F32)
            dg1_ref[...] = jnp.zeros(dg1_ref.shape, F32)

        _fill_shift_buffer(dcbuf, 0, dc_ref[...])
        _fill_shift_buffer(dcbuf, tm, jnp.where(i < n_tiles - 1, dch_ref[...], 0.0))
        _build_shifts(dcbuf)
        dp_ref[:, 0:2 * D_A] = dpa_ref[...]
        for q in range(H_B):
            cols = slice(q * LANES, (q + 1) * LANES)

            def row_block(rb, carry, q=q, cols=cols):
                base = pl.multiple_of(rb * RB, RB)
                rows = pl.ds(base, RB)
                gv = g_ref[rows, cols]
                acc = [jnp.zeros((RB, LANES), F32) for _ in range(CONV_CHAINS)]
                parts = []
                for k in range(CONV_W):
                    xk = _shifted(dcbuf, q, base, CONV_W - 1 - k)
                    acc[k % CONV_CHAINS] = acc[k % CONV_CHAINS] + xk * cw_ref[k:k + 1, cols]
                    parts.append(jnp.sum((gv * xk).reshape(RB // 8, 8, LANES), axis=0))
                dg = functools.reduce(lambda a, b: a + b, acc)
                val = p_ref[rows, cols]
                sg = _sigmoid(p_ref[rows, D_B + q * LANES:D_B + (q + 1) * LANES])
                dp_ref[rows, 2 * D_A + q * LANES:2 * D_A + (q + 1) * LANES] = (dg * sg).astype(BF16)
                dp_ref[rows, 2 * D_A + D_B + q * LANES:2 * D_A + D_B + (q + 1) * LANES] = (
                    dg * val * sg * (1.0 - sg)).astype(BF16)
                for k in range(CONV_W):
                    dwacc[k * 8:(k + 1) * 8, cols] += parts[k]
                return carry

            lax.fori_loop(0, tm // RB, row_block, 0)

        dh = jnp.zeros((tm, D), F32)
        for j in range(N_DEV):
            dh = dh + _dot_nt(dp_ref[:, j * bn:(j + 1) * bn], win_ref[j])
        xv = x_ref[...]
        dxn, dg = _rms_bwd(dh, xv, _rms(xv), g1_ref[...])
        dx = dres_ref[...] + dxn
        dx_ref[...] = dx
        dxb_ref[...] = dx.astype(BF16)
        dg1_ref[...] += dg

        @pl.when(i == n_tiles - 1)
        def _():
            for k in range(CONV_W):
                dcw_ref[k:k + 1, :] = jnp.sum(dwacc[k * 8:(k + 1) * 8, :], axis=0, keepdims=True)

    return pl.pallas_call(
        body, name="mixer_bwd_b", grid=(n_tiles,),
        in_specs=[_rows(tm, D_B),
                  pl.BlockSpec((HALO, D_B), lambda i: (jnp.minimum((i + 1) * hb, T // HALO - 1), 0)),
                  _rows(tm, D_B),
                  pl.BlockSpec((tm, 2 * D_B), lambda i: (i, 1)),
                  _full((CONV_W, D_B)), _rows(tm, 2 * D_A),
                  pl.BlockSpec(wg_in.shape, lambda i: (0, 0, 0), pipeline_mode=pl.Buffered(1)),
                  _rows(tm, D), _full((1, D)), _rows(tm, D)],
        out_specs=[_rows(tm, IN_COLS), _full((CONV_W, D_B)), _rows(tm, D), _rows(tm, D), _full((1, D))],
        out_shape=[jax.ShapeDtypeStruct((T, IN_COLS), BF16), jax.ShapeDtypeStruct((CONV_W, D_B), F32),
                   jax.ShapeDtypeStruct((T, D), F32), jax.ShapeDtypeStruct((T, D), BF16),
                   jax.ShapeDtypeStruct((1, D), F32)],
        scratch_shapes=[pltpu.VMEM((8, H_B, tm + HALO, LANES), F32), pltpu.VMEM((CONV_W * 8, D_B), F32)],
        compiler_params=_cparams(40),
    )(*map(_hbm, (dc, dc, g, proj, cw, dpa, wg_in, x, g1, dres)))


def _wgrad(name, a, g, bn, square_a=False):
    k = a.shape[1]
    n = g.shape[1]
    tk = min(k, 1024)
    tn = min(n, max(bn, 1024))
    nsub = tn // bn
    tt = TT_WGRAD
    nt = T // tt

    def body(a_ref, g_ref, o_ref, ob_ref):
        t = pl.program_id(2)

        @pl.when(t == 0)
        def _():
            o_ref[...] = jnp.zeros(o_ref.shape, F32)

        av = a_ref[...]
        if square_a:
            av = av * av
        for s in range(nsub):
            o_ref[s] += _dot_tn(av, g_ref[:, s * bn:(s + 1) * bn])

        @pl.when(t == nt - 1)
        def _():
            ob_ref[...] = o_ref[...].astype(BF16)

    ospec = pl.BlockSpec((nsub, tk, bn), lambda ki, ni, ti: (ni, ki, 0))
    return pl.pallas_call(
        body, name=name, grid=(k // tk, n // tn, nt),
        in_specs=[pl.BlockSpec((tt, tk), lambda ki, ni, ti: (ti, ki)),
                  pl.BlockSpec((tt, tn), lambda ki, ni, ti: (ti, ni))],
        out_specs=[ospec, ospec],
        out_shape=[jax.ShapeDtypeStruct((n // bn, k, bn), F32), jax.ShapeDtypeStruct((n // bn, k, bn), BF16)],
        compiler_params=_cparams(40, ("arbitrary", "arbitrary", "arbitrary")),
    )(a, g)


def _coords():
    return lax.axis_index("x"), lax.axis_index("y"), lax.axis_index("c")


def _flat(x, y, c):
    return 4 * x + 2 * y + c


def _peer(k):
    x, y, c = _coords()
    return (x ^ ((k >> 2) & 1), y ^ ((k >> 1) & 1), c ^ (k & 1))


def _hbm(a):
    return pltpu.with_memory_space_constraint(a, pltpu.HBM)


def _hbm_like(a):
    return pltpu.HBM(a.shape, a.dtype)


def _peer_sems():
    return pltpu.SemaphoreType.DMA((N_DEV - 1,))


def _place_own(shards):
    n = len(shards)
    shapes = [(s.shape if l is None else s.shape[1:]) for s, l, _ in shards]

    def body(*refs):
        ins, outs = refs[:n], refs[n:2 * n]
        stage_in, stage_out = refs[2 * n:3 * n], refs[3 * n:4 * n]
        in_sems, out_sems = refs[4 * n], refs[4 * n + 1]
        me = _flat(*_coords())
        srcs = [ins[a] if shards[a][1] is None else ins[a].at[shards[a][1]] for a in range(n)]
        loads = [pltpu.make_async_copy(srcs[a], stage_in[a], in_sems.at[a]) for a in range(n)]
        stores = [pltpu.make_async_copy(stage_out[a], outs[a].at[me], out_sems.at[a]) for a in range(n)]
        for cp in loads:
            cp.start()
        for a in range(n):
            loads[a].wait()
            stage_out[a][...] = stage_in[a][...].astype(stage_out[a].dtype)
            stores[a].start()
        for cp in stores:
            cp.wait()

    return pl.pallas_call(
        body, name="place_own", in_specs=[HBM] * n, out_specs=[HBM] * n,
        out_shape=[pltpu.HBM((N_DEV,) + shapes[a], shards[a][2]) for a in range(n)],
        scratch_shapes=[*[pltpu.VMEM(shapes[a], shards[a][0].dtype) for a in range(n)],
                        *[pltpu.VMEM(shapes[a], shards[a][2]) for a in range(n)],
                        pltpu.SemaphoreType.DMA((n,)), pltpu.SemaphoreType.DMA((n,))],
        compiler_params=pltpu.CompilerParams(vmem_limit_bytes=40 * MIB),
    )(*[_hbm(s) for s, _, _ in shards])


SIBLING = 1
CHIP_PEERS = (2, 4, 6)
FIRST_PEERS = (SIBLING,) + CHIP_PEERS


def _gather_start(lands):
    n = len(lands)

    def body(*refs):
        lnd, send, recv = refs[:n], refs[n:2 * n], refs[2 * n:3 * n]
        me = _flat(*_coords())
        for a in range(n):
            for j, k in enumerate(FIRST_PEERS):
                pltpu.make_async_remote_copy(
                    src_ref=lnd[a].at[me], dst_ref=lnd[a].at[me], send_sem=send[a].at[j],
                    recv_sem=recv[a].at[j], device_id=_peer(k), device_id_type=MESH_ID).start()

    sems = pltpu.SemaphoreType.DMA((len(FIRST_PEERS),))
    outs = pl.pallas_call(
        body, name="gather_start",
        out_shape=(*[sems] * (2 * n), *[_hbm_like(l) for l in lands]),
        in_specs=[HBM] * n, out_specs=(*[SEM] * (2 * n), *[HBM] * n),
        input_output_aliases={i: 2 * n + i for i in range(n)},
        compiler_params=pltpu.CompilerParams(has_side_effects=EFFECT),
    )(*[_hbm(l) for l in lands])
    return outs[:n], outs[n:2 * n], outs[2 * n:]


def _gather_pass_on(name, lands, sends, recvs, after):
    n = len(lands)

    def body(*refs):
        lnd, send, recv = refs[:n], refs[n:2 * n], refs[2 * n:3 * n]
        send2, recv2 = refs[3 * n + 1:4 * n + 1], refs[4 * n + 1:5 * n + 1]
        sibling = _peer(SIBLING)
        for a in range(n):
            for j, k in enumerate(CHIP_PEERS):
                cp = pltpu.make_async_remote_copy(
                    src_ref=lnd[a].at[0], dst_ref=lnd[a].at[0], send_sem=send[a].at[1 + j],
                    recv_sem=recv[a].at[1 + j], device_id=_peer(k), device_id_type=MESH_ID)
                cp.wait_send()
                cp.wait_recv()
                blk = _flat(*_peer(k))
                pltpu.make_async_remote_copy(
                    src_ref=lnd[a].at[blk], dst_ref=lnd[a].at[blk], send_sem=send2[a].at[j],
                    recv_sem=recv2[a].at[j], device_id=sibling, device_id_type=MESH_ID).start()

    sems = pltpu.SemaphoreType.DMA((len(CHIP_PEERS),))
    outs = pl.pallas_call(
        body, name=name, out_shape=(*[sems] * (2 * n), *[_hbm_like(l) for l in lands]),
        in_specs=(*[HBM] * n, *[SEM] * (2 * n), HBM), out_specs=(*[SEM] * (2 * n), *[HBM] * n),
        input_output_aliases={i: 2 * n + i for i in range(n)},
        compiler_params=pltpu.CompilerParams(has_side_effects=EFFECT),
    )(*lands, *sends, *recvs, after)
    return outs[:n], outs[n:2 * n], outs[2 * n:]


def _gather_wait(name, lands, sends, recvs, sends2, recvs2, after):
    n = len(lands)

    def body(*refs):
        lnd, send, recv = refs[:n], refs[n:2 * n], refs[2 * n:3 * n]
        send2, recv2 = refs[3 * n:4 * n], refs[4 * n:5 * n]
        sibling = _peer(SIBLING)
        for a in range(n):
            own = pltpu.make_async_remote_copy(
                src_ref=lnd[a].at[0], dst_ref=lnd[a].at[0], send_sem=send[a].at[0], recv_sem=recv[a].at[0],
                device_id=sibling, device_id_type=MESH_ID)
            own.wait_send()
            own.wait_recv()
            for j in range(len(CHIP_PEERS)):
                cp = pltpu.make_async_remote_copy(
                    src_ref=lnd[a].at[0], dst_ref=lnd[a].at[0], send_sem=send2[a].at[j], recv_sem=recv2[a].at[j],
                    device_id=sibling, device_id_type=MESH_ID)
                cp.wait_send()
                cp.wait_recv()

    return pl.pallas_call(
        body, name=name, out_shape=[_hbm_like(l) for l in lands],
        in_specs=(*[HBM] * n, *[SEM] * (4 * n), HBM), out_specs=[HBM] * n,
        input_output_aliases={i: i for i in range(n)},
        compiler_params=pltpu.CompilerParams(has_side_effects=EFFECT),
    )(*lands, *sends, *recvs, *sends2, *recvs2, after)


def _scatter_start(name, grads, lands, layer):
    n = len(grads)

    def body(*refs):
        g, lnd, send, recv = refs[:n], refs[n:2 * n], refs[2 * n:3 * n], refs[3 * n:4 * n]
        for a in range(n):
            for k in range(1, N_DEV):
                to = _peer(k)
                pltpu.make_async_remote_copy(
                    src_ref=g[a].at[_flat(*to)], dst_ref=lnd[a].at[layer, k - 1], send_sem=send[a].at[k - 1],
                    recv_sem=recv[a].at[k - 1], device_id=to, device_id_type=MESH_ID).start()

    outs = pl.pallas_call(
        body, name=name,
        out_shape=(*[_peer_sems()] * (2 * n), *[_hbm_like(g) for g in grads], *[_hbm_like(l) for l in lands]),
        in_specs=[HBM] * (2 * n), out_specs=(*[SEM] * (2 * n), *[HBM] * (2 * n)),
        input_output_aliases={i: 2 * n + i for i in range(2 * n)},
        compiler_params=pltpu.CompilerParams(has_side_effects=EFFECT),
    )(*[_hbm(g) for g in grads], *[_hbm(l) for l in lands])
    return outs[:n], outs[n:2 * n], outs[2 * n:3 * n], outs[3 * n:]


def _scatter_wait(name, grads, lands, sends, recvs, after):
    n, nw = len(grads), len(lands)

    def body(*refs):
        g, lnd = refs[:n], refs[n:n + nw]
        send, recv = refs[n + nw:2 * n + nw], refs[2 * n + nw:3 * n + nw]
        for a in range(n):
            for k in range(1, N_DEV):
                cp = pltpu.make_async_remote_copy(
                    src_ref=g[a].at[0], dst_ref=lnd[a // DEPTH].at[a % DEPTH, 0],
                    send_sem=send[a].at[k - 1], recv_sem=recv[a].at[k - 1],
                    device_id=_peer(k), device_id_type=MESH_ID)
                cp.wait_send()
                cp.wait_recv()

    outs = pl.pallas_call(
        body, name=name,
        out_shape=(*[_hbm_like(g) for g in grads], *[_hbm_like(l) for l in lands]),
        in_specs=(*[HBM] * (n + nw), *[SEM] * (2 * n), HBM), out_specs=[HBM] * (n + nw),
        input_output_aliases={i: i for i in range(n + nw)},
        compiler_params=pltpu.CompilerParams(has_side_effects=EFFECT),
    )(*grads, *lands, *sends, *recvs, after)
    return outs[n:]


def _all_reduce_small(part, deps):
    rows = part.shape[0]
    br = rows // N_DEV
    assert br * N_DEV == rows and br % 8 == 0
    nd = len(deps)

    def body(p_ref, *refs):
        o_ref, slots, send1, recv1, send2, recv2 = refs[nd:]
        me = _flat(*_coords())

        def block(ref, d):
            return ref.at[pl.ds(pl.multiple_of(d * br, 8), br), :]

        slots[me] = p_ref[pl.ds(pl.multiple_of(me * br, 8), br), :]
        scatter, gather = [], []
        for k in range(1, N_DEV):
            to = _peer(k)
            scatter.append(pltpu.make_async_remote_copy(
                src_ref=block(p_ref, _flat(*to)), dst_ref=slots.at[me],
                send_sem=send1.at[k - 1], recv_sem=recv1.at[k - 1], device_id=to, device_id_type=MESH_ID))
            gather.append(pltpu.make_async_remote_copy(
                src_ref=block(o_ref, me), dst_ref=block(o_ref, me),
                send_sem=send2.at[k - 1], recv_sem=recv2.at[k - 1], device_id=to, device_id_type=MESH_ID))
        for cp in scatter:
            cp.start()
        for cp in scatter:
            cp.wait()
        acc = slots[0]
        for d in range(1, N_DEV):
            acc = acc + slots[d]
        o_ref[pl.ds(pl.multiple_of(me * br, 8), br), :] = acc
        for cp in gather:
            cp.start()
        for cp in gather:
            cp.wait()

    return pl.pallas_call(
        body, name="all_reduce_small",
        in_specs=[pl.BlockSpec(memory_space=pltpu.VMEM)] + [HBM] * nd, out_specs=pl.BlockSpec(memory_space=pltpu.VMEM),
        out_shape=jax.ShapeDtypeStruct(part.shape, F32),
        scratch_shapes=[pltpu.VMEM((N_DEV, br, LANES), F32)] + [pltpu.SemaphoreType.DMA((N_DEV - 1,))] * 4,
        compiler_params=pltpu.CompilerParams(vmem_limit_bytes=24 * MIB),
    )(part, *map(_hbm, deps))


def _adam_math(w, g, m, v):
    m = ADAM_B1 * m + (1.0 - ADAM_B1) * g
    v = ADAM_B2 * v + (1.0 - ADAM_B2) * (g * g)
    m_hat = m / (1.0 - ADAM_B1 ** ADAM_STEP)
    v_hat = v / (1.0 - ADAM_B2 ** ADAM_STEP)
    delta = -ADAM_LR * (m_hat / (jnp.sqrt(v_hat) + ADAM_EPS) + ADAM_WD * w)
    return delta, m, v


def _adam_sharded(name, me, g32, recv, w, m, v):
    _, r, c = w.shape
    tr = min(r, 512)
    nblk = r // tr
    assert len(g32) == DEPTH == 2

    def body(me_ref, g0_ref, g1_ref, recv_ref, w_ref, m_ref, v_ref, g_ref, d_ref, nm_ref, nv_ref):
        g = jnp.where(pl.program_id(0) == 0, g0_ref[...], g1_ref[...])
        for k in range(N_DEV - 1):
            g = g + recv_ref[k].astype(F32)
        delta, nm, nv = _adam_math(w_ref[...], g, m_ref[...], v_ref[...])
        g_ref[...] = g
        d_ref[...] = delta
        nm_ref[...] = nm
        nv_ref[...] = nv

    blk = pl.BlockSpec((None, tr, c), lambda l, i, me_ref: (l, i, 0))
    own0 = pl.BlockSpec((None, tr, c), lambda l, i, me_ref: (me_ref[0], jnp.where(l == 0, i, nblk - 1), 0))
    own1 = pl.BlockSpec((None, tr, c), lambda l, i, me_ref: (me_ref[0], jnp.where(l == 1, i, 0), 0))
    shp = jax.ShapeDtypeStruct(w.shape, F32)
    return pl.pallas_call(
        body, name=name,
        grid_spec=pltpu.PrefetchScalarGridSpec(
            num_scalar_prefetch=1, grid=(DEPTH, nblk),
            in_specs=[own0, own1, pl.BlockSpec((None, N_DEV - 1, tr, c), lambda l, i, me_ref: (l, 0, i, 0)),
                      blk, blk, blk],
            out_specs=[blk] * 4),
        out_shape=[shp] * 4,
        compiler_params=_cparams(32, ("arbitrary", "arbitrary")),
    )(me, *[_hbm(a) for a in (*g32, recv, w, m, v)])


def _adam_small(gs, ws, ms, vs):
    n = len(gs)

    def body(*refs):
        g, w, m, v = refs[:n], refs[n:2 * n], refs[2 * n:3 * n], refs[3 * n:4 * n]
        d, nm, nv = refs[4 * n:5 * n], refs[5 * n:6 * n], refs[6 * n:7 * n]
        for a in range(n):
            delta, new_m, new_v = _adam_math(w[a][...], g[a][...], m[a][...], v[a][...])
            d[a][...] = delta
            nm[a][...] = new_m
            nv[a][...] = new_v

    specs = [_full(a.shape) for a in gs]
    shapes = [jax.ShapeDtypeStruct(a.shape, F32) for a in gs]
    outs = pl.pallas_call(
        body, name="adam_small", grid=(1,),
        in_specs=specs * 4, out_specs=specs * 3, out_shape=shapes * 3,
        compiler_params=_cparams(24),
    )(*gs, *ws, *ms, *vs)
    return outs[:n], outs[n:2 * n], outs[2 * n:]


SMALL = ["norm1_g", "sgu_ln_g", "sgu_ln_b", "sgu_w", "sgu_b", "conv_b", "conv_ln_g", "conv_ln_b", "norm2_g",
         "final_g"]


def _pack(arrays):
    flat = jnp.concatenate([a.reshape(-1) for a in arrays])
    pad = (-flat.shape[0]) % (N_DEV * 8 * LANES)
    return jnp.pad(flat, (0, pad)).reshape(-1, LANES)


def _unpack(packed, shapes):
    flat = packed.reshape(-1)
    out, off = [], 0
    for s in shapes:
        size = 1
        for d in s:
            size *= d
        out.append(flat[off:off + size].reshape(s))
        off += size
    return out


def kernel(x, norm1_g, w_in, sgu_ln_g, sgu_ln_b, sgu_w, sgu_b, conv_w, conv_b, conv_ln_g, conv_ln_b, w_out, norm2_g, w_ff1, w_ff2, final_g, loss_target, m_norm1_g, m_w_in, m_sgu_ln_g, m_sgu_ln_b, m_sgu_w, m_sgu_b, m_conv_w, m_conv_b, m_conv_ln_g, m_conv_ln_b, m_w_out, m_norm2_g, m_w_ff1, m_w_ff2, m_final_g, v_norm1_g, v_w_in, v_sgu_ln_g, v_sgu_ln_b, v_sgu_w, v_sgu_b, v_conv_w, v_conv_b, v_conv_ln_g, v_conv_ln_b, v_w_out, v_norm2_g, v_w_ff1, v_w_ff2, v_final_g):
    x2d = x.reshape(T, D)
    tgt = loss_target.reshape(T, D)
    cw_shard = conv_w.reshape(CONV_W, LANES)

    gnames, shards = [], []
    for l in range(DEPTH):
        for k, w in (("w_in", w_in), ("w_out", w_out), ("w_ff1", w_ff1), ("w_ff2", w_ff2)):
            gnames.append(f"{k}{l}")
            shards.append((w, l, BF16))
        if l == 0:
            gnames.insert(1, "conv_w")
            shards.insert(1, (cw_shard, None, F32))
    sends, recvs, lands = _gather_start(_place_own(shards))
    gidx = {k: i for i, k in enumerate(gnames)}

    passed = {}

    def pass_on(ks, after):
        idx = [gidx[k] for k in ks]
        s2, r2, ld = _gather_pass_on("gather_pass_on_" + "_".join(ks), [lands[i] for i in idx],
                                     [sends[i] for i in idx], [recvs[i] for i in idx], after)
        for j, k in enumerate(ks):
            passed[k] = (s2[j], r2[j], ld[j])

    def gathered(ks, after):
        idx = [gidx[k] for k in ks]
        return _gather_wait("gather_wait_" + "_".join(ks), [passed[k][2] for k in ks], [sends[i] for i in idx],
                            [recvs[i] for i in idx], [passed[k][0] for k in ks], [passed[k][1] for k in ks], after)

    saved = []
    xl = x2d
    cw_full = None
    for l in range(DEPTH):
        g1 = norm1_g[l].reshape(1, D)
        g2 = norm2_g[l].reshape(1, D)
        lg, lb = sgu_ln_g[l].reshape(1, D_A), sgu_ln_b[l].reshape(1, D_A)
        bst = sgu_b[l].T
        cb = conv_b[l].reshape(1, D_B)
        clg, clb = conv_ln_g[l].reshape(1, D_B), conv_ln_b[l].reshape(1, D_B)
        first = [f"w_in{l}"] if l == 0 else [f"w_in{l}", f"w_out{l}"]
        pass_on(first, xl)
        wg_in, *rest = gathered(first, xl)
        h, proj = _fwd_in(xl, g1, wg_in)
        if l == 0:
            pass_on(["conv_w", "w_out0"], h)
            cw_g, wout = gathered(["conv_w", "w_out0"], proj)
            cw_full = cw_g.reshape(N_DEV, DEPTH, CONV_W, D_B // N_DEV).transpose(1, 2, 0, 3).reshape(
                DEPTH, CONV_W, D_B)
        else:
            (wout,) = rest
            pass_on([f"w_ff1{l}"], proj)
        wout = wout.reshape(D, D)
        mix, c, g, x1, h2 = _mixer_fwd(proj, lg, lb, sgu_w[l], bst, cw_full[l], cb, clg, clb, wout, xl, g2)
        pass_on([f"w_ff1{l}"] if l == 0 else [f"w_ff2{l}"], h2)
        (wg_ff1,) = gathered([f"w_ff1{l}"], h2)
        r = _fwd_ff1(h2, wg_ff1)
        if l == 0:
            pass_on(["w_ff20"], r)
        (w2,) = gathered([f"w_ff2{l}"], r)
        w2 = w2.reshape(D_FF, D)
        saved.append(dict(x=xl, h=h, proj=proj, mix=mix, c=c, g=g, x1=x1, h2=h2, r=r, wg_in=wg_in, wout=wout,
                          wg_ff1=wg_ff1, w2=w2, g1=g1, g2=g2, lg=lg, lb=lb, bst=bst, clg=clg, clb=clb))
        if l + 1 < DEPTH:
            xl = _fwd_ff2(r, w2, x1)
        else:
            dx, dxb, loss_part, d_final_g = _fwd_ff2_loss(r, w2, x1, final_g.reshape(1, D), tgt)

    names = ["w_in", "w_out", "w_ff1", "w_ff2"]
    block = dict(w_in=(D, IN_COLS // N_DEV), w_out=(D // N_DEV, D), w_ff1=(D, D_FF // N_DEV), w_ff2=(D_FF // N_DEV, D))
    land = {k: lax.empty((DEPTH, N_DEV - 1) + block[k], BF16) for k in names}
    big32 = {k: [None] * DEPTH for k in names}
    big16 = {k: [None] * DEPTH for k in names}
    ssend = {k: [None] * DEPTH for k in names}
    srecv = {k: [None] * DEPTH for k in names}

    def send_grads(l, grads):
        ks = list(grads)
        for k in ks:
            big32[k][l] = grads[k][0].reshape((N_DEV,) + block[k])
        sends, recvs, g16s, lands = _scatter_start(
            "scatter_start_" + "_".join(ks) + str(l), [grads[k][1].reshape((N_DEV,) + block[k]) for k in ks],
            [land[k] for k in ks], l)
        for i, k in enumerate(ks):
            ssend[k][l], srecv[k][l], big16[k][l], land[k] = sends[i], recvs[i], g16s[i], lands[i]

    small = {}
    d_conv_w = [None] * DEPTH
    for l in reversed(range(DEPTH)):
        s = saved[l]
        send_grads(l, dict(w_ff2=_wgrad("wgrad_ff2", s["r"], dxb, D, square_a=True)))
        df1, dx1, dx1b, dg2 = _bwd_mlp(dxb, dx, s["w2"], s["r"], s["wg_ff1"], s["x1"], s["g2"], big16["w_ff2"][l])
        send_grads(l, dict(w_ff1=_wgrad("wgrad_ff1", s["h2"], df1, D_FF // N_DEV),
                           w_out=_wgrad("wgrad_out", s["mix"], dx1b, D)))
        dpa, dc, dlg, dlb, dwm, dbs, dcb, dclg, dclb = _mixer_bwd_a(
            dx1b, s["wout"], s["proj"], s["c"], s["lg"], s["lb"], sgu_w[l], s["bst"], s["clg"], s["clb"],
            big16["w_out"][l])
        dproj, d_conv_w[l], dx, dxb, dg1 = _mixer_bwd_b(dc, s["g"], s["proj"], cw_full[l], dpa, s["wg_in"], s["x"],
                                                         s["g1"], dx1)
        send_grads(l, dict(w_in=_wgrad("wgrad_in", s["h"], dproj, IN_COLS // N_DEV)))
        small[l] = dict(norm1_g=dg1, sgu_ln_g=dlg, sgu_ln_b=dlb, sgu_w=dwm, sgu_b=dbs, conv_b=dcb, conv_ln_g=dclg,
                        conv_ln_b=dclb, norm2_g=dg2)
    grad_x = dx.reshape(1, T, D)

    me_block = _flat(*_coords()).astype(jnp.int32).reshape(1)
    wmv = dict(w_in=(w_in, m_w_in, v_w_in), w_out=(w_out, m_w_out, v_w_out), w_ff1=(w_ff1, m_w_ff1, v_w_ff1),
               w_ff2=(w_ff2, m_w_ff2, v_w_ff2))
    res = {}

    def update(tag, ks, after):
        recv = _scatter_wait("scatter_wait_" + tag, [g for k in ks for g in big16[k]], [land[k] for k in ks],
                             [q for k in ks for q in ssend[k]], [q for k in ks for q in srecv[k]], after)
        for k, rk in zip(ks, recv):
            res[k] = _adam_sharded("adam_" + k, me_block, big32[k], rk, *wmv[k])

    update("early", ["w_ff2", "w_ff1", "w_out"], big16["w_in"][0])

    rep = dict(norm1_g=(norm1_g, m_norm1_g, v_norm1_g), sgu_ln_g=(sgu_ln_g, m_sgu_ln_g, v_sgu_ln_g),
               sgu_ln_b=(sgu_ln_b, m_sgu_ln_b, v_sgu_ln_b), sgu_w=(sgu_w, m_sgu_w, v_sgu_w),
               sgu_b=(sgu_b, m_sgu_b, v_sgu_b), conv_b=(conv_b, m_conv_b, v_conv_b),
               conv_ln_g=(conv_ln_g, m_conv_ln_g, v_conv_ln_g), conv_ln_b=(conv_ln_b, m_conv_ln_b, v_conv_ln_b),
               norm2_g=(norm2_g, m_norm2_g, v_norm2_g), final_g=(final_g, m_final_g, v_final_g))
    parts = []
    for k in SMALL:
        if k == "final_g":
            parts.append(d_final_g.reshape(rep[k][0].shape))
        else:
            parts.append(jnp.stack([small[l][k].reshape(rep[k][0].shape[1:]) for l in range(DEPTH)]))
    parts.append(jnp.stack(d_conv_w))
    parts.append(loss_part[0, 0:1])
    shapes = [p.shape for p in parts]
    packed_sum = _all_reduce_small(_pack(parts), [res[k][0] for k in ("w_ff2", "w_ff1", "w_out")])
    update("late", ["w_in"], packed_sum)
    summed = _unpack(packed_sum, shapes)
    loss = summed[-1][0]
    me = _flat(*_coords())
    g_conv_w = lax.dynamic_slice_in_dim(summed[-2], me * (D_B // N_DEV), D_B // N_DEV, axis=2)
    g_small = summed[:-2] + [g_conv_w]
    keys = SMALL + ["conv_w"]
    rep["conv_w"] = (conv_w, m_conv_w, v_conv_w)
    def two_d(a):
        return a.reshape(1, -1) if a.ndim == 1 else a

    d_s, nm_s, nv_s = _adam_small([two_d(g) for g in g_small], *[[two_d(rep[k][i]) for k in keys] for i in range(3)])
    for i, k in enumerate(keys):
        shape = rep[k][0].shape
        res[k] = (g_small[i], d_s[i].reshape(shape), nm_s[i].reshape(shape), nv_s[i].reshape(shape))

    order = ["norm1_g", "w_in", "sgu_ln_g", "sgu_ln_b", "sgu_w", "sgu_b", "conv_w", "conv_b", "conv_ln_g",
             "conv_ln_b", "w_out", "norm2_g", "w_ff1", "w_ff2", "final_g"]
    return (loss, grad_x, *[res[k][0] for k in order], *[res[k][1] for k in order],
            *[res[k][2] for k in order], *[res[k][3] for k in order])
```

```python
import functools

import jax
import jax.numpy as jnp
from jax import lax
from jax.experimental import pallas as pl
from jax.experimental.pallas import tpu as pltpu

F32 = jnp.float32
BF16 = jnp.bfloat16

N_DEV = 8
DEPTH = 2
T = 4096
D = 1024
D_A = 512
D_B = 512
CHUNK = 128
H_A = 4
H_B = 4
CONV_W = 31
HALO = 32
D_FF = 4096
IN_COLS = 2048
EPS = 1e-6

ADAM_LR = 0.001
ADAM_B1 = 0.9
ADAM_B2 = 0.999
ADAM_EPS = 1e-08
ADAM_WD = 0.01
ADAM_STEP = 10

TM = 512
TM_FWD = 1024
TT_WGRAD = 4096
TM_SGU = 512
RB = 64
CONV_CHAINS = 2
LANES = 128
MIB = 1024 * 1024
SCOPED_VMEM_MIB = 60

SQRT_HALF = 0.7071067811865476
INV_SQRT_2PI = 0.3989422804014327

MESH_ID = pl.DeviceIdType.MESH
HBM = pl.BlockSpec(memory_space=pltpu.HBM)
SEM = pl.BlockSpec(memory_space=pltpu.SEMAPHORE)
EFFECT = pltpu.SideEffectType.DATAFLOW_SIDE_EFFECTING


def _cparams(vmem_mib, sem=("arbitrary",)):
    assert vmem_mib <= SCOPED_VMEM_MIB
    return pltpu.CompilerParams(dimension_semantics=sem, vmem_limit_bytes=SCOPED_VMEM_MIB * MIB)


def _full(shape):
    return pl.BlockSpec(shape, lambda *_: (0,) * len(shape))


def _rows(tm, cols):
    return pl.BlockSpec((tm, cols), lambda i: (i, 0))


def _gelu(x):
    cdf = 0.5 * (1.0 + lax.erf(x * SQRT_HALF))
    return x * cdf, cdf


def _gelu_grad(x, cdf):
    return cdf + x * (INV_SQRT_2PI * jnp.exp(-0.5 * x * x))


def _sigmoid(x):
    return 1.0 / (1.0 + jnp.exp(-x))


def _ln(x):
    mu = jnp.mean(x, axis=-1, keepdims=True)
    xc = x - mu
    rstd = lax.rsqrt(jnp.mean(xc * xc, axis=-1, keepdims=True) + EPS)
    return xc * rstd, rstd


def _ln_bwd(dyh, xhat, rstd):
    return rstd * (dyh - jnp.mean(dyh, axis=-1, keepdims=True) - xhat * jnp.mean(dyh * xhat, axis=-1, keepdims=True))


def _rms(x):
    return lax.rsqrt(jnp.mean(x * x, axis=-1, keepdims=True) + EPS)


def _rms_bwd(dh, x, r, g):
    n = x * r
    dn = dh * g
    dx = r * (dn - n * jnp.mean(dn * n, axis=-1, keepdims=True))
    return dx, jnp.sum(dh * n, axis=0, keepdims=True)


def _dot(a, b):
    return jnp.dot(a, b, preferred_element_type=F32)


def _dot_nt(a, b):
    return lax.dot_general(a, b, (((1,), (1,)), ((), ())), preferred_element_type=F32)


def _dot_tn(a, b):
    return lax.dot_general(a, b, (((0,), (0,)), ((), ())), preferred_element_type=F32)


def _tril_mask():
    r = lax.broadcasted_iota(jnp.int32, (CHUNK, CHUNK), 0)
    c = lax.broadcasted_iota(jnp.int32, (CHUNK, CHUNK), 1)
    return r >= c


def _fwd_in(x, g1, wg):
    bn = wg.shape[2]

    def body(x_ref, g_ref, w_ref, h_ref, p_ref):
        xv = x_ref[...]
        h = (xv * _rms(xv) * g_ref[...]).astype(BF16)
        h_ref[...] = h
        for j in range(N_DEV):
            p_ref[:, j * bn:(j + 1) * bn] = _dot(h, w_ref[j])

    return pl.pallas_call(
        body, name="fwd_in", grid=(T // TM_FWD,),
        in_specs=[_rows(TM_FWD, D), _full((1, D)), _full(wg.shape)],
        out_specs=[_rows(TM_FWD, D), _rows(TM_FWD, IN_COLS)],
        out_shape=[jax.ShapeDtypeStruct((T, D), BF16), jax.ShapeDtypeStruct((T, IN_COLS), F32)],
        compiler_params=_cparams(32),
    )(*map(_hbm, (x, g1, wg)))


def _fill_shift_buffer(sh_ref, row0, value):
    for q in range(sh_ref.shape[1]):
        sh_ref[0, q, row0:row0 + value.shape[0], :] = value[:, q * LANES:(q + 1) * LANES]


def _build_shifts(sh_ref):
    rows = sh_ref.shape[2]
    for p in range(1, 8):
        for q in range(sh_ref.shape[1]):
            sh_ref[p, q, 0:rows - 8, :] = sh_ref[0, q, p:p + rows - 8, :]


def _shifted(sh_ref, q, base, off):
    start = base + (off - off % 8)
    if not isinstance(start, int):
        start = pl.multiple_of(start, 8)
    return sh_ref[off % 8, q, pl.ds(start, RB), :]


def _conv_taps(sh_ref, w_ref, q, base, first_tap_row, step):
    cols = slice(q * LANES, (q + 1) * LANES)
    acc = [jnp.zeros((RB, LANES), F32) for _ in range(CONV_CHAINS)]
    for k in range(CONV_W):
        term = _shifted(sh_ref, q, base, first_tap_row + step * k) * w_ref[k:k + 1, cols]
        acc[k % CONV_CHAINS] = acc[k % CONV_CHAINS] + term
    return functools.reduce(lambda a, b: a + b, acc)


def _mixer_fwd(proj, lg, lb, wm, bst, cw, cb, clg, clb, wout, x, g2):
    tm = TM_SGU
    hb = tm // HALO

    def body(p_ref, ph_ref, lg_ref, lb_ref, wm_ref, bs_ref, cw_ref, cb_ref, clg_ref, clb_ref, wo_ref, x_ref, g2_ref,
             mix_ref, c_ref, g_ref, x1_ref, h2_ref, gbuf):
        i = pl.program_id(0)
        u, _ = _gelu(p_ref[:, 0:D_A])
        vg, _ = _gelu(p_ref[:, D_A:2 * D_A])
        xhat, _ = _ln(vg)
        v = (xhat * lg_ref[...] + lb_ref[...]).astype(BF16)
        mask = _tril_mask()
        for h in range(H_A):
            hc = slice(h * CHUNK, (h + 1) * CHUNK)
            wmh = jnp.where(mask, wm_ref[h], 0.0).astype(BF16)
            for c in range(tm // CHUNK):
                rc = slice(c * CHUNK, (c + 1) * CHUNK)
                mixed = _dot(wmh, v[rc, hc]) + bs_ref[:, h:h + 1]
                mix_ref[rc, hc] = (u[rc, hc] * mixed).astype(BF16)
        x1_ref[...] = x_ref[...] + _dot(mix_ref[:, 0:D_A], wo_ref[0:D_A, :])

        g = p_ref[:, 2 * D_A:2 * D_A + D_B] * _sigmoid(p_ref[:, 2 * D_A + D_B:IN_COLS])
        g_ref[...] = g
        gh = ph_ref[:, 0:D_B] * _sigmoid(ph_ref[:, D_B:2 * D_B])
        _fill_shift_buffer(gbuf, 0, jnp.where(i > 0, gh, 0.0))
        _fill_shift_buffer(gbuf, HALO, g)
        _build_shifts(gbuf)
        for q in range(H_B):
            cols = slice(q * LANES, (q + 1) * LANES)
            for rb in range(tm // RB):
                acc = _conv_taps(gbuf, cw_ref, q, rb * RB, HALO - (CONV_W - 1), 1)
                c_ref[rb * RB:(rb + 1) * RB, cols] = acc + cb_ref[:, cols]
        for q in range(H_B):
            cols = slice(q * LANES, (q + 1) * LANES)
            chat, _ = _ln(c_ref[:, cols])
            z = chat * clg_ref[:, cols] + clb_ref[:, cols]
            mix_ref[:, D_A + q * LANES:D_A + (q + 1) * LANES] = (z * _sigmoid(z)).astype(BF16)

        x1 = x1_ref[...] + _dot(mix_ref[:, D_A:D], wo_ref[D_A:D, :])
        x1_ref[...] = x1
        h2_ref[...] = (x1 * _rms(x1) * g2_ref[...]).astype(BF16)

    vec = _full((1, D_A))
    return pl.pallas_call(
        body, name="mixer_fwd", grid=(T // tm,),
        in_specs=[_rows(tm, IN_COLS),
                  pl.BlockSpec((HALO, 2 * D_B), lambda i: (jnp.maximum(i * hb - 1, 0), 1)),
                  vec, vec, _full((H_A, CHUNK, CHUNK)), _full((CHUNK, H_A)),
                  _full((CONV_W, D_B)), vec, vec, vec,
                  pl.BlockSpec((D, D), lambda i: (0, 0), pipeline_mode=pl.Buffered(1)), _rows(tm, D), _full((1, D))],
        out_specs=[_rows(tm, D), _rows(tm, D_B), _rows(tm, D_B), _rows(tm, D), _rows(tm, D)],
        out_shape=[jax.ShapeDtypeStruct((T, D), BF16), jax.ShapeDtypeStruct((T, D_B), F32),
                   jax.ShapeDtypeStruct((T, D_B), F32), jax.ShapeDtypeStruct((T, D), F32),
                   jax.ShapeDtypeStruct((T, D), BF16)],
        scratch_shapes=[pltpu.VMEM((8, H_B, HALO + tm, LANES), F32)],
        compiler_params=_cparams(40),
    )(*map(_hbm, (proj, proj, lg, lb, wm, bst, cw, cb, clg, clb, wout, x, g2)))


def _fwd_ff1(h2, wg):
    bn = wg.shape[2]

    def body(h_ref, w_ref, r_ref):
        h = h_ref[...]
        for j in range(N_DEV):
            r_ref[:, j * bn:(j + 1) * bn] = jnp.maximum(_dot(h, w_ref[j]), 0.0).astype(BF16)

    return pl.pallas_call(
        body, name="fwd_ff1", grid=(T // TM_FWD,),
        in_specs=[_rows(TM_FWD, D), _full(wg.shape)],
        out_specs=_rows(TM_FWD, D_FF),
        out_shape=jax.ShapeDtypeStruct((T, D_FF), BF16),
        compiler_params=_cparams(48),
    )(h2, wg)


def _fwd_ff2(r, w2, x1):
    def body(r_ref, w_ref, x_ref, o_ref):
        rv = r_ref[...]
        o_ref[...] = x_ref[...] + _dot(rv * rv, w_ref[...])

    return pl.pallas_call(
        body, name="fwd_ff2", grid=(T // TM_FWD,),
        in_specs=[_rows(TM_FWD, D_FF), _full((D_FF, D)), _rows(TM_FWD, D)],
        out_specs=_rows(TM_FWD, D),
        out_shape=jax.ShapeDtypeStruct((T, D), F32),
        compiler_params=_cparams(48),
    )(r, w2, x1)


def _fwd_ff2_loss(r, w2, x1, gf, tgt):
    def body(r_ref, w_ref, x_ref, g_ref, t_ref, dx_ref, dxb_ref, loss_ref, dg_ref):
        i = pl.program_id(0)

        @pl.when(i == 0)
        def _():
            loss_ref[...] = jnp.zeros(loss_ref.shape, F32)
            dg_ref[...] = jnp.zeros(dg_ref.shape, F32)

        rv = r_ref[...]
        xv = x_ref[...] + _dot(rv * rv, w_ref[...])
        rn = _rms(xv)
        diff = xv * rn * g_ref[...] - t_ref[...]
        loss_ref[...] += 0.5 * jnp.sum(jnp.mean(diff * diff, axis=-1, keepdims=True), axis=0, keepdims=True)
        dx, dg = _rms_bwd(diff * (1.0 / D), xv, rn, g_ref[...])
        dx_ref[...] = dx
        dxb_ref[...] = dx.astype(BF16)
        dg_ref[...] += dg

    return pl.pallas_call(
        body, name="fwd_ff2_loss", grid=(T // TM,),
        in_specs=[_rows(TM, D_FF), pl.BlockSpec((D_FF, D), lambda i: (0, 0), pipeline_mode=pl.Buffered(1)),
                  _rows(TM, D), _full((1, D)), _rows(TM, D)],
        out_specs=[_rows(TM, D), _rows(TM, D), _full((8, LANES)), _full((1, D))],
        out_shape=[jax.ShapeDtypeStruct((T, D), F32), jax.ShapeDtypeStruct((T, D), BF16),
                   jax.ShapeDtypeStruct((8, LANES), F32), jax.ShapeDtypeStruct((1, D), F32)],
        compiler_params=_cparams(40),
    )(*map(_hbm, (r, w2, x1, gf, tgt)))


def _bwd_mlp(dxb, dres, w2, r, wg1, x1, g2, dep):
    bn = wg1.shape[2]

    def body(d_ref, dres_ref, w2_ref, r_ref, w1_ref, x_ref, g_ref, dep_ref, df1_ref, dx_ref, dxb_ref, dg_ref):
        i = pl.program_id(0)

        @pl.when(i == 0)
        def _():
            dg_ref[...] = jnp.zeros(dg_ref.shape, F32)

        d = d_ref[...]
        dh = jnp.zeros((TM, D), F32)
        for j in range(N_DEV):
            cols = slice(j * bn, (j + 1) * bn)
            df1 = (2.0 * r_ref[:, cols].astype(F32) * _dot_nt(d, w2_ref[cols, :])).astype(BF16)
            df1_ref[:, cols] = df1
            dh = dh + _dot_nt(df1, w1_ref[j])
        xv = x_ref[...]
        dxn, dg = _rms_bwd(dh, xv, _rms(xv), g_ref[...])
        dx = dres_ref[...] + dxn
        dx_ref[...] = dx
        dxb_ref[...] = dx.astype(BF16)
        dg_ref[...] += dg

    once = dict(pipeline_mode=pl.Buffered(1))
    return pl.pallas_call(
        body, name="bwd_mlp", grid=(T // TM,),
        in_specs=[_rows(TM, D), _rows(TM, D), pl.BlockSpec((D_FF, D), lambda i: (0, 0), **once), _rows(TM, D_FF),
                  pl.BlockSpec(wg1.shape, lambda i: (0, 0, 0), **once), _rows(TM, D), _full((1, D)), HBM],
        out_specs=[_rows(TM, D_FF), _rows(TM, D), _rows(TM, D), _full((1, D))],
        out_shape=[jax.ShapeDtypeStruct((T, D_FF), BF16), jax.ShapeDtypeStruct((T, D), F32),
                   jax.ShapeDtypeStruct((T, D), BF16), jax.ShapeDtypeStruct((1, D), F32)],
        compiler_params=_cparams(56),
    )(*map(_hbm, (dxb, dres, w2, r, wg1, x1, g2, dep)))


def _mixer_bwd_a(dxb, wout, proj, c, lg, lb, wm, bst, clg, clb, dep):
    tm = TM_SGU
    n_tiles = T // tm

    def body(dx_ref, wo_ref, p_ref, c_ref, lg_ref, lb_ref, wm_ref, bs_ref, clg_ref, clb_ref, dep_ref,
             dpa_ref, dc_ref, dlg_ref, dlb_ref, dwm_ref, dbs_ref, dcb_ref, dclg_ref, dclb_ref,
             dv_buf, db_acc):
        i = pl.program_id(0)

        @pl.when(i == 0)
        def _():
            for ref in (dlg_ref, dlb_ref, dwm_ref, dbs_ref, dcb_ref, dclg_ref, dclb_ref, db_acc):
                ref[...] = jnp.zeros(ref.shape, F32)

        dmix = _dot_nt(dx_ref[...], wo_ref[...])
        ua = p_ref[:, 0:D_A]
        va = p_ref[:, D_A:2 * D_A]
        u, cdf_u = _gelu(ua)
        vg, cdf_v = _gelu(va)
        xhat, rstd = _ln(vg)
        v = (xhat * lg_ref[...] + lb_ref[...]).astype(BF16)
        mask = _tril_mask()
        for h in range(H_A):
            hc = slice(h * CHUNK, (h + 1) * CHUNK)
            wmh = jnp.where(mask, wm_ref[h], 0.0).astype(BF16)
            for cidx in range(tm // CHUNK):
                rc = slice(cidx * CHUNK, (cidx + 1) * CHUNK)
                vb = v[rc, hc]
                mixed = _dot(wmh, vb) + bs_ref[:, h:h + 1]
                da = dmix[rc, hc]
                dpa_ref[rc, hc] = (da * mixed * _gelu_grad(ua[rc, hc], cdf_u[rc, hc])).astype(BF16)
                dmixed = da * u[rc, hc]
                dmb = dmixed.astype(BF16)
                dv_buf[rc, hc] = _dot_tn(wmh, dmb)
                dwm_ref[h] += _dot_nt(dmb, vb)
                db_acc[:, hc] += dmixed
        dv = dv_buf[...]
        dlb_ref[...] += jnp.sum(dv, axis=0, keepdims=True)
        dlg_ref[...] += jnp.sum(dv * xhat, axis=0, keepdims=True)
        dvg = _ln_bwd(dv * lg_ref[...], xhat, rstd)
        dpa_ref[:, D_A:2 * D_A] = (dvg * _gelu_grad(va, cdf_v)).astype(BF16)

        for q in range(H_B):
            cols = slice(q * LANES, (q + 1) * LANES)
            chat, crstd = _ln(c_ref[:, cols])
            z = chat * clg_ref[:, cols] + clb_ref[:, cols]
            sg = _sigmoid(z)
            dz = dmix[:, D_A + q * LANES:D_A + (q + 1) * LANES] * (sg * (1.0 + z * (1.0 - sg)))
            dclb_ref[:, cols] += jnp.sum(dz, axis=0, keepdims=True)
            dclg_ref[:, cols] += jnp.sum(dz * chat, axis=0, keepdims=True)
            dc = _ln_bwd(dz * clg_ref[:, cols], chat, crstd)
            dc_ref[:, cols] = dc
            dcb_ref[:, cols] += jnp.sum(dc, axis=0, keepdims=True)

        @pl.when(i == n_tiles - 1)
        def _():
            for h in range(H_A):
                hc = slice(h * CHUNK, (h + 1) * CHUNK)
                dwm_ref[h] = jnp.where(mask, dwm_ref[h], 0.0)
                dbs_ref[h:h + 1, :] = jnp.sum(db_acc[:, hc].T, axis=0, keepdims=True)

    vec = _full((1, D_A))
    vshape = jax.ShapeDtypeStruct((1, D_A), F32)
    return pl.pallas_call(
        body, name="mixer_bwd_a", grid=(n_tiles,),
        in_specs=[_rows(tm, D), _full((D, D)), _rows(tm, IN_COLS), _rows(tm, D_B), vec, vec,
                  _full((H_A, CHUNK, CHUNK)), _full((CHUNK, H_A)), vec, vec, HBM],
        out_specs=[_rows(tm, 2 * D_A), _rows(tm, D_B), vec, vec, _full((H_A, CHUNK, CHUNK)),
                   _full((H_A, CHUNK)), vec, vec, vec],
        out_shape=[jax.ShapeDtypeStruct((T, 2 * D_A), BF16), jax.ShapeDtypeStruct((T, D_B), F32), vshape, vshape,
                   jax.ShapeDtypeStruct((H_A, CHUNK, CHUNK), F32), jax.ShapeDtypeStruct((H_A, CHUNK), F32),
                   vshape, vshape, vshape],
        scratch_shapes=[pltpu.VMEM((tm, D_A), F32), pltpu.VMEM((CHUNK, D_A), F32)],
        compiler_params=_cparams(32),
    )(*map(_hbm, (dxb, wout, proj, c, lg, lb, wm, bst, clg, clb, dep)))


def _mixer_bwd_b(dc, g, proj, cw, dpa, wg_in, x, g1, dres):
    tm = TM_SGU
    n_tiles = T // tm
    hb = tm // HALO
    bn = wg_in.shape[2]

    def body(dc_ref, dch_ref, g_ref, p_ref, cw_ref, dpa_ref, win_ref, x_ref, g1_ref, dres_ref,
             dp_ref, dcw_ref, dx_ref, dxb_ref, dg1_ref, dcbuf, dwacc):
        i = pl.program_id(0)

        @pl.when(i == 0)
        def _():
            dwacc[...] = jnp.zeros(dwacc.shape, F32)
            dg1_ref[...] = jnp.zeros(dg1_ref.shape, F32)

        _fill_shift_buffer(dcbuf, 0, dc_ref[...])
        _fill_shift_buffer(dcbuf, tm, jnp.where(i < n_tiles - 1, dch_ref[...], 0.0))
        _build_shifts(dcbuf)
        dp_ref[:, 0:2 * D_A] = dpa_ref[...]
        for q in range(H_B):
            cols = slice(q * LANES, (q + 1) * LANES)

            def row_block(rb, carry, q=q, cols=cols):
                base = pl.multiple_of(rb * RB, RB)
                rows = pl.ds(base, RB)
                gv = g_ref[rows, cols]
                acc = [jnp.zeros((RB, LANES), F32) for _ in range(CONV_CHAINS)]
                parts = []
                for k in range(CONV_W):
                    xk = _shifted(dcbuf, q, base, CONV_W - 1 - k)
                    acc[k % CONV_CHAINS] = acc[k % CONV_CHAINS] + xk * cw_ref[k:k + 1, cols]
                    parts.append(jnp.sum((gv * xk).reshape(RB // 8, 8, LANES), axis=0))
                dg = functools.reduce(lambda a, b: a + b, acc)
                val = p_ref[rows, cols]
                sg = _sigmoid(p_ref[rows, D_B + q * LANES:D_B + (q + 1) * LANES])
                dp_ref[rows, 2 * D_A + q * LANES:2 * D_A + (q + 1) * LANES] = (dg * sg).astype(BF16)
                dp_ref[rows, 2 * D_A + D_B + q * LANES:2 * D_A + D_B + (q + 1) * LANES] = (
                    dg * val * sg * (1.0 - sg)).astype(BF16)
                for k in range(CONV_W):
                    dwacc[k * 8:(k + 1) * 8, cols] += parts[k]
                return carry

            lax.fori_loop(0, tm // RB, row_block, 0)

        dh = jnp.zeros((tm, D), F32)
        for j in range(N_DEV):
            dh = dh + _dot_nt(dp_ref[:, j * bn:(j + 1) * bn], win_ref[j])
        xv = x_ref[...]
        dxn, dg = _rms_bwd(dh, xv, _rms(xv), g1_ref[...])
        dx = dres_ref[...] + dxn
        dx_ref[...] = dx
        dxb_ref[...] = dx.astype(BF16)
        dg1_ref[...] += dg

        @pl.when(i == n_tiles - 1)
        def _():
            for k in range(CONV_W):
                dcw_ref[k:k + 1, :] = jnp.sum(dwacc[k * 8:(k + 1) * 8, :], axis=0, keepdims=True)

    return pl.pallas_call(
        body, name="mixer_bwd_b", grid=(n_tiles,),
        in_specs=[_rows(tm, D_B),
                  pl.BlockSpec((HALO, D_B), lambda i: (jnp.minimum((i + 1) * hb, T // HALO - 1), 0)),
                  _rows(tm, D_B),
                  pl.BlockSpec((tm, 2 * D_B), lambda i: (i, 1)),
                  _full((CONV_W, D_B)), _rows(tm, 2 * D_A),
                  pl.BlockSpec(wg_in.shape, lambda i: (0, 0, 0), pipeline_mode=pl.Buffered(1)),
                  _rows(tm, D), _full((1, D)), _rows(tm, D)],
        out_specs=[_rows(tm, IN_COLS), _full((CONV_W, D_B)), _rows(tm, D), _rows(tm, D), _full((1, D))],
        out_shape=[jax.ShapeDtypeStruct((T, IN_COLS), BF16), jax.ShapeDtypeStruct((CONV_W, D_B), F32),
                   jax.ShapeDtypeStruct((T, D), F32), jax.ShapeDtypeStruct((T, D), BF16),
                   jax.ShapeDtypeStruct((1, D), F32)],
        scratch_shapes=[pltpu.VMEM((8, H_B, tm + HALO, LANES), F32), pltpu.VMEM((CONV_W * 8, D_B), F32)],
        compiler_params=_cparams(40),
    )(*map(_hbm, (dc, dc, g, proj, cw, dpa, wg_in, x, g1, dres)))


def _wgrad(name, a, g, bn, square_a=False):
    k = a.shape[1]
    n = g.shape[1]
    tk = min(k, 1024)
    tn = min(n, max(bn, 1024))
    nsub = tn // bn
    tt = TT_WGRAD
    nt = T // tt

    def body(a_ref, g_ref, o_ref, ob_ref):
        t = pl.program_id(2)

        @pl.when(t == 0)
        def _():
            o_ref[...] = jnp.zeros(o_ref.shape, F32)

        av = a_ref[...]
        if square_a:
            av = av * av
        for s in range(nsub):
            o_ref[s] += _dot_tn(av, g_ref[:, s * bn:(s + 1) * bn])

        @pl.when(t == nt - 1)
        def _():
            ob_ref[...] = o_ref[...].astype(BF16)

    ospec = pl.BlockSpec((nsub, tk, bn), lambda ki, ni, ti: (ni, ki, 0))
    return pl.pallas_call(
        body, name=name, grid=(k // tk, n // tn, nt),
        in_specs=[pl.BlockSpec((tt, tk), lambda ki, ni, ti: (ti, ki)),
                  pl.BlockSpec((tt, tn), lambda ki, ni, ti: (ti, ni))],
        out_specs=[ospec, ospec],
        out_shape=[jax.ShapeDtypeStruct((n // bn, k, bn), F32), jax.ShapeDtypeStruct((n // bn, k, bn), BF16)],
        compiler_params=_cparams(40, ("arbitrary", "arbitrary", "arbitrary")),
    )(a, g)


def _coords():
    return lax.axis_index("x"), lax.axis_index("y"), lax.axis_index("c")


def _flat(x, y, c):
    return 4 * x + 2 * y + c


def _peer(k):
    x, y, c = _coords()
    return (x ^ ((k >> 2) & 1), y ^ ((k >> 1) & 1), c ^ (k & 1))


def _hbm(a):
    return pltpu.with_memory_space_constraint(a, pltpu.HBM)


def _hbm_like(a):
    return pltpu.HBM(a.shape, a.dtype)


def _peer_sems():
    return pltpu.SemaphoreType.DMA((N_DEV - 1,))


def _place_own(name, shards):
    n = len(shards)
    shapes = [(s.shape if l is None else s.shape[1:]) for s, l, _ in shards]

    def body(*refs):
        ins, outs = refs[:n], refs[n:2 * n]
        stage_in, stage_out = refs[2 * n:3 * n], refs[3 * n:4 * n]
        in_sems, out_sems = refs[4 * n], refs[4 * n + 1]
        me = _flat(*_coords())
        srcs = [ins[a] if shards[a][1] is None else ins[a].at[shards[a][1]] for a in range(n)]
        loads = [pltpu.make_async_copy(srcs[a], stage_in[a], in_sems.at[a]) for a in range(n)]
        stores = [pltpu.make_async_copy(stage_out[a], outs[a].at[me], out_sems.at[a]) for a in range(n)]
        for cp in loads:
            cp.start()
        for a in range(n):
            loads[a].wait()
            stage_out[a][...] = stage_in[a][...].astype(stage_out[a].dtype)
            stores[a].start()
        for cp in stores:
            cp.wait()

    return pl.pallas_call(
        body, name=name, in_specs=[HBM] * n, out_specs=[HBM] * n,
        out_shape=[pltpu.HBM((N_DEV,) + shapes[a], shards[a][2]) for a in range(n)],
        scratch_shapes=[*[pltpu.VMEM(shapes[a], shards[a][0].dtype) for a in range(n)],
                        *[pltpu.VMEM(shapes[a], shards[a][2]) for a in range(n)],
                        pltpu.SemaphoreType.DMA((n,)), pltpu.SemaphoreType.DMA((n,))],
        compiler_params=pltpu.CompilerParams(vmem_limit_bytes=40 * MIB),
    )(*[_hbm(s) for s, _, _ in shards])


SIBLING = 1
CHIP_PEERS = (2, 4, 6)
FIRST_PEERS = (SIBLING,) + CHIP_PEERS


def _gather_start(name, lands):
    n = len(lands)

    def body(*refs):
        lnd, send, recv = refs[:n], refs[n:2 * n], refs[2 * n:3 * n]
        me = _flat(*_coords())
        for a in range(n):
            for j, k in enumerate(FIRST_PEERS):
                pltpu.make_async_remote_copy(
                    src_ref=lnd[a].at[me], dst_ref=lnd[a].at[me], send_sem=send[a].at[j],
                    recv_sem=recv[a].at[j], device_id=_peer(k), device_id_type=MESH_ID).start()

    sems = pltpu.SemaphoreType.DMA((len(FIRST_PEERS),))
    outs = pl.pallas_call(
        body, name=name,
        out_shape=(*[sems] * (2 * n), *[_hbm_like(l) for l in lands]),
        in_specs=[HBM] * n, out_specs=(*[SEM] * (2 * n), *[HBM] * n),
        input_output_aliases={i: 2 * n + i for i in range(n)},
        compiler_params=pltpu.CompilerParams(has_side_effects=EFFECT),
    )(*[_hbm(l) for l in lands])
    return outs[:n], outs[n:2 * n], outs[2 * n:]


def _gather_pass_on(name, lands, sends, recvs, after):
    n = len(lands)

    def body(*refs):
        lnd, send, recv = refs[:n], refs[n:2 * n], refs[2 * n:3 * n]
        send2, recv2 = refs[3 * n + 1:4 * n + 1], refs[4 * n + 1:5 * n + 1]
        sibling = _peer(SIBLING)
        for a in range(n):
            for j, k in enumerate(CHIP_PEERS):
                cp = pltpu.make_async_remote_copy(
                    src_ref=lnd[a].at[0], dst_ref=lnd[a].at[0], send_sem=send[a].at[1 + j],
                    recv_sem=recv[a].at[1 + j], device_id=_peer(k), device_id_type=MESH_ID)
                cp.wait_send()
                cp.wait_recv()
                blk = _flat(*_peer(k))
                pltpu.make_async_remote_copy(
                    src_ref=lnd[a].at[blk], dst_ref=lnd[a].at[blk], send_sem=send2[a].at[j],
                    recv_sem=recv2[a].at[j], device_id=sibling, device_id_type=MESH_ID).start()

    sems = pltpu.SemaphoreType.DMA((len(CHIP_PEERS),))
    outs = pl.pallas_call(
        body, name=name, out_shape=(*[sems] * (2 * n), *[_hbm_like(l) for l in lands]),
        in_specs=(*[HBM] * n, *[SEM] * (2 * n), HBM), out_specs=(*[SEM] * (2 * n), *[HBM] * n),
        input_output_aliases={i: 2 * n + i for i in range(n)},
        compiler_params=pltpu.CompilerParams(has_side_effects=EFFECT),
    )(*lands, *sends, *recvs, after)
    return outs[:n], outs[n:2 * n], outs[2 * n:]


def _gather_wait(name, lands, sends, recvs, sends2, recvs2, after):
    n = len(lands)

    def body(*refs):
        lnd, send, recv = refs[:n], refs[n:2 * n], refs[2 * n:3 * n]
        send2, recv2 = refs[3 * n:4 * n], refs[4 * n:5 * n]
        sibling = _peer(SIBLING)
        for a in range(n):
            own = pltpu.make_async_remote_copy(
                src_ref=lnd[a].at[0], dst_ref=lnd[a].at[0], send_sem=send[a].at[0], recv_sem=recv[a].at[0],
                device_id=sibling, device_id_type=MESH_ID)
            own.wait_send()
            own.wait_recv()
            for j in range(len(CHIP_PEERS)):
                cp = pltpu.make_async_remote_copy(
                    src_ref=lnd[a].at[0], dst_ref=lnd[a].at[0], send_sem=send2[a].at[j], recv_sem=recv2[a].at[j],
                    device_id=sibling, device_id_type=MESH_ID)
                cp.wait_send()
                cp.wait_recv()

    return pl.pallas_call(
        body, name=name, out_shape=[_hbm_like(l) for l in lands],
        in_specs=(*[HBM] * n, *[SEM] * (4 * n), HBM), out_specs=[HBM] * n,
        input_output_aliases={i: i for i in range(n)},
        compiler_params=pltpu.CompilerParams(has_side_effects=EFFECT),
    )(*lands, *sends, *recvs, *sends2, *recvs2, after)


def _scatter_start(name, grads, lands, layer):
    n = len(grads)

    def body(*refs):
        g, lnd, send, recv = refs[:n], refs[n:2 * n], refs[2 * n:3 * n], refs[3 * n:4 * n]
        for a in range(n):
            for k in range(1, N_DEV):
                to = _peer(k)
                pltpu.make_async_remote_copy(
                    src_ref=g[a].at[_flat(*to)], dst_ref=lnd[a].at[layer, k - 1], send_sem=send[a].at[k - 1],
                    recv_sem=recv[a].at[k - 1], device_id=to, device_id_type=MESH_ID).start()

    outs = pl.pallas_call(
        body, name=name,
        out_shape=(*[_peer_sems()] * (2 * n), *[_hbm_like(g) for g in grads], *[_hbm_like(l) for l in lands]),
        in_specs=[HBM] * (2 * n), out_specs=(*[SEM] * (2 * n), *[HBM] * (2 * n)),
        input_output_aliases={i: 2 * n + i for i in range(2 * n)},
        compiler_params=pltpu.CompilerParams(has_side_effects=EFFECT),
    )(*[_hbm(g) for g in grads], *[_hbm(l) for l in lands])
    return outs[:n], outs[n:2 * n], outs[2 * n:3 * n], outs[3 * n:]


def _scatter_wait(name, grads, lands, sends, recvs, after):
    n, nw = len(grads), len(lands)

    def body(*refs):
        g, lnd = refs[:n], refs[n:n + nw]
        send, recv = refs[n + nw:2 * n + nw], refs[2 * n + nw:3 * n + nw]
        for a in range(n):
            for k in range(1, N_DEV):
                cp = pltpu.make_async_remote_copy(
                    src_ref=g[a].at[0], dst_ref=lnd[a // DEPTH].at[a % DEPTH, 0],
                    send_sem=send[a].at[k - 1], recv_sem=recv[a].at[k - 1],
                    device_id=_peer(k), device_id_type=MESH_ID)
                cp.wait_send()
                cp.wait_recv()

    outs = pl.pallas_call(
        body, name=name,
        out_shape=(*[_hbm_like(g) for g in grads], *[_hbm_like(l) for l in lands]),
        in_specs=(*[HBM] * (n + nw), *[SEM] * (2 * n), HBM), out_specs=[HBM] * (n + nw),
        input_output_aliases={i: i for i in range(n + nw)},
        compiler_params=pltpu.CompilerParams(has_side_effects=EFFECT),
    )(*grads, *lands, *sends, *recvs, after)
    return outs[n:]


def _all_reduce_small(part, deps):
    rows = part.shape[0]
    br = rows // N_DEV
    assert br * N_DEV == rows and br % 8 == 0
    nd = len(deps)

    def body(p_ref, *refs):
        o_ref, slots, send1, recv1, send2, recv2 = refs[nd:]
        me = _flat(*_coords())

        def block(ref, d):
            return ref.at[pl.ds(pl.multiple_of(d * br, 8), br), :]

        slots[me] = p_ref[pl.ds(pl.multiple_of(me * br, 8), br), :]
        scatter, gather = [], []
        for k in range(1, N_DEV):
            to = _peer(k)
            scatter.append(pltpu.make_async_remote_copy(
                src_ref=block(p_ref, _flat(*to)), dst_ref=slots.at[me],
                send_sem=send1.at[k - 1], recv_sem=recv1.at[k - 1], device_id=to, device_id_type=MESH_ID))
            gather.append(pltpu.make_async_remote_copy(
                src_ref=block(o_ref, me), dst_ref=block(o_ref, me),
                send_sem=send2.at[k - 1], recv_sem=recv2.at[k - 1], device_id=to, device_id_type=MESH_ID))
        for cp in scatter:
            cp.start()
        for cp in scatter:
            cp.wait()
        acc = slots[0]
        for d in range(1, N_DEV):
            acc = acc + slots[d]
        o_ref[pl.ds(pl.multiple_of(me * br, 8), br), :] = acc
        for cp in gather:
            cp.start()
        for cp in gather:
            cp.wait()

    return pl.pallas_call(
        body, name="all_reduce_small",
        in_specs=[pl.BlockSpec(memory_space=pltpu.VMEM)] + [HBM] * nd, out_specs=pl.BlockSpec(memory_space=pltpu.VMEM),
        out_shape=jax.ShapeDtypeStruct(part.shape, F32),
        scratch_shapes=[pltpu.VMEM((N_DEV, br, LANES), F32)] + [pltpu.SemaphoreType.DMA((N_DEV - 1,))] * 4,
        compiler_params=pltpu.CompilerParams(vmem_limit_bytes=24 * MIB),
    )(part, *map(_hbm, deps))


def _adam_math(w, g, m, v):
    m = ADAM_B1 * m + (1.0 - ADAM_B1) * g
    v = ADAM_B2 * v + (1.0 - ADAM_B2) * (g * g)
    m_hat = m / (1.0 - ADAM_B1 ** ADAM_STEP)
    v_hat = v / (1.0 - ADAM_B2 ** ADAM_STEP)
    delta = -ADAM_LR * (m_hat / (jnp.sqrt(v_hat) + ADAM_EPS) + ADAM_WD * w)
    return delta, m, v


def _adam_sharded(name, me, g32, recv, w, m, v):
    _, r, c = w.shape
    tr = min(r, 512)
    nblk = r // tr
    assert len(g32) == DEPTH == 2

    def body(me_ref, g0_ref, g1_ref, recv_ref, w_ref, m_ref, v_ref, g_ref, d_ref, nm_ref, nv_ref):
        g = jnp.where(pl.program_id(0) == 0, g0_ref[...], g1_ref[...])
        for k in range(N_DEV - 1):
            g = g + recv_ref[k].astype(F32)
        delta, nm, nv = _adam_math(w_ref[...], g, m_ref[...], v_ref[...])
        g_ref[...] = g
        d_ref[...] = delta
        nm_ref[...] = nm
        nv_ref[...] = nv

    blk = pl.BlockSpec((None, tr, c), lambda l, i, me_ref: (l, i, 0))
    own0 = pl.BlockSpec((None, tr, c), lambda l, i, me_ref: (me_ref[0], jnp.where(l == 0, i, nblk - 1), 0))
    own1 = pl.BlockSpec((None, tr, c), lambda l, i, me_ref: (me_ref[0], jnp.where(l == 1, i, 0), 0))
    shp = jax.ShapeDtypeStruct(w.shape, F32)
    return pl.pallas_call(
        body, name=name,
        grid_spec=pltpu.PrefetchScalarGridSpec(
            num_scalar_prefetch=1, grid=(DEPTH, nblk),
            in_specs=[own0, own1, pl.BlockSpec((None, N_DEV - 1, tr, c), lambda l, i, me_ref: (l, 0, i, 0)),
                      blk, blk, blk],
            out_specs=[blk] * 4),
        out_shape=[shp] * 4,
        compiler_params=_cparams(32, ("arbitrary", "arbitrary")),
    )(me, *[_hbm(a) for a in (*g32, recv, w, m, v)])


def _adam_small(gs, ws, ms, vs):
    n = len(gs)

    def body(*refs):
        g, w, m, v = refs[:n], refs[n:2 * n], refs[2 * n:3 * n], refs[3 * n:4 * n]
        d, nm, nv = refs[4 * n:5 * n], refs[5 * n:6 * n], refs[6 * n:7 * n]
        for a in range(n):
            delta, new_m, new_v = _adam_math(w[a][...], g[a][...], m[a][...], v[a][...])
            d[a][...] = delta
            nm[a][...] = new_m
            nv[a][...] = new_v

    specs = [_full(a.shape) for a in gs]
    shapes = [jax.ShapeDtypeStruct(a.shape, F32) for a in gs]
    outs = pl.pallas_call(
        body, name="adam_small", grid=(1,),
        in_specs=specs * 4, out_specs=specs * 3, out_shape=shapes * 3,
        compiler_params=_cparams(24),
    )(*gs, *ws, *ms, *vs)
    return outs[:n], outs[n:2 * n], outs[2 * n:]


SMALL = ["norm1_g", "sgu_ln_g", "sgu_ln_b", "sgu_w", "sgu_b", "conv_b", "conv_ln_g", "conv_ln_b", "norm2_g",
         "final_g"]


def _pack(arrays):
    flat = jnp.concatenate([a.reshape(-1) for a in arrays])
    pad = (-flat.shape[0]) % (N_DEV * 8 * LANES)
    return jnp.pad(flat, (0, pad)).reshape(-1, LANES)


def _unpack(packed, shapes):
    flat = packed.reshape(-1)
    out, off = [], 0
    for s in shapes:
        size = 1
        for d in s:
            size *= d
        out.append(flat[off:off + size].reshape(s))
        off += size
    return out


def kernel(x, norm1_g, w_in, sgu_ln_g, sgu_ln_b, sgu_w, sgu_b, conv_w, conv_b, conv_ln_g, conv_ln_b, w_out, norm2_g, w_ff1, w_ff2, final_g, loss_target, m_norm1_g, m_w_in, m_sgu_ln_g, m_sgu_ln_b, m_sgu_w, m_sgu_b, m_conv_w, m_conv_b, m_conv_ln_g, m_conv_ln_b, m_w_out, m_norm2_g, m_w_ff1, m_w_ff2, m_final_g, v_norm1_g, v_w_in, v_sgu_ln_g, v_sgu_ln_b, v_sgu_w, v_sgu_b, v_conv_w, v_conv_b, v_conv_ln_g, v_conv_ln_b, v_w_out, v_norm2_g, v_w_ff1, v_w_ff2, v_final_g):
    x2d = x.reshape(T, D)
    tgt = loss_target.reshape(T, D)
    cw_shard = conv_w.reshape(CONV_W, LANES)

    gnames, shards = [], []
    for l in range(DEPTH):
        for k, w in (("w_in", w_in), ("w_out", w_out), ("w_ff1", w_ff1), ("w_ff2", w_ff2)):
            gnames.append(f"{k}{l}")
            shards.append((w, l, BF16))
        if l == 0:
            gnames.insert(1, "conv_w")
            shards.insert(1, (cw_shard, None, F32))
    s0, r0, l0 = _gather_start("gather_start_first", _place_own("place_own_first", shards[:1]))
    s1, r1, l1 = _gather_start("gather_start_rest", _place_own("place_own_rest", shards[1:]))
    sends, recvs, lands = [*s0, *s1], [*r0, *r1], [*l0, *l1]
    gidx = {k: i for i, k in enumerate(gnames)}

    passed = {}

    def pass_on(ks, after):
        idx = [gidx[k] for k in ks]
        s2, r2, ld = _gather_pass_on("gather_pass_on_" + "_".join(ks), [lands[i] for i in idx],
                                     [sends[i] for i in idx], [recvs[i] for i in idx], after)
        for j, k in enumerate(ks):
            passed[k] = (s2[j], r2[j], ld[j])

    def gathered(ks, after):
        idx = [gidx[k] for k in ks]
        return _gather_wait("gather_wait_" + "_".join(ks), [passed[k][2] for k in ks], [sends[i] for i in idx],
                            [recvs[i] for i in idx], [passed[k][0] for k in ks], [passed[k][1] for k in ks], after)

    saved = []
    xl = x2d
    cw_full = None
    for l in range(DEPTH):
        g1 = norm1_g[l].reshape(1, D)
        g2 = norm2_g[l].reshape(1, D)
        lg, lb = sgu_ln_g[l].reshape(1, D_A), sgu_ln_b[l].reshape(1, D_A)
        bst = sgu_b[l].T
        cb = conv_b[l].reshape(1, D_B)
        clg, clb = conv_ln_g[l].reshape(1, D_B), conv_ln_b[l].reshape(1, D_B)
        first = [f"w_in{l}"] if l == 0 else [f"w_in{l}", f"w_out{l}"]
        pass_on(first, l1[0] if l == 0 else xl)
        wg_in, *rest = gathered(first, xl)
        h, proj = _fwd_in(xl, g1, wg_in)
        if l == 0:
            pass_on(["conv_w", "w_out0"], h)
            cw_g, wout = gathered(["conv_w", "w_out0"], proj)
            cw_full = cw_g.reshape(N_DEV, DEPTH, CONV_W, D_B // N_DEV).transpose(1, 2, 0, 3).reshape(
                DEPTH, CONV_W, D_B)
        else:
            (wout,) = rest
            pass_on([f"w_ff1{l}"], proj)
        wout = wout.reshape(D, D)
        mix, c, g, x1, h2 = _mixer_fwd(proj, lg, lb, sgu_w[l], bst, cw_full[l], cb, clg, clb, wout, xl, g2)
        pass_on([f"w_ff1{l}"] if l == 0 else [f"w_ff2{l}"], h2)
        (wg_ff1,) = gathered([f"w_ff1{l}"], h2)
        r = _fwd_ff1(h2, wg_ff1)
        if l == 0:
            pass_on(["w_ff20"], r)
        (w2,) = gathered([f"w_ff2{l}"], r)
        w2 = w2.reshape(D_FF, D)
        saved.append(dict(x=xl, h=h, proj=proj, mix=mix, c=c, g=g, x1=x1, h2=h2, r=r, wg_in=wg_in, wout=wout,
                          wg_ff1=wg_ff1, w2=w2, g1=g1, g2=g2, lg=lg, lb=lb, bst=bst, clg=clg, clb=clb))
        if l + 1 < DEPTH:
            xl = _fwd_ff2(r, w2, x1)
        else:
            dx, dxb, loss_part, d_final_g = _fwd_ff2_loss(r, w2, x1, final_g.reshape(1, D), tgt)

    names = ["w_in", "w_out", "w_ff1", "w_ff2"]
    block = dict(w_in=(D, IN_COLS // N_DEV), w_out=(D // N_DEV, D), w_ff1=(D, D_FF // N_DEV), w_ff2=(D_FF // N_DEV, D))
    land = {k: lax.empty((DEPTH, N_DEV - 1) + block[k], BF16) for k in names}
    big32 = {k: [None] * DEPTH for k in names}
    big16 = {k: [None] * DEPTH for k in names}
    ssend = {k: [None] * DEPTH for k in names}
    srecv = {k: [None] * DEPTH for k in names}

    def send_grads(l, grads):
        ks = list(grads)
        for k in ks:
            big32[k][l] = grads[k][0].reshape((N_DEV,) + block[k])
        sends, recvs, g16s, lands = _scatter_start(
            "scatter_start_" + "_".join(ks) + str(l), [grads[k][1].reshape((N_DEV,) + block[k]) for k in ks],
            [land[k] for k in ks], l)
        for i, k in enumerate(ks):
            ssend[k][l], srecv[k][l], big16[k][l], land[k] = sends[i], recvs[i], g16s[i], lands[i]

    small = {}
    d_conv_w = [None] * DEPTH
    for l in reversed(range(DEPTH)):
        s = saved[l]
        send_grads(l, dict(w_ff2=_wgrad("wgrad_ff2", s["r"], dxb, D, square_a=True)))
        df1, dx1, dx1b, dg2 = _bwd_mlp(dxb, dx, s["w2"], s["r"], s["wg_ff1"], s["x1"], s["g2"], big16["w_ff2"][l])
        send_grads(l, dict(w_ff1=_wgrad("wgrad_ff1", s["h2"], df1, D_FF // N_DEV),
                           w_out=_wgrad("wgrad_out", s["mix"], dx1b, D)))
        dpa, dc, dlg, dlb, dwm, dbs, dcb, dclg, dclb = _mixer_bwd_a(
            dx1b, s["wout"], s["proj"], s["c"], s["lg"], s["lb"], sgu_w[l], s["bst"], s["clg"], s["clb"],
            big16["w_out"][l])
        dproj, d_conv_w[l], dx, dxb, dg1 = _mixer_bwd_b(dc, s["g"], s["proj"], cw_full[l], dpa, s["wg_in"], s["x"],
                                                         s["g1"], dx1)
        send_grads(l, dict(w_in=_wgrad("wgrad_in", s["h"], dproj, IN_COLS // N_DEV)))
        small[l] = dict(norm1_g=dg1, sgu_ln_g=dlg, sgu_ln_b=dlb, sgu_w=dwm, sgu_b=dbs, conv_b=dcb, conv_ln_g=dclg,
                        conv_ln_b=dclb, norm2_g=dg2)
    grad_x = dx.reshape(1, T, D)

    me_block = _flat(*_coords()).astype(jnp.int32).reshape(1)
    wmv = dict(w_in=(w_in, m_w_in, v_w_in), w_out=(w_out, m_w_out, v_w_out), w_ff1=(w_ff1, m_w_ff1, v_w_ff1),
               w_ff2=(w_ff2, m_w_ff2, v_w_ff2))
    res = {}

    def update(tag, ks, after):
        recv = _scatter_wait("scatter_wait_" + tag, [g for k in ks for g in big16[k]], [land[k] for k in ks],
                             [q for k in ks for q in ssend[k]], [q for k in ks for q in srecv[k]], after)
        for k, rk in zip(ks, recv):
            res[k] = _adam_sharded("adam_" + k, me_block, big32[k], rk, *wmv[k])

    update("early", ["w_ff2", "w_ff1", "w_out"], big16["w_in"][0])

    rep = dict(norm1_g=(norm1_g, m_norm1_g, v_norm1_g), sgu_ln_g=(sgu_ln_g, m_sgu_ln_g, v_sgu_ln_g),
               sgu_ln_b=(sgu_ln_b, m_sgu_ln_b, v_sgu_ln_b), sgu_w=(sgu_w, m_sgu_w, v_sgu_w),
               sgu_b=(sgu_b, m_sgu_b, v_sgu_b), conv_b=(conv_b, m_conv_b, v_conv_b),
               conv_ln_g=(conv_ln_g, m_conv_ln_g, v_conv_ln_g), conv_ln_b=(conv_ln_b, m_conv_ln_b, v_conv_ln_b),
               norm2_g=(norm2_g, m_norm2_g, v_norm2_g), final_g=(final_g, m_final_g, v_final_g))
    parts = []
    for k in SMALL:
        if k == "final_g":
            parts.append(d_final_g.reshape(rep[k][0].shape))
        else:
            parts.append(jnp.stack([small[l][k].reshape(rep[k][0].shape[1:]) for l in range(DEPTH)]))
    parts.append(jnp.stack(d_conv_w))
    parts.append(loss_part[0, 0:1])
    shapes = [p.shape for p in parts]
    packed_sum = _all_reduce_small(_pack(parts), [res[k][0] for k in ("w_ff2", "w_ff1", "w_out")])
    update("late", ["w_in"], packed_sum)
    summed = _unpack(packed_sum, shapes)
    loss = summed[-1][0]
    me = _flat(*_coords())
    g_conv_w = lax.dynamic_slice_in_dim(summed[-2], me * (D_B // N_DEV), D_B // N_DEV, axis=2)
    g_small = summed[:-2] + [g_conv_w]
    keys = SMALL + ["conv_w"]
    rep["conv_w"] = (conv_w, m_conv_w, v_conv_w)
    def two_d(a):
        return a.reshape(1, -1) if a.ndim == 1 else a

    d_s, nm_s, nv_s = _adam_small([two_d(g) for g in g_small], *[[two_d(rep[k][i]) for k in keys] for i in range(3)])
    for i, k in enumerate(keys):
        shape = rep[k][0].shape
        res[k] = (g_small[i], d_s[i].reshape(shape), nm_s[i].reshape(shape), nv_s[i].reshape(shape))

    order = ["norm1_g", "w_in", "sgu_ln_g", "sgu_ln_b", "sgu_w", "sgu_b", "conv_w", "conv_b", "conv_ln_g",
             "conv_ln_b", "w_out", "norm2_g", "w_ff1", "w_ff2", "final_g"]
    return (loss, grad_x, *[res[k][0] for k in order], *[res[k][1] for k in order],
            *[res[k][2] for k in order], *[res[k][3] for k in order])
```

```python
import functools

import jax
import jax.numpy as jnp
from jax import lax
from jax.experimental import pallas as pl
from jax.experimental.pallas import tpu as pltpu

F32 = jnp.float32
BF16 = jnp.bfloat16

N_DEV = 8
DEPTH = 2
T = 4096
D = 1024
D_A = 512
D_B = 512
CHUNK = 128
H_A = 4
H_B = 4
CONV_W = 31
HALO = 32
D_FF = 4096
IN_COLS = 2048
EPS = 1e-6

ADAM_LR = 0.001
ADAM_B1 = 0.9
ADAM_B2 = 0.999
ADAM_EPS = 1e-08
ADAM_WD = 0.01
ADAM_STEP = 10

TM = 512
TM_FWD = 1024
TT_WGRAD = 4096
TM_SGU = 512
RB = 64
CONV_CHAINS = 2
LANES = 128
MIB = 1024 * 1024
SCOPED_VMEM_MIB = 60

SQRT_HALF = 0.7071067811865476
INV_SQRT_2PI = 0.3989422804014327

MESH_ID = pl.DeviceIdType.MESH
HBM = pl.BlockSpec(memory_space=pltpu.HBM)
SEM = pl.BlockSpec(memory_space=pltpu.SEMAPHORE)
EFFECT = pltpu.SideEffectType.DATAFLOW_SIDE_EFFECTING


def _cparams(vmem_mib, sem=("arbitrary",)):
    assert vmem_mib <= SCOPED_VMEM_MIB
    return pltpu.CompilerParams(dimension_semantics=sem, vmem_limit_bytes=SCOPED_VMEM_MIB * MIB)


def _full(shape):
    return pl.BlockSpec(shape, lambda *_: (0,) * len(shape))


def _rows(tm, cols):
    return pl.BlockSpec((tm, cols), lambda i: (i, 0))


def _gelu(x):
    cdf = 0.5 * (1.0 + lax.erf(x * SQRT_HALF))
    return x * cdf, cdf


def _gelu_grad(x, cdf):
    return cdf + x * (INV_SQRT_2PI * jnp.exp(-0.5 * x * x))


def _sigmoid(x):
    return 1.0 / (1.0 + jnp.exp(-x))


def _ln(x):
    mu = jnp.mean(x, axis=-1, keepdims=True)
    xc = x - mu
    rstd = lax.rsqrt(jnp.mean(xc * xc, axis=-1, keepdims=True) + EPS)
    return xc * rstd, rstd


def _ln_bwd(dyh, xhat, rstd):
    return rstd * (dyh - jnp.mean(dyh, axis=-1, keepdims=True) - xhat * jnp.mean(dyh * xhat, axis=-1, keepdims=True))


def _rms(x):
    return lax.rsqrt(jnp.mean(x * x, axis=-1, keepdims=True) + EPS)


def _rms_bwd(dh, x, r, g):
    n = x * r
    dn = dh * g
    dx = r * (dn - n * jnp.mean(dn * n, axis=-1, keepdims=True))
    return dx, jnp.sum(dh * n, axis=0, keepdims=True)


def _dot(a, b):
    return jnp.dot(a, b, preferred_element_type=F32)


def _dot_nt(a, b):
    return lax.dot_general(a, b, (((1,), (1,)), ((), ())), preferred_element_type=F32)


def _dot_tn(a, b):
    return lax.dot_general(a, b, (((0,), (0,)), ((), ())), preferred_element_type=F32)


def _tril_mask():
    r = lax.broadcasted_iota(jnp.int32, (CHUNK, CHUNK), 0)
    c = lax.broadcasted_iota(jnp.int32, (CHUNK, CHUNK), 1)
    return r >= c


def _fwd_in(x, g1, wg):
    bn = wg.shape[2]

    def body(x_ref, g_ref, w_ref, h_ref, p_ref):
        xv = x_ref[...]
        h = (xv * _rms(xv) * g_ref[...]).astype(BF16)
        h_ref[...] = h
        for j in range(N_DEV):
            p_ref[:, j * bn:(j + 1) * bn] = _dot(h, w_ref[j])

    return pl.pallas_call(
        body, name="fwd_in", grid=(T // TM_FWD,),
        in_specs=[_rows(TM_FWD, D), _full((1, D)), _full(wg.shape)],
        out_specs=[_rows(TM_FWD, D), _rows(TM_FWD, IN_COLS)],
        out_shape=[jax.ShapeDtypeStruct((T, D), BF16), jax.ShapeDtypeStruct((T, IN_COLS), F32)],
        compiler_params=_cparams(32),
    )(*map(_hbm, (x, g1, wg)))


def _fill_shift_buffer(sh_ref, row0, value):
    for q in range(sh_ref.shape[1]):
        sh_ref[0, q, row0:row0 + value.shape[0], :] = value[:, q * LANES:(q + 1) * LANES]


def _build_shifts(sh_ref):
    rows = sh_ref.shape[2]
    for p in range(1, 8):
        for q in range(sh_ref.shape[1]):
            sh_ref[p, q, 0:rows - 8, :] = sh_ref[0, q, p:p + rows - 8, :]


def _shifted(sh_ref, q, base, off):
    start = base + (off - off % 8)
    if not isinstance(start, int):
        start = pl.multiple_of(start, 8)
    return sh_ref[off % 8, q, pl.ds(start, RB), :]


def _conv_taps(sh_ref, w_ref, q, base, first_tap_row, step):
    cols = slice(q * LANES, (q + 1) * LANES)
    acc = [jnp.zeros((RB, LANES), F32) for _ in range(CONV_CHAINS)]
    for k in range(CONV_W):
        term = _shifted(sh_ref, q, base, first_tap_row + step * k) * w_ref[k:k + 1, cols]
        acc[k % CONV_CHAINS] = acc[k % CONV_CHAINS] + term
    return functools.reduce(lambda a, b: a + b, acc)


def _mixer_fwd(proj, lg, lb, wm, bst, cw, cb, clg, clb, wout, x, g2):
    tm = TM_SGU
    hb = tm // HALO

    def body(p_ref, ph_ref, lg_ref, lb_ref, wm_ref, bs_ref, cw_ref, cb_ref, clg_ref, clb_ref, wo_ref, x_ref, g2_ref,
             mix_ref, c_ref, g_ref, x1_ref, h2_ref, gbuf):
        i = pl.program_id(0)
        u, _ = _gelu(p_ref[:, 0:D_A])
        vg, _ = _gelu(p_ref[:, D_A:2 * D_A])
        xhat, _ = _ln(vg)
        v = (xhat * lg_ref[...] + lb_ref[...]).astype(BF16)
        mask = _tril_mask()
        for h in range(H_A):
            hc = slice(h * CHUNK, (h + 1) * CHUNK)
            wmh = jnp.where(mask, wm_ref[h], 0.0).astype(BF16)
            for c in range(tm // CHUNK):
                rc = slice(c * CHUNK, (c + 1) * CHUNK)
                mixed = _dot(wmh, v[rc, hc]) + bs_ref[:, h:h + 1]
                mix_ref[rc, hc] = (u[rc, hc] * mixed).astype(BF16)
        x1_ref[...] = x_ref[...] + _dot(mix_ref[:, 0:D_A], wo_ref[0:D_A, :])

        g = p_ref[:, 2 * D_A:2 * D_A + D_B] * _sigmoid(p_ref[:, 2 * D_A + D_B:IN_COLS])
        g_ref[...] = g
        gh = ph_ref[:, 0:D_B] * _sigmoid(ph_ref[:, D_B:2 * D_B])
        _fill_shift_buffer(gbuf, 0, jnp.where(i > 0, gh, 0.0))
        _fill_shift_buffer(gbuf, HALO, g)
        _build_shifts(gbuf)
        for q in range(H_B):
            cols = slice(q * LANES, (q + 1) * LANES)
            for rb in range(tm // RB):
                acc = _conv_taps(gbuf, cw_ref, q, rb * RB, HALO - (CONV_W - 1), 1)
                c_ref[rb * RB:(rb + 1) * RB, cols] = acc + cb_ref[:, cols]
        for q in range(H_B):
            cols = slice(q * LANES, (q + 1) * LANES)
            chat, _ = _ln(c_ref[:, cols])
            z = chat * clg_ref[:, cols] + clb_ref[:, cols]
            mix_ref[:, D_A + q * LANES:D_A + (q + 1) * LANES] = (z * _sigmoid(z)).astype(BF16)

        x1 = x1_ref[...] + _dot(mix_ref[:, D_A:D], wo_ref[D_A:D, :])
        x1_ref[...] = x1
        h2_ref[...] = (x1 * _rms(x1) * g2_ref[...]).astype(BF16)

    vec = _full((1, D_A))
    return pl.pallas_call(
        body, name="mixer_fwd", grid=(T // tm,),
        in_specs=[_rows(tm, IN_COLS),
                  pl.BlockSpec((HALO, 2 * D_B), lambda i: (jnp.maximum(i * hb - 1, 0), 1)),
                  vec, vec, _full((H_A, CHUNK, CHUNK)), _full((CHUNK, H_A)),
                  _full((CONV_W, D_B)), vec, vec, vec,
                  pl.BlockSpec((D, D), lambda i: (0, 0), pipeline_mode=pl.Buffered(1)), _rows(tm, D), _full((1, D))],
        out_specs=[_rows(tm, D), _rows(tm, D_B), _rows(tm, D_B), _rows(tm, D), _rows(tm, D)],
        out_shape=[jax.ShapeDtypeStruct((T, D), BF16), jax.ShapeDtypeStruct((T, D_B), F32),
                   jax.ShapeDtypeStruct((T, D_B), F32), jax.ShapeDtypeStruct((T, D), F32),
                   jax.ShapeDtypeStruct((T, D), BF16)],
        scratch_shapes=[pltpu.VMEM((8, H_B, HALO + tm, LANES), F32)],
        compiler_params=_cparams(40),
    )(*map(_hbm, (proj, proj, lg, lb, wm, bst, cw, cb, clg, clb, wout, x, g2)))


def _fwd_ff1(h2, wg):
    bn = wg.shape[2]

    def body(h_ref, w_ref, r_ref):
        h = h_ref[...]
        for j in range(N_DEV):
            r_ref[:, j * bn:(j + 1) * bn] = jnp.maximum(_dot(h, w_ref[j]), 0.0).astype(BF16)

    return pl.pallas_call(
        body, name="fwd_ff1", grid=(T // TM_FWD,),
        in_specs=[_rows(TM_FWD, D), _full(wg.shape)],
        out_specs=_rows(TM_FWD, D_FF),
        out_shape=jax.ShapeDtypeStruct((T, D_FF), BF16),
        compiler_params=_cparams(48),
    )(h2, wg)


def _fwd_ff2(r, w2, x1):
    def body(r_ref, w_ref, x_ref, o_ref):
        rv = r_ref[...]
        o_ref[...] = x_ref[...] + _dot(rv * rv, w_ref[...])

    return pl.pallas_call(
        body, name="fwd_ff2", grid=(T // TM_FWD,),
        in_specs=[_rows(TM_FWD, D_FF), _full((D_FF, D)), _rows(TM_FWD, D)],
        out_specs=_rows(TM_FWD, D),
        out_shape=jax.ShapeDtypeStruct((T, D), F32),
        compiler_params=_cparams(48),
    )(r, w2, x1)


def _fwd_ff2_loss(r, w2, x1, gf, tgt):
    def body(r_ref, w_ref, x_ref, g_ref, t_ref, dx_ref, dxb_ref, loss_ref, dg_ref):
        i = pl.program_id(0)

        @pl.when(i == 0)
        def _():
            loss_ref[...] = jnp.zeros(loss_ref.shape, F32)
            dg_ref[...] = jnp.zeros(dg_ref.shape, F32)

        rv = r_ref[...]
        xv = x_ref[...] + _dot(rv * rv, w_ref[...])
        rn = _rms(xv)
        diff = xv * rn * g_ref[...] - t_ref[...]
        loss_ref[...] += 0.5 * jnp.sum(jnp.mean(diff * diff, axis=-1, keepdims=True), axis=0, keepdims=True)
        dx, dg = _rms_bwd(diff * (1.0 / D), xv, rn, g_ref[...])
        dx_ref[...] = dx
        dxb_ref[...] = dx.astype(BF16)
        dg_ref[...] += dg

    return pl.pallas_call(
        body, name="fwd_ff2_loss", grid=(T // TM,),
        in_specs=[_rows(TM, D_FF), pl.BlockSpec((D_FF, D), lambda i: (0, 0), pipeline_mode=pl.Buffered(1)),
                  _rows(TM, D), _full((1, D)), _rows(TM, D)],
        out_specs=[_rows(TM, D), _rows(TM, D), _full((8, LANES)), _full((1, D))],
        out_shape=[jax.ShapeDtypeStruct((T, D), F32), jax.ShapeDtypeStruct((T, D), BF16),
                   jax.ShapeDtypeStruct((8, LANES), F32), jax.ShapeDtypeStruct((1, D), F32)],
        compiler_params=_cparams(40),
    )(*map(_hbm, (r, w2, x1, gf, tgt)))


def _bwd_mlp(dxb, dres, w2, r, wg1, x1, g2, dep):
    bn = wg1.shape[2]

    def body(d_ref, dres_ref, w2_ref, r_ref, w1_ref, x_ref, g_ref, dep_ref, df1_ref, dx_ref, dxb_ref, dg_ref):
        i = pl.program_id(0)

        @pl.when(i == 0)
        def _():
            dg_ref[...] = jnp.zeros(dg_ref.shape, F32)

        d = d_ref[...]
        dh = jnp.zeros((TM, D), F32)
        for j in range(N_DEV):
            cols = slice(j * bn, (j + 1) * bn)
            df1 = (2.0 * r_ref[:, cols].astype(F32) * _dot_nt(d, w2_ref[cols, :])).astype(BF16)
            df1_ref[:, cols] = df1
            dh = dh + _dot_nt(df1, w1_ref[j])
        xv = x_ref[...]
        dxn, dg = _rms_bwd(dh, xv, _rms(xv), g_ref[...])
        dx = dres_ref[...] + dxn
        dx_ref[...] = dx
        dxb_ref[...] = dx.astype(BF16)
        dg_ref[...] += dg

    once = dict(pipeline_mode=pl.Buffered(1))
    return pl.pallas_call(
        body, name="bwd_mlp", grid=(T // TM,),
        in_specs=[_rows(TM, D), _rows(TM, D), pl.BlockSpec((D_FF, D), lambda i: (0, 0), **once), _rows(TM, D_FF),
                  pl.BlockSpec(wg1.shape, lambda i: (0, 0, 0), **once), _rows(TM, D), _full((1, D)), HBM],
        out_specs=[_rows(TM, D_FF), _rows(TM, D), _rows(TM, D), _full((1, D))],
        out_shape=[jax.ShapeDtypeStruct((T, D_FF), BF16), jax.ShapeDtypeStruct((T, D), F32),
                   jax.ShapeDtypeStruct((T, D), BF16), jax.ShapeDtypeStruct((1, D), F32)],
        compiler_params=_cparams(56),
    )(*map(_hbm, (dxb, dres, w2, r, wg1, x1, g2, dep)))


def _mixer_bwd_a(dxb, wout, proj, c, lg, lb, wm, bst, clg, clb, dep):
    tm = TM_SGU
    n_tiles = T // tm

    def body(dx_ref, wo_ref, p_ref, c_ref, lg_ref, lb_ref, wm_ref, bs_ref, clg_ref, clb_ref, dep_ref,
             dpa_ref, dc_ref, dlg_ref, dlb_ref, dwm_ref, dbs_ref, dcb_ref, dclg_ref, dclb_ref,
             dv_buf, db_acc):
        i = pl.program_id(0)

        @pl.when(i == 0)
        def _():
            for ref in (dlg_ref, dlb_ref, dwm_ref, dbs_ref, dcb_ref, dclg_ref, dclb_ref, db_acc):
                ref[...] = jnp.zeros(ref.shape, F32)

        dmix = _dot_nt(dx_ref[...], wo_ref[...])
        ua = p_ref[:, 0:D_A]
        va = p_ref[:, D_A:2 * D_A]
        u, cdf_u = _gelu(ua)
        vg, cdf_v = _gelu(va)
        xhat, rstd = _ln(vg)
        v = (xhat * lg_ref[...] + lb_ref[...]).astype(BF16)
        mask = _tril_mask()
        for h in range(H_A):
            hc = slice(h * CHUNK, (h + 1) * CHUNK)
            wmh = jnp.where(mask, wm_ref[h], 0.0).astype(BF16)
            for cidx in range(tm // CHUNK):
                rc = slice(cidx * CHUNK, (cidx + 1) * CHUNK)
                vb = v[rc, hc]
                mixed = _dot(wmh, vb) + bs_ref[:, h:h + 1]
                da = dmix[rc, hc]
                dpa_ref[rc, hc] = (da * mixed * _gelu_grad(ua[rc, hc], cdf_u[rc, hc])).astype(BF16)
                dmixed = da * u[rc, hc]
                dmb = dmixed.astype(BF16)
                dv_buf[rc, hc] = _dot_tn(wmh, dmb)
                dwm_ref[h] += _dot_nt(dmb, vb)
                db_acc[:, hc] += dmixed
        dv = dv_buf[...]
        dlb_ref[...] += jnp.sum(dv, axis=0, keepdims=True)
        dlg_ref[...] += jnp.sum(dv * xhat, axis=0, keepdims=True)
        dvg = _ln_bwd(dv * lg_ref[...], xhat, rstd)
        dpa_ref[:, D_A:2 * D_A] = (dvg * _gelu_grad(va, cdf_v)).astype(BF16)

        for q in range(H_B):
            cols = slice(q * LANES, (q + 1) * LANES)
            chat, crstd = _ln(c_ref[:, cols])
            z = chat * clg_ref[:, cols] + clb_ref[:, cols]
            sg = _sigmoid(z)
            dz = dmix[:, D_A + q * LANES:D_A + (q + 1) * LANES] * (sg * (1.0 + z * (1.0 - sg)))
            dclb_ref[:, cols] += jnp.sum(dz, axis=0, keepdims=True)
            dclg_ref[:, cols] += jnp.sum(dz * chat, axis=0, keepdims=True)
            dc = _ln_bwd(dz * clg_ref[:, cols], chat, crstd)
            dc_ref[:, cols] = dc
            dcb_ref[:, cols] += jnp.sum(dc, axis=0, keepdims=True)

        @pl.when(i == n_tiles - 1)
        def _():
            for h in range(H_A):
                hc = slice(h * CHUNK, (h + 1) * CHUNK)
                dwm_ref[h] = jnp.where(mask, dwm_ref[h], 0.0)
                dbs_ref[h:h + 1, :] = jnp.sum(db_acc[:, hc].T, axis=0, keepdims=True)

    vec = _full((1, D_A))
    vshape = jax.ShapeDtypeStruct((1, D_A), F32)
    return pl.pallas_call(
        body, name="mixer_bwd_a", grid=(n_tiles,),
        in_specs=[_rows(tm, D), _full((D, D)), _rows(tm, IN_COLS), _rows(tm, D_B), vec, vec,
                  _full((H_A, CHUNK, CHUNK)), _full((CHUNK, H_A)), vec, vec, HBM],
        out_specs=[_rows(tm, 2 * D_A), _rows(tm, D_B), vec, vec, _full((H_A, CHUNK, CHUNK)),
                   _full((H_A, CHUNK)), vec, vec, vec],
        out_shape=[jax.ShapeDtypeStruct((T, 2 * D_A), BF16), jax.ShapeDtypeStruct((T, D_B), F32), vshape, vshape,
                   jax.ShapeDtypeStruct((H_A, CHUNK, CHUNK), F32), jax.ShapeDtypeStruct((H_A, CHUNK), F32),
                   vshape, vshape, vshape],
        scratch_shapes=[pltpu.VMEM((tm, D_A), F32), pltpu.VMEM((CHUNK, D_A), F32)],
        compiler_params=_cparams(32),
    )(*map(_hbm, (dxb, wout, proj, c, lg, lb, wm, bst, clg, clb, dep)))


def _mixer_bwd_b(dc, g, proj, cw, dpa, wg_in, x, g1, dres):
    tm = TM_SGU
    n_tiles = T // tm
    hb = tm // HALO
    bn = wg_in.shape[2]

    def body(dc_ref, dch_ref, g_ref, p_ref, cw_ref, dpa_ref, win_ref, x_ref, g1_ref, dres_ref,
             dp_ref, dcw_ref, dx_ref, dxb_ref, dg1_ref, dcbuf, dwacc):
        i = pl.program_id(0)

        @pl.when(i == 0)
        def _():
            dwacc[...] = jnp.zeros(dwacc.shape, F32)
            dg1_ref[...] = jnp.zeros(dg1_ref.shape, F32)

        _fill_shift_buffer(dcbuf, 0, dc_ref[...])
        _fill_shift_buffer(dcbuf, tm, jnp.where(i < n_tiles - 1, dch_ref[...], 0.0))
        _build_shifts(dcbuf)
        dp_ref[:, 0:2 * D_A] = dpa_ref[...]
        for q in range(H_B):
            cols = slice(q * LANES, (q + 1) * LANES)

            def row_block(rb, carry, q=q, cols=cols):
                base = pl.multiple_of(rb * RB, RB)
                rows = pl.ds(base, RB)
                gv = g_ref[rows, cols]
                acc = [jnp.zeros((RB, LANES), F32) for _ in range(CONV_CHAINS)]
                parts = []
                for k in range(CONV_W):
                    xk = _shifted(dcbuf, q, base, CONV_W - 1 - k)
                    acc[k % CONV_CHAINS] = acc[k % CONV_CHAINS] + xk * cw_ref[k:k + 1, cols]
                    parts.append(jnp.sum((gv * xk).reshape(RB // 8, 8, LANES), axis=0))
                dg = functools.reduce(lambda a, b: a + b, acc)
                val = p_ref[rows, cols]
                sg = _sigmoid(p_ref[rows, D_B + q * LANES:D_B + (q + 1) * LANES])
                dp_ref[rows, 2 * D_A + q * LANES:2 * D_A + (q + 1) * LANES] = (dg * sg).astype(BF16)
                dp_ref[rows, 2 * D_A + D_B + q * LANES:2 * D_A + D_B + (q + 1) * LANES] = (
                    dg * val * sg * (1.0 - sg)).astype(BF16)
                for k in range(CONV_W):
                    dwacc[k * 8:(k + 1) * 8, cols] += parts[k]
                return carry

            lax.fori_loop(0, tm // RB, row_block, 0)

        dh = jnp.zeros((tm, D), F32)
        for j in range(N_DEV):
            dh = dh + _dot_nt(dp_ref[:, j * bn:(j + 1) * bn], win_ref[j])
        xv = x_ref[...]
        dxn, dg = _rms_bwd(dh, xv, _rms(xv), g1_ref[...])
        dx = dres_ref[...] + dxn
        dx_ref[...] = dx
        dxb_ref[...] = dx.astype(BF16)
        dg1_ref[...] += dg

        @pl.when(i == n_tiles - 1)
        def _():
            for k in range(CONV_W):
                dcw_ref[k:k + 1, :] = jnp.sum(dwacc[k * 8:(k + 1) * 8, :], axis=0, keepdims=True)

    return pl.pallas_call(
        body, name="mixer_bwd_b", grid=(n_tiles,),
        in_specs=[_rows(tm, D_B),
                  pl.BlockSpec((HALO, D_B), lambda i: (jnp.minimum((i + 1) * hb, T // HALO - 1), 0)),
                  _rows(tm, D_B),
                  pl.BlockSpec((tm, 2 * D_B), lambda i: (i, 1)),
                  _full((CONV_W, D_B)), _rows(tm, 2 * D_A),
                  pl.BlockSpec(wg_in.shape, lambda i: (0, 0, 0), pipeline_mode=pl.Buffered(1)),
                  _rows(tm, D), _full((1, D)), _rows(tm, D)],
        out_specs=[_rows(tm, IN_COLS), _full((CONV_W, D_B)), _rows(tm, D), _rows(tm, D), _full((1, D))],
        out_shape=[jax.ShapeDtypeStruct((T, IN_COLS), BF16), jax.ShapeDtypeStruct((CONV_W, D_B), F32),
                   jax.ShapeDtypeStruct((T, D), F32), jax.ShapeDtypeStruct((T, D), BF16),
                   jax.ShapeDtypeStruct((1, D), F32)],
        scratch_shapes=[pltpu.VMEM((8, H_B, tm + HALO, LANES), F32), pltpu.VMEM((CONV_W * 8, D_B), F32)],
        compiler_params=_cparams(40),
    )(*map(_hbm, (dc, dc, g, proj, cw, dpa, wg_in, x, g1, dres)))


def _wgrad(name, a, g, bn, square_a=False):
    k = a.shape[1]
    n = g.shape[1]
    tk = min(k, 1024)
    tn = min(n, max(bn, 1024))
    nsub = tn // bn
    tt = TT_WGRAD
    nt = T // tt

    def body(a_ref, g_ref, o_ref, ob_ref):
        t = pl.program_id(2)

        @pl.when(t == 0)
        def _():
            o_ref[...] = jnp.zeros(o_ref.shape, F32)

        av = a_ref[...]
        if square_a:
            av = av * av
        for s in range(nsub):
            o_ref[s] += _dot_tn(av, g_ref[:, s * bn:(s + 1) * bn])

        @pl.when(t == nt - 1)
        def _():
            ob_ref[...] = o_ref[...].astype(BF16)

    ospec = pl.BlockSpec((nsub, tk, bn), lambda ki, ni, ti: (ni, ki, 0))
    return pl.pallas_call(
        body, name=name, grid=(k // tk, n // tn, nt),
        in_specs=[pl.BlockSpec((tt, tk), lambda ki, ni, ti: (ti, ki)),
                  pl.BlockSpec((tt, tn), lambda ki, ni, ti: (ti, ni))],
        out_specs=[ospec, ospec],
        out_shape=[jax.ShapeDtypeStruct((n // bn, k, bn), F32), jax.ShapeDtypeStruct((n // bn, k, bn), BF16)],
        compiler_params=_cparams(40, ("arbitrary", "arbitrary", "arbitrary")),
    )(a, g)


def _coords():
    return lax.axis_index("x"), lax.axis_index("y"), lax.axis_index("c")


def _flat(x, y, c):
    return 4 * x + 2 * y + c


def _peer(k):
    x, y, c = _coords()
    return (x ^ ((k >> 2) & 1), y ^ ((k >> 1) & 1), c ^ (k & 1))


def _hbm(a):
    return pltpu.with_memory_space_constraint(a, pltpu.HBM)


def _hbm_like(a):
    return pltpu.HBM(a.shape, a.dtype)


def _peer_sems():
    return pltpu.SemaphoreType.DMA((N_DEV - 1,))


def _place_own(shards):
    n = len(shards)
    shapes = [(s.shape if l is None else s.shape[1:]) for s, l, _ in shards]

    def body(*refs):
        ins, outs = refs[:n], refs[n:2 * n]
        stage_in, stage_out = refs[2 * n:3 * n], refs[3 * n:4 * n]
        in_sems, out_sems = refs[4 * n], refs[4 * n + 1]
        me = _flat(*_coords())
        srcs = [ins[a] if shards[a][1] is None else ins[a].at[shards[a][1]] for a in range(n)]
        loads = [pltpu.make_async_copy(srcs[a], stage_in[a], in_sems.at[a]) for a in range(n)]
        stores = [pltpu.make_async_copy(stage_out[a], outs[a].at[me], out_sems.at[a]) for a in range(n)]
        for cp in loads:
            cp.start()
        for a in range(n):
            loads[a].wait()
            stage_out[a][...] = stage_in[a][...].astype(stage_out[a].dtype)
            stores[a].start()
        for cp in stores:
            cp.wait()

    return pl.pallas_call(
        body, name="place_own", in_specs=[HBM] * n, out_specs=[HBM] * n,
        out_shape=[pltpu.HBM((N_DEV,) + shapes[a], shards[a][2]) for a in range(n)],
        scratch_shapes=[*[pltpu.VMEM(shapes[a], shards[a][0].dtype) for a in range(n)],
                        *[pltpu.VMEM(shapes[a], shards[a][2]) for a in range(n)],
                        pltpu.SemaphoreType.DMA((n,)), pltpu.SemaphoreType.DMA((n,))],
        compiler_params=pltpu.CompilerParams(vmem_limit_bytes=40 * MIB),
    )(*[_hbm(s) for s, _, _ in shards])


SIBLING = 1
CHIP_PEERS = (2, 4, 6)
FIRST_PEERS = (SIBLING,) + CHIP_PEERS


def _gather_start(lands):
    n = len(lands)

    def body(*refs):
        lnd, send, recv = refs[:n], refs[n:2 * n], refs[2 * n:3 * n]
        me = _flat(*_coords())
        for a in range(n):
            for j, k in enumerate(FIRST_PEERS):
                pltpu.make_async_remote_copy(
                    src_ref=lnd[a].at[me], dst_ref=lnd[a].at[me], send_sem=send[a].at[j],
                    recv_sem=recv[a].at[j], device_id=_peer(k), device_id_type=MESH_ID).start()

    sems = pltpu.SemaphoreType.DMA((len(FIRST_PEERS),))
    outs = pl.pallas_call(
        body, name="gather_start",
        out_shape=(*[sems] * (2 * n), *[_hbm_like(l) for l in lands]),
        in_specs=[HBM] * n, out_specs=(*[SEM] * (2 * n), *[HBM] * n),
        input_output_aliases={i: 2 * n + i for i in range(n)},
        compiler_params=pltpu.CompilerParams(has_side_effects=EFFECT),
    )(*[_hbm(l) for l in lands])
    return outs[:n], outs[n:2 * n], outs[2 * n:]


def _gather_pass_on(name, lands, sends, recvs, after):
    n = len(lands)

    def body(*refs):
        lnd, send, recv = refs[:n], refs[n:2 * n], refs[2 * n:3 * n]
        send2, recv2 = refs[3 * n + 1:4 * n + 1], refs[4 * n + 1:5 * n + 1]
        sibling = _peer(SIBLING)
        for a in range(n):
            for j, k in enumerate(CHIP_PEERS):
                cp = pltpu.make_async_remote_copy(
                    src_ref=lnd[a].at[0], dst_ref=lnd[a].at[0], send_sem=send[a].at[1 + j],
                    recv_sem=recv[a].at[1 + j], device_id=_peer(k), device_id_type=MESH_ID)
                cp.wait_send()
                cp.wait_recv()
                blk = _flat(*_peer(k))
                pltpu.make_async_remote_copy(
                    src_ref=lnd[a].at[blk], dst_ref=lnd[a].at[blk], send_sem=send2[a].at[j],
                    recv_sem=recv2[a].at[j], device_id=sibling, device_id_type=MESH_ID).start()

    sems = pltpu.SemaphoreType.DMA((len(CHIP_PEERS),))
    outs = pl.pallas_call(
        body, name=name, out_shape=(*[sems] * (2 * n), *[_hbm_like(l) for l in lands]),
        in_specs=(*[HBM] * n, *[SEM] * (2 * n), HBM), out_specs=(*[SEM] * (2 * n), *[HBM] * n),
        input_output_aliases={i: 2 * n + i for i in range(n)},
        compiler_params=pltpu.CompilerParams(has_side_effects=EFFECT),
    )(*lands, *sends, *recvs, after)
    return outs[:n], outs[n:2 * n], outs[2 * n:]


def _gather_wait(name, lands, sends, recvs, sends2, recvs2, after):
    n = len(lands)

    def body(*refs):
        lnd, send, recv = refs[:n], refs[n:2 * n], refs[2 * n:3 * n]
        send2, recv2 = refs[3 * n:4 * n], refs[4 * n:5 * n]
        sibling = _peer(SIBLING)
        for a in range(n):
            own = pltpu.make_async_remote_copy(
                src_ref=lnd[a].at[0], dst_ref=lnd[a].at[0], send_sem=send[a].at[0], recv_sem=recv[a].at[0],
                device_id=sibling, device_id_type=MESH_ID)
            own.wait_send()
            own.wait_recv()
            for j in range(len(CHIP_PEERS)):
                cp = pltpu.make_async_remote_copy(
                    src_ref=lnd[a].at[0], dst_ref=lnd[a].at[0], send_sem=send2[a].at[j], recv_sem=recv2[a].at[j],
                    device_id=sibling, device_id_type=MESH_ID)
                cp.wait_send()
                cp.wait_recv()

    return pl.pallas_call(
        body, name=name, out_shape=[_hbm_like(l) for l in lands],
        in_specs=(*[HBM] * n, *[SEM] * (4 * n), HBM), out_specs=[HBM] * n,
        input_output_aliases={i: i for i in range(n)},
        compiler_params=pltpu.CompilerParams(has_side_effects=EFFECT),
    )(*lands, *sends, *recvs, *sends2, *recvs2, after)


def _scatter_start(name, grads, lands, layer):
    n = len(grads)

    def body(*refs):
        g, lnd, send, recv = refs[:n], refs[n:2 * n], refs[2 * n:3 * n], refs[3 * n:4 * n]
        for a in range(n):
            for k in range(1, N_DEV):
                to = _peer(k)
                pltpu.make_async_remote_copy(
                    src_ref=g[a].at[_flat(*to)], dst_ref=lnd[a].at[layer, k - 1], send_sem=send[a].at[k - 1],
                    recv_sem=recv[a].at[k - 1], device_id=to, device_id_type=MESH_ID).start()

    outs = pl.pallas_call(
        body, name=name,
        out_shape=(*[_peer_sems()] * (2 * n), *[_hbm_like(g) for g in grads], *[_hbm_like(l) for l in lands]),
        in_specs=[HBM] * (2 * n), out_specs=(*[SEM] * (2 * n), *[HBM] * (2 * n)),
        input_output_aliases={i: 2 * n + i for i in range(2 * n)},
        compiler_params=pltpu.CompilerParams(has_side_effects=EFFECT),
    )(*[_hbm(g) for g in grads], *[_hbm(l) for l in lands])
    return outs[:n], outs[n:2 * n], outs[2 * n:3 * n], outs[3 * n:]


def _scatter_wait(name, grads, lands, sends, recvs, after):
    n, nw = len(grads), len(lands)

    def body(*refs):
        g, lnd = refs[:n], refs[n:n + nw]
        send, recv = refs[n + nw:2 * n + nw], refs[2 * n + nw:3 * n + nw]
        for a in range(n):
            for k in range(1, N_DEV):
                cp = pltpu.make_async_remote_copy(
                    src_ref=g[a].at[0], dst_ref=lnd[a // DEPTH].at[a % DEPTH, 0],
                    send_sem=send[a].at[k - 1], recv_sem=recv[a].at[k - 1],
                    device_id=_peer(k), device_id_type=MESH_ID)
                cp.wait_send()
                cp.wait_recv()

    outs = pl.pallas_call(
        body, name=name,
        out_shape=(*[_hbm_like(g) for g in grads], *[_hbm_like(l) for l in lands]),
        in_specs=(*[HBM] * (n + nw), *[SEM] * (2 * n), HBM), out_specs=[HBM] * (n + nw),
        input_output_aliases={i: i for i in range(n + nw)},
        compiler_params=pltpu.CompilerParams(has_side_effects=EFFECT),
    )(*grads, *lands, *sends, *recvs, after)
    return outs[n:]


def _rows_block(ref, d, br):
    return ref.at[pl.ds(pl.multiple_of(d * br, 8), br), :]


def _all_reduce_start(part):
    rows = part.shape[0]
    br = rows // N_DEV
    assert br * N_DEV == rows and br % 8 == 0

    def body(p_ref, _slots_in, send, recv, slots):
        me = _flat(*_coords())
        for k in range(1, N_DEV):
            to = _peer(k)
            pltpu.make_async_remote_copy(
                src_ref=_rows_block(p_ref, _flat(*to), br), dst_ref=slots.at[me], send_sem=send.at[k - 1],
                recv_sem=recv.at[k - 1], device_id=to, device_id_type=MESH_ID).start()

    slots = lax.empty((N_DEV, br, LANES), F32)
    return pl.pallas_call(
        body, name="all_reduce_start",
        out_shape=(_peer_sems(), _peer_sems(), _hbm_like(slots)),
        in_specs=[HBM, HBM], out_specs=(SEM, SEM, HBM), input_output_aliases={1: 2},
        compiler_params=pltpu.CompilerParams(has_side_effects=EFFECT),
    )(_hbm(part), _hbm(slots))


def _all_reduce_sum(part, slots, sends, recvs, after):
    rows = part.shape[0]
    br = rows // N_DEV

    def body(p_ref, slots, _total_in, send1, recv1, _after, send2, recv2, total, stage, acc_ref, sems):
        me = _flat(*_coords())
        for k in range(1, N_DEV):
            cp = pltpu.make_async_remote_copy(
                src_ref=_rows_block(p_ref, 0, br), dst_ref=slots.at[0], send_sem=send1.at[k - 1],
                recv_sem=recv1.at[k - 1], device_id=_peer(k), device_id_type=MESH_ID)
            cp.wait_send()
            cp.wait_recv()
        loads = [pltpu.make_async_copy(_rows_block(p_ref, me, br), stage.at[me], sems.at[0])]
        for k in range(1, N_DEV):
            d = _flat(*_peer(k))
            loads.append(pltpu.make_async_copy(slots.at[d], stage.at[d], sems.at[k]))
        for cp in loads:
            cp.start()
        for cp in loads:
            cp.wait()
        acc = stage[0]
        for d in range(1, N_DEV):
            acc = acc + stage[d]
        acc_ref[...] = acc
        store = pltpu.make_async_copy(acc_ref, _rows_block(total, me, br), sems.at[0])
        store.start()
        store.wait()
        for k in range(1, N_DEV):
            pltpu.make_async_remote_copy(
                src_ref=_rows_block(total, me, br), dst_ref=_rows_block(total, me, br), send_sem=send2.at[k - 1],
                recv_sem=recv2.at[k - 1], device_id=_peer(k), device_id_type=MESH_ID).start()

    total = lax.empty(part.shape, F32)
    return pl.pallas_call(
        body, name="all_reduce_sum",
        out_shape=(_peer_sems(), _peer_sems(), _hbm_like(total)),
        in_specs=[HBM, HBM, HBM, SEM, SEM, HBM], out_specs=(SEM, SEM, HBM), input_output_aliases={2: 2},
        scratch_shapes=[pltpu.VMEM((N_DEV, br, LANES), F32), pltpu.VMEM((br, LANES), F32),
                        pltpu.SemaphoreType.DMA((N_DEV,))],
        compiler_params=pltpu.CompilerParams(has_side_effects=EFFECT),
    )(_hbm(part), slots, _hbm(total), sends, recvs, after)


def _all_reduce_wait(total, sends, recvs, deps):
    br = total.shape[0] // N_DEV
    nd = len(deps)

    def body(total, send, recv, *_):
        for k in range(1, N_DEV):
            cp = pltpu.make_async_remote_copy(
                src_ref=_rows_block(total, 0, br), dst_ref=_rows_block(total, 0, br), send_sem=send.at[k - 1],
                recv_sem=recv.at[k - 1], device_id=_peer(k), device_id_type=MESH_ID)
            cp.wait_send()
            cp.wait_recv()

    return pl.pallas_call(
        body, name="all_reduce_wait", out_shape=_hbm_like(total),
        in_specs=[HBM, SEM, SEM] + [HBM] * nd, out_specs=HBM, input_output_aliases={0: 0},
        compiler_params=pltpu.CompilerParams(has_side_effects=EFFECT),
    )(total, sends, recvs, *map(_hbm, deps))


def _adam_math(w, g, m, v):
    m = ADAM_B1 * m + (1.0 - ADAM_B1) * g
    v = ADAM_B2 * v + (1.0 - ADAM_B2) * (g * g)
    m_hat = m / (1.0 - ADAM_B1 ** ADAM_STEP)
    v_hat = v / (1.0 - ADAM_B2 ** ADAM_STEP)
    delta = -ADAM_LR * (m_hat / (jnp.sqrt(v_hat) + ADAM_EPS) + ADAM_WD * w)
    return delta, m, v


def _adam_sharded(name, me, g32, recv, w, m, v):
    _, r, c = w.shape
    tr = min(r, 512)
    nblk = r // tr
    assert len(g32) == DEPTH == 2

    def body(me_ref, g0_ref, g1_ref, recv_ref, w_ref, m_ref, v_ref, g_ref, d_ref, nm_ref, nv_ref):
        g = jnp.where(pl.program_id(0) == 0, g0_ref[...], g1_ref[...])
        for k in range(N_DEV - 1):
            g = g + recv_ref[k].astype(F32)
        delta, nm, nv = _adam_math(w_ref[...], g, m_ref[...], v_ref[...])
        g_ref[...] = g
        d_ref[...] = delta
        nm_ref[...] = nm
        nv_ref[...] = nv

    blk = pl.BlockSpec((None, tr, c), lambda l, i, me_ref: (l, i, 0))
    own0 = pl.BlockSpec((None, tr, c), lambda l, i, me_ref: (me_ref[0], jnp.where(l == 0, i, nblk - 1), 0))
    own1 = pl.BlockSpec((None, tr, c), lambda l, i, me_ref: (me_ref[0], jnp.where(l == 1, i, 0), 0))
    shp = jax.ShapeDtypeStruct(w.shape, F32)
    return pl.pallas_call(
        body, name=name,
        grid_spec=pltpu.PrefetchScalarGridSpec(
            num_scalar_prefetch=1, grid=(DEPTH, nblk),
            in_specs=[own0, own1, pl.BlockSpec((None, N_DEV - 1, tr, c), lambda l, i, me_ref: (l, 0, i, 0)),
                      blk, blk, blk],
            out_specs=[blk] * 4),
        out_shape=[shp] * 4,
        compiler_params=_cparams(32, ("arbitrary", "arbitrary")),
    )(me, *[_hbm(a) for a in (*g32, recv, w, m, v)])


def _adam_small(gs, ws, ms, vs):
    n = len(gs)

    def body(*refs):
        g, w, m, v = refs[:n], refs[n:2 * n], refs[2 * n:3 * n], refs[3 * n:4 * n]
        d, nm, nv = refs[4 * n:5 * n], refs[5 * n:6 * n], refs[6 * n:7 * n]
        for a in range(n):
            delta, new_m, new_v = _adam_math(w[a][...], g[a][...], m[a][...], v[a][...])
            d[a][...] = delta
            nm[a][...] = new_m
            nv[a][...] = new_v

    specs = [_full(a.shape) for a in gs]
    shapes = [jax.ShapeDtypeStruct(a.shape, F32) for a in gs]
    outs = pl.pallas_call(
        body, name="adam_small", grid=(1,),
        in_specs=specs * 4, out_specs=specs * 3, out_shape=shapes * 3,
        compiler_params=_cparams(24),
    )(*gs, *ws, *ms, *vs)
    return outs[:n], outs[n:2 * n], outs[2 * n:]


SMALL = ["norm1_g", "sgu_ln_g", "sgu_ln_b", "sgu_w", "sgu_b", "conv_b", "conv_ln_g", "conv_ln_b", "norm2_g",
         "final_g"]


def _pack(arrays):
    flat = jnp.concatenate([a.reshape(-1) for a in arrays])
    pad = (-flat.shape[0]) % (N_DEV * 8 * LANES)
    return jnp.pad(flat, (0, pad)).reshape(-1, LANES)


def _unpack(packed, shapes):
    flat = packed.reshape(-1)
    out, off = [], 0
    for s in shapes:
        size = 1
        for d in s:
            size *= d
        out.append(flat[off:off + size].reshape(s))
        off += size
    return out


def kernel(x, norm1_g, w_in, sgu_ln_g, sgu_ln_b, sgu_w, sgu_b, conv_w, conv_b, conv_ln_g, conv_ln_b, w_out, norm2_g, w_ff1, w_ff2, final_g, loss_target, m_norm1_g, m_w_in, m_sgu_ln_g, m_sgu_ln_b, m_sgu_w, m_sgu_b, m_conv_w, m_conv_b, m_conv_ln_g, m_conv_ln_b, m_w_out, m_norm2_g, m_w_ff1, m_w_ff2, m_final_g, v_norm1_g, v_w_in, v_sgu_ln_g, v_sgu_ln_b, v_sgu_w, v_sgu_b, v_conv_w, v_conv_b, v_conv_ln_g, v_conv_ln_b, v_w_out, v_norm2_g, v_w_ff1, v_w_ff2, v_final_g):
    x2d = x.reshape(T, D)
    tgt = loss_target.reshape(T, D)
    cw_shard = conv_w.reshape(CONV_W, LANES)

    gnames, shards = [], []
    for l in range(DEPTH):
        for k, w in (("w_in", w_in), ("w_out", w_out), ("w_ff1", w_ff1), ("w_ff2", w_ff2)):
            gnames.append(f"{k}{l}")
            shards.append((w, l, BF16))
        if l == 0:
            gnames.insert(1, "conv_w")
            shards.insert(1, (cw_shard, None, F32))
    sends, recvs, lands = _gather_start(_place_own(shards))
    gidx = {k: i for i, k in enumerate(gnames)}

    passed = {}

    def pass_on(ks, after):
        idx = [gidx[k] for k in ks]
        s2, r2, ld = _gather_pass_on("gather_pass_on_" + "_".join(ks), [lands[i] for i in idx],
                                     [sends[i] for i in idx], [recvs[i] for i in idx], after)
        for j, k in enumerate(ks):
            passed[k] = (s2[j], r2[j], ld[j])

    def gathered(ks, after):
        idx = [gidx[k] for k in ks]
        return _gather_wait("gather_wait_" + "_".join(ks), [passed[k][2] for k in ks], [sends[i] for i in idx],
                            [recvs[i] for i in idx], [passed[k][0] for k in ks], [passed[k][1] for k in ks], after)

    saved = []
    xl = x2d
    cw_full = None
    for l in range(DEPTH):
        g1 = norm1_g[l].reshape(1, D)
        g2 = norm2_g[l].reshape(1, D)
        lg, lb = sgu_ln_g[l].reshape(1, D_A), sgu_ln_b[l].reshape(1, D_A)
        bst = sgu_b[l].T
        cb = conv_b[l].reshape(1, D_B)
        clg, clb = conv_ln_g[l].reshape(1, D_B), conv_ln_b[l].reshape(1, D_B)
        first = [f"w_in{l}"] if l == 0 else [f"w_in{l}", f"w_out{l}"]
        pass_on(first, xl)
        wg_in, *rest = gathered(first, xl)
        h, proj = _fwd_in(xl, g1, wg_in)
        if l == 0:
            pass_on(["conv_w", "w_out0"], h)
            cw_g, wout = gathered(["conv_w", "w_out0"], proj)
            cw_full = cw_g.reshape(N_DEV, DEPTH, CONV_W, D_B // N_DEV).transpose(1, 2, 0, 3).reshape(
                DEPTH, CONV_W, D_B)
        else:
            (wout,) = rest
            pass_on([f"w_ff1{l}"], proj)
        wout = wout.reshape(D, D)
        mix, c, g, x1, h2 = _mixer_fwd(proj, lg, lb, sgu_w[l], bst, cw_full[l], cb, clg, clb, wout, xl, g2)
        pass_on([f"w_ff1{l}"] if l == 0 else [f"w_ff2{l}"], h2)
        (wg_ff1,) = gathered([f"w_ff1{l}"], h2)
        r = _fwd_ff1(h2, wg_ff1)
        if l == 0:
            pass_on(["w_ff20"], r)
        (w2,) = gathered([f"w_ff2{l}"], r)
        w2 = w2.reshape(D_FF, D)
        saved.append(dict(x=xl, h=h, proj=proj, mix=mix, c=c, g=g, x1=x1, h2=h2, r=r, wg_in=wg_in, wout=wout,
                          wg_ff1=wg_ff1, w2=w2, g1=g1, g2=g2, lg=lg, lb=lb, bst=bst, clg=clg, clb=clb))
        if l + 1 < DEPTH:
            xl = _fwd_ff2(r, w2, x1)
        else:
            dx, dxb, loss_part, d_final_g = _fwd_ff2_loss(r, w2, x1, final_g.reshape(1, D), tgt)

    names = ["w_in", "w_out", "w_ff1", "w_ff2"]
    block = dict(w_in=(D, IN_COLS // N_DEV), w_out=(D // N_DEV, D), w_ff1=(D, D_FF // N_DEV), w_ff2=(D_FF // N_DEV, D))
    land = {k: lax.empty((DEPTH, N_DEV - 1) + block[k], BF16) for k in names}
    big32 = {k: [None] * DEPTH for k in names}
    big16 = {k: [None] * DEPTH for k in names}
    ssend = {k: [None] * DEPTH for k in names}
    srecv = {k: [None] * DEPTH for k in names}

    def send_grads(l, grads):
        ks = list(grads)
        for k in ks:
            big32[k][l] = grads[k][0].reshape((N_DEV,) + block[k])
        sends, recvs, g16s, lands = _scatter_start(
            "scatter_start_" + "_".join(ks) + str(l), [grads[k][1].reshape((N_DEV,) + block[k]) for k in ks],
            [land[k] for k in ks], l)
        for i, k in enumerate(ks):
            ssend[k][l], srecv[k][l], big16[k][l], land[k] = sends[i], recvs[i], g16s[i], lands[i]

    small = {}
    d_conv_w = [None] * DEPTH
    for l in reversed(range(DEPTH)):
        s = saved[l]
        send_grads(l, dict(w_ff2=_wgrad("wgrad_ff2", s["r"], dxb, D, square_a=True)))
        df1, dx1, dx1b, dg2 = _bwd_mlp(dxb, dx, s["w2"], s["r"], s["wg_ff1"], s["x1"], s["g2"], big16["w_ff2"][l])
        send_grads(l, dict(w_ff1=_wgrad("wgrad_ff1", s["h2"], df1, D_FF // N_DEV),
                           w_out=_wgrad("wgrad_out", s["mix"], dx1b, D)))
        dpa, dc, dlg, dlb, dwm, dbs, dcb, dclg, dclb = _mixer_bwd_a(
            dx1b, s["wout"], s["proj"], s["c"], s["lg"], s["lb"], sgu_w[l], s["bst"], s["clg"], s["clb"],
            big16["w_out"][l])
        dproj, d_conv_w[l], dx, dxb, dg1 = _mixer_bwd_b(dc, s["g"], s["proj"], cw_full[l], dpa, s["wg_in"], s["x"],
                                                         s["g1"], dx1)
        send_grads(l, dict(w_in=_wgrad("wgrad_in", s["h"], dproj, IN_COLS // N_DEV)))
        small[l] = dict(norm1_g=dg1, sgu_ln_g=dlg, sgu_ln_b=dlb, sgu_w=dwm, sgu_b=dbs, conv_b=dcb, conv_ln_g=dclg,
                        conv_ln_b=dclb, norm2_g=dg2)
    grad_x = dx.reshape(1, T, D)

    me_block = _flat(*_coords()).astype(jnp.int32).reshape(1)
    wmv = dict(w_in=(w_in, m_w_in, v_w_in), w_out=(w_out, m_w_out, v_w_out), w_ff1=(w_ff1, m_w_ff1, v_w_ff1),
               w_ff2=(w_ff2, m_w_ff2, v_w_ff2))
    res = {}

    def update(tag, ks, after):
        recv = _scatter_wait("scatter_wait_" + tag, [g for k in ks for g in big16[k]], [land[k] for k in ks],
                             [q for k in ks for q in ssend[k]], [q for k in ks for q in srecv[k]], after)
        for k, rk in zip(ks, recv):
            res[k] = _adam_sharded("adam_" + k, me_block, big32[k], rk, *wmv[k])

    rep = dict(norm1_g=(norm1_g, m_norm1_g, v_norm1_g), sgu_ln_g=(sgu_ln_g, m_sgu_ln_g, v_sgu_ln_g),
               sgu_ln_b=(sgu_ln_b, m_sgu_ln_b, v_sgu_ln_b), sgu_w=(sgu_w, m_sgu_w, v_sgu_w),
               sgu_b=(sgu_b, m_sgu_b, v_sgu_b), conv_b=(conv_b, m_conv_b, v_conv_b),
               conv_ln_g=(conv_ln_g, m_conv_ln_g, v_conv_ln_g), conv_ln_b=(conv_ln_b, m_conv_ln_b, v_conv_ln_b),
               norm2_g=(norm2_g, m_norm2_g, v_norm2_g), final_g=(final_g, m_final_g, v_final_g))
    parts = []
    for k in SMALL:
        if k == "final_g":
            parts.append(d_final_g.reshape(rep[k][0].shape))
        else:
            parts.append(jnp.stack([small[l][k].reshape(rep[k][0].shape[1:]) for l in range(DEPTH)]))
    parts.append(jnp.stack(d_conv_w))
    parts.append(loss_part[0, 0:1])
    shapes = [p.shape for p in parts]
    packed = _pack(parts)
    ar_sends, ar_recvs, ar_slots = _all_reduce_start(packed)
    ar_sends, ar_recvs, total = _all_reduce_sum(packed, ar_slots, ar_sends, ar_recvs, big16["w_in"][0])
    update("early", ["w_ff2", "w_ff1", "w_out"], total)
    packed_sum = _all_reduce_wait(total, ar_sends, ar_recvs, [res[k][0] for k in ("w_ff2", "w_ff1", "w_out")])
    update("late", ["w_in"], packed_sum)
    summed = _unpack(packed_sum, shapes)
    loss = summed[-1][0]
    me = _flat(*_coords())
    g_conv_w = lax.dynamic_slice_in_dim(summed[-2], me * (D_B // N_DEV), D_B // N_DEV, axis=2)
    g_small = summed[:-2] + [g_conv_w]
    keys = SMALL + ["conv_w"]
    rep["conv_w"] = (conv_w, m_conv_w, v_conv_w)
    def two_d(a):
        return a.reshape(1, -1) if a.ndim == 1 else a

    d_s, nm_s, nv_s = _adam_small([two_d(g) for g in g_small], *[[two_d(rep[k][i]) for k in keys] for i in range(3)])
    for i, k in enumerate(keys):
        shape = rep[k][0].shape
        res[k] = (g_small[i], d_s[i].reshape(shape), nm_s[i].reshape(shape), nv_s[i].reshape(shape))

    order = ["norm1_g", "w_in", "sgu_ln_g", "sgu_ln_b", "sgu_w", "sgu_b", "conv_w", "conv_b", "conv_ln_g",
             "conv_ln_b", "w_out", "norm2_g", "w_ff1", "w_ff2", "final_g"]
    return (loss, grad_x, *[res[k][0] for k in order], *[res[k][1] for k in order],
            *[res[k][2] for k in order], *[res[k][3] for k in order])
```

```python
import functools

import jax
import jax.numpy as jnp
from jax import lax
from jax.experimental import pallas as pl
from jax.experimental.pallas import tpu as pltpu

F32 = jnp.float32
BF16 = jnp.bfloat16

N_DEV = 8
DEPTH = 2
T = 4096
D = 1024
D_A = 512
D_B = 512
CHUNK = 128
H_A = 4
H_B = 4
CONV_W = 31
HALO = 32
D_FF = 4096
IN_COLS = 2048
EPS = 1e-6

ADAM_LR = 0.001
ADAM_B1 = 0.9
ADAM_B2 = 0.999
ADAM_EPS = 1e-08
ADAM_WD = 0.01
ADAM_STEP = 10

TM = 512
TM_FWD = 1024
TT_WGRAD = 4096
TM_SGU = 512
RB = 64
CONV_CHAINS = 2
LANES = 128
MIB = 1024 * 1024
SCOPED_VMEM_MIB = 60

SQRT_HALF = 0.7071067811865476
INV_SQRT_2PI = 0.3989422804014327

MESH_ID = pl.DeviceIdType.MESH
HBM = pl.BlockSpec(memory_space=pltpu.HBM)
SEM = pl.BlockSpec(memory_space=pltpu.SEMAPHORE)
EFFECT = pltpu.SideEffectType.DATAFLOW_SIDE_EFFECTING


def _cparams(vmem_mib, sem=("arbitrary",)):
    assert vmem_mib <= SCOPED_VMEM_MIB
    return pltpu.CompilerParams(dimension_semantics=sem, vmem_limit_bytes=SCOPED_VMEM_MIB * MIB)


def _full(shape):
    return pl.BlockSpec(shape, lambda *_: (0,) * len(shape))


def _rows(tm, cols):
    return pl.BlockSpec((tm, cols), lambda i: (i, 0))


def _gelu(x):
    cdf = 0.5 * (1.0 + lax.erf(x * SQRT_HALF))
    return x * cdf, cdf


def _gelu_grad(x, cdf):
    return cdf + x * (INV_SQRT_2PI * jnp.exp(-0.5 * x * x))


def _sigmoid(x):
    return 1.0 / (1.0 + jnp.exp(-x))


def _ln(x):
    mu = jnp.mean(x, axis=-1, keepdims=True)
    xc = x - mu
    rstd = lax.rsqrt(jnp.mean(xc * xc, axis=-1, keepdims=True) + EPS)
    return xc * rstd, rstd


def _ln_bwd(dyh, xhat, rstd):
    return rstd * (dyh - jnp.mean(dyh, axis=-1, keepdims=True) - xhat * jnp.mean(dyh * xhat, axis=-1, keepdims=True))


def _rms(x):
    return lax.rsqrt(jnp.mean(x * x, axis=-1, keepdims=True) + EPS)


def _rms_bwd(dh, x, r, g):
    n = x * r
    dn = dh * g
    dx = r * (dn - n * jnp.mean(dn * n, axis=-1, keepdims=True))
    return dx, jnp.sum(dh * n, axis=0, keepdims=True)


def _dot(a, b):
    return jnp.dot(a, b, preferred_element_type=F32)


def _dot_nt(a, b):
    return lax.dot_general(a, b, (((1,), (1,)), ((), ())), preferred_element_type=F32)


def _dot_tn(a, b):
    return lax.dot_general(a, b, (((0,), (0,)), ((), ())), preferred_element_type=F32)


def _tril_mask():
    r = lax.broadcasted_iota(jnp.int32, (CHUNK, CHUNK), 0)
    c = lax.broadcasted_iota(jnp.int32, (CHUNK, CHUNK), 1)
    return r >= c


def _fwd_in(x, g1, wg):
    bn = wg.shape[2]

    def body(x_ref, g_ref, w_ref, h_ref, p_ref):
        xv = x_ref[...]
        h = (xv * _rms(xv) * g_ref[...]).astype(BF16)
        h_ref[...] = h
        for j in range(N_DEV):
            p_ref[:, j * bn:(j + 1) * bn] = _dot(h, w_ref[j])

    return pl.pallas_call(
        body, name="fwd_in", grid=(T // TM_FWD,),
        in_specs=[_rows(TM_FWD, D), _full((1, D)), _full(wg.shape)],
        out_specs=[_rows(TM_FWD, D), _rows(TM_FWD, IN_COLS)],
        out_shape=[jax.ShapeDtypeStruct((T, D), BF16), jax.ShapeDtypeStruct((T, IN_COLS), F32)],
        compiler_params=_cparams(32),
    )(*map(_hbm, (x, g1, wg)))


def _fill_shift_buffer(sh_ref, row0, value):
    for q in range(sh_ref.shape[1]):
        sh_ref[0, q, row0:row0 + value.shape[0], :] = value[:, q * LANES:(q + 1) * LANES]


def _build_shifts(sh_ref):
    rows = sh_ref.shape[2]
    for p in range(1, 8):
        for q in range(sh_ref.shape[1]):
            sh_ref[p, q, 0:rows - 8, :] = sh_ref[0, q, p:p + rows - 8, :]


def _shifted(sh_ref, q, base, off):
    start = base + (off - off % 8)
    if not isinstance(start, int):
        start = pl.multiple_of(start, 8)
    return sh_ref[off % 8, q, pl.ds(start, RB), :]


def _conv_taps(sh_ref, w_ref, q, base, first_tap_row, step):
    cols = slice(q * LANES, (q + 1) * LANES)
    acc = [jnp.zeros((RB, LANES), F32) for _ in range(CONV_CHAINS)]
    for k in range(CONV_W):
        term = _shifted(sh_ref, q, base, first_tap_row + step * k) * w_ref[k:k + 1, cols]
        acc[k % CONV_CHAINS] = acc[k % CONV_CHAINS] + term
    return functools.reduce(lambda a, b: a + b, acc)


def _mixer_fwd(proj, lg, lb, wm, bst, cw, cb, clg, clb, wout, x, g2):
    tm = TM_SGU
    hb = tm // HALO

    def body(p_ref, ph_ref, lg_ref, lb_ref, wm_ref, bs_ref, cw_ref, cb_ref, clg_ref, clb_ref, wo_ref, x_ref, g2_ref,
             mix_ref, c_ref, g_ref, x1_ref, h2_ref, gbuf):
        i = pl.program_id(0)
        u, _ = _gelu(p_ref[:, 0:D_A])
        vg, _ = _gelu(p_ref[:, D_A:2 * D_A])
        xhat, _ = _ln(vg)
        v = (xhat * lg_ref[...] + lb_ref[...]).astype(BF16)
        mask = _tril_mask()
        for h in range(H_A):
            hc = slice(h * CHUNK, (h + 1) * CHUNK)
            wmh = jnp.where(mask, wm_ref[h], 0.0).astype(BF16)
            for c in range(tm // CHUNK):
                rc = slice(c * CHUNK, (c + 1) * CHUNK)
                mixed = _dot(wmh, v[rc, hc]) + bs_ref[:, h:h + 1]
                mix_ref[rc, hc] = (u[rc, hc] * mixed).astype(BF16)
        x1_ref[...] = x_ref[...] + _dot(mix_ref[:, 0:D_A], wo_ref[0:D_A, :])

        g = p_ref[:, 2 * D_A:2 * D_A + D_B] * _sigmoid(p_ref[:, 2 * D_A + D_B:IN_COLS])
        g_ref[...] = g
        gh = ph_ref[:, 0:D_B] * _sigmoid(ph_ref[:, D_B:2 * D_B])
        _fill_shift_buffer(gbuf, 0, jnp.where(i > 0, gh, 0.0))
        _fill_shift_buffer(gbuf, HALO, g)
        _build_shifts(gbuf)
        for q in range(H_B):
            cols = slice(q * LANES, (q + 1) * LANES)
            for rb in range(tm // RB):
                acc = _conv_taps(gbuf, cw_ref, q, rb * RB, HALO - (CONV_W - 1), 1)
                c_ref[rb * RB:(rb + 1) * RB, cols] = acc + cb_ref[:, cols]
        for q in range(H_B):
            cols = slice(q * LANES, (q + 1) * LANES)
            chat, _ = _ln(c_ref[:, cols])
            z = chat * clg_ref[:, cols] + clb_ref[:, cols]
            mix_ref[:, D_A + q * LANES:D_A + (q + 1) * LANES] = (z * _sigmoid(z)).astype(BF16)

        x1 = x1_ref[...] + _dot(mix_ref[:, D_A:D], wo_ref[D_A:D, :])
        x1_ref[...] = x1
        h2_ref[...] = (x1 * _rms(x1) * g2_ref[...]).astype(BF16)

    vec = _full((1, D_A))
    return pl.pallas_call(
        body, name="mixer_fwd", grid=(T // tm,),
        in_specs=[_rows(tm, IN_COLS),
                  pl.BlockSpec((HALO, 2 * D_B), lambda i: (jnp.maximum(i * hb - 1, 0), 1)),
                  vec, vec, _full((H_A, CHUNK, CHUNK)), _full((CHUNK, H_A)),
                  _full((CONV_W, D_B)), vec, vec, vec,
                  pl.BlockSpec((D, D), lambda i: (0, 0), pipeline_mode=pl.Buffered(1)), _rows(tm, D), _full((1, D))],
        out_specs=[_rows(tm, D), _rows(tm, D_B), _rows(tm, D_B), _rows(tm, D), _rows(tm, D)],
        out_shape=[jax.ShapeDtypeStruct((T, D), BF16), jax.ShapeDtypeStruct((T, D_B), F32),
                   jax.ShapeDtypeStruct((T, D_B), F32), jax.ShapeDtypeStruct((T, D), F32),
                   jax.ShapeDtypeStruct((T, D), BF16)],
        scratch_shapes=[pltpu.VMEM((8, H_B, HALO + tm, LANES), F32)],
        compiler_params=_cparams(40),
    )(*map(_hbm, (proj, proj, lg, lb, wm, bst, cw, cb, clg, clb, wout, x, g2)))


def _fwd_ff1(h2, wg):
    bn = wg.shape[2]

    def body(h_ref, w_ref, r_ref):
        h = h_ref[...]
        for j in range(N_DEV):
            r_ref[:, j * bn:(j + 1) * bn] = jnp.maximum(_dot(h, w_ref[j]), 0.0).astype(BF16)

    return pl.pallas_call(
        body, name="fwd_ff1", grid=(T // TM_FWD,),
        in_specs=[_rows(TM_FWD, D), _full(wg.shape)],
        out_specs=_rows(TM_FWD, D_FF),
        out_shape=jax.ShapeDtypeStruct((T, D_FF), BF16),
        compiler_params=_cparams(48),
    )(h2, wg)


def _fwd_ff2(r, w2, x1):
    def body(r_ref, w_ref, x_ref, o_ref):
        rv = r_ref[...]
        o_ref[...] = x_ref[...] + _dot(rv * rv, w_ref[...])

    return pl.pallas_call(
        body, name="fwd_ff2", grid=(T // TM_FWD,),
        in_specs=[_rows(TM_FWD, D_FF), _full((D_FF, D)), _rows(TM_FWD, D)],
        out_specs=_rows(TM_FWD, D),
        out_shape=jax.ShapeDtypeStruct((T, D), F32),
        compiler_params=_cparams(48),
    )(r, w2, x1)


def _fwd_ff2_loss(r, w2, x1, gf, tgt):
    def body(r_ref, w_ref, x_ref, g_ref, t_ref, dx_ref, dxb_ref, loss_ref, dg_ref):
        i = pl.program_id(0)

        @pl.when(i == 0)
        def _():
            loss_ref[...] = jnp.zeros(loss_ref.shape, F32)
            dg_ref[...] = jnp.zeros(dg_ref.shape, F32)

        rv = r_ref[...]
        xv = x_ref[...] + _dot(rv * rv, w_ref[...])
        rn = _rms(xv)
        diff = xv * rn * g_ref[...] - t_ref[...]
        loss_ref[...] += 0.5 * jnp.sum(jnp.mean(diff * diff, axis=-1, keepdims=True), axis=0, keepdims=True)
        dx, dg = _rms_bwd(diff * (1.0 / D), xv, rn, g_ref[...])
        dx_ref[...] = dx
        dxb_ref[...] = dx.astype(BF16)
        dg_ref[...] += dg

    return pl.pallas_call(
        body, name="fwd_ff2_loss", grid=(T // TM,),
        in_specs=[_rows(TM, D_FF), pl.BlockSpec((D_FF, D), lambda i: (0, 0), pipeline_mode=pl.Buffered(1)),
                  _rows(TM, D), _full((1, D)), _rows(TM, D)],
        out_specs=[_rows(TM, D), _rows(TM, D), _full((8, LANES)), _full((1, D))],
        out_shape=[jax.ShapeDtypeStruct((T, D), F32), jax.ShapeDtypeStruct((T, D), BF16),
                   jax.ShapeDtypeStruct((8, LANES), F32), jax.ShapeDtypeStruct((1, D), F32)],
        compiler_params=_cparams(40),
    )(*map(_hbm, (r, w2, x1, gf, tgt)))


def _bwd_mlp(dxb, dres, w2, r, wg1, x1, g2, dep):
    bn = wg1.shape[2]

    def body(d_ref, dres_ref, w2_ref, r_ref, w1_ref, x_ref, g_ref, dep_ref, df1_ref, dx_ref, dxb_ref, dg_ref):
        i = pl.program_id(0)

        @pl.when(i == 0)
        def _():
            dg_ref[...] = jnp.zeros(dg_ref.shape, F32)

        d = d_ref[...]
        dh = jnp.zeros((TM, D), F32)
        for j in range(N_DEV):
            cols = slice(j * bn, (j + 1) * bn)
            df1 = (2.0 * r_ref[:, cols].astype(F32) * _dot_nt(d, w2_ref[cols, :])).astype(BF16)
            df1_ref[:, cols] = df1
            dh = dh + _dot_nt(df1, w1_ref[j])
        xv = x_ref[...]
        dxn, dg = _rms_bwd(dh, xv, _rms(xv), g_ref[...])
        dx = dres_ref[...] + dxn
        dx_ref[...] = dx
        dxb_ref[...] = dx.astype(BF16)
        dg_ref[...] += dg

    once = dict(pipeline_mode=pl.Buffered(1))
    return pl.pallas_call(
        body, name="bwd_mlp", grid=(T // TM,),
        in_specs=[_rows(TM, D), _rows(TM, D), pl.BlockSpec((D_FF, D), lambda i: (0, 0), **once), _rows(TM, D_FF),
                  pl.BlockSpec(wg1.shape, lambda i: (0, 0, 0), **once), _rows(TM, D), _full((1, D)), HBM],
        out_specs=[_rows(TM, D_FF), _rows(TM, D), _rows(TM, D), _full((1, D))],
        out_shape=[jax.ShapeDtypeStruct((T, D_FF), BF16), jax.ShapeDtypeStruct((T, D), F32),
                   jax.ShapeDtypeStruct((T, D), BF16), jax.ShapeDtypeStruct((1, D), F32)],
        compiler_params=_cparams(56),
    )(*map(_hbm, (dxb, dres, w2, r, wg1, x1, g2, dep)))


def _mixer_bwd_a(dxb, wout, proj, c, lg, lb, wm, bst, clg, clb, dep):
    tm = TM_SGU
    n_tiles = T // tm

    def body(dx_ref, wo_ref, p_ref, c_ref, lg_ref, lb_ref, wm_ref, bs_ref, clg_ref, clb_ref, dep_ref,
             dpa_ref, dc_ref, dlg_ref, dlb_ref, dwm_ref, dbs_ref, dcb_ref, dclg_ref, dclb_ref,
             dv_buf, db_acc):
        i = pl.program_id(0)

        @pl.when(i == 0)
        def _():
            for ref in (dlg_ref, dlb_ref, dwm_ref, dbs_ref, dcb_ref, dclg_ref, dclb_ref, db_acc):
                ref[...] = jnp.zeros(ref.shape, F32)

        dmix = _dot_nt(dx_ref[...], wo_ref[...])
        ua = p_ref[:, 0:D_A]
        va = p_ref[:, D_A:2 * D_A]
        u, cdf_u = _gelu(ua)
        vg, cdf_v = _gelu(va)
        xhat, rstd = _ln(vg)
        v = (xhat * lg_ref[...] + lb_ref[...]).astype(BF16)
        mask = _tril_mask()
        for h in range(H_A):
            hc = slice(h * CHUNK, (h + 1) * CHUNK)
            wmh = jnp.where(mask, wm_ref[h], 0.0).astype(BF16)
            for cidx in range(tm // CHUNK):
                rc = slice(cidx * CHUNK, (cidx + 1) * CHUNK)
                vb = v[rc, hc]
                mixed = _dot(wmh, vb) + bs_ref[:, h:h + 1]
                da = dmix[rc, hc]
                dpa_ref[rc, hc] = (da * mixed * _gelu_grad(ua[rc, hc], cdf_u[rc, hc])).astype(BF16)
                dmixed = da * u[rc, hc]
                dmb = dmixed.astype(BF16)
                dv_buf[rc, hc] = _dot_tn(wmh, dmb)
                dwm_ref[h] += _dot_nt(dmb, vb)
                db_acc[:, hc] += dmixed
        dv = dv_buf[...]
        dlb_ref[...] += jnp.sum(dv, axis=0, keepdims=True)
        dlg_ref[...] += jnp.sum(dv * xhat, axis=0, keepdims=True)
        dvg = _ln_bwd(dv * lg_ref[...], xhat, rstd)
        dpa_ref[:, D_A:2 * D_A] = (dvg * _gelu_grad(va, cdf_v)).astype(BF16)

        for q in range(H_B):
            cols = slice(q * LANES, (q + 1) * LANES)
            chat, crstd = _ln(c_ref[:, cols])
            z = chat * clg_ref[:, cols] + clb_ref[:, cols]
            sg = _sigmoid(z)
            dz = dmix[:, D_A + q * LANES:D_A + (q + 1) * LANES] * (sg * (1.0 + z * (1.0 - sg)))
            dclb_ref[:, cols] += jnp.sum(dz, axis=0, keepdims=True)
            dclg_ref[:, cols] += jnp.sum(dz * chat, axis=0, keepdims=True)
            dc = _ln_bwd(dz * clg_ref[:, cols], chat, crstd)
            dc_ref[:, cols] = dc
            dcb_ref[:, cols] += jnp.sum(dc, axis=0, keepdims=True)

        @pl.when(i == n_tiles - 1)
        def _():
            for h in range(H_A):
                hc = slice(h * CHUNK, (h + 1) * CHUNK)
                dwm_ref[h] = jnp.where(mask, dwm_ref[h], 0.0)
                dbs_ref[h:h + 1, :] = jnp.sum(db_acc[:, hc].T, axis=0, keepdims=True)

    vec = _full((1, D_A))
    vshape = jax.ShapeDtypeStruct((1, D_A), F32)
    return pl.pallas_call(
        body, name="mixer_bwd_a", grid=(n_tiles,),
        in_specs=[_rows(tm, D), _full((D, D)), _rows(tm, IN_COLS), _rows(tm, D_B), vec, vec,
                  _full((H_A, CHUNK, CHUNK)), _full((CHUNK, H_A)), vec, vec, HBM],
        out_specs=[_rows(tm, 2 * D_A), _rows(tm, D_B), vec, vec, _full((H_A, CHUNK, CHUNK)),
                   _full((H_A, CHUNK)), vec, vec, vec],
        out_shape=[jax.ShapeDtypeStruct((T, 2 * D_A), BF16), jax.ShapeDtypeStruct((T, D_B), F32), vshape, vshape,
                   jax.ShapeDtypeStruct((H_A, CHUNK, CHUNK), F32), jax.ShapeDtypeStruct((H_A, CHUNK), F32),
                   vshape, vshape, vshape],
        scratch_shapes=[pltpu.VMEM((tm, D_A), F32), pltpu.VMEM((CHUNK, D_A), F32)],
        compiler_params=_cparams(32),
    )(*map(_hbm, (dxb, wout, proj, c, lg, lb, wm, bst, clg, clb, dep)))


def _mixer_bwd_b(dc, g, proj, cw, dpa, wg_in, x, g1, dres):
    tm = TM_SGU
    n_tiles = T // tm
    hb = tm // HALO
    bn = wg_in.shape[2]

    def body(dc_ref, dch_ref, g_ref, p_ref, cw_ref, dpa_ref, win_ref, x_ref, g1_ref, dres_ref,
             dp_ref, dcw_ref, dx_ref, dxb_ref, dg1_ref, dcbuf, dwacc):
        i = pl.program_id(0)

        @pl.when(i == 0)
        def _():
            dwacc[...] = jnp.zeros(dwacc.shape, F32)
            dg1_ref[...] = jnp.zeros(dg1_ref.shape, F32)

        _fill_shift_buffer(dcbuf, 0, dc_ref[...])
        _fill_shift_buffer(dcbuf, tm, jnp.where(i < n_tiles - 1, dch_ref[...], 0.0))
        _build_shifts(dcbuf)
        dp_ref[:, 0:2 * D_A] = dpa_ref[...]
        for q in range(H_B):
            cols = slice(q * LANES, (q + 1) * LANES)

            def row_block(rb, carry, q=q, cols=cols):
                base = pl.multiple_of(rb * RB, RB)
                rows = pl.ds(base, RB)
                gv = g_ref[rows, cols]
                acc = [jnp.zeros((RB, LANES), F32) for _ in range(CONV_CHAINS)]
                parts = []
                for k in range(CONV_W):
                    xk = _shifted(dcbuf, q, base, CONV_W - 1 - k)
                    acc[k % CONV_CHAINS] = acc[k % CONV_CHAINS] + xk * cw_ref[k:k + 1, cols]
                    parts.append(jnp.sum((gv * xk).reshape(RB // 8, 8, LANES), axis=0))
                dg = functools.reduce(lambda a, b: a + b, acc)
                val = p_ref[rows, cols]
                sg = _sigmoid(p_ref[rows, D_B + q * LANES:D_B + (q + 1) * LANES])
                dp_ref[rows, 2 * D_A + q * LANES:2 * D_A + (q + 1) * LANES] = (dg * sg).astype(BF16)
                dp_ref[rows, 2 * D_A + D_B + q * LANES:2 * D_A + D_B + (q + 1) * LANES] = (
                    dg * val * sg * (1.0 - sg)).astype(BF16)
                for k in range(CONV_W):
                    dwacc[k * 8:(k + 1) * 8, cols] += parts[k]
                return carry

            lax.fori_loop(0, tm // RB, row_block, 0)

        dh = jnp.zeros((tm, D), F32)
        for j in range(N_DEV):
            dh = dh + _dot_nt(dp_ref[:, j * bn:(j + 1) * bn], win_ref[j])
        xv = x_ref[...]
        dxn, dg = _rms_bwd(dh, xv, _rms(xv), g1_ref[...])
        dx = dres_ref[...] + dxn
        dx_ref[...] = dx
        dxb_ref[...] = dx.astype(BF16)
        dg1_ref[...] += dg

        @pl.when(i == n_tiles - 1)
        def _():
            for k in range(CONV_W):
                dcw_ref[k:k + 1, :] = jnp.sum(dwacc[k * 8:(k + 1) * 8, :], axis=0, keepdims=True)

    return pl.pallas_call(
        body, name="mixer_bwd_b", grid=(n_tiles,),
        in_specs=[_rows(tm, D_B),
                  pl.BlockSpec((HALO, D_B), lambda i: (jnp.minimum((i + 1) * hb, T // HALO - 1), 0)),
                  _rows(tm, D_B),
                  pl.BlockSpec((tm, 2 * D_B), lambda i: (i, 1)),
                  _full((CONV_W, D_B)), _rows(tm, 2 * D_A),
                  pl.BlockSpec(wg_in.shape, lambda i: (0, 0, 0), pipeline_mode=pl.Buffered(1)),
                  _rows(tm, D), _full((1, D)), _rows(tm, D)],
        out_specs=[_rows(tm, IN_COLS), _full((CONV_W, D_B)), _rows(tm, D), _rows(tm, D), _full((1, D))],
        out_shape=[jax.ShapeDtypeStruct((T, IN_COLS), BF16), jax.ShapeDtypeStruct((CONV_W, D_B), F32),
                   jax.ShapeDtypeStruct((T, D), F32), jax.ShapeDtypeStruct((T, D), BF16),
                   jax.ShapeDtypeStruct((1, D), F32)],
        scratch_shapes=[pltpu.VMEM((8, H_B, tm + HALO, LANES), F32), pltpu.VMEM((CONV_W * 8, D_B), F32)],
        compiler_params=_cparams(40),
    )(*map(_hbm, (dc, dc, g, proj, cw, dpa, wg_in, x, g1, dres)))


def _wgrad(name, a, g, bn, square_a=False, dep=None):
    deps = [] if dep is None else [dep]
    k = a.shape[1]
    n = g.shape[1]
    tk = min(k, 1024)
    tn = min(n, max(bn, 1024))
    nsub = tn // bn
    tt = TT_WGRAD
    nt = T // tt

    def body(a_ref, g_ref, *refs):
        o_ref, ob_ref = refs[len(deps):]
        t = pl.program_id(2)

        @pl.when(t == 0)
        def _():
            o_ref[...] = jnp.zeros(o_ref.shape, F32)

        av = a_ref[...]
        if square_a:
            av = av * av
        for s in range(nsub):
            o_ref[s] += _dot_tn(av, g_ref[:, s * bn:(s + 1) * bn])

        @pl.when(t == nt - 1)
        def _():
            ob_ref[...] = o_ref[...].astype(BF16)

    ospec = pl.BlockSpec((nsub, tk, bn), lambda ki, ni, ti: (ni, ki, 0))
    return pl.pallas_call(
        body, name=name, grid=(k // tk, n // tn, nt),
        in_specs=[pl.BlockSpec((tt, tk), lambda ki, ni, ti: (ti, ki)),
                  pl.BlockSpec((tt, tn), lambda ki, ni, ti: (ti, ni))] + [HBM] * len(deps),
        out_specs=[ospec, ospec],
        out_shape=[jax.ShapeDtypeStruct((n // bn, k, bn), F32), jax.ShapeDtypeStruct((n // bn, k, bn), BF16)],
        compiler_params=_cparams(40, ("arbitrary", "arbitrary", "arbitrary")),
    )(a, g, *map(_hbm, deps))


def _coords():
    return lax.axis_index("x"), lax.axis_index("y"), lax.axis_index("c")


def _flat(x, y, c):
    return 4 * x + 2 * y + c


def _peer(k):
    x, y, c = _coords()
    return (x ^ ((k >> 2) & 1), y ^ ((k >> 1) & 1), c ^ (k & 1))


def _hbm(a):
    return pltpu.with_memory_space_constraint(a, pltpu.HBM)


def _hbm_like(a):
    return pltpu.HBM(a.shape, a.dtype)


def _peer_sems():
    return pltpu.SemaphoreType.DMA((N_DEV - 1,))


def _place_own(shards):
    n = len(shards)
    shapes = [(s.shape if l is None else s.shape[1:]) for s, l, _ in shards]

    def body(*refs):
        ins, outs = refs[:n], refs[n:2 * n]
        stage_in, stage_out = refs[2 * n:3 * n], refs[3 * n:4 * n]
        in_sems, out_sems = refs[4 * n], refs[4 * n + 1]
        me = _flat(*_coords())
        srcs = [ins[a] if shards[a][1] is None else ins[a].at[shards[a][1]] for a in range(n)]
        loads = [pltpu.make_async_copy(srcs[a], stage_in[a], in_sems.at[a]) for a in range(n)]
        stores = [pltpu.make_async_copy(stage_out[a], outs[a].at[me], out_sems.at[a]) for a in range(n)]
        for cp in loads:
            cp.start()
        for a in range(n):
            loads[a].wait()
            stage_out[a][...] = stage_in[a][...].astype(stage_out[a].dtype)
            stores[a].start()
        for cp in stores:
            cp.wait()

    return pl.pallas_call(
        body, name="place_own", in_specs=[HBM] * n, out_specs=[HBM] * n,
        out_shape=[pltpu.HBM((N_DEV,) + shapes[a], shards[a][2]) for a in range(n)],
        scratch_shapes=[*[pltpu.VMEM(shapes[a], shards[a][0].dtype) for a in range(n)],
                        *[pltpu.VMEM(shapes[a], shards[a][2]) for a in range(n)],
                        pltpu.SemaphoreType.DMA((n,)), pltpu.SemaphoreType.DMA((n,))],
        compiler_params=pltpu.CompilerParams(vmem_limit_bytes=40 * MIB),
    )(*[_hbm(s) for s, _, _ in shards])


SIBLING = 1
CHIP_PEERS = (2, 4, 6)
FIRST_PEERS = (SIBLING,) + CHIP_PEERS


def _gather_start(lands):
    n = len(lands)

    def body(*refs):
        lnd, send, recv = refs[:n], refs[n:2 * n], refs[2 * n:3 * n]
        me = _flat(*_coords())
        for a in range(n):
            for j, k in enumerate(FIRST_PEERS):
                pltpu.make_async_remote_copy(
                    src_ref=lnd[a].at[me], dst_ref=lnd[a].at[me], send_sem=send[a].at[j],
                    recv_sem=recv[a].at[j], device_id=_peer(k), device_id_type=MESH_ID).start()

    sems = pltpu.SemaphoreType.DMA((len(FIRST_PEERS),))
    outs = pl.pallas_call(
        body, name="gather_start",
        out_shape=(*[sems] * (2 * n), *[_hbm_like(l) for l in lands]),
        in_specs=[HBM] * n, out_specs=(*[SEM] * (2 * n), *[HBM] * n),
        input_output_aliases={i: 2 * n + i for i in range(n)},
        compiler_params=pltpu.CompilerParams(has_side_effects=EFFECT),
    )(*[_hbm(l) for l in lands])
    return outs[:n], outs[n:2 * n], outs[2 * n:]


def _gather_pass_on(name, lands, sends, recvs, after):
    n = len(lands)

    def body(*refs):
        lnd, send, recv = refs[:n], refs[n:2 * n], refs[2 * n:3 * n]
        send2, recv2 = refs[3 * n + 1:4 * n + 1], refs[4 * n + 1:5 * n + 1]
        sibling = _peer(SIBLING)
        for a in range(n):
            for j, k in enumerate(CHIP_PEERS):
                cp = pltpu.make_async_remote_copy(
                    src_ref=lnd[a].at[0], dst_ref=lnd[a].at[0], send_sem=send[a].at[1 + j],
                    recv_sem=recv[a].at[1 + j], device_id=_peer(k), device_id_type=MESH_ID)
                cp.wait_send()
                cp.wait_recv()
                blk = _flat(*_peer(k))
                pltpu.make_async_remote_copy(
                    src_ref=lnd[a].at[blk], dst_ref=lnd[a].at[blk], send_sem=send2[a].at[j],
                    recv_sem=recv2[a].at[j], device_id=sibling, device_id_type=MESH_ID).start()

    sems = pltpu.SemaphoreType.DMA((len(CHIP_PEERS),))
    outs = pl.pallas_call(
        body, name=name, out_shape=(*[sems] * (2 * n), *[_hbm_like(l) for l in lands]),
        in_specs=(*[HBM] * n, *[SEM] * (2 * n), HBM), out_specs=(*[SEM] * (2 * n), *[HBM] * n),
        input_output_aliases={i: 2 * n + i for i in range(n)},
        compiler_params=pltpu.CompilerParams(has_side_effects=EFFECT),
    )(*lands, *sends, *recvs, after)
    return outs[:n], outs[n:2 * n], outs[2 * n:]


def _gather_wait(name, lands, sends, recvs, sends2, recvs2, after):
    n = len(lands)

    def body(*refs):
        lnd, send, recv = refs[:n], refs[n:2 * n], refs[2 * n:3 * n]
        send2, recv2 = refs[3 * n:4 * n], refs[4 * n:5 * n]
        sibling = _peer(SIBLING)
        for a in range(n):
            own = pltpu.make_async_remote_copy(
                src_ref=lnd[a].at[0], dst_ref=lnd[a].at[0], send_sem=send[a].at[0], recv_sem=recv[a].at[0],
                device_id=sibling, device_id_type=MESH_ID)
            own.wait_send()
            own.wait_recv()
            for j in range(len(CHIP_PEERS)):
                cp = pltpu.make_async_remote_copy(
                    src_ref=lnd[a].at[0], dst_ref=lnd[a].at[0], send_sem=send2[a].at[j], recv_sem=recv2[a].at[j],
                    device_id=sibling, device_id_type=MESH_ID)
                cp.wait_send()
                cp.wait_recv()

    return pl.pallas_call(
        body, name=name, out_shape=[_hbm_like(l) for l in lands],
        in_specs=(*[HBM] * n, *[SEM] * (4 * n), HBM), out_specs=[HBM] * n,
        input_output_aliases={i: i for i in range(n)},
        compiler_params=pltpu.CompilerParams(has_side_effects=EFFECT),
    )(*lands, *sends, *recvs, *sends2, *recvs2, after)


def _scatter_start(name, grads, lands, layer, after=None):
    n = len(grads)
    afters = [] if after is None else [after]

    def body(*refs):
        g, lnd = refs[:n], refs[n:2 * n]
        send, recv = refs[2 * n + len(afters):3 * n + len(afters)], refs[3 * n + len(afters):4 * n + len(afters)]
        for a in range(n):
            for k in range(1, N_DEV):
                to = _peer(k)
                pltpu.make_async_remote_copy(
                    src_ref=g[a].at[_flat(*to)], dst_ref=lnd[a].at[layer, k - 1], send_sem=send[a].at[k - 1],
                    recv_sem=recv[a].at[k - 1], device_id=to, device_id_type=MESH_ID).start()

    outs = pl.pallas_call(
        body, name=name,
        out_shape=(*[_peer_sems()] * (2 * n), *[_hbm_like(g) for g in grads], *[_hbm_like(l) for l in lands]),
        in_specs=[HBM] * (2 * n + len(afters)), out_specs=(*[SEM] * (2 * n), *[HBM] * (2 * n)),
        input_output_aliases={i: 2 * n + i for i in range(2 * n)},
        compiler_params=pltpu.CompilerParams(has_side_effects=EFFECT),
    )(*[_hbm(g) for g in grads], *[_hbm(l) for l in lands], *afters)
    return outs[:n], outs[n:2 * n], outs[2 * n:3 * n], outs[3 * n:]


def _scatter_wait(name, grads, lands, sends, recvs, after):
    n, nw = len(grads), len(lands)

    def body(*refs):
        g, lnd = refs[:n], refs[n:n + nw]
        send, recv = refs[n + nw:2 * n + nw], refs[2 * n + nw:3 * n + nw]
        for a in range(n):
            for k in range(1, N_DEV):
                cp = pltpu.make_async_remote_copy(
                    src_ref=g[a].at[0], dst_ref=lnd[a // DEPTH].at[a % DEPTH, 0],
                    send_sem=send[a].at[k - 1], recv_sem=recv[a].at[k - 1],
                    device_id=_peer(k), device_id_type=MESH_ID)
                cp.wait_send()
                cp.wait_recv()

    outs = pl.pallas_call(
        body, name=name,
        out_shape=(*[_hbm_like(g) for g in grads], *[_hbm_like(l) for l in lands]),
        in_specs=(*[HBM] * (n + nw), *[SEM] * (2 * n), HBM), out_specs=[HBM] * (n + nw),
        input_output_aliases={i: i for i in range(n + nw)},
        compiler_params=pltpu.CompilerParams(has_side_effects=EFFECT),
    )(*grads, *lands, *sends, *recvs, after)
    return outs[n:]


def _rows_block(ref, d, br):
    return ref.at[pl.ds(pl.multiple_of(d * br, 8), br), :]


def _all_reduce_start(part):
    rows = part.shape[0]
    br = rows // N_DEV
    assert br * N_DEV == rows and br % 8 == 0

    def body(p_ref, _slots_in, send, recv, slots):
        me = _flat(*_coords())
        for k in range(1, N_DEV):
            to = _peer(k)
            pltpu.make_async_remote_copy(
                src_ref=_rows_block(p_ref, _flat(*to), br), dst_ref=slots.at[me], send_sem=send.at[k - 1],
                recv_sem=recv.at[k - 1], device_id=to, device_id_type=MESH_ID).start()

    slots = lax.empty((N_DEV, br, LANES), F32)
    return pl.pallas_call(
        body, name="all_reduce_start",
        out_shape=(_peer_sems(), _peer_sems(), _hbm_like(slots)),
        in_specs=[HBM, HBM], out_specs=(SEM, SEM, HBM), input_output_aliases={1: 2},
        compiler_params=pltpu.CompilerParams(has_side_effects=EFFECT),
    )(_hbm(part), _hbm(slots))


def _all_reduce_sum(part, slots, sends, recvs, after):
    rows = part.shape[0]
    br = rows // N_DEV

    def body(p_ref, slots, _total_in, send1, recv1, _after, send2, recv2, total, stage, acc_ref, sems):
        me = _flat(*_coords())
        for k in range(1, N_DEV):
            cp = pltpu.make_async_remote_copy(
                src_ref=_rows_block(p_ref, 0, br), dst_ref=slots.at[0], send_sem=send1.at[k - 1],
                recv_sem=recv1.at[k - 1], device_id=_peer(k), device_id_type=MESH_ID)
            cp.wait_send()
            cp.wait_recv()
        loads = [pltpu.make_async_copy(_rows_block(p_ref, me, br), stage.at[me], sems.at[0])]
        for k in range(1, N_DEV):
            d = _flat(*_peer(k))
            loads.append(pltpu.make_async_copy(slots.at[d], stage.at[d], sems.at[k]))
        for cp in loads:
            cp.start()
        for cp in loads:
            cp.wait()
        acc = stage[0]
        for d in range(1, N_DEV):
            acc = acc + stage[d]
        acc_ref[...] = acc
        store = pltpu.make_async_copy(acc_ref, _rows_block(total, me, br), sems.at[0])
        store.start()
        store.wait()
        for k in range(1, N_DEV):
            pltpu.make_async_remote_copy(
                src_ref=_rows_block(total, me, br), dst_ref=_rows_block(total, me, br), send_sem=send2.at[k - 1],
                recv_sem=recv2.at[k - 1], device_id=_peer(k), device_id_type=MESH_ID).start()

    total = lax.empty(part.shape, F32)
    return pl.pallas_call(
        body, name="all_reduce_sum",
        out_shape=(_peer_sems(), _peer_sems(), _hbm_like(total)),
        in_specs=[HBM, HBM, HBM, SEM, SEM, HBM], out_specs=(SEM, SEM, HBM), input_output_aliases={2: 2},
        scratch_shapes=[pltpu.VMEM((N_DEV, br, LANES), F32), pltpu.VMEM((br, LANES), F32),
                        pltpu.SemaphoreType.DMA((N_DEV,))],
        compiler_params=pltpu.CompilerParams(has_side_effects=EFFECT),
    )(_hbm(part), slots, _hbm(total), sends, recvs, after)


def _all_reduce_wait(total, sends, recvs, deps):
    br = total.shape[0] // N_DEV
    nd = len(deps)

    def body(total, send, recv, *_):
        for k in range(1, N_DEV):
            cp = pltpu.make_async_remote_copy(
                src_ref=_rows_block(total, 0, br), dst_ref=_rows_block(total, 0, br), send_sem=send.at[k - 1],
                recv_sem=recv.at[k - 1], device_id=_peer(k), device_id_type=MESH_ID)
            cp.wait_send()
            cp.wait_recv()

    return pl.pallas_call(
        body, name="all_reduce_wait", out_shape=_hbm_like(total),
        in_specs=[HBM, SEM, SEM] + [HBM] * nd, out_specs=HBM, input_output_aliases={0: 0},
        compiler_params=pltpu.CompilerParams(has_side_effects=EFFECT),
    )(total, sends, recvs, *map(_hbm, deps))


def _adam_math(w, g, m, v):
    m = ADAM_B1 * m + (1.0 - ADAM_B1) * g
    v = ADAM_B2 * v + (1.0 - ADAM_B2) * (g * g)
    m_hat = m / (1.0 - ADAM_B1 ** ADAM_STEP)
    v_hat = v / (1.0 - ADAM_B2 ** ADAM_STEP)
    delta = -ADAM_LR * (m_hat / (jnp.sqrt(v_hat) + ADAM_EPS) + ADAM_WD * w)
    return delta, m, v


def _adam_sharded(name, me, g32, recv, w, m, v):
    _, r, c = w.shape
    tr = min(r, 512)
    nblk = r // tr
    assert len(g32) == DEPTH == 2

    def body(me_ref, g0_ref, g1_ref, recv_ref, w_ref, m_ref, v_ref, g_ref, d_ref, nm_ref, nv_ref):
        g = jnp.where(pl.program_id(0) == 0, g0_ref[...], g1_ref[...])
        for k in range(N_DEV - 1):
            g = g + recv_ref[k].astype(F32)
        delta, nm, nv = _adam_math(w_ref[...], g, m_ref[...], v_ref[...])
        g_ref[...] = g
        d_ref[...] = delta
        nm_ref[...] = nm
        nv_ref[...] = nv

    blk = pl.BlockSpec((None, tr, c), lambda l, i, me_ref: (l, i, 0))
    own0 = pl.BlockSpec((None, tr, c), lambda l, i, me_ref: (me_ref[0], jnp.where(l == 0, i, nblk - 1), 0))
    own1 = pl.BlockSpec((None, tr, c), lambda l, i, me_ref: (me_ref[0], jnp.where(l == 1, i, 0), 0))
    shp = jax.ShapeDtypeStruct(w.shape, F32)
    return pl.pallas_call(
        body, name=name,
        grid_spec=pltpu.PrefetchScalarGridSpec(
            num_scalar_prefetch=1, grid=(DEPTH, nblk),
            in_specs=[own0, own1, pl.BlockSpec((None, N_DEV - 1, tr, c), lambda l, i, me_ref: (l, 0, i, 0)),
                      blk, blk, blk],
            out_specs=[blk] * 4),
        out_shape=[shp] * 4,
        compiler_params=_cparams(32, ("arbitrary", "arbitrary")),
    )(me, *[_hbm(a) for a in (*g32, recv, w, m, v)])


def _adam_small(gs, ws, ms, vs):
    n = len(gs)

    def body(*refs):
        g, w, m, v = refs[:n], refs[n:2 * n], refs[2 * n:3 * n], refs[3 * n:4 * n]
        d, nm, nv = refs[4 * n:5 * n], refs[5 * n:6 * n], refs[6 * n:7 * n]
        for a in range(n):
            delta, new_m, new_v = _adam_math(w[a][...], g[a][...], m[a][...], v[a][...])
            d[a][...] = delta
            nm[a][...] = new_m
            nv[a][...] = new_v

    specs = [_full(a.shape) for a in gs]
    shapes = [jax.ShapeDtypeStruct(a.shape, F32) for a in gs]
    outs = pl.pallas_call(
        body, name="adam_small", grid=(1,),
        in_specs=specs * 4, out_specs=specs * 3, out_shape=shapes * 3,
        compiler_params=_cparams(24),
    )(*gs, *ws, *ms, *vs)
    return outs[:n], outs[n:2 * n], outs[2 * n:]


SMALL = ["norm1_g", "sgu_ln_g", "sgu_ln_b", "sgu_w", "sgu_b", "conv_b", "conv_ln_g", "conv_ln_b", "norm2_g",
         "final_g"]


def _pack(arrays):
    flat = jnp.concatenate([a.reshape(-1) for a in arrays])
    pad = (-flat.shape[0]) % (N_DEV * 8 * LANES)
    return jnp.pad(flat, (0, pad)).reshape(-1, LANES)


def _unpack(packed, shapes):
    flat = packed.reshape(-1)
    out, off = [], 0
    for s in shapes:
        size = 1
        for d in s:
            size *= d
        out.append(flat[off:off + size].reshape(s))
        off += size
    return out


def kernel(x, norm1_g, w_in, sgu_ln_g, sgu_ln_b, sgu_w, sgu_b, conv_w, conv_b, conv_ln_g, conv_ln_b, w_out, norm2_g, w_ff1, w_ff2, final_g, loss_target, m_norm1_g, m_w_in, m_sgu_ln_g, m_sgu_ln_b, m_sgu_w, m_sgu_b, m_conv_w, m_conv_b, m_conv_ln_g, m_conv_ln_b, m_w_out, m_norm2_g, m_w_ff1, m_w_ff2, m_final_g, v_norm1_g, v_w_in, v_sgu_ln_g, v_sgu_ln_b, v_sgu_w, v_sgu_b, v_conv_w, v_conv_b, v_conv_ln_g, v_conv_ln_b, v_w_out, v_norm2_g, v_w_ff1, v_w_ff2, v_final_g):
    x2d = x.reshape(T, D)
    tgt = loss_target.reshape(T, D)
    cw_shard = conv_w.reshape(CONV_W, LANES)

    gnames, shards = [], []
    for l in range(DEPTH):
        for k, w in (("w_in", w_in), ("w_out", w_out), ("w_ff1", w_ff1), ("w_ff2", w_ff2)):
            gnames.append(f"{k}{l}")
            shards.append((w, l, BF16))
        if l == 0:
            gnames.insert(1, "conv_w")
            shards.insert(1, (cw_shard, None, F32))
    sends, recvs, lands = _gather_start(_place_own(shards))
    gidx = {k: i for i, k in enumerate(gnames)}

    passed = {}

    def pass_on(ks, after):
        idx = [gidx[k] for k in ks]
        s2, r2, ld = _gather_pass_on("gather_pass_on_" + "_".join(ks), [lands[i] for i in idx],
                                     [sends[i] for i in idx], [recvs[i] for i in idx], after)
        for j, k in enumerate(ks):
            passed[k] = (s2[j], r2[j], ld[j])

    def gathered(ks, after):
        idx = [gidx[k] for k in ks]
        return _gather_wait("gather_wait_" + "_".join(ks), [passed[k][2] for k in ks], [sends[i] for i in idx],
                            [recvs[i] for i in idx], [passed[k][0] for k in ks], [passed[k][1] for k in ks], after)

    saved = []
    xl = x2d
    cw_full = None
    for l in range(DEPTH):
        g1 = norm1_g[l].reshape(1, D)
        g2 = norm2_g[l].reshape(1, D)
        lg, lb = sgu_ln_g[l].reshape(1, D_A), sgu_ln_b[l].reshape(1, D_A)
        bst = sgu_b[l].T
        cb = conv_b[l].reshape(1, D_B)
        clg, clb = conv_ln_g[l].reshape(1, D_B), conv_ln_b[l].reshape(1, D_B)
        first = [f"w_in{l}"] if l == 0 else [f"w_in{l}", f"w_out{l}"]
        pass_on(first, xl)
        wg_in, *rest = gathered(first, xl)
        h, proj = _fwd_in(xl, g1, wg_in)
        if l == 0:
            pass_on(["conv_w", "w_out0"], h)
            cw_g, wout = gathered(["conv_w", "w_out0"], proj)
            cw_full = cw_g.reshape(N_DEV, DEPTH, CONV_W, D_B // N_DEV).transpose(1, 2, 0, 3).reshape(
                DEPTH, CONV_W, D_B)
        else:
            (wout,) = rest
            pass_on([f"w_ff1{l}"], proj)
        wout = wout.reshape(D, D)
        mix, c, g, x1, h2 = _mixer_fwd(proj, lg, lb, sgu_w[l], bst, cw_full[l], cb, clg, clb, wout, xl, g2)
        pass_on([f"w_ff1{l}"] if l == 0 else [f"w_ff2{l}"], h2)
        (wg_ff1,) = gathered([f"w_ff1{l}"], h2)
        r = _fwd_ff1(h2, wg_ff1)
        if l == 0:
            pass_on(["w_ff20"], r)
        (w2,) = gathered([f"w_ff2{l}"], r)
        w2 = w2.reshape(D_FF, D)
        saved.append(dict(x=xl, h=h, proj=proj, mix=mix, c=c, g=g, x1=x1, h2=h2, r=r, wg_in=wg_in, wout=wout,
                          wg_ff1=wg_ff1, w2=w2, g1=g1, g2=g2, lg=lg, lb=lb, bst=bst, clg=clg, clb=clb))
        if l + 1 < DEPTH:
            xl = _fwd_ff2(r, w2, x1)
        else:
            dx, dxb, loss_part, d_final_g = _fwd_ff2_loss(r, w2, x1, final_g.reshape(1, D), tgt)

    names = ["w_in", "w_out", "w_ff1", "w_ff2"]
    block = dict(w_in=(D, IN_COLS // N_DEV), w_out=(D // N_DEV, D), w_ff1=(D, D_FF // N_DEV), w_ff2=(D_FF // N_DEV, D))
    land = {k: lax.empty((DEPTH, N_DEV - 1) + block[k], BF16) for k in names}
    big32 = {k: [None] * DEPTH for k in names}
    big16 = {k: [None] * DEPTH for k in names}
    ssend = {k: [None] * DEPTH for k in names}
    srecv = {k: [None] * DEPTH for k in names}

    def send_grads(l, grads, after=None):
        ks = list(grads)
        for k in ks:
            big32[k][l] = grads[k][0].reshape((N_DEV,) + block[k])
        sends, recvs, g16s, lands = _scatter_start(
            "scatter_start_" + "_".join(ks) + str(l), [grads[k][1].reshape((N_DEV,) + block[k]) for k in ks],
            [land[k] for k in ks], l, after)
        for i, k in enumerate(ks):
            ssend[k][l], srecv[k][l], big16[k][l], land[k] = sends[i], recvs[i], g16s[i], lands[i]

    rep = dict(norm1_g=(norm1_g, m_norm1_g, v_norm1_g), sgu_ln_g=(sgu_ln_g, m_sgu_ln_g, v_sgu_ln_g),
               sgu_ln_b=(sgu_ln_b, m_sgu_ln_b, v_sgu_ln_b), sgu_w=(sgu_w, m_sgu_w, v_sgu_w),
               sgu_b=(sgu_b, m_sgu_b, v_sgu_b), conv_b=(conv_b, m_conv_b, v_conv_b),
               conv_ln_g=(conv_ln_g, m_conv_ln_g, v_conv_ln_g), conv_ln_b=(conv_ln_b, m_conv_ln_b, v_conv_ln_b),
               norm2_g=(norm2_g, m_norm2_g, v_norm2_g), final_g=(final_g, m_final_g, v_final_g))

    def small_parts():
        parts = []
        for k in SMALL:
            if k == "final_g":
                parts.append(d_final_g.reshape(rep[k][0].shape))
            else:
                parts.append(jnp.stack([small[l][k].reshape(rep[k][0].shape[1:]) for l in range(DEPTH)]))
        parts.append(jnp.stack(d_conv_w))
        parts.append(loss_part[0, 0:1])
        return parts

    small = {}
    d_conv_w = [None] * DEPTH
    for l in reversed(range(DEPTH)):
        s = saved[l]
        send_grads(l, dict(w_ff2=_wgrad("wgrad_ff2", s["r"], dxb, D, square_a=True)))
        df1, dx1, dx1b, dg2 = _bwd_mlp(dxb, dx, s["w2"], s["r"], s["wg_ff1"], s["x1"], s["g2"], big16["w_ff2"][l])
        send_grads(l, dict(w_ff1=_wgrad("wgrad_ff1", s["h2"], df1, D_FF // N_DEV),
                           w_out=_wgrad("wgrad_out", s["mix"], dx1b, D)))
        dpa, dc, dlg, dlb, dwm, dbs, dcb, dclg, dclb = _mixer_bwd_a(
            dx1b, s["wout"], s["proj"], s["c"], s["lg"], s["lb"], sgu_w[l], s["bst"], s["clg"], s["clb"],
            big16["w_out"][l])
        dproj, d_conv_w[l], dx, dxb, dg1 = _mixer_bwd_b(dc, s["g"], s["proj"], cw_full[l], dpa, s["wg_in"], s["x"],
                                                         s["g1"], dx1)
        small[l] = dict(norm1_g=dg1, sgu_ln_g=dlg, sgu_ln_b=dlb, sgu_w=dwm, sgu_b=dbs, conv_b=dcb, conv_ln_g=dclg,
                        conv_ln_b=dclb, norm2_g=dg2)
        if l > 0:
            send_grads(l, dict(w_in=_wgrad("wgrad_in", s["h"], dproj, IN_COLS // N_DEV)))
        else:
            parts = small_parts()
            packed = _pack(parts)
            ar_sends, ar_recvs, ar_slots = _all_reduce_start(packed)
            g_in = _wgrad("wgrad_in", s["h"], dproj, IN_COLS // N_DEV, dep=ar_slots)
            ar_sends, ar_recvs, total = _all_reduce_sum(packed, ar_slots, ar_sends, ar_recvs, _hbm(g_in[1]))
            send_grads(l, dict(w_in=g_in), total)
    grad_x = dx.reshape(1, T, D)

    me_block = _flat(*_coords()).astype(jnp.int32).reshape(1)
    wmv = dict(w_in=(w_in, m_w_in, v_w_in), w_out=(w_out, m_w_out, v_w_out), w_ff1=(w_ff1, m_w_ff1, v_w_ff1),
               w_ff2=(w_ff2, m_w_ff2, v_w_ff2))
    res = {}

    def update(tag, ks, after):
        recv = _scatter_wait("scatter_wait_" + tag, [g for k in ks for g in big16[k]], [land[k] for k in ks],
                             [q for k in ks for q in ssend[k]], [q for k in ks for q in srecv[k]], after)
        for k, rk in zip(ks, recv):
            res[k] = _adam_sharded("adam_" + k, me_block, big32[k], rk, *wmv[k])

    shapes = [p.shape for p in parts]
    update("early", ["w_ff2", "w_ff1", "w_out"], big16["w_in"][0])
    packed_sum = _all_reduce_wait(total, ar_sends, ar_recvs, [res[k][0] for k in ("w_ff2", "w_ff1", "w_out")])
    update("late", ["w_in"], packed_sum)
    summed = _unpack(packed_sum, shapes)
    loss = summed[-1][0]
    me = _flat(*_coords())
    g_conv_w = lax.dynamic_slice_in_dim(summed[-2], me * (D_B // N_DEV), D_B // N_DEV, axis=2)
    g_small = summed[:-2] + [g_conv_w]
    keys = SMALL + ["conv_w"]
    rep["conv_w"] = (conv_w, m_conv_w, v_conv_w)
    def two_d(a):
        return a.reshape(1, -1) if a.ndim == 1 else a

    d_s, nm_s, nv_s = _adam_small([two_d(g) for g in g_small], *[[two_d(rep[k][i]) for k in keys] for i in range(3)])
    for i, k in enumerate(keys):
        shape = rep[k][0].shape
        res[k] = (g_small[i], d_s[i].reshape(shape), nm_s[i].reshape(shape), nv_s[i].reshape(shape))

    order = ["norm1_g", "w_in", "sgu_ln_g", "sgu_ln_b", "sgu_w", "sgu_b", "conv_w", "conv_b", "conv_ln_g",
             "conv_ln_b", "w_out", "norm2_g", "w_ff1", "w_ff2", "final_g"]
    return (loss, grad_x, *[res[k][0] for k in order], *[res[k][1] for k in order],
            *[res[k][2] for k in order], *[res[k][3] for k in order])
```

```python
import functools

import jax
import jax.numpy as jnp
from jax import lax
from jax.experimental import pallas as pl
from jax.experimental.pallas import tpu as pltpu

F32 = jnp.float32
BF16 = jnp.bfloat16

N_DEV = 8
DEPTH = 2
T = 4096
D = 1024
D_A = 512
D_B = 512
CHUNK = 128
H_A = 4
H_B = 4
CONV_W = 31
HALO = 32
D_FF = 4096
IN_COLS = 2048
EPS = 1e-6

ADAM_LR = 0.001
ADAM_B1 = 0.9
ADAM_B2 = 0.999
ADAM_EPS = 1e-08
ADAM_WD = 0.01
ADAM_STEP = 10

TM = 512
TM_FWD = 1024
TT_WGRAD = 4096
TM_SGU = 512
RB = 64
CONV_CHAINS = 2
LANES = 128
MIB = 1024 * 1024
SCOPED_VMEM_MIB = 60

SQRT_HALF = 0.7071067811865476
INV_SQRT_2PI = 0.3989422804014327

MESH_ID = pl.DeviceIdType.MESH
HBM = pl.BlockSpec(memory_space=pltpu.HBM)
SEM = pl.BlockSpec(memory_space=pltpu.SEMAPHORE)
EFFECT = pltpu.SideEffectType.DATAFLOW_SIDE_EFFECTING


def _cparams(vmem_mib, sem=("arbitrary",)):
    assert vmem_mib <= SCOPED_VMEM_MIB
    return pltpu.CompilerParams(dimension_semantics=sem, vmem_limit_bytes=SCOPED_VMEM_MIB * MIB)


def _full(shape):
    return pl.BlockSpec(shape, lambda *_: (0,) * len(shape))


def _rows(tm, cols):
    return pl.BlockSpec((tm, cols), lambda i: (i, 0))


def _gelu(x):
    cdf = 0.5 * (1.0 + lax.erf(x * SQRT_HALF))
    return x * cdf, cdf


def _gelu_grad(x, cdf):
    return cdf + x * (INV_SQRT_2PI * jnp.exp(-0.5 * x * x))


def _sigmoid(x):
    return 1.0 / (1.0 + jnp.exp(-x))


def _ln(x):
    mu = jnp.mean(x, axis=-1, keepdims=True)
    xc = x - mu
    rstd = lax.rsqrt(jnp.mean(xc * xc, axis=-1, keepdims=True) + EPS)
    return xc * rstd, rstd


def _ln_bwd(dyh, xhat, rstd):
    return rstd * (dyh - jnp.mean(dyh, axis=-1, keepdims=True) - xhat * jnp.mean(dyh * xhat, axis=-1, keepdims=True))


def _rms(x):
    return lax.rsqrt(jnp.mean(x * x, axis=-1, keepdims=True) + EPS)


def _rms_bwd(dh, x, r, g):
    n = x * r
    dn = dh * g
    dx = r * (dn - n * jnp.mean(dn * n, axis=-1, keepdims=True))
    return dx, jnp.sum(dh * n, axis=0, keepdims=True)


def _dot(a, b):
    return jnp.dot(a, b, preferred_element_type=F32)


def _dot_nt(a, b):
    return lax.dot_general(a, b, (((1,), (1,)), ((), ())), preferred_element_type=F32)


def _dot_tn(a, b):
    return lax.dot_general(a, b, (((0,), (0,)), ((), ())), preferred_element_type=F32)


def _tril_mask():
    r = lax.broadcasted_iota(jnp.int32, (CHUNK, CHUNK), 0)
    c = lax.broadcasted_iota(jnp.int32, (CHUNK, CHUNK), 1)
    return r >= c


def _fwd_in(x, g1, wg):
    bn = wg.shape[2]

    def body(x_ref, g_ref, w_ref, h_ref, p_ref):
        xv = x_ref[...]
        h = (xv * _rms(xv) * g_ref[...]).astype(BF16)
        h_ref[...] = h
        for j in range(N_DEV):
            p_ref[:, j * bn:(j + 1) * bn] = _dot(h, w_ref[j])

    return pl.pallas_call(
        body, name="fwd_in", grid=(T // TM_FWD,),
        in_specs=[_rows(TM_FWD, D), _full((1, D)), _full(wg.shape)],
        out_specs=[_rows(TM_FWD, D), _rows(TM_FWD, IN_COLS)],
        out_shape=[jax.ShapeDtypeStruct((T, D), BF16), jax.ShapeDtypeStruct((T, IN_COLS), F32)],
        compiler_params=_cparams(32),
    )(*map(_hbm, (x, g1, wg)))


def _fill_shift_buffer(sh_ref, row0, value):
    for q in range(sh_ref.shape[1]):
        sh_ref[0, q, row0:row0 + value.shape[0], :] = value[:, q * LANES:(q + 1) * LANES]


def _build_shifts(sh_ref):
    rows = sh_ref.shape[2]
    for p in range(1, 8):
        for q in range(sh_ref.shape[1]):
            sh_ref[p, q, 0:rows - 8, :] = sh_ref[0, q, p:p + rows - 8, :]


def _shifted(sh_ref, q, base, off):
    start = base + (off - off % 8)
    if not isinstance(start, int):
        start = pl.multiple_of(start, 8)
    return sh_ref[off % 8, q, pl.ds(start, RB), :]


def _conv_taps(sh_ref, w_ref, q, base, first_tap_row, step):
    cols = slice(q * LANES, (q + 1) * LANES)
    acc = [jnp.zeros((RB, LANES), F32) for _ in range(CONV_CHAINS)]
    for k in range(CONV_W):
        term = _shifted(sh_ref, q, base, first_tap_row + step * k) * w_ref[k:k + 1, cols]
        acc[k % CONV_CHAINS] = acc[k % CONV_CHAINS] + term
    return functools.reduce(lambda a, b: a + b, acc)


def _mixer_fwd(proj, lg, lb, wm, bst, cw, cb, clg, clb, wout, x, g2):
    tm = TM_SGU
    hb = tm // HALO

    def body(p_ref, ph_ref, lg_ref, lb_ref, wm_ref, bs_ref, cw_ref, cb_ref, clg_ref, clb_ref, wo_ref, x_ref, g2_ref,
             mix_ref, c_ref, g_ref, x1_ref, h2_ref, gbuf):
        i = pl.program_id(0)
        u, _ = _gelu(p_ref[:, 0:D_A])
        vg, _ = _gelu(p_ref[:, D_A:2 * D_A])
        xhat, _ = _ln(vg)
        v = (xhat * lg_ref[...] + lb_ref[...]).astype(BF16)
        mask = _tril_mask()
        for h in range(H_A):
            hc = slice(h * CHUNK, (h + 1) * CHUNK)
            wmh = jnp.where(mask, wm_ref[h], 0.0).astype(BF16)
            for c in range(tm // CHUNK):
                rc = slice(c * CHUNK, (c + 1) * CHUNK)
                mixed = _dot(wmh, v[rc, hc]) + bs_ref[:, h:h + 1]
                mix_ref[rc, hc] = (u[rc, hc] * mixed).astype(BF16)
        x1_ref[...] = x_ref[...] + _dot(mix_ref[:, 0:D_A], wo_ref[0:D_A, :])

        g = p_ref[:, 2 * D_A:2 * D_A + D_B] * _sigmoid(p_ref[:, 2 * D_A + D_B:IN_COLS])
        g_ref[...] = g
        gh = ph_ref[:, 0:D_B] * _sigmoid(ph_ref[:, D_B:2 * D_B])
        _fill_shift_buffer(gbuf, 0, jnp.where(i > 0, gh, 0.0))
        _fill_shift_buffer(gbuf, HALO, g)
        _build_shifts(gbuf)
        for q in range(H_B):
            cols = slice(q * LANES, (q + 1) * LANES)
            for rb in range(tm // RB):
                acc = _conv_taps(gbuf, cw_ref, q, rb * RB, HALO - (CONV_W - 1), 1)
                c_ref[rb * RB:(rb + 1) * RB, cols] = acc + cb_ref[:, cols]
        for q in range(H_B):
            cols = slice(q * LANES, (q + 1) * LANES)
            chat, _ = _ln(c_ref[:, cols])
            z = chat * clg_ref[:, cols] + clb_ref[:, cols]
            mix_ref[:, D_A + q * LANES:D_A + (q + 1) * LANES] = (z * _sigmoid(z)).astype(BF16)

        x1 = x1_ref[...] + _dot(mix_ref[:, D_A:D], wo_ref[D_A:D, :])
        x1_ref[...] = x1
        h2_ref[...] = (x1 * _rms(x1) * g2_ref[...]).astype(BF16)

    vec = _full((1, D_A))
    return pl.pallas_call(
        body, name="mixer_fwd", grid=(T // tm,),
        in_specs=[_rows(tm, IN_COLS),
                  pl.BlockSpec((HALO, 2 * D_B), lambda i: (jnp.maximum(i * hb - 1, 0), 1)),
                  vec, vec, _full((H_A, CHUNK, CHUNK)), _full((CHUNK, H_A)),
                  _full((CONV_W, D_B)), vec, vec, vec,
                  pl.BlockSpec((D, D), lambda i: (0, 0), pipeline_mode=pl.Buffered(1)), _rows(tm, D), _full((1, D))],
        out_specs=[_rows(tm, D), _rows(tm, D_B), _rows(tm, D_B), _rows(tm, D), _rows(tm, D)],
        out_shape=[jax.ShapeDtypeStruct((T, D), BF16), jax.ShapeDtypeStruct((T, D_B), F32),
                   jax.ShapeDtypeStruct((T, D_B), F32), jax.ShapeDtypeStruct((T, D), F32),
                   jax.ShapeDtypeStruct((T, D), BF16)],
        scratch_shapes=[pltpu.VMEM((8, H_B, HALO + tm, LANES), F32)],
        compiler_params=_cparams(40),
    )(*map(_hbm, (proj, proj, lg, lb, wm, bst, cw, cb, clg, clb, wout, x, g2)))


def _fwd_ff1(h2, wg):
    bn = wg.shape[2]

    def body(h_ref, w_ref, r_ref):
        h = h_ref[...]
        for j in range(N_DEV):
            r_ref[:, j * bn:(j + 1) * bn] = jnp.maximum(_dot(h, w_ref[j]), 0.0).astype(BF16)

    return pl.pallas_call(
        body, name="fwd_ff1", grid=(T // TM_FWD,),
        in_specs=[_rows(TM_FWD, D), _full(wg.shape)],
        out_specs=_rows(TM_FWD, D_FF),
        out_shape=jax.ShapeDtypeStruct((T, D_FF), BF16),
        compiler_params=_cparams(48),
    )(h2, wg)


def _fwd_ff2(r, w2, x1):
    def body(r_ref, w_ref, x_ref, o_ref):
        rv = r_ref[...]
        o_ref[...] = x_ref[...] + _dot(rv * rv, w_ref[...])

    return pl.pallas_call(
        body, name="fwd_ff2", grid=(T // TM_FWD,),
        in_specs=[_rows(TM_FWD, D_FF), _full((D_FF, D)), _rows(TM_FWD, D)],
        out_specs=_rows(TM_FWD, D),
        out_shape=jax.ShapeDtypeStruct((T, D), F32),
        compiler_params=_cparams(48),
    )(r, w2, x1)


def _fwd_ff2_loss(r, w2, x1, gf, tgt):
    def body(r_ref, w_ref, x_ref, g_ref, t_ref, dx_ref, dxb_ref, loss_ref, dg_ref):
        i = pl.program_id(0)

        @pl.when(i == 0)
        def _():
            loss_ref[...] = jnp.zeros(loss_ref.shape, F32)
            dg_ref[...] = jnp.zeros(dg_ref.shape, F32)

        rv = r_ref[...]
        xv = x_ref[...] + _dot(rv * rv, w_ref[...])
        rn = _rms(xv)
        diff = xv * rn * g_ref[...] - t_ref[...]
        loss_ref[...] += 0.5 * jnp.sum(jnp.mean(diff * diff, axis=-1, keepdims=True), axis=0, keepdims=True)
        dx, dg = _rms_bwd(diff * (1.0 / D), xv, rn, g_ref[...])
        dx_ref[...] = dx
        dxb_ref[...] = dx.astype(BF16)
        dg_ref[...] += dg

    return pl.pallas_call(
        body, name="fwd_ff2_loss", grid=(T // TM,),
        in_specs=[_rows(TM, D_FF), pl.BlockSpec((D_FF, D), lambda i: (0, 0), pipeline_mode=pl.Buffered(1)),
                  _rows(TM, D), _full((1, D)), _rows(TM, D)],
        out_specs=[_rows(TM, D), _rows(TM, D), _full((8, LANES)), _full((1, D))],
        out_shape=[jax.ShapeDtypeStruct((T, D), F32), jax.ShapeDtypeStruct((T, D), BF16),
                   jax.ShapeDtypeStruct((8, LANES), F32), jax.ShapeDtypeStruct((1, D), F32)],
        compiler_params=_cparams(40),
    )(*map(_hbm, (r, w2, x1, gf, tgt)))


def _bwd_mlp(dxb, dres, w2, r, wg1, x1, g2, dep):
    bn = wg1.shape[2]

    def body(d_ref, dres_ref, w2_ref, r_ref, w1_ref, x_ref, g_ref, dep_ref, df1_ref, dx_ref, dxb_ref, dg_ref):
        i = pl.program_id(0)

        @pl.when(i == 0)
        def _():
            dg_ref[...] = jnp.zeros(dg_ref.shape, F32)

        d = d_ref[...]
        dh = jnp.zeros((TM, D), F32)
        for j in range(N_DEV):
            cols = slice(j * bn, (j + 1) * bn)
            df1 = (2.0 * r_ref[:, cols].astype(F32) * _dot_nt(d, w2_ref[cols, :])).astype(BF16)
            df1_ref[:, cols] = df1
            dh = dh + _dot_nt(df1, w1_ref[j])
        xv = x_ref[...]
        dxn, dg = _rms_bwd(dh, xv, _rms(xv), g_ref[...])
        dx = dres_ref[...] + dxn
        dx_ref[...] = dx
        dxb_ref[...] = dx.astype(BF16)
        dg_ref[...] += dg

    once = dict(pipeline_mode=pl.Buffered(1))
    return pl.pallas_call(
        body, name="bwd_mlp", grid=(T // TM,),
        in_specs=[_rows(TM, D), _rows(TM, D), pl.BlockSpec((D_FF, D), lambda i: (0, 0), **once), _rows(TM, D_FF),
                  pl.BlockSpec(wg1.shape, lambda i: (0, 0, 0), **once), _rows(TM, D), _full((1, D)), HBM],
        out_specs=[_rows(TM, D_FF), _rows(TM, D), _rows(TM, D), _full((1, D))],
        out_shape=[jax.ShapeDtypeStruct((T, D_FF), BF16), jax.ShapeDtypeStruct((T, D), F32),
                   jax.ShapeDtypeStruct((T, D), BF16), jax.ShapeDtypeStruct((1, D), F32)],
        compiler_params=_cparams(56),
    )(*map(_hbm, (dxb, dres, w2, r, wg1, x1, g2, dep)))


def _mixer_bwd_a(dxb, wout, proj, c, lg, lb, wm, bst, clg, clb, dep):
    tm = TM_SGU
    n_tiles = T // tm

    def body(dx_ref, wo_ref, p_ref, c_ref, lg_ref, lb_ref, wm_ref, bs_ref, clg_ref, clb_ref, dep_ref,
             dpa_ref, dc_ref, dlg_ref, dlb_ref, dwm_ref, dbs_ref, dcb_ref, dclg_ref, dclb_ref,
             dv_buf, db_acc):
        i = pl.program_id(0)

        @pl.when(i == 0)
        def _():
            for ref in (dlg_ref, dlb_ref, dwm_ref, dbs_ref, dcb_ref, dclg_ref, dclb_ref, db_acc):
                ref[...] = jnp.zeros(ref.shape, F32)

        dmix = _dot_nt(dx_ref[...], wo_ref[...])
        ua = p_ref[:, 0:D_A]
        va = p_ref[:, D_A:2 * D_A]
        u, cdf_u = _gelu(ua)
        vg, cdf_v = _gelu(va)
        xhat, rstd = _ln(vg)
        v = (xhat * lg_ref[...] + lb_ref[...]).astype(BF16)
        mask = _tril_mask()
        for h in range(H_A):
            hc = slice(h * CHUNK, (h + 1) * CHUNK)
            wmh = jnp.where(mask, wm_ref[h], 0.0).astype(BF16)
            for cidx in range(tm // CHUNK):
                rc = slice(cidx * CHUNK, (cidx + 1) * CHUNK)
                vb = v[rc, hc]
                mixed = _dot(wmh, vb) + bs_ref[:, h:h + 1]
                da = dmix[rc, hc]
                dpa_ref[rc, hc] = (da * mixed * _gelu_grad(ua[rc, hc], cdf_u[rc, hc])).astype(BF16)
                dmixed = da * u[rc, hc]
                dmb = dmixed.astype(BF16)
                dv_buf[rc, hc] = _dot_tn(wmh, dmb)
                dwm_ref[h] += _dot_nt(dmb, vb)
                db_acc[:, hc] += dmixed
        dv = dv_buf[...]
        dlb_ref[...] += jnp.sum(dv, axis=0, keepdims=True)
        dlg_ref[...] += jnp.sum(dv * xhat, axis=0, keepdims=True)
        dvg = _ln_bwd(dv * lg_ref[...], xhat, rstd)
        dpa_ref[:, D_A:2 * D_A] = (dvg * _gelu_grad(va, cdf_v)).astype(BF16)

        for q in range(H_B):
            cols = slice(q * LANES, (q + 1) * LANES)
            chat, crstd = _ln(c_ref[:, cols])
            z = chat * clg_ref[:, cols] + clb_ref[:, cols]
            sg = _sigmoid(z)
            dz = dmix[:, D_A + q * LANES:D_A + (q + 1) * LANES] * (sg * (1.0 + z * (1.0 - sg)))
            dclb_ref[:, cols] += jnp.sum(dz, axis=0, keepdims=True)
            dclg_ref[:, cols] += jnp.sum(dz * chat, axis=0, keepdims=True)
            dc = _ln_bwd(dz * clg_ref[:, cols], chat, crstd)
            dc_ref[:, cols] = dc
            dcb_ref[:, cols] += jnp.sum(dc, axis=0, keepdims=True)

        @pl.when(i == n_tiles - 1)
        def _():
            for h in range(H_A):
                hc = slice(h * CHUNK, (h + 1) * CHUNK)
                dwm_ref[h] = jnp.where(mask, dwm_ref[h], 0.0)
                dbs_ref[h:h + 1, :] = jnp.sum(db_acc[:, hc].T, axis=0, keepdims=True)

    vec = _full((1, D_A))
    vshape = jax.ShapeDtypeStruct((1, D_A), F32)
    return pl.pallas_call(
        body, name="mixer_bwd_a", grid=(n_tiles,),
        in_specs=[_rows(tm, D), _full((D, D)), _rows(tm, IN_COLS), _rows(tm, D_B), vec, vec,
                  _full((H_A, CHUNK, CHUNK)), _full((CHUNK, H_A)), vec, vec, HBM],
        out_specs=[_rows(tm, 2 * D_A), _rows(tm, D_B), vec, vec, _full((H_A, CHUNK, CHUNK)),
                   _full((H_A, CHUNK)), vec, vec, vec],
        out_shape=[jax.ShapeDtypeStruct((T, 2 * D_A), BF16), jax.ShapeDtypeStruct((T, D_B), F32), vshape, vshape,
                   jax.ShapeDtypeStruct((H_A, CHUNK, CHUNK), F32), jax.ShapeDtypeStruct((H_A, CHUNK), F32),
                   vshape, vshape, vshape],
        scratch_shapes=[pltpu.VMEM((tm, D_A), F32), pltpu.VMEM((CHUNK, D_A), F32)],
        compiler_params=_cparams(32),
    )(*map(_hbm, (dxb, wout, proj, c, lg, lb, wm, bst, clg, clb, dep)))


def _mixer_bwd_b(dc, g, proj, cw, dpa, wg_in, x, g1, dres):
    tm = TM_SGU
    n_tiles = T // tm
    hb = tm // HALO
    bn = wg_in.shape[2]

    def body(dc_ref, dch_ref, g_ref, p_ref, cw_ref, dpa_ref, win_ref, x_ref, g1_ref, dres_ref,
             dp_ref, dcw_ref, dx_ref, dxb_ref, dg1_ref, dcbuf, dwacc):
        i = pl.program_id(0)

        @pl.when(i == 0)
        def _():
            dwacc[...] = jnp.zeros(dwacc.shape, F32)
            dg1_ref[...] = jnp.zeros(dg1_ref.shape, F32)

        _fill_shift_buffer(dcbuf, 0, dc_ref[...])
        _fill_shift_buffer(dcbuf, tm, jnp.where(i < n_tiles - 1, dch_ref[...], 0.0))
        _build_shifts(dcbuf)
        dp_ref[:, 0:2 * D_A] = dpa_ref[...]
        for q in range(H_B):
            cols = slice(q * LANES, (q + 1) * LANES)

            def row_block(rb, carry, q=q, cols=cols):
                base = pl.multiple_of(rb * RB, RB)
                rows = pl.ds(base, RB)
                gv = g_ref[rows, cols]
                acc = [jnp.zeros((RB, LANES), F32) for _ in range(CONV_CHAINS)]
                parts = []
                for k in range(CONV_W):
                    xk = _shifted(dcbuf, q, base, CONV_W - 1 - k)
                    acc[k % CONV_CHAINS] = acc[k % CONV_CHAINS] + xk * cw_ref[k:k + 1, cols]
                    parts.append(jnp.sum((gv * xk).reshape(RB // 8, 8, LANES), axis=0))
                dg = functools.reduce(lambda a, b: a + b, acc)
                val = p_ref[rows, cols]
                sg = _sigmoid(p_ref[rows, D_B + q * LANES:D_B + (q + 1) * LANES])
                dp_ref[rows, 2 * D_A + q * LANES:2 * D_A + (q + 1) * LANES] = (dg * sg).astype(BF16)
                dp_ref[rows, 2 * D_A + D_B + q * LANES:2 * D_A + D_B + (q + 1) * LANES] = (
                    dg * val * sg * (1.0 - sg)).astype(BF16)
                for k in range(CONV_W):
                    dwacc[k * 8:(k + 1) * 8, cols] += parts[k]
                return carry

            lax.fori_loop(0, tm // RB, row_block, 0)

        dh = jnp.zeros((tm, D), F32)
        for j in range(N_DEV):
            dh = dh + _dot_nt(dp_ref[:, j * bn:(j + 1) * bn], win_ref[j])
        xv = x_ref[...]
        dxn, dg = _rms_bwd(dh, xv, _rms(xv), g1_ref[...])
        dx = dres_ref[...] + dxn
        dx_ref[...] = dx
        dxb_ref[...] = dx.astype(BF16)
        dg1_ref[...] += dg

        @pl.when(i == n_tiles - 1)
        def _():
            for k in range(CONV_W):
                dcw_ref[k:k + 1, :] = jnp.sum(dwacc[k * 8:(k + 1) * 8, :], axis=0, keepdims=True)

    return pl.pallas_call(
        body, name="mixer_bwd_b", grid=(n_tiles,),
        in_specs=[_rows(tm, D_B),
                  pl.BlockSpec((HALO, D_B), lambda i: (jnp.minimum((i + 1) * hb, T // HALO - 1), 0)),
                  _rows(tm, D_B),
                  pl.BlockSpec((tm, 2 * D_B), lambda i: (i, 1)),
                  _full((CONV_W, D_B)), _rows(tm, 2 * D_A),
                  pl.BlockSpec(wg_in.shape, lambda i: (0, 0, 0), pipeline_mode=pl.Buffered(1)),
                  _rows(tm, D), _full((1, D)), _rows(tm, D)],
        out_specs=[_rows(tm, IN_COLS), _full((CONV_W, D_B)), _rows(tm, D), _rows(tm, D), _full((1, D))],
        out_shape=[jax.ShapeDtypeStruct((T, IN_COLS), BF16), jax.ShapeDtypeStruct((CONV_W, D_B), F32),
                   jax.ShapeDtypeStruct((T, D), F32), jax.ShapeDtypeStruct((T, D), BF16),
                   jax.ShapeDtypeStruct((1, D), F32)],
        scratch_shapes=[pltpu.VMEM((8, H_B, tm + HALO, LANES), F32), pltpu.VMEM((CONV_W * 8, D_B), F32)],
        compiler_params=_cparams(40),
    )(*map(_hbm, (dc, dc, g, proj, cw, dpa, wg_in, x, g1, dres)))


def _wgrad(name, a, g, bn, square_a=False, dep=None):
    deps = [] if dep is None else [dep]
    k = a.shape[1]
    n = g.shape[1]
    tk = min(k, 1024)
    tn = min(n, max(bn, 1024))
    nsub = tn // bn
    tt = TT_WGRAD
    nt = T // tt

    def body(a_ref, g_ref, *refs):
        o_ref, ob_ref = refs[len(deps):]
        t = pl.program_id(2)

        @pl.when(t == 0)
        def _():
            o_ref[...] = jnp.zeros(o_ref.shape, F32)

        av = a_ref[...]
        if square_a:
            av = av * av
        for s in range(nsub):
            o_ref[s] += _dot_tn(av, g_ref[:, s * bn:(s + 1) * bn])

        @pl.when(t == nt - 1)
        def _():
            ob_ref[...] = o_ref[...].astype(BF16)

    ospec = pl.BlockSpec((nsub, tk, bn), lambda ki, ni, ti: (ni, ki, 0))
    return pl.pallas_call(
        body, name=name, grid=(k // tk, n // tn, nt),
        in_specs=[pl.BlockSpec((tt, tk), lambda ki, ni, ti: (ti, ki)),
                  pl.BlockSpec((tt, tn), lambda ki, ni, ti: (ti, ni))] + [HBM] * len(deps),
        out_specs=[ospec, ospec],
        out_shape=[jax.ShapeDtypeStruct((n // bn, k, bn), F32), jax.ShapeDtypeStruct((n // bn, k, bn), BF16)],
        compiler_params=_cparams(40, ("arbitrary", "arbitrary", "arbitrary")),
    )(a, g, *map(_hbm, deps))


def _coords():
    return lax.axis_index("x"), lax.axis_index("y"), lax.axis_index("c")


def _flat(x, y, c):
    return 4 * x + 2 * y + c


def _peer(k):
    x, y, c = _coords()
    return (x ^ ((k >> 2) & 1), y ^ ((k >> 1) & 1), c ^ (k & 1))


def _hbm(a):
    return pltpu.with_memory_space_constraint(a, pltpu.HBM)


def _hbm_like(a):
    return pltpu.HBM(a.shape, a.dtype)


def _peer_sems():
    return pltpu.SemaphoreType.DMA((N_DEV - 1,))


SIBLING = 1
CHIP_PEERS = (2, 4, 6)
FIRST_PEERS = (SIBLING,) + CHIP_PEERS


def _gather_start(shards):
    n = len(shards)
    shapes = [(s.shape if l is None else s.shape[1:]) for s, l, _ in shards]

    def body(*refs):
        ins, send, recv, lnd = refs[:n], refs[n:2 * n], refs[2 * n:3 * n], refs[3 * n:4 * n]
        stage_in, stage_out = refs[4 * n:5 * n], refs[5 * n:6 * n]
        in_sems, out_sems = refs[6 * n], refs[6 * n + 1]
        me = _flat(*_coords())
        srcs = [ins[a] if shards[a][1] is None else ins[a].at[shards[a][1]] for a in range(n)]
        loads = [pltpu.make_async_copy(srcs[a], stage_in[a], in_sems.at[a]) for a in range(n)]
        stores = [pltpu.make_async_copy(stage_out[a], lnd[a].at[me], out_sems.at[a]) for a in range(n)]

        def send_block(a):
            stores[a].wait()
            for j, k in enumerate(FIRST_PEERS):
                pltpu.make_async_remote_copy(
                    src_ref=lnd[a].at[me], dst_ref=lnd[a].at[me], send_sem=send[a].at[j],
                    recv_sem=recv[a].at[j], device_id=_peer(k), device_id_type=MESH_ID).start()

        for cp in loads:
            cp.start()
        for a in range(n):
            loads[a].wait()
            stage_out[a][...] = stage_in[a][...].astype(stage_out[a].dtype)
            stores[a].start()
            if a > 0:
                send_block(a - 1)
        send_block(n - 1)

    sems = pltpu.SemaphoreType.DMA((len(FIRST_PEERS),))
    outs = pl.pallas_call(
        body, name="gather_start",
        out_shape=(*[sems] * (2 * n), *[pltpu.HBM((N_DEV,) + shapes[a], shards[a][2]) for a in range(n)]),
        in_specs=[HBM] * n, out_specs=(*[SEM] * (2 * n), *[HBM] * n),
        scratch_shapes=[*[pltpu.VMEM(shapes[a], shards[a][0].dtype) for a in range(n)],
                        *[pltpu.VMEM(shapes[a], shards[a][2]) for a in range(n)],
                        pltpu.SemaphoreType.DMA((n,)), pltpu.SemaphoreType.DMA((n,))],
        compiler_params=pltpu.CompilerParams(has_side_effects=EFFECT, vmem_limit_bytes=40 * MIB),
    )(*[_hbm(s) for s, _, _ in shards])
    return outs[:n], outs[n:2 * n], outs[2 * n:]


def _gather_pass_on(name, lands, sends, recvs, after):
    n = len(lands)

    def body(*refs):
        lnd, send, recv = refs[:n], refs[n:2 * n], refs[2 * n:3 * n]
        send2, recv2 = refs[3 * n + 1:4 * n + 1], refs[4 * n + 1:5 * n + 1]
        sibling = _peer(SIBLING)
        for a in range(n):
            for j, k in enumerate(CHIP_PEERS):
                cp = pltpu.make_async_remote_copy(
                    src_ref=lnd[a].at[0], dst_ref=lnd[a].at[0], send_sem=send[a].at[1 + j],
                    recv_sem=recv[a].at[1 + j], device_id=_peer(k), device_id_type=MESH_ID)
                cp.wait_send()
                cp.wait_recv()
                blk = _flat(*_peer(k))
                pltpu.make_async_remote_copy(
                    src_ref=lnd[a].at[blk], dst_ref=lnd[a].at[blk], send_sem=send2[a].at[j],
                    recv_sem=recv2[a].at[j], device_id=sibling, device_id_type=MESH_ID).start()

    sems = pltpu.SemaphoreType.DMA((len(CHIP_PEERS),))
    outs = pl.pallas_call(
        body, name=name, out_shape=(*[sems] * (2 * n), *[_hbm_like(l) for l in lands]),
        in_specs=(*[HBM] * n, *[SEM] * (2 * n), HBM), out_specs=(*[SEM] * (2 * n), *[HBM] * n),
        input_output_aliases={i: 2 * n + i for i in range(n)},
        compiler_params=pltpu.CompilerParams(has_side_effects=EFFECT),
    )(*lands, *sends, *recvs, after)
    return outs[:n], outs[n:2 * n], outs[2 * n:]


def _gather_wait(name, lands, sends, recvs, sends2, recvs2, after):
    n = len(lands)

    def body(*refs):
        lnd, send, recv = refs[:n], refs[n:2 * n], refs[2 * n:3 * n]
        send2, recv2 = refs[3 * n:4 * n], refs[4 * n:5 * n]
        sibling = _peer(SIBLING)
        for a in range(n):
            own = pltpu.make_async_remote_copy(
                src_ref=lnd[a].at[0], dst_ref=lnd[a].at[0], send_sem=send[a].at[0], recv_sem=recv[a].at[0],
                device_id=sibling, device_id_type=MESH_ID)
            own.wait_send()
            own.wait_recv()
            for j in range(len(CHIP_PEERS)):
                cp = pltpu.make_async_remote_copy(
                    src_ref=lnd[a].at[0], dst_ref=lnd[a].at[0], send_sem=send2[a].at[j], recv_sem=recv2[a].at[j],
                    device_id=sibling, device_id_type=MESH_ID)
                cp.wait_send()
                cp.wait_recv()

    return pl.pallas_call(
        body, name=name, out_shape=[_hbm_like(l) for l in lands],
        in_specs=(*[HBM] * n, *[SEM] * (4 * n), HBM), out_specs=[HBM] * n,
        input_output_aliases={i: i for i in range(n)},
        compiler_params=pltpu.CompilerParams(has_side_effects=EFFECT),
    )(*lands, *sends, *recvs, *sends2, *recvs2, after)


def _scatter_start(name, grads, lands, layer, after=None):
    n = len(grads)
    afters = [] if after is None else [after]

    def body(*refs):
        g, lnd = refs[:n], refs[n:2 * n]
        send, recv = refs[2 * n + len(afters):3 * n + len(afters)], refs[3 * n + len(afters):4 * n + len(afters)]
        for a in range(n):
            for k in range(1, N_DEV):
                to = _peer(k)
                pltpu.make_async_remote_copy(
                    src_ref=g[a].at[_flat(*to)], dst_ref=lnd[a].at[layer, k - 1], send_sem=send[a].at[k - 1],
                    recv_sem=recv[a].at[k - 1], device_id=to, device_id_type=MESH_ID).start()

    outs = pl.pallas_call(
        body, name=name,
        out_shape=(*[_peer_sems()] * (2 * n), *[_hbm_like(g) for g in grads], *[_hbm_like(l) for l in lands]),
        in_specs=[HBM] * (2 * n + len(afters)), out_specs=(*[SEM] * (2 * n), *[HBM] * (2 * n)),
        input_output_aliases={i: 2 * n + i for i in range(2 * n)},
        compiler_params=pltpu.CompilerParams(has_side_effects=EFFECT),
    )(*[_hbm(g) for g in grads], *[_hbm(l) for l in lands], *afters)
    return outs[:n], outs[n:2 * n], outs[2 * n:3 * n], outs[3 * n:]


def _scatter_wait(name, grads, lands, sends, recvs, after):
    n, nw = len(grads), len(lands)

    def body(*refs):
        g, lnd = refs[:n], refs[n:n + nw]
        send, recv = refs[n + nw:2 * n + nw], refs[2 * n + nw:3 * n + nw]
        for a in range(n):
            for k in range(1, N_DEV):
                cp = pltpu.make_async_remote_copy(
                    src_ref=g[a].at[0], dst_ref=lnd[a // DEPTH].at[a % DEPTH, 0],
                    send_sem=send[a].at[k - 1], recv_sem=recv[a].at[k - 1],
                    device_id=_peer(k), device_id_type=MESH_ID)
                cp.wait_send()
                cp.wait_recv()

    outs = pl.pallas_call(
        body, name=name,
        out_shape=(*[_hbm_like(g) for g in grads], *[_hbm_like(l) for l in lands]),
        in_specs=(*[HBM] * (n + nw), *[SEM] * (2 * n), HBM), out_specs=[HBM] * (n + nw),
        input_output_aliases={i: i for i in range(n + nw)},
        compiler_params=pltpu.CompilerParams(has_side_effects=EFFECT),
    )(*grads, *lands, *sends, *recvs, after)
    return outs[n:]


def _rows_block(ref, d, br):
    return ref.at[pl.ds(pl.multiple_of(d * br, 8), br), :]


def _all_reduce_start(part):
    rows = part.shape[0]
    br = rows // N_DEV
    assert br * N_DEV == rows and br % 8 == 0

    def body(p_ref, _slots_in, send, recv, slots):
        me = _flat(*_coords())
        for k in range(1, N_DEV):
            to = _peer(k)
            pltpu.make_async_remote_copy(
                src_ref=_rows_block(p_ref, _flat(*to), br), dst_ref=slots.at[me], send_sem=send.at[k - 1],
                recv_sem=recv.at[k - 1], device_id=to, device_id_type=MESH_ID).start()

    slots = lax.empty((N_DEV, br, LANES), F32)
    return pl.pallas_call(
        body, name="all_reduce_start",
        out_shape=(_peer_sems(), _peer_sems(), _hbm_like(slots)),
        in_specs=[HBM, HBM], out_specs=(SEM, SEM, HBM), input_output_aliases={1: 2},
        compiler_params=pltpu.CompilerParams(has_side_effects=EFFECT),
    )(_hbm(part), _hbm(slots))


def _all_reduce_sum(part, slots, sends, recvs, after):
    rows = part.shape[0]
    br = rows // N_DEV

    def body(p_ref, slots, _total_in, send1, recv1, _after, send2, recv2, total, stage, acc_ref, sems):
        me = _flat(*_coords())
        for k in range(1, N_DEV):
            cp = pltpu.make_async_remote_copy(
                src_ref=_rows_block(p_ref, 0, br), dst_ref=slots.at[0], send_sem=send1.at[k - 1],
                recv_sem=recv1.at[k - 1], device_id=_peer(k), device_id_type=MESH_ID)
            cp.wait_send()
            cp.wait_recv()
        loads = [pltpu.make_async_copy(_rows_block(p_ref, me, br), stage.at[me], sems.at[0])]
        for k in range(1, N_DEV):
            d = _flat(*_peer(k))
            loads.append(pltpu.make_async_copy(slots.at[d], stage.at[d], sems.at[k]))
        for cp in loads:
            cp.start()
        for cp in loads:
            cp.wait()
        acc = stage[0]
        for d in range(1, N_DEV):
            acc = acc + stage[d]
        acc_ref[...] = acc
        store = pltpu.make_async_copy(acc_ref, _rows_block(total, me, br), sems.at[0])
        store.start()
        store.wait()
        for k in range(1, N_DEV):
            pltpu.make_async_remote_copy(
                src_ref=_rows_block(total, me, br), dst_ref=_rows_block(total, me, br), send_sem=send2.at[k - 1],
                recv_sem=recv2.at[k - 1], device_id=_peer(k), device_id_type=MESH_ID).start()

    total = lax.empty(part.shape, F32)
    return pl.pallas_call(
        body, name="all_reduce_sum",
        out_shape=(_peer_sems(), _peer_sems(), _hbm_like(total)),
        in_specs=[HBM, HBM, HBM, SEM, SEM, HBM], out_specs=(SEM, SEM, HBM), input_output_aliases={2: 2},
        scratch_shapes=[pltpu.VMEM((N_DEV, br, LANES), F32), pltpu.VMEM((br, LANES), F32),
                        pltpu.SemaphoreType.DMA((N_DEV,))],
        compiler_params=pltpu.CompilerParams(has_side_effects=EFFECT),
    )(_hbm(part), slots, _hbm(total), sends, recvs, after)


def _all_reduce_wait(total, sends, recvs, deps):
    br = total.shape[0] // N_DEV
    nd = len(deps)

    def body(total, send, recv, *_):
        for k in range(1, N_DEV):
            cp = pltpu.make_async_remote_copy(
                src_ref=_rows_block(total, 0, br), dst_ref=_rows_block(total, 0, br), send_sem=send.at[k - 1],
                recv_sem=recv.at[k - 1], device_id=_peer(k), device_id_type=MESH_ID)
            cp.wait_send()
            cp.wait_recv()

    return pl.pallas_call(
        body, name="all_reduce_wait", out_shape=_hbm_like(total),
        in_specs=[HBM, SEM, SEM] + [HBM] * nd, out_specs=HBM, input_output_aliases={0: 0},
        compiler_params=pltpu.CompilerParams(has_side_effects=EFFECT),
    )(total, sends, recvs, *map(_hbm, deps))


def _adam_math(w, g, m, v):
    m = ADAM_B1 * m + (1.0 - ADAM_B1) * g
    v = ADAM_B2 * v + (1.0 - ADAM_B2) * (g * g)
    m_hat = m / (1.0 - ADAM_B1 ** ADAM_STEP)
    v_hat = v / (1.0 - ADAM_B2 ** ADAM_STEP)
    delta = -ADAM_LR * (m_hat / (jnp.sqrt(v_hat) + ADAM_EPS) + ADAM_WD * w)
    return delta, m, v


def _adam_sharded(name, me, g32, recv, w, m, v):
    _, r, c = w.shape
    tr = min(r, 512)
    nblk = r // tr
    assert len(g32) == DEPTH == 2

    def body(me_ref, g0_ref, g1_ref, recv_ref, w_ref, m_ref, v_ref, g_ref, d_ref, nm_ref, nv_ref):
        g = jnp.where(pl.program_id(0) == 0, g0_ref[...], g1_ref[...])
        for k in range(N_DEV - 1):
            g = g + recv_ref[k].astype(F32)
        delta, nm, nv = _adam_math(w_ref[...], g, m_ref[...], v_ref[...])
        g_ref[...] = g
        d_ref[...] = delta
        nm_ref[...] = nm
        nv_ref[...] = nv

    blk = pl.BlockSpec((None, tr, c), lambda l, i, me_ref: (l, i, 0))
    own0 = pl.BlockSpec((None, tr, c), lambda l, i, me_ref: (me_ref[0], jnp.where(l == 0, i, nblk - 1), 0))
    own1 = pl.BlockSpec((None, tr, c), lambda l, i, me_ref: (me_ref[0], jnp.where(l == 1, i, 0), 0))
    shp = jax.ShapeDtypeStruct(w.shape, F32)
    return pl.pallas_call(
        body, name=name,
        grid_spec=pltpu.PrefetchScalarGridSpec(
            num_scalar_prefetch=1, grid=(DEPTH, nblk),
            in_specs=[own0, own1, pl.BlockSpec((None, N_DEV - 1, tr, c), lambda l, i, me_ref: (l, 0, i, 0)),
                      blk, blk, blk],
            out_specs=[blk] * 4),
        out_shape=[shp] * 4,
        compiler_params=_cparams(32, ("arbitrary", "arbitrary")),
    )(me, *[_hbm(a) for a in (*g32, recv, w, m, v)])


def _adam_small(gs, ws, ms, vs):
    n = len(gs)

    def body(*refs):
        g, w, m, v = refs[:n], refs[n:2 * n], refs[2 * n:3 * n], refs[3 * n:4 * n]
        d, nm, nv = refs[4 * n:5 * n], refs[5 * n:6 * n], refs[6 * n:7 * n]
        for a in range(n):
            delta, new_m, new_v = _adam_math(w[a][...], g[a][...], m[a][...], v[a][...])
            d[a][...] = delta
            nm[a][...] = new_m
            nv[a][...] = new_v

    specs = [_full(a.shape) for a in gs]
    shapes = [jax.ShapeDtypeStruct(a.shape, F32) for a in gs]
    outs = pl.pallas_call(
        body, name="adam_small", grid=(1,),
        in_specs=specs * 4, out_specs=specs * 3, out_shape=shapes * 3,
        compiler_params=_cparams(24),
    )(*gs, *ws, *ms, *vs)
    return outs[:n], outs[n:2 * n], outs[2 * n:]


SMALL = ["norm1_g", "sgu_ln_g", "sgu_ln_b", "sgu_w", "sgu_b", "conv_b", "conv_ln_g", "conv_ln_b", "norm2_g",
         "final_g"]


def _pack(arrays):
    flat = jnp.concatenate([a.reshape(-1) for a in arrays])
    pad = (-flat.shape[0]) % (N_DEV * 8 * LANES)
    return jnp.pad(flat, (0, pad)).reshape(-1, LANES)


def _unpack(packed, shapes):
    flat = packed.reshape(-1)
    out, off = [], 0
    for s in shapes:
        size = 1
        for d in s:
            size *= d
        out.append(flat[off:off + size].reshape(s))
        off += size
    return out


def kernel(x, norm1_g, w_in, sgu_ln_g, sgu_ln_b, sgu_w, sgu_b, conv_w, conv_b, conv_ln_g, conv_ln_b, w_out, norm2_g, w_ff1, w_ff2, final_g, loss_target, m_norm1_g, m_w_in, m_sgu_ln_g, m_sgu_ln_b, m_sgu_w, m_sgu_b, m_conv_w, m_conv_b, m_conv_ln_g, m_conv_ln_b, m_w_out, m_norm2_g, m_w_ff1, m_w_ff2, m_final_g, v_norm1_g, v_w_in, v_sgu_ln_g, v_sgu_ln_b, v_sgu_w, v_sgu_b, v_conv_w, v_conv_b, v_conv_ln_g, v_conv_ln_b, v_w_out, v_norm2_g, v_w_ff1, v_w_ff2, v_final_g):
    x2d = x.reshape(T, D)
    tgt = loss_target.reshape(T, D)
    cw_shard = conv_w.reshape(CONV_W, LANES)

    gnames, shards = [], []
    for l in range(DEPTH):
        for k, w in (("w_in", w_in), ("w_out", w_out), ("w_ff1", w_ff1), ("w_ff2", w_ff2)):
            gnames.append(f"{k}{l}")
            shards.append((w, l, BF16))
        if l == 0:
            gnames.insert(1, "conv_w")
            shards.insert(1, (cw_shard, None, F32))
    sends, recvs, lands = _gather_start(shards)
    gidx = {k: i for i, k in enumerate(gnames)}

    passed = {}

    def pass_on(ks, after):
        idx = [gidx[k] for k in ks]
        s2, r2, ld = _gather_pass_on("gather_pass_on_" + "_".join(ks), [lands[i] for i in idx],
                                     [sends[i] for i in idx], [recvs[i] for i in idx], after)
        for j, k in enumerate(ks):
            passed[k] = (s2[j], r2[j], ld[j])

    def gathered(ks, after):
        idx = [gidx[k] for k in ks]
        return _gather_wait("gather_wait_" + "_".join(ks), [passed[k][2] for k in ks], [sends[i] for i in idx],
                            [recvs[i] for i in idx], [passed[k][0] for k in ks], [passed[k][1] for k in ks], after)

    saved = []
    xl = x2d
    cw_full = None
    for l in range(DEPTH):
        g1 = norm1_g[l].reshape(1, D)
        g2 = norm2_g[l].reshape(1, D)
        lg, lb = sgu_ln_g[l].reshape(1, D_A), sgu_ln_b[l].reshape(1, D_A)
        bst = sgu_b[l].T
        cb = conv_b[l].reshape(1, D_B)
        clg, clb = conv_ln_g[l].reshape(1, D_B), conv_ln_b[l].reshape(1, D_B)
        first = [f"w_in{l}"] if l == 0 else [f"w_in{l}", f"w_out{l}"]
        pass_on(first, xl)
        wg_in, *rest = gathered(first, xl)
        h, proj = _fwd_in(xl, g1, wg_in)
        if l == 0:
            pass_on(["conv_w", "w_out0"], h)
            cw_g, wout = gathered(["conv_w", "w_out0"], proj)
            cw_full = cw_g.reshape(N_DEV, DEPTH, CONV_W, D_B // N_DEV).transpose(1, 2, 0, 3).reshape(
                DEPTH, CONV_W, D_B)
        else:
            (wout,) = rest
            pass_on([f"w_ff1{l}"], proj)
        wout = wout.reshape(D, D)
        mix, c, g, x1, h2 = _mixer_fwd(proj, lg, lb, sgu_w[l], bst, cw_full[l], cb, clg, clb, wout, xl, g2)
        pass_on([f"w_ff1{l}"] if l == 0 else [f"w_ff2{l}"], h2)
        (wg_ff1,) = gathered([f"w_ff1{l}"], h2)
        r = _fwd_ff1(h2, wg_ff1)
        if l == 0:
            pass_on(["w_ff20"], r)
        (w2,) = gathered([f"w_ff2{l}"], r)
        w2 = w2.reshape(D_FF, D)
        saved.append(dict(x=xl, h=h, proj=proj, mix=mix, c=c, g=g, x1=x1, h2=h2, r=r, wg_in=wg_in, wout=wout,
                          wg_ff1=wg_ff1, w2=w2, g1=g1, g2=g2, lg=lg, lb=lb, bst=bst, clg=clg, clb=clb))
        if l + 1 < DEPTH:
            xl = _fwd_ff2(r, w2, x1)
        else:
            dx, dxb, loss_part, d_final_g = _fwd_ff2_loss(r, w2, x1, final_g.reshape(1, D), tgt)

    names = ["w_in", "w_out", "w_ff1", "w_ff2"]
    block = dict(w_in=(D, IN_COLS // N_DEV), w_out=(D // N_DEV, D), w_ff1=(D, D_FF // N_DEV), w_ff2=(D_FF // N_DEV, D))
    land = {k: lax.empty((DEPTH, N_DEV - 1) + block[k], BF16) for k in names}
    big32 = {k: [None] * DEPTH for k in names}
    big16 = {k: [None] * DEPTH for k in names}
    ssend = {k: [None] * DEPTH for k in names}
    srecv = {k: [None] * DEPTH for k in names}

    def send_grads(l, grads, after=None):
        ks = list(grads)
        for k in ks:
            big32[k][l] = grads[k][0].reshape((N_DEV,) + block[k])
        sends, recvs, g16s, lands = _scatter_start(
            "scatter_start_" + "_".join(ks) + str(l), [grads[k][1].reshape((N_DEV,) + block[k]) for k in ks],
            [land[k] for k in ks], l, after)
        for i, k in enumerate(ks):
            ssend[k][l], srecv[k][l], big16[k][l], land[k] = sends[i], recvs[i], g16s[i], lands[i]

    rep = dict(norm1_g=(norm1_g, m_norm1_g, v_norm1_g), sgu_ln_g=(sgu_ln_g, m_sgu_ln_g, v_sgu_ln_g),
               sgu_ln_b=(sgu_ln_b, m_sgu_ln_b, v_sgu_ln_b), sgu_w=(sgu_w, m_sgu_w, v_sgu_w),
               sgu_b=(sgu_b, m_sgu_b, v_sgu_b), conv_b=(conv_b, m_conv_b, v_conv_b),
               conv_ln_g=(conv_ln_g, m_conv_ln_g, v_conv_ln_g), conv_ln_b=(conv_ln_b, m_conv_ln_b, v_conv_ln_b),
               norm2_g=(norm2_g, m_norm2_g, v_norm2_g), final_g=(final_g, m_final_g, v_final_g))

    def small_parts():
        parts = []
        for k in SMALL:
            if k == "final_g":
                parts.append(d_final_g.reshape(rep[k][0].shape))
            else:
                parts.append(jnp.stack([small[l][k].reshape(rep[k][0].shape[1:]) for l in range(DEPTH)]))
        parts.append(jnp.stack(d_conv_w))
        parts.append(loss_part[0, 0:1])
        return parts

    small = {}
    d_conv_w = [None] * DEPTH
    for l in reversed(range(DEPTH)):
        s = saved[l]
        send_grads(l, dict(w_ff2=_wgrad("wgrad_ff2", s["r"], dxb, D, square_a=True)))
        df1, dx1, dx1b, dg2 = _bwd_mlp(dxb, dx, s["w2"], s["r"], s["wg_ff1"], s["x1"], s["g2"], big16["w_ff2"][l])
        send_grads(l, dict(w_ff1=_wgrad("wgrad_ff1", s["h2"], df1, D_FF // N_DEV),
                           w_out=_wgrad("wgrad_out", s["mix"], dx1b, D)))
        dpa, dc, dlg, dlb, dwm, dbs, dcb, dclg, dclb = _mixer_bwd_a(
            dx1b, s["wout"], s["proj"], s["c"], s["lg"], s["lb"], sgu_w[l], s["bst"], s["clg"], s["clb"],
            big16["w_out"][l])
        dproj, d_conv_w[l], dx, dxb, dg1 = _mixer_bwd_b(dc, s["g"], s["proj"], cw_full[l], dpa, s["wg_in"], s["x"],
                                                         s["g1"], dx1)
        small[l] = dict(norm1_g=dg1, sgu_ln_g=dlg, sgu_ln_b=dlb, sgu_w=dwm, sgu_b=dbs, conv_b=dcb, conv_ln_g=dclg,
                        conv_ln_b=dclb, norm2_g=dg2)
        if l > 0:
            send_grads(l, dict(w_in=_wgrad("wgrad_in", s["h"], dproj, IN_COLS // N_DEV)))
        else:
            parts = small_parts()
            packed = _pack(parts)
            ar_sends, ar_recvs, ar_slots = _all_reduce_start(packed)
            g_in = _wgrad("wgrad_in", s["h"], dproj, IN_COLS // N_DEV, dep=ar_slots)
            ar_sends, ar_recvs, total = _all_reduce_sum(packed, ar_slots, ar_sends, ar_recvs, _hbm(g_in[1]))
            send_grads(l, dict(w_in=g_in), total)
    grad_x = dx.reshape(1, T, D)

    me_block = _flat(*_coords()).astype(jnp.int32).reshape(1)
    wmv = dict(w_in=(w_in, m_w_in, v_w_in), w_out=(w_out, m_w_out, v_w_out), w_ff1=(w_ff1, m_w_ff1, v_w_ff1),
               w_ff2=(w_ff2, m_w_ff2, v_w_ff2))
    res = {}

    def update(tag, ks, after):
        recv = _scatter_wait("scatter_wait_" + tag, [g for k in ks for g in big16[k]], [land[k] for k in ks],
                             [q for k in ks for q in ssend[k]], [q for k in ks for q in srecv[k]], after)
        for k, rk in zip(ks, recv):
            res[k] = _adam_sharded("adam_" + k, me_block, big32[k], rk, *wmv[k])

    shapes = [p.shape for p in parts]
    update("early", ["w_ff2", "w_ff1", "w_out"], big16["w_in"][0])
    packed_sum = _all_reduce_wait(total, ar_sends, ar_recvs, [res[k][0] for k in ("w_ff2", "w_ff1", "w_out")])
    update("late", ["w_in"], packed_sum)
    summed = _unpack(packed_sum, shapes)
    loss = summed[-1][0]
    me = _flat(*_coords())
    g_conv_w = lax.dynamic_slice_in_dim(summed[-2], me * (D_B // N_DEV), D_B // N_DEV, axis=2)
    g_small = summed[:-2] + [g_conv_w]
    keys = SMALL + ["conv_w"]
    rep["conv_w"] = (conv_w, m_conv_w, v_conv_w)
    def two_d(a):
        return a.reshape(1, -1) if a.ndim == 1 else a

    d_s, nm_s, nv_s = _adam_small([two_d(g) for g in g_small], *[[two_d(rep[k][i]) for k in keys] for i in range(3)])
    for i, k in enumerate(keys):
        shape = rep[k][0].shape
        res[k] = (g_small[i], d_s[i].reshape(shape), nm_s[i].reshape(shape), nv_s[i].reshape(shape))

    order = ["norm1_g", "w_in", "sgu_ln_g", "sgu_ln_b", "sgu_w", "sgu_b", "conv_w", "conv_b", "conv_ln_g",
             "conv_ln_b", "w_out", "norm2_g", "w_ff1", "w_ff2", "final_g"]
    return (loss, grad_x, *[res[k][0] for k in order], *[res[k][1] for k in order],
            *[res[k][2] for k in order], *[res[k][3] for k in order])
```

```python
import functools

import jax
import jax.numpy as jnp
from jax import lax
from jax.experimental import pallas as pl
from jax.experimental.pallas import tpu as pltpu

F32 = jnp.float32
BF16 = jnp.bfloat16

N_DEV = 8
DEPTH = 2
T = 4096
D = 1024
D_A = 512
D_B = 512
CHUNK = 128
H_A = 4
H_B = 4
CONV_W = 31
HALO = 32
D_FF = 4096
IN_COLS = 2048
EPS = 1e-6

ADAM_LR = 0.001
ADAM_B1 = 0.9
ADAM_B2 = 0.999
ADAM_EPS = 1e-08
ADAM_WD = 0.01
ADAM_STEP = 10

TM = 512
TM_FWD = 1024
TT_WGRAD = 4096
TM_SGU = 512
RB = 64
CONV_CHAINS = 2
LANES = 128
MIB = 1024 * 1024
SCOPED_VMEM_MIB = 60

SQRT_HALF = 0.7071067811865476
INV_SQRT_2PI = 0.3989422804014327

MESH_ID = pl.DeviceIdType.MESH
HBM = pl.BlockSpec(memory_space=pltpu.HBM)
SEM = pl.BlockSpec(memory_space=pltpu.SEMAPHORE)
EFFECT = pltpu.SideEffectType.DATAFLOW_SIDE_EFFECTING


def _cparams(vmem_mib, sem=("arbitrary",)):
    assert vmem_mib <= SCOPED_VMEM_MIB
    return pltpu.CompilerParams(dimension_semantics=sem, vmem_limit_bytes=SCOPED_VMEM_MIB * MIB)


def _full(shape):
    return pl.BlockSpec(shape, lambda *_: (0,) * len(shape))


def _rows(tm, cols):
    return pl.BlockSpec((tm, cols), lambda i: (i, 0))


def _gelu(x):
    cdf = 0.5 * (1.0 + lax.erf(x * SQRT_HALF))
    return x * cdf, cdf


def _gelu_grad(x, cdf):
    return cdf + x * (INV_SQRT_2PI * jnp.exp(-0.5 * x * x))


def _sigmoid(x):
    return 1.0 / (1.0 + jnp.exp(-x))


def _ln(x):
    mu = jnp.mean(x, axis=-1, keepdims=True)
    xc = x - mu
    rstd = lax.rsqrt(jnp.mean(xc * xc, axis=-1, keepdims=True) + EPS)
    return xc * rstd, rstd


def _ln_bwd(dyh, xhat, rstd):
    return rstd * (dyh - jnp.mean(dyh, axis=-1, keepdims=True) - xhat * jnp.mean(dyh * xhat, axis=-1, keepdims=True))


def _rms(x):
    return lax.rsqrt(jnp.mean(x * x, axis=-1, keepdims=True) + EPS)


def _rms_bwd(dh, x, r, g):
    n = x * r
    dn = dh * g
    dx = r * (dn - n * jnp.mean(dn * n, axis=-1, keepdims=True))
    return dx, jnp.sum(dh * n, axis=0, keepdims=True)


def _dot(a, b):
    return jnp.dot(a, b, preferred_element_type=F32)


def _dot_nt(a, b):
    return lax.dot_general(a, b, (((1,), (1,)), ((), ())), preferred_element_type=F32)


def _dot_tn(a, b):
    return lax.dot_general(a, b, (((0,), (0,)), ((), ())), preferred_element_type=F32)


def _tril_mask():
    r = lax.broadcasted_iota(jnp.int32, (CHUNK, CHUNK), 0)
    c = lax.broadcasted_iota(jnp.int32, (CHUNK, CHUNK), 1)
    return r >= c


def _fwd_in(x, g1, wg):
    bn = wg.shape[2]

    def body(x_ref, g_ref, w_ref, h_ref, p_ref):
        xv = x_ref[...]
        h = (xv * _rms(xv) * g_ref[...]).astype(BF16)
        h_ref[...] = h
        for j in range(N_DEV):
            p_ref[:, j * bn:(j + 1) * bn] = _dot(h, w_ref[j])

    return pl.pallas_call(
        body, name="fwd_in", grid=(T // TM_FWD,),
        in_specs=[_rows(TM_FWD, D), _full((1, D)), _full(wg.shape)],
        out_specs=[_rows(TM_FWD, D), _rows(TM_FWD, IN_COLS)],
        out_shape=[jax.ShapeDtypeStruct((T, D), BF16), jax.ShapeDtypeStruct((T, IN_COLS), F32)],
        compiler_params=_cparams(32),
    )(*map(_hbm, (x, g1, wg)))


def _fill_shift_buffer(sh_ref, row0, value):
    for q in range(sh_ref.shape[1]):
        sh_ref[0, q, row0:row0 + value.shape[0], :] = value[:, q * LANES:(q + 1) * LANES]


def _build_shifts(sh_ref):
    rows = sh_ref.shape[2]
    for p in range(1, 8):
        for q in range(sh_ref.shape[1]):
            sh_ref[p, q, 0:rows - 8, :] = sh_ref[0, q, p:p + rows - 8, :]


def _shifted(sh_ref, q, base, off):
    start = base + (off - off % 8)
    if not isinstance(start, int):
        start = pl.multiple_of(start, 8)
    return sh_ref[off % 8, q, pl.ds(start, RB), :]


def _conv_taps(sh_ref, w_ref, q, base, first_tap_row, step):
    cols = slice(q * LANES, (q + 1) * LANES)
    acc = [jnp.zeros((RB, LANES), F32) for _ in range(CONV_CHAINS)]
    for k in range(CONV_W):
        term = _shifted(sh_ref, q, base, first_tap_row + step * k) * w_ref[k:k + 1, cols]
        acc[k % CONV_CHAINS] = acc[k % CONV_CHAINS] + term
    return functools.reduce(lambda a, b: a + b, acc)


def _mixer_fwd(proj, lg, lb, wm, bst, cw, cb, clg, clb, wout, x, g2):
    tm = TM_SGU
    hb = tm // HALO

    def body(p_ref, ph_ref, lg_ref, lb_ref, wm_ref, bs_ref, cw_ref, cb_ref, clg_ref, clb_ref, wo_ref, x_ref, g2_ref,
             mix_ref, c_ref, g_ref, x1_ref, h2_ref, gbuf):
        i = pl.program_id(0)
        u, _ = _gelu(p_ref[:, 0:D_A])
        vg, _ = _gelu(p_ref[:, D_A:2 * D_A])
        xhat, _ = _ln(vg)
        v = (xhat * lg_ref[...] + lb_ref[...]).astype(BF16)
        mask = _tril_mask()
        for h in range(H_A):
            hc = slice(h * CHUNK, (h + 1) * CHUNK)
            wmh = jnp.where(mask, wm_ref[h], 0.0).astype(BF16)
            for c in range(tm // CHUNK):
                rc = slice(c * CHUNK, (c + 1) * CHUNK)
                mixed = _dot(wmh, v[rc, hc]) + bs_ref[:, h:h + 1]
                mix_ref[rc, hc] = (u[rc, hc] * mixed).astype(BF16)
        x1_ref[...] = x_ref[...] + _dot(mix_ref[:, 0:D_A], wo_ref[0:D_A, :])

        g = p_ref[:, 2 * D_A:2 * D_A + D_B] * _sigmoid(p_ref[:, 2 * D_A + D_B:IN_COLS])
        g_ref[...] = g
        gh = ph_ref[:, 0:D_B] * _sigmoid(ph_ref[:, D_B:2 * D_B])
        _fill_shift_buffer(gbuf, 0, jnp.where(i > 0, gh, 0.0))
        _fill_shift_buffer(gbuf, HALO, g)
        _build_shifts(gbuf)
        for q in range(H_B):
            cols = slice(q * LANES, (q + 1) * LANES)
            for rb in range(tm // RB):
                acc = _conv_taps(gbuf, cw_ref, q, rb * RB, HALO - (CONV_W - 1), 1)
                c_ref[rb * RB:(rb + 1) * RB, cols] = acc + cb_ref[:, cols]
        for q in range(H_B):
            cols = slice(q * LANES, (q + 1) * LANES)
            chat, _ = _ln(c_ref[:, cols])
            z = chat * clg_ref[:, cols] + clb_ref[:, cols]
            mix_ref[:, D_A + q * LANES:D_A + (q + 1) * LANES] = (z * _sigmoid(z)).astype(BF16)

        x1 = x1_ref[...] + _dot(mix_ref[:, D_A:D], wo_ref[D_A:D, :])
        x1_ref[...] = x1
        h2_ref[...] = (x1 * _rms(x1) * g2_ref[...]).astype(BF16)

    vec = _full((1, D_A))
    return pl.pallas_call(
        body, name="mixer_fwd", grid=(T // tm,),
        in_specs=[_rows(tm, IN_COLS),
                  pl.BlockSpec((HALO, 2 * D_B), lambda i: (jnp.maximum(i * hb - 1, 0), 1)),
                  vec, vec, _full((H_A, CHUNK, CHUNK)), _full((CHUNK, H_A)),
                  _full((CONV_W, D_B)), vec, vec, vec,
                  pl.BlockSpec((D, D), lambda i: (0, 0), pipeline_mode=pl.Buffered(1)), _rows(tm, D), _full((1, D))],
        out_specs=[_rows(tm, D), _rows(tm, D_B), _rows(tm, D_B), _rows(tm, D), _rows(tm, D)],
        out_shape=[jax.ShapeDtypeStruct((T, D), BF16), jax.ShapeDtypeStruct((T, D_B), F32),
                   jax.ShapeDtypeStruct((T, D_B), F32), jax.ShapeDtypeStruct((T, D), F32),
                   jax.ShapeDtypeStruct((T, D), BF16)],
        scratch_shapes=[pltpu.VMEM((8, H_B, HALO + tm, LANES), F32)],
        compiler_params=_cparams(40),
    )(*map(_hbm, (proj, proj, lg, lb, wm, bst, cw, cb, clg, clb, wout, x, g2)))


def _fwd_ff1(h2, wg):
    bn = wg.shape[2]

    def body(h_ref, w_ref, r_ref):
        h = h_ref[...]
        for j in range(N_DEV):
            r_ref[:, j * bn:(j + 1) * bn] = jnp.maximum(_dot(h, w_ref[j]), 0.0).astype(BF16)

    return pl.pallas_call(
        body, name="fwd_ff1", grid=(T // TM_FWD,),
        in_specs=[_rows(TM_FWD, D), _full(wg.shape)],
        out_specs=_rows(TM_FWD, D_FF),
        out_shape=jax.ShapeDtypeStruct((T, D_FF), BF16),
        compiler_params=_cparams(48),
    )(h2, wg)


def _fwd_ff2(r, w2, x1):
    def body(r_ref, w_ref, x_ref, o_ref):
        rv = r_ref[...]
        o_ref[...] = x_ref[...] + _dot(rv * rv, w_ref[...])

    return pl.pallas_call(
        body, name="fwd_ff2", grid=(T // TM_FWD,),
        in_specs=[_rows(TM_FWD, D_FF), _full((D_FF, D)), _rows(TM_FWD, D)],
        out_specs=_rows(TM_FWD, D),
        out_shape=jax.ShapeDtypeStruct((T, D), F32),
        compiler_params=_cparams(48),
    )(r, w2, x1)


def _fwd_ff2_loss(r, w2, x1, gf, tgt):
    def body(r_ref, w_ref, x_ref, g_ref, t_ref, dx_ref, dxb_ref, loss_ref, dg_ref):
        i = pl.program_id(0)

        @pl.when(i == 0)
        def _():
            loss_ref[...] = jnp.zeros(loss_ref.shape, F32)
            dg_ref[...] = jnp.zeros(dg_ref.shape, F32)

        rv = r_ref[...]
        xv = x_ref[...] + _dot(rv * rv, w_ref[...])
        rn = _rms(xv)
        diff = xv * rn * g_ref[...] - t_ref[...]
        loss_ref[...] += 0.5 * jnp.sum(jnp.mean(diff * diff, axis=-1, keepdims=True), axis=0, keepdims=True)
        dx, dg = _rms_bwd(diff * (1.0 / D), xv, rn, g_ref[...])
        dx_ref[...] = dx
        dxb_ref[...] = dx.astype(BF16)
        dg_ref[...] += dg

    return pl.pallas_call(
        body, name="fwd_ff2_loss", grid=(T // TM,),
        in_specs=[_rows(TM, D_FF), pl.BlockSpec((D_FF, D), lambda i: (0, 0), pipeline_mode=pl.Buffered(1)),
                  _rows(TM, D), _full((1, D)), _rows(TM, D)],
        out_specs=[_rows(TM, D), _rows(TM, D), _full((8, LANES)), _full((1, D))],
        out_shape=[jax.ShapeDtypeStruct((T, D), F32), jax.ShapeDtypeStruct((T, D), BF16),
                   jax.ShapeDtypeStruct((8, LANES), F32), jax.ShapeDtypeStruct((1, D), F32)],
        compiler_params=_cparams(40),
    )(*map(_hbm, (r, w2, x1, gf, tgt)))


def _bwd_mlp(dxb, dres, w2, r, wg1, x1, g2, dep):
    bn = wg1.shape[2]

    def body(d_ref, dres_ref, w2_ref, r_ref, w1_ref, x_ref, g_ref, dep_ref, df1_ref, dx_ref, dxb_ref, dg_ref):
        i = pl.program_id(0)

        @pl.when(i == 0)
        def _():
            dg_ref[...] = jnp.zeros(dg_ref.shape, F32)

        d = d_ref[...]
        dh = jnp.zeros((TM, D), F32)
        for j in range(N_DEV):
            cols = slice(j * bn, (j + 1) * bn)
            df1 = (2.0 * r_ref[:, cols].astype(F32) * _dot_nt(d, w2_ref[cols, :])).astype(BF16)
            df1_ref[:, cols] = df1
            dh = dh + _dot_nt(df1, w1_ref[j])
        xv = x_ref[...]
        dxn, dg = _rms_bwd(dh, xv, _rms(xv), g_ref[...])
        dx = dres_ref[...] + dxn
        dx_ref[...] = dx
        dxb_ref[...] = dx.astype(BF16)
        dg_ref[...] += dg

    once = dict(pipeline_mode=pl.Buffered(1))
    return pl.pallas_call(
        body, name="bwd_mlp", grid=(T // TM,),
        in_specs=[_rows(TM, D), _rows(TM, D), pl.BlockSpec((D_FF, D), lambda i: (0, 0), **once), _rows(TM, D_FF),
                  pl.BlockSpec(wg1.shape, lambda i: (0, 0, 0), **once), _rows(TM, D), _full((1, D)), HBM],
        out_specs=[_rows(TM, D_FF), _rows(TM, D), _rows(TM, D), _full((1, D))],
        out_shape=[jax.ShapeDtypeStruct((T, D_FF), BF16), jax.ShapeDtypeStruct((T, D), F32),
                   jax.ShapeDtypeStruct((T, D), BF16), jax.ShapeDtypeStruct((1, D), F32)],
        compiler_params=_cparams(56),
    )(*map(_hbm, (dxb, dres, w2, r, wg1, x1, g2, dep)))


def _mixer_bwd_a(dxb, wout, proj, c, lg, lb, wm, bst, clg, clb, dep):
    tm = TM_SGU
    n_tiles = T // tm

    def body(dx_ref, wo_ref, p_ref, c_ref, lg_ref, lb_ref, wm_ref, bs_ref, clg_ref, clb_ref, dep_ref,
             dpa_ref, dc_ref, dlg_ref, dlb_ref, dwm_ref, dbs_ref, dcb_ref, dclg_ref, dclb_ref,
             dv_buf, db_acc):
        i = pl.program_id(0)

        @pl.when(i == 0)
        def _():
            for ref in (dlg_ref, dlb_ref, dwm_ref, dbs_ref, dcb_ref, dclg_ref, dclb_ref, db_acc):
                ref[...] = jnp.zeros(ref.shape, F32)

        dmix = _dot_nt(dx_ref[...], wo_ref[...])
        ua = p_ref[:, 0:D_A]
        va = p_ref[:, D_A:2 * D_A]
        u, cdf_u = _gelu(ua)
        vg, cdf_v = _gelu(va)
        xhat, rstd = _ln(vg)
        v = (xhat * lg_ref[...] + lb_ref[...]).astype(BF16)
        mask = _tril_mask()
        for h in range(H_A):
            hc = slice(h * CHUNK, (h + 1) * CHUNK)
            wmh = jnp.where(mask, wm_ref[h], 0.0).astype(BF16)
            for cidx in range(tm // CHUNK):
                rc = slice(cidx * CHUNK, (cidx + 1) * CHUNK)
                vb = v[rc, hc]
                mixed = _dot(wmh, vb) + bs_ref[:, h:h + 1]
                da = dmix[rc, hc]
                dpa_ref[rc, hc] = (da * mixed * _gelu_grad(ua[rc, hc], cdf_u[rc, hc])).astype(BF16)
                dmixed = da * u[rc, hc]
                dmb = dmixed.astype(BF16)
                dv_buf[rc, hc] = _dot_tn(wmh, dmb)
                dwm_ref[h] += _dot_nt(dmb, vb)
                db_acc[:, hc] += dmixed
        dv = dv_buf[...]
        dlb_ref[...] += jnp.sum(dv, axis=0, keepdims=True)
        dlg_ref[...] += jnp.sum(dv * xhat, axis=0, keepdims=True)
        dvg = _ln_bwd(dv * lg_ref[...], xhat, rstd)
        dpa_ref[:, D_A:2 * D_A] = (dvg * _gelu_grad(va, cdf_v)).astype(BF16)

        for q in range(H_B):
            cols = slice(q * LANES, (q + 1) * LANES)
            chat, crstd = _ln(c_ref[:, cols])
            z = chat * clg_ref[:, cols] + clb_ref[:, cols]
            sg = _sigmoid(z)
            dz = dmix[:, D_A + q * LANES:D_A + (q + 1) * LANES] * (sg * (1.0 + z * (1.0 - sg)))
            dclb_ref[:, cols] += jnp.sum(dz, axis=0, keepdims=True)
            dclg_ref[:, cols] += jnp.sum(dz * chat, axis=0, keepdims=True)
            dc = _ln_bwd(dz * clg_ref[:, cols], chat, crstd)
            dc_ref[:, cols] = dc
            dcb_ref[:, cols] += jnp.sum(dc, axis=0, keepdims=True)

        @pl.when(i == n_tiles - 1)
        def _():
            for h in range(H_A):
                hc = slice(h * CHUNK, (h + 1) * CHUNK)
                dwm_ref[h] = jnp.where(mask, dwm_ref[h], 0.0)
                dbs_ref[h:h + 1, :] = jnp.sum(db_acc[:, hc].T, axis=0, keepdims=True)

    vec = _full((1, D_A))
    vshape = jax.ShapeDtypeStruct((1, D_A), F32)
    return pl.pallas_call(
        body, name="mixer_bwd_a", grid=(n_tiles,),
        in_specs=[_rows(tm, D), _full((D, D)), _rows(tm, IN_COLS), _rows(tm, D_B), vec, vec,
                  _full((H_A, CHUNK, CHUNK)), _full((CHUNK, H_A)), vec, vec, HBM],
        out_specs=[_rows(tm, 2 * D_A), _rows(tm, D_B), vec, vec, _full((H_A, CHUNK, CHUNK)),
                   _full((H_A, CHUNK)), vec, vec, vec],
        out_shape=[jax.ShapeDtypeStruct((T, 2 * D_A), BF16), jax.ShapeDtypeStruct((T, D_B), F32), vshape, vshape,
                   jax.ShapeDtypeStruct((H_A, CHUNK, CHUNK), F32), jax.ShapeDtypeStruct((H_A, CHUNK), F32),
                   vshape, vshape, vshape],
        scratch_shapes=[pltpu.VMEM((tm, D_A), F32), pltpu.VMEM((CHUNK, D_A), F32)],
        compiler_params=_cparams(32),
    )(*map(_hbm, (dxb, wout, proj, c, lg, lb, wm, bst, clg, clb, dep)))


def _mixer_bwd_b(dc, g, proj, cw, dpa, wg_in, x, g1, dres):
    tm = TM_SGU
    n_tiles = T // tm
    hb = tm // HALO
    bn = wg_in.shape[2]

    def body(dc_ref, dch_ref, g_ref, p_ref, cw_ref, dpa_ref, win_ref, x_ref, g1_ref, dres_ref,
             dp_ref, dcw_ref, dx_ref, dxb_ref, dg1_ref, dcbuf, dwacc):
        i = pl.program_id(0)

        @pl.when(i == 0)
        def _():
            dwacc[...] = jnp.zeros(dwacc.shape, F32)
            dg1_ref[...] = jnp.zeros(dg1_ref.shape, F32)

        _fill_shift_buffer(dcbuf, 0, dc_ref[...])
        _fill_shift_buffer(dcbuf, tm, jnp.where(i < n_tiles - 1, dch_ref[...], 0.0))
        _build_shifts(dcbuf)
        dp_ref[:, 0:2 * D_A] = dpa_ref[...]
        for q in range(H_B):
            cols = slice(q * LANES, (q + 1) * LANES)

            def row_block(rb, carry, q=q, cols=cols):
                base = pl.multiple_of(rb * RB, RB)
                rows = pl.ds(base, RB)
                gv = g_ref[rows, cols]
                acc = [jnp.zeros((RB, LANES), F32) for _ in range(CONV_CHAINS)]
                parts = []
                for k in range(CONV_W):
                    xk = _shifted(dcbuf, q, base, CONV_W - 1 - k)
                    acc[k % CONV_CHAINS] = acc[k % CONV_CHAINS] + xk * cw_ref[k:k + 1, cols]
                    parts.append(jnp.sum((gv * xk).reshape(RB // 8, 8, LANES), axis=0))
                dg = functools.reduce(lambda a, b: a + b, acc)
                val = p_ref[rows, cols]
                sg = _sigmoid(p_ref[rows, D_B + q * LANES:D_B + (q + 1) * LANES])
                dp_ref[rows, 2 * D_A + q * LANES:2 * D_A + (q + 1) * LANES] = (dg * sg).astype(BF16)
                dp_ref[rows, 2 * D_A + D_B + q * LANES:2 * D_A + D_B + (q + 1) * LANES] = (
                    dg * val * sg * (1.0 - sg)).astype(BF16)
                for k in range(CONV_W):
                    dwacc[k * 8:(k + 1) * 8, cols] += parts[k]
                return carry

            lax.fori_loop(0, tm // RB, row_block, 0)

        dh = jnp.zeros((tm, D), F32)
        for j in range(N_DEV):
            dh = dh + _dot_nt(dp_ref[:, j * bn:(j + 1) * bn], win_ref[j])
        xv = x_ref[...]
        dxn, dg = _rms_bwd(dh, xv, _rms(xv), g1_ref[...])
        dx = dres_ref[...] + dxn
        dx_ref[...] = dx
        dxb_ref[...] = dx.astype(BF16)
        dg1_ref[...] += dg

        @pl.when(i == n_tiles - 1)
        def _():
            for k in range(CONV_W):
                dcw_ref[k:k + 1, :] = jnp.sum(dwacc[k * 8:(k + 1) * 8, :], axis=0, keepdims=True)

    return pl.pallas_call(
        body, name="mixer_bwd_b", grid=(n_tiles,),
        in_specs=[_rows(tm, D_B),
                  pl.BlockSpec((HALO, D_B), lambda i: (jnp.minimum((i + 1) * hb, T // HALO - 1), 0)),
                  _rows(tm, D_B),
                  pl.BlockSpec((tm, 2 * D_B), lambda i: (i, 1)),
                  _full((CONV_W, D_B)), _rows(tm, 2 * D_A),
                  pl.BlockSpec(wg_in.shape, lambda i: (0, 0, 0), pipeline_mode=pl.Buffered(1)),
                  _rows(tm, D), _full((1, D)), _rows(tm, D)],
        out_specs=[_rows(tm, IN_COLS), _full((CONV_W, D_B)), _rows(tm, D), _rows(tm, D), _full((1, D))],
        out_shape=[jax.ShapeDtypeStruct((T, IN_COLS), BF16), jax.ShapeDtypeStruct((CONV_W, D_B), F32),
                   jax.ShapeDtypeStruct((T, D), F32), jax.ShapeDtypeStruct((T, D), BF16),
                   jax.ShapeDtypeStruct((1, D), F32)],
        scratch_shapes=[pltpu.VMEM((8, H_B, tm + HALO, LANES), F32), pltpu.VMEM((CONV_W * 8, D_B), F32)],
        compiler_params=_cparams(40),
    )(*map(_hbm, (dc, dc, g, proj, cw, dpa, wg_in, x, g1, dres)))


def _wgrad(name, a, g, bn, square_a=False, dep=None):
    deps = [] if dep is None else [dep]
    k = a.shape[1]
    n = g.shape[1]
    tk = min(k, 1024)
    tn = min(n, max(bn, 1024))
    nsub = tn // bn
    tt = TT_WGRAD
    nt = T // tt

    def body(a_ref, g_ref, *refs):
        o_ref, ob_ref = refs[len(deps):]
        t = pl.program_id(2)

        @pl.when(t == 0)
        def _():
            o_ref[...] = jnp.zeros(o_ref.shape, F32)

        av = a_ref[...]
        if square_a:
            av = av * av
        for s in range(nsub):
            o_ref[s] += _dot_tn(av, g_ref[:, s * bn:(s + 1) * bn])

        @pl.when(t == nt - 1)
        def _():
            ob_ref[...] = o_ref[...].astype(BF16)

    ospec = pl.BlockSpec((nsub, tk, bn), lambda ki, ni, ti: (ni, ki, 0))
    return pl.pallas_call(
        body, name=name, grid=(k // tk, n // tn, nt),
        in_specs=[pl.BlockSpec((tt, tk), lambda ki, ni, ti: (ti, ki)),
                  pl.BlockSpec((tt, tn), lambda ki, ni, ti: (ti, ni))] + [HBM] * len(deps),
        out_specs=[ospec, ospec],
        out_shape=[jax.ShapeDtypeStruct((n // bn, k, bn), F32), jax.ShapeDtypeStruct((n // bn, k, bn), BF16)],
        compiler_params=_cparams(40, ("arbitrary", "arbitrary", "arbitrary")),
    )(a, g, *map(_hbm, deps))


def _coords():
    return lax.axis_index("x"), lax.axis_index("y"), lax.axis_index("c")


def _flat(x, y, c):
    return 4 * x + 2 * y + c


def _peer(k):
    x, y, c = _coords()
    return (x ^ ((k >> 2) & 1), y ^ ((k >> 1) & 1), c ^ (k & 1))


def _hbm(a):
    return pltpu.with_memory_space_constraint(a, pltpu.HBM)


def _hbm_like(a):
    return pltpu.HBM(a.shape, a.dtype)


def _peer_sems():
    return pltpu.SemaphoreType.DMA((N_DEV - 1,))


SIBLING = 1
CHIP_PEERS = (2, 4, 6)
FIRST_PEERS = (SIBLING,) + CHIP_PEERS


def _gather_start(shards):
    n = len(shards)
    shapes = [(s.shape if l is None else s.shape[1:]) for s, l, _ in shards]

    def body(*refs):
        ins, send, recv, lnd = refs[:n], refs[n:2 * n], refs[2 * n:3 * n], refs[3 * n:4 * n]
        stage_in, stage_out = refs[4 * n:5 * n], refs[5 * n:6 * n]
        in_sems, out_sems = refs[6 * n], refs[6 * n + 1]
        me = _flat(*_coords())
        srcs = [ins[a] if shards[a][1] is None else ins[a].at[shards[a][1]] for a in range(n)]
        loads = [pltpu.make_async_copy(srcs[a], stage_in[a], in_sems.at[a]) for a in range(n)]
        stores = [pltpu.make_async_copy(stage_out[a], lnd[a].at[me], out_sems.at[a]) for a in range(n)]

        def copies(a):
            return [pltpu.make_async_remote_copy(
                src_ref=lnd[a].at[me], dst_ref=lnd[a].at[me], send_sem=send[a].at[j], recv_sem=recv[a].at[j],
                device_id=_peer(k), device_id_type=MESH_ID) for j, k in enumerate(FIRST_PEERS)]

        def send_block(a):
            stores[a].wait()
            for cp in copies(a):
                cp.start()

        for cp in loads:
            cp.start()
        for a in range(n):
            loads[a].wait()
            stage_out[a][...] = stage_in[a][...].astype(stage_out[a].dtype)
            stores[a].start()
            if a == 0:
                send_block(0)
        for cp in copies(0):
            cp.wait_send()
        for a in range(1, n):
            send_block(a)

    sems = pltpu.SemaphoreType.DMA((len(FIRST_PEERS),))
    outs = pl.pallas_call(
        body, name="gather_start",
        out_shape=(*[sems] * (2 * n), *[pltpu.HBM((N_DEV,) + shapes[a], shards[a][2]) for a in range(n)]),
        in_specs=[HBM] * n, out_specs=(*[SEM] * (2 * n), *[HBM] * n),
        scratch_shapes=[*[pltpu.VMEM(shapes[a], shards[a][0].dtype) for a in range(n)],
                        *[pltpu.VMEM(shapes[a], shards[a][2]) for a in range(n)],
                        pltpu.SemaphoreType.DMA((n,)), pltpu.SemaphoreType.DMA((n,))],
        compiler_params=pltpu.CompilerParams(has_side_effects=EFFECT, vmem_limit_bytes=40 * MIB),
    )(*[_hbm(s) for s, _, _ in shards])
    return outs[:n], outs[n:2 * n], outs[2 * n:]


def _gather_pass_on(name, lands, sends, recvs, after, sent=False):
    n = len(lands)

    def body(*refs):
        lnd, send, recv = refs[:n], refs[n:2 * n], refs[2 * n:3 * n]
        send2, recv2 = refs[3 * n + 1:4 * n + 1], refs[4 * n + 1:5 * n + 1]
        sibling = _peer(SIBLING)
        for a in range(n):
            for j, k in enumerate(CHIP_PEERS):
                cp = pltpu.make_async_remote_copy(
                    src_ref=lnd[a].at[0], dst_ref=lnd[a].at[0], send_sem=send[a].at[1 + j],
                    recv_sem=recv[a].at[1 + j], device_id=_peer(k), device_id_type=MESH_ID)
                if not sent:
                    cp.wait_send()
                cp.wait_recv()
                blk = _flat(*_peer(k))
                pltpu.make_async_remote_copy(
                    src_ref=lnd[a].at[blk], dst_ref=lnd[a].at[blk], send_sem=send2[a].at[j],
                    recv_sem=recv2[a].at[j], device_id=sibling, device_id_type=MESH_ID).start()

    sems = pltpu.SemaphoreType.DMA((len(CHIP_PEERS),))
    outs = pl.pallas_call(
        body, name=name, out_shape=(*[sems] * (2 * n), *[_hbm_like(l) for l in lands]),
        in_specs=(*[HBM] * n, *[SEM] * (2 * n), HBM), out_specs=(*[SEM] * (2 * n), *[HBM] * n),
        input_output_aliases={i: 2 * n + i for i in range(n)},
        compiler_params=pltpu.CompilerParams(has_side_effects=EFFECT),
    )(*lands, *sends, *recvs, after)
    return outs[:n], outs[n:2 * n], outs[2 * n:]


def _gather_wait(name, lands, sends, recvs, sends2, recvs2, after, sent=False):
    n = len(lands)

    def body(*refs):
        lnd, send, recv = refs[:n], refs[n:2 * n], refs[2 * n:3 * n]
        send2, recv2 = refs[3 * n:4 * n], refs[4 * n:5 * n]
        sibling = _peer(SIBLING)
        for a in range(n):
            own = pltpu.make_async_remote_copy(
                src_ref=lnd[a].at[0], dst_ref=lnd[a].at[0], send_sem=send[a].at[0], recv_sem=recv[a].at[0],
                device_id=sibling, device_id_type=MESH_ID)
            if not sent:
                own.wait_send()
            own.wait_recv()
            for j in range(len(CHIP_PEERS)):
                cp = pltpu.make_async_remote_copy(
                    src_ref=lnd[a].at[0], dst_ref=lnd[a].at[0], send_sem=send2[a].at[j], recv_sem=recv2[a].at[j],
                    device_id=sibling, device_id_type=MESH_ID)
                cp.wait_send()
                cp.wait_recv()

    return pl.pallas_call(
        body, name=name, out_shape=[_hbm_like(l) for l in lands],
        in_specs=(*[HBM] * n, *[SEM] * (4 * n), HBM), out_specs=[HBM] * n,
        input_output_aliases={i: i for i in range(n)},
        compiler_params=pltpu.CompilerParams(has_side_effects=EFFECT),
    )(*lands, *sends, *recvs, *sends2, *recvs2, after)


def _scatter_start(name, grads, lands, layer, after=None):
    n = len(grads)
    afters = [] if after is None else [after]

    def body(*refs):
        g, lnd = refs[:n], refs[n:2 * n]
        send, recv = refs[2 * n + len(afters):3 * n + len(afters)], refs[3 * n + len(afters):4 * n + len(afters)]
        for a in range(n):
            for k in range(1, N_DEV):
                to = _peer(k)
                pltpu.make_async_remote_copy(
                    src_ref=g[a].at[_flat(*to)], dst_ref=lnd[a].at[layer, k - 1], send_sem=send[a].at[k - 1],
                    recv_sem=recv[a].at[k - 1], device_id=to, device_id_type=MESH_ID).start()

    outs = pl.pallas_call(
        body, name=name,
        out_shape=(*[_peer_sems()] * (2 * n), *[_hbm_like(g) for g in grads], *[_hbm_like(l) for l in lands]),
        in_specs=[HBM] * (2 * n + len(afters)), out_specs=(*[SEM] * (2 * n), *[HBM] * (2 * n)),
        input_output_aliases={i: 2 * n + i for i in range(2 * n)},
        compiler_params=pltpu.CompilerParams(has_side_effects=EFFECT),
    )(*[_hbm(g) for g in grads], *[_hbm(l) for l in lands], *afters)
    return outs[:n], outs[n:2 * n], outs[2 * n:3 * n], outs[3 * n:]


def _scatter_wait(name, grads, lands, sends, recvs, after):
    n, nw = len(grads), len(lands)

    def body(*refs):
        g, lnd = refs[:n], refs[n:n + nw]
        send, recv = refs[n + nw:2 * n + nw], refs[2 * n + nw:3 * n + nw]
        for a in range(n):
            for k in range(1, N_DEV):
                cp = pltpu.make_async_remote_copy(
                    src_ref=g[a].at[0], dst_ref=lnd[a // DEPTH].at[a % DEPTH, 0],
                    send_sem=send[a].at[k - 1], recv_sem=recv[a].at[k - 1],
                    device_id=_peer(k), device_id_type=MESH_ID)
                cp.wait_send()
                cp.wait_recv()

    outs = pl.pallas_call(
        body, name=name,
        out_shape=(*[_hbm_like(g) for g in grads], *[_hbm_like(l) for l in lands]),
        in_specs=(*[HBM] * (n + nw), *[SEM] * (2 * n), HBM), out_specs=[HBM] * (n + nw),
        input_output_aliases={i: i for i in range(n + nw)},
        compiler_params=pltpu.CompilerParams(has_side_effects=EFFECT),
    )(*grads, *lands, *sends, *recvs, after)
    return outs[n:]


def _rows_block(ref, d, br):
    return ref.at[pl.ds(pl.multiple_of(d * br, 8), br), :]


def _all_reduce_start(part):
    rows = part.shape[0]
    br = rows // N_DEV
    assert br * N_DEV == rows and br % 8 == 0

    def body(p_ref, _slots_in, send, recv, slots):
        me = _flat(*_coords())
        for k in range(1, N_DEV):
            to = _peer(k)
            pltpu.make_async_remote_copy(
                src_ref=_rows_block(p_ref, _flat(*to), br), dst_ref=slots.at[me], send_sem=send.at[k - 1],
                recv_sem=recv.at[k - 1], device_id=to, device_id_type=MESH_ID).start()

    slots = lax.empty((N_DEV, br, LANES), F32)
    return pl.pallas_call(
        body, name="all_reduce_start",
        out_shape=(_peer_sems(), _peer_sems(), _hbm_like(slots)),
        in_specs=[HBM, HBM], out_specs=(SEM, SEM, HBM), input_output_aliases={1: 2},
        compiler_params=pltpu.CompilerParams(has_side_effects=EFFECT),
    )(_hbm(part), _hbm(slots))


def _all_reduce_sum(part, slots, sends, recvs, after):
    rows = part.shape[0]
    br = rows // N_DEV

    def body(p_ref, slots, _total_in, send1, recv1, _after, send2, recv2, total, stage, acc_ref, sems):
        me = _flat(*_coords())
        for k in range(1, N_DEV):
            cp = pltpu.make_async_remote_copy(
                src_ref=_rows_block(p_ref, 0, br), dst_ref=slots.at[0], send_sem=send1.at[k - 1],
                recv_sem=recv1.at[k - 1], device_id=_peer(k), device_id_type=MESH_ID)
            cp.wait_send()
            cp.wait_recv()
        loads = [pltpu.make_async_copy(_rows_block(p_ref, me, br), stage.at[me], sems.at[0])]
        for k in range(1, N_DEV):
            d = _flat(*_peer(k))
            loads.append(pltpu.make_async_copy(slots.at[d], stage.at[d], sems.at[k]))
        for cp in loads:
            cp.start()
        for cp in loads:
            cp.wait()
        acc = stage[0]
        for d in range(1, N_DEV):
            acc = acc + stage[d]
        acc_ref[...] = acc
        store = pltpu.make_async_copy(acc_ref, _rows_block(total, me, br), sems.at[0])
        store.start()
        store.wait()
        for k in range(1, N_DEV):
            pltpu.make_async_remote_copy(
                src_ref=_rows_block(total, me, br), dst_ref=_rows_block(total, me, br), send_sem=send2.at[k - 1],
                recv_sem=recv2.at[k - 1], device_id=_peer(k), device_id_type=MESH_ID).start()

    total = lax.empty(part.shape, F32)
    return pl.pallas_call(
        body, name="all_reduce_sum",
        out_shape=(_peer_sems(), _peer_sems(), _hbm_like(total)),
        in_specs=[HBM, HBM, HBM, SEM, SEM, HBM], out_specs=(SEM, SEM, HBM), input_output_aliases={2: 2},
        scratch_shapes=[pltpu.VMEM((N_DEV, br, LANES), F32), pltpu.VMEM((br, LANES), F32),
                        pltpu.SemaphoreType.DMA((N_DEV,))],
        compiler_params=pltpu.CompilerParams(has_side_effects=EFFECT),
    )(_hbm(part), slots, _hbm(total), sends, recvs, after)


def _all_reduce_wait(total, sends, recvs, deps):
    br = total.shape[0] // N_DEV
    nd = len(deps)

    def body(total, send, recv, *_):
        for k in range(1, N_DEV):
            cp = pltpu.make_async_remote_copy(
                src_ref=_rows_block(total, 0, br), dst_ref=_rows_block(total, 0, br), send_sem=send.at[k - 1],
                recv_sem=recv.at[k - 1], device_id=_peer(k), device_id_type=MESH_ID)
            cp.wait_send()
            cp.wait_recv()

    return pl.pallas_call(
        body, name="all_reduce_wait", out_shape=_hbm_like(total),
        in_specs=[HBM, SEM, SEM] + [HBM] * nd, out_specs=HBM, input_output_aliases={0: 0},
        compiler_params=pltpu.CompilerParams(has_side_effects=EFFECT),
    )(total, sends, recvs, *map(_hbm, deps))


def _adam_math(w, g, m, v):
    m = ADAM_B1 * m + (1.0 - ADAM_B1) * g
    v = ADAM_B2 * v + (1.0 - ADAM_B2) * (g * g)
    m_hat = m / (1.0 - ADAM_B1 ** ADAM_STEP)
    v_hat = v / (1.0 - ADAM_B2 ** ADAM_STEP)
    delta = -ADAM_LR * (m_hat / (jnp.sqrt(v_hat) + ADAM_EPS) + ADAM_WD * w)
    return delta, m, v


def _adam_sharded(name, me, g32, recv, w, m, v):
    _, r, c = w.shape
    tr = min(r, 512)
    nblk = r // tr
    assert len(g32) == DEPTH == 2

    def body(me_ref, g0_ref, g1_ref, recv_ref, w_ref, m_ref, v_ref, g_ref, d_ref, nm_ref, nv_ref):
        g = jnp.where(pl.program_id(0) == 0, g0_ref[...], g1_ref[...])
        for k in range(N_DEV - 1):
            g = g + recv_ref[k].astype(F32)
        delta, nm, nv = _adam_math(w_ref[...], g, m_ref[...], v_ref[...])
        g_ref[...] = g
        d_ref[...] = delta
        nm_ref[...] = nm
        nv_ref[...] = nv

    blk = pl.BlockSpec((None, tr, c), lambda l, i, me_ref: (l, i, 0))
    own0 = pl.BlockSpec((None, tr, c), lambda l, i, me_ref: (me_ref[0], jnp.where(l == 0, i, nblk - 1), 0))
    own1 = pl.BlockSpec((None, tr, c), lambda l, i, me_ref: (me_ref[0], jnp.where(l == 1, i, 0), 0))
    shp = jax.ShapeDtypeStruct(w.shape, F32)
    return pl.pallas_call(
        body, name=name,
        grid_spec=pltpu.PrefetchScalarGridSpec(
            num_scalar_prefetch=1, grid=(DEPTH, nblk),
            in_specs=[own0, own1, pl.BlockSpec((None, N_DEV - 1, tr, c), lambda l, i, me_ref: (l, 0, i, 0)),
                      blk, blk, blk],
            out_specs=[blk] * 4),
        out_shape=[shp] * 4,
        compiler_params=_cparams(32, ("arbitrary", "arbitrary")),
    )(me, *[_hbm(a) for a in (*g32, recv, w, m, v)])


def _adam_small(gs, ws, ms, vs):
    n = len(gs)

    def body(*refs):
        g, w, m, v = refs[:n], refs[n:2 * n], refs[2 * n:3 * n], refs[3 * n:4 * n]
        d, nm, nv = refs[4 * n:5 * n], refs[5 * n:6 * n], refs[6 * n:7 * n]
        for a in range(n):
            delta, new_m, new_v = _adam_math(w[a][...], g[a][...], m[a][...], v[a][...])
            d[a][...] = delta
            nm[a][...] = new_m
            nv[a][...] = new_v

    specs = [_full(a.shape) for a in gs]
    shapes = [jax.ShapeDtypeStruct(a.shape, F32) for a in gs]
    outs = pl.pallas_call(
        body, name="adam_small", grid=(1,),
        in_specs=specs * 4, out_specs=specs * 3, out_shape=shapes * 3,
        compiler_params=_cparams(24),
    )(*gs, *ws, *ms, *vs)
    return outs[:n], outs[n:2 * n], outs[2 * n:]


SMALL = ["norm1_g", "sgu_ln_g", "sgu_ln_b", "sgu_w", "sgu_b", "conv_b", "conv_ln_g", "conv_ln_b", "norm2_g",
         "final_g"]


def _pack(arrays):
    flat = jnp.concatenate([a.reshape(-1) for a in arrays])
    pad = (-flat.shape[0]) % (N_DEV * 8 * LANES)
    return jnp.pad(flat, (0, pad)).reshape(-1, LANES)


def _unpack(packed, shapes):
    flat = packed.reshape(-1)
    out, off = [], 0
    for s in shapes:
        size = 1
        for d in s:
            size *= d
        out.append(flat[off:off + size].reshape(s))
        off += size
    return out


def kernel(x, norm1_g, w_in, sgu_ln_g, sgu_ln_b, sgu_w, sgu_b, conv_w, conv_b, conv_ln_g, conv_ln_b, w_out, norm2_g, w_ff1, w_ff2, final_g, loss_target, m_norm1_g, m_w_in, m_sgu_ln_g, m_sgu_ln_b, m_sgu_w, m_sgu_b, m_conv_w, m_conv_b, m_conv_ln_g, m_conv_ln_b, m_w_out, m_norm2_g, m_w_ff1, m_w_ff2, m_final_g, v_norm1_g, v_w_in, v_sgu_ln_g, v_sgu_ln_b, v_sgu_w, v_sgu_b, v_conv_w, v_conv_b, v_conv_ln_g, v_conv_ln_b, v_w_out, v_norm2_g, v_w_ff1, v_w_ff2, v_final_g):
    x2d = x.reshape(T, D)
    tgt = loss_target.reshape(T, D)
    cw_shard = conv_w.reshape(CONV_W, LANES)

    gnames, shards = [], []
    for l in range(DEPTH):
        for k, w in (("w_in", w_in), ("w_out", w_out), ("w_ff1", w_ff1), ("w_ff2", w_ff2)):
            gnames.append(f"{k}{l}")
            shards.append((w, l, BF16))
        if l == 0:
            gnames.insert(1, "conv_w")
            shards.insert(1, (cw_shard, None, F32))
    sends, recvs, lands = _gather_start(shards)
    gidx = {k: i for i, k in enumerate(gnames)}

    passed = {}

    def pass_on(ks, after):
        idx = [gidx[k] for k in ks]
        s2, r2, ld = _gather_pass_on("gather_pass_on_" + "_".join(ks), [lands[i] for i in idx],
                                     [sends[i] for i in idx], [recvs[i] for i in idx], after, sent=idx == [0])
        for j, k in enumerate(ks):
            passed[k] = (s2[j], r2[j], ld[j])

    def gathered(ks, after):
        idx = [gidx[k] for k in ks]
        return _gather_wait("gather_wait_" + "_".join(ks), [passed[k][2] for k in ks], [sends[i] for i in idx],
                            [recvs[i] for i in idx], [passed[k][0] for k in ks], [passed[k][1] for k in ks], after,
                            sent=idx == [0])

    saved = []
    xl = x2d
    cw_full = None
    for l in range(DEPTH):
        g1 = norm1_g[l].reshape(1, D)
        g2 = norm2_g[l].reshape(1, D)
        lg, lb = sgu_ln_g[l].reshape(1, D_A), sgu_ln_b[l].reshape(1, D_A)
        bst = sgu_b[l].T
        cb = conv_b[l].reshape(1, D_B)
        clg, clb = conv_ln_g[l].reshape(1, D_B), conv_ln_b[l].reshape(1, D_B)
        first = [f"w_in{l}"] if l == 0 else [f"w_in{l}", f"w_out{l}"]
        pass_on(first, xl)
        wg_in, *rest = gathered(first, xl)
        h, proj = _fwd_in(xl, g1, wg_in)
        if l == 0:
            pass_on(["conv_w", "w_out0"], h)
            cw_g, wout = gathered(["conv_w", "w_out0"], proj)
            cw_full = cw_g.reshape(N_DEV, DEPTH, CONV_W, D_B // N_DEV).transpose(1, 2, 0, 3).reshape(
                DEPTH, CONV_W, D_B)
        else:
            (wout,) = rest
            pass_on([f"w_ff1{l}"], proj)
        wout = wout.reshape(D, D)
        mix, c, g, x1, h2 = _mixer_fwd(proj, lg, lb, sgu_w[l], bst, cw_full[l], cb, clg, clb, wout, xl, g2)
        pass_on([f"w_ff1{l}"] if l == 0 else [f"w_ff2{l}"], h2)
        (wg_ff1,) = gathered([f"w_ff1{l}"], h2)
        r = _fwd_ff1(h2, wg_ff1)
        if l == 0:
            pass_on(["w_ff20"], r)
        (w2,) = gathered([f"w_ff2{l}"], r)
        w2 = w2.reshape(D_FF, D)
        saved.append(dict(x=xl, h=h, proj=proj, mix=mix, c=c, g=g, x1=x1, h2=h2, r=r, wg_in=wg_in, wout=wout,
                          wg_ff1=wg_ff1, w2=w2, g1=g1, g2=g2, lg=lg, lb=lb, bst=bst, clg=clg, clb=clb))
        if l + 1 < DEPTH:
            xl = _fwd_ff2(r, w2, x1)
        else:
            dx, dxb, loss_part, d_final_g = _fwd_ff2_loss(r, w2, x1, final_g.reshape(1, D), tgt)

    names = ["w_in", "w_out", "w_ff1", "w_ff2"]
    block = dict(w_in=(D, IN_COLS // N_DEV), w_out=(D // N_DEV, D), w_ff1=(D, D_FF // N_DEV), w_ff2=(D_FF // N_DEV, D))
    land = {k: lax.empty((DEPTH, N_DEV - 1) + block[k], BF16) for k in names}
    big32 = {k: [None] * DEPTH for k in names}
    big16 = {k: [None] * DEPTH for k in names}
    ssend = {k: [None] * DEPTH for k in names}
    srecv = {k: [None] * DEPTH for k in names}

    def send_grads(l, grads, after=None):
        ks = list(grads)
        for k in ks:
            big32[k][l] = grads[k][0].reshape((N_DEV,) + block[k])
        sends, recvs, g16s, lands = _scatter_start(
            "scatter_start_" + "_".join(ks) + str(l), [grads[k][1].reshape((N_DEV,) + block[k]) for k in ks],
            [land[k] for k in ks], l, after)
        for i, k in enumerate(ks):
            ssend[k][l], srecv[k][l], big16[k][l], land[k] = sends[i], recvs[i], g16s[i], lands[i]

    rep = dict(norm1_g=(norm1_g, m_norm1_g, v_norm1_g), sgu_ln_g=(sgu_ln_g, m_sgu_ln_g, v_sgu_ln_g),
               sgu_ln_b=(sgu_ln_b, m_sgu_ln_b, v_sgu_ln_b), sgu_w=(sgu_w, m_sgu_w, v_sgu_w),
               sgu_b=(sgu_b, m_sgu_b, v_sgu_b), conv_b=(conv_b, m_conv_b, v_conv_b),
               conv_ln_g=(conv_ln_g, m_conv_ln_g, v_conv_ln_g), conv_ln_b=(conv_ln_b, m_conv_ln_b, v_conv_ln_b),
               norm2_g=(norm2_g, m_norm2_g, v_norm2_g), final_g=(final_g, m_final_g, v_final_g))

    def small_parts():
        parts = []
        for k in SMALL:
            if k == "final_g":
                parts.append(d_final_g.reshape(rep[k][0].shape))
            else:
                parts.append(jnp.stack([small[l][k].reshape(rep[k][0].shape[1:]) for l in range(DEPTH)]))
        parts.append(jnp.stack(d_conv_w))
        parts.append(loss_part[0, 0:1])
        return parts

    small = {}
    d_conv_w = [None] * DEPTH
    for l in reversed(range(DEPTH)):
        s = saved[l]
        send_grads(l, dict(w_ff2=_wgrad("wgrad_ff2", s["r"], dxb, D, square_a=True)))
        df1, dx1, dx1b, dg2 = _bwd_mlp(dxb, dx, s["w2"], s["r"], s["wg_ff1"], s["x1"], s["g2"], big16["w_ff2"][l])
        send_grads(l, dict(w_ff1=_wgrad("wgrad_ff1", s["h2"], df1, D_FF // N_DEV),
                           w_out=_wgrad("wgrad_out", s["mix"], dx1b, D)))
        dpa, dc, dlg, dlb, dwm, dbs, dcb, dclg, dclb = _mixer_bwd_a(
            dx1b, s["wout"], s["proj"], s["c"], s["lg"], s["lb"], sgu_w[l], s["bst"], s["clg"], s["clb"],
            big16["w_out"][l])
        dproj, d_conv_w[l], dx, dxb, dg1 = _mixer_bwd_b(dc, s["g"], s["proj"], cw_full[l], dpa, s["wg_in"], s["x"],
                                                         s["g1"], dx1)
        small[l] = dict(norm1_g=dg1, sgu_ln_g=dlg, sgu_ln_b=dlb, sgu_w=dwm, sgu_b=dbs, conv_b=dcb, conv_ln_g=dclg,
                        conv_ln_b=dclb, norm2_g=dg2)
        if l > 0:
            send_grads(l, dict(w_in=_wgrad("wgrad_in", s["h"], dproj, IN_COLS // N_DEV)))
        else:
            parts = small_parts()
            packed = _pack(parts)
            ar_sends, ar_recvs, ar_slots = _all_reduce_start(packed)
            g_in = _wgrad("wgrad_in", s["h"], dproj, IN_COLS // N_DEV, dep=ar_slots)
            ar_sends, ar_recvs, total = _all_reduce_sum(packed, ar_slots, ar_sends, ar_recvs, _hbm(g_in[1]))
            send_grads(l, dict(w_in=g_in), total)
    grad_x = dx.reshape(1, T, D)

    me_block = _flat(*_coords()).astype(jnp.int32).reshape(1)
    wmv = dict(w_in=(w_in, m_w_in, v_w_in), w_out=(w_out, m_w_out, v_w_out), w_ff1=(w_ff1, m_w_ff1, v_w_ff1),
               w_ff2=(w_ff2, m_w_ff2, v_w_ff2))
    res = {}

    def update(tag, ks, after):
        recv = _scatter_wait("scatter_wait_" + tag, [g for k in ks for g in big16[k]], [land[k] for k in ks],
                             [q for k in ks for q in ssend[k]], [q for k in ks for q in srecv[k]], after)
        for k, rk in zip(ks, recv):
            res[k] = _adam_sharded("adam_" + k, me_block, big32[k], rk, *wmv[k])

    shapes = [p.shape for p in parts]
    update("early", ["w_ff2", "w_ff1", "w_out"], big16["w_in"][0])
    packed_sum = _all_reduce_wait(total, ar_sends, ar_recvs, [res[k][0] for k in ("w_ff2", "w_ff1", "w_out")])
    update("late", ["w_in"], packed_sum)
    summed = _unpack(packed_sum, shapes)
    loss = summed[-1][0]
    me = _flat(*_coords())
    g_conv_w = lax.dynamic_slice_in_dim(summed[-2], me * (D_B // N_DEV), D_B // N_DEV, axis=2)
    g_small = summed[:-2] + [g_conv_w]
    keys = SMALL + ["conv_w"]
    rep["conv_w"] = (conv_w, m_conv_w, v_conv_w)
    def two_d(a):
        return a.reshape(1, -1) if a.ndim == 1 else a

    d_s, nm_s, nv_s = _adam_small([two_d(g) for g in g_small], *[[two_d(rep[k][i]) for k in keys] for i in range(3)])
    for i, k in enumerate(keys):
        shape = rep[k][0].shape
        res[k] = (g_small[i], d_s[i].reshape(shape), nm_s[i].reshape(shape), nv_s[i].reshape(shape))

    order = ["norm1_g", "w_in", "sgu_ln_g", "sgu_ln_b", "sgu_w", "sgu_b", "conv_w", "conv_b", "conv_ln_g",
             "conv_ln_b", "w_out", "norm2_g", "w_ff1", "w_ff2", "final_g"]
    return (loss, grad_x, *[res[k][0] for k in order], *[res[k][1] for k in order],
            *[res[k][2] for k in order], *[res[k][3] for k in order])
```

```python
import functools

import jax
import jax.numpy as jnp
from jax import lax
from jax.experimental import pallas as pl
from jax.experimental.pallas import tpu as pltpu

F32 = jnp.float32
BF16 = jnp.bfloat16

N_DEV = 8
DEPTH = 2
T = 4096
D = 1024
D_A = 512
D_B = 512
CHUNK = 128
H_A = 4
H_B = 4
CONV_W = 31
HALO = 32
D_FF = 4096
IN_COLS = 2048
EPS = 1e-6

ADAM_LR = 0.001
ADAM_B1 = 0.9
ADAM_B2 = 0.999
ADAM_EPS = 1e-08
ADAM_WD = 0.01
ADAM_STEP = 10

TM = 512
TM_FWD = 1024
TT_WGRAD = 4096
TM_SGU = 512
RB = 64
CONV_CHAINS = 2
LANES = 128
MIB = 1024 * 1024
SCOPED_VMEM_MIB = 60

SQRT_HALF = 0.7071067811865476
INV_SQRT_2PI = 0.3989422804014327

MESH_ID = pl.DeviceIdType.MESH
HBM = pl.BlockSpec(memory_space=pltpu.HBM)
SEM = pl.BlockSpec(memory_space=pltpu.SEMAPHORE)
EFFECT = pltpu.SideEffectType.DATAFLOW_SIDE_EFFECTING


def _cparams(vmem_mib, sem=("arbitrary",)):
    assert vmem_mib <= SCOPED_VMEM_MIB
    return pltpu.CompilerParams(dimension_semantics=sem, vmem_limit_bytes=SCOPED_VMEM_MIB * MIB)


def _full(shape):
    return pl.BlockSpec(shape, lambda *_: (0,) * len(shape))


def _rows(tm, cols):
    return pl.BlockSpec((tm, cols), lambda i: (i, 0))


def _gelu(x):
    cdf = 0.5 * (1.0 + lax.erf(x * SQRT_HALF))
    return x * cdf, cdf


def _gelu_grad(x, cdf):
    return cdf + x * (INV_SQRT_2PI * jnp.exp(-0.5 * x * x))


def _sigmoid(x):
    return 1.0 / (1.0 + jnp.exp(-x))


def _ln(x):
    mu = jnp.mean(x, axis=-1, keepdims=True)
    xc = x - mu
    rstd = lax.rsqrt(jnp.mean(xc * xc, axis=-1, keepdims=True) + EPS)
    return xc * rstd, rstd


def _ln_bwd(dyh, xhat, rstd):
    return rstd * (dyh - jnp.mean(dyh, axis=-1, keepdims=True) - xhat * jnp.mean(dyh * xhat, axis=-1, keepdims=True))


def _rms(x):
    return lax.rsqrt(jnp.mean(x * x, axis=-1, keepdims=True) + EPS)


def _rms_bwd(dh, x, r, g):
    n = x * r
    dn = dh * g
    dx = r * (dn - n * jnp.mean(dn * n, axis=-1, keepdims=True))
    return dx, jnp.sum(dh * n, axis=0, keepdims=True)


def _dot(a, b):
    return jnp.dot(a, b, preferred_element_type=F32)


def _dot_nt(a, b):
    return lax.dot_general(a, b, (((1,), (1,)), ((), ())), preferred_element_type=F32)


def _dot_tn(a, b):
    return lax.dot_general(a, b, (((0,), (0,)), ((), ())), preferred_element_type=F32)


def _tril_mask():
    r = lax.broadcasted_iota(jnp.int32, (CHUNK, CHUNK), 0)
    c = lax.broadcasted_iota(jnp.int32, (CHUNK, CHUNK), 1)
    return r >= c


def _fwd_in(x, g1, wg):
    bn = wg.shape[2]

    def body(x_ref, g_ref, w_ref, h_ref, p_ref):
        xv = x_ref[...]
        h = (xv * _rms(xv) * g_ref[...]).astype(BF16)
        h_ref[...] = h
        for j in range(N_DEV):
            p_ref[:, j * bn:(j + 1) * bn] = _dot(h, w_ref[j])

    return pl.pallas_call(
        body, name="fwd_in", grid=(T // TM_FWD,),
        in_specs=[_rows(TM_FWD, D), _full((1, D)), _full(wg.shape)],
        out_specs=[_rows(TM_FWD, D), _rows(TM_FWD, IN_COLS)],
        out_shape=[jax.ShapeDtypeStruct((T, D), BF16), jax.ShapeDtypeStruct((T, IN_COLS), F32)],
        compiler_params=_cparams(32),
    )(*map(_hbm, (x, g1, wg)))


def _fill_shift_buffer(sh_ref, row0, value):
    for q in range(sh_ref.shape[1]):
        sh_ref[0, q, row0:row0 + value.shape[0], :] = value[:, q * LANES:(q + 1) * LANES]


def _build_shifts(sh_ref):
    rows = sh_ref.shape[2]
    for p in range(1, 8):
        for q in range(sh_ref.shape[1]):
            sh_ref[p, q, 0:rows - 8, :] = sh_ref[0, q, p:p + rows - 8, :]


def _shifted(sh_ref, q, base, off):
    start = base + (off - off % 8)
    if not isinstance(start, int):
        start = pl.multiple_of(start, 8)
    return sh_ref[off % 8, q, pl.ds(start, RB), :]


def _conv_taps(sh_ref, w_ref, q, base, first_tap_row, step):
    cols = slice(q * LANES, (q + 1) * LANES)
    acc = [jnp.zeros((RB, LANES), F32) for _ in range(CONV_CHAINS)]
    for k in range(CONV_W):
        term = _shifted(sh_ref, q, base, first_tap_row + step * k) * w_ref[k:k + 1, cols]
        acc[k % CONV_CHAINS] = acc[k % CONV_CHAINS] + term
    return functools.reduce(lambda a, b: a + b, acc)


def _mixer_fwd(proj, lg, lb, wm, bst, cw, cb, clg, clb, wout, x, g2):
    tm = TM_SGU
    hb = tm // HALO

    def body(p_ref, ph_ref, lg_ref, lb_ref, wm_ref, bs_ref, cw_ref, cb_ref, clg_ref, clb_ref, wo_ref, x_ref, g2_ref,
             mix_ref, c_ref, g_ref, x1_ref, h2_ref, gbuf):
        i = pl.program_id(0)
        u, _ = _gelu(p_ref[:, 0:D_A])
        vg, _ = _gelu(p_ref[:, D_A:2 * D_A])
        xhat, _ = _ln(vg)
        v = (xhat * lg_ref[...] + lb_ref[...]).astype(BF16)
        mask = _tril_mask()
        for h in range(H_A):
            hc = slice(h * CHUNK, (h + 1) * CHUNK)
            wmh = jnp.where(mask, wm_ref[h], 0.0).astype(BF16)
            for c in range(tm // CHUNK):
                rc = slice(c * CHUNK, (c + 1) * CHUNK)
                mixed = _dot(wmh, v[rc, hc]) + bs_ref[:, h:h + 1]
                mix_ref[rc, hc] = (u[rc, hc] * mixed).astype(BF16)
        x1_ref[...] = x_ref[...] + _dot(mix_ref[:, 0:D_A], wo_ref[0:D_A, :])

        g = p_ref[:, 2 * D_A:2 * D_A + D_B] * _sigmoid(p_ref[:, 2 * D_A + D_B:IN_COLS])
        g_ref[...] = g
        gh = ph_ref[:, 0:D_B] * _sigmoid(ph_ref[:, D_B:2 * D_B])
        _fill_shift_buffer(gbuf, 0, jnp.where(i > 0, gh, 0.0))
        _fill_shift_buffer(gbuf, HALO, g)
        _build_shifts(gbuf)
        for q in range(H_B):
            cols = slice(q * LANES, (q + 1) * LANES)
            for rb in range(tm // RB):
                acc = _conv_taps(gbuf, cw_ref, q, rb * RB, HALO - (CONV_W - 1), 1)
                c_ref[rb * RB:(rb + 1) * RB, cols] = acc + cb_ref[:, cols]
        for q in range(H_B):
            cols = slice(q * LANES, (q + 1) * LANES)
            chat, _ = _ln(c_ref[:, cols])
            z = chat * clg_ref[:, cols] + clb_ref[:, cols]
            mix_ref[:, D_A + q * LANES:D_A + (q + 1) * LANES] = (z * _sigmoid(z)).astype(BF16)

        x1 = x1_ref[...] + _dot(mix_ref[:, D_A:D], wo_ref[D_A:D, :])
        x1_ref[...] = x1
        h2_ref[...] = (x1 * _rms(x1) * g2_ref[...]).astype(BF16)

    vec = _full((1, D_A))
    return pl.pallas_call(
        body, name="mixer_fwd", grid=(T // tm,),
        in_specs=[_rows(tm, IN_COLS),
                  pl.BlockSpec((HALO, 2 * D_B), lambda i: (jnp.maximum(i * hb - 1, 0), 1)),
                  vec, vec, _full((H_A, CHUNK, CHUNK)), _full((CHUNK, H_A)),
                  _full((CONV_W, D_B)), vec, vec, vec,
                  pl.BlockSpec((D, D), lambda i: (0, 0), pipeline_mode=pl.Buffered(1)), _rows(tm, D), _full((1, D))],
        out_specs=[_rows(tm, D), _rows(tm, D_B), _rows(tm, D_B), _rows(tm, D), _rows(tm, D)],
        out_shape=[jax.ShapeDtypeStruct((T, D), BF16), jax.ShapeDtypeStruct((T, D_B), F32),
                   jax.ShapeDtypeStruct((T, D_B), F32), jax.ShapeDtypeStruct((T, D), F32),
                   jax.ShapeDtypeStruct((T, D), BF16)],
        scratch_shapes=[pltpu.VMEM((8, H_B, HALO + tm, LANES), F32)],
        compiler_params=_cparams(40),
    )(*map(_hbm, (proj, proj, lg, lb, wm, bst, cw, cb, clg, clb, wout, x, g2)))


def _fwd_ff1(h2, wg):
    bn = wg.shape[2]

    def body(h_ref, w_ref, r_ref):
        h = h_ref[...]
        for j in range(N_DEV):
            r_ref[:, j * bn:(j + 1) * bn] = jnp.maximum(_dot(h, w_ref[j]), 0.0).astype(BF16)

    return pl.pallas_call(
        body, name="fwd_ff1", grid=(T // TM_FWD,),
        in_specs=[_rows(TM_FWD, D), _full(wg.shape)],
        out_specs=_rows(TM_FWD, D_FF),
        out_shape=jax.ShapeDtypeStruct((T, D_FF), BF16),
        compiler_params=_cparams(48),
    )(h2, wg)


def _fwd_ff2(r, w2, x1):
    def body(r_ref, w_ref, x_ref, o_ref):
        rv = r_ref[...]
        o_ref[...] = x_ref[...] + _dot(rv * rv, w_ref[...])

    return pl.pallas_call(
        body, name="fwd_ff2", grid=(T // TM_FWD,),
        in_specs=[_rows(TM_FWD, D_FF), _full((D_FF, D)), _rows(TM_FWD, D)],
        out_specs=_rows(TM_FWD, D),
        out_shape=jax.ShapeDtypeStruct((T, D), F32),
        compiler_params=_cparams(48),
    )(r, w2, x1)


def _fwd_ff2_loss(r, w2, x1, gf, tgt):
    def body(r_ref, w_ref, x_ref, g_ref, t_ref, dx_ref, dxb_ref, loss_ref, dg_ref):
        i = pl.program_id(0)

        @pl.when(i == 0)
        def _():
            loss_ref[...] = jnp.zeros(loss_ref.shape, F32)
            dg_ref[...] = jnp.zeros(dg_ref.shape, F32)

        rv = r_ref[...]
        xv = x_ref[...] + _dot(rv * rv, w_ref[...])
        rn = _rms(xv)
        diff = xv * rn * g_ref[...] - t_ref[...]
        loss_ref[...] += 0.5 * jnp.sum(jnp.mean(diff * diff, axis=-1, keepdims=True), axis=0, keepdims=True)
        dx, dg = _rms_bwd(diff * (1.0 / D), xv, rn, g_ref[...])
        dx_ref[...] = dx
        dxb_ref[...] = dx.astype(BF16)
        dg_ref[...] += dg

    return pl.pallas_call(
        body, name="fwd_ff2_loss", grid=(T // TM,),
        in_specs=[_rows(TM, D_FF), pl.BlockSpec((D_FF, D), lambda i: (0, 0), pipeline_mode=pl.Buffered(1)),
                  _rows(TM, D), _full((1, D)), _rows(TM, D)],
        out_specs=[_rows(TM, D), _rows(TM, D), _full((8, LANES)), _full((1, D))],
        out_shape=[jax.ShapeDtypeStruct((T, D), F32), jax.ShapeDtypeStruct((T, D), BF16),
                   jax.ShapeDtypeStruct((8, LANES), F32), jax.ShapeDtypeStruct((1, D), F32)],
        compiler_params=_cparams(40),
    )(*map(_hbm, (r, w2, x1, gf, tgt)))


def _bwd_mlp(dxb, dres, w2, r, wg1, x1, g2, dep):
    bn = wg1.shape[2]

    def body(d_ref, dres_ref, w2_ref, r_ref, w1_ref, x_ref, g_ref, dep_ref, df1_ref, dx_ref, dxb_ref, dg_ref):
        i = pl.program_id(0)

        @pl.when(i == 0)
        def _():
            dg_ref[...] = jnp.zeros(dg_ref.shape, F32)

        d = d_ref[...]
        dh = jnp.zeros((TM, D), F32)
        for j in range(N_DEV):
            cols = slice(j * bn, (j + 1) * bn)
            df1 = (2.0 * r_ref[:, cols].astype(F32) * _dot_nt(d, w2_ref[cols, :])).astype(BF16)
            df1_ref[:, cols] = df1
            dh = dh + _dot_nt(df1, w1_ref[j])
        xv = x_ref[...]
        dxn, dg = _rms_bwd(dh, xv, _rms(xv), g_ref[...])
        dx = dres_ref[...] + dxn
        dx_ref[...] = dx
        dxb_ref[...] = dx.astype(BF16)
        dg_ref[...] += dg

    once = dict(pipeline_mode=pl.Buffered(1))
    return pl.pallas_call(
        body, name="bwd_mlp", grid=(T // TM,),
        in_specs=[_rows(TM, D), _rows(TM, D), pl.BlockSpec((D_FF, D), lambda i: (0, 0), **once), _rows(TM, D_FF),
                  pl.BlockSpec(wg1.shape, lambda i: (0, 0, 0), **once), _rows(TM, D), _full((1, D)), HBM],
        out_specs=[_rows(TM, D_FF), _rows(TM, D), _rows(TM, D), _full((1, D))],
        out_shape=[jax.ShapeDtypeStruct((T, D_FF), BF16), jax.ShapeDtypeStruct((T, D), F32),
                   jax.ShapeDtypeStruct((T, D), BF16), jax.ShapeDtypeStruct((1, D), F32)],
        compiler_params=_cparams(56),
    )(*map(_hbm, (dxb, dres, w2, r, wg1, x1, g2, dep)))


def _mixer_bwd_a(dxb, wout, proj, c, lg, lb, wm, bst, clg, clb, mix, dep):
    tm = TM_SGU
    n_tiles = T // tm

    def body(dx_ref, wo_ref, p_ref, c_ref, lg_ref, lb_ref, wm_ref, bs_ref, clg_ref, clb_ref, mix_ref, dep_ref,
             dpa_ref, dc_ref, dlg_ref, dlb_ref, dwm_ref, dbs_ref, dcb_ref, dclg_ref, dclb_ref, dwo_ref, dwob_ref,
             dv_buf, db_acc):
        i = pl.program_id(0)

        @pl.when(i == 0)
        def _():
            for ref in (dlg_ref, dlb_ref, dwm_ref, dbs_ref, dcb_ref, dclg_ref, dclb_ref, db_acc, dwo_ref):
                ref[...] = jnp.zeros(ref.shape, F32)

        dwo_ref[0] += _dot_tn(mix_ref[...], dx_ref[...])
        dmix = _dot_nt(dx_ref[...], wo_ref[...])
        ua = p_ref[:, 0:D_A]
        va = p_ref[:, D_A:2 * D_A]
        u, cdf_u = _gelu(ua)
        vg, cdf_v = _gelu(va)
        xhat, rstd = _ln(vg)
        v = (xhat * lg_ref[...] + lb_ref[...]).astype(BF16)
        mask = _tril_mask()
        for h in range(H_A):
            hc = slice(h * CHUNK, (h + 1) * CHUNK)
            wmh = jnp.where(mask, wm_ref[h], 0.0).astype(BF16)
            for cidx in range(tm // CHUNK):
                rc = slice(cidx * CHUNK, (cidx + 1) * CHUNK)
                vb = v[rc, hc]
                mixed = _dot(wmh, vb) + bs_ref[:, h:h + 1]
                da = dmix[rc, hc]
                dpa_ref[rc, hc] = (da * mixed * _gelu_grad(ua[rc, hc], cdf_u[rc, hc])).astype(BF16)
                dmixed = da * u[rc, hc]
                dmb = dmixed.astype(BF16)
                dv_buf[rc, hc] = _dot_tn(wmh, dmb)
                dwm_ref[h] += _dot_nt(dmb, vb)
                db_acc[:, hc] += dmixed
        dv = dv_buf[...]
        dlb_ref[...] += jnp.sum(dv, axis=0, keepdims=True)
        dlg_ref[...] += jnp.sum(dv * xhat, axis=0, keepdims=True)
        dvg = _ln_bwd(dv * lg_ref[...], xhat, rstd)
        dpa_ref[:, D_A:2 * D_A] = (dvg * _gelu_grad(va, cdf_v)).astype(BF16)

        for q in range(H_B):
            cols = slice(q * LANES, (q + 1) * LANES)
            chat, crstd = _ln(c_ref[:, cols])
            z = chat * clg_ref[:, cols] + clb_ref[:, cols]
            sg = _sigmoid(z)
            dz = dmix[:, D_A + q * LANES:D_A + (q + 1) * LANES] * (sg * (1.0 + z * (1.0 - sg)))
            dclb_ref[:, cols] += jnp.sum(dz, axis=0, keepdims=True)
            dclg_ref[:, cols] += jnp.sum(dz * chat, axis=0, keepdims=True)
            dc = _ln_bwd(dz * clg_ref[:, cols], chat, crstd)
            dc_ref[:, cols] = dc
            dcb_ref[:, cols] += jnp.sum(dc, axis=0, keepdims=True)

        @pl.when(i == n_tiles - 1)
        def _():
            for h in range(H_A):
                hc = slice(h * CHUNK, (h + 1) * CHUNK)
                dwm_ref[h] = jnp.where(mask, dwm_ref[h], 0.0)
                dbs_ref[h:h + 1, :] = jnp.sum(db_acc[:, hc].T, axis=0, keepdims=True)
            dwob_ref[...] = dwo_ref[...].astype(BF16)

    vec = _full((1, D_A))
    vshape = jax.ShapeDtypeStruct((1, D_A), F32)
    return pl.pallas_call(
        body, name="mixer_bwd_a", grid=(n_tiles,),
        in_specs=[_rows(tm, D), _full((D, D)), _rows(tm, IN_COLS), _rows(tm, D_B), vec, vec,
                  _full((H_A, CHUNK, CHUNK)), _full((CHUNK, H_A)), vec, vec, _rows(tm, D), HBM],
        out_specs=[_rows(tm, 2 * D_A), _rows(tm, D_B), vec, vec, _full((H_A, CHUNK, CHUNK)),
                   _full((H_A, CHUNK)), vec, vec, vec, _full((1, D, D)), _full((1, D, D))],
        out_shape=[jax.ShapeDtypeStruct((T, 2 * D_A), BF16), jax.ShapeDtypeStruct((T, D_B), F32), vshape, vshape,
                   jax.ShapeDtypeStruct((H_A, CHUNK, CHUNK), F32), jax.ShapeDtypeStruct((H_A, CHUNK), F32),
                   vshape, vshape, vshape, jax.ShapeDtypeStruct((1, D, D), F32),
                   jax.ShapeDtypeStruct((1, D, D), BF16)],
        scratch_shapes=[pltpu.VMEM((tm, D_A), F32), pltpu.VMEM((CHUNK, D_A), F32)],
        compiler_params=_cparams(48),
    )(*map(_hbm, (dxb, wout, proj, c, lg, lb, wm, bst, clg, clb, mix, dep)))


def _mixer_bwd_b(dc, g, proj, cw, dpa, wg_in, x, g1, dres):
    tm = TM_SGU
    n_tiles = T // tm
    hb = tm // HALO
    bn = wg_in.shape[2]

    def body(dc_ref, dch_ref, g_ref, p_ref, cw_ref, dpa_ref, win_ref, x_ref, g1_ref, dres_ref,
             dp_ref, dcw_ref, dx_ref, dxb_ref, dg1_ref, dcbuf, dwacc):
        i = pl.program_id(0)

        @pl.when(i == 0)
        def _():
            dwacc[...] = jnp.zeros(dwacc.shape, F32)
            dg1_ref[...] = jnp.zeros(dg1_ref.shape, F32)

        _fill_shift_buffer(dcbuf, 0, dc_ref[...])
        _fill_shift_buffer(dcbuf, tm, jnp.where(i < n_tiles - 1, dch_ref[...], 0.0))
        _build_shifts(dcbuf)
        dp_ref[:, 0:2 * D_A] = dpa_ref[...]
        for q in range(H_B):
            cols = slice(q * LANES, (q + 1) * LANES)

            def row_block(rb, carry, q=q, cols=cols):
                base = pl.multiple_of(rb * RB, RB)
                rows = pl.ds(base, RB)
                gv = g_ref[rows, cols]
                acc = [jnp.zeros((RB, LANES), F32) for _ in range(CONV_CHAINS)]
                parts = []
                for k in range(CONV_W):
                    xk = _shifted(dcbuf, q, base, CONV_W - 1 - k)
                    acc[k % CONV_CHAINS] = acc[k % CONV_CHAINS] + xk * cw_ref[k:k + 1, cols]
                    parts.append(jnp.sum((gv * xk).reshape(RB // 8, 8, LANES), axis=0))
                dg = functools.reduce(lambda a, b: a + b, acc)
                val = p_ref[rows, cols]
                sg = _sigmoid(p_ref[rows, D_B + q * LANES:D_B + (q + 1) * LANES])
                dp_ref[rows, 2 * D_A + q * LANES:2 * D_A + (q + 1) * LANES] = (dg * sg).astype(BF16)
                dp_ref[rows, 2 * D_A + D_B + q * LANES:2 * D_A + D_B + (q + 1) * LANES] = (
                    dg * val * sg * (1.0 - sg)).astype(BF16)
                for k in range(CONV_W):
                    dwacc[k * 8:(k + 1) * 8, cols] += parts[k]
                return carry

            lax.fori_loop(0, tm // RB, row_block, 0)

        dh = jnp.zeros((tm, D), F32)
        for j in range(N_DEV):
            dh = dh + _dot_nt(dp_ref[:, j * bn:(j + 1) * bn], win_ref[j])
        xv = x_ref[...]
        dxn, dg = _rms_bwd(dh, xv, _rms(xv), g1_ref[...])
        dx = dres_ref[...] + dxn
        dx_ref[...] = dx
        dxb_ref[...] = dx.astype(BF16)
        dg1_ref[...] += dg

        @pl.when(i == n_tiles - 1)
        def _():
            for k in range(CONV_W):
                dcw_ref[k:k + 1, :] = jnp.sum(dwacc[k * 8:(k + 1) * 8, :], axis=0, keepdims=True)

    return pl.pallas_call(
        body, name="mixer_bwd_b", grid=(n_tiles,),
        in_specs=[_rows(tm, D_B),
                  pl.BlockSpec((HALO, D_B), lambda i: (jnp.minimum((i + 1) * hb, T // HALO - 1), 0)),
                  _rows(tm, D_B),
                  pl.BlockSpec((tm, 2 * D_B), lambda i: (i, 1)),
                  _full((CONV_W, D_B)), _rows(tm, 2 * D_A),
                  pl.BlockSpec(wg_in.shape, lambda i: (0, 0, 0), pipeline_mode=pl.Buffered(1)),
                  _rows(tm, D), _full((1, D)), _rows(tm, D)],
        out_specs=[_rows(tm, IN_COLS), _full((CONV_W, D_B)), _rows(tm, D), _rows(tm, D), _full((1, D))],
        out_shape=[jax.ShapeDtypeStruct((T, IN_COLS), BF16), jax.ShapeDtypeStruct((CONV_W, D_B), F32),
                   jax.ShapeDtypeStruct((T, D), F32), jax.ShapeDtypeStruct((T, D), BF16),
                   jax.ShapeDtypeStruct((1, D), F32)],
        scratch_shapes=[pltpu.VMEM((8, H_B, tm + HALO, LANES), F32), pltpu.VMEM((CONV_W * 8, D_B), F32)],
        compiler_params=_cparams(40),
    )(*map(_hbm, (dc, dc, g, proj, cw, dpa, wg_in, x, g1, dres)))


def _wgrad(name, a, g, bn, square_a=False, dep=None):
    deps = list(dep or [])
    k = a.shape[1]
    n = g.shape[1]
    tk = min(k, 1024)
    tn = min(n, max(bn, 1024))
    nsub = tn // bn
    tt = TT_WGRAD
    nt = T // tt

    def body(a_ref, g_ref, *refs):
        o_ref, ob_ref = refs[len(deps):]
        t = pl.program_id(2)

        @pl.when(t == 0)
        def _():
            o_ref[...] = jnp.zeros(o_ref.shape, F32)

        av = a_ref[...]
        if square_a:
            av = av * av
        for s in range(nsub):
            o_ref[s] += _dot_tn(av, g_ref[:, s * bn:(s + 1) * bn])

        @pl.when(t == nt - 1)
        def _():
            ob_ref[...] = o_ref[...].astype(BF16)

    ospec = pl.BlockSpec((nsub, tk, bn), lambda ki, ni, ti: (ni, ki, 0))
    return pl.pallas_call(
        body, name=name, grid=(k // tk, n // tn, nt),
        in_specs=[pl.BlockSpec((tt, tk), lambda ki, ni, ti: (ti, ki)),
                  pl.BlockSpec((tt, tn), lambda ki, ni, ti: (ti, ni))] + [HBM] * len(deps),
        out_specs=[ospec, ospec],
        out_shape=[jax.ShapeDtypeStruct((n // bn, k, bn), F32), jax.ShapeDtypeStruct((n // bn, k, bn), BF16)],
        compiler_params=_cparams(40, ("arbitrary", "arbitrary", "arbitrary")),
    )(a, g, *map(_hbm, deps))


def _coords():
    return lax.axis_index("x"), lax.axis_index("y"), lax.axis_index("c")


def _flat(x, y, c):
    return 4 * x + 2 * y + c


def _peer(k):
    x, y, c = _coords()
    return (x ^ ((k >> 2) & 1), y ^ ((k >> 1) & 1), c ^ (k & 1))


def _hbm(a):
    return pltpu.with_memory_space_constraint(a, pltpu.HBM)


def _hbm_like(a):
    return pltpu.HBM(a.shape, a.dtype)


def _peer_sems():
    return pltpu.SemaphoreType.DMA((N_DEV - 1,))


SIBLING = 1
CHIP_PEERS = (2, 4, 6)
FIRST_PEERS = (SIBLING,) + CHIP_PEERS


def _gather_start(shards):
    n = len(shards)
    shapes = [(s.shape if l is None else s.shape[1:]) for s, l, _ in shards]

    def body(*refs):
        ins, send, recv, lnd = refs[:n], refs[n:2 * n], refs[2 * n:3 * n], refs[3 * n:4 * n]
        stage_in, stage_out = refs[4 * n:5 * n], refs[5 * n:6 * n]
        in_sems, out_sems = refs[6 * n], refs[6 * n + 1]
        me = _flat(*_coords())
        srcs = [ins[a] if shards[a][1] is None else ins[a].at[shards[a][1]] for a in range(n)]
        loads = [pltpu.make_async_copy(srcs[a], stage_in[a], in_sems.at[a]) for a in range(n)]
        stores = [pltpu.make_async_copy(stage_out[a], lnd[a].at[me], out_sems.at[a]) for a in range(n)]

        def send_block(a):
            stores[a].wait()
            for j, k in enumerate(FIRST_PEERS):
                pltpu.make_async_remote_copy(
                    src_ref=lnd[a].at[me], dst_ref=lnd[a].at[me], send_sem=send[a].at[j],
                    recv_sem=recv[a].at[j], device_id=_peer(k), device_id_type=MESH_ID).start()

        for cp in loads:
            cp.start()
        for a in range(n):
            loads[a].wait()
            stage_out[a][...] = stage_in[a][...].astype(stage_out[a].dtype)
            stores[a].start()
            if a > 0:
                send_block(a - 1)
        send_block(n - 1)

    sems = pltpu.SemaphoreType.DMA((len(FIRST_PEERS),))
    outs = pl.pallas_call(
        body, name="gather_start",
        out_shape=(*[sems] * (2 * n), *[pltpu.HBM((N_DEV,) + shapes[a], shards[a][2]) for a in range(n)]),
        in_specs=[HBM] * n, out_specs=(*[SEM] * (2 * n), *[HBM] * n),
        scratch_shapes=[*[pltpu.VMEM(shapes[a], shards[a][0].dtype) for a in range(n)],
                        *[pltpu.VMEM(shapes[a], shards[a][2]) for a in range(n)],
                        pltpu.SemaphoreType.DMA((n,)), pltpu.SemaphoreType.DMA((n,))],
        compiler_params=pltpu.CompilerParams(has_side_effects=EFFECT, vmem_limit_bytes=40 * MIB),
    )(*[_hbm(s) for s, _, _ in shards])
    return outs[:n], outs[n:2 * n], outs[2 * n:]


def _gather_pass_on(name, lands, sends, recvs, after):
    n = len(lands)

    def body(*refs):
        lnd, send, recv = refs[:n], refs[n:2 * n], refs[2 * n:3 * n]
        send2, recv2 = refs[3 * n + 1:4 * n + 1], refs[4 * n + 1:5 * n + 1]
        sibling = _peer(SIBLING)
        for a in range(n):
            for j, k in enumerate(CHIP_PEERS):
                cp = pltpu.make_async_remote_copy(
                    src_ref=lnd[a].at[0], dst_ref=lnd[a].at[0], send_sem=send[a].at[1 + j],
                    recv_sem=recv[a].at[1 + j], device_id=_peer(k), device_id_type=MESH_ID)
                cp.wait_send()
                cp.wait_recv()
                blk = _flat(*_peer(k))
                pltpu.make_async_remote_copy(
                    src_ref=lnd[a].at[blk], dst_ref=lnd[a].at[blk], send_sem=send2[a].at[j],
                    recv_sem=recv2[a].at[j], device_id=sibling, device_id_type=MESH_ID).start()

    sems = pltpu.SemaphoreType.DMA((len(CHIP_PEERS),))
    outs = pl.pallas_call(
        body, name=name, out_shape=(*[sems] * (2 * n), *[_hbm_like(l) for l in lands]),
        in_specs=(*[HBM] * n, *[SEM] * (2 * n), HBM), out_specs=(*[SEM] * (2 * n), *[HBM] * n),
        input_output_aliases={i: 2 * n + i for i in range(n)},
        compiler_params=pltpu.CompilerParams(has_side_effects=EFFECT),
    )(*lands, *sends, *recvs, after)
    return outs[:n], outs[n:2 * n], outs[2 * n:]


def _gather_wait(name, lands, sends, recvs, sends2, recvs2, after):
    n = len(lands)

    def body(*refs):
        lnd, send, recv = refs[:n], refs[n:2 * n], refs[2 * n:3 * n]
        send2, recv2 = refs[3 * n:4 * n], refs[4 * n:5 * n]
        sibling = _peer(SIBLING)
        for a in range(n):
            own = pltpu.make_async_remote_copy(
                src_ref=lnd[a].at[0], dst_ref=lnd[a].at[0], send_sem=send[a].at[0], recv_sem=recv[a].at[0],
                device_id=sibling, device_id_type=MESH_ID)
            own.wait_send()
            own.wait_recv()
            for j in range(len(CHIP_PEERS)):
                cp = pltpu.make_async_remote_copy(
                    src_ref=lnd[a].at[0], dst_ref=lnd[a].at[0], send_sem=send2[a].at[j], recv_sem=recv2[a].at[j],
                    device_id=sibling, device_id_type=MESH_ID)
                cp.wait_send()
                cp.wait_recv()

    return pl.pallas_call(
        body, name=name, out_shape=[_hbm_like(l) for l in lands],
        in_specs=(*[HBM] * n, *[SEM] * (4 * n), HBM), out_specs=[HBM] * n,
        input_output_aliases={i: i for i in range(n)},
        compiler_params=pltpu.CompilerParams(has_side_effects=EFFECT),
    )(*lands, *sends, *recvs, *sends2, *recvs2, after)


def _scatter_start(name, grads, lands, layer, after=None):
    n = len(grads)
    afters = [] if after is None else [after]

    def body(*refs):
        g, lnd = refs[:n], refs[n:2 * n]
        send, recv = refs[2 * n + len(afters):3 * n + len(afters)], refs[3 * n + len(afters):4 * n + len(afters)]
        for a in range(n):
            for k in range(1, N_DEV):
                to = _peer(k)
                pltpu.make_async_remote_copy(
                    src_ref=g[a].at[_flat(*to)], dst_ref=lnd[a].at[layer, k - 1], send_sem=send[a].at[k - 1],
                    recv_sem=recv[a].at[k - 1], device_id=to, device_id_type=MESH_ID).start()

    outs = pl.pallas_call(
        body, name=name,
        out_shape=(*[_peer_sems()] * (2 * n), *[_hbm_like(g) for g in grads], *[_hbm_like(l) for l in lands]),
        in_specs=[HBM] * (2 * n + len(afters)), out_specs=(*[SEM] * (2 * n), *[HBM] * (2 * n)),
        input_output_aliases={i: 2 * n + i for i in range(2 * n)},
        compiler_params=pltpu.CompilerParams(has_side_effects=EFFECT),
    )(*[_hbm(g) for g in grads], *[_hbm(l) for l in lands], *afters)
    return outs[:n], outs[n:2 * n], outs[2 * n:3 * n], outs[3 * n:]


def _scatter_wait(name, grads, lands, sends, recvs, after):
    n, nw = len(grads), len(lands)

    def body(*refs):
        g, lnd = refs[:n], refs[n:n + nw]
        send, recv = refs[n + nw:2 * n + nw], refs[2 * n + nw:3 * n + nw]
        for a in range(n):
            for k in range(1, N_DEV):
                cp = pltpu.make_async_remote_copy(
                    src_ref=g[a].at[0], dst_ref=lnd[a // DEPTH].at[a % DEPTH, 0],
                    send_sem=send[a].at[k - 1], recv_sem=recv[a].at[k - 1],
                    device_id=_peer(k), device_id_type=MESH_ID)
                cp.wait_send()
                cp.wait_recv()

    outs = pl.pallas_call(
        body, name=name,
        out_shape=(*[_hbm_like(g) for g in grads], *[_hbm_like(l) for l in lands]),
        in_specs=(*[HBM] * (n + nw), *[SEM] * (2 * n), HBM), out_specs=[HBM] * (n + nw),
        input_output_aliases={i: i for i in range(n + nw)},
        compiler_params=pltpu.CompilerParams(has_side_effects=EFFECT),
    )(*grads, *lands, *sends, *recvs, after)
    return outs[n:]


def _rows_block(ref, d, br):
    return ref.at[pl.ds(pl.multiple_of(d * br, 8), br), :]


def _all_reduce_start(part):
    rows = part.shape[0]
    br = rows // N_DEV
    assert br * N_DEV == rows and br % 8 == 0

    def body(p_ref, _slots_in, send, recv, slots):
        me = _flat(*_coords())
        for k in range(1, N_DEV):
            to = _peer(k)
            pltpu.make_async_remote_copy(
                src_ref=_rows_block(p_ref, _flat(*to), br), dst_ref=slots.at[me], send_sem=send.at[k - 1],
                recv_sem=recv.at[k - 1], device_id=to, device_id_type=MESH_ID).start()

    slots = lax.empty((N_DEV, br, LANES), F32)
    return pl.pallas_call(
        body, name="all_reduce_start",
        out_shape=(_peer_sems(), _peer_sems(), _hbm_like(slots)),
        in_specs=[HBM, HBM], out_specs=(SEM, SEM, HBM), input_output_aliases={1: 2},
        compiler_params=pltpu.CompilerParams(has_side_effects=EFFECT),
    )(_hbm(part), _hbm(slots))


def _all_reduce_sum(part, slots, sends, recvs, after):
    rows = part.shape[0]
    br = rows // N_DEV

    def body(p_ref, slots, _total_in, send1, recv1, _after, send2, recv2, total, stage, acc_ref, sems):
        me = _flat(*_coords())
        for k in range(1, N_DEV):
            cp = pltpu.make_async_remote_copy(
                src_ref=_rows_block(p_ref, 0, br), dst_ref=slots.at[0], send_sem=send1.at[k - 1],
                recv_sem=recv1.at[k - 1], device_id=_peer(k), device_id_type=MESH_ID)
            cp.wait_send()
            cp.wait_recv()
        loads = [pltpu.make_async_copy(_rows_block(p_ref, me, br), stage.at[me], sems.at[0])]
        for k in range(1, N_DEV):
            d = _flat(*_peer(k))
            loads.append(pltpu.make_async_copy(slots.at[d], stage.at[d], sems.at[k]))
        for cp in loads:
            cp.start()
        for cp in loads:
            cp.wait()
        acc = stage[0]
        for d in range(1, N_DEV):
            acc = acc + stage[d]
        acc_ref[...] = acc
        store = pltpu.make_async_copy(acc_ref, _rows_block(total, me, br), sems.at[0])
        store.start()
        store.wait()
        for k in range(1, N_DEV):
            pltpu.make_async_remote_copy(
                src_ref=_rows_block(total, me, br), dst_ref=_rows_block(total, me, br), send_sem=send2.at[k - 1],
                recv_sem=recv2.at[k - 1], device_id=_peer(k), device_id_type=MESH_ID).start()

    total = lax.empty(part.shape, F32)
    return pl.pallas_call(
        body, name="all_reduce_sum",
        out_shape=(_peer_sems(), _peer_sems(), _hbm_like(total)),
        in_specs=[HBM, HBM, HBM, SEM, SEM, HBM], out_specs=(SEM, SEM, HBM), input_output_aliases={2: 2},
        scratch_shapes=[pltpu.VMEM((N_DEV, br, LANES), F32), pltpu.VMEM((br, LANES), F32),
                        pltpu.SemaphoreType.DMA((N_DEV,))],
        compiler_params=pltpu.CompilerParams(has_side_effects=EFFECT),
    )(_hbm(part), slots, _hbm(total), sends, recvs, after)


def _all_reduce_wait(total, sends, recvs, deps):
    br = total.shape[0] // N_DEV
    nd = len(deps)

    def body(total, send, recv, *_):
        for k in range(1, N_DEV):
            cp = pltpu.make_async_remote_copy(
                src_ref=_rows_block(total, 0, br), dst_ref=_rows_block(total, 0, br), send_sem=send.at[k - 1],
                recv_sem=recv.at[k - 1], device_id=_peer(k), device_id_type=MESH_ID)
            cp.wait_send()
            cp.wait_recv()

    return pl.pallas_call(
        body, name="all_reduce_wait", out_shape=_hbm_like(total),
        in_specs=[HBM, SEM, SEM] + [HBM] * nd, out_specs=HBM, input_output_aliases={0: 0},
        compiler_params=pltpu.CompilerParams(has_side_effects=EFFECT),
    )(total, sends, recvs, *map(_hbm, deps))


def _adam_math(w, g, m, v):
    m = ADAM_B1 * m + (1.0 - ADAM_B1) * g
    v = ADAM_B2 * v + (1.0 - ADAM_B2) * (g * g)
    m_hat = m / (1.0 - ADAM_B1 ** ADAM_STEP)
    v_hat = v / (1.0 - ADAM_B2 ** ADAM_STEP)
    delta = -ADAM_LR * (m_hat / (jnp.sqrt(v_hat) + ADAM_EPS) + ADAM_WD * w)
    return delta, m, v


def _adam_sharded(name, me, g32, recv, w, m, v):
    _, r, c = w.shape
    tr = min(r, 512)
    nblk = r // tr
    assert len(g32) == DEPTH == 2

    def body(me_ref, g0_ref, g1_ref, recv_ref, w_ref, m_ref, v_ref, g_ref, d_ref, nm_ref, nv_ref):
        g = jnp.where(pl.program_id(0) == 0, g0_ref[...], g1_ref[...])
        for k in range(N_DEV - 1):
            g = g + recv_ref[k].astype(F32)
        delta, nm, nv = _adam_math(w_ref[...], g, m_ref[...], v_ref[...])
        g_ref[...] = g
        d_ref[...] = delta
        nm_ref[...] = nm
        nv_ref[...] = nv

    blk = pl.BlockSpec((None, tr, c), lambda l, i, me_ref: (l, i, 0))
    own0 = pl.BlockSpec((None, tr, c), lambda l, i, me_ref: (me_ref[0], jnp.where(l == 0, i, nblk - 1), 0))
    own1 = pl.BlockSpec((None, tr, c), lambda l, i, me_ref: (me_ref[0], jnp.where(l == 1, i, 0), 0))
    shp = jax.ShapeDtypeStruct(w.shape, F32)
    return pl.pallas_call(
        body, name=name,
        grid_spec=pltpu.PrefetchScalarGridSpec(
            num_scalar_prefetch=1, grid=(DEPTH, nblk),
            in_specs=[own0, own1, pl.BlockSpec((None, N_DEV - 1, tr, c), lambda l, i, me_ref: (l, 0, i, 0)),
                      blk, blk, blk],
            out_specs=[blk] * 4),
        out_shape=[shp] * 4,
        compiler_params=_cparams(32, ("arbitrary", "arbitrary")),
    )(me, *[_hbm(a) for a in (*g32, recv, w, m, v)])


def _adam_small(gs, ws, ms, vs):
    n = len(gs)

    def body(*refs):
        g, w, m, v = refs[:n], refs[n:2 * n], refs[2 * n:3 * n], refs[3 * n:4 * n]
        d, nm, nv = refs[4 * n:5 * n], refs[5 * n:6 * n], refs[6 * n:7 * n]
        for a in range(n):
            delta, new_m, new_v = _adam_math(w[a][...], g[a][...], m[a][...], v[a][...])
            d[a][...] = delta
            nm[a][...] = new_m
            nv[a][...] = new_v

    specs = [_full(a.shape) for a in gs]
    shapes = [jax.ShapeDtypeStruct(a.shape, F32) for a in gs]
    outs = pl.pallas_call(
        body, name="adam_small", grid=(1,),
        in_specs=specs * 4, out_specs=specs * 3, out_shape=shapes * 3,
        compiler_params=_cparams(24),
    )(*gs, *ws, *ms, *vs)
    return outs[:n], outs[n:2 * n], outs[2 * n:]


SMALL = ["norm1_g", "sgu_ln_g", "sgu_ln_b", "sgu_w", "sgu_b", "conv_b", "conv_ln_g", "conv_ln_b", "norm2_g",
         "final_g"]


def _pack(arrays):
    flat = jnp.concatenate([a.reshape(-1) for a in arrays])
    pad = (-flat.shape[0]) % (N_DEV * 8 * LANES)
    return jnp.pad(flat, (0, pad)).reshape(-1, LANES)


def _unpack(packed, shapes):
    flat = packed.reshape(-1)
    out, off = [], 0
    for s in shapes:
        size = 1
        for d in s:
            size *= d
        out.append(flat[off:off + size].reshape(s))
        off += size
    return out


def kernel(x, norm1_g, w_in, sgu_ln_g, sgu_ln_b, sgu_w, sgu_b, conv_w, conv_b, conv_ln_g, conv_ln_b, w_out, norm2_g, w_ff1, w_ff2, final_g, loss_target, m_norm1_g, m_w_in, m_sgu_ln_g, m_sgu_ln_b, m_sgu_w, m_sgu_b, m_conv_w, m_conv_b, m_conv_ln_g, m_conv_ln_b, m_w_out, m_norm2_g, m_w_ff1, m_w_ff2, m_final_g, v_norm1_g, v_w_in, v_sgu_ln_g, v_sgu_ln_b, v_sgu_w, v_sgu_b, v_conv_w, v_conv_b, v_conv_ln_g, v_conv_ln_b, v_w_out, v_norm2_g, v_w_ff1, v_w_ff2, v_final_g):
    x2d = x.reshape(T, D)
    tgt = loss_target.reshape(T, D)
    cw_shard = conv_w.reshape(CONV_W, LANES)

    gnames, shards = [], []
    for l in range(DEPTH):
        for k, w in (("w_in", w_in), ("w_out", w_out), ("w_ff1", w_ff1), ("w_ff2", w_ff2)):
            gnames.append(f"{k}{l}")
            shards.append((w, l, BF16))
        if l == 0:
            gnames.insert(1, "conv_w")
            shards.insert(1, (cw_shard, None, F32))
    sends, recvs, lands = _gather_start(shards)
    gidx = {k: i for i, k in enumerate(gnames)}

    passed = {}

    def pass_on(ks, after):
        idx = [gidx[k] for k in ks]
        s2, r2, ld = _gather_pass_on("gather_pass_on_" + "_".join(ks), [lands[i] for i in idx],
                                     [sends[i] for i in idx], [recvs[i] for i in idx], after)
        for j, k in enumerate(ks):
            passed[k] = (s2[j], r2[j], ld[j])

    def gathered(ks, after):
        idx = [gidx[k] for k in ks]
        return _gather_wait("gather_wait_" + "_".join(ks), [passed[k][2] for k in ks], [sends[i] for i in idx],
                            [recvs[i] for i in idx], [passed[k][0] for k in ks], [passed[k][1] for k in ks], after)

    saved = []
    xl = x2d
    cw_full = None
    for l in range(DEPTH):
        g1 = norm1_g[l].reshape(1, D)
        g2 = norm2_g[l].reshape(1, D)
        lg, lb = sgu_ln_g[l].reshape(1, D_A), sgu_ln_b[l].reshape(1, D_A)
        bst = sgu_b[l].T
        cb = conv_b[l].reshape(1, D_B)
        clg, clb = conv_ln_g[l].reshape(1, D_B), conv_ln_b[l].reshape(1, D_B)
        first = [f"w_in{l}"] if l == 0 else [f"w_in{l}", f"w_out{l}"]
        pass_on(first, xl)
        wg_in, *rest = gathered(first, xl)
        h, proj = _fwd_in(xl, g1, wg_in)
        if l == 0:
            pass_on(["conv_w", "w_out0"], h)
            cw_g, wout = gathered(["conv_w", "w_out0"], proj)
            cw_full = cw_g.reshape(N_DEV, DEPTH, CONV_W, D_B // N_DEV).transpose(1, 2, 0, 3).reshape(
                DEPTH, CONV_W, D_B)
        else:
            (wout,) = rest
            pass_on([f"w_ff1{l}"], proj)
        wout = wout.reshape(D, D)
        mix, c, g, x1, h2 = _mixer_fwd(proj, lg, lb, sgu_w[l], bst, cw_full[l], cb, clg, clb, wout, xl, g2)
        pass_on([f"w_ff1{l}"] if l == 0 else [f"w_ff2{l}"], h2)
        (wg_ff1,) = gathered([f"w_ff1{l}"], h2)
        r = _fwd_ff1(h2, wg_ff1)
        if l == 0:
            pass_on(["w_ff20"], r)
        (w2,) = gathered([f"w_ff2{l}"], r)
        w2 = w2.reshape(D_FF, D)
        saved.append(dict(x=xl, h=h, proj=proj, mix=mix, c=c, g=g, x1=x1, h2=h2, r=r, wg_in=wg_in, wout=wout,
                          wg_ff1=wg_ff1, w2=w2, g1=g1, g2=g2, lg=lg, lb=lb, bst=bst, clg=clg, clb=clb))
        if l + 1 < DEPTH:
            xl = _fwd_ff2(r, w2, x1)
        else:
            dx, dxb, loss_part, d_final_g = _fwd_ff2_loss(r, w2, x1, final_g.reshape(1, D), tgt)

    names = ["w_in", "w_out", "w_ff1", "w_ff2"]
    block = dict(w_in=(D, IN_COLS // N_DEV), w_out=(D // N_DEV, D), w_ff1=(D, D_FF // N_DEV), w_ff2=(D_FF // N_DEV, D))
    land = {k: lax.empty((DEPTH, N_DEV - 1) + block[k], BF16) for k in names}
    big32 = {k: [None] * DEPTH for k in names}
    big16 = {k: [None] * DEPTH for k in names}
    ssend = {k: [None] * DEPTH for k in names}
    srecv = {k: [None] * DEPTH for k in names}

    def send_grads(l, grads, after=None):
        ks = list(grads)
        for k in ks:
            big32[k][l] = grads[k][0].reshape((N_DEV,) + block[k])
        sends, recvs, g16s, lands = _scatter_start(
            "scatter_start_" + "_".join(ks) + str(l), [grads[k][1].reshape((N_DEV,) + block[k]) for k in ks],
            [land[k] for k in ks], l, after)
        for i, k in enumerate(ks):
            ssend[k][l], srecv[k][l], big16[k][l], land[k] = sends[i], recvs[i], g16s[i], lands[i]

    rep = dict(norm1_g=(norm1_g, m_norm1_g, v_norm1_g), sgu_ln_g=(sgu_ln_g, m_sgu_ln_g, v_sgu_ln_g),
               sgu_ln_b=(sgu_ln_b, m_sgu_ln_b, v_sgu_ln_b), sgu_w=(sgu_w, m_sgu_w, v_sgu_w),
               sgu_b=(sgu_b, m_sgu_b, v_sgu_b), conv_b=(conv_b, m_conv_b, v_conv_b),
               conv_ln_g=(conv_ln_g, m_conv_ln_g, v_conv_ln_g), conv_ln_b=(conv_ln_b, m_conv_ln_b, v_conv_ln_b),
               norm2_g=(norm2_g, m_norm2_g, v_norm2_g), final_g=(final_g, m_final_g, v_final_g))

    def small_parts():
        parts = []
        for k in SMALL:
            if k == "final_g":
                parts.append(d_final_g.reshape(rep[k][0].shape))
            else:
                parts.append(jnp.stack([small[l][k].reshape(rep[k][0].shape[1:]) for l in range(DEPTH)]))
        parts.append(jnp.stack(d_conv_w))
        parts.append(loss_part[0, 0:1])
        return parts

    small = {}
    d_conv_w = [None] * DEPTH
    for l in reversed(range(DEPTH)):
        s = saved[l]
        send_grads(l, dict(w_ff2=_wgrad("wgrad_ff2", s["r"], dxb, D, square_a=True)))
        df1, dx1, dx1b, dg2 = _bwd_mlp(dxb, dx, s["w2"], s["r"], s["wg_ff1"], s["x1"], s["g2"], big16["w_ff2"][l])
        send_grads(l, dict(w_ff1=_wgrad("wgrad_ff1", s["h2"], df1, D_FF // N_DEV)))
        dpa, dc, dlg, dlb, dwm, dbs, dcb, dclg, dclb, *g_out = _mixer_bwd_a(
            dx1b, s["wout"], s["proj"], s["c"], s["lg"], s["lb"], sgu_w[l], s["bst"], s["clg"], s["clb"], s["mix"],
            big16["w_ff1"][l])
        send_grads(l, dict(w_out=g_out))
        dproj, d_conv_w[l], dx, dxb, dg1 = _mixer_bwd_b(dc, s["g"], s["proj"], cw_full[l], dpa, s["wg_in"], s["x"],
                                                         s["g1"], dx1)
        small[l] = dict(norm1_g=dg1, sgu_ln_g=dlg, sgu_ln_b=dlb, sgu_w=dwm, sgu_b=dbs, conv_b=dcb, conv_ln_g=dclg,
                        conv_ln_b=dclb, norm2_g=dg2)
        if l > 0:
            send_grads(l, dict(w_in=_wgrad("wgrad_in", s["h"], dproj, IN_COLS // N_DEV, dep=[big16["w_out"][l]])))
        else:
            parts = small_parts()
            packed = _pack(parts)
            ar_sends, ar_recvs, ar_slots = _all_reduce_start(packed)
            g_in = _wgrad("wgrad_in", s["h"], dproj, IN_COLS // N_DEV, dep=[ar_slots, big16["w_out"][l]])
            ar_sends, ar_recvs, total = _all_reduce_sum(packed, ar_slots, ar_sends, ar_recvs, _hbm(g_in[1]))
            send_grads(l, dict(w_in=g_in), total)
    grad_x = dx.reshape(1, T, D)

    me_block = _flat(*_coords()).astype(jnp.int32).reshape(1)
    wmv = dict(w_in=(w_in, m_w_in, v_w_in), w_out=(w_out, m_w_out, v_w_out), w_ff1=(w_ff1, m_w_ff1, v_w_ff1),
               w_ff2=(w_ff2, m_w_ff2, v_w_ff2))
    res = {}

    def update(tag, ks, after):
        recv = _scatter_wait("scatter_wait_" + tag, [g for k in ks for g in big16[k]], [land[k] for k in ks],
                             [q for k in ks for q in ssend[k]], [q for k in ks for q in srecv[k]], after)
        for k, rk in zip(ks, recv):
            res[k] = _adam_sharded("adam_" + k, me_block, big32[k], rk, *wmv[k])

    shapes = [p.shape for p in parts]
    update("early", ["w_ff2", "w_ff1", "w_out"], big16["w_in"][0])
    packed_sum = _all_reduce_wait(total, ar_sends, ar_recvs, [res[k][0] for k in ("w_ff2", "w_ff1", "w_out")])
    update("late", ["w_in"], packed_sum)
    summed = _unpack(packed_sum, shapes)
    loss = summed[-1][0]
    me = _flat(*_coords())
    g_conv_w = lax.dynamic_slice_in_dim(summed[-2], me * (D_B // N_DEV), D_B // N_DEV, axis=2)
    g_small = summed[:-2] + [g_conv_w]
    keys = SMALL + ["conv_w"]
    rep["conv_w"] = (conv_w, m_conv_w, v_conv_w)
    def two_d(a):
        return a.reshape(1, -1) if a.ndim == 1 else a

    d_s, nm_s, nv_s = _adam_small([two_d(g) for g in g_small], *[[two_d(rep[k][i]) for k in keys] for i in range(3)])
    for i, k in enumerate(keys):
        shape = rep[k][0].shape
        res[k] = (g_small[i], d_s[i].reshape(shape), nm_s[i].reshape(shape), nv_s[i].reshape(shape))

    order = ["norm1_g", "w_in", "sgu_ln_g", "sgu_ln_b", "sgu_w", "sgu_b", "conv_w", "conv_b", "conv_ln_g",
             "conv_ln_b", "w_out", "norm2_g", "w_ff1", "w_ff2", "final_g"]
    return (loss, grad_x, *[res[k][0] for k in order], *[res[k][1] for k in order],
            *[res[k][2] for k in order], *[res[k][3] for k in order])
```
